```python
import jax
import jax.numpy as jnp
from jax import lax
import numpy as np

D_MODEL = 1024
BATCH = 8
SEQ = 8192
DEPTH = 2

HEAD_DIM = 64
ROT_DIM = HEAD_DIM // 4
ROPE_THETA = 500000.0
BLOCK = 128
RMS_EPS = 1e-6
LN_EPS = 1e-5

A_HEADS = (D_MODEL // 2) // HEAD_DIM
A_KV_HEADS = 2
A_WINDOW = 128
CONV_CH = D_MODEL // 2
CONV_WIDTH = 31
C_HEADS = (D_MODEL // 2) // HEAD_DIM
DILATED_PAIRS = ((128, 1), (512, 4), (2048, 16))
D_CH = D_MODEL // 2
D_GROUPS = D_CH // HEAD_DIM
CHUNK = 128
D_FF = ((8 * D_MODEL // 3 + 255) // 256) * 256

A_Q = A_HEADS * HEAD_DIM
A_KV = A_KV_HEADS * HEAD_DIM
EVEN_IN = A_Q + 2 * A_KV + 2 * CONV_CH
EVEN_OUT = A_Q + CONV_CH
C_W = C_HEADS * HEAD_DIM
ODD_IN = 3 * C_W + 2 * D_CH
ODD_OUT = C_W + D_CH
N_EVEN = (DEPTH + 1) // 2
N_ODD = DEPTH // 2

kernel_name = 'hybrid_swa_sink_conformer_dilated_gmlp'


def rms_norm(x, g):
    xf = x.astype(jnp.float32)
    y = xf * lax.rsqrt(jnp.mean(xf * xf, axis=-1, keepdims=True) + RMS_EPS)
    return (y * g.astype(jnp.float32)).astype(x.dtype)


def layer_norm(x, g, b):
    xf = x.astype(jnp.float32)
    mu = jnp.mean(xf, axis=-1, keepdims=True)
    xc = xf - mu
    var = jnp.mean(xc * xc, axis=-1, keepdims=True)
    y = xc * lax.rsqrt(var + LN_EPS) * g.astype(jnp.float32) + b.astype(jnp.float32)
    return y.astype(x.dtype)


def rotary(x, pos):
    half = ROT_DIM // 2
    inv_freq = ROPE_THETA ** (-jnp.arange(half, dtype=jnp.float32) * (2.0 / ROT_DIM))
    ang = pos.astype(jnp.float32)[:, None] * inv_freq[None, :]
    cos = jnp.cos(ang)[None, :, None, :]
    sin = jnp.sin(ang)[None, :, None, :]
    xr = x[..., :ROT_DIM].astype(jnp.float32)
    x1, x2 = xr[..., :half], xr[..., half:]
    rot = jnp.concatenate([x1 * cos - x2 * sin, x2 * cos + x1 * sin], axis=-1).astype(x.dtype)
    return jnp.concatenate([rot, x[..., ROT_DIM:]], axis=-1)


def band_attention(q, k, v, max_dist, sink=None):
    B, S, Hq, hd = q.shape
    Hkv = k.shape[2]
    G = Hq // Hkv
    n = S // BLOCK
    qb = q.reshape(B, n, BLOCK, Hkv, G, hd)
    kb = k.reshape(B, n, BLOCK, Hkv, hd)
    vb = v.reshape(B, n, BLOCK, Hkv, hd)
    prev = lambda t: jnp.pad(t, ((0, 0), (1, 0), (0, 0), (0, 0), (0, 0)))[:, :-1]
    kk = jnp.concatenate([prev(kb), kb], axis=2)
    vv = jnp.concatenate([prev(vb), vb], axis=2)
    s = jnp.einsum('bnqhgd,bnjhd->bnhgqj', qb, kk,
                   preferred_element_type=jnp.float32) * (hd ** -0.5)
    qi = jnp.arange(BLOCK)[:, None]
    kj = jnp.arange(2 * BLOCK)[None, :]
    dist = qi + BLOCK - kj
    key_pos = jnp.arange(n)[:, None, None] * BLOCK + kj[None] - BLOCK
    valid = (dist >= 0)[None] & (dist <= max_dist)[None] & (key_pos >= 0)
    s = jnp.where(valid[None, :, None, None], s, -jnp.inf)
    m = jnp.max(s, axis=-1)
    if sink is not None:
        sink_b = sink.astype(jnp.float32).reshape(Hkv, G)[None, None, :, :, None]
        m = jnp.maximum(m, sink_b)
    p = jnp.exp(s - m[..., None])
    l = jnp.sum(p, axis=-1)
    if sink is not None:
        l = l + jnp.exp(sink_b - m)
    o = jnp.einsum('bnhgqj,bnjhd->bnqhgd', p.astype(v.dtype), vv,
                   preferred_element_type=jnp.float32)
    o = o / jnp.transpose(l, (0, 1, 4, 2, 3))[..., None]
    lse = jnp.transpose(m + jnp.log(l), (0, 1, 4, 2, 3)).reshape(B, S, Hq)
    return o.reshape(B, S, Hq, hd).astype(q.dtype), lse


def dilated_window_attention(q, k, v, window, dilation):
    B, S, H, hd = q.shape
    span = dilation * BLOCK
    s_pad = -(-S // span) * span
    sub = s_pad // dilation

    def fold(t):
        t = jnp.pad(t, ((0, 0), (0, s_pad - S), (0, 0), (0, 0)))
        t = t.reshape(B, sub, dilation, t.shape[2], hd)
        return jnp.transpose(t, (0, 2, 1, 3, 4)).reshape(B * dilation, sub, t.shape[3], hd)

    o, lse = band_attention(fold(q), fold(k), fold(v), window // dilation)
    o = jnp.transpose(o.reshape(B, dilation, sub, H, hd), (0, 2, 1, 3, 4)).reshape(B, s_pad, H, hd)
    lse = jnp.transpose(lse.reshape(B, dilation, sub, H), (0, 2, 1, 3)).reshape(B, s_pad, H)
    return o[:, :S], lse[:, :S]


def causal_depthwise_conv(x, w, b):
    C = x.shape[-1]
    y = lax.conv_general_dilated(
        x, w[:, None, :].astype(x.dtype), window_strides=(1,),
        padding=[(CONV_WIDTH - 1, 0)], dimension_numbers=('NWC', 'WIO', 'NWC'),
        feature_group_count=C)
    return y + b.astype(x.dtype)


def even_mixer(h, w_in, sinks, conv_w, conv_b, ln_g, ln_b, w_out, pos):
    B, S, _ = h.shape
    proj = jnp.einsum('bsd,de->bse', h, w_in)
    q, k, v, glu = jnp.split(proj, [A_Q, A_Q + A_KV, A_Q + 2 * A_KV], axis=-1)
    q = rotary(q.reshape(B, S, A_HEADS, HEAD_DIM), pos)
    k = rotary(k.reshape(B, S, A_KV_HEADS, HEAD_DIM), pos)
    v = v.reshape(B, S, A_KV_HEADS, HEAD_DIM)
    a, _ = band_attention(q, k, v, A_WINDOW - 1, sinks)
    a = a.reshape(B, S, A_Q)
    g_a, g_b = jnp.split(glu, 2, axis=-1)
    c = g_a * jax.nn.sigmoid(g_b)
    c = causal_depthwise_conv(c, conv_w, conv_b)
    c = jax.nn.silu(layer_norm(c, ln_g, ln_b))
    return jnp.einsum('bse,ed->bsd', jnp.concatenate([a, c], axis=-1), w_out)


def odd_mixer(h, w_in, sgu_ln_g, sgu_ln_b, spatial_w, spatial_b, w_out, pos):
    B, S, _ = h.shape
    proj = jnp.einsum('bsd,de->bse', h, w_in)
    q, k, v, z = jnp.split(proj, [C_W, 2 * C_W, 3 * C_W], axis=-1)
    q = rotary(q.reshape(B, S, C_HEADS, HEAD_DIM), pos)
    k = rotary(k.reshape(B, S, C_HEADS, HEAD_DIM), pos)
    v = v.reshape(B, S, C_HEADS, HEAD_DIM)
    outs, lses = [], []
    for window, dilation in DILATED_PAIRS:
        o_r, lse_r = dilated_window_attention(q, k, v, window, dilation)
        outs.append(o_r)
        lses.append(lse_r)
    alpha = jax.nn.softmax(jnp.stack(lses, axis=0), axis=0)
    c_out = jnp.einsum('rbsh,rbshd->bshd', alpha, jnp.stack(outs, axis=0).astype(jnp.float32))
    c_out = c_out.astype(h.dtype).reshape(B, S, C_W)
    z = jax.nn.gelu(z)
    u, g = jnp.split(z, 2, axis=-1)
    g = layer_norm(g, sgu_ln_g, sgu_ln_b).reshape(B, S // CHUNK, CHUNK, D_GROUPS, HEAD_DIM)
    causal = jnp.tril(jnp.ones((CHUNK, CHUNK), dtype=bool))
    w_s = jnp.where(causal[None], spatial_w, 0).astype(g.dtype)
    mixed = jnp.einsum('gts,bcsgd->bctgd', w_s, g) + spatial_b.T.astype(g.dtype)[None, None, :, :, None]
    d_out = u * mixed.reshape(B, S, D_CH)
    return jnp.einsum('bse,ed->bsd', jnp.concatenate([c_out, d_out], axis=-1), w_out)


def swiglu(h, w_gate, w_up, w_down):
    gate = jnp.einsum('bsd,df->bsf', h, w_gate)
    up = jnp.einsum('bsd,df->bsf', h, w_up)
    return jnp.einsum('bsf,fd->bsd', jax.nn.silu(gate) * up, w_down)


def _fwd_setup_inputs(seed: int = 0) -> dict:
    key = jax.random.key(seed)
    ks = jax.random.split(key, 21)
    f32 = jnp.float32

    def nrm(k, shape, scale):
        return jax.random.normal(k, shape, f32) * scale

    return {
        'x': nrm(ks[0], (BATCH, SEQ, D_MODEL), 1.0),
        'ev_norm_g': 1.0 + nrm(ks[1], (N_EVEN, D_MODEL), 0.02),
        'ev_w_in': nrm(ks[2], (N_EVEN, D_MODEL, EVEN_IN), D_MODEL ** -0.5),
        'ev_sinks': nrm(ks[3], (N_EVEN, A_HEADS), 0.5),
        'ev_conv_w': nrm(ks[4], (N_EVEN, CONV_WIDTH, CONV_CH), CONV_WIDTH ** -0.5),
        'ev_conv_b': nrm(ks[5], (N_EVEN, CONV_CH), 0.02),
        'ev_conv_ln_g': 1.0 + nrm(ks[6], (N_EVEN, CONV_CH), 0.02),
        'ev_conv_ln_b': nrm(ks[7], (N_EVEN, CONV_CH), 0.02),
        'ev_w_out': nrm(ks[8], (N_EVEN, EVEN_OUT, D_MODEL), EVEN_OUT ** -0.5),
        'od_norm_g': 1.0 + nrm(ks[9], (N_ODD, D_MODEL), 0.02),
        'od_w_in': nrm(ks[10], (N_ODD, D_MODEL, ODD_IN), D_MODEL ** -0.5),
        'od_sgu_ln_g': 1.0 + nrm(ks[11], (N_ODD, D_CH), 0.02),
        'od_sgu_ln_b': nrm(ks[12], (N_ODD, D_CH), 0.02),
        'od_spatial_w': nrm(ks[13], (N_ODD, D_GROUPS, CHUNK, CHUNK), CHUNK ** -0.5),
        'od_spatial_b': 1.0 + nrm(ks[14], (N_ODD, D_GROUPS, CHUNK), 0.02),
        'od_w_out': nrm(ks[15], (N_ODD, ODD_OUT, D_MODEL), ODD_OUT ** -0.5),
        'ffn_norm_g': 1.0 + nrm(ks[16], (DEPTH, D_MODEL), 0.02),
        'ffn_w_gate': nrm(ks[17], (DEPTH, D_MODEL, D_FF), D_MODEL ** -0.5),
        'ffn_w_up': nrm(ks[18], (DEPTH, D_MODEL, D_FF), D_MODEL ** -0.5),
        'ffn_w_down': nrm(ks[19], (DEPTH, D_FF, D_MODEL), D_FF ** -0.5),
        'final_norm_g': 1.0 + nrm(ks[20], (D_MODEL,), 0.02),
    }


def _fwd_reference(x, ev_norm_g, ev_w_in, ev_sinks, ev_conv_w, ev_conv_b, ev_conv_ln_g,
              ev_conv_ln_b, ev_w_out, od_norm_g, od_w_in, od_sgu_ln_g, od_sgu_ln_b,
              od_spatial_w, od_spatial_b, od_w_out, ffn_norm_g, ffn_w_gate, ffn_w_up,
              ffn_w_down, final_norm_g):
    pos = jnp.arange(x.shape[1], dtype=jnp.int32)
    h = x
    for layer in range(DEPTH):
        i = layer // 2
        if layer % 2 == 0:
            h = h + even_mixer(rms_norm(h, ev_norm_g[i]), ev_w_in[i], ev_sinks[i],
                               ev_conv_w[i], ev_conv_b[i], ev_conv_ln_g[i],
                               ev_conv_ln_b[i], ev_w_out[i], pos)
        else:
            h = h + odd_mixer(rms_norm(h, od_norm_g[i]), od_w_in[i], od_sgu_ln_g[i],
                              od_sgu_ln_b[i], od_spatial_w[i], od_spatial_b[i],
                              od_w_out[i], pos)
        h = h + swiglu(rms_norm(h, ffn_norm_g[layer]), ffn_w_gate[layer],
                       ffn_w_up[layer], ffn_w_down[layer])
    return rms_norm(h, final_norm_g)


import jax as _jax
import jax.numpy as _jnp

TWIN_FORMAT = 'train_step'
FWD_PARAMS = ['x', 'ev_norm_g', 'ev_w_in', 'ev_sinks', 'ev_conv_w', 'ev_conv_b', 'ev_conv_ln_g', 'ev_conv_ln_b', 'ev_w_out', 'od_norm_g', 'od_w_in', 'od_sgu_ln_g', 'od_sgu_ln_b', 'od_spatial_w', 'od_spatial_b', 'od_w_out', 'ffn_norm_g', 'ffn_w_gate', 'ffn_w_up', 'ffn_w_down', 'final_norm_g']
TWIN_WEIGHTS = ['ev_norm_g', 'ev_w_in', 'ev_sinks', 'ev_conv_w', 'ev_conv_b', 'ev_conv_ln_g', 'ev_conv_ln_b', 'ev_w_out', 'od_norm_g', 'od_w_in', 'od_sgu_ln_g', 'od_sgu_ln_b', 'od_spatial_w', 'od_spatial_b', 'od_w_out', 'ffn_norm_g', 'ffn_w_gate', 'ffn_w_up', 'ffn_w_down', 'final_norm_g']
TWIN_DIFF_INPUT = 'x'
TWIN_INPUTS = ['x', 'ev_norm_g', 'ev_w_in', 'ev_sinks', 'ev_conv_w', 'ev_conv_b', 'ev_conv_ln_g', 'ev_conv_ln_b', 'ev_w_out', 'od_norm_g', 'od_w_in', 'od_sgu_ln_g', 'od_sgu_ln_b', 'od_spatial_w', 'od_spatial_b', 'od_w_out', 'ffn_norm_g', 'ffn_w_gate', 'ffn_w_up', 'ffn_w_down', 'final_norm_g', 'loss_target', 'm_ev_norm_g', 'm_ev_w_in', 'm_ev_sinks', 'm_ev_conv_w', 'm_ev_conv_b', 'm_ev_conv_ln_g', 'm_ev_conv_ln_b', 'm_ev_w_out', 'm_od_norm_g', 'm_od_w_in', 'm_od_sgu_ln_g', 'm_od_sgu_ln_b', 'm_od_spatial_w', 'm_od_spatial_b', 'm_od_w_out', 'm_ffn_norm_g', 'm_ffn_w_gate', 'm_ffn_w_up', 'm_ffn_w_down', 'm_final_norm_g', 'v_ev_norm_g', 'v_ev_w_in', 'v_ev_sinks', 'v_ev_conv_w', 'v_ev_conv_b', 'v_ev_conv_ln_g', 'v_ev_conv_ln_b', 'v_ev_w_out', 'v_od_norm_g', 'v_od_w_in', 'v_od_sgu_ln_g', 'v_od_sgu_ln_b', 'v_od_spatial_w', 'v_od_spatial_b', 'v_od_w_out', 'v_ffn_norm_g', 'v_ffn_w_gate', 'v_ffn_w_up', 'v_ffn_w_down', 'v_final_norm_g']
TWIN_OUTPUTS = ['loss', 'grad_x', 'grad_ev_norm_g', 'grad_ev_w_in', 'grad_ev_sinks', 'grad_ev_conv_w', 'grad_ev_conv_b', 'grad_ev_conv_ln_g', 'grad_ev_conv_ln_b', 'grad_ev_w_out', 'grad_od_norm_g', 'grad_od_w_in', 'grad_od_sgu_ln_g', 'grad_od_sgu_ln_b', 'grad_od_spatial_w', 'grad_od_spatial_b', 'grad_od_w_out', 'grad_ffn_norm_g', 'grad_ffn_w_gate', 'grad_ffn_w_up', 'grad_ffn_w_down', 'grad_final_norm_g', 'delta_ev_norm_g', 'delta_ev_w_in', 'delta_ev_sinks', 'delta_ev_conv_w', 'delta_ev_conv_b', 'delta_ev_conv_ln_g', 'delta_ev_conv_ln_b', 'delta_ev_w_out', 'delta_od_norm_g', 'delta_od_w_in', 'delta_od_sgu_ln_g', 'delta_od_sgu_ln_b', 'delta_od_spatial_w', 'delta_od_spatial_b', 'delta_od_w_out', 'delta_ffn_norm_g', 'delta_ffn_w_gate', 'delta_ffn_w_up', 'delta_ffn_w_down', 'delta_final_norm_g', 'new_m_ev_norm_g', 'new_m_ev_w_in', 'new_m_ev_sinks', 'new_m_ev_conv_w', 'new_m_ev_conv_b', 'new_m_ev_conv_ln_g', 'new_m_ev_conv_ln_b', 'new_m_ev_w_out', 'new_m_od_norm_g', 'new_m_od_w_in', 'new_m_od_sgu_ln_g', 'new_m_od_sgu_ln_b', 'new_m_od_spatial_w', 'new_m_od_spatial_b', 'new_m_od_w_out', 'new_m_ffn_norm_g', 'new_m_ffn_w_gate', 'new_m_ffn_w_up', 'new_m_ffn_w_down', 'new_m_final_norm_g', 'new_v_ev_norm_g', 'new_v_ev_w_in', 'new_v_ev_sinks', 'new_v_ev_conv_w', 'new_v_ev_conv_b', 'new_v_ev_conv_ln_g', 'new_v_ev_conv_ln_b', 'new_v_ev_w_out', 'new_v_od_norm_g', 'new_v_od_w_in', 'new_v_od_sgu_ln_g', 'new_v_od_sgu_ln_b', 'new_v_od_spatial_w', 'new_v_od_spatial_b', 'new_v_od_w_out', 'new_v_ffn_norm_g', 'new_v_ffn_w_gate', 'new_v_ffn_w_up', 'new_v_ffn_w_down', 'new_v_final_norm_g']
TWIN_LEAF_KINDS = {'loss': 'loss', 'grad_x': 'grad_x', 'grad_ev_norm_g': 'grad_w', 'grad_ev_w_in': 'grad_w', 'grad_ev_sinks': 'grad_w', 'grad_ev_conv_w': 'grad_w', 'grad_ev_conv_b': 'grad_w', 'grad_ev_conv_ln_g': 'grad_w', 'grad_ev_conv_ln_b': 'grad_w', 'grad_ev_w_out': 'grad_w', 'grad_od_norm_g': 'grad_w', 'grad_od_w_in': 'grad_w', 'grad_od_sgu_ln_g': 'grad_w', 'grad_od_sgu_ln_b': 'grad_w', 'grad_od_spatial_w': 'grad_w', 'grad_od_spatial_b': 'grad_w', 'grad_od_w_out': 'grad_w', 'grad_ffn_norm_g': 'grad_w', 'grad_ffn_w_gate': 'grad_w', 'grad_ffn_w_up': 'grad_w', 'grad_ffn_w_down': 'grad_w', 'grad_final_norm_g': 'grad_w', 'delta_ev_norm_g': 'delta_w', 'delta_ev_w_in': 'delta_w', 'delta_ev_sinks': 'delta_w', 'delta_ev_conv_w': 'delta_w', 'delta_ev_conv_b': 'delta_w', 'delta_ev_conv_ln_g': 'delta_w', 'delta_ev_conv_ln_b': 'delta_w', 'delta_ev_w_out': 'delta_w', 'delta_od_norm_g': 'delta_w', 'delta_od_w_in': 'delta_w', 'delta_od_sgu_ln_g': 'delta_w', 'delta_od_sgu_ln_b': 'delta_w', 'delta_od_spatial_w': 'delta_w', 'delta_od_spatial_b': 'delta_w', 'delta_od_w_out': 'delta_w', 'delta_ffn_norm_g': 'delta_w', 'delta_ffn_w_gate': 'delta_w', 'delta_ffn_w_up': 'delta_w', 'delta_ffn_w_down': 'delta_w', 'delta_final_norm_g': 'delta_w', 'new_m_ev_norm_g': 'new_m', 'new_m_ev_w_in': 'new_m', 'new_m_ev_sinks': 'new_m', 'new_m_ev_conv_w': 'new_m', 'new_m_ev_conv_b': 'new_m', 'new_m_ev_conv_ln_g': 'new_m', 'new_m_ev_conv_ln_b': 'new_m', 'new_m_ev_w_out': 'new_m', 'new_m_od_norm_g': 'new_m', 'new_m_od_w_in': 'new_m', 'new_m_od_sgu_ln_g': 'new_m', 'new_m_od_sgu_ln_b': 'new_m', 'new_m_od_spatial_w': 'new_m', 'new_m_od_spatial_b': 'new_m', 'new_m_od_w_out': 'new_m', 'new_m_ffn_norm_g': 'new_m', 'new_m_ffn_w_gate': 'new_m', 'new_m_ffn_w_up': 'new_m', 'new_m_ffn_w_down': 'new_m', 'new_m_final_norm_g': 'new_m', 'new_v_ev_norm_g': 'new_v', 'new_v_ev_w_in': 'new_v', 'new_v_ev_sinks': 'new_v', 'new_v_ev_conv_w': 'new_v', 'new_v_ev_conv_b': 'new_v', 'new_v_ev_conv_ln_g': 'new_v', 'new_v_ev_conv_ln_b': 'new_v', 'new_v_ev_w_out': 'new_v', 'new_v_od_norm_g': 'new_v', 'new_v_od_w_in': 'new_v', 'new_v_od_sgu_ln_g': 'new_v', 'new_v_od_sgu_ln_b': 'new_v', 'new_v_od_spatial_w': 'new_v', 'new_v_od_spatial_b': 'new_v', 'new_v_od_w_out': 'new_v', 'new_v_ffn_norm_g': 'new_v', 'new_v_ffn_w_gate': 'new_v', 'new_v_ffn_w_up': 'new_v', 'new_v_ffn_w_down': 'new_v', 'new_v_final_norm_g': 'new_v'}


def _forward(args):
    return _fwd_reference(*[args[k] for k in FWD_PARAMS])


def _output_shape():
    def fwd():
        inp = _fwd_setup_inputs(0)
        return _fwd_reference(*[inp[k] for k in FWD_PARAMS])
    out = _jax.eval_shape(fwd)
    return out.shape, out.dtype

N_MICROBATCH = 1
ADAM_LR = 0.001
ADAM_B1 = 0.9
ADAM_B2 = 0.999
ADAM_EPS = 1e-08
ADAM_WD = 0.01
ADAM_STEP = 10
PER_EXAMPLE_BATCH_AXIS = {'x': 0, 'loss_target': 0}
SHARED_INPUTS = []
_WEIGHT_DTYPES = {'ev_norm_g': _jnp.float32, 'ev_w_in': _jnp.float32, 'ev_sinks': _jnp.float32, 'ev_conv_w': _jnp.float32, 'ev_conv_b': _jnp.float32, 'ev_conv_ln_g': _jnp.float32, 'ev_conv_ln_b': _jnp.float32, 'ev_w_out': _jnp.float32, 'od_norm_g': _jnp.float32, 'od_w_in': _jnp.float32, 'od_sgu_ln_g': _jnp.float32, 'od_sgu_ln_b': _jnp.float32, 'od_spatial_w': _jnp.float32, 'od_spatial_b': _jnp.float32, 'od_w_out': _jnp.float32, 'ffn_norm_g': _jnp.float32, 'ffn_w_gate': _jnp.float32, 'ffn_w_up': _jnp.float32, 'ffn_w_down': _jnp.float32, 'final_norm_g': _jnp.float32}
MOMENT_SCALE = {'ev_norm_g': 1.516009e-01, 'ev_w_in': 1.148949e-01, 'ev_sinks': 3.711380e-02, 'ev_conv_w': 1.776386e-01, 'ev_conv_b': 4.173748e-01, 'ev_conv_ln_g': 2.578778e-01, 'ev_conv_ln_b': 2.349359e-01, 'ev_w_out': 1.313231e-01, 'od_norm_g': 1.495576e-01, 'od_w_in': 9.205174e-02, 'od_sgu_ln_g': 9.453450e-02, 'od_sgu_ln_b': 9.092048e-02, 'od_spatial_w': 6.477145e-02, 'od_spatial_b': 9.507488e-02, 'od_w_out': 1.186162e-01, 'ffn_norm_g': 1.771225e-01, 'ffn_w_gate': 7.446656e-02, 'ffn_w_up': 7.261281e-02, 'ffn_w_down': 1.200256e-01, 'final_norm_g': 6.410986e+01}


def _to_microbatches(a, axis):
    t = _jnp.moveaxis(a, axis, 0)
    t = t.reshape((N_MICROBATCH, t.shape[0] // N_MICROBATCH) + t.shape[1:])
    return _jnp.moveaxis(t, 1, axis + 1)


def setup_inputs(seed: int = 0) -> dict:
    inp = _fwd_setup_inputs(seed)
    key = _jax.random.fold_in(_jax.random.key(seed), 7919)
    shape, _ = _output_shape()
    out = dict(inp)
    out["loss_target"] = _jax.random.normal(_jax.random.fold_in(key, 0), shape, _jnp.float32)
    for i, name in enumerate(TWIN_WEIGHTS):
        w = inp[name].astype(_jnp.float32)
        if MOMENT_SCALE is None:
            s = _jnp.sqrt(_jnp.mean(_jnp.square(w)) + 1e-30)
        else:
            s = MOMENT_SCALE[name]
        km, kv = _jax.random.split(_jax.random.fold_in(key, i + 1))
        out[name] = w
        out["m_" + name] = s * _jax.random.normal(km, w.shape, _jnp.float32)
        out["v_" + name] = (s * s) * _jax.random.uniform(kv, w.shape, _jnp.float32, 0.5, 1.5)
    if N_MICROBATCH > 1:
        for name, axis in PER_EXAMPLE_BATCH_AXIS.items():
            out[name] = _to_microbatches(out[name], axis)
    return {'x': out['x'], 'ev_norm_g': out['ev_norm_g'], 'ev_w_in': out['ev_w_in'], 'ev_sinks': out['ev_sinks'], 'ev_conv_w': out['ev_conv_w'], 'ev_conv_b': out['ev_conv_b'], 'ev_conv_ln_g': out['ev_conv_ln_g'], 'ev_conv_ln_b': out['ev_conv_ln_b'], 'ev_w_out': out['ev_w_out'], 'od_norm_g': out['od_norm_g'], 'od_w_in': out['od_w_in'], 'od_sgu_ln_g': out['od_sgu_ln_g'], 'od_sgu_ln_b': out['od_sgu_ln_b'], 'od_spatial_w': out['od_spatial_w'], 'od_spatial_b': out['od_spatial_b'], 'od_w_out': out['od_w_out'], 'ffn_norm_g': out['ffn_norm_g'], 'ffn_w_gate': out['ffn_w_gate'], 'ffn_w_up': out['ffn_w_up'], 'ffn_w_down': out['ffn_w_down'], 'final_norm_g': out['final_norm_g'], 'loss_target': out['loss_target'], 'm_ev_norm_g': out['m_ev_norm_g'], 'm_ev_w_in': out['m_ev_w_in'], 'm_ev_sinks': out['m_ev_sinks'], 'm_ev_conv_w': out['m_ev_conv_w'], 'm_ev_conv_b': out['m_ev_conv_b'], 'm_ev_conv_ln_g': out['m_ev_conv_ln_g'], 'm_ev_conv_ln_b': out['m_ev_conv_ln_b'], 'm_ev_w_out': out['m_ev_w_out'], 'm_od_norm_g': out['m_od_norm_g'], 'm_od_w_in': out['m_od_w_in'], 'm_od_sgu_ln_g': out['m_od_sgu_ln_g'], 'm_od_sgu_ln_b': out['m_od_sgu_ln_b'], 'm_od_spatial_w': out['m_od_spatial_w'], 'm_od_spatial_b': out['m_od_spatial_b'], 'm_od_w_out': out['m_od_w_out'], 'm_ffn_norm_g': out['m_ffn_norm_g'], 'm_ffn_w_gate': out['m_ffn_w_gate'], 'm_ffn_w_up': out['m_ffn_w_up'], 'm_ffn_w_down': out['m_ffn_w_down'], 'm_final_norm_g': out['m_final_norm_g'], 'v_ev_norm_g': out['v_ev_norm_g'], 'v_ev_w_in': out['v_ev_w_in'], 'v_ev_sinks': out['v_ev_sinks'], 'v_ev_conv_w': out['v_ev_conv_w'], 'v_ev_conv_b': out['v_ev_conv_b'], 'v_ev_conv_ln_g': out['v_ev_conv_ln_g'], 'v_ev_conv_ln_b': out['v_ev_conv_ln_b'], 'v_ev_w_out': out['v_ev_w_out'], 'v_od_norm_g': out['v_od_norm_g'], 'v_od_w_in': out['v_od_w_in'], 'v_od_sgu_ln_g': out['v_od_sgu_ln_g'], 'v_od_sgu_ln_b': out['v_od_sgu_ln_b'], 'v_od_spatial_w': out['v_od_spatial_w'], 'v_od_spatial_b': out['v_od_spatial_b'], 'v_od_w_out': out['v_od_w_out'], 'v_ffn_norm_g': out['v_ffn_norm_g'], 'v_ffn_w_gate': out['v_ffn_w_gate'], 'v_ffn_w_up': out['v_ffn_w_up'], 'v_ffn_w_down': out['v_ffn_w_down'], 'v_final_norm_g': out['v_final_norm_g']}


def _loss(weights, diff, rest, loss_target):
    with _jax.named_scope("forward"):
        args = {**rest, TWIN_DIFF_INPUT: diff, **{k: w.astype(_WEIGHT_DTYPES[k]) for k, w in weights.items()}}
        y = _forward(args)
    with _jax.named_scope("loss_head"):
        err = _jnp.square(y.astype(_jnp.float32) - loss_target)
        return 0.5 * _jnp.sum(_jnp.mean(err, axis=-1)) if err.ndim else 0.5 * err


def _adamw(w, g, m, v):
    m = ADAM_B1 * m + (1.0 - ADAM_B1) * g
    v = ADAM_B2 * v + (1.0 - ADAM_B2) * _jnp.square(g)
    m_hat = m / (1.0 - ADAM_B1 ** ADAM_STEP)
    v_hat = v / (1.0 - ADAM_B2 ** ADAM_STEP)
    delta = -ADAM_LR * (m_hat / (_jnp.sqrt(v_hat) + ADAM_EPS) + ADAM_WD * w)
    return delta, m, v


def reference(x, ev_norm_g, ev_w_in, ev_sinks, ev_conv_w, ev_conv_b, ev_conv_ln_g, ev_conv_ln_b, ev_w_out, od_norm_g, od_w_in, od_sgu_ln_g, od_sgu_ln_b, od_spatial_w, od_spatial_b, od_w_out, ffn_norm_g, ffn_w_gate, ffn_w_up, ffn_w_down, final_norm_g, loss_target, m_ev_norm_g, m_ev_w_in, m_ev_sinks, m_ev_conv_w, m_ev_conv_b, m_ev_conv_ln_g, m_ev_conv_ln_b, m_ev_w_out, m_od_norm_g, m_od_w_in, m_od_sgu_ln_g, m_od_sgu_ln_b, m_od_spatial_w, m_od_spatial_b, m_od_w_out, m_ffn_norm_g, m_ffn_w_gate, m_ffn_w_up, m_ffn_w_down, m_final_norm_g, v_ev_norm_g, v_ev_w_in, v_ev_sinks, v_ev_conv_w, v_ev_conv_b, v_ev_conv_ln_g, v_ev_conv_ln_b, v_ev_w_out, v_od_norm_g, v_od_w_in, v_od_sgu_ln_g, v_od_sgu_ln_b, v_od_spatial_w, v_od_spatial_b, v_od_w_out, v_ffn_norm_g, v_ffn_w_gate, v_ffn_w_up, v_ffn_w_down, v_final_norm_g):
    given = dict(x=x, ev_norm_g=ev_norm_g, ev_w_in=ev_w_in, ev_sinks=ev_sinks, ev_conv_w=ev_conv_w, ev_conv_b=ev_conv_b, ev_conv_ln_g=ev_conv_ln_g, ev_conv_ln_b=ev_conv_ln_b, ev_w_out=ev_w_out, od_norm_g=od_norm_g, od_w_in=od_w_in, od_sgu_ln_g=od_sgu_ln_g, od_sgu_ln_b=od_sgu_ln_b, od_spatial_w=od_spatial_w, od_spatial_b=od_spatial_b, od_w_out=od_w_out, ffn_norm_g=ffn_norm_g, ffn_w_gate=ffn_w_gate, ffn_w_up=ffn_w_up, ffn_w_down=ffn_w_down, final_norm_g=final_norm_g, loss_target=loss_target, m_ev_norm_g=m_ev_norm_g, m_ev_w_in=m_ev_w_in, m_ev_sinks=m_ev_sinks, m_ev_conv_w=m_ev_conv_w, m_ev_conv_b=m_ev_conv_b, m_ev_conv_ln_g=m_ev_conv_ln_g, m_ev_conv_ln_b=m_ev_conv_ln_b, m_ev_w_out=m_ev_w_out, m_od_norm_g=m_od_norm_g, m_od_w_in=m_od_w_in, m_od_sgu_ln_g=m_od_sgu_ln_g, m_od_sgu_ln_b=m_od_sgu_ln_b, m_od_spatial_w=m_od_spatial_w, m_od_spatial_b=m_od_spatial_b, m_od_w_out=m_od_w_out, m_ffn_norm_g=m_ffn_norm_g, m_ffn_w_gate=m_ffn_w_gate, m_ffn_w_up=m_ffn_w_up, m_ffn_w_down=m_ffn_w_down, m_final_norm_g=m_final_norm_g, v_ev_norm_g=v_ev_norm_g, v_ev_w_in=v_ev_w_in, v_ev_sinks=v_ev_sinks, v_ev_conv_w=v_ev_conv_w, v_ev_conv_b=v_ev_conv_b, v_ev_conv_ln_g=v_ev_conv_ln_g, v_ev_conv_ln_b=v_ev_conv_ln_b, v_ev_w_out=v_ev_w_out, v_od_norm_g=v_od_norm_g, v_od_w_in=v_od_w_in, v_od_sgu_ln_g=v_od_sgu_ln_g, v_od_sgu_ln_b=v_od_sgu_ln_b, v_od_spatial_w=v_od_spatial_w, v_od_spatial_b=v_od_spatial_b, v_od_w_out=v_od_w_out, v_ffn_norm_g=v_ffn_norm_g, v_ffn_w_gate=v_ffn_w_gate, v_ffn_w_up=v_ffn_w_up, v_ffn_w_down=v_ffn_w_down, v_final_norm_g=v_final_norm_g)
    weights = {n: given[n] for n in TWIN_WEIGHTS}
    shared = {n: given[n] for n in SHARED_INPUTS}
    per_example = {n: given[n] for n in ['x']}
    grad_fn = _jax.value_and_grad(_loss, argnums=(0, 1))

    def one_microbatch(ex, loss_target):
        ex = dict(ex)
        diff = ex.pop(TWIN_DIFF_INPUT)
        return grad_fn(weights, diff, {**shared, **ex}, loss_target)

    if N_MICROBATCH == 1:
        loss, (grad_w, grad_x) = one_microbatch(per_example, given["loss_target"])
    else:
        def body(carry, xs):
            loss_sum, grad_sum = carry
            l_k, (gw_k, gx_k) = one_microbatch(xs[0], xs[1])
            with _jax.named_scope("update"):
                return (loss_sum + l_k, _jax.tree.map(_jnp.add, grad_sum, gw_k)), gx_k

        init = (_jnp.zeros((), _jnp.float32), _jax.tree.map(_jnp.zeros_like, weights))
        (loss, grad_w), grad_x = _jax.lax.scan(body, init, (per_example, given["loss_target"]))
    with _jax.named_scope("update"):
        delta_w, new_m, new_v = {}, {}, {}
        for n in TWIN_WEIGHTS:
            delta_w[n], new_m[n], new_v[n] = _adamw(weights[n], grad_w[n], given["m_" + n], given["v_" + n])
    return (loss, grad_x, *[grad_w[n] for n in TWIN_WEIGHTS], *[delta_w[n] for n in TWIN_WEIGHTS],
            *[new_m[n] for n in TWIN_WEIGHTS], *[new_v[n] for n in TWIN_WEIGHTS])
```

```python
import functools
import math

import jax
import jax.numpy as jnp
from jax import lax
from jax.experimental import pallas as pl
from jax.experimental.pallas import tpu as pltpu

F32 = jnp.float32
ACT = jnp.bfloat16

D_MODEL = 1024
HEAD_DIM = 64
N_HEADS = 8
ATT_W = N_HEADS * HEAD_DIM
A_KV_HEADS = 2
CONV_CH = 512
CONV_WIDTH = 31
CONV_HALO = 32
D_CH = 512
D_GROUPS = 8
BLOCK = 128
D_FF = 2816
ROT_DIM = 16
ROPE_THETA = 500000.0
RMS_EPS = 1e-6
LN_EPS = 1e-5
DILATED = ((128, 1), (512, 4), (2048, 16))
NEG = -1e30
LANES = 128

ADAM_LR = 0.001
ADAM_B1 = 0.9
ADAM_B2 = 0.999
ADAM_EPS = 1e-08
ADAM_WD = 0.01
ADAM_STEP = 10

V7X_VMEM_LIMIT = 56 * 1024 * 1024

NN = (((1,), (0,)), ((), ()))
NT = (((1,), (1,)), ((), ()))
TN = (((0,), (0,)), ((), ()))
MESH = pl.DeviceIdType.MESH
AXES = ("x", "y", "c")


def _dot(a, b, dims):
    return lax.dot_general(a, b, dims, preferred_element_type=F32)


def _cp(*sem):
    return pltpu.CompilerParams(dimension_semantics=sem if sem else None,
                                vmem_limit_bytes=V7X_VMEM_LIMIT)


def _tile(n, prefs):
    for p in prefs:
        if n % p == 0:
            return p
    return n


def _sigmoid(x):
    return 1.0 / (1.0 + jnp.exp(-x))


def _rowspec(tm, w, col=0):
    return pl.BlockSpec((tm, w), lambda i, col=col: (i, col))


def _fullspec(shape):
    nd = len(shape)
    return pl.BlockSpec(shape, lambda *a, nd=nd: (0,) * nd)


def _rope_tables(seq):
    half = ROT_DIM // 2
    inv_freq = ROPE_THETA ** (-jnp.arange(half, dtype=F32) * (2.0 / ROT_DIM))
    ang = jnp.arange(seq, dtype=jnp.int32).astype(F32)[:, None] * inv_freq[None, :]
    cos, sin = jnp.cos(ang), jnp.sin(ang)
    lane = jnp.arange(LANES)
    jm = lane % HEAD_DIM
    idx = jm % half
    c = jnp.where(jm[None, :] < ROT_DIM, cos[:, idx], 1.0)
    sa = jnp.where(jm[None, :] < half, -sin[:, idx], 0.0)
    sb = jnp.where((jm[None, :] >= half) & (jm[None, :] < ROT_DIM), sin[:, idx], 0.0)
    return c.astype(F32), sa.astype(F32), sb.astype(F32)


def _rope(x, c, sa, sb):
    return x * c + pltpu.roll(x, LANES - 8, 1) * sa + pltpu.roll(x, 8, 1) * sb


def _rope_t(d, c, sa, sb):
    return d * c + pltpu.roll(d * sa, 8, 1) + pltpu.roll(d * sb, LANES - 8, 1)


def rmsnorm(h, g, name):
    s = h.shape[0]
    tm = _tile(s, (512,))

    def body(h_ref, g_ref, o_ref):
        x = h_ref[...]
        r = lax.rsqrt(jnp.mean(x * x, axis=-1, keepdims=True) + RMS_EPS)
        o_ref[...] = (x * r * g_ref[...]).astype(o_ref.dtype)

    return pl.pallas_call(
        body, grid=(s // tm,), name=name,
        in_specs=[_rowspec(tm, D_MODEL), _fullspec((1, D_MODEL))],
        out_specs=_rowspec(tm, D_MODEL),
        out_shape=jax.ShapeDtypeStruct((s, D_MODEL), ACT),
        compiler_params=_cp("parallel"))(h, g)


def inproj(n, w, tabs, nqk, splits, name):
    s = n.shape[0]
    ntot = w.shape[1]
    assert sum(splits) == ntot and splits[0] == nqk
    tm = _tile(s, (256,))

    def body(n_ref, w_ref, c_ref, sa_ref, sb_ref, *outs):
        res = _dot(n_ref[...], w_ref[...], NN)
        c, sa, sb = c_ref[...], sa_ref[...], sb_ref[...]
        for g in range(nqk // LANES):
            x = res[:, g * LANES:(g + 1) * LANES]
            outs[0][:, g * LANES:(g + 1) * LANES] = _rope(x, c, sa, sb).astype(ACT)
        off = nqk
        for o_ref, wd in zip(outs[1:], splits[1:]):
            o_ref[...] = res[:, off:off + wd].astype(ACT)
            off += wd

    return pl.pallas_call(
        body, grid=(s // tm,), name=name,
        in_specs=[_rowspec(tm, D_MODEL), _fullspec((D_MODEL, ntot))] + [_rowspec(tm, LANES)] * 3,
        out_specs=[_rowspec(tm, wd) for wd in splits],
        out_shape=[jax.ShapeDtypeStruct((s, wd), ACT) for wd in splits],
        compiler_params=_cp("parallel"))(n, w, *tabs)


def ffn_up(n, wg, wu, name):
    s = n.shape[0]
    f = wg.shape[1]
    tm = _tile(s, (512,))
    tf = _tile(f, (1408, 512, 256, 128))

    def body(n_ref, wg_ref, wu_ref, g_ref, u_ref, a_ref):
        a = n_ref[...]
        g = _dot(a, wg_ref[...], NN)
        u = _dot(a, wu_ref[...], NN)
        g_ref[...] = g.astype(ACT)
        u_ref[...] = u.astype(ACT)
        a_ref[...] = (g * _sigmoid(g) * u).astype(ACT)

    wspec = pl.BlockSpec((D_MODEL, tf), lambda j, i: (0, j))
    ospec = pl.BlockSpec((tm, tf), lambda j, i: (i, j))
    return pl.pallas_call(
        body, grid=(f // tf, s // tm), name=name,
        in_specs=[pl.BlockSpec((tm, D_MODEL), lambda j, i: (i, 0)), wspec, wspec],
        out_specs=[ospec] * 3,
        out_shape=[jax.ShapeDtypeStruct((s, f), ACT)] * 3,
        compiler_params=_cp("parallel", "parallel"))(n, wg, wu)


def mm_res(parts, w, h, gnext, name):
    s = h.shape[0]
    tm = _tile(s, (256,))
    widths = [p.shape[1] for p in parts]
    assert sum(widths) == w.shape[0]
    np_ = len(parts)

    def body(*refs):
        p_refs = refs[:np_]
        w_ref, h_ref = refs[np_], refs[np_ + 1]
        rest = refs[np_ + 2:]
        acc = h_ref[...]
        off = 0
        for p_ref, wd in zip(p_refs, widths):
            acc = acc + _dot(p_ref[...], w_ref[off:off + wd, :], NN)
            off += wd
        if gnext is None:
            rest[0][...] = acc
        else:
            g_ref, ho_ref, no_ref = rest
            ho_ref[...] = acc
            r = lax.rsqrt(jnp.mean(acc * acc, axis=-1, keepdims=True) + RMS_EPS)
            no_ref[...] = (acc * r * g_ref[...]).astype(ACT)

    in_specs = [_rowspec(tm, wd) for wd in widths] + [_fullspec(w.shape), _rowspec(tm, D_MODEL)]
    args = list(parts) + [w, h]
    out_specs = [_rowspec(tm, D_MODEL)]
    out_shape = [jax.ShapeDtypeStruct((s, D_MODEL), F32)]
    if gnext is not None:
        in_specs.append(_fullspec((1, D_MODEL)))
        args.append(gnext)
        out_specs.append(_rowspec(tm, D_MODEL))
        out_shape.append(jax.ShapeDtypeStruct((s, D_MODEL), ACT))
    out = pl.pallas_call(
        body, grid=(s // tm,), name=name, in_specs=in_specs, out_specs=out_specs,
        out_shape=out_shape, compiler_params=_cp("parallel"))(*args)
    return (out[0], None) if gnext is None else (out[0], out[1])


def _band_masks(n, max_dist):
    qi = lax.broadcasted_iota(jnp.int32, (BLOCK, BLOCK), 0)
    kj = lax.broadcasted_iota(jnp.int32, (BLOCK, BLOCK), 1)
    m_cur = kj <= qi
    m_prev = (kj > qi) if max_dist == BLOCK - 1 else (kj >= qi)
    m_prev = jnp.logical_and(m_prev, n > 0)
    return m_cur, m_prev


def _fold(a, d):
    return a.reshape(a.shape[0] // d, d * a.shape[1])


def attn_fwd(qk, v, *, d, hkv, max_dist, sink, out_dtype, name):
    s = qk.shape[0]
    kvw = hkv * HEAD_DIM
    wqk = ATT_W + kvw
    assert qk.shape[1] == wqk and (d == 1 or (wqk % ATT_W == 0 and wqk % kvw == 0))
    nb = s // d // BLOCK
    grp = N_HEADS // hkv
    qpb, kpb, koff = wqk // ATT_W, wqk // kvw, ATT_W // kvw

    def body(*refs):
        if sink is None:
            q_ref, kc_ref, kp_ref, vc_ref, vp_ref, o_ref, l_ref = refs
        else:
            q_ref, kc_ref, kp_ref, vc_ref, vp_ref, s_ref, o_ref, l_ref = refs
        n = pl.program_id(1)
        m_cur, m_prev = _band_masks(n, max_dist)
        for h in range(N_HEADS):
            kh = h // grp
            qs = slice(h * HEAD_DIM, (h + 1) * HEAD_DIM)
            ks = slice(kh * HEAD_DIM, (kh + 1) * HEAD_DIM)
            q = q_ref[:, qs]
            sc = jnp.where(m_cur, _dot(q, kc_ref[:, ks], NT) * 0.125, NEG)
            sp = jnp.where(m_prev, _dot(q, kp_ref[:, ks], NT) * 0.125, NEG)
            m = jnp.maximum(jnp.max(sc, axis=-1, keepdims=True), jnp.max(sp, axis=-1, keepdims=True))
            if sink is not None:
                sk = s_ref[h:h + 1, 0:1]
                m = jnp.maximum(m, sk)
            pc = jnp.exp(sc - m)
            pp = jnp.exp(sp - m)
            l = jnp.sum(pc, axis=-1, keepdims=True) + jnp.sum(pp, axis=-1, keepdims=True)
            if sink is not None:
                l = l + jnp.exp(sk - m)
            o = _dot(pc.astype(ACT), vc_ref[:, ks], NN) + _dot(pp.astype(ACT), vp_ref[:, ks], NN)
            o_ref[:, qs] = (o / l).astype(o_ref.dtype)
            l_ref[:, qs] = jnp.broadcast_to(m + jnp.log(l), (BLOCK, HEAD_DIM))

    prev = lambda n: jnp.maximum(n - 1, 0)
    in_specs = [
        pl.BlockSpec((BLOCK, ATT_W), lambda r, n: (n, r * qpb)),
        pl.BlockSpec((BLOCK, kvw), lambda r, n: (n, r * kpb + koff)),
        pl.BlockSpec((BLOCK, kvw), lambda r, n: (prev(n), r * kpb + koff)),
        pl.BlockSpec((BLOCK, kvw), lambda r, n: (n, r)),
        pl.BlockSpec((BLOCK, kvw), lambda r, n: (prev(n), r)),
    ]
    qkf, vf = _fold(qk, d), _fold(v, d)
    args = [qkf, qkf, qkf, vf, vf]
    if sink is not None:
        in_specs.append(_fullspec((N_HEADS, LANES)))
        args.append(sink)
    ospec = pl.BlockSpec((BLOCK, ATT_W), lambda r, n: (n, r))
    o, lse = pl.pallas_call(
        body, grid=(d, nb), name=name, in_specs=in_specs, out_specs=[ospec, ospec],
        out_shape=[jax.ShapeDtypeStruct((s // d, d * ATT_W), out_dtype),
                   jax.ShapeDtypeStruct((s // d, d * ATT_W), F32)],
        compiler_params=_cp("parallel", "parallel"))(*args)
    return o.reshape(s, ATT_W), lse.reshape(s, ATT_W)


def attn_bwd(qk, v, do_src, do_col, o, lse, *, d, hkv, max_dist, sink, acc, name):
    s = qk.shape[0]
    kvw = hkv * HEAD_DIM
    wqk = ATT_W + kvw
    nb = s // d // BLOCK
    grp = N_HEADS // hkv
    qpb, kpb, koff = wqk // ATT_W, wqk // kvw, ATT_W // kvw
    dob = do_src.shape[1] // ATT_W
    has_sink, has_acc = sink is not None, acc is not None

    def body(*refs):
        refs = list(refs)
        q_ref, kc_ref, kp_ref, vc_ref, vp_ref, do_ref, o_ref, l_ref = refs[:8]
        pos = 8
        if has_sink:
            s_ref = refs[pos]
            pos += 1
        if has_acc:
            dqa_ref, dka_ref, dva_ref = refs[pos:pos + 3]
            pos += 3
        dq_ref, dk_ref, dv_ref = refs[pos:pos + 3]
        pos += 3
        if has_sink:
            ds_ref = refs[pos]
            pos += 1
        ck_ref, cv_ref = refs[pos:pos + 2]
        r_id = pl.program_id(0)
        n = pl.program_id(1)

        @pl.when(n == 0)
        def _():
            ck_ref[...] = jnp.zeros_like(ck_ref)
            cv_ref[...] = jnp.zeros_like(cv_ref)

        if has_sink:
            @pl.when(jnp.logical_and(n == 0, r_id == 0))
            def _():
                ds_ref[...] = jnp.zeros_like(ds_ref)

        @pl.when(n < nb)
        def _():
            m_cur, m_prev = _band_masks(n, max_dist)
            cur_k = [None] * hkv
            cur_v = [None] * hkv
            prv_k = [None] * hkv
            prv_v = [None] * hkv
            for h in range(N_HEADS):
                kh = h // grp
                qs = slice(h * HEAD_DIM, (h + 1) * HEAD_DIM)
                ks = slice(kh * HEAD_DIM, (kh + 1) * HEAD_DIM)
                q = q_ref[:, qs]
                kc, kp, vc, vp = kc_ref[:, ks], kp_ref[:, ks], vc_ref[:, ks], vp_ref[:, ks]
                do_h = do_ref[:, qs]
                lse_h = l_ref[:, h * HEAD_DIM:h * HEAD_DIM + 1]
                sc = jnp.where(m_cur, _dot(q, kc, NT) * 0.125, NEG)
                sp = jnp.where(m_prev, _dot(q, kp, NT) * 0.125, NEG)
                pc = jnp.exp(sc - lse_h)
                pp = jnp.exp(sp - lse_h)
                e = jnp.sum(do_h.astype(F32) * o_ref[:, qs].astype(F32), axis=-1, keepdims=True)
                dsc = pc * (_dot(do_h, vc, NT) - e)
                dsp = pp * (_dot(do_h, vp, NT) - e)
                dsc_a, dsp_a = dsc.astype(ACT), dsp.astype(ACT)
                dq = (_dot(dsc_a, kc, NN) + _dot(dsp_a, kp, NN)) * 0.125
                if has_acc:
                    dq = dq + dqa_ref[:, qs]
                dq_ref[:, qs] = dq
                parts = (
                    (cur_k, _dot(dsc_a, q, TN) * 0.125), (prv_k, _dot(dsp_a, q, TN) * 0.125),
                    (cur_v, _dot(pc.astype(ACT), do_h, TN)), (prv_v, _dot(pp.astype(ACT), do_h, TN)))
                for lst, val in parts:
                    lst[kh] = val if lst[kh] is None else lst[kh] + val
                if has_sink:
                    sk = s_ref[h:h + 1, 0:1]
                    dsk = -jnp.sum(jnp.exp(sk - lse_h) * e, axis=0, keepdims=True)
                    ds_ref[h:h + 1, :] = ds_ref[h:h + 1, :] + dsk
            for kh in range(hkv):
                ks = slice(kh * HEAD_DIM, (kh + 1) * HEAD_DIM)
                dk = ck_ref[:, ks] + prv_k[kh]
                dv = cv_ref[:, ks] + prv_v[kh]
                if has_acc:
                    dk = dk + dka_ref[:, ks]
                    dv = dv + dva_ref[:, ks]
                dk_ref[:, ks] = dk
                dv_ref[:, ks] = dv
                ck_ref[:, ks] = cur_k[kh]
                cv_ref[:, ks] = cur_v[kh]

        @pl.when(n == nb)
        def _():
            dk = ck_ref[...]
            dv = cv_ref[...]
            if has_acc:
                dk = dk + dka_ref[...]
                dv = dv + dva_ref[...]
            dk_ref[...] = dk
            dv_ref[...] = dv

    qrow = lambda n: jnp.minimum(n, nb - 1)
    prow = lambda n: jnp.maximum(jnp.minimum(n, nb - 1) - 1, 0)
    krow = lambda n: jnp.maximum(n - 1, 0)
    in_specs = [
        pl.BlockSpec((BLOCK, ATT_W), lambda r, n: (qrow(n), r * qpb)),
        pl.BlockSpec((BLOCK, kvw), lambda r, n: (qrow(n), r * kpb + koff)),
        pl.BlockSpec((BLOCK, kvw), lambda r, n: (prow(n), r * kpb + koff)),
        pl.BlockSpec((BLOCK, kvw), lambda r, n: (qrow(n), r)),
        pl.BlockSpec((BLOCK, kvw), lambda r, n: (prow(n), r)),
        pl.BlockSpec((BLOCK, ATT_W), lambda r, n: (qrow(n), r * dob + do_col)),
        pl.BlockSpec((BLOCK, ATT_W), lambda r, n: (qrow(n), r)),
        pl.BlockSpec((BLOCK, ATT_W), lambda r, n: (qrow(n), r)),
    ]
    qkf, vf = _fold(qk, d), _fold(v, d)
    args = [qkf, qkf, qkf, vf, vf, _fold(do_src, d), _fold(o, d), _fold(lse, d)]
    if has_sink:
        in_specs.append(_fullspec((N_HEADS, LANES)))
        args.append(sink)
    qspec = pl.BlockSpec((BLOCK, ATT_W), lambda r, n: (qrow(n), r))
    kspec = pl.BlockSpec((BLOCK, kvw), lambda r, n: (krow(n), r))
    if has_acc:
        in_specs += [qspec, kspec, kspec]
        args += [_fold(a, d) for a in acc]
    out_specs = [qspec, kspec, kspec]
    out_shape = [jax.ShapeDtypeStruct((s // d, d * ATT_W), F32),
                 jax.ShapeDtypeStruct((s // d, d * kvw), F32),
                 jax.ShapeDtypeStruct((s // d, d * kvw), F32)]
    if has_sink:
        out_specs.append(_fullspec((N_HEADS, LANES)))
        out_shape.append(jax.ShapeDtypeStruct((N_HEADS, LANES), F32))
    out = pl.pallas_call(
        body, grid=(d, nb + 1), name=name, in_specs=in_specs, out_specs=out_specs,
        out_shape=out_shape,
        scratch_shapes=[pltpu.VMEM((BLOCK, kvw), F32), pltpu.VMEM((BLOCK, kvw), F32)],
        compiler_params=_cp("arbitrary", "arbitrary"))(*args)
    res = [out[0].reshape(s, ATT_W), out[1].reshape(s, kvw), out[2].reshape(s, kvw)]
    if has_sink:
        res.append(out[3])
    return res


def combine_fwd(os_, lses, name):
    s = os_[0].shape[0]
    tm = _tile(s, (512,))

    def body(o1, o2, o3, l1, l2, l3, c_ref, l_ref):
        a, b, c = l1[...], l2[...], l3[...]
        m = jnp.maximum(jnp.maximum(a, b), c)
        wa, wb, wc = jnp.exp(a - m), jnp.exp(b - m), jnp.exp(c - m)
        tot = wa + wb + wc
        c_ref[...] = ((wa * o1[...] + wb * o2[...] + wc * o3[...]) / tot).astype(ACT)
        l_ref[...] = m + jnp.log(tot)

    spec = _rowspec(tm, ATT_W)
    return pl.pallas_call(
        body, grid=(s // tm,), name=name, in_specs=[spec] * 6, out_specs=[spec, spec],
        out_shape=[jax.ShapeDtypeStruct((s, ATT_W), ACT), jax.ShapeDtypeStruct((s, ATT_W), F32)],
        compiler_params=_cp("parallel"))(*os_, *lses)


def assemble(parts, tabs, name):
    s = parts[0][0].shape[0]
    tm = _tile(s, (256,))
    widths = [p.shape[1] for p, _ in parts]
    flags = [f for _, f in parts]
    np_ = len(parts)

    def body(*refs):
        p_refs = refs[:np_]
        c_ref, sa_ref, sb_ref, o_ref = refs[np_:]
        c, sa, sb = c_ref[...], sa_ref[...], sb_ref[...]
        off = 0
        for p_ref, wd, fl in zip(p_refs, widths, flags):
            if fl:
                for g in range(wd // LANES):
                    x = p_ref[:, g * LANES:(g + 1) * LANES].astype(F32)
                    o_ref[:, off + g * LANES:off + (g + 1) * LANES] = _rope_t(x, c, sa, sb).astype(ACT)
            else:
                o_ref[:, off:off + wd] = p_ref[...].astype(ACT)
            off += wd

    tot = sum(widths)
    return pl.pallas_call(
        body, grid=(s // tm,), name=name,
        in_specs=[_rowspec(tm, wd) for wd in widths] + [_rowspec(tm, LANES)] * 3,
        out_specs=_rowspec(tm, tot), out_shape=jax.ShapeDtypeStruct((s, tot), ACT),
        compiler_params=_cp("parallel"))(*[p for p, _ in parts], *tabs)


def _ln_stats(x):
    mu = jnp.mean(x, axis=-1, keepdims=True)
    xc = x - mu
    var = jnp.mean(xc * xc, axis=-1, keepdims=True)
    rstd = lax.rsqrt(var + LN_EPS)
    return xc * rstd, rstd


def conv_fwd(ga, gb, cw, cb, lg, lb, name):
    s = ga.shape[0]
    tm = _tile(s, (256,))
    hb = tm // CONV_HALO

    def body(ga_ref, gb_ref, gah_ref, gbh_ref, cw_ref, cb_ref, lg_ref, lb_ref, c_ref, c1_ref, buf):
        i = pl.program_id(0)
        halo = gah_ref[...].astype(F32) * _sigmoid(gbh_ref[...].astype(F32))
        buf[0:CONV_HALO, :] = jnp.where(i > 0, halo, 0.0)
        buf[CONV_HALO:, :] = ga_ref[...].astype(F32) * _sigmoid(gb_ref[...].astype(F32))
        acc = jnp.broadcast_to(cb_ref[...], (tm, CONV_CH))
        for j in range(CONV_WIDTH):
            acc = acc + cw_ref[j:j + 1, :] * buf[pl.ds(CONV_HALO - (CONV_WIDTH - 1) + j, tm), :]
        c1_ref[...] = acc.astype(c1_ref.dtype)
        xh, _ = _ln_stats(acc)
        y = xh * lg_ref[...] + lb_ref[...]
        c_ref[...] = (y * _sigmoid(y)).astype(ACT)

    hspec = pl.BlockSpec((CONV_HALO, CONV_CH), lambda i: (jnp.maximum(i * hb - 1, 0), 0))
    vec = _fullspec((1, CONV_CH))
    spec = _rowspec(tm, CONV_CH)
    return pl.pallas_call(
        body, grid=(s // tm,), name=name,
        in_specs=[spec, spec, hspec, hspec, _fullspec((CONV_WIDTH, CONV_CH)), vec, vec, vec],
        out_specs=[spec, spec],
        out_shape=[jax.ShapeDtypeStruct((s, CONV_CH), ACT), jax.ShapeDtypeStruct((s, CONV_CH), F32)],
        scratch_shapes=[pltpu.VMEM((tm + CONV_HALO, CONV_CH), F32)],
        compiler_params=_cp("parallel"))(ga, gb, ga, gb, cw, cb, lg, lb)


def conv_bwd_ln(c1, dsrc, dcol, lg, lb, name):
    s = c1.shape[0]
    tm = _tile(s, (256,))

    def body(c1_ref, d_ref, lg_ref, lb_ref, o_ref, dg_ref, db_ref):
        @pl.when(pl.program_id(0) == 0)
        def _():
            dg_ref[...] = jnp.zeros_like(dg_ref)
            db_ref[...] = jnp.zeros_like(db_ref)

        xh, rstd = _ln_stats(c1_ref[...].astype(F32))
        y = xh * lg_ref[...] + lb_ref[...]
        sg = _sigmoid(y)
        dy = d_ref[...].astype(F32) * (sg * (1.0 + y * (1.0 - sg)))
        dg_ref[0:1, :] = dg_ref[0:1, :] + jnp.sum(dy * xh, axis=0, keepdims=True)
        db_ref[0:1, :] = db_ref[0:1, :] + jnp.sum(dy, axis=0, keepdims=True)
        dxh = dy * lg_ref[...]
        o_ref[...] = rstd * (dxh - jnp.mean(dxh, axis=-1, keepdims=True)
                             - xh * jnp.mean(dxh * xh, axis=-1, keepdims=True))

    vec = _fullspec((1, CONV_CH))
    acc = _fullspec((8, CONV_CH))
    return pl.pallas_call(
        body, grid=(s // tm,), name=name,
        in_specs=[_rowspec(tm, CONV_CH), _rowspec(tm, CONV_CH, dcol), vec, vec],
        out_specs=[_rowspec(tm, CONV_CH), acc, acc],
        out_shape=[jax.ShapeDtypeStruct((s, CONV_CH), F32)] + [jax.ShapeDtypeStruct((8, CONV_CH), F32)] * 2,
        compiler_params=_cp("arbitrary"))(c1, dsrc, lg, lb)


def conv_bwd_conv(dc1, ga, gb, cw, name):
    s = ga.shape[0]
    tm = _tile(s, (256,))
    hb = tm // CONV_HALO
    nt = s // tm
    last_h = s // CONV_HALO - 1

    def body(d_ref, dn_ref, ga_ref, gb_ref, gah_ref, gbh_ref, cw_ref,
             dga_ref, dgb_ref, dw_ref, db_ref, dbuf, cbuf):
        i = pl.program_id(0)

        @pl.when(i == 0)
        def _():
            dw_ref[...] = jnp.zeros_like(dw_ref)
            db_ref[...] = jnp.zeros_like(db_ref)

        d = d_ref[...]
        dbuf[0:tm, :] = d
        dbuf[tm:, :] = jnp.where(i < nt - 1, dn_ref[...], 0.0)
        halo = gah_ref[...].astype(F32) * _sigmoid(gbh_ref[...].astype(F32))
        cbuf[0:CONV_HALO, :] = jnp.where(i > 0, halo, 0.0)
        a = ga_ref[...].astype(F32)
        sg = _sigmoid(gb_ref[...].astype(F32))
        cbuf[CONV_HALO:, :] = a * sg
        dc0 = jnp.zeros((tm, CONV_CH), F32)
        for j in range(CONV_WIDTH):
            dc0 = dc0 + cw_ref[j:j + 1, :] * dbuf[pl.ds(CONV_WIDTH - 1 - j, tm), :]
            tap = cbuf[pl.ds(CONV_HALO - (CONV_WIDTH - 1) + j, tm), :]
            dw_ref[j:j + 1, :] = dw_ref[j:j + 1, :] + jnp.sum(d * tap, axis=0, keepdims=True)
        db_ref[0:1, :] = db_ref[0:1, :] + jnp.sum(d, axis=0, keepdims=True)
        dga_ref[...] = (dc0 * sg).astype(ACT)
        dgb_ref[...] = (dc0 * a * sg * (1.0 - sg)).astype(ACT)

    spec = _rowspec(tm, CONV_CH)
    hprev = pl.BlockSpec((CONV_HALO, CONV_CH), lambda i: (jnp.maximum(i * hb - 1, 0), 0))
    hnext = pl.BlockSpec((CONV_HALO, CONV_CH), lambda i: (jnp.minimum((i + 1) * hb, last_h), 0))
    return pl.pallas_call(
        body, grid=(nt,), name=name,
        in_specs=[spec, hnext, spec, spec, hprev, hprev, _fullspec((CONV_WIDTH, CONV_CH))],
        out_specs=[spec, spec, _fullspec((CONV_HALO, CONV_CH)), _fullspec((8, CONV_CH))],
        out_shape=[jax.ShapeDtypeStruct((s, CONV_CH), ACT)] * 2
        + [jax.ShapeDtypeStruct((CONV_HALO, CONV_CH), F32), jax.ShapeDtypeStruct((8, CONV_CH), F32)],
        scratch_shapes=[pltpu.VMEM((tm + CONV_HALO, CONV_CH), F32)] * 2,
        compiler_params=_cp("arbitrary"))(dc1, dc1, ga, gb, ga, gb, cw)


_GELU_K = math.sqrt(2.0 / math.pi)
_GELU_C = 0.044715


def _gelu(x):
    return 0.5 * x * (1.0 + jnp.tanh(_GELU_K * (x + _GELU_C * x * x * x)))


def _gelu_grad(x):
    t = jnp.tanh(_GELU_K * (x + _GELU_C * x * x * x))
    return 0.5 * (1.0 + t) + 0.5 * x * (1.0 - t * t) * _GELU_K * (1.0 + 3.0 * _GELU_C * x * x)


def _tril():
    qi = lax.broadcasted_iota(jnp.int32, (BLOCK, BLOCK), 0)
    kj = lax.broadcasted_iota(jnp.int32, (BLOCK, BLOCK), 1)
    return kj <= qi


def gmlp_fwd(z, lg, lb, sw, sbt, name):
    s = z.shape[0]

    def body(z_ref, lg_ref, lb_ref, sw_ref, sb_ref, o_ref):
        zz = _gelu(z_ref[...].astype(F32))
        u = zz[:, :D_CH]
        xh, _ = _ln_stats(zz[:, D_CH:])
        gn = (xh * lg_ref[...] + lb_ref[...]).astype(ACT)
        tril = _tril()
        for g in range(D_GROUPS):
            cs = slice(g * HEAD_DIM, (g + 1) * HEAD_DIM)
            w = jnp.where(tril, sw_ref[g], 0.0).astype(ACT)
            mixed = _dot(w, gn[:, cs], NN) + sb_ref[:, g:g + 1]
            o_ref[:, cs] = (u[:, cs] * mixed).astype(ACT)

    return pl.pallas_call(
        body, grid=(s // BLOCK,), name=name,
        in_specs=[_rowspec(BLOCK, 2 * D_CH), _fullspec((1, D_CH)), _fullspec((1, D_CH)),
                  _fullspec((D_GROUPS, BLOCK, BLOCK)), _fullspec((BLOCK, D_GROUPS))],
        out_specs=_rowspec(BLOCK, D_CH), out_shape=jax.ShapeDtypeStruct((s, D_CH), ACT),
        compiler_params=_cp("parallel"))(z, lg, lb, sw, sbt)


def gmlp_bwd(z, dsrc, dcol, lg, lb, sw, sbt, name):
    s = z.shape[0]

    def body(z_ref, d_ref, lg_ref, lb_ref, sw_ref, sb_ref, dz_ref, dw_ref, dsb_ref, dg_ref, db_ref, dgn_buf):
        @pl.when(pl.program_id(0) == 0)
        def _():
            dw_ref[...] = jnp.zeros_like(dw_ref)
            dsb_ref[...] = jnp.zeros_like(dsb_ref)
            dg_ref[...] = jnp.zeros_like(dg_ref)
            db_ref[...] = jnp.zeros_like(db_ref)

        zr = z_ref[...].astype(F32)
        zz = _gelu(zr)
        u = zz[:, :D_CH]
        xh, rstd = _ln_stats(zz[:, D_CH:])
        gn = (xh * lg_ref[...] + lb_ref[...]).astype(ACT)
        dd = d_ref[...].astype(F32)
        tril = _tril()
        for g in range(D_GROUPS):
            cs = slice(g * HEAD_DIM, (g + 1) * HEAD_DIM)
            w = jnp.where(tril, sw_ref[g], 0.0).astype(ACT)
            gn_g = gn[:, cs]
            mixed = _dot(w, gn_g, NN) + sb_ref[:, g:g + 1]
            du = dd[:, cs] * mixed
            dmix = dd[:, cs] * u[:, cs]
            dmix_a = dmix.astype(ACT)
            dz_ref[:, cs] = (du * _gelu_grad(zr[:, cs])).astype(ACT)
            dw_ref[g] = dw_ref[g] + jnp.where(tril, _dot(dmix_a, gn_g, NT), 0.0)
            dsb_ref[:, g:g + 1] = dsb_ref[:, g:g + 1] + jnp.sum(dmix, axis=-1, keepdims=True)
            dgn_buf[:, cs] = _dot(w, dmix_a, TN)
        dgn = dgn_buf[...]
        dg_ref[0:1, :] = dg_ref[0:1, :] + jnp.sum(dgn * xh, axis=0, keepdims=True)
        db_ref[0:1, :] = db_ref[0:1, :] + jnp.sum(dgn, axis=0, keepdims=True)
        dxh = dgn * lg_ref[...]
        dgate = rstd * (dxh - jnp.mean(dxh, axis=-1, keepdims=True)
                        - xh * jnp.mean(dxh * xh, axis=-1, keepdims=True))
        dz_ref[:, D_CH:] = (dgate * _gelu_grad(zr[:, D_CH:])).astype(ACT)

    vec = _fullspec((1, D_CH))
    acc = _fullspec((8, D_CH))
    return pl.pallas_call(
        body, grid=(s // BLOCK,), name=name,
        in_specs=[_rowspec(BLOCK, 2 * D_CH), _rowspec(BLOCK, D_CH, dcol), vec, vec,
                  _fullspec((D_GROUPS, BLOCK, BLOCK)), _fullspec((BLOCK, D_GROUPS))],
        out_specs=[_rowspec(BLOCK, 2 * D_CH), _fullspec((D_GROUPS, BLOCK, BLOCK)),
                   _fullspec((BLOCK, D_GROUPS)), acc, acc],
        out_shape=[jax.ShapeDtypeStruct((s, 2 * D_CH), ACT),
                   jax.ShapeDtypeStruct((D_GROUPS, BLOCK, BLOCK), F32),
                   jax.ShapeDtypeStruct((BLOCK, D_GROUPS), F32),
                   jax.ShapeDtypeStruct((8, D_CH), F32), jax.ShapeDtypeStruct((8, D_CH), F32)],
        scratch_shapes=[pltpu.VMEM((BLOCK, D_CH), F32)],
        compiler_params=_cp("arbitrary"))(z, dsrc, lg, lb, sw, sbt)


def _rms_bwd(dn, x, g):
    r = lax.rsqrt(jnp.mean(x * x, axis=-1, keepdims=True) + RMS_EPS)
    u = dn * g
    dx = r * u - x * (r * r * r) * jnp.mean(x * u, axis=-1, keepdims=True)
    return dx, dn * x * r


def final_loss(h, g, target, name):
    s = h.shape[0]
    tm = _tile(s, (256,))

    def body(h_ref, g_ref, t_ref, loss_ref, dh_ref, dg_ref):
        @pl.when(pl.program_id(0) == 0)
        def _():
            loss_ref[...] = jnp.zeros_like(loss_ref)
            dg_ref[...] = jnp.zeros_like(dg_ref)

        x = h_ref[...]
        r = lax.rsqrt(jnp.mean(x * x, axis=-1, keepdims=True) + RMS_EPS)
        diff = x * r * g_ref[...] - t_ref[...]
        part = jnp.sum(jnp.sum(diff * diff, axis=-1, keepdims=True), axis=0, keepdims=True)
        loss_ref[...] = loss_ref[...] + part * (0.5 / D_MODEL)
        dx, dgt = _rms_bwd(diff * (1.0 / D_MODEL), x, g_ref[...])
        dh_ref[...] = dx
        dg_ref[0:1, :] = dg_ref[0:1, :] + jnp.sum(dgt, axis=0, keepdims=True)

    spec = _rowspec(tm, D_MODEL)
    return pl.pallas_call(
        body, grid=(s // tm,), name=name,
        in_specs=[spec, _fullspec((1, D_MODEL)), spec],
        out_specs=[_fullspec((8, LANES)), spec, _fullspec((8, D_MODEL))],
        out_shape=[jax.ShapeDtypeStruct((8, LANES), F32), jax.ShapeDtypeStruct((s, D_MODEL), F32),
                   jax.ShapeDtypeStruct((8, D_MODEL), F32)],
        compiler_params=_cp("arbitrary"))(h, g, target)


def mm_nt(dy, w, name):
    s, n = dy.shape
    k = w.shape[0]
    tm = _tile(s, (512,))
    tk = _tile(k, (512,))

    def body(d_ref, w_ref, o_ref):
        o_ref[...] = _dot(d_ref[...].astype(ACT), w_ref[...], NT).astype(ACT)

    return pl.pallas_call(
        body, grid=(k // tk, s // tm), name=name,
        in_specs=[pl.BlockSpec((tm, n), lambda j, i: (i, 0)), pl.BlockSpec((tk, n), lambda j, i: (j, 0))],
        out_specs=pl.BlockSpec((tm, tk), lambda j, i: (i, j)),
        out_shape=jax.ShapeDtypeStruct((s, k), ACT),
        compiler_params=_cp("parallel", "parallel"))(dy, w)


def ffn_down_bwd(dh, wd, gate, up, name):
    s = dh.shape[0]
    f = wd.shape[0]
    tm = _tile(s, (512,))
    tf = _tile(f, (1408, 512, 256, 128))

    def body(d_ref, w_ref, g_ref, u_ref, dg_ref, du_ref):
        dact = _dot(d_ref[...].astype(ACT), w_ref[...], NT)
        g = g_ref[...].astype(F32)
        sg = _sigmoid(g)
        dg_ref[...] = (dact * u_ref[...].astype(F32) * (sg * (1.0 + g * (1.0 - sg)))).astype(ACT)
        du_ref[...] = (dact * g * sg).astype(ACT)

    tile = pl.BlockSpec((tm, tf), lambda j, i: (i, j))
    return pl.pallas_call(
        body, grid=(f // tf, s // tm), name=name,
        in_specs=[pl.BlockSpec((tm, D_MODEL), lambda j, i: (i, 0)),
                  pl.BlockSpec((tf, D_MODEL), lambda j, i: (j, 0)), tile, tile],
        out_specs=[tile, tile], out_shape=[jax.ShapeDtypeStruct((s, f), ACT)] * 2,
        compiler_params=_cp("parallel", "parallel"))(dh, wd, gate, up)


def mm_nt_rms(parts, h, g, dh, name):
    s = h.shape[0]
    tm = _tile(s, (256,))
    np_ = len(parts)

    def body(*refs):
        d_refs = refs[:np_]
        w_refs = refs[np_:2 * np_]
        h_ref, g_ref, dh_ref, o_ref, dg_ref = refs[2 * np_:]

        @pl.when(pl.program_id(0) == 0)
        def _():
            dg_ref[...] = jnp.zeros_like(dg_ref)

        dn = None
        for d_ref, w_ref in zip(d_refs, w_refs):
            t = _dot(d_ref[...], w_ref[...], NT)
            dn = t if dn is None else dn + t
        dx, dgt = _rms_bwd(dn, h_ref[...], g_ref[...])
        o_ref[...] = dh_ref[...] + dx
        dg_ref[0:1, :] = dg_ref[0:1, :] + jnp.sum(dgt, axis=0, keepdims=True)

    spec = _rowspec(tm, D_MODEL)
    return pl.pallas_call(
        body, grid=(s // tm,), name=name,
        in_specs=[_rowspec(tm, d.shape[1]) for d, _ in parts] + [_fullspec(w.shape) for _, w in parts]
        + [spec, _fullspec((1, D_MODEL)), spec],
        out_specs=[spec, _fullspec((8, D_MODEL))],
        out_shape=[jax.ShapeDtypeStruct((s, D_MODEL), F32), jax.ShapeDtypeStruct((8, D_MODEL), F32)],
        compiler_params=_cp("arbitrary"))(*[d for d, _ in parts], *[w for _, w in parts], h, g, dh)


def mm_tn(a, b, name):
    s, k = a.shape
    n = b.shape[1]
    tk = _tile(k, (512, 1408, 256, 128))
    tn = _tile(n, (1408, 1280, 1024, 896, 512, 256, 128))
    ts = _tile(s, (512,))

    def body(a_ref, b_ref, o_ref):
        t = _dot(a_ref[...].astype(ACT), b_ref[...].astype(ACT), TN)

        @pl.when(pl.program_id(2) == 0)
        def _():
            o_ref[...] = t

        @pl.when(pl.program_id(2) > 0)
        def _():
            o_ref[...] = o_ref[...] + t

    return pl.pallas_call(
        body, grid=(k // tk, n // tn, s // ts), name=name,
        in_specs=[pl.BlockSpec((ts, tk), lambda i, j, t: (t, i)), pl.BlockSpec((ts, tn), lambda i, j, t: (t, j))],
        out_specs=pl.BlockSpec((tk, tn), lambda i, j, t: (i, j)),
        out_shape=jax.ShapeDtypeStruct((k, n), F32),
        compiler_params=_cp("parallel", "parallel", "arbitrary"))(a, b)


def _adamw_math(w, g, m, v):
    m = ADAM_B1 * m + (1.0 - ADAM_B1) * g
    v = ADAM_B2 * v + (1.0 - ADAM_B2) * (g * g)
    m_hat = m / (1.0 - ADAM_B1 ** ADAM_STEP)
    v_hat = v / (1.0 - ADAM_B2 ** ADAM_STEP)
    delta = -ADAM_LR * (m_hat / (jnp.sqrt(v_hat) + ADAM_EPS) + ADAM_WD * w)
    return delta, m, v


def adamw_big(w, g, m, v, name):
    r, c = w.shape
    tr = _tile(r, (256, 128, 64, 32, 16, 8))

    def body(w_ref, g_ref, m_ref, v_ref, d_ref, mo_ref, vo_ref):
        d, mm, vv = _adamw_math(w_ref[...], g_ref[...], m_ref[...], v_ref[...])
        d_ref[...] = d
        mo_ref[...] = mm
        vo_ref[...] = vv

    spec = _rowspec(tr, c)
    return pl.pallas_call(
        body, grid=(r // tr,), name=name, in_specs=[spec] * 4, out_specs=[spec] * 3,
        out_shape=[jax.ShapeDtypeStruct((r, c), F32)] * 3,
        compiler_params=_cp("parallel"))(w, g, m, v)


def adamw_small(ws, gs, ms, vs, name):
    k = len(ws)

    def body(*refs):
        for i in range(k):
            w_ref, g_ref, m_ref, v_ref = (refs[j * k + i] for j in range(4))
            d, mm, vv = _adamw_math(w_ref[...], g_ref[...], m_ref[...], v_ref[...])
            refs[4 * k + i][...] = d
            refs[5 * k + i][...] = mm
            refs[6 * k + i][...] = vv

    shapes = [jax.ShapeDtypeStruct(w.shape, F32) for w in ws]
    specs = [_fullspec(w.shape) for w in ws]
    out = pl.pallas_call(
        body, grid=(1,), name=name, in_specs=specs * 4, out_specs=specs * 3, out_shape=shapes * 3,
        compiler_params=_cp("arbitrary"))(*ws, *gs, *ms, *vs)
    return out[:k], out[k:2 * k], out[2 * k:]


def sum_slots(x, out_dtype, name):
    g, r, c = x.shape
    tr = _tile(r, (256, 128, 64, 32, 16, 8))

    def body(x_ref, o_ref):
        acc = x_ref[0].astype(F32)
        for i in range(1, g):
            acc = acc + x_ref[i].astype(F32)
        o_ref[...] = acc.astype(o_ref.dtype)

    return pl.pallas_call(
        body, grid=(r // tr,), name=name,
        in_specs=[pl.BlockSpec((g, tr, c), lambda i: (0, i, 0))], out_specs=_rowspec(tr, c),
        out_shape=jax.ShapeDtypeStruct((r, c), out_dtype),
        compiler_params=_cp("parallel"))(x)


def _group_exchange(x, axes, gather, name):
    gsz = 2 ** len(axes)
    out_shape = (gsz,) + x.shape if gather else x.shape
    if not gather:
        assert x.shape[0] == gsz

    def body(x_ref, o_ref, send_sems, recv_sems, local_sem):
        pos = {a: lax.axis_index(a) for a in AXES}
        me = 0
        for a in axes:
            me = me * 2 + pos[a]
        own = pltpu.make_async_copy(x_ref if gather else x_ref.at[me], o_ref.at[me], local_sem)
        own.start()
        copies = []
        for mask in range(1, gsz):
            peer = dict(pos)
            for idx, a in enumerate(axes):
                if (mask >> (len(axes) - 1 - idx)) & 1:
                    peer[a] = 1 - pos[a]
            prank = 0
            for a in axes:
                prank = prank * 2 + peer[a]
            copies.append(pltpu.make_async_remote_copy(
                src_ref=x_ref if gather else x_ref.at[prank], dst_ref=o_ref.at[me],
                send_sem=send_sems.at[mask - 1], recv_sem=recv_sems.at[mask - 1],
                device_id=tuple(peer[a] for a in AXES), device_id_type=MESH))
        for cp in copies:
            cp.start()
        for cp in copies:
            cp.wait()
        own.wait()

    hbm = pl.BlockSpec(memory_space=pltpu.HBM)
    return pl.pallas_call(
        body, name=name, in_specs=[hbm], out_specs=hbm,
        out_shape=jax.ShapeDtypeStruct(out_shape, x.dtype),
        scratch_shapes=[pltpu.SemaphoreType.DMA((gsz - 1,)), pltpu.SemaphoreType.DMA((gsz - 1,)),
                        pltpu.SemaphoreType.DMA(())],
        )(x)


def all_gather8(x, name):
    g4 = _group_exchange(x, ("x", "y"), True, name + "_xy")
    g2 = _group_exchange(g4, ("c",), True, name + "_c")
    return jnp.swapaxes(g2, 0, 1).reshape((8,) + x.shape)


def reduce_scatter8(x, name):
    r, c = x.shape[1:]
    x = jnp.swapaxes(x.reshape(4, 2, r, c), 0, 1)
    e1 = _group_exchange(x, ("c",), False, name + "_c")
    p1 = sum_slots(e1.reshape(2, 4 * r, c), ACT, name + "_sum_c").reshape(4, r, c)
    e2 = _group_exchange(p1, ("x", "y"), False, name + "_xy")
    return sum_slots(e2, F32, name + "_sum_xy")


def _local_step(x, target, W, P):
    s = x.shape[0]
    tabs = _rope_tables(s)
    sinkb = jnp.broadcast_to(P["ev_sinks"].reshape(N_HEADS, 1), (N_HEADS, LANES))
    sbt = P["od_spatial_b"].reshape(D_GROUPS, BLOCK).T
    sw = P["od_spatial_w"].reshape(D_GROUPS, BLOCK, BLOCK)
    fg = P["ffn_norm_g"]

    n0 = rmsnorm(x, P["ev_norm_g"], "rms_in")
    qk_e, v_e, ga, gb = inproj(n0, W["ev_w_in"], tabs, ATT_W + 128, (ATT_W + 128, 128, CONV_CH, CONV_CH), "ev_inproj")
    a_e, lse_e = attn_fwd(qk_e, v_e, d=1, hkv=A_KV_HEADS, max_dist=BLOCK - 1, sink=sinkb, out_dtype=ACT,
                          name="ev_attn")
    c_act, c1 = conv_fwd(ga, gb, P["ev_conv_w"], P["ev_conv_b"], P["ev_conv_ln_g"], P["ev_conv_ln_b"], "ev_conv")
    h1, n1 = mm_res([a_e, c_act], W["ev_w_out"], x, fg[0:1], "ev_outproj")
    gate0, up0, act0 = ffn_up(n1, W["ffn_w_gate"][0], W["ffn_w_up"][0], "ffn0_up")
    h2, n2 = mm_res([act0], W["ffn_w_down"][0], h1, P["od_norm_g"], "ffn0_down")
    qk_o, v_o, z = inproj(n2, W["od_w_in"], tabs, 2 * ATT_W, (2 * ATT_W, ATT_W, 2 * D_CH), "od_inproj")
    outs, lses = [], []
    for window, dil in DILATED:
        assert window // dil == BLOCK
        o_r, l_r = attn_fwd(qk_o, v_o, d=dil, hkv=N_HEADS, max_dist=BLOCK, sink=None, out_dtype=F32,
                            name="od_attn_d%d" % dil)
        outs.append(o_r)
        lses.append(l_r)
    c_out, lse_o = combine_fwd(outs, lses, "od_combine")
    d_out = gmlp_fwd(z, P["od_sgu_ln_g"], P["od_sgu_ln_b"], sw, sbt, "od_gmlp")
    h3, n3 = mm_res([c_out, d_out], W["od_w_out"], h2, fg[1:2], "od_outproj")
    gate1, up1, act1 = ffn_up(n3, W["ffn_w_gate"][1], W["ffn_w_up"][1], "ffn1_up")
    h4, _ = mm_res([act1], W["ffn_w_down"][1], h3, None, "ffn1_down")
    loss_part, dh4, dg_final = final_loss(h4, P["final_norm_g"], target, "loss_head")

    G = {}

    def ffn_bwd(layer, dh_out, h_in, n_in, gate, up, act):
        wg, wu, wd = W["ffn_w_gate"][layer], W["ffn_w_up"][layer], W["ffn_w_down"][layer]
        tag = "ffn%d" % layer
        dgate, dup = ffn_down_bwd(dh_out, wd, gate, up, tag + "_down_bwd")
        g_wd = mm_tn(act, dh_out, tag + "_dwd")
        dh_in, dgn = mm_nt_rms([(dgate, wg), (dup, wu)], h_in, fg[layer:layer + 1], dh_out, tag + "_up_bwd")
        g_wg = mm_tn(n_in, dgate, tag + "_dwg")
        g_wu = mm_tn(n_in, dup, tag + "_dwu")
        return dh_in, dgn[0:1], g_wg, g_wu, g_wd

    dh3, dgn_f1, gwg1, gwu1, gwd1 = ffn_bwd(1, dh4, h3, n3, gate1, up1, act1)

    dcd = mm_nt(dh3, W["od_w_out"], "od_outproj_bwd")
    G["od_w_out"] = jnp.concatenate([mm_tn(c_out, dh3, "od_dwo_c"), mm_tn(d_out, dh3, "od_dwo_d")], axis=0)
    dz, g_sw, g_sbt, g_slg, g_slb = gmlp_bwd(z, dcd, 1, P["od_sgu_ln_g"], P["od_sgu_ln_b"], sw, sbt, "od_gmlp_bwd")
    acc = None
    for window, dil in DILATED:
        acc = attn_bwd(qk_o, v_o, dcd, 0, c_out, lse_o, d=dil, hkv=N_HEADS, max_dist=BLOCK, sink=None, acc=acc,
                       name="od_attn_bwd_d%d" % dil)
    dproj_o = assemble([(acc[0], True), (acc[1], True), (acc[2], False), (dz, False)], tabs, "od_dproj")
    dh2, dgn_od = mm_nt_rms([(dproj_o, W["od_w_in"])], h2, P["od_norm_g"], dh3, "od_inproj_bwd")
    G["od_w_in"] = mm_tn(n2, dproj_o, "od_dwi")

    dh1, dgn_f0, gwg0, gwu0, gwd0 = ffn_bwd(0, dh2, h1, n1, gate0, up0, act0)
    G["ffn_w_gate"] = jnp.stack([gwg0, gwg1])
    G["ffn_w_up"] = jnp.stack([gwu0, gwu1])
    G["ffn_w_down"] = jnp.stack([gwd0, gwd1])

    dac = mm_nt(dh1, W["ev_w_out"], "ev_outproj_bwd")
    G["ev_w_out"] = jnp.concatenate([mm_tn(a_e, dh1, "ev_dwo_a"), mm_tn(c_act, dh1, "ev_dwo_c")], axis=0)
    dc1, g_clg, g_clb = conv_bwd_ln(c1, dac, 1, P["ev_conv_ln_g"], P["ev_conv_ln_b"], "ev_conv_bwd_ln")
    dga, dgb, g_cw, g_cb = conv_bwd_conv(dc1, ga, gb, P["ev_conv_w"], "ev_conv_bwd")
    dq, dk, dv, dsink = attn_bwd(qk_e, v_e, dac, 0, a_e, lse_e, d=1, hkv=A_KV_HEADS, max_dist=BLOCK - 1,
                                 sink=sinkb, acc=None, name="ev_attn_bwd")
    dproj_e = assemble([(dq, True), (dk, True), (dv, False), (dga, False), (dgb, False)], tabs, "ev_dproj")
    dx, dgn_ev = mm_nt_rms([(dproj_e, W["ev_w_in"])], x, P["ev_norm_g"], dh1, "ev_inproj_bwd")
    G["ev_w_in"] = mm_tn(n0, dproj_e, "ev_dwi")

    small = {
        "ev_norm_g": dgn_ev[0:1],
        "ev_sinks": dsink[:, 0:1].reshape(1, N_HEADS),
        "ev_conv_w": g_cw[:CONV_WIDTH],
        "ev_conv_b": g_cb[0:1],
        "ev_conv_ln_g": g_clg[0:1],
        "ev_conv_ln_b": g_clb[0:1],
        "od_norm_g": dgn_od[0:1],
        "od_sgu_ln_g": g_slg[0:1],
        "od_sgu_ln_b": g_slb[0:1],
        "od_spatial_w": g_sw.reshape(D_GROUPS * BLOCK, BLOCK),
        "od_spatial_b": g_sbt.T,
        "ffn_norm_g": jnp.concatenate([dgn_f0, dgn_f1], axis=0),
        "final_norm_g": dg_final[0:1],
    }
    return loss_part, dx, G, small


PACK_W = 1024
BIG = ("ev_w_in", "ev_w_out", "od_w_in", "od_w_out", "ffn_w_gate", "ffn_w_up", "ffn_w_down")
COL_SHARDED = ("ev_w_in", "od_w_in", "ffn_w_gate", "ffn_w_up")


def _unshard(name, g):
    nd = g.ndim - 1
    if name in COL_SHARDED:
        g = jnp.moveaxis(g, 0, nd - 1)
        return g.reshape(g.shape[:nd - 1] + (g.shape[nd - 1] * g.shape[nd],))
    g = jnp.moveaxis(g, 0, nd - 2)
    return g.reshape(g.shape[:nd - 2] + (g.shape[nd - 2] * g.shape[nd - 1], g.shape[nd]))


def _shard_slots(name, full, shard_shape):
    nd = full.ndim
    if name in COL_SHARDED:
        t = full.reshape(full.shape[:nd - 1] + (8, shard_shape[-1]))
        return jnp.moveaxis(t, nd - 1, 0)
    t = full.reshape(full.shape[:nd - 2] + (8, shard_shape[-2], full.shape[-1]))
    return jnp.moveaxis(t, nd - 2, 0)


def kernel(x, ev_norm_g, ev_w_in, ev_sinks, ev_conv_w, ev_conv_b, ev_conv_ln_g, ev_conv_ln_b, ev_w_out, od_norm_g, od_w_in, od_sgu_ln_g, od_sgu_ln_b, od_spatial_w, od_spatial_b, od_w_out, ffn_norm_g, ffn_w_gate, ffn_w_up, ffn_w_down, final_norm_g, loss_target, m_ev_norm_g, m_ev_w_in, m_ev_sinks, m_ev_conv_w, m_ev_conv_b, m_ev_conv_ln_g, m_ev_conv_ln_b, m_ev_w_out, m_od_norm_g, m_od_w_in, m_od_sgu_ln_g, m_od_sgu_ln_b, m_od_spatial_w, m_od_spatial_b, m_od_w_out, m_ffn_norm_g, m_ffn_w_gate, m_ffn_w_up, m_ffn_w_down, m_final_norm_g, v_ev_norm_g, v_ev_w_in, v_ev_sinks, v_ev_conv_w, v_ev_conv_b, v_ev_conv_ln_g, v_ev_conv_ln_b, v_ev_w_out, v_od_norm_g, v_od_w_in, v_od_sgu_ln_g, v_od_sgu_ln_b, v_od_spatial_w, v_od_spatial_b, v_od_w_out, v_ffn_norm_g, v_ffn_w_gate, v_ffn_w_up, v_ffn_w_down, v_final_norm_g):
    names = ["ev_norm_g", "ev_w_in", "ev_sinks", "ev_conv_w", "ev_conv_b", "ev_conv_ln_g", "ev_conv_ln_b", "ev_w_out",
             "od_norm_g", "od_w_in", "od_sgu_ln_g", "od_sgu_ln_b", "od_spatial_w", "od_spatial_b", "od_w_out",
             "ffn_norm_g", "ffn_w_gate", "ffn_w_up", "ffn_w_down", "final_norm_g"]
    wts = dict(zip(names, [ev_norm_g, ev_w_in, ev_sinks, ev_conv_w, ev_conv_b, ev_conv_ln_g, ev_conv_ln_b, ev_w_out,
                           od_norm_g, od_w_in, od_sgu_ln_g, od_sgu_ln_b, od_spatial_w, od_spatial_b, od_w_out,
                           ffn_norm_g, ffn_w_gate, ffn_w_up, ffn_w_down, final_norm_g]))
    mom = dict(zip(names, [m_ev_norm_g, m_ev_w_in, m_ev_sinks, m_ev_conv_w, m_ev_conv_b, m_ev_conv_ln_g, m_ev_conv_ln_b,
                           m_ev_w_out, m_od_norm_g, m_od_w_in, m_od_sgu_ln_g, m_od_sgu_ln_b, m_od_spatial_w,
                           m_od_spatial_b, m_od_w_out, m_ffn_norm_g, m_ffn_w_gate, m_ffn_w_up, m_ffn_w_down,
                           m_final_norm_g]))
    vel = dict(zip(names, [v_ev_norm_g, v_ev_w_in, v_ev_sinks, v_ev_conv_w, v_ev_conv_b, v_ev_conv_ln_g, v_ev_conv_ln_b,
                           v_ev_w_out, v_od_norm_g, v_od_w_in, v_od_sgu_ln_g, v_od_sgu_ln_b, v_od_spatial_w,
                           v_od_spatial_b, v_od_w_out, v_ffn_norm_g, v_ffn_w_gate, v_ffn_w_up, v_ffn_w_down,
                           v_final_norm_g]))
    me = 4 * lax.axis_index("x") + 2 * lax.axis_index("y") + lax.axis_index("c")

    shard = {n: (wts[n] if n.startswith("ffn") else wts[n][0]) for n in BIG}
    rows = {n: shard[n].size // PACK_W for n in BIG}
    packed = jnp.concatenate([shard[n].astype(ACT).reshape(rows[n], PACK_W) for n in BIG], axis=0)
    gathered = all_gather8(packed, "ag_big")
    W = {}
    off = 0
    for n in BIG:
        g = gathered[:, off:off + rows[n]].reshape((8,) + shard[n].shape)
        W[n] = _unshard(n, g)
        off += rows[n]

    sp = jnp.zeros((40, LANES), F32)
    sp = sp.at[0:CONV_WIDTH, 0:64].set(ev_conv_w[0])
    sp = sp.at[32, :].set(od_norm_g[0])
    sp = sp.at[33, 0:64].set(od_sgu_ln_g[0])
    sp = sp.at[34, 0:64].set(od_sgu_ln_b[0])
    spg = _group_exchange(sp, AXES, True, "ag_small")
    P = {
        "ev_norm_g": ev_norm_g, "ev_sinks": ev_sinks, "ev_conv_b": ev_conv_b,
        "ev_conv_ln_g": ev_conv_ln_g, "ev_conv_ln_b": ev_conv_ln_b,
        "od_spatial_w": od_spatial_w, "od_spatial_b": od_spatial_b, "ffn_norm_g": ffn_norm_g,
        "final_norm_g": final_norm_g.reshape(1, D_MODEL),
        "ev_conv_w": jnp.moveaxis(spg[:, 0:CONV_WIDTH, 0:64], 0, 1).reshape(CONV_WIDTH, CONV_CH),
        "od_norm_g": spg[:, 32, :].reshape(1, D_MODEL),
        "od_sgu_ln_g": spg[:, 33, 0:64].reshape(1, D_CH),
        "od_sgu_ln_b": spg[:, 34, 0:64].reshape(1, D_CH),
    }

    loss_part, dx, G, small = _local_step(x[0], loss_target[0], W, P)
    loss = lax.psum(loss_part[0, 0], AXES)

    slots = jnp.concatenate(
        [_shard_slots(n, G[n], shard[n].shape).astype(ACT).reshape(8, rows[n], PACK_W) for n in BIG], axis=1)
    gsh = reduce_scatter8(slots, "rs_big")
    out_g, out_d, out_m, out_v = {}, {}, {}, {}
    off = 0
    for n in BIG:
        shp = wts[n].shape
        two_d = (-1, shp[-1])
        g = gsh[off:off + rows[n]].reshape(shard[n].shape)
        off += rows[n]
        d, mm, vv = adamw_big(wts[n].reshape(two_d), g.reshape(two_d), mom[n].reshape(two_d),
                              vel[n].reshape(two_d), "adamw_" + n)
        out_g[n], out_d[n], out_m[n], out_v[n] = g.reshape(shp), d.reshape(shp), mm.reshape(shp), vv.reshape(shp)

    wide = ["ev_norm_g", "ev_sinks", "ev_conv_w", "ev_conv_b", "ev_conv_ln_g", "ev_conv_ln_b", "od_norm_g",
            "od_sgu_ln_g", "od_sgu_ln_b", "ffn_norm_g", "final_norm_g"]
    blk_a = jnp.concatenate(
        [jnp.pad(small[n], ((0, 0), (0, D_MODEL - small[n].shape[1]))) for n in wide], axis=0)
    blk_a = jnp.pad(blk_a, ((0, 48 - blk_a.shape[0]), (0, 0)))
    blk_b = jnp.concatenate([small["od_spatial_w"], small["od_spatial_b"]], axis=0)
    sum_a = sum_slots(_group_exchange(blk_a, AXES, True, "ar_small_a"), F32, "ar_small_a_sum")
    sum_b = sum_slots(_group_exchange(blk_b, AXES, True, "ar_small_b"), F32, "ar_small_b_sum")
    full = {}
    off = 0
    for n in wide:
        r_, c_ = small[n].shape
        full[n] = sum_a[off:off + r_, 0:c_]
        off += r_
    full["od_spatial_w"] = sum_b[0:D_GROUPS * BLOCK]
    full["od_spatial_b"] = sum_b[D_GROUPS * BLOCK:D_GROUPS * BLOCK + D_GROUPS]
    full["ev_conv_w"] = lax.dynamic_slice_in_dim(full["ev_conv_w"], me * 64, 64, axis=1)
    full["od_norm_g"] = lax.dynamic_slice_in_dim(full["od_norm_g"], me * 128, 128, axis=1)
    full["od_sgu_ln_g"] = lax.dynamic_slice_in_dim(full["od_sgu_ln_g"], me * 64, 64, axis=1)
    full["od_sgu_ln_b"] = lax.dynamic_slice_in_dim(full["od_sgu_ln_b"], me * 64, 64, axis=1)

    small_names = [n for n in names if n not in BIG]
    view = {n: ((-1, wts[n].shape[-1]) if wts[n].ndim > 1 else (1, -1)) for n in small_names}
    ds, ms, vs = adamw_small([wts[n].reshape(view[n]) for n in small_names],
                             [full[n].reshape(view[n]) for n in small_names],
                             [mom[n].reshape(view[n]) for n in small_names],
                             [vel[n].reshape(view[n]) for n in small_names], "adamw_small")
    for i, n in enumerate(small_names):
        shp = wts[n].shape
        out_g[n], out_d[n], out_m[n], out_v[n] = (full[n].reshape(shp), ds[i].reshape(shp), ms[i].reshape(shp),
                                                  vs[i].reshape(shp))

    return (loss, dx[None], *[out_g[n] for n in names], *[out_d[n] for n in names],
            *[out_m[n] for n in names], *[out_v[n] for n in names])
```

```python
import functools
import math

import jax
import jax.numpy as jnp
from jax import lax
from jax.experimental import pallas as pl
from jax.experimental.pallas import tpu as pltpu

F32 = jnp.float32
ACT = jnp.bfloat16

D_MODEL = 1024
HEAD_DIM = 64
N_HEADS = 8
ATT_W = N_HEADS * HEAD_DIM
A_KV_HEADS = 2
CONV_CH = 512
CONV_WIDTH = 31
CONV_HALO = 32
D_CH = 512
D_GROUPS = 8
BLOCK = 128
D_FF = 2816
ROT_DIM = 16
ROPE_THETA = 500000.0
RMS_EPS = 1e-6
LN_EPS = 1e-5
DILATED = ((128, 1), (512, 4), (2048, 16))
NEG = -1e30
LANES = 128

ADAM_LR = 0.001
ADAM_B1 = 0.9
ADAM_B2 = 0.999
ADAM_EPS = 1e-08
ADAM_WD = 0.01
ADAM_STEP = 10

V7X_VMEM_LIMIT = 56 * 1024 * 1024
SMALL_BLOCK_BYTES = 6 * 1024 * 1024
N_DEV = 8

NN = (((1,), (0,)), ((), ()))
NT = (((1,), (1,)), ((), ()))
TN = (((0,), (0,)), ((), ()))
MESH = pl.DeviceIdType.MESH
AXES = ("x", "y", "c")


def _dot(a, b, dims):
    return lax.dot_general(a, b, dims, preferred_element_type=F32)


def _cp(*sem):
    return pltpu.CompilerParams(dimension_semantics=sem if sem else None,
                                vmem_limit_bytes=V7X_VMEM_LIMIT)


def _tile(n, prefs):
    for p in prefs:
        if n % p == 0:
            return p
    return n


def _sigmoid(x):
    return 1.0 / (1.0 + jnp.exp(-x))


def _rowspec(tm, w, col=0):
    return pl.BlockSpec((tm, w), lambda i, col=col: (i, col))


def _fullspec(shape):
    nd = len(shape)
    return pl.BlockSpec(shape, lambda *a, nd=nd: (0,) * nd)


def _rope_tables(seq):
    half = ROT_DIM // 2
    inv_freq = ROPE_THETA ** (-jnp.arange(half, dtype=F32) * (2.0 / ROT_DIM))
    ang = jnp.arange(seq, dtype=jnp.int32).astype(F32)[:, None] * inv_freq[None, :]
    cos, sin = jnp.cos(ang), jnp.sin(ang)
    lane = jnp.arange(LANES)
    jm = lane % HEAD_DIM
    idx = jm % half
    c = jnp.where(jm[None, :] < ROT_DIM, cos[:, idx], 1.0)
    sa = jnp.where(jm[None, :] < half, -sin[:, idx], 0.0)
    sb = jnp.where((jm[None, :] >= half) & (jm[None, :] < ROT_DIM), sin[:, idx], 0.0)
    return c.astype(F32), sa.astype(F32), sb.astype(F32)


def _rope(x, c, sa, sb):
    return x * c + pltpu.roll(x, LANES - 8, 1) * sa + pltpu.roll(x, 8, 1) * sb


def _rope_t(d, c, sa, sb):
    return d * c + pltpu.roll(d * sa, 8, 1) + pltpu.roll(d * sb, LANES - 8, 1)


def rmsnorm(h, g, name):
    s = h.shape[0]
    tm = _tile(s, (512,))

    def body(h_ref, g_ref, o_ref):
        x = h_ref[...]
        r = lax.rsqrt(jnp.mean(x * x, axis=-1, keepdims=True) + RMS_EPS)
        o_ref[...] = (x * r * g_ref[...]).astype(o_ref.dtype)

    return pl.pallas_call(
        body, grid=(s // tm,), name=name,
        in_specs=[_rowspec(tm, D_MODEL), _fullspec((1, D_MODEL))],
        out_specs=_rowspec(tm, D_MODEL),
        out_shape=jax.ShapeDtypeStruct((s, D_MODEL), ACT),
        compiler_params=_cp("parallel"))(h, g)


def inproj(n, w, tabs, nqk, splits, name):
    s = n.shape[0]
    ntot = w.shape[1]
    assert sum(splits) == ntot and splits[0] == nqk
    tm = _tile(s, (256,))

    def body(n_ref, w_ref, c_ref, sa_ref, sb_ref, *outs):
        res = _dot(n_ref[...], w_ref[...], NN)
        c, sa, sb = c_ref[...], sa_ref[...], sb_ref[...]
        for g in range(nqk // LANES):
            x = res[:, g * LANES:(g + 1) * LANES]
            outs[0][:, g * LANES:(g + 1) * LANES] = _rope(x, c, sa, sb).astype(ACT)
        off = nqk
        for o_ref, wd in zip(outs[1:], splits[1:]):
            o_ref[...] = res[:, off:off + wd].astype(ACT)
            off += wd

    return pl.pallas_call(
        body, grid=(s // tm,), name=name,
        in_specs=[_rowspec(tm, D_MODEL), _fullspec((D_MODEL, ntot))] + [_rowspec(tm, LANES)] * 3,
        out_specs=[_rowspec(tm, wd) for wd in splits],
        out_shape=[jax.ShapeDtypeStruct((s, wd), ACT) for wd in splits],
        compiler_params=_cp("parallel"))(n, w, *tabs)


def ffn_up(n, wg, wu, name):
    s = n.shape[0]
    f = wg.shape[1]
    tm = _tile(s, (512,))
    tf = _tile(f, (1408, 512, 256, 128))

    def body(n_ref, wg_ref, wu_ref, g_ref, u_ref, a_ref):
        a = n_ref[...]
        g = _dot(a, wg_ref[...], NN)
        u = _dot(a, wu_ref[...], NN)
        g_ref[...] = g.astype(ACT)
        u_ref[...] = u.astype(ACT)
        a_ref[...] = (g * _sigmoid(g) * u).astype(ACT)

    wspec = pl.BlockSpec((D_MODEL, tf), lambda j, i: (0, j))
    ospec = pl.BlockSpec((tm, tf), lambda j, i: (i, j))
    return pl.pallas_call(
        body, grid=(f // tf, s // tm), name=name,
        in_specs=[pl.BlockSpec((tm, D_MODEL), lambda j, i: (i, 0)), wspec, wspec],
        out_specs=[ospec] * 3,
        out_shape=[jax.ShapeDtypeStruct((s, f), ACT)] * 3,
        compiler_params=_cp("parallel", "parallel"))(n, wg, wu)


def mm_res(parts, w, h, gnext, name):
    s = h.shape[0]
    tm = _tile(s, (256,))
    widths = [p.shape[1] for p in parts]
    assert sum(widths) == w.shape[0]
    np_ = len(parts)

    def body(*refs):
        p_refs = refs[:np_]
        w_ref, h_ref = refs[np_], refs[np_ + 1]
        rest = refs[np_ + 2:]
        acc = h_ref[...]
        off = 0
        for p_ref, wd in zip(p_refs, widths):
            acc = acc + _dot(p_ref[...], w_ref[off:off + wd, :], NN)
            off += wd
        if gnext is None:
            rest[0][...] = acc
        else:
            g_ref, ho_ref, no_ref = rest
            ho_ref[...] = acc
            r = lax.rsqrt(jnp.mean(acc * acc, axis=-1, keepdims=True) + RMS_EPS)
            no_ref[...] = (acc * r * g_ref[...]).astype(ACT)

    in_specs = [_rowspec(tm, wd) for wd in widths] + [_fullspec(w.shape), _rowspec(tm, D_MODEL)]
    args = list(parts) + [w, h]
    out_specs = [_rowspec(tm, D_MODEL)]
    out_shape = [jax.ShapeDtypeStruct((s, D_MODEL), F32)]
    if gnext is not None:
        in_specs.append(_fullspec((1, D_MODEL)))
        args.append(gnext)
        out_specs.append(_rowspec(tm, D_MODEL))
        out_shape.append(jax.ShapeDtypeStruct((s, D_MODEL), ACT))
    out = pl.pallas_call(
        body, grid=(s // tm,), name=name, in_specs=in_specs, out_specs=out_specs,
        out_shape=out_shape, compiler_params=_cp("parallel"))(*args)
    return (out[0], None) if gnext is None else (out[0], out[1])


def _band_masks(n, max_dist):
    qi = lax.broadcasted_iota(jnp.int32, (BLOCK, BLOCK), 0)
    kj = lax.broadcasted_iota(jnp.int32, (BLOCK, BLOCK), 1)
    m_cur = kj <= qi
    m_prev = (kj > qi) if max_dist == BLOCK - 1 else (kj >= qi)
    m_prev = jnp.logical_and(m_prev, n > 0)
    return m_cur, m_prev


def _fold(a, d):
    return a.reshape(a.shape[0] // d, d * a.shape[1])


def attn_fwd(qk, v, *, d, hkv, max_dist, sink, out_dtype, name):
    s = qk.shape[0]
    kvw = hkv * HEAD_DIM
    wqk = ATT_W + kvw
    assert qk.shape[1] == wqk and (d == 1 or (wqk % ATT_W == 0 and wqk % kvw == 0))
    nb = s // d // BLOCK
    grp = N_HEADS // hkv
    qpb, kpb, koff = wqk // ATT_W, wqk // kvw, ATT_W // kvw

    def body(*refs):
        if sink is None:
            q_ref, kc_ref, kp_ref, vc_ref, vp_ref, o_ref, l_ref = refs
        else:
            q_ref, kc_ref, kp_ref, vc_ref, vp_ref, s_ref, o_ref, l_ref = refs
        n = pl.program_id(1)
        m_cur, m_prev = _band_masks(n, max_dist)
        for h in range(N_HEADS):
            kh = h // grp
            qs = slice(h * HEAD_DIM, (h + 1) * HEAD_DIM)
            ks = slice(kh * HEAD_DIM, (kh + 1) * HEAD_DIM)
            q = q_ref[:, qs]
            sc = jnp.where(m_cur, _dot(q, kc_ref[:, ks], NT) * 0.125, NEG)
            sp = jnp.where(m_prev, _dot(q, kp_ref[:, ks], NT) * 0.125, NEG)
            m = jnp.maximum(jnp.max(sc, axis=-1, keepdims=True), jnp.max(sp, axis=-1, keepdims=True))
            if sink is not None:
                sk = s_ref[h:h + 1, 0:1]
                m = jnp.maximum(m, sk)
            pc = jnp.exp(sc - m)
            pp = jnp.exp(sp - m)
            l = jnp.sum(pc, axis=-1, keepdims=True) + jnp.sum(pp, axis=-1, keepdims=True)
            if sink is not None:
                l = l + jnp.exp(sk - m)
            o = _dot(pc.astype(ACT), vc_ref[:, ks], NN) + _dot(pp.astype(ACT), vp_ref[:, ks], NN)
            o_ref[:, qs] = (o / l).astype(o_ref.dtype)
            l_ref[:, qs] = jnp.broadcast_to(m + jnp.log(l), (BLOCK, HEAD_DIM))

    prev = lambda n: jnp.maximum(n - 1, 0)
    in_specs = [
        pl.BlockSpec((BLOCK, ATT_W), lambda r, n: (n, r * qpb)),
        pl.BlockSpec((BLOCK, kvw), lambda r, n: (n, r * kpb + koff)),
        pl.BlockSpec((BLOCK, kvw), lambda r, n: (prev(n), r * kpb + koff)),
        pl.BlockSpec((BLOCK, kvw), lambda r, n: (n, r)),
        pl.BlockSpec((BLOCK, kvw), lambda r, n: (prev(n), r)),
    ]
    qkf, vf = _fold(qk, d), _fold(v, d)
    args = [qkf, qkf, qkf, vf, vf]
    if sink is not None:
        in_specs.append(_fullspec((N_HEADS, LANES)))
        args.append(sink)
    ospec = pl.BlockSpec((BLOCK, ATT_W), lambda r, n: (n, r))
    o, lse = pl.pallas_call(
        body, grid=(d, nb), name=name, in_specs=in_specs, out_specs=[ospec, ospec],
        out_shape=[jax.ShapeDtypeStruct((s // d, d * ATT_W), out_dtype),
                   jax.ShapeDtypeStruct((s // d, d * ATT_W), F32)],
        compiler_params=_cp("parallel", "parallel"))(*args)
    return o.reshape(s, ATT_W), lse.reshape(s, ATT_W)


def attn_bwd(qk, v, do_src, do_col, o, lse, *, d, hkv, max_dist, sink, acc, name):
    s = qk.shape[0]
    kvw = hkv * HEAD_DIM
    wqk = ATT_W + kvw
    nb = s // d // BLOCK
    grp = N_HEADS // hkv
    qpb, kpb, koff = wqk // ATT_W, wqk // kvw, ATT_W // kvw
    dob = do_src.shape[1] // ATT_W
    has_sink, has_acc = sink is not None, acc is not None

    def body(*refs):
        refs = list(refs)
        q_ref, kc_ref, kp_ref, vc_ref, vp_ref, do_ref, o_ref, l_ref = refs[:8]
        pos = 8
        if has_sink:
            s_ref = refs[pos]
            pos += 1
        if has_acc:
            dqa_ref, dka_ref, dva_ref = refs[pos:pos + 3]
            pos += 3
        dq_ref, dk_ref, dv_ref = refs[pos:pos + 3]
        pos += 3
        if has_sink:
            ds_ref = refs[pos]
            pos += 1
        ck_ref, cv_ref = refs[pos:pos + 2]
        r_id = pl.program_id(0)
        n = pl.program_id(1)

        @pl.when(n == 0)
        def _():
            ck_ref[...] = jnp.zeros_like(ck_ref)
            cv_ref[...] = jnp.zeros_like(cv_ref)

        if has_sink:
            @pl.when(jnp.logical_and(n == 0, r_id == 0))
            def _():
                ds_ref[...] = jnp.zeros_like(ds_ref)

        @pl.when(n < nb)
        def _():
            m_cur, m_prev = _band_masks(n, max_dist)
            cur_k = [None] * hkv
            cur_v = [None] * hkv
            prv_k = [None] * hkv
            prv_v = [None] * hkv
            for h in range(N_HEADS):
                kh = h // grp
                qs = slice(h * HEAD_DIM, (h + 1) * HEAD_DIM)
                ks = slice(kh * HEAD_DIM, (kh + 1) * HEAD_DIM)
                q = q_ref[:, qs]
                kc, kp, vc, vp = kc_ref[:, ks], kp_ref[:, ks], vc_ref[:, ks], vp_ref[:, ks]
                do_h = do_ref[:, qs]
                lse_h = l_ref[:, h * HEAD_DIM:h * HEAD_DIM + 1]
                sc = jnp.where(m_cur, _dot(q, kc, NT) * 0.125, NEG)
                sp = jnp.where(m_prev, _dot(q, kp, NT) * 0.125, NEG)
                pc = jnp.exp(sc - lse_h)
                pp = jnp.exp(sp - lse_h)
                e = jnp.sum(do_h.astype(F32) * o_ref[:, qs].astype(F32), axis=-1, keepdims=True)
                dsc = pc * (_dot(do_h, vc, NT) - e)
                dsp = pp * (_dot(do_h, vp, NT) - e)
                dsc_a, dsp_a = dsc.astype(ACT), dsp.astype(ACT)
                dq = (_dot(dsc_a, kc, NN) + _dot(dsp_a, kp, NN)) * 0.125
                if has_acc:
                    dq = dq + dqa_ref[:, qs]
                dq_ref[:, qs] = dq
                parts = (
                    (cur_k, _dot(dsc_a, q, TN) * 0.125), (prv_k, _dot(dsp_a, q, TN) * 0.125),
                    (cur_v, _dot(pc.astype(ACT), do_h, TN)), (prv_v, _dot(pp.astype(ACT), do_h, TN)))
                for lst, val in parts:
                    lst[kh] = val if lst[kh] is None else lst[kh] + val
                if has_sink:
                    sk = s_ref[h:h + 1, 0:1]
                    dsk = -jnp.sum(jnp.exp(sk - lse_h) * e, axis=0, keepdims=True)
                    ds_ref[h:h + 1, :] = ds_ref[h:h + 1, :] + dsk
            for kh in range(hkv):
                ks = slice(kh * HEAD_DIM, (kh + 1) * HEAD_DIM)
                dk = ck_ref[:, ks] + prv_k[kh]
                dv = cv_ref[:, ks] + prv_v[kh]
                if has_acc:
                    dk = dk + dka_ref[:, ks]
                    dv = dv + dva_ref[:, ks]
                dk_ref[:, ks] = dk
                dv_ref[:, ks] = dv
                ck_ref[:, ks] = cur_k[kh]
                cv_ref[:, ks] = cur_v[kh]

        @pl.when(n == nb)
        def _():
            dk = ck_ref[...]
            dv = cv_ref[...]
            if has_acc:
                dk = dk + dka_ref[...]
                dv = dv + dva_ref[...]
            dk_ref[...] = dk
            dv_ref[...] = dv

    qrow = lambda n: jnp.minimum(n, nb - 1)
    prow = lambda n: jnp.maximum(jnp.minimum(n, nb - 1) - 1, 0)
    krow = lambda n: jnp.maximum(n - 1, 0)
    in_specs = [
        pl.BlockSpec((BLOCK, ATT_W), lambda r, n: (qrow(n), r * qpb)),
        pl.BlockSpec((BLOCK, kvw), lambda r, n: (qrow(n), r * kpb + koff)),
        pl.BlockSpec((BLOCK, kvw), lambda r, n: (prow(n), r * kpb + koff)),
        pl.BlockSpec((BLOCK, kvw), lambda r, n: (qrow(n), r)),
        pl.BlockSpec((BLOCK, kvw), lambda r, n: (prow(n), r)),
        pl.BlockSpec((BLOCK, ATT_W), lambda r, n: (qrow(n), r * dob + do_col)),
        pl.BlockSpec((BLOCK, ATT_W), lambda r, n: (qrow(n), r)),
        pl.BlockSpec((BLOCK, ATT_W), lambda r, n: (qrow(n), r)),
    ]
    qkf, vf = _fold(qk, d), _fold(v, d)
    args = [qkf, qkf, qkf, vf, vf, _fold(do_src, d), _fold(o, d), _fold(lse, d)]
    if has_sink:
        in_specs.append(_fullspec((N_HEADS, LANES)))
        args.append(sink)
    qspec = pl.BlockSpec((BLOCK, ATT_W), lambda r, n: (qrow(n), r))
    kspec = pl.BlockSpec((BLOCK, kvw), lambda r, n: (krow(n), r))
    if has_acc:
        in_specs += [qspec, kspec, kspec]
        args += [_fold(a, d) for a in acc]
    out_specs = [qspec, kspec, kspec]
    out_shape = [jax.ShapeDtypeStruct((s // d, d * ATT_W), F32),
                 jax.ShapeDtypeStruct((s // d, d * kvw), F32),
                 jax.ShapeDtypeStruct((s // d, d * kvw), F32)]
    if has_sink:
        out_specs.append(_fullspec((N_HEADS, LANES)))
        out_shape.append(jax.ShapeDtypeStruct((N_HEADS, LANES), F32))
    out = pl.pallas_call(
        body, grid=(d, nb + 1), name=name, in_specs=in_specs, out_specs=out_specs,
        out_shape=out_shape,
        scratch_shapes=[pltpu.VMEM((BLOCK, kvw), F32), pltpu.VMEM((BLOCK, kvw), F32)],
        compiler_params=_cp("arbitrary", "arbitrary"))(*args)
    res = [out[0].reshape(s, ATT_W), out[1].reshape(s, kvw), out[2].reshape(s, kvw)]
    if has_sink:
        res.append(out[3])
    return res


def combine_fwd(os_, lses, name):
    s = os_[0].shape[0]
    tm = _tile(s, (512,))

    def body(o1, o2, o3, l1, l2, l3, c_ref, l_ref):
        a, b, c = l1[...], l2[...], l3[...]
        m = jnp.maximum(jnp.maximum(a, b), c)
        wa, wb, wc = jnp.exp(a - m), jnp.exp(b - m), jnp.exp(c - m)
        tot = wa + wb + wc
        c_ref[...] = ((wa * o1[...] + wb * o2[...] + wc * o3[...]) / tot).astype(ACT)
        l_ref[...] = m + jnp.log(tot)

    spec = _rowspec(tm, ATT_W)
    return pl.pallas_call(
        body, grid=(s // tm,), name=name, in_specs=[spec] * 6, out_specs=[spec, spec],
        out_shape=[jax.ShapeDtypeStruct((s, ATT_W), ACT), jax.ShapeDtypeStruct((s, ATT_W), F32)],
        compiler_params=_cp("parallel"))(*os_, *lses)


def assemble(parts, tabs, name):
    s = parts[0][0].shape[0]
    tm = _tile(s, (256,))
    widths = [p.shape[1] for p, _ in parts]
    flags = [f for _, f in parts]
    np_ = len(parts)

    def body(*refs):
        p_refs = refs[:np_]
        c_ref, sa_ref, sb_ref, o_ref = refs[np_:]
        c, sa, sb = c_ref[...], sa_ref[...], sb_ref[...]
        off = 0
        for p_ref, wd, fl in zip(p_refs, widths, flags):
            if fl:
                for g in range(wd // LANES):
                    x = p_ref[:, g * LANES:(g + 1) * LANES].astype(F32)
                    o_ref[:, off + g * LANES:off + (g + 1) * LANES] = _rope_t(x, c, sa, sb).astype(ACT)
            else:
                o_ref[:, off:off + wd] = p_ref[...].astype(ACT)
            off += wd

    tot = sum(widths)
    return pl.pallas_call(
        body, grid=(s // tm,), name=name,
        in_specs=[_rowspec(tm, wd) for wd in widths] + [_rowspec(tm, LANES)] * 3,
        out_specs=_rowspec(tm, tot), out_shape=jax.ShapeDtypeStruct((s, tot), ACT),
        compiler_params=_cp("parallel"))(*[p for p, _ in parts], *tabs)


def _ln_stats(x):
    mu = jnp.mean(x, axis=-1, keepdims=True)
    xc = x - mu
    var = jnp.mean(xc * xc, axis=-1, keepdims=True)
    rstd = lax.rsqrt(var + LN_EPS)
    return xc * rstd, rstd


def conv_fwd(ga, gb, cw, cb, lg, lb, name):
    s = ga.shape[0]
    tm = _tile(s, (256,))
    hb = tm // CONV_HALO

    def body(ga_ref, gb_ref, gah_ref, gbh_ref, cw_ref, cb_ref, lg_ref, lb_ref, c_ref, c1_ref, buf):
        i = pl.program_id(0)
        halo = gah_ref[...].astype(F32) * _sigmoid(gbh_ref[...].astype(F32))
        buf[0:CONV_HALO, :] = jnp.where(i > 0, halo, 0.0)
        buf[CONV_HALO:, :] = ga_ref[...].astype(F32) * _sigmoid(gb_ref[...].astype(F32))
        acc = jnp.broadcast_to(cb_ref[...], (tm, CONV_CH))
        for j in range(CONV_WIDTH):
            acc = acc + cw_ref[j:j + 1, :] * buf[pl.ds(CONV_HALO - (CONV_WIDTH - 1) + j, tm), :]
        c1_ref[...] = acc.astype(c1_ref.dtype)
        xh, _ = _ln_stats(acc)
        y = xh * lg_ref[...] + lb_ref[...]
        c_ref[...] = (y * _sigmoid(y)).astype(ACT)

    hspec = pl.BlockSpec((CONV_HALO, CONV_CH), lambda i: (jnp.maximum(i * hb - 1, 0), 0))
    vec = _fullspec((1, CONV_CH))
    spec = _rowspec(tm, CONV_CH)
    return pl.pallas_call(
        body, grid=(s // tm,), name=name,
        in_specs=[spec, spec, hspec, hspec, _fullspec((CONV_WIDTH, CONV_CH)), vec, vec, vec],
        out_specs=[spec, spec],
        out_shape=[jax.ShapeDtypeStruct((s, CONV_CH), ACT), jax.ShapeDtypeStruct((s, CONV_CH), F32)],
        scratch_shapes=[pltpu.VMEM((tm + CONV_HALO, CONV_CH), F32)],
        compiler_params=_cp("parallel"))(ga, gb, ga, gb, cw, cb, lg, lb)


def conv_bwd_ln(c1, dsrc, dcol, lg, lb, name):
    s = c1.shape[0]
    tm = _tile(s, (256,))

    def body(c1_ref, d_ref, lg_ref, lb_ref, o_ref, dg_ref, db_ref):
        @pl.when(pl.program_id(0) == 0)
        def _():
            dg_ref[...] = jnp.zeros_like(dg_ref)
            db_ref[...] = jnp.zeros_like(db_ref)

        xh, rstd = _ln_stats(c1_ref[...].astype(F32))
        y = xh * lg_ref[...] + lb_ref[...]
        sg = _sigmoid(y)
        dy = d_ref[...].astype(F32) * (sg * (1.0 + y * (1.0 - sg)))
        dg_ref[0:1, :] = dg_ref[0:1, :] + jnp.sum(dy * xh, axis=0, keepdims=True)
        db_ref[0:1, :] = db_ref[0:1, :] + jnp.sum(dy, axis=0, keepdims=True)
        dxh = dy * lg_ref[...]
        o_ref[...] = rstd * (dxh - jnp.mean(dxh, axis=-1, keepdims=True)
                             - xh * jnp.mean(dxh * xh, axis=-1, keepdims=True))

    vec = _fullspec((1, CONV_CH))
    acc = _fullspec((8, CONV_CH))
    return pl.pallas_call(
        body, grid=(s // tm,), name=name,
        in_specs=[_rowspec(tm, CONV_CH), _rowspec(tm, CONV_CH, dcol), vec, vec],
        out_specs=[_rowspec(tm, CONV_CH), acc, acc],
        out_shape=[jax.ShapeDtypeStruct((s, CONV_CH), F32)] + [jax.ShapeDtypeStruct((8, CONV_CH), F32)] * 2,
        compiler_params=_cp("arbitrary"))(c1, dsrc, lg, lb)


def conv_bwd_conv(dc1, ga, gb, cw, name):
    s = ga.shape[0]
    tm = _tile(s, (256,))
    hb = tm // CONV_HALO
    nt = s // tm
    last_h = s // CONV_HALO - 1

    def body(d_ref, dn_ref, ga_ref, gb_ref, gah_ref, gbh_ref, cw_ref,
             dga_ref, dgb_ref, dw_ref, db_ref, dbuf, cbuf):
        i = pl.program_id(0)

        @pl.when(i == 0)
        def _():
            dw_ref[...] = jnp.zeros_like(dw_ref)
            db_ref[...] = jnp.zeros_like(db_ref)

        d = d_ref[...]
        dbuf[0:tm, :] = d
        dbuf[tm:, :] = jnp.where(i < nt - 1, dn_ref[...], 0.0)
        halo = gah_ref[...].astype(F32) * _sigmoid(gbh_ref[...].astype(F32))
        cbuf[0:CONV_HALO, :] = jnp.where(i > 0, halo, 0.0)
        a = ga_ref[...].astype(F32)
        sg = _sigmoid(gb_ref[...].astype(F32))
        cbuf[CONV_HALO:, :] = a * sg
        dc0 = jnp.zeros((tm, CONV_CH), F32)
        for j in range(CONV_WIDTH):
            dc0 = dc0 + cw_ref[j:j + 1, :] * dbuf[pl.ds(CONV_WIDTH - 1 - j, tm), :]
            tap = cbuf[pl.ds(CONV_HALO - (CONV_WIDTH - 1) + j, tm), :]
            dw_ref[j:j + 1, :] = dw_ref[j:j + 1, :] + jnp.sum(d * tap, axis=0, keepdims=True)
        db_ref[0:1, :] = db_ref[0:1, :] + jnp.sum(d, axis=0, keepdims=True)
        dga_ref[...] = (dc0 * sg).astype(ACT)
        dgb_ref[...] = (dc0 * a * sg * (1.0 - sg)).astype(ACT)

    spec = _rowspec(tm, CONV_CH)
    hprev = pl.BlockSpec((CONV_HALO, CONV_CH), lambda i: (jnp.maximum(i * hb - 1, 0), 0))
    hnext = pl.BlockSpec((CONV_HALO, CONV_CH), lambda i: (jnp.minimum((i + 1) * hb, last_h), 0))
    return pl.pallas_call(
        body, grid=(nt,), name=name,
        in_specs=[spec, hnext, spec, spec, hprev, hprev, _fullspec((CONV_WIDTH, CONV_CH))],
        out_specs=[spec, spec, _fullspec((CONV_HALO, CONV_CH)), _fullspec((8, CONV_CH))],
        out_shape=[jax.ShapeDtypeStruct((s, CONV_CH), ACT)] * 2
        + [jax.ShapeDtypeStruct((CONV_HALO, CONV_CH), F32), jax.ShapeDtypeStruct((8, CONV_CH), F32)],
        scratch_shapes=[pltpu.VMEM((tm + CONV_HALO, CONV_CH), F32)] * 2,
        compiler_params=_cp("arbitrary"))(dc1, dc1, ga, gb, ga, gb, cw)


_GELU_K = math.sqrt(2.0 / math.pi)
_GELU_C = 0.044715


def _gelu(x):
    return 0.5 * x * (1.0 + jnp.tanh(_GELU_K * (x + _GELU_C * x * x * x)))


def _gelu_grad(x):
    t = jnp.tanh(_GELU_K * (x + _GELU_C * x * x * x))
    return 0.5 * (1.0 + t) + 0.5 * x * (1.0 - t * t) * _GELU_K * (1.0 + 3.0 * _GELU_C * x * x)


def _tril():
    qi = lax.broadcasted_iota(jnp.int32, (BLOCK, BLOCK), 0)
    kj = lax.broadcasted_iota(jnp.int32, (BLOCK, BLOCK), 1)
    return kj <= qi


def gmlp_fwd(z, lg, lb, sw, sbt, name):
    s = z.shape[0]

    def body(z_ref, lg_ref, lb_ref, sw_ref, sb_ref, o_ref):
        zz = _gelu(z_ref[...].astype(F32))
        u = zz[:, :D_CH]
        xh, _ = _ln_stats(zz[:, D_CH:])
        gn = (xh * lg_ref[...] + lb_ref[...]).astype(ACT)
        tril = _tril()
        for g in range(D_GROUPS):
            cs = slice(g * HEAD_DIM, (g + 1) * HEAD_DIM)
            w = jnp.where(tril, sw_ref[g], 0.0).astype(ACT)
            mixed = _dot(w, gn[:, cs], NN) + sb_ref[:, g:g + 1]
            o_ref[:, cs] = (u[:, cs] * mixed).astype(ACT)

    return pl.pallas_call(
        body, grid=(s // BLOCK,), name=name,
        in_specs=[_rowspec(BLOCK, 2 * D_CH), _fullspec((1, D_CH)), _fullspec((1, D_CH)),
                  _fullspec((D_GROUPS, BLOCK, BLOCK)), _fullspec((BLOCK, D_GROUPS))],
        out_specs=_rowspec(BLOCK, D_CH), out_shape=jax.ShapeDtypeStruct((s, D_CH), ACT),
        compiler_params=_cp("parallel"))(z, lg, lb, sw, sbt)


def gmlp_bwd(z, dsrc, dcol, lg, lb, sw, sbt, name):
    s = z.shape[0]

    def body(z_ref, d_ref, lg_ref, lb_ref, sw_ref, sb_ref, dz_ref, dw_ref, dsb_ref, dg_ref, db_ref, dgn_buf):
        @pl.when(pl.program_id(0) == 0)
        def _():
            dw_ref[...] = jnp.zeros_like(dw_ref)
            dsb_ref[...] = jnp.zeros_like(dsb_ref)
            dg_ref[...] = jnp.zeros_like(dg_ref)
            db_ref[...] = jnp.zeros_like(db_ref)

        zr = z_ref[...].astype(F32)
        zz = _gelu(zr)
        u = zz[:, :D_CH]
        xh, rstd = _ln_stats(zz[:, D_CH:])
        gn = (xh * lg_ref[...] + lb_ref[...]).astype(ACT)
        dd = d_ref[...].astype(F32)
        tril = _tril()
        for g in range(D_GROUPS):
            cs = slice(g * HEAD_DIM, (g + 1) * HEAD_DIM)
            w = jnp.where(tril, sw_ref[g], 0.0).astype(ACT)
            gn_g = gn[:, cs]
            mixed = _dot(w, gn_g, NN) + sb_ref[:, g:g + 1]
            du = dd[:, cs] * mixed
            dmix = dd[:, cs] * u[:, cs]
            dmix_a = dmix.astype(ACT)
            dz_ref[:, cs] = (du * _gelu_grad(zr[:, cs])).astype(ACT)
            dw_ref[g] = dw_ref[g] + jnp.where(tril, _dot(dmix_a, gn_g, NT), 0.0)
            dsb_ref[:, g:g + 1] = dsb_ref[:, g:g + 1] + jnp.sum(dmix, axis=-1, keepdims=True)
            dgn_buf[:, cs] = _dot(w, dmix_a, TN)
        dgn = dgn_buf[...]
        dg_ref[0:1, :] = dg_ref[0:1, :] + jnp.sum(dgn * xh, axis=0, keepdims=True)
        db_ref[0:1, :] = db_ref[0:1, :] + jnp.sum(dgn, axis=0, keepdims=True)
        dxh = dgn * lg_ref[...]
        dgate = rstd * (dxh - jnp.mean(dxh, axis=-1, keepdims=True)
                        - xh * jnp.mean(dxh * xh, axis=-1, keepdims=True))
        dz_ref[:, D_CH:] = (dgate * _gelu_grad(zr[:, D_CH:])).astype(ACT)

    vec = _fullspec((1, D_CH))
    acc = _fullspec((8, D_CH))
    return pl.pallas_call(
        body, grid=(s // BLOCK,), name=name,
        in_specs=[_rowspec(BLOCK, 2 * D_CH), _rowspec(BLOCK, D_CH, dcol), vec, vec,
                  _fullspec((D_GROUPS, BLOCK, BLOCK)), _fullspec((BLOCK, D_GROUPS))],
        out_specs=[_rowspec(BLOCK, 2 * D_CH), _fullspec((D_GROUPS, BLOCK, BLOCK)),
                   _fullspec((BLOCK, D_GROUPS)), acc, acc],
        out_shape=[jax.ShapeDtypeStruct((s, 2 * D_CH), ACT),
                   jax.ShapeDtypeStruct((D_GROUPS, BLOCK, BLOCK), F32),
                   jax.ShapeDtypeStruct((BLOCK, D_GROUPS), F32),
                   jax.ShapeDtypeStruct((8, D_CH), F32), jax.ShapeDtypeStruct((8, D_CH), F32)],
        scratch_shapes=[pltpu.VMEM((BLOCK, D_CH), F32)],
        compiler_params=_cp("arbitrary"))(z, dsrc, lg, lb, sw, sbt)


def _rms_bwd(dn, x, g):
    r = lax.rsqrt(jnp.mean(x * x, axis=-1, keepdims=True) + RMS_EPS)
    u = dn * g
    dx = r * u - x * (r * r * r) * jnp.mean(x * u, axis=-1, keepdims=True)
    return dx, dn * x * r


def final_loss(h, g, target, name):
    s = h.shape[0]
    tm = _tile(s, (256,))

    def body(h_ref, g_ref, t_ref, loss_ref, dh_ref, dg_ref):
        @pl.when(pl.program_id(0) == 0)
        def _():
            loss_ref[...] = jnp.zeros_like(loss_ref)
            dg_ref[...] = jnp.zeros_like(dg_ref)

        x = h_ref[...]
        r = lax.rsqrt(jnp.mean(x * x, axis=-1, keepdims=True) + RMS_EPS)
        diff = x * r * g_ref[...] - t_ref[...]
        part = jnp.sum(jnp.sum(diff * diff, axis=-1, keepdims=True), axis=0, keepdims=True)
        loss_ref[...] = loss_ref[...] + part * (0.5 / D_MODEL)
        dx, dgt = _rms_bwd(diff * (1.0 / D_MODEL), x, g_ref[...])
        dh_ref[...] = dx
        dg_ref[0:1, :] = dg_ref[0:1, :] + jnp.sum(dgt, axis=0, keepdims=True)

    spec = _rowspec(tm, D_MODEL)
    return pl.pallas_call(
        body, grid=(s // tm,), name=name,
        in_specs=[spec, _fullspec((1, D_MODEL)), spec],
        out_specs=[_fullspec((8, LANES)), spec, _fullspec((8, D_MODEL))],
        out_shape=[jax.ShapeDtypeStruct((8, LANES), F32), jax.ShapeDtypeStruct((s, D_MODEL), F32),
                   jax.ShapeDtypeStruct((8, D_MODEL), F32)],
        compiler_params=_cp("arbitrary"))(h, g, target)


def mm_nt(dy, w, name):
    s, n = dy.shape
    k = w.shape[0]
    tm = _tile(s, (512,))
    tk = _tile(k, (512,))

    def body(d_ref, w_ref, o_ref):
        o_ref[...] = _dot(d_ref[...].astype(ACT), w_ref[...], NT).astype(ACT)

    return pl.pallas_call(
        body, grid=(k // tk, s // tm), name=name,
        in_specs=[pl.BlockSpec((tm, n), lambda j, i: (i, 0)), pl.BlockSpec((tk, n), lambda j, i: (j, 0))],
        out_specs=pl.BlockSpec((tm, tk), lambda j, i: (i, j)),
        out_shape=jax.ShapeDtypeStruct((s, k), ACT),
        compiler_params=_cp("parallel", "parallel"))(dy, w)


def ffn_down_bwd(dh, wd, gate, up, name):
    s = dh.shape[0]
    f = wd.shape[0]
    tm = _tile(s, (512,))
    tf = _tile(f, (1408, 512, 256, 128))

    def body(d_ref, w_ref, g_ref, u_ref, dg_ref, du_ref):
        dact = _dot(d_ref[...].astype(ACT), w_ref[...], NT)
        g = g_ref[...].astype(F32)
        sg = _sigmoid(g)
        dg_ref[...] = (dact * u_ref[...].astype(F32) * (sg * (1.0 + g * (1.0 - sg)))).astype(ACT)
        du_ref[...] = (dact * g * sg).astype(ACT)

    tile = pl.BlockSpec((tm, tf), lambda j, i: (i, j))
    return pl.pallas_call(
        body, grid=(f // tf, s // tm), name=name,
        in_specs=[pl.BlockSpec((tm, D_MODEL), lambda j, i: (i, 0)),
                  pl.BlockSpec((tf, D_MODEL), lambda j, i: (j, 0)), tile, tile],
        out_specs=[tile, tile], out_shape=[jax.ShapeDtypeStruct((s, f), ACT)] * 2,
        compiler_params=_cp("parallel", "parallel"))(dh, wd, gate, up)


def mm_nt_rms(parts, h, g, dh, name):
    s = h.shape[0]
    tm = _tile(s, (256,))
    np_ = len(parts)

    def body(*refs):
        d_refs = refs[:np_]
        w_refs = refs[np_:2 * np_]
        h_ref, g_ref, dh_ref, o_ref, dg_ref = refs[2 * np_:]

        @pl.when(pl.program_id(0) == 0)
        def _():
            dg_ref[...] = jnp.zeros_like(dg_ref)

        dn = None
        for d_ref, w_ref in zip(d_refs, w_refs):
            t = _dot(d_ref[...], w_ref[...], NT)
            dn = t if dn is None else dn + t
        dx, dgt = _rms_bwd(dn, h_ref[...], g_ref[...])
        o_ref[...] = dh_ref[...] + dx
        dg_ref[0:1, :] = dg_ref[0:1, :] + jnp.sum(dgt, axis=0, keepdims=True)

    spec = _rowspec(tm, D_MODEL)
    return pl.pallas_call(
        body, grid=(s // tm,), name=name,
        in_specs=[_rowspec(tm, d.shape[1]) for d, _ in parts] + [_fullspec(w.shape) for _, w in parts]
        + [spec, _fullspec((1, D_MODEL)), spec],
        out_specs=[spec, _fullspec((8, D_MODEL))],
        out_shape=[jax.ShapeDtypeStruct((s, D_MODEL), F32), jax.ShapeDtypeStruct((8, D_MODEL), F32)],
        compiler_params=_cp("arbitrary"))(*[d for d, _ in parts], *[w for _, w in parts], h, g, dh)


def mm_tn(a, b, name):
    s, k = a.shape
    n = b.shape[1]
    tk = _tile(k, (512, 1408, 256, 128))
    tn = _tile(n, (1408, 1280, 1024, 896, 512, 256, 128))
    ts = _tile(s, (512,))

    def body(a_ref, b_ref, o_ref):
        t = _dot(a_ref[...].astype(ACT), b_ref[...].astype(ACT), TN)

        @pl.when(pl.program_id(2) == 0)
        def _():
            o_ref[...] = t

        @pl.when(pl.program_id(2) > 0)
        def _():
            o_ref[...] = o_ref[...] + t

    return pl.pallas_call(
        body, grid=(k // tk, n // tn, s // ts), name=name,
        in_specs=[pl.BlockSpec((ts, tk), lambda i, j, t: (t, i)), pl.BlockSpec((ts, tn), lambda i, j, t: (t, j))],
        out_specs=pl.BlockSpec((tk, tn), lambda i, j, t: (i, j)),
        out_shape=jax.ShapeDtypeStruct((k, n), F32),
        compiler_params=_cp("parallel", "parallel", "arbitrary"))(a, b)


def _adamw_math(w, g, m, v):
    m = ADAM_B1 * m + (1.0 - ADAM_B1) * g
    v = ADAM_B2 * v + (1.0 - ADAM_B2) * (g * g)
    m_hat = m / (1.0 - ADAM_B1 ** ADAM_STEP)
    v_hat = v / (1.0 - ADAM_B2 ** ADAM_STEP)
    delta = -ADAM_LR * (m_hat / (jnp.sqrt(v_hat) + ADAM_EPS) + ADAM_WD * w)
    return delta, m, v


def sum_adamw(parts, w, m, v, name):
    r, c = w.shape
    tr = _tile(r, (256, 128, 64, 32, 16, 8))

    def body(p_ref, w_ref, m_ref, v_ref, g_ref, d_ref, mo_ref, vo_ref):
        g = p_ref[0].astype(F32)
        for i in range(1, N_DEV):
            g = g + p_ref[i].astype(F32)
        d, mm, vv = _adamw_math(w_ref[...], g, m_ref[...], v_ref[...])
        g_ref[...] = g
        d_ref[...] = d
        mo_ref[...] = mm
        vo_ref[...] = vv

    spec = _rowspec(tr, c)
    return pl.pallas_call(
        body, grid=(r // tr,), name=name,
        in_specs=[pl.BlockSpec((N_DEV, tr, c), lambda i: (0, i, 0))] + [spec] * 3, out_specs=[spec] * 4,
        out_shape=[jax.ShapeDtypeStruct((r, c), F32)] * 4,
        compiler_params=_cp("parallel"))(parts, w, m, v)


def adamw_small(ws, gs, ms, vs, name):
    k = len(ws)

    def body(*refs):
        for i in range(k):
            w_ref, g_ref, m_ref, v_ref = (refs[j * k + i] for j in range(4))
            d, mm, vv = _adamw_math(w_ref[...], g_ref[...], m_ref[...], v_ref[...])
            refs[4 * k + i][...] = d
            refs[5 * k + i][...] = mm
            refs[6 * k + i][...] = vv

    shapes = [jax.ShapeDtypeStruct(w.shape, F32) for w in ws]
    specs = [_fullspec(w.shape) for w in ws]
    out = pl.pallas_call(
        body, grid=(1,), name=name, in_specs=specs * 4, out_specs=specs * 3, out_shape=shapes * 3,
        compiler_params=_cp("arbitrary"))(*ws, *gs, *ms, *vs)
    return out[:k], out[k:2 * k], out[2 * k:]


def sum_slots(x, out_dtype, name):
    g, r, c = x.shape
    tr = r if x.size * x.dtype.itemsize <= SMALL_BLOCK_BYTES else _tile(r, (256, 128, 64, 32, 16, 8))

    def body(x_ref, o_ref):
        acc = x_ref[0].astype(F32)
        for i in range(1, g):
            acc = acc + x_ref[i].astype(F32)
        o_ref[...] = acc.astype(o_ref.dtype)

    return pl.pallas_call(
        body, grid=(r // tr,), name=name,
        in_specs=[pl.BlockSpec((g, tr, c), lambda i: (0, i, 0))], out_specs=_rowspec(tr, c),
        out_shape=jax.ShapeDtypeStruct((r, c), out_dtype),
        compiler_params=_cp("parallel"))(x)


def _group_exchange(x, axes, gather, name):
    gsz = 2 ** len(axes)
    out_shape = (gsz,) + x.shape if gather else x.shape
    if not gather:
        assert x.shape[0] == gsz

    def body(x_ref, o_ref, send_sems, recv_sems, local_sem):
        pos = {a: lax.axis_index(a) for a in AXES}
        me = 0
        for a in axes:
            me = me * 2 + pos[a]
        own = pltpu.make_async_copy(x_ref if gather else x_ref.at[me], o_ref.at[me], local_sem)
        own.start()
        copies = []
        for mask in range(1, gsz):
            peer = dict(pos)
            for idx, a in enumerate(axes):
                if (mask >> (len(axes) - 1 - idx)) & 1:
                    peer[a] = 1 - pos[a]
            prank = 0
            for a in axes:
                prank = prank * 2 + peer[a]
            copies.append(pltpu.make_async_remote_copy(
                src_ref=x_ref if gather else x_ref.at[prank], dst_ref=o_ref.at[me],
                send_sem=send_sems.at[mask - 1], recv_sem=recv_sems.at[mask - 1],
                device_id=tuple(peer[a] for a in AXES), device_id_type=MESH))
        for cp in copies:
            cp.start()
        for cp in copies:
            cp.wait()
        own.wait()

    hbm = pl.BlockSpec(memory_space=pltpu.HBM)
    return pl.pallas_call(
        body, name=name, in_specs=[hbm], out_specs=hbm,
        out_shape=jax.ShapeDtypeStruct(out_shape, x.dtype),
        scratch_shapes=[pltpu.SemaphoreType.DMA((gsz - 1,)), pltpu.SemaphoreType.DMA((gsz - 1,)),
                        pltpu.SemaphoreType.DMA(())],
        )(x)


HBM_SPEC = pl.BlockSpec(memory_space=pltpu.HBM)
SEM_SPEC = pl.BlockSpec(memory_space=pltpu.SEMAPHORE)
DATAFLOW = pltpu.SideEffectType.DATAFLOW_SIDE_EFFECTING


def _my_rank():
    return 4 * lax.axis_index("x") + 2 * lax.axis_index("y") + lax.axis_index("c")


def _exchange_copies(x_refs, land_refs, send, recv, a2a):
    pos = [lax.axis_index(a) for a in AXES]
    me = _my_rank()
    copies = []
    for x_ref, land_ref, s_ref, r_ref in zip(x_refs, land_refs, send, recv):
        for k in range(N_DEV - 1):
            bits = ((k + 1) >> 2 & 1, (k + 1) >> 1 & 1, (k + 1) & 1)
            peer = tuple(1 - p if b else p for p, b in zip(pos, bits))
            prank = 4 * peer[0] + 2 * peer[1] + peer[2]
            copies.append(pltpu.make_async_remote_copy(
                src_ref=x_ref.at[prank] if a2a else x_ref, dst_ref=land_ref.at[me],
                send_sem=s_ref.at[k], recv_sem=r_ref.at[k], device_id=peer, device_id_type=MESH))
    return copies


def exchange_start(xs, a2a, name):
    n = len(xs)
    me = _my_rank()
    lands = []
    for x in xs:
        own = lax.dynamic_index_in_dim(x, me, 0, keepdims=True) if a2a else x[None]
        shape = x.shape if a2a else (N_DEV,) + x.shape
        lands.append(lax.dynamic_update_slice(lax.empty(shape, x.dtype), own, (me,) + (0,) * (len(shape) - 1)))

    def body(*refs):
        x_refs, land_refs = refs[:n], refs[n:2 * n]
        outs = refs[2 * n:]
        for cp in _exchange_copies(x_refs, land_refs, outs[:n], outs[n:2 * n], a2a):
            cp.start()
        token = outs[4 * n]
        token[...] = jnp.zeros_like(token)

    sems = [pltpu.SemaphoreType.DMA((N_DEV - 1,))] * n
    out = pl.pallas_call(
        body, name=name,
        out_shape=tuple(sems + sems + [pltpu.HBM(x.shape, x.dtype) for x in xs]
                        + [pltpu.HBM(l.shape, l.dtype) for l in lands] + [jax.ShapeDtypeStruct((8, LANES), F32)]),
        in_specs=[HBM_SPEC] * (2 * n),
        out_specs=tuple([SEM_SPEC] * (2 * n) + [HBM_SPEC] * (2 * n) + [pl.BlockSpec(memory_space=pltpu.VMEM)]),
        input_output_aliases={i: 2 * n + i for i in range(2 * n)},
        compiler_params=pltpu.CompilerParams(has_side_effects=DATAFLOW),
    )(*[pltpu.with_memory_space_constraint(a, pltpu.HBM) for a in list(xs) + lands])
    return (out[:n], out[n:2 * n], out[2 * n:3 * n], out[3 * n:4 * n]), out[4 * n]


def exchange_wait(handles, after, a2a, name):
    send, recv, x_thru, land_thru = handles
    n = len(x_thru)

    def body(*refs):
        x_refs, land_refs = refs[:n], refs[n:2 * n]
        s_refs, r_refs = refs[2 * n:3 * n], refs[3 * n:4 * n]
        for cp in _exchange_copies(x_refs, land_refs, s_refs, r_refs, a2a):
            cp.wait_send()
            cp.wait_recv()

    out = pl.pallas_call(
        body, name=name,
        out_shape=tuple([pltpu.HBM(a.shape, a.dtype) for a in list(x_thru) + list(land_thru)]),
        in_specs=[HBM_SPEC] * (2 * n) + [SEM_SPEC] * (2 * n) + [pl.BlockSpec(memory_space=pl.ANY)],
        out_specs=tuple([HBM_SPEC] * (2 * n)),
        input_output_aliases={i: i for i in range(2 * n)},
        compiler_params=pltpu.CompilerParams(has_side_effects=DATAFLOW),
    )(*x_thru, *land_thru, *send, *recv, after)
    return out[n:2 * n]


def _local_step(x, target, weight, emit, P):
    s = x.shape[0]
    tabs = _rope_tables(s)
    sinkb = jnp.broadcast_to(P["ev_sinks"].reshape(N_HEADS, 1), (N_HEADS, LANES))
    sbt = P["od_spatial_b"].reshape(D_GROUPS, BLOCK).T
    sw = P["od_spatial_w"].reshape(D_GROUPS, BLOCK, BLOCK)
    fg = P["ffn_norm_g"]
    latest = [None]

    def out(name, layer, grad):
        tok = emit(name, layer, grad)
        if tok is not None:
            latest[0] = tok

    def dep(a):
        return a if latest[0] is None else a + latest[0][0:1, 0:1]

    n0 = rmsnorm(x, P["ev_norm_g"], "rms_in")
    qk_e, v_e, ga, gb = inproj(n0, weight("ev_w_in", 0, n0), tabs, ATT_W + 128,
                               (ATT_W + 128, 128, CONV_CH, CONV_CH), "ev_inproj")
    a_e, lse_e = attn_fwd(qk_e, v_e, d=1, hkv=A_KV_HEADS, max_dist=BLOCK - 1, sink=sinkb, out_dtype=ACT,
                          name="ev_attn")
    c_act, c1 = conv_fwd(ga, gb, P["ev_conv_w"], P["ev_conv_b"], P["ev_conv_ln_g"], P["ev_conv_ln_b"], "ev_conv")
    h1, n1 = mm_res([a_e, c_act], weight("ev_w_out", 0, c_act), x, fg[0:1], "ev_outproj")
    gate0, up0, act0 = ffn_up(n1, weight("ffn_w_gate", 0, n1), weight("ffn_w_up", 0, n1), "ffn0_up")
    h2, n2 = mm_res([act0], weight("ffn_w_down", 0, act0), h1, P["od_norm_g"], "ffn0_down")
    qk_o, v_o, z = inproj(n2, weight("od_w_in", 0, n2), tabs, 2 * ATT_W, (2 * ATT_W, ATT_W, 2 * D_CH), "od_inproj")
    outs, lses = [], []
    for window, dil in DILATED:
        assert window // dil == BLOCK
        o_r, l_r = attn_fwd(qk_o, v_o, d=dil, hkv=N_HEADS, max_dist=BLOCK, sink=None, out_dtype=F32,
                            name="od_attn_d%d" % dil)
        outs.append(o_r)
        lses.append(l_r)
    c_out, lse_o = combine_fwd(outs, lses, "od_combine")
    d_out = gmlp_fwd(z, P["od_sgu_ln_g"], P["od_sgu_ln_b"], sw, sbt, "od_gmlp")
    h3, n3 = mm_res([c_out, d_out], weight("od_w_out", 0, d_out), h2, fg[1:2], "od_outproj")
    gate1, up1, act1 = ffn_up(n3, weight("ffn_w_gate", 1, n3), weight("ffn_w_up", 1, n3), "ffn1_up")
    h4, _ = mm_res([act1], weight("ffn_w_down", 1, act1), h3, None, "ffn1_down")
    loss_part, dh4, dg_final = final_loss(h4, P["final_norm_g"], target, "loss_head")

    def ffn_bwd(layer, dh_out, h_in, n_in, gate, up, act):
        wg, wu, wd = (weight(n, layer, dh_out) for n in ("ffn_w_gate", "ffn_w_up", "ffn_w_down"))
        tag = "ffn%d" % layer
        dgate, dup = ffn_down_bwd(dh_out, wd, gate, up, tag + "_down_bwd")
        g_wd = mm_tn(act, dh_out, tag + "_dwd")
        dh_in, dgn = mm_nt_rms([(dgate, wg), (dup, wu)], h_in, dep(fg[layer:layer + 1]), dh_out, tag + "_up_bwd")
        out("ffn_w_down", layer, g_wd)
        out("ffn_w_gate", layer, mm_tn(n_in, dgate, tag + "_dwg"))
        out("ffn_w_up", layer, mm_tn(n_in, dup, tag + "_dwu"))
        return dh_in, dgn[0:1]

    dh3, dgn_f1 = ffn_bwd(1, dh4, h3, n3, gate1, up1, act1)

    dcd = mm_nt(dh3, weight("od_w_out", 0, dh3), "od_outproj_bwd")
    dz, g_sw, g_sbt, g_slg, g_slb = gmlp_bwd(z, dcd, 1, dep(P["od_sgu_ln_g"]), P["od_sgu_ln_b"], sw, sbt,
                                             "od_gmlp_bwd")
    out("od_w_out", 0, jnp.concatenate([mm_tn(c_out, dh3, "od_dwo_c"), mm_tn(d_out, dh3, "od_dwo_d")], axis=0))
    acc = None
    for window, dil in DILATED:
        acc = attn_bwd(qk_o, v_o, dcd, 0, c_out, lse_o, d=dil, hkv=N_HEADS, max_dist=BLOCK, sink=None, acc=acc,
                       name="od_attn_bwd_d%d" % dil)
    dproj_o = assemble([(acc[0], True), (acc[1], True), (acc[2], False), (dz, False)], tabs, "od_dproj")
    dh2, dgn_od = mm_nt_rms([(dproj_o, weight("od_w_in", 0, dproj_o))], h2, dep(P["od_norm_g"]), dh3,
                            "od_inproj_bwd")
    out("od_w_in", 0, mm_tn(n2, dproj_o, "od_dwi"))

    dh1, dgn_f0 = ffn_bwd(0, dh2, h1, n1, gate0, up0, act0)

    dac = mm_nt(dh1, weight("ev_w_out", 0, dh1), "ev_outproj_bwd")
    dc1, g_clg, g_clb = conv_bwd_ln(c1, dac, 1, dep(P["ev_conv_ln_g"]), P["ev_conv_ln_b"], "ev_conv_bwd_ln")
    out("ev_w_out", 0, jnp.concatenate([mm_tn(a_e, dh1, "ev_dwo_a"), mm_tn(c_act, dh1, "ev_dwo_c")], axis=0))
    dga, dgb, g_cw, g_cb = conv_bwd_conv(dc1, ga, gb, P["ev_conv_w"], "ev_conv_bwd")
    dq, dk, dv, dsink = attn_bwd(qk_e, v_e, dac, 0, a_e, lse_e, d=1, hkv=A_KV_HEADS, max_dist=BLOCK - 1,
                                 sink=sinkb, acc=None, name="ev_attn_bwd")
    dproj_e = assemble([(dq, True), (dk, True), (dv, False), (dga, False), (dgb, False)], tabs, "ev_dproj")
    dx, dgn_ev = mm_nt_rms([(dproj_e, weight("ev_w_in", 0, dproj_e))], x, dep(P["ev_norm_g"]), dh1,
                           "ev_inproj_bwd")
    out("ev_w_in", 0, mm_tn(n0, dproj_e, "ev_dwi"))

    small = {
        "ev_norm_g": dgn_ev[0:1],
        "ev_sinks": dsink[:, 0:1].reshape(1, N_HEADS),
        "ev_conv_w": g_cw[:CONV_WIDTH],
        "ev_conv_b": g_cb[0:1],
        "ev_conv_ln_g": g_clg[0:1],
        "ev_conv_ln_b": g_clb[0:1],
        "od_norm_g": dgn_od[0:1],
        "od_sgu_ln_g": g_slg[0:1],
        "od_sgu_ln_b": g_slb[0:1],
        "od_spatial_w": g_sw.reshape(D_GROUPS * BLOCK, BLOCK),
        "od_spatial_b": g_sbt.T,
        "ffn_norm_g": jnp.concatenate([dgn_f0, dgn_f1], axis=0),
        "final_norm_g": dg_final[0:1],
    }
    return loss_part, dx, small


BIG = ("ev_w_in", "ev_w_out", "od_w_in", "od_w_out", "ffn_w_gate", "ffn_w_up", "ffn_w_down")
COL_SHARDED = ("ev_w_in", "od_w_in", "ffn_w_gate", "ffn_w_up")
GATHER_GROUPS = (
    (("ev_w_in", 0),),
    (("ev_w_out", 0), ("ffn_w_gate", 0), ("ffn_w_up", 0), ("ffn_w_down", 0)),
    (("od_w_in", 0),),
    (("od_w_out", 0), ("ffn_w_gate", 1), ("ffn_w_up", 1), ("ffn_w_down", 1)),
)
REDUCE_GROUPS = (
    (("ffn_w_down", 1), ("ffn_w_gate", 1), ("ffn_w_up", 1)),
    (("od_w_out", 0),),
    (("od_w_in", 0),),
    (("ffn_w_down", 0), ("ffn_w_gate", 0), ("ffn_w_up", 0)),
    (("ev_w_out", 0),),
    (("ev_w_in", 0),),
)


def _unshard(name, g):
    if name in COL_SHARDED:
        return jnp.moveaxis(g, 0, 1).reshape(g.shape[1], N_DEV * g.shape[2])
    return g.reshape(N_DEV * g.shape[1], g.shape[2])


def _shard_slots(name, full):
    r, c = full.shape
    if name in COL_SHARDED:
        return jnp.moveaxis(full.reshape(r, N_DEV, c // N_DEV), 1, 0)
    return full.reshape(N_DEV, r // N_DEV, c)


def kernel(x, ev_norm_g, ev_w_in, ev_sinks, ev_conv_w, ev_conv_b, ev_conv_ln_g, ev_conv_ln_b, ev_w_out, od_norm_g, od_w_in, od_sgu_ln_g, od_sgu_ln_b, od_spatial_w, od_spatial_b, od_w_out, ffn_norm_g, ffn_w_gate, ffn_w_up, ffn_w_down, final_norm_g, loss_target, m_ev_norm_g, m_ev_w_in, m_ev_sinks, m_ev_conv_w, m_ev_conv_b, m_ev_conv_ln_g, m_ev_conv_ln_b, m_ev_w_out, m_od_norm_g, m_od_w_in, m_od_sgu_ln_g, m_od_sgu_ln_b, m_od_spatial_w, m_od_spatial_b, m_od_w_out, m_ffn_norm_g, m_ffn_w_gate, m_ffn_w_up, m_ffn_w_down, m_final_norm_g, v_ev_norm_g, v_ev_w_in, v_ev_sinks, v_ev_conv_w, v_ev_conv_b, v_ev_conv_ln_g, v_ev_conv_ln_b, v_ev_w_out, v_od_norm_g, v_od_w_in, v_od_sgu_ln_g, v_od_sgu_ln_b, v_od_spatial_w, v_od_spatial_b, v_od_w_out, v_ffn_norm_g, v_ffn_w_gate, v_ffn_w_up, v_ffn_w_down, v_final_norm_g):
    names = ["ev_norm_g", "ev_w_in", "ev_sinks", "ev_conv_w", "ev_conv_b", "ev_conv_ln_g", "ev_conv_ln_b", "ev_w_out",
             "od_norm_g", "od_w_in", "od_sgu_ln_g", "od_sgu_ln_b", "od_spatial_w", "od_spatial_b", "od_w_out",
             "ffn_norm_g", "ffn_w_gate", "ffn_w_up", "ffn_w_down", "final_norm_g"]
    wts = dict(zip(names, [ev_norm_g, ev_w_in, ev_sinks, ev_conv_w, ev_conv_b, ev_conv_ln_g, ev_conv_ln_b, ev_w_out,
                           od_norm_g, od_w_in, od_sgu_ln_g, od_sgu_ln_b, od_spatial_w, od_spatial_b, od_w_out,
                           ffn_norm_g, ffn_w_gate, ffn_w_up, ffn_w_down, final_norm_g]))
    mom = dict(zip(names, [m_ev_norm_g, m_ev_w_in, m_ev_sinks, m_ev_conv_w, m_ev_conv_b, m_ev_conv_ln_g, m_ev_conv_ln_b,
                           m_ev_w_out, m_od_norm_g, m_od_w_in, m_od_sgu_ln_g, m_od_sgu_ln_b, m_od_spatial_w,
                           m_od_spatial_b, m_od_w_out, m_ffn_norm_g, m_ffn_w_gate, m_ffn_w_up, m_ffn_w_down,
                           m_final_norm_g]))
    vel = dict(zip(names, [v_ev_norm_g, v_ev_w_in, v_ev_sinks, v_ev_conv_w, v_ev_conv_b, v_ev_conv_ln_g, v_ev_conv_ln_b,
                           v_ev_w_out, v_od_norm_g, v_od_w_in, v_od_sgu_ln_g, v_od_sgu_ln_b, v_od_spatial_w,
                           v_od_spatial_b, v_od_w_out, v_ffn_norm_g, v_ffn_w_gate, v_ffn_w_up, v_ffn_w_down,
                           v_final_norm_g]))
    me = _my_rank()

    gather_order = [k for grp in GATHER_GROUPS for k in grp]
    ag_handles, ag_token = exchange_start([wts[n][l].astype(ACT) for n, l in gather_order], False, "ag_start")
    full_w = {}

    def weight(name, layer, after):
        if (name, layer) not in full_w:
            gi = [i for i, grp in enumerate(GATHER_GROUPS) if (name, layer) in grp][0]
            idx = [gather_order.index(k) for k in GATHER_GROUPS[gi]]
            lands = exchange_wait(tuple([h[i] for i in idx] for h in ag_handles), after, False, "ag_wait%d" % gi)
            for k, land in zip(GATHER_GROUPS[gi], lands):
                full_w[k] = _unshard(k[0], land)
        return full_w[(name, layer)]

    pending, rs_started = {}, []

    def emit(name, layer, grad):
        pending[(name, layer)] = grad
        for gi, grp in enumerate(REDUCE_GROUPS):
            if (name, layer) in grp and all(k in pending for k in grp):
                handles, token = exchange_start([_shard_slots(k[0], pending[k]).astype(ACT) for k in grp], True,
                                                "rs_start%d" % gi)
                rs_started.append((gi, handles))
                return token
        return None

    sp = jnp.zeros((40, LANES), F32)
    sp = sp.at[0:CONV_WIDTH, 0:64].set(ev_conv_w[0])
    sp = sp.at[32, :].set(od_norm_g[0])
    sp = sp.at[33, 0:64].set(od_sgu_ln_g[0])
    sp = sp.at[34, 0:64].set(od_sgu_ln_b[0])
    spg = _group_exchange(sp, AXES, True, "ag_small")
    P = {
        "ev_norm_g": ev_norm_g + ag_token[0:1, 0:1], "ev_sinks": ev_sinks, "ev_conv_b": ev_conv_b,
        "ev_conv_ln_g": ev_conv_ln_g, "ev_conv_ln_b": ev_conv_ln_b,
        "od_spatial_w": od_spatial_w, "od_spatial_b": od_spatial_b, "ffn_norm_g": ffn_norm_g,
        "final_norm_g": final_norm_g.reshape(1, D_MODEL),
        "ev_conv_w": jnp.moveaxis(spg[:, 0:CONV_WIDTH, 0:64], 0, 1).reshape(CONV_WIDTH, CONV_CH),
        "od_norm_g": spg[:, 32, :].reshape(1, D_MODEL),
        "od_sgu_ln_g": spg[:, 33, 0:64].reshape(1, D_CH),
        "od_sgu_ln_b": spg[:, 34, 0:64].reshape(1, D_CH),
    }

    loss_part, dx, small = _local_step(x[0], loss_target[0], weight, emit, P)
    loss = lax.psum(loss_part[0, 0], AXES)

    per_layer = {}
    for gi, handles in rs_started:
        lands = exchange_wait(handles, dx, True, "rs_wait%d" % gi)
        for (n, l), land in zip(REDUCE_GROUPS[gi], lands):
            per_layer[(n, l)] = sum_adamw(land, wts[n][l], mom[n][l], vel[n][l], "adamw_%s%d" % (n, l))
    out_g, out_d, out_m, out_v = {}, {}, {}, {}
    for n in BIG:
        for j, dst in enumerate((out_g, out_d, out_m, out_v)):
            dst[n] = jnp.stack([per_layer[(n, l)][j] for l in range(wts[n].shape[0])])

    wide = ["ev_norm_g", "ev_sinks", "ev_conv_w", "ev_conv_b", "ev_conv_ln_g", "ev_conv_ln_b", "od_norm_g",
            "od_sgu_ln_g", "od_sgu_ln_b", "ffn_norm_g", "final_norm_g"]
    blk_a = jnp.concatenate(
        [jnp.pad(small[n], ((0, 0), (0, D_MODEL - small[n].shape[1]))) for n in wide], axis=0)
    blk_a = jnp.pad(blk_a, ((0, 48 - blk_a.shape[0]), (0, 0)))
    blk_b = jnp.concatenate([small["od_spatial_w"], small["od_spatial_b"]], axis=0)
    sum_a = sum_slots(_group_exchange(blk_a, AXES, True, "ar_small_a"), F32, "ar_small_a_sum")
    sum_b = sum_slots(_group_exchange(blk_b, AXES, True, "ar_small_b"), F32, "ar_small_b_sum")
    full = {}
    off = 0
    for n in wide:
        r_, c_ = small[n].shape
        full[n] = sum_a[off:off + r_, 0:c_]
        off += r_
    full["od_spatial_w"] = sum_b[0:D_GROUPS * BLOCK]
    full["od_spatial_b"] = sum_b[D_GROUPS * BLOCK:D_GROUPS * BLOCK + D_GROUPS]
    full["ev_conv_w"] = lax.dynamic_slice_in_dim(full["ev_conv_w"], me * 64, 64, axis=1)
    full["od_norm_g"] = lax.dynamic_slice_in_dim(full["od_norm_g"], me * 128, 128, axis=1)
    full["od_sgu_ln_g"] = lax.dynamic_slice_in_dim(full["od_sgu_ln_g"], me * 64, 64, axis=1)
    full["od_sgu_ln_b"] = lax.dynamic_slice_in_dim(full["od_sgu_ln_b"], me * 64, 64, axis=1)

    small_names = [n for n in names if n not in BIG]
    view = {n: ((-1, wts[n].shape[-1]) if wts[n].ndim > 1 else (1, -1)) for n in small_names}
    ds, ms, vs = adamw_small([wts[n].reshape(view[n]) for n in small_names],
                             [full[n].reshape(view[n]) for n in small_names],
                             [mom[n].reshape(view[n]) for n in small_names],
                             [vel[n].reshape(view[n]) for n in small_names], "adamw_small")
    for i, n in enumerate(small_names):
        shp = wts[n].shape
        out_g[n], out_d[n], out_m[n], out_v[n] = (full[n].reshape(shp), ds[i].reshape(shp), ms[i].reshape(shp),
                                                  vs[i].reshape(shp))

    return (loss, dx[None], *[out_g[n] for n in names], *[out_d[n] for n in names],
            *[out_m[n] for n in names], *[out_v[n] for n in names])
```

```python
import functools
import math

import jax
import jax.numpy as jnp
from jax import lax
from jax.experimental import pallas as pl
from jax.experimental.pallas import tpu as pltpu

F32 = jnp.float32
ACT = jnp.bfloat16

D_MODEL = 1024
HEAD_DIM = 64
N_HEADS = 8
ATT_W = N_HEADS * HEAD_DIM
A_KV_HEADS = 2
CONV_CH = 512
CONV_WIDTH = 31
CONV_HALO = 32
D_CH = 512
D_GROUPS = 8
BLOCK = 128
D_FF = 2816
ROT_DIM = 16
ROPE_THETA = 500000.0
RMS_EPS = 1e-6
LN_EPS = 1e-5
DILATED = ((128, 1), (512, 4), (2048, 16))
NEG = -1e30
LANES = 128

ADAM_LR = 0.001
ADAM_B1 = 0.9
ADAM_B2 = 0.999
ADAM_EPS = 1e-08
ADAM_WD = 0.01
ADAM_STEP = 10

V7X_VMEM_LIMIT = 56 * 1024 * 1024
SMALL_BLOCK_BYTES = 6 * 1024 * 1024
N_DEV = 8

NN = (((1,), (0,)), ((), ()))
NT = (((1,), (1,)), ((), ()))
TN = (((0,), (0,)), ((), ()))
MESH = pl.DeviceIdType.MESH
AXES = ("x", "y", "c")


def _dot(a, b, dims):
    return lax.dot_general(a, b, dims, preferred_element_type=F32)


def _cp(*sem):
    return pltpu.CompilerParams(dimension_semantics=sem if sem else None,
                                vmem_limit_bytes=V7X_VMEM_LIMIT)


def _tile(n, prefs):
    for p in prefs:
        if n % p == 0:
            return p
    return n


def _sigmoid(x):
    return 1.0 / (1.0 + jnp.exp(-x))


def _rowspec(tm, w, col=0):
    return pl.BlockSpec((tm, w), lambda i, col=col: (i, col))


def _fullspec(shape):
    nd = len(shape)
    return pl.BlockSpec(shape, lambda *a, nd=nd: (0,) * nd)


def _rope_tables(seq):
    half = ROT_DIM // 2
    inv_freq = ROPE_THETA ** (-jnp.arange(half, dtype=F32) * (2.0 / ROT_DIM))
    ang = jnp.arange(seq, dtype=jnp.int32).astype(F32)[:, None] * inv_freq[None, :]
    cos, sin = jnp.cos(ang), jnp.sin(ang)
    lane = jnp.arange(LANES)
    jm = lane % HEAD_DIM
    idx = jm % half
    c = jnp.where(jm[None, :] < ROT_DIM, cos[:, idx], 1.0)
    sa = jnp.where(jm[None, :] < half, -sin[:, idx], 0.0)
    sb = jnp.where((jm[None, :] >= half) & (jm[None, :] < ROT_DIM), sin[:, idx], 0.0)
    return c.astype(F32), sa.astype(F32), sb.astype(F32)


def _rope(x, c, sa, sb):
    return x * c + pltpu.roll(x, LANES - 8, 1) * sa + pltpu.roll(x, 8, 1) * sb


def _rope_t(d, c, sa, sb):
    return d * c + pltpu.roll(d * sa, 8, 1) + pltpu.roll(d * sb, LANES - 8, 1)


def rmsnorm(h, g, name):
    s = h.shape[0]
    tm = _tile(s, (512,))

    def body(h_ref, g_ref, o_ref):
        x = h_ref[...]
        r = lax.rsqrt(jnp.mean(x * x, axis=-1, keepdims=True) + RMS_EPS)
        o_ref[...] = (x * r * g_ref[...]).astype(o_ref.dtype)

    return pl.pallas_call(
        body, grid=(s // tm,), name=name,
        in_specs=[_rowspec(tm, D_MODEL), _fullspec((1, D_MODEL))],
        out_specs=_rowspec(tm, D_MODEL),
        out_shape=jax.ShapeDtypeStruct((s, D_MODEL), ACT),
        compiler_params=_cp("parallel"))(h, g)


def inproj(n, w, tabs, nqk, splits, name):
    s = n.shape[0]
    ntot = w.shape[1]
    assert sum(splits) == ntot and splits[0] == nqk
    tm = _tile(s, (256,))

    def body(n_ref, w_ref, c_ref, sa_ref, sb_ref, *outs):
        res = _dot(n_ref[...], w_ref[...], NN)
        c, sa, sb = c_ref[...], sa_ref[...], sb_ref[...]
        for g in range(nqk // LANES):
            x = res[:, g * LANES:(g + 1) * LANES]
            outs[0][:, g * LANES:(g + 1) * LANES] = _rope(x, c, sa, sb).astype(ACT)
        off = nqk
        for o_ref, wd in zip(outs[1:], splits[1:]):
            o_ref[...] = res[:, off:off + wd].astype(ACT)
            off += wd

    return pl.pallas_call(
        body, grid=(s // tm,), name=name,
        in_specs=[_rowspec(tm, D_MODEL), _fullspec((D_MODEL, ntot))] + [_rowspec(tm, LANES)] * 3,
        out_specs=[_rowspec(tm, wd) for wd in splits],
        out_shape=[jax.ShapeDtypeStruct((s, wd), ACT) for wd in splits],
        compiler_params=_cp("parallel"))(n, w, *tabs)


def ffn_up(n, wg, wu, name):
    s = n.shape[0]
    f = wg.shape[1]
    tm = _tile(s, (512,))
    tf = _tile(f, (1408, 512, 256, 128))

    def body(n_ref, wg_ref, wu_ref, g_ref, u_ref, a_ref):
        a = n_ref[...]
        g = _dot(a, wg_ref[...], NN)
        u = _dot(a, wu_ref[...], NN)
        g_ref[...] = g.astype(ACT)
        u_ref[...] = u.astype(ACT)
        a_ref[...] = (g * _sigmoid(g) * u).astype(ACT)

    wspec = pl.BlockSpec((D_MODEL, tf), lambda j, i: (0, j))
    ospec = pl.BlockSpec((tm, tf), lambda j, i: (i, j))
    return pl.pallas_call(
        body, grid=(f // tf, s // tm), name=name,
        in_specs=[pl.BlockSpec((tm, D_MODEL), lambda j, i: (i, 0)), wspec, wspec],
        out_specs=[ospec] * 3,
        out_shape=[jax.ShapeDtypeStruct((s, f), ACT)] * 3,
        compiler_params=_cp("parallel", "parallel"))(n, wg, wu)


def mm_res(parts, w, h, gnext, name):
    s = h.shape[0]
    tm = _tile(s, (256,))
    widths = [p.shape[1] for p in parts]
    assert sum(widths) == w.shape[0]
    np_ = len(parts)

    def body(*refs):
        p_refs = refs[:np_]
        w_ref, h_ref = refs[np_], refs[np_ + 1]
        rest = refs[np_ + 2:]
        acc = h_ref[...]
        off = 0
        for p_ref, wd in zip(p_refs, widths):
            acc = acc + _dot(p_ref[...], w_ref[off:off + wd, :], NN)
            off += wd
        if gnext is None:
            rest[0][...] = acc
        else:
            g_ref, ho_ref, no_ref = rest
            ho_ref[...] = acc
            r = lax.rsqrt(jnp.mean(acc * acc, axis=-1, keepdims=True) + RMS_EPS)
            no_ref[...] = (acc * r * g_ref[...]).astype(ACT)

    in_specs = [_rowspec(tm, wd) for wd in widths] + [_fullspec(w.shape), _rowspec(tm, D_MODEL)]
    args = list(parts) + [w, h]
    out_specs = [_rowspec(tm, D_MODEL)]
    out_shape = [jax.ShapeDtypeStruct((s, D_MODEL), F32)]
    if gnext is not None:
        in_specs.append(_fullspec((1, D_MODEL)))
        args.append(gnext)
        out_specs.append(_rowspec(tm, D_MODEL))
        out_shape.append(jax.ShapeDtypeStruct((s, D_MODEL), ACT))
    out = pl.pallas_call(
        body, grid=(s // tm,), name=name, in_specs=in_specs, out_specs=out_specs,
        out_shape=out_shape, compiler_params=_cp("parallel"))(*args)
    return (out[0], None) if gnext is None else (out[0], out[1])


def _band_masks(n, max_dist):
    qi = lax.broadcasted_iota(jnp.int32, (BLOCK, BLOCK), 0)
    kj = lax.broadcasted_iota(jnp.int32, (BLOCK, BLOCK), 1)
    m_cur = kj <= qi
    m_prev = (kj > qi) if max_dist == BLOCK - 1 else (kj >= qi)
    m_prev = jnp.logical_and(m_prev, n > 0)
    return m_cur, m_prev


def _fold(a, d):
    return a.reshape(a.shape[0] // d, d * a.shape[1])


def attn_fwd(qk, v, *, d, hkv, max_dist, sink, out_dtype, name):
    s = qk.shape[0]
    kvw = hkv * HEAD_DIM
    wqk = ATT_W + kvw
    assert qk.shape[1] == wqk and (d == 1 or (wqk % ATT_W == 0 and wqk % kvw == 0))
    nb = s // d // BLOCK
    grp = N_HEADS // hkv
    qpb, kpb, koff = wqk // ATT_W, wqk // kvw, ATT_W // kvw

    def body(*refs):
        if sink is None:
            q_ref, kc_ref, kp_ref, vc_ref, vp_ref, o_ref, l_ref, o_buf = refs
        else:
            q_ref, kc_ref, kp_ref, vc_ref, vp_ref, s_ref, o_ref, l_ref, o_buf = refs
        n = pl.program_id(1)
        m_cur, m_prev = _band_masks(n, max_dist)
        qsl = [slice(h * HEAD_DIM, (h + 1) * HEAD_DIM) for h in range(N_HEADS)]
        ksl = [slice((h // grp) * HEAD_DIM, (h // grp + 1) * HEAD_DIM) for h in range(N_HEADS)]
        scores = []
        for h in range(N_HEADS):
            q = q_ref[:, qsl[h]] * 0.125
            scores.append((_dot(q, kc_ref[:, ksl[h]], NT), _dot(q, kp_ref[:, ksl[h]], NT)))
        probs = []
        for h in range(N_HEADS):
            sc = jnp.where(m_cur, scores[h][0], NEG)
            sp = jnp.where(m_prev, scores[h][1], NEG)
            m = jnp.max(jnp.maximum(sc, sp), axis=-1, keepdims=True)
            if sink is not None:
                sk = s_ref[h:h + 1, 0:1]
                m = jnp.maximum(m, sk)
            pc = jnp.exp(sc - m)
            pp = jnp.exp(sp - m)
            l = jnp.sum(pc + pp, axis=-1, keepdims=True)
            if sink is not None:
                l = l + jnp.exp(sk - m)
            l_ref[:, qsl[h]] = jnp.broadcast_to(m + jnp.log(l), (BLOCK, HEAD_DIM))
            probs.append((pc.astype(ACT), pp.astype(ACT), 1.0 / l))
        for h in range(N_HEADS):
            pc, pp, rl = probs[h]
            o = _dot(pc, vc_ref[:, ksl[h]], NN) + _dot(pp, vp_ref[:, ksl[h]], NN)
            o_buf[:, qsl[h]] = o * rl
        o_ref[...] = o_buf[...].astype(o_ref.dtype)

    prev = lambda n: jnp.maximum(n - 1, 0)
    in_specs = [
        pl.BlockSpec((BLOCK, ATT_W), lambda r, n: (n, r * qpb)),
        pl.BlockSpec((BLOCK, kvw), lambda r, n: (n, r * kpb + koff)),
        pl.BlockSpec((BLOCK, kvw), lambda r, n: (prev(n), r * kpb + koff)),
        pl.BlockSpec((BLOCK, kvw), lambda r, n: (n, r)),
        pl.BlockSpec((BLOCK, kvw), lambda r, n: (prev(n), r)),
    ]
    qkf, vf = _fold(qk, d), _fold(v, d)
    args = [qkf, qkf, qkf, vf, vf]
    if sink is not None:
        in_specs.append(_fullspec((N_HEADS, LANES)))
        args.append(sink)
    ospec = pl.BlockSpec((BLOCK, ATT_W), lambda r, n: (n, r))
    o, lse = pl.pallas_call(
        body, grid=(d, nb), name=name, in_specs=in_specs, out_specs=[ospec, ospec],
        out_shape=[jax.ShapeDtypeStruct((s // d, d * ATT_W), out_dtype),
                   jax.ShapeDtypeStruct((s // d, d * ATT_W), F32)],
        scratch_shapes=[pltpu.VMEM((BLOCK, ATT_W), F32)],
        compiler_params=_cp("parallel", "parallel"))(*args)
    return o.reshape(s, ATT_W), lse.reshape(s, ATT_W)


def attn_bwd(qk, v, do_src, do_col, o, lse, *, d, hkv, max_dist, sink, out_dtype, name):
    s = qk.shape[0]
    kvw = hkv * HEAD_DIM
    wqk = ATT_W + kvw
    nb = s // d // BLOCK
    grp = N_HEADS // hkv
    qpb, kpb, koff = wqk // ATT_W, wqk // kvw, ATT_W // kvw
    dob = do_src.shape[1] // ATT_W
    has_sink = sink is not None

    def body(*refs):
        refs = list(refs)
        q_ref, kc_ref, kp_ref, vc_ref, vp_ref, do_ref, o_ref, l_ref = refs[:8]
        pos = 8
        if has_sink:
            s_ref = refs[pos]
            pos += 1
        dq_ref, dk_ref, dv_ref = refs[pos:pos + 3]
        pos += 3
        if has_sink:
            ds_ref = refs[pos]
            pos += 1
        ck_ref, cv_ref, dq_buf, dk_buf, dv_buf = refs[pos:pos + 5]
        r_id = pl.program_id(0)
        n = pl.program_id(1)

        @pl.when(n == 0)
        def _():
            ck_ref[...] = jnp.zeros_like(ck_ref)
            cv_ref[...] = jnp.zeros_like(cv_ref)

        if has_sink:
            @pl.when(jnp.logical_and(n == 0, r_id == 0))
            def _():
                ds_ref[...] = jnp.zeros_like(ds_ref)

        @pl.when(n < nb)
        def _():
            m_cur, m_prev = _band_masks(n, max_dist)
            qsl = [slice(h * HEAD_DIM, (h + 1) * HEAD_DIM) for h in range(N_HEADS)]
            ksl = [slice((h // grp) * HEAD_DIM, (h // grp + 1) * HEAD_DIM) for h in range(N_HEADS)]
            qs, first = [], []
            for h in range(N_HEADS):
                q = q_ref[:, qsl[h]] * 0.125
                do_h = do_ref[:, qsl[h]]
                qs.append(q)
                first.append((_dot(q, kc_ref[:, ksl[h]], NT), _dot(q, kp_ref[:, ksl[h]], NT),
                              _dot(do_h, vc_ref[:, ksl[h]], NT), _dot(do_h, vp_ref[:, ksl[h]], NT)))
            mid = []
            for h in range(N_HEADS):
                sc, sp, dpc, dpp = first[h]
                lse_h = l_ref[:, h * HEAD_DIM:h * HEAD_DIM + 1]
                pc = jnp.exp(jnp.where(m_cur, sc, NEG) - lse_h)
                pp = jnp.exp(jnp.where(m_prev, sp, NEG) - lse_h)
                e = jnp.sum(do_ref[:, qsl[h]].astype(F32) * o_ref[:, qsl[h]].astype(F32), axis=-1, keepdims=True)
                mid.append(((pc * (dpc - e)).astype(ACT), (pp * (dpp - e)).astype(ACT),
                            pc.astype(ACT), pp.astype(ACT)))
                if has_sink:
                    sk = s_ref[h:h + 1, 0:1]
                    dsk = -jnp.sum(jnp.exp(sk - lse_h) * e, axis=0, keepdims=True)
                    ds_ref[h:h + 1, :] = ds_ref[h:h + 1, :] + dsk
            cur_k = [None] * hkv
            cur_v = [None] * hkv
            prv_k = [None] * hkv
            prv_v = [None] * hkv
            for h in range(N_HEADS):
                kh = h // grp
                dsc, dsp, pc, pp = mid[h]
                do_h = do_ref[:, qsl[h]]
                dq = (_dot(dsc, kc_ref[:, ksl[h]], NN) + _dot(dsp, kp_ref[:, ksl[h]], NN)) * 0.125
                dq_buf[:, qsl[h]] = dq
                parts = ((cur_k, _dot(dsc, qs[h], TN)), (prv_k, _dot(dsp, qs[h], TN)),
                         (cur_v, _dot(pc, do_h, TN)), (prv_v, _dot(pp, do_h, TN)))
                for lst, val in parts:
                    lst[kh] = val if lst[kh] is None else lst[kh] + val
            for kh in range(hkv):
                ks = slice(kh * HEAD_DIM, (kh + 1) * HEAD_DIM)
                dk_buf[:, ks] = ck_ref[:, ks] + prv_k[kh]
                dv_buf[:, ks] = cv_ref[:, ks] + prv_v[kh]
                ck_ref[:, ks] = cur_k[kh]
                cv_ref[:, ks] = cur_v[kh]
            dq_ref[...] = dq_buf[...].astype(dq_ref.dtype)
            dk_ref[...] = dk_buf[...].astype(dk_ref.dtype)
            dv_ref[...] = dv_buf[...].astype(dv_ref.dtype)

        @pl.when(n == nb)
        def _():
            dk_ref[...] = ck_ref[...].astype(dk_ref.dtype)
            dv_ref[...] = cv_ref[...].astype(dv_ref.dtype)

    qrow = lambda n: jnp.minimum(n, nb - 1)
    prow = lambda n: jnp.maximum(jnp.minimum(n, nb - 1) - 1, 0)
    krow = lambda n: jnp.maximum(n - 1, 0)
    in_specs = [
        pl.BlockSpec((BLOCK, ATT_W), lambda r, n: (qrow(n), r * qpb)),
        pl.BlockSpec((BLOCK, kvw), lambda r, n: (qrow(n), r * kpb + koff)),
        pl.BlockSpec((BLOCK, kvw), lambda r, n: (prow(n), r * kpb + koff)),
        pl.BlockSpec((BLOCK, kvw), lambda r, n: (qrow(n), r)),
        pl.BlockSpec((BLOCK, kvw), lambda r, n: (prow(n), r)),
        pl.BlockSpec((BLOCK, ATT_W), lambda r, n: (qrow(n), r * dob + do_col)),
        pl.BlockSpec((BLOCK, ATT_W), lambda r, n: (qrow(n), r)),
        pl.BlockSpec((BLOCK, ATT_W), lambda r, n: (qrow(n), r)),
    ]
    qkf, vf = _fold(qk, d), _fold(v, d)
    args = [qkf, qkf, qkf, vf, vf, _fold(do_src, d), _fold(o, d), _fold(lse, d)]
    if has_sink:
        in_specs.append(_fullspec((N_HEADS, LANES)))
        args.append(sink)
    qspec = pl.BlockSpec((BLOCK, ATT_W), lambda r, n: (qrow(n), r))
    kspec = pl.BlockSpec((BLOCK, kvw), lambda r, n: (krow(n), r))
    out_specs = [qspec, kspec, kspec]
    out_shape = [jax.ShapeDtypeStruct((s // d, d * ATT_W), out_dtype),
                 jax.ShapeDtypeStruct((s // d, d * kvw), out_dtype),
                 jax.ShapeDtypeStruct((s // d, d * kvw), out_dtype)]
    if has_sink:
        out_specs.append(_fullspec((N_HEADS, LANES)))
        out_shape.append(jax.ShapeDtypeStruct((N_HEADS, LANES), F32))
    out = pl.pallas_call(
        body, grid=(d, nb + 1), name=name, in_specs=in_specs, out_specs=out_specs,
        out_shape=out_shape,
        scratch_shapes=[pltpu.VMEM((BLOCK, kvw), F32), pltpu.VMEM((BLOCK, kvw), F32),
                        pltpu.VMEM((BLOCK, ATT_W), F32), pltpu.VMEM((BLOCK, kvw), F32), pltpu.VMEM((BLOCK, kvw), F32)],
        compiler_params=_cp("arbitrary", "arbitrary"))(*args)
    res = [out[0].reshape(s, ATT_W), out[1].reshape(s, kvw), out[2].reshape(s, kvw)]
    if has_sink:
        res.append(out[3])
    return res


def combine_fwd(os_, lses, name):
    s = os_[0].shape[0]
    tm = _tile(s, (512,))

    def body(o1, o2, o3, l1, l2, l3, c_ref, l_ref):
        a, b, c = l1[...], l2[...], l3[...]
        m = jnp.maximum(jnp.maximum(a, b), c)
        wa, wb, wc = jnp.exp(a - m), jnp.exp(b - m), jnp.exp(c - m)
        tot = wa + wb + wc
        c_ref[...] = ((wa * o1[...] + wb * o2[...] + wc * o3[...]) / tot).astype(ACT)
        l_ref[...] = m + jnp.log(tot)

    spec = _rowspec(tm, ATT_W)
    return pl.pallas_call(
        body, grid=(s // tm,), name=name, in_specs=[spec] * 6, out_specs=[spec, spec],
        out_shape=[jax.ShapeDtypeStruct((s, ATT_W), ACT), jax.ShapeDtypeStruct((s, ATT_W), F32)],
        compiler_params=_cp("parallel"))(*os_, *lses)


def assemble(parts, tabs, name):
    s = parts[0][0][0].shape[0]
    tm = _tile(s, (256,))
    widths = [terms[0].shape[1] for terms, _ in parts]
    counts = [len(terms) for terms, _ in parts]
    flags = [f for _, f in parts]
    flat = [t for terms, _ in parts for t in terms]

    def body(*refs):
        c_ref, sa_ref, sb_ref, o_ref = refs[len(flat):]
        c, sa, sb = c_ref[...], sa_ref[...], sb_ref[...]
        off = 0
        first = 0
        for wd, cnt, fl in zip(widths, counts, flags):
            t_refs = refs[first:first + cnt]
            first += cnt
            for g in range(wd // LANES):
                cols = slice(g * LANES, (g + 1) * LANES)
                x = t_refs[0][:, cols].astype(F32)
                for t_ref in t_refs[1:]:
                    x = x + t_ref[:, cols].astype(F32)
                if fl:
                    x = _rope_t(x, c, sa, sb)
                o_ref[:, off + g * LANES:off + (g + 1) * LANES] = x.astype(ACT)
            off += wd

    tot = sum(widths)
    return pl.pallas_call(
        body, grid=(s // tm,), name=name,
        in_specs=[_rowspec(tm, wd) for wd, cnt in zip(widths, counts) for _ in range(cnt)]
        + [_rowspec(tm, LANES)] * 3,
        out_specs=_rowspec(tm, tot), out_shape=jax.ShapeDtypeStruct((s, tot), ACT),
        compiler_params=_cp("parallel"))(*flat, *tabs)


def _ln_stats(x):
    mu = jnp.mean(x, axis=-1, keepdims=True)
    xc = x - mu
    var = jnp.mean(xc * xc, axis=-1, keepdims=True)
    rstd = lax.rsqrt(var + LN_EPS)
    return xc * rstd, rstd


def conv_fwd(ga, gb, cw, cb, lg, lb, name):
    s = ga.shape[0]
    tm = _tile(s, (256,))
    hb = tm // CONV_HALO

    def body(ga_ref, gb_ref, gah_ref, gbh_ref, cw_ref, cb_ref, lg_ref, lb_ref, c_ref, c1_ref, buf):
        i = pl.program_id(0)
        halo = gah_ref[...].astype(F32) * _sigmoid(gbh_ref[...].astype(F32))
        buf[0:CONV_HALO, :] = jnp.where(i > 0, halo, 0.0)
        buf[CONV_HALO:, :] = ga_ref[...].astype(F32) * _sigmoid(gb_ref[...].astype(F32))
        acc = jnp.broadcast_to(cb_ref[...], (tm, CONV_CH))
        for j in range(CONV_WIDTH):
            acc = acc + cw_ref[j:j + 1, :] * buf[pl.ds(CONV_HALO - (CONV_WIDTH - 1) + j, tm), :]
        c1_ref[...] = acc.astype(c1_ref.dtype)
        xh, _ = _ln_stats(acc)
        y = xh * lg_ref[...] + lb_ref[...]
        c_ref[...] = (y * _sigmoid(y)).astype(ACT)

    hspec = pl.BlockSpec((CONV_HALO, CONV_CH), lambda i: (jnp.maximum(i * hb - 1, 0), 0))
    vec = _fullspec((1, CONV_CH))
    spec = _rowspec(tm, CONV_CH)
    return pl.pallas_call(
        body, grid=(s // tm,), name=name,
        in_specs=[spec, spec, hspec, hspec, _fullspec((CONV_WIDTH, CONV_CH)), vec, vec, vec],
        out_specs=[spec, spec],
        out_shape=[jax.ShapeDtypeStruct((s, CONV_CH), ACT), jax.ShapeDtypeStruct((s, CONV_CH), F32)],
        scratch_shapes=[pltpu.VMEM((tm + CONV_HALO, CONV_CH), F32)],
        compiler_params=_cp("parallel"))(ga, gb, ga, gb, cw, cb, lg, lb)


def conv_bwd_ln(c1, dsrc, dcol, lg, lb, name):
    s = c1.shape[0]
    tm = _tile(s, (256,))

    def body(c1_ref, d_ref, lg_ref, lb_ref, o_ref, dg_ref, db_ref):
        @pl.when(pl.program_id(0) == 0)
        def _():
            dg_ref[...] = jnp.zeros_like(dg_ref)
            db_ref[...] = jnp.zeros_like(db_ref)

        xh, rstd = _ln_stats(c1_ref[...].astype(F32))
        y = xh * lg_ref[...] + lb_ref[...]
        sg = _sigmoid(y)
        dy = d_ref[...].astype(F32) * (sg * (1.0 + y * (1.0 - sg)))
        dg_ref[0:1, :] = dg_ref[0:1, :] + jnp.sum(dy * xh, axis=0, keepdims=True)
        db_ref[0:1, :] = db_ref[0:1, :] + jnp.sum(dy, axis=0, keepdims=True)
        dxh = dy * lg_ref[...]
        o_ref[...] = rstd * (dxh - jnp.mean(dxh, axis=-1, keepdims=True)
                             - xh * jnp.mean(dxh * xh, axis=-1, keepdims=True))

    vec = _fullspec((1, CONV_CH))
    acc = _fullspec((8, CONV_CH))
    return pl.pallas_call(
        body, grid=(s // tm,), name=name,
        in_specs=[_rowspec(tm, CONV_CH), _rowspec(tm, CONV_CH, dcol), vec, vec],
        out_specs=[_rowspec(tm, CONV_CH), acc, acc],
        out_shape=[jax.ShapeDtypeStruct((s, CONV_CH), F32)] + [jax.ShapeDtypeStruct((8, CONV_CH), F32)] * 2,
        compiler_params=_cp("arbitrary"))(c1, dsrc, lg, lb)


def conv_bwd_conv(dc1, ga, gb, cw, name):
    s = ga.shape[0]
    tm = _tile(s, (256,))
    hb = tm // CONV_HALO
    nt = s // tm
    last_h = s // CONV_HALO - 1

    def body(d_ref, dn_ref, ga_ref, gb_ref, gah_ref, gbh_ref, cw_ref,
             dga_ref, dgb_ref, dw_ref, db_ref, dbuf, cbuf):
        i = pl.program_id(0)

        @pl.when(i == 0)
        def _():
            dw_ref[...] = jnp.zeros_like(dw_ref)
            db_ref[...] = jnp.zeros_like(db_ref)

        d = d_ref[...]
        dbuf[0:tm, :] = d
        dbuf[tm:, :] = jnp.where(i < nt - 1, dn_ref[...], 0.0)
        halo = gah_ref[...].astype(F32) * _sigmoid(gbh_ref[...].astype(F32))
        cbuf[0:CONV_HALO, :] = jnp.where(i > 0, halo, 0.0)
        a = ga_ref[...].astype(F32)
        sg = _sigmoid(gb_ref[...].astype(F32))
        cbuf[CONV_HALO:, :] = a * sg
        dc0 = jnp.zeros((tm, CONV_CH), F32)
        for j in range(CONV_WIDTH):
            dc0 = dc0 + cw_ref[j:j + 1, :] * dbuf[pl.ds(CONV_WIDTH - 1 - j, tm), :]
            tap = cbuf[pl.ds(CONV_HALO - (CONV_WIDTH - 1) + j, tm), :]
            dw_ref[j:j + 1, :] = dw_ref[j:j + 1, :] + jnp.sum(d * tap, axis=0, keepdims=True)
        db_ref[0:1, :] = db_ref[0:1, :] + jnp.sum(d, axis=0, keepdims=True)
        dga_ref[...] = (dc0 * sg).astype(ACT)
        dgb_ref[...] = (dc0 * a * sg * (1.0 - sg)).astype(ACT)

    spec = _rowspec(tm, CONV_CH)
    hprev = pl.BlockSpec((CONV_HALO, CONV_CH), lambda i: (jnp.maximum(i * hb - 1, 0), 0))
    hnext = pl.BlockSpec((CONV_HALO, CONV_CH), lambda i: (jnp.minimum((i + 1) * hb, last_h), 0))
    return pl.pallas_call(
        body, grid=(nt,), name=name,
        in_specs=[spec, hnext, spec, spec, hprev, hprev, _fullspec((CONV_WIDTH, CONV_CH))],
        out_specs=[spec, spec, _fullspec((CONV_HALO, CONV_CH)), _fullspec((8, CONV_CH))],
        out_shape=[jax.ShapeDtypeStruct((s, CONV_CH), ACT)] * 2
        + [jax.ShapeDtypeStruct((CONV_HALO, CONV_CH), F32), jax.ShapeDtypeStruct((8, CONV_CH), F32)],
        scratch_shapes=[pltpu.VMEM((tm + CONV_HALO, CONV_CH), F32)] * 2,
        compiler_params=_cp("arbitrary"))(dc1, dc1, ga, gb, ga, gb, cw)


_GELU_K = math.sqrt(2.0 / math.pi)
_GELU_C = 0.044715


def _gelu(x):
    return 0.5 * x * (1.0 + jnp.tanh(_GELU_K * (x + _GELU_C * x * x * x)))


def _gelu_grad(x):
    t = jnp.tanh(_GELU_K * (x + _GELU_C * x * x * x))
    return 0.5 * (1.0 + t) + 0.5 * x * (1.0 - t * t) * _GELU_K * (1.0 + 3.0 * _GELU_C * x * x)


def _tril():
    qi = lax.broadcasted_iota(jnp.int32, (BLOCK, BLOCK), 0)
    kj = lax.broadcasted_iota(jnp.int32, (BLOCK, BLOCK), 1)
    return kj <= qi


def gmlp_fwd(z, lg, lb, sw, sbt, name):
    s = z.shape[0]

    def body(z_ref, lg_ref, lb_ref, sw_ref, sb_ref, o_ref, o_buf):
        zz = _gelu(z_ref[...].astype(F32))
        u = zz[:, :D_CH]
        xh, _ = _ln_stats(zz[:, D_CH:])
        gn = (xh * lg_ref[...] + lb_ref[...]).astype(ACT)
        tril = _tril()
        for g in range(D_GROUPS):
            cs = slice(g * HEAD_DIM, (g + 1) * HEAD_DIM)
            w = jnp.where(tril, sw_ref[g], 0.0).astype(ACT)
            mixed = _dot(w, gn[:, cs], NN) + sb_ref[:, g:g + 1]
            o_buf[:, cs] = u[:, cs] * mixed
        o_ref[...] = o_buf[...].astype(ACT)

    return pl.pallas_call(
        body, grid=(s // BLOCK,), name=name,
        in_specs=[_rowspec(BLOCK, 2 * D_CH), _fullspec((1, D_CH)), _fullspec((1, D_CH)),
                  _fullspec((D_GROUPS, BLOCK, BLOCK)), _fullspec((BLOCK, D_GROUPS))],
        out_specs=_rowspec(BLOCK, D_CH), out_shape=jax.ShapeDtypeStruct((s, D_CH), ACT),
        scratch_shapes=[pltpu.VMEM((BLOCK, D_CH), F32)],
        compiler_params=_cp("parallel"))(z, lg, lb, sw, sbt)


def gmlp_bwd(z, dsrc, dcol, lg, lb, sw, sbt, name):
    s = z.shape[0]

    def body(z_ref, d_ref, lg_ref, lb_ref, sw_ref, sb_ref, dz_ref, dw_ref, dsb_ref, dg_ref, db_ref, dgn_buf, du_buf):
        @pl.when(pl.program_id(0) == 0)
        def _():
            dw_ref[...] = jnp.zeros_like(dw_ref)
            dsb_ref[...] = jnp.zeros_like(dsb_ref)
            dg_ref[...] = jnp.zeros_like(dg_ref)
            db_ref[...] = jnp.zeros_like(db_ref)

        zr = z_ref[...].astype(F32)
        zz = _gelu(zr)
        u = zz[:, :D_CH]
        xh, rstd = _ln_stats(zz[:, D_CH:])
        gn = (xh * lg_ref[...] + lb_ref[...]).astype(ACT)
        dd = d_ref[...].astype(F32)
        tril = _tril()
        for g in range(D_GROUPS):
            cs = slice(g * HEAD_DIM, (g + 1) * HEAD_DIM)
            w = jnp.where(tril, sw_ref[g], 0.0).astype(ACT)
            gn_g = gn[:, cs]
            mixed = _dot(w, gn_g, NN) + sb_ref[:, g:g + 1]
            du = dd[:, cs] * mixed
            dmix = dd[:, cs] * u[:, cs]
            dmix_a = dmix.astype(ACT)
            du_buf[:, cs] = du
            dw_ref[g] = dw_ref[g] + jnp.where(tril, _dot(dmix_a, gn_g, NT), 0.0)
            dsb_ref[:, g:g + 1] = dsb_ref[:, g:g + 1] + jnp.sum(dmix, axis=-1, keepdims=True)
            dgn_buf[:, cs] = _dot(w, dmix_a, TN)
        dz_ref[:, :D_CH] = (du_buf[...] * _gelu_grad(zr[:, :D_CH])).astype(ACT)
        dgn = dgn_buf[...]
        dg_ref[0:1, :] = dg_ref[0:1, :] + jnp.sum(dgn * xh, axis=0, keepdims=True)
        db_ref[0:1, :] = db_ref[0:1, :] + jnp.sum(dgn, axis=0, keepdims=True)
        dxh = dgn * lg_ref[...]
        dgate = rstd * (dxh - jnp.mean(dxh, axis=-1, keepdims=True)
                        - xh * jnp.mean(dxh * xh, axis=-1, keepdims=True))
        dz_ref[:, D_CH:] = (dgate * _gelu_grad(zr[:, D_CH:])).astype(ACT)

    vec = _fullspec((1, D_CH))
    acc = _fullspec((8, D_CH))
    return pl.pallas_call(
        body, grid=(s // BLOCK,), name=name,
        in_specs=[_rowspec(BLOCK, 2 * D_CH), _rowspec(BLOCK, D_CH, dcol), vec, vec,
                  _fullspec((D_GROUPS, BLOCK, BLOCK)), _fullspec((BLOCK, D_GROUPS))],
        out_specs=[_rowspec(BLOCK, 2 * D_CH), _fullspec((D_GROUPS, BLOCK, BLOCK)),
                   _fullspec((BLOCK, D_GROUPS)), acc, acc],
        out_shape=[jax.ShapeDtypeStruct((s, 2 * D_CH), ACT),
                   jax.ShapeDtypeStruct((D_GROUPS, BLOCK, BLOCK), F32),
                   jax.ShapeDtypeStruct((BLOCK, D_GROUPS), F32),
                   jax.ShapeDtypeStruct((8, D_CH), F32), jax.ShapeDtypeStruct((8, D_CH), F32)],
        scratch_shapes=[pltpu.VMEM((BLOCK, D_CH), F32)] * 2,
        compiler_params=_cp("arbitrary"))(z, dsrc, lg, lb, sw, sbt)


def _rms_bwd(dn, x, g):
    r = lax.rsqrt(jnp.mean(x * x, axis=-1, keepdims=True) + RMS_EPS)
    u = dn * g
    dx = r * u - x * (r * r * r) * jnp.mean(x * u, axis=-1, keepdims=True)
    return dx, dn * x * r


def final_loss(h, g, target, name):
    s = h.shape[0]
    tm = _tile(s, (256,))

    def body(h_ref, g_ref, t_ref, loss_ref, dh_ref, dg_ref):
        @pl.when(pl.program_id(0) == 0)
        def _():
            loss_ref[...] = jnp.zeros_like(loss_ref)
            dg_ref[...] = jnp.zeros_like(dg_ref)

        x = h_ref[...]
        r = lax.rsqrt(jnp.mean(x * x, axis=-1, keepdims=True) + RMS_EPS)
        diff = x * r * g_ref[...] - t_ref[...]
        part = jnp.sum(jnp.sum(diff * diff, axis=-1, keepdims=True), axis=0, keepdims=True)
        loss_ref[...] = loss_ref[...] + part * (0.5 / D_MODEL)
        dx, dgt = _rms_bwd(diff * (1.0 / D_MODEL), x, g_ref[...])
        dh_ref[...] = dx
        dg_ref[0:1, :] = dg_ref[0:1, :] + jnp.sum(dgt, axis=0, keepdims=True)

    spec = _rowspec(tm, D_MODEL)
    return pl.pallas_call(
        body, grid=(s // tm,), name=name,
        in_specs=[spec, _fullspec((1, D_MODEL)), spec],
        out_specs=[_fullspec((8, LANES)), spec, _fullspec((8, D_MODEL))],
        out_shape=[jax.ShapeDtypeStruct((8, LANES), F32), jax.ShapeDtypeStruct((s, D_MODEL), F32),
                   jax.ShapeDtypeStruct((8, D_MODEL), F32)],
        compiler_params=_cp("arbitrary"))(h, g, target)


def mm_nt(dy, w, name):
    s, n = dy.shape
    k = w.shape[0]
    tm = _tile(s, (512,))
    tk = _tile(k, (512,))

    def body(d_ref, w_ref, o_ref):
        o_ref[...] = _dot(d_ref[...].astype(ACT), w_ref[...], NT).astype(ACT)

    return pl.pallas_call(
        body, grid=(k // tk, s // tm), name=name,
        in_specs=[pl.BlockSpec((tm, n), lambda j, i: (i, 0)), pl.BlockSpec((tk, n), lambda j, i: (j, 0))],
        out_specs=pl.BlockSpec((tm, tk), lambda j, i: (i, j)),
        out_shape=jax.ShapeDtypeStruct((s, k), ACT),
        compiler_params=_cp("parallel", "parallel"))(dy, w)


def ffn_down_bwd(dh, wd, gate, up, name):
    s = dh.shape[0]
    f = wd.shape[0]
    tm = _tile(s, (512,))
    tf = _tile(f, (1408, 512, 256, 128))

    def body(d_ref, w_ref, g_ref, u_ref, dg_ref, du_ref):
        dact = _dot(d_ref[...].astype(ACT), w_ref[...], NT)
        g = g_ref[...].astype(F32)
        sg = _sigmoid(g)
        dg_ref[...] = (dact * u_ref[...].astype(F32) * (sg * (1.0 + g * (1.0 - sg)))).astype(ACT)
        du_ref[...] = (dact * g * sg).astype(ACT)

    tile = pl.BlockSpec((tm, tf), lambda j, i: (i, j))
    return pl.pallas_call(
        body, grid=(f // tf, s // tm), name=name,
        in_specs=[pl.BlockSpec((tm, D_MODEL), lambda j, i: (i, 0)),
                  pl.BlockSpec((tf, D_MODEL), lambda j, i: (j, 0)), tile, tile],
        out_specs=[tile, tile], out_shape=[jax.ShapeDtypeStruct((s, f), ACT)] * 2,
        compiler_params=_cp("parallel", "parallel"))(dh, wd, gate, up)


def mm_nt_rms(parts, h, g, dh, name):
    s = h.shape[0]
    tm = _tile(s, (256,))
    np_ = len(parts)

    def body(*refs):
        d_refs = refs[:np_]
        w_refs = refs[np_:2 * np_]
        h_ref, g_ref, dh_ref, o_ref, dg_ref = refs[2 * np_:]

        @pl.when(pl.program_id(0) == 0)
        def _():
            dg_ref[...] = jnp.zeros_like(dg_ref)

        dn = None
        for d_ref, w_ref in zip(d_refs, w_refs):
            t = _dot(d_ref[...], w_ref[...], NT)
            dn = t if dn is None else dn + t
        dx, dgt = _rms_bwd(dn, h_ref[...], g_ref[...])
        o_ref[...] = dh_ref[...] + dx
        dg_ref[0:1, :] = dg_ref[0:1, :] + jnp.sum(dgt, axis=0, keepdims=True)

    spec = _rowspec(tm, D_MODEL)
    return pl.pallas_call(
        body, grid=(s // tm,), name=name,
        in_specs=[_rowspec(tm, d.shape[1]) for d, _ in parts] + [_fullspec(w.shape) for _, w in parts]
        + [spec, _fullspec((1, D_MODEL)), spec],
        out_specs=[spec, _fullspec((8, D_MODEL))],
        out_shape=[jax.ShapeDtypeStruct((s, D_MODEL), F32), jax.ShapeDtypeStruct((8, D_MODEL), F32)],
        compiler_params=_cp("arbitrary"))(*[d for d, _ in parts], *[w for _, w in parts], h, g, dh)


def mm_tn(a, b, name):
    s, k = a.shape
    n = b.shape[1]
    tk = _tile(k, (512, 1408, 256, 128))
    tn = _tile(n, (1408, 1280, 1024, 896, 512, 256, 128))
    ts = _tile(s, (2048, 512) if b.dtype == ACT else (1024, 512))

    def body(a_ref, b_ref, o_ref):
        t = _dot(a_ref[...].astype(ACT), b_ref[...].astype(ACT), TN)

        @pl.when(pl.program_id(2) == 0)
        def _():
            o_ref[...] = t

        @pl.when(pl.program_id(2) > 0)
        def _():
            o_ref[...] = o_ref[...] + t

    return pl.pallas_call(
        body, grid=(k // tk, n // tn, s // ts), name=name,
        in_specs=[pl.BlockSpec((ts, tk), lambda i, j, t: (t, i)), pl.BlockSpec((ts, tn), lambda i, j, t: (t, j))],
        out_specs=pl.BlockSpec((tk, tn), lambda i, j, t: (i, j)),
        out_shape=jax.ShapeDtypeStruct((k, n), F32),
        compiler_params=_cp("parallel", "parallel", "arbitrary"))(a, b)


def _adamw_math(w, g, m, v):
    m = ADAM_B1 * m + (1.0 - ADAM_B1) * g
    v = ADAM_B2 * v + (1.0 - ADAM_B2) * (g * g)
    m_hat = m / (1.0 - ADAM_B1 ** ADAM_STEP)
    v_hat = v / (1.0 - ADAM_B2 ** ADAM_STEP)
    delta = -ADAM_LR * (m_hat / (jnp.sqrt(v_hat) + ADAM_EPS) + ADAM_WD * w)
    return delta, m, v


def sum_adamw(parts, w, m, v, name):
    r, c = w.shape
    tr = _tile(r, (256, 128, 64, 32, 16, 8))

    def body(p_ref, w_ref, m_ref, v_ref, g_ref, d_ref, mo_ref, vo_ref):
        g = p_ref[0].astype(F32)
        for i in range(1, N_DEV):
            g = g + p_ref[i].astype(F32)
        d, mm, vv = _adamw_math(w_ref[...], g, m_ref[...], v_ref[...])
        g_ref[...] = g
        d_ref[...] = d
        mo_ref[...] = mm
        vo_ref[...] = vv

    spec = _rowspec(tr, c)
    return pl.pallas_call(
        body, grid=(r // tr,), name=name,
        in_specs=[pl.BlockSpec((N_DEV, tr, c), lambda i: (0, i, 0))] + [spec] * 3, out_specs=[spec] * 4,
        out_shape=[jax.ShapeDtypeStruct((r, c), F32)] * 4,
        compiler_params=_cp("parallel"))(parts, w, m, v)


def adamw_small(ws, gs, ms, vs, name):
    k = len(ws)

    def body(*refs):
        for i in range(k):
            w_ref, g_ref, m_ref, v_ref = (refs[j * k + i] for j in range(4))
            d, mm, vv = _adamw_math(w_ref[...], g_ref[...], m_ref[...], v_ref[...])
            refs[4 * k + i][...] = d
            refs[5 * k + i][...] = mm
            refs[6 * k + i][...] = vv

    shapes = [jax.ShapeDtypeStruct(w.shape, F32) for w in ws]
    specs = [_fullspec(w.shape) for w in ws]
    out = pl.pallas_call(
        body, grid=(1,), name=name, in_specs=specs * 4, out_specs=specs * 3, out_shape=shapes * 3,
        compiler_params=_cp("arbitrary"))(*ws, *gs, *ms, *vs)
    return out[:k], out[k:2 * k], out[2 * k:]


def sum_slots(x, out_dtype, name):
    g, r, c = x.shape
    tr = r if x.size * x.dtype.itemsize <= SMALL_BLOCK_BYTES else _tile(r, (256, 128, 64, 32, 16, 8))

    def body(x_ref, o_ref):
        acc = x_ref[0].astype(F32)
        for i in range(1, g):
            acc = acc + x_ref[i].astype(F32)
        o_ref[...] = acc.astype(o_ref.dtype)

    return pl.pallas_call(
        body, grid=(r // tr,), name=name,
        in_specs=[pl.BlockSpec((g, tr, c), lambda i: (0, i, 0))], out_specs=_rowspec(tr, c),
        out_shape=jax.ShapeDtypeStruct((r, c), out_dtype),
        compiler_params=_cp("parallel"))(x)


HBM_SPEC = pl.BlockSpec(memory_space=pltpu.HBM)
SEM_SPEC = pl.BlockSpec(memory_space=pltpu.SEMAPHORE)
DATAFLOW = pltpu.SideEffectType.DATAFLOW_SIDE_EFFECTING


def _my_rank():
    return 4 * lax.axis_index("x") + 2 * lax.axis_index("y") + lax.axis_index("c")


def _exchange_copies(x_refs, land_refs, send, recv, a2a):
    pos = [lax.axis_index(a) for a in AXES]
    me = _my_rank()
    copies = []
    for x_ref, land_ref, s_ref, r_ref in zip(x_refs, land_refs, send, recv):
        for k in range(N_DEV - 1):
            bits = ((k + 1) >> 2 & 1, (k + 1) >> 1 & 1, (k + 1) & 1)
            peer = tuple(1 - p if b else p for p, b in zip(pos, bits))
            prank = 4 * peer[0] + 2 * peer[1] + peer[2]
            copies.append(pltpu.make_async_remote_copy(
                src_ref=x_ref.at[prank] if a2a else x_ref, dst_ref=land_ref.at[me],
                send_sem=s_ref.at[k], recv_sem=r_ref.at[k], device_id=peer, device_id_type=MESH))
    return copies


def exchange_start(xs, a2a, name):
    n = len(xs)
    me = _my_rank()
    lands = []
    for x in xs:
        own = lax.dynamic_index_in_dim(x, me, 0, keepdims=True) if a2a else x[None]
        shape = x.shape if a2a else (N_DEV,) + x.shape
        lands.append(lax.dynamic_update_slice(lax.empty(shape, x.dtype), own, (me,) + (0,) * (len(shape) - 1)))

    def body(*refs):
        x_refs, land_refs = refs[:n], refs[n:2 * n]
        outs = refs[2 * n:]
        for cp in _exchange_copies(x_refs, land_refs, outs[:n], outs[n:2 * n], a2a):
            cp.start()
        token = outs[4 * n]
        token[...] = jnp.zeros_like(token)

    sems = [pltpu.SemaphoreType.DMA((N_DEV - 1,))] * n
    out = pl.pallas_call(
        body, name=name,
        out_shape=tuple(sems + sems + [pltpu.HBM(x.shape, x.dtype) for x in xs]
                        + [pltpu.HBM(l.shape, l.dtype) for l in lands] + [jax.ShapeDtypeStruct((8, LANES), F32)]),
        in_specs=[HBM_SPEC] * (2 * n),
        out_specs=tuple([SEM_SPEC] * (2 * n) + [HBM_SPEC] * (2 * n) + [pl.BlockSpec(memory_space=pltpu.VMEM)]),
        input_output_aliases={i: 2 * n + i for i in range(2 * n)},
        compiler_params=pltpu.CompilerParams(has_side_effects=DATAFLOW),
    )(*[pltpu.with_memory_space_constraint(a, pltpu.HBM) for a in list(xs) + lands])
    return (out[:n], out[n:2 * n], out[2 * n:3 * n], out[3 * n:4 * n]), out[4 * n]


def exchange_wait(handles, after, a2a, name):
    send, recv, x_thru, land_thru = handles
    n = len(x_thru)

    def body(*refs):
        x_refs, land_refs = refs[:n], refs[n:2 * n]
        s_refs, r_refs = refs[2 * n:3 * n], refs[3 * n:4 * n]
        for cp in _exchange_copies(x_refs, land_refs, s_refs, r_refs, a2a):
            cp.wait_send()
            cp.wait_recv()

    out = pl.pallas_call(
        body, name=name,
        out_shape=tuple([pltpu.HBM(a.shape, a.dtype) for a in list(x_thru) + list(land_thru)]),
        in_specs=[HBM_SPEC] * (2 * n) + [SEM_SPEC] * (2 * n) + [pl.BlockSpec(memory_space=pl.ANY)],
        out_specs=tuple([HBM_SPEC] * (2 * n)),
        input_output_aliases={i: i for i in range(2 * n)},
        compiler_params=pltpu.CompilerParams(has_side_effects=DATAFLOW),
    )(*x_thru, *land_thru, *send, *recv, after)
    return out[n:2 * n]


def _local_step(x, target, weight, emit, P):
    s = x.shape[0]
    tabs = _rope_tables(s)
    sinkb = jnp.broadcast_to(P["ev_sinks"].reshape(N_HEADS, 1), (N_HEADS, LANES))
    sbt = P["od_spatial_b"].reshape(D_GROUPS, BLOCK).T
    sw = P["od_spatial_w"].reshape(D_GROUPS, BLOCK, BLOCK)
    fg = P["ffn_norm_g"]
    latest = [None]

    def out(name, layer, grad):
        tok = emit(name, layer, grad)
        if tok is not None:
            latest[0] = tok

    def dep(a):
        return a if latest[0] is None else a + latest[0][0:1, 0:1]

    n0 = rmsnorm(x, P["ev_norm_g"], "rms_in")
    qk_e, v_e, ga, gb = inproj(n0, weight("ev_w_in", 0, n0), tabs, ATT_W + 128,
                               (ATT_W + 128, 128, CONV_CH, CONV_CH), "ev_inproj")
    a_e, lse_e = attn_fwd(qk_e, v_e, d=1, hkv=A_KV_HEADS, max_dist=BLOCK - 1, sink=sinkb, out_dtype=ACT,
                          name="ev_attn")
    c_act, c1 = conv_fwd(ga, gb, P["ev_conv_w"], P["ev_conv_b"], P["ev_conv_ln_g"], P["ev_conv_ln_b"], "ev_conv")
    h1, n1 = mm_res([a_e, c_act], weight("ev_w_out", 0, c_act), x, fg[0:1], "ev_outproj")
    gate0, up0, act0 = ffn_up(n1, weight("ffn_w_gate", 0, n1), weight("ffn_w_up", 0, n1), "ffn0_up")
    h2, n2 = mm_res([act0], weight("ffn_w_down", 0, act0), h1, P["od_norm_g"], "ffn0_down")
    qk_o, v_o, z = inproj(n2, weight("od_w_in", 0, n2), tabs, 2 * ATT_W, (2 * ATT_W, ATT_W, 2 * D_CH), "od_inproj")
    outs, lses = [], []
    for window, dil in DILATED:
        assert window // dil == BLOCK
        o_r, l_r = attn_fwd(qk_o, v_o, d=dil, hkv=N_HEADS, max_dist=BLOCK, sink=None, out_dtype=F32,
                            name="od_attn_d%d" % dil)
        outs.append(o_r)
        lses.append(l_r)
    c_out, lse_o = combine_fwd(outs, lses, "od_combine")
    d_out = gmlp_fwd(z, P["od_sgu_ln_g"], P["od_sgu_ln_b"], sw, sbt, "od_gmlp")
    h3, n3 = mm_res([c_out, d_out], weight("od_w_out", 0, d_out), h2, fg[1:2], "od_outproj")
    gate1, up1, act1 = ffn_up(n3, weight("ffn_w_gate", 1, n3), weight("ffn_w_up", 1, n3), "ffn1_up")
    h4, _ = mm_res([act1], weight("ffn_w_down", 1, act1), h3, None, "ffn1_down")
    loss_part, dh4, dg_final = final_loss(h4, P["final_norm_g"], target, "loss_head")

    def ffn_bwd(layer, dh_out, h_in, n_in, gate, up, act):
        wg, wu, wd = (weight(n, layer, dh_out) for n in ("ffn_w_gate", "ffn_w_up", "ffn_w_down"))
        tag = "ffn%d" % layer
        dgate, dup = ffn_down_bwd(dh_out, wd, gate, up, tag + "_down_bwd")
        g_wd = mm_tn(act, dh_out, tag + "_dwd")
        dh_in, dgn = mm_nt_rms([(dgate, wg), (dup, wu)], h_in, dep(fg[layer:layer + 1]), dh_out, tag + "_up_bwd")
        out("ffn_w_down", layer, g_wd)
        out("ffn_w_gate", layer, mm_tn(n_in, dgate, tag + "_dwg"))
        out("ffn_w_up", layer, mm_tn(n_in, dup, tag + "_dwu"))
        return dh_in, dgn[0:1]

    dh3, dgn_f1 = ffn_bwd(1, dh4, h3, n3, gate1, up1, act1)

    dcd = mm_nt(dh3, weight("od_w_out", 0, dh3), "od_outproj_bwd")
    dz, g_sw, g_sbt, g_slg, g_slb = gmlp_bwd(z, dcd, 1, dep(P["od_sgu_ln_g"]), P["od_sgu_ln_b"], sw, sbt,
                                             "od_gmlp_bwd")
    out("od_spatial", 0, jnp.concatenate([g_sw.reshape(D_GROUPS * BLOCK, BLOCK), g_sbt.T], axis=0))
    out("od_w_out", 0, jnp.concatenate([mm_tn(c_out, dh3, "od_dwo_c"), mm_tn(d_out, dh3, "od_dwo_d")], axis=0))
    dqkv = [attn_bwd(qk_o, v_o, dcd if dil == 1 else dcd[:, :ATT_W], 0, c_out, lse_o, d=dil, hkv=N_HEADS,
                     max_dist=BLOCK, sink=None, out_dtype=ACT, name="od_attn_bwd_d%d" % dil)
            for window, dil in DILATED]
    dproj_o = assemble([([b[0] for b in dqkv], True), ([b[1] for b in dqkv], True), ([b[2] for b in dqkv], False),
                        ([dz], False)], tabs, "od_dproj")
    dh2, dgn_od = mm_nt_rms([(dproj_o, weight("od_w_in", 0, dproj_o))], h2, dep(P["od_norm_g"]), dh3,
                            "od_inproj_bwd")
    out("od_w_in", 0, mm_tn(n2, dproj_o, "od_dwi"))

    dh1, dgn_f0 = ffn_bwd(0, dh2, h1, n1, gate0, up0, act0)

    dac = mm_nt(dh1, weight("ev_w_out", 0, dh1), "ev_outproj_bwd")
    dc1, g_clg, g_clb = conv_bwd_ln(c1, dac, 1, dep(P["ev_conv_ln_g"]), P["ev_conv_ln_b"], "ev_conv_bwd_ln")
    out("ev_w_out", 0, jnp.concatenate([mm_tn(a_e, dh1, "ev_dwo_a"), mm_tn(c_act, dh1, "ev_dwo_c")], axis=0))
    dga, dgb, g_cw, g_cb = conv_bwd_conv(dc1, ga, gb, P["ev_conv_w"], "ev_conv_bwd")
    dq, dk, dv, dsink = attn_bwd(qk_e, v_e, dac, 0, a_e, lse_e, d=1, hkv=A_KV_HEADS, max_dist=BLOCK - 1,
                                 sink=sinkb, out_dtype=F32, name="ev_attn_bwd")
    dproj_e = assemble([([dq], True), ([dk], True), ([dv], False), ([dga], False), ([dgb], False)], tabs,
                       "ev_dproj")
    out("ev_w_in", 0, mm_tn(n0, dproj_e, "ev_dwi"))
    dx, dgn_ev = mm_nt_rms([(dproj_e, weight("ev_w_in", 0, dproj_e))], x, dep(P["ev_norm_g"]), dh1,
                           "ev_inproj_bwd")

    small = {
        "ev_norm_g": dgn_ev[0:1],
        "ev_sinks": dsink[:, 0:1].reshape(1, N_HEADS),
        "ev_conv_w": g_cw[:CONV_WIDTH],
        "ev_conv_b": g_cb[0:1],
        "ev_conv_ln_g": g_clg[0:1],
        "ev_conv_ln_b": g_clb[0:1],
        "od_norm_g": dgn_od[0:1],
        "od_sgu_ln_g": g_slg[0:1],
        "od_sgu_ln_b": g_slb[0:1],
        "od_spatial_w": g_sw.reshape(D_GROUPS * BLOCK, BLOCK),
        "od_spatial_b": g_sbt.T,
        "ffn_norm_g": jnp.concatenate([dgn_f0, dgn_f1], axis=0),
        "final_norm_g": dg_final[0:1],
    }
    return loss_part, dx, small


BIG = ("ev_w_in", "ev_w_out", "od_w_in", "od_w_out", "ffn_w_gate", "ffn_w_up", "ffn_w_down")
COL_SHARDED = ("ev_w_in", "od_w_in", "ffn_w_gate", "ffn_w_up")
GATHER_GROUPS = (
    (("ev_w_in", 0),),
    (("ev_w_out", 0), ("ffn_w_gate", 0), ("ffn_w_up", 0), ("ffn_w_down", 0)),
    (("od_w_in", 0),),
    (("od_w_out", 0), ("ffn_w_gate", 1), ("ffn_w_up", 1), ("ffn_w_down", 1)),
)
REDUCE_GROUPS = (
    (("ffn_w_down", 1), ("ffn_w_gate", 1), ("ffn_w_up", 1)),
    (("od_w_out", 0),),
    (("od_w_in", 0),),
    (("ffn_w_down", 0), ("ffn_w_gate", 0), ("ffn_w_up", 0)),
    (("ev_w_out", 0),),
    (("ev_w_in", 0),),
)


def _unshard(name, g):
    if name in COL_SHARDED:
        return jnp.moveaxis(g, 0, 1).reshape(g.shape[1], N_DEV * g.shape[2])
    return g.reshape(N_DEV * g.shape[1], g.shape[2])


def _shard_slots(name, full):
    r, c = full.shape
    if name in COL_SHARDED:
        return jnp.moveaxis(full.reshape(r, N_DEV, c // N_DEV), 1, 0)
    return full.reshape(N_DEV, r // N_DEV, c)


def kernel(x, ev_norm_g, ev_w_in, ev_sinks, ev_conv_w, ev_conv_b, ev_conv_ln_g, ev_conv_ln_b, ev_w_out, od_norm_g, od_w_in, od_sgu_ln_g, od_sgu_ln_b, od_spatial_w, od_spatial_b, od_w_out, ffn_norm_g, ffn_w_gate, ffn_w_up, ffn_w_down, final_norm_g, loss_target, m_ev_norm_g, m_ev_w_in, m_ev_sinks, m_ev_conv_w, m_ev_conv_b, m_ev_conv_ln_g, m_ev_conv_ln_b, m_ev_w_out, m_od_norm_g, m_od_w_in, m_od_sgu_ln_g, m_od_sgu_ln_b, m_od_spatial_w, m_od_spatial_b, m_od_w_out, m_ffn_norm_g, m_ffn_w_gate, m_ffn_w_up, m_ffn_w_down, m_final_norm_g, v_ev_norm_g, v_ev_w_in, v_ev_sinks, v_ev_conv_w, v_ev_conv_b, v_ev_conv_ln_g, v_ev_conv_ln_b, v_ev_w_out, v_od_norm_g, v_od_w_in, v_od_sgu_ln_g, v_od_sgu_ln_b, v_od_spatial_w, v_od_spatial_b, v_od_w_out, v_ffn_norm_g, v_ffn_w_gate, v_ffn_w_up, v_ffn_w_down, v_final_norm_g):
    names = ["ev_norm_g", "ev_w_in", "ev_sinks", "ev_conv_w", "ev_conv_b", "ev_conv_ln_g", "ev_conv_ln_b", "ev_w_out",
             "od_norm_g", "od_w_in", "od_sgu_ln_g", "od_sgu_ln_b", "od_spatial_w", "od_spatial_b", "od_w_out",
             "ffn_norm_g", "ffn_w_gate", "ffn_w_up", "ffn_w_down", "final_norm_g"]
    wts = dict(zip(names, [ev_norm_g, ev_w_in, ev_sinks, ev_conv_w, ev_conv_b, ev_conv_ln_g, ev_conv_ln_b, ev_w_out,
                           od_norm_g, od_w_in, od_sgu_ln_g, od_sgu_ln_b, od_spatial_w, od_spatial_b, od_w_out,
                           ffn_norm_g, ffn_w_gate, ffn_w_up, ffn_w_down, final_norm_g]))
    mom = dict(zip(names, [m_ev_norm_g, m_ev_w_in, m_ev_sinks, m_ev_conv_w, m_ev_conv_b, m_ev_conv_ln_g, m_ev_conv_ln_b,
                           m_ev_w_out, m_od_norm_g, m_od_w_in, m_od_sgu_ln_g, m_od_sgu_ln_b, m_od_spatial_w,
                           m_od_spatial_b, m_od_w_out, m_ffn_norm_g, m_ffn_w_gate, m_ffn_w_up, m_ffn_w_down,
                           m_final_norm_g]))
    vel = dict(zip(names, [v_ev_norm_g, v_ev_w_in, v_ev_sinks, v_ev_conv_w, v_ev_conv_b, v_ev_conv_ln_g, v_ev_conv_ln_b,
                           v_ev_w_out, v_od_norm_g, v_od_w_in, v_od_sgu_ln_g, v_od_sgu_ln_b, v_od_spatial_w,
                           v_od_spatial_b, v_od_w_out, v_ffn_norm_g, v_ffn_w_gate, v_ffn_w_up, v_ffn_w_down,
                           v_final_norm_g]))
    me = _my_rank()

    sp = jnp.zeros((40, LANES), F32)
    sp = sp.at[0:CONV_WIDTH, 0:64].set(ev_conv_w[0])
    sp = sp.at[32, :].set(od_norm_g[0])
    sp = sp.at[33, 0:64].set(od_sgu_ln_g[0])
    sp = sp.at[34, 0:64].set(od_sgu_ln_b[0])

    gather_order = [k for grp in GATHER_GROUPS for k in grp]
    ag_handles, ag_token = exchange_start([sp] + [wts[n][l].astype(ACT) for n, l in gather_order], False, "ag_start")
    full_w = {}
    P = {
        "ev_norm_g": ev_norm_g + ag_token[0:1, 0:1], "ev_sinks": ev_sinks, "ev_conv_b": ev_conv_b,
        "ev_conv_ln_g": ev_conv_ln_g, "ev_conv_ln_b": ev_conv_ln_b,
        "od_spatial_w": od_spatial_w, "od_spatial_b": od_spatial_b, "ffn_norm_g": ffn_norm_g,
        "final_norm_g": final_norm_g.reshape(1, D_MODEL),
    }

    def weight(name, layer, after):
        if (name, layer) not in full_w:
            gi = [i for i, grp in enumerate(GATHER_GROUPS) if (name, layer) in grp][0]
            idx = [1 + gather_order.index(k) for k in GATHER_GROUPS[gi]]
            if gi == 0:
                idx = [0] + idx
            lands = exchange_wait(tuple([h[i] for i in idx] for h in ag_handles), after, False, "ag_wait%d" % gi)
            if gi == 0:
                spg, lands = lands[0], lands[1:]
                P["ev_conv_w"] = jnp.moveaxis(spg[:, 0:CONV_WIDTH, 0:64], 0, 1).reshape(CONV_WIDTH, CONV_CH)
                P["od_norm_g"] = spg[:, 32, :].reshape(1, D_MODEL)
                P["od_sgu_ln_g"] = spg[:, 33, 0:64].reshape(1, D_CH)
                P["od_sgu_ln_b"] = spg[:, 34, 0:64].reshape(1, D_CH)
            for k, land in zip(GATHER_GROUPS[gi], lands):
                full_w[k] = _unshard(k[0], land)
        return full_w[(name, layer)]

    pending, rs_started, spatial = {}, [], []

    def emit(name, layer, grad):
        if name == "od_spatial":
            handles, token = exchange_start([grad], False, "ar_start_b")
            spatial.append(handles)
            return token
        pending[(name, layer)] = grad
        for gi, grp in enumerate(REDUCE_GROUPS):
            if (name, layer) in grp and all(k in pending for k in grp):
                handles, token = exchange_start([_shard_slots(k[0], pending[k]).astype(ACT) for k in grp], True,
                                                "rs_start%d" % gi)
                rs_started.append((gi, handles))
                return token
        return None

    loss_part, dx, small = _local_step(x[0], loss_target[0], weight, emit, P)
    loss = lax.psum(loss_part[0, 0], AXES)

    wide = ["ev_norm_g", "ev_sinks", "ev_conv_w", "ev_conv_b", "ev_conv_ln_g", "ev_conv_ln_b", "od_norm_g",
            "od_sgu_ln_g", "od_sgu_ln_b", "ffn_norm_g", "final_norm_g"]
    blk_a = jnp.concatenate(
        [jnp.pad(small[n], ((0, 0), (0, D_MODEL - small[n].shape[1]))) for n in wide], axis=0)
    blk_a = jnp.pad(blk_a, ((0, 48 - blk_a.shape[0]), (0, 0)))
    ar_a, ar_token = exchange_start([blk_a], False, "ar_start_a")

    per_layer = {}
    for gi, handles in rs_started:
        lands = exchange_wait(handles, ar_token, True, "rs_wait%d" % gi)
        for (n, l), land in zip(REDUCE_GROUPS[gi], lands):
            per_layer[(n, l)] = sum_adamw(land, wts[n][l], mom[n][l], vel[n][l], "adamw_%s%d" % (n, l))
    out_g, out_d, out_m, out_v = {}, {}, {}, {}
    for n in BIG:
        for j, dst in enumerate((out_g, out_d, out_m, out_v)):
            dst[n] = jnp.stack([per_layer[(n, l)][j] for l in range(wts[n].shape[0])])

    last = per_layer[REDUCE_GROUPS[-1][-1]][0]
    sum_b = sum_slots(exchange_wait(spatial[0], last, False, "ar_wait_b")[0], F32, "ar_sum_b")
    sum_a = sum_slots(exchange_wait(ar_a, last, False, "ar_wait_a")[0], F32, "ar_sum_a")
    full = {}
    off = 0
    for n in wide:
        r_, c_ = small[n].shape
        full[n] = sum_a[off:off + r_, 0:c_]
        off += r_
    full["od_spatial_w"] = sum_b[0:D_GROUPS * BLOCK]
    full["od_spatial_b"] = sum_b[D_GROUPS * BLOCK:D_GROUPS * BLOCK + D_GROUPS]
    full["ev_conv_w"] = lax.dynamic_slice_in_dim(full["ev_conv_w"], me * 64, 64, axis=1)
    full["od_norm_g"] = lax.dynamic_slice_in_dim(full["od_norm_g"], me * 128, 128, axis=1)
    full["od_sgu_ln_g"] = lax.dynamic_slice_in_dim(full["od_sgu_ln_g"], me * 64, 64, axis=1)
    full["od_sgu_ln_b"] = lax.dynamic_slice_in_dim(full["od_sgu_ln_b"], me * 64, 64, axis=1)

    small_names = [n for n in names if n not in BIG]
    view = {n: ((-1, wts[n].shape[-1]) if wts[n].ndim > 1 else (1, -1)) for n in small_names}
    ds, ms, vs = adamw_small([wts[n].reshape(view[n]) for n in small_names],
                             [full[n].reshape(view[n]) for n in small_names],
                             [mom[n].reshape(view[n]) for n in small_names],
                             [vel[n].reshape(view[n]) for n in small_names], "adamw_small")
    for i, n in enumerate(small_names):
        shp = wts[n].shape
        out_g[n], out_d[n], out_m[n], out_v[n] = (full[n].reshape(shp), ds[i].reshape(shp), ms[i].reshape(shp),
                                                  vs[i].reshape(shp))

    return (loss, dx[None], *[out_g[n] for n in names], *[out_d[n] for n in names],
            *[out_m[n] for n in names], *[out_v[n] for n in names])
```

```python
import functools
import math

import jax
import jax.numpy as jnp
from jax import lax
from jax.experimental import pallas as pl
from jax.experimental.pallas import tpu as pltpu

F32 = jnp.float32
ACT = jnp.bfloat16

D_MODEL = 1024
HEAD_DIM = 64
N_HEADS = 8
ATT_W = N_HEADS * HEAD_DIM
A_KV_HEADS = 2
CONV_CH = 512
CONV_WIDTH = 31
CONV_HALO = 32
D_CH = 512
D_GROUPS = 8
BLOCK = 128
D_FF = 2816
ROT_DIM = 16
ROPE_THETA = 500000.0
RMS_EPS = 1e-6
LN_EPS = 1e-5
DILATED = ((128, 1), (512, 4), (2048, 16))
NEG = -1e30
LANES = 128

ADAM_LR = 0.001
ADAM_B1 = 0.9
ADAM_B2 = 0.999
ADAM_EPS = 1e-08
ADAM_WD = 0.01
ADAM_STEP = 10

V7X_VMEM_LIMIT = 56 * 1024 * 1024
SMALL_BLOCK_BYTES = 6 * 1024 * 1024
N_DEV = 8

NN = (((1,), (0,)), ((), ()))
NT = (((1,), (1,)), ((), ()))
TN = (((0,), (0,)), ((), ()))
MESH = pl.DeviceIdType.MESH
AXES = ("x", "y", "c")


def _dot(a, b, dims):
    return lax.dot_general(a, b, dims, preferred_element_type=F32)


def _cp(*sem):
    return pltpu.CompilerParams(dimension_semantics=sem if sem else None,
                                vmem_limit_bytes=V7X_VMEM_LIMIT)


def _tile(n, prefs):
    for p in prefs:
        if n % p == 0:
            return p
    return n


def _sigmoid(x):
    return 1.0 / (1.0 + jnp.exp(-x))


def _rowspec(tm, w, col=0):
    return pl.BlockSpec((tm, w), lambda i, col=col: (i, col))


def _fullspec(shape):
    nd = len(shape)
    return pl.BlockSpec(shape, lambda *a, nd=nd: (0,) * nd)


def _rope_tables(seq):
    half = ROT_DIM // 2
    inv_freq = ROPE_THETA ** (-jnp.arange(half, dtype=F32) * (2.0 / ROT_DIM))
    ang = jnp.arange(seq, dtype=jnp.int32).astype(F32)[:, None] * inv_freq[None, :]
    cos, sin = jnp.cos(ang), jnp.sin(ang)
    lane = jnp.arange(LANES)
    jm = lane % HEAD_DIM
    idx = jm % half
    c = jnp.where(jm[None, :] < ROT_DIM, cos[:, idx], 1.0)
    sa = jnp.where(jm[None, :] < half, -sin[:, idx], 0.0)
    sb = jnp.where((jm[None, :] >= half) & (jm[None, :] < ROT_DIM), sin[:, idx], 0.0)
    return c.astype(F32), sa.astype(F32), sb.astype(F32)


def _rope(x, c, sa, sb):
    return x * c + pltpu.roll(x, LANES - 8, 1) * sa + pltpu.roll(x, 8, 1) * sb


def _rope_t(d, c, sa, sb):
    return d * c + pltpu.roll(d * sa, 8, 1) + pltpu.roll(d * sb, LANES - 8, 1)


def rmsnorm(h, g, name):
    s = h.shape[0]
    tm = _tile(s, (512,))

    def body(h_ref, g_ref, o_ref):
        x = h_ref[...]
        r = lax.rsqrt(jnp.mean(x * x, axis=-1, keepdims=True) + RMS_EPS)
        o_ref[...] = (x * r * g_ref[...]).astype(o_ref.dtype)

    return pl.pallas_call(
        body, grid=(s // tm,), name=name,
        in_specs=[_rowspec(tm, D_MODEL), _fullspec((1, D_MODEL))],
        out_specs=_rowspec(tm, D_MODEL),
        out_shape=jax.ShapeDtypeStruct((s, D_MODEL), ACT),
        compiler_params=_cp("parallel"))(h, g)


def inproj(n, w, tabs, nqk, splits, name):
    s = n.shape[0]
    ntot = w.shape[1]
    assert sum(splits) == ntot and splits[0] == nqk
    tm = _tile(s, (256,))

    def body(n_ref, w_ref, c_ref, sa_ref, sb_ref, *outs):
        res = _dot(n_ref[...], w_ref[...], NN)
        c, sa, sb = c_ref[...], sa_ref[...], sb_ref[...]
        for g in range(nqk // LANES):
            x = res[:, g * LANES:(g + 1) * LANES]
            outs[0][:, g * LANES:(g + 1) * LANES] = _rope(x, c, sa, sb).astype(ACT)
        off = nqk
        for o_ref, wd in zip(outs[1:], splits[1:]):
            o_ref[...] = res[:, off:off + wd].astype(ACT)
            off += wd

    return pl.pallas_call(
        body, grid=(s // tm,), name=name,
        in_specs=[_rowspec(tm, D_MODEL), _fullspec((D_MODEL, ntot))] + [_rowspec(tm, LANES)] * 3,
        out_specs=[_rowspec(tm, wd) for wd in splits],
        out_shape=[jax.ShapeDtypeStruct((s, wd), ACT) for wd in splits],
        compiler_params=_cp("parallel"))(n, w, *tabs)


def ffn_up(n, wg, wu, name):
    s = n.shape[0]
    f = wg.shape[1]
    tm = _tile(s, (512,))
    tf = _tile(f, (1408, 512, 256, 128))

    def body(n_ref, wg_ref, wu_ref, g_ref, u_ref, a_ref):
        a = n_ref[...]
        g = _dot(a, wg_ref[...], NN)
        u = _dot(a, wu_ref[...], NN)
        g_ref[...] = g.astype(ACT)
        u_ref[...] = u.astype(ACT)
        a_ref[...] = (g * _sigmoid(g) * u).astype(ACT)

    wspec = pl.BlockSpec((D_MODEL, tf), lambda j, i: (0, j))
    ospec = pl.BlockSpec((tm, tf), lambda j, i: (i, j))
    return pl.pallas_call(
        body, grid=(f // tf, s // tm), name=name,
        in_specs=[pl.BlockSpec((tm, D_MODEL), lambda j, i: (i, 0)), wspec, wspec],
        out_specs=[ospec] * 3,
        out_shape=[jax.ShapeDtypeStruct((s, f), ACT)] * 3,
        compiler_params=_cp("parallel", "parallel"))(n, wg, wu)


def mm_res(parts, w, h, gnext, name):
    s = h.shape[0]
    tm = _tile(s, (256,))
    widths = [p.shape[1] for p in parts]
    assert sum(widths) == w.shape[0]
    np_ = len(parts)

    def body(*refs):
        p_refs = refs[:np_]
        w_ref, h_ref = refs[np_], refs[np_ + 1]
        rest = refs[np_ + 2:]
        acc = h_ref[...]
        off = 0
        for p_ref, wd in zip(p_refs, widths):
            acc = acc + _dot(p_ref[...], w_ref[off:off + wd, :], NN)
            off += wd
        if gnext is None:
            rest[0][...] = acc
        else:
            g_ref, ho_ref, no_ref = rest
            ho_ref[...] = acc
            r = lax.rsqrt(jnp.mean(acc * acc, axis=-1, keepdims=True) + RMS_EPS)
            no_ref[...] = (acc * r * g_ref[...]).astype(ACT)

    in_specs = [_rowspec(tm, wd) for wd in widths] + [_fullspec(w.shape), _rowspec(tm, D_MODEL)]
    args = list(parts) + [w, h]
    out_specs = [_rowspec(tm, D_MODEL)]
    out_shape = [jax.ShapeDtypeStruct((s, D_MODEL), F32)]
    if gnext is not None:
        in_specs.append(_fullspec((1, D_MODEL)))
        args.append(gnext)
        out_specs.append(_rowspec(tm, D_MODEL))
        out_shape.append(jax.ShapeDtypeStruct((s, D_MODEL), ACT))
    out = pl.pallas_call(
        body, grid=(s // tm,), name=name, in_specs=in_specs, out_specs=out_specs,
        out_shape=out_shape, compiler_params=_cp("parallel"))(*args)
    return (out[0], None) if gnext is None else (out[0], out[1])


def _band_mask(n, max_dist):
    qi = lax.broadcasted_iota(jnp.int32, (BLOCK, 2 * BLOCK), 0)
    kj = lax.broadcasted_iota(jnp.int32, (BLOCK, 2 * BLOCK), 1)
    dist = qi + BLOCK - kj
    valid = jnp.logical_and(dist >= 0, dist <= max_dist)
    return jnp.logical_and(valid, jnp.logical_or(kj >= BLOCK, n > 0))


def _fold(a, d):
    return a.reshape(a.shape[0] // d, d * a.shape[1])


def attn_fwd(qk, v, *, d, hkv, max_dist, sink, out_dtype, name):
    s = qk.shape[0]
    kvw = hkv * HEAD_DIM
    wqk = ATT_W + kvw
    assert qk.shape[1] == wqk and (d == 1 or (wqk % ATT_W == 0 and wqk % kvw == 0))
    nb = s // d // BLOCK
    grp = N_HEADS // hkv
    qpb, kpb, koff = wqk // ATT_W, wqk // kvw, ATT_W // kvw

    def body(*refs):
        if sink is None:
            q_ref, kc_ref, kp_ref, vc_ref, vp_ref, o_ref, l_ref, o_buf = refs
        else:
            q_ref, kc_ref, kp_ref, vc_ref, vp_ref, s_ref, o_ref, l_ref, o_buf = refs
        n = pl.program_id(1)
        valid = _band_mask(n, max_dist)
        qsl = [slice(h * HEAD_DIM, (h + 1) * HEAD_DIM) for h in range(N_HEADS)]
        ksl = [slice((h // grp) * HEAD_DIM, (h // grp + 1) * HEAD_DIM) for h in range(N_HEADS)]
        kk = jnp.concatenate([kp_ref[...], kc_ref[...]], axis=0)
        vv = jnp.concatenate([vp_ref[...], vc_ref[...]], axis=0)
        scores = []
        for h in range(N_HEADS):
            q = q_ref[:, qsl[h]] * 0.125
            scores.append(_dot(q, kk[:, ksl[h]], NT))
        probs = []
        lane = lax.broadcasted_iota(jnp.int32, (BLOCK, LANES), 1)
        lse_tile = jnp.zeros((BLOCK, LANES), F32)
        for h in range(N_HEADS):
            sc = jnp.where(valid, scores[h], NEG)
            m = jnp.max(sc, axis=-1, keepdims=True)
            if sink is not None:
                sk = s_ref[h:h + 1, 0:1]
                m = jnp.maximum(m, sk)
            p = jnp.exp(sc - m)
            l = jnp.sum(p, axis=-1, keepdims=True)
            if sink is not None:
                l = l + jnp.exp(sk - m)
            lse_tile = jnp.where(lane == h, m + jnp.log(l), lse_tile)
            probs.append((p.astype(ACT), 1.0 / l))
        l_ref[...] = lse_tile
        for h in range(N_HEADS):
            p, rl = probs[h]
            o_buf[:, qsl[h]] = _dot(p, vv[:, ksl[h]], NN) * rl
        o_ref[...] = o_buf[...].astype(o_ref.dtype)

    prev = lambda n: jnp.maximum(n - 1, 0)
    in_specs = [
        pl.BlockSpec((BLOCK, ATT_W), lambda r, n: (n, r * qpb)),
        pl.BlockSpec((BLOCK, kvw), lambda r, n: (n, r * kpb + koff)),
        pl.BlockSpec((BLOCK, kvw), lambda r, n: (prev(n), r * kpb + koff)),
        pl.BlockSpec((BLOCK, kvw), lambda r, n: (n, r)),
        pl.BlockSpec((BLOCK, kvw), lambda r, n: (prev(n), r)),
    ]
    qkf, vf = _fold(qk, d), _fold(v, d)
    args = [qkf, qkf, qkf, vf, vf]
    if sink is not None:
        in_specs.append(_fullspec((N_HEADS, LANES)))
        args.append(sink)
    ospec = pl.BlockSpec((BLOCK, ATT_W), lambda r, n: (n, r))
    lspec = pl.BlockSpec((BLOCK, LANES), lambda r, n: (n, r))
    o, lse = pl.pallas_call(
        body, grid=(d, nb), name=name, in_specs=in_specs, out_specs=[ospec, lspec],
        out_shape=[jax.ShapeDtypeStruct((s // d, d * ATT_W), out_dtype),
                   jax.ShapeDtypeStruct((s // d, d * LANES), F32)],
        scratch_shapes=[pltpu.VMEM((BLOCK, ATT_W), F32)],
        compiler_params=_cp("parallel", "parallel"))(*args)
    return o.reshape(s, ATT_W), lse.reshape(s, LANES)


def attn_bwd(qk, v, do_src, do_col, o, lse, *, d, hkv, max_dist, sink, out_dtype, name):
    s = qk.shape[0]
    kvw = hkv * HEAD_DIM
    wqk = ATT_W + kvw
    nb = s // d // BLOCK
    grp = N_HEADS // hkv
    qpb, kpb, koff = wqk // ATT_W, wqk // kvw, ATT_W // kvw
    dob = do_src.shape[1] // ATT_W
    has_sink = sink is not None

    def body(*refs):
        refs = list(refs)
        q_ref, kc_ref, kp_ref, vc_ref, vp_ref, do_ref, o_ref, l_ref = refs[:8]
        pos = 8
        if has_sink:
            s_ref = refs[pos]
            pos += 1
        dq_ref, dk_ref, dv_ref = refs[pos:pos + 3]
        pos += 3
        if has_sink:
            ds_ref = refs[pos]
            pos += 1
        ck_ref, cv_ref, dq_buf, dk_buf, dv_buf = refs[pos:pos + 5]
        r_id = pl.program_id(0)
        n = pl.program_id(1)

        @pl.when(n == 0)
        def _():
            ck_ref[...] = jnp.zeros_like(ck_ref)
            cv_ref[...] = jnp.zeros_like(cv_ref)

        if has_sink:
            @pl.when(jnp.logical_and(n == 0, r_id == 0))
            def _():
                ds_ref[...] = jnp.zeros_like(ds_ref)

        @pl.when(n < nb)
        def _():
            valid = _band_mask(n, max_dist)
            qsl = [slice(h * HEAD_DIM, (h + 1) * HEAD_DIM) for h in range(N_HEADS)]
            ksl = [slice((h // grp) * HEAD_DIM, (h // grp + 1) * HEAD_DIM) for h in range(N_HEADS)]
            kk = jnp.concatenate([kp_ref[...], kc_ref[...]], axis=0)
            vv = jnp.concatenate([vp_ref[...], vc_ref[...]], axis=0)
            qs, first = [], []
            for h in range(N_HEADS):
                q = q_ref[:, qsl[h]] * 0.125
                qs.append(q)
                first.append((_dot(q, kk[:, ksl[h]], NT), _dot(do_ref[:, qsl[h]], vv[:, ksl[h]], NT)))
            mid = []
            for h in range(N_HEADS):
                sc, dp = first[h]
                lse_h = l_ref[:, h:h + 1]
                p = jnp.exp(jnp.where(valid, sc, NEG) - lse_h)
                e = jnp.sum(do_ref[:, qsl[h]].astype(F32) * o_ref[:, qsl[h]].astype(F32), axis=-1, keepdims=True)
                mid.append(((p * (dp - e)).astype(ACT), p.astype(ACT)))
                if has_sink:
                    sk = s_ref[h:h + 1, 0:1]
                    dsk = -jnp.sum(jnp.exp(sk - lse_h) * e, axis=0, keepdims=True)
                    ds_ref[h:h + 1, :] = ds_ref[h:h + 1, :] + dsk
            dkk = [None] * hkv
            dvv = [None] * hkv
            for h in range(N_HEADS):
                kh = h // grp
                ds, p = mid[h]
                dq_buf[:, qsl[h]] = _dot(ds, kk[:, ksl[h]], NN) * 0.125
                for lst, val in ((dkk, _dot(ds, qs[h], TN)), (dvv, _dot(p, do_ref[:, qsl[h]], TN))):
                    lst[kh] = val if lst[kh] is None else lst[kh] + val
            for kh in range(hkv):
                ks = slice(kh * HEAD_DIM, (kh + 1) * HEAD_DIM)
                dk_buf[:, ks] = ck_ref[:, ks] + dkk[kh][:BLOCK]
                dv_buf[:, ks] = cv_ref[:, ks] + dvv[kh][:BLOCK]
                ck_ref[:, ks] = dkk[kh][BLOCK:]
                cv_ref[:, ks] = dvv[kh][BLOCK:]
            dq_ref[...] = dq_buf[...].astype(dq_ref.dtype)
            dk_ref[...] = dk_buf[...].astype(dk_ref.dtype)
            dv_ref[...] = dv_buf[...].astype(dv_ref.dtype)

        @pl.when(n == nb)
        def _():
            dk_ref[...] = ck_ref[...].astype(dk_ref.dtype)
            dv_ref[...] = cv_ref[...].astype(dv_ref.dtype)

    qrow = lambda n: jnp.minimum(n, nb - 1)
    prow = lambda n: jnp.maximum(jnp.minimum(n, nb - 1) - 1, 0)
    krow = lambda n: jnp.maximum(n - 1, 0)
    in_specs = [
        pl.BlockSpec((BLOCK, ATT_W), lambda r, n: (qrow(n), r * qpb)),
        pl.BlockSpec((BLOCK, kvw), lambda r, n: (qrow(n), r * kpb + koff)),
        pl.BlockSpec((BLOCK, kvw), lambda r, n: (prow(n), r * kpb + koff)),
        pl.BlockSpec((BLOCK, kvw), lambda r, n: (qrow(n), r)),
        pl.BlockSpec((BLOCK, kvw), lambda r, n: (prow(n), r)),
        pl.BlockSpec((BLOCK, ATT_W), lambda r, n: (qrow(n), r * dob + do_col)),
        pl.BlockSpec((BLOCK, ATT_W), lambda r, n: (qrow(n), r)),
        pl.BlockSpec((BLOCK, LANES), lambda r, n: (qrow(n), r)),
    ]
    qkf, vf = _fold(qk, d), _fold(v, d)
    args = [qkf, qkf, qkf, vf, vf, _fold(do_src, d), _fold(o, d), _fold(lse, d)]
    if has_sink:
        in_specs.append(_fullspec((N_HEADS, LANES)))
        args.append(sink)
    qspec = pl.BlockSpec((BLOCK, ATT_W), lambda r, n: (qrow(n), r))
    kspec = pl.BlockSpec((BLOCK, kvw), lambda r, n: (krow(n), r))
    out_specs = [qspec, kspec, kspec]
    out_shape = [jax.ShapeDtypeStruct((s // d, d * ATT_W), out_dtype),
                 jax.ShapeDtypeStruct((s // d, d * kvw), out_dtype),
                 jax.ShapeDtypeStruct((s // d, d * kvw), out_dtype)]
    if has_sink:
        out_specs.append(_fullspec((N_HEADS, LANES)))
        out_shape.append(jax.ShapeDtypeStruct((N_HEADS, LANES), F32))
    out = pl.pallas_call(
        body, grid=(d, nb + 1), name=name, in_specs=in_specs, out_specs=out_specs,
        out_shape=out_shape,
        scratch_shapes=[pltpu.VMEM((BLOCK, kvw), F32), pltpu.VMEM((BLOCK, kvw), F32),
                        pltpu.VMEM((BLOCK, ATT_W), F32), pltpu.VMEM((BLOCK, kvw), F32), pltpu.VMEM((BLOCK, kvw), F32)],
        compiler_params=_cp("arbitrary", "arbitrary"))(*args)
    res = [out[0].reshape(s, ATT_W), out[1].reshape(s, kvw), out[2].reshape(s, kvw)]
    if has_sink:
        res.append(out[3])
    return res


def combine_fwd(os_, lses, name):
    s = os_[0].shape[0]
    tm = _tile(s, (512,))

    def body(o1, o2, o3, l1, l2, l3, c_ref, l_ref, c_buf):
        a, b, c = l1[...], l2[...], l3[...]
        m = jnp.maximum(jnp.maximum(a, b), c)
        wa, wb, wc = jnp.exp(a - m), jnp.exp(b - m), jnp.exp(c - m)
        tot = wa + wb + wc
        l_ref[...] = m + jnp.log(tot)
        rt = 1.0 / tot
        wa, wb, wc = wa * rt, wb * rt, wc * rt
        for h in range(N_HEADS):
            cs = slice(h * HEAD_DIM, (h + 1) * HEAD_DIM)
            c_buf[:, cs] = (wa[:, h:h + 1] * o1[:, cs] + wb[:, h:h + 1] * o2[:, cs] + wc[:, h:h + 1] * o3[:, cs])
        c_ref[...] = c_buf[...].astype(ACT)

    spec = _rowspec(tm, ATT_W)
    lspec = _rowspec(tm, LANES)
    return pl.pallas_call(
        body, grid=(s // tm,), name=name, in_specs=[spec] * 3 + [lspec] * 3, out_specs=[spec, lspec],
        out_shape=[jax.ShapeDtypeStruct((s, ATT_W), ACT), jax.ShapeDtypeStruct((s, LANES), F32)],
        scratch_shapes=[pltpu.VMEM((tm, ATT_W), F32)],
        compiler_params=_cp("parallel"))(*os_, *lses)


def assemble(parts, tabs, name):
    s = parts[0][0][0].shape[0]
    tm = _tile(s, (256,))
    widths = [terms[0].shape[1] for terms, _ in parts]
    counts = [len(terms) for terms, _ in parts]
    flags = [f for _, f in parts]
    flat = [t for terms, _ in parts for t in terms]

    def body(*refs):
        c_ref, sa_ref, sb_ref, o_ref = refs[len(flat):]
        c, sa, sb = c_ref[...], sa_ref[...], sb_ref[...]
        off = 0
        first = 0
        for wd, cnt, fl in zip(widths, counts, flags):
            t_refs = refs[first:first + cnt]
            first += cnt
            for g in range(wd // LANES):
                cols = slice(g * LANES, (g + 1) * LANES)
                x = t_refs[0][:, cols].astype(F32)
                for t_ref in t_refs[1:]:
                    x = x + t_ref[:, cols].astype(F32)
                if fl:
                    x = _rope_t(x, c, sa, sb)
                o_ref[:, off + g * LANES:off + (g + 1) * LANES] = x.astype(ACT)
            off += wd

    tot = sum(widths)
    return pl.pallas_call(
        body, grid=(s // tm,), name=name,
        in_specs=[_rowspec(tm, wd) for wd, cnt in zip(widths, counts) for _ in range(cnt)]
        + [_rowspec(tm, LANES)] * 3,
        out_specs=_rowspec(tm, tot), out_shape=jax.ShapeDtypeStruct((s, tot), ACT),
        compiler_params=_cp("parallel"))(*flat, *tabs)


def _ln_stats(x):
    mu = jnp.mean(x, axis=-1, keepdims=True)
    xc = x - mu
    var = jnp.mean(xc * xc, axis=-1, keepdims=True)
    rstd = lax.rsqrt(var + LN_EPS)
    return xc * rstd, rstd


SUBLANES = 8


TAP_ROWS = 32


def _tap_sum(buf, cw_ref, offsets, tm, res_ref):
    for r0 in range(0, tm, TAP_ROWS):
        acc = None
        for ph in range(SUBLANES):
            taps = [j for j, off in enumerate(offsets) if off % SUBLANES == ph]
            if not taps:
                continue
            rows = TAP_ROWS if ph == 0 else TAP_ROWS + SUBLANES
            part = None
            for j in taps:
                term = cw_ref[j:j + 1, :] * buf[pl.ds(offsets[j] - ph + r0, rows), :]
                part = term if part is None else part + term
            part = part[ph:ph + TAP_ROWS]
            acc = part if acc is None else acc + part
        res_ref[pl.ds(r0, TAP_ROWS), :] = acc


def conv_fwd(ga, gb, cw, cb, lg, lb, name):
    s = ga.shape[0]
    tm = _tile(s, (256,))
    hb = tm // CONV_HALO

    def body(ga_ref, gb_ref, gah_ref, gbh_ref, cw_ref, cb_ref, lg_ref, lb_ref, c_ref, c1_ref, buf):
        i = pl.program_id(0)
        halo = gah_ref[...].astype(F32) * _sigmoid(gbh_ref[...].astype(F32))
        buf[0:CONV_HALO, :] = jnp.where(i > 0, halo, 0.0)
        buf[CONV_HALO:, :] = ga_ref[...].astype(F32) * _sigmoid(gb_ref[...].astype(F32))
        first = CONV_HALO - (CONV_WIDTH - 1)
        _tap_sum(buf, cw_ref, [first + j for j in range(CONV_WIDTH)], tm, c1_ref)
        acc = c1_ref[...] + cb_ref[...]
        c1_ref[...] = acc
        xh, _ = _ln_stats(acc)
        y = xh * lg_ref[...] + lb_ref[...]
        c_ref[...] = (y * _sigmoid(y)).astype(ACT)

    hspec = pl.BlockSpec((CONV_HALO, CONV_CH), lambda i: (jnp.maximum(i * hb - 1, 0), 0))
    vec = _fullspec((1, CONV_CH))
    spec = _rowspec(tm, CONV_CH)
    return pl.pallas_call(
        body, grid=(s // tm,), name=name,
        in_specs=[spec, spec, hspec, hspec, _fullspec((CONV_WIDTH, CONV_CH)), vec, vec, vec],
        out_specs=[spec, spec],
        out_shape=[jax.ShapeDtypeStruct((s, CONV_CH), ACT), jax.ShapeDtypeStruct((s, CONV_CH), F32)],
        scratch_shapes=[pltpu.VMEM((tm + CONV_HALO, CONV_CH), F32)],
        compiler_params=_cp("parallel"))(ga, gb, ga, gb, cw, cb, lg, lb)


def conv_bwd_ln(c1, dsrc, dcol, lg, lb, name):
    s = c1.shape[0]
    tm = _tile(s, (256,))

    def body(c1_ref, d_ref, lg_ref, lb_ref, o_ref, dg_ref, db_ref):
        @pl.when(pl.program_id(0) == 0)
        def _():
            dg_ref[...] = jnp.zeros_like(dg_ref)
            db_ref[...] = jnp.zeros_like(db_ref)

        xh, rstd = _ln_stats(c1_ref[...].astype(F32))
        y = xh * lg_ref[...] + lb_ref[...]
        sg = _sigmoid(y)
        dy = d_ref[...].astype(F32) * (sg * (1.0 + y * (1.0 - sg)))
        dg_ref[0:1, :] = dg_ref[0:1, :] + jnp.sum(dy * xh, axis=0, keepdims=True)
        db_ref[0:1, :] = db_ref[0:1, :] + jnp.sum(dy, axis=0, keepdims=True)
        dxh = dy * lg_ref[...]
        o_ref[...] = rstd * (dxh - jnp.mean(dxh, axis=-1, keepdims=True)
                             - xh * jnp.mean(dxh * xh, axis=-1, keepdims=True))

    vec = _fullspec((1, CONV_CH))
    acc = _fullspec((8, CONV_CH))
    return pl.pallas_call(
        body, grid=(s // tm,), name=name,
        in_specs=[_rowspec(tm, CONV_CH), _rowspec(tm, CONV_CH, dcol), vec, vec],
        out_specs=[_rowspec(tm, CONV_CH), acc, acc],
        out_shape=[jax.ShapeDtypeStruct((s, CONV_CH), F32)] + [jax.ShapeDtypeStruct((8, CONV_CH), F32)] * 2,
        compiler_params=_cp("arbitrary"))(c1, dsrc, lg, lb)


def conv_bwd_conv(dc1, ga, gb, cw, name):
    s = ga.shape[0]
    tm = _tile(s, (256,))
    hb = tm // CONV_HALO
    nt = s // tm
    last_h = s // CONV_HALO - 1
    first = CONV_HALO - (CONV_WIDTH - 1)


    def rows8(x):
        return jnp.sum(x.reshape(x.shape[0] // SUBLANES, SUBLANES, CONV_CH), axis=0)

    def body(d_ref, dn_ref, ga_ref, gb_ref, gah_ref, gbh_ref, cw_ref,
             dga_ref, dgb_ref, dw_ref, db_ref, dbuf, cbuf, sbuf):
        i = pl.program_id(0)

        @pl.when(i == 0)
        def _():
            dw_ref[...] = jnp.zeros_like(dw_ref)
            db_ref[...] = jnp.zeros_like(db_ref)

        d = d_ref[...]
        dbuf[0:tm, :] = d
        dbuf[tm:, :] = jnp.where(i < nt - 1, dn_ref[...], 0.0)
        halo = gah_ref[...].astype(F32) * _sigmoid(gbh_ref[...].astype(F32))
        cbuf[0:CONV_HALO, :] = jnp.where(i > 0, halo, 0.0)
        a = ga_ref[...].astype(F32)
        sg = _sigmoid(gb_ref[...].astype(F32))
        cbuf[CONV_HALO:, :] = a * sg
        for ph in range(SUBLANES):
            taps = [j for j in range(CONV_WIDTH) if (first + j) % SUBLANES == ph]
            if ph:
                sbuf[0:tm + CONV_HALO - SUBLANES, :] = cbuf[pl.ds(ph, tm + CONV_HALO - SUBLANES), :]
            src = sbuf if ph else cbuf
            for r0 in range(0, tm, TAP_ROWS):
                d_blk = dbuf[pl.ds(r0, TAP_ROWS), :]
                for j in taps:
                    tap = src[pl.ds(first + j - ph + r0, TAP_ROWS), :]
                    rows = slice(j * SUBLANES, (j + 1) * SUBLANES)
                    dw_ref[rows, :] = dw_ref[rows, :] + rows8(d_blk * tap)
        db_ref[...] = db_ref[...] + rows8(d)
        _tap_sum(dbuf, cw_ref, [CONV_WIDTH - 1 - j for j in range(CONV_WIDTH)], tm, sbuf)
        dc0 = sbuf[0:tm, :]
        dga_ref[...] = (dc0 * sg).astype(ACT)
        dgb_ref[...] = (dc0 * a * sg * (1.0 - sg)).astype(ACT)

    spec = _rowspec(tm, CONV_CH)
    hprev = pl.BlockSpec((CONV_HALO, CONV_CH), lambda i: (jnp.maximum(i * hb - 1, 0), 0))
    hnext = pl.BlockSpec((CONV_HALO, CONV_CH), lambda i: (jnp.minimum((i + 1) * hb, last_h), 0))
    return pl.pallas_call(
        body, grid=(nt,), name=name,
        in_specs=[spec, hnext, spec, spec, hprev, hprev, _fullspec((CONV_WIDTH, CONV_CH))],
        out_specs=[spec, spec, _fullspec((CONV_HALO * SUBLANES, CONV_CH)), _fullspec((SUBLANES, CONV_CH))],
        out_shape=[jax.ShapeDtypeStruct((s, CONV_CH), ACT)] * 2
        + [jax.ShapeDtypeStruct((CONV_HALO * SUBLANES, CONV_CH), F32), jax.ShapeDtypeStruct((SUBLANES, CONV_CH), F32)],
        scratch_shapes=[pltpu.VMEM((tm + CONV_HALO, CONV_CH), F32)] * 3,
        compiler_params=_cp("arbitrary"))(dc1, dc1, ga, gb, ga, gb, cw)


_GELU_K = math.sqrt(2.0 / math.pi)
_GELU_C = 0.044715


def _gelu(x):
    return 0.5 * x * (1.0 + jnp.tanh(_GELU_K * (x + _GELU_C * x * x * x)))


def _gelu_grad(x):
    t = jnp.tanh(_GELU_K * (x + _GELU_C * x * x * x))
    return 0.5 * (1.0 + t) + 0.5 * x * (1.0 - t * t) * _GELU_K * (1.0 + 3.0 * _GELU_C * x * x)


def _tril():
    qi = lax.broadcasted_iota(jnp.int32, (BLOCK, BLOCK), 0)
    kj = lax.broadcasted_iota(jnp.int32, (BLOCK, BLOCK), 1)
    return kj <= qi


def gmlp_fwd(z, lg, lb, sw, sbt, name):
    s = z.shape[0]

    def body(z_ref, lg_ref, lb_ref, sw_ref, sb_ref, o_ref, o_buf):
        zz = _gelu(z_ref[...].astype(F32))
        u = zz[:, :D_CH]
        xh, _ = _ln_stats(zz[:, D_CH:])
        gn = (xh * lg_ref[...] + lb_ref[...]).astype(ACT)
        tril = _tril()
        for g in range(D_GROUPS):
            cs = slice(g * HEAD_DIM, (g + 1) * HEAD_DIM)
            w = jnp.where(tril, sw_ref[g], 0.0).astype(ACT)
            mixed = _dot(w, gn[:, cs], NN) + sb_ref[:, g:g + 1]
            o_buf[:, cs] = u[:, cs] * mixed
        o_ref[...] = o_buf[...].astype(ACT)

    return pl.pallas_call(
        body, grid=(s // BLOCK,), name=name,
        in_specs=[_rowspec(BLOCK, 2 * D_CH), _fullspec((1, D_CH)), _fullspec((1, D_CH)),
                  _fullspec((D_GROUPS, BLOCK, BLOCK)), _fullspec((BLOCK, D_GROUPS))],
        out_specs=_rowspec(BLOCK, D_CH), out_shape=jax.ShapeDtypeStruct((s, D_CH), ACT),
        scratch_shapes=[pltpu.VMEM((BLOCK, D_CH), F32)],
        compiler_params=_cp("parallel"))(z, lg, lb, sw, sbt)


def gmlp_bwd(z, dsrc, dcol, lg, lb, sw, sbt, name):
    s = z.shape[0]

    def body(z_ref, d_ref, lg_ref, lb_ref, sw_ref, sb_ref, dz_ref, dw_ref, dsb_ref, dg_ref, db_ref, dgn_buf, du_buf):
        @pl.when(pl.program_id(0) == 0)
        def _():
            dw_ref[...] = jnp.zeros_like(dw_ref)
            dsb_ref[...] = jnp.zeros_like(dsb_ref)
            dg_ref[...] = jnp.zeros_like(dg_ref)
            db_ref[...] = jnp.zeros_like(db_ref)

        zr = z_ref[...].astype(F32)
        zz = _gelu(zr)
        u = zz[:, :D_CH]
        xh, rstd = _ln_stats(zz[:, D_CH:])
        gn = (xh * lg_ref[...] + lb_ref[...]).astype(ACT)
        dd = d_ref[...].astype(F32)
        tril = _tril()
        for g in range(D_GROUPS):
            cs = slice(g * HEAD_DIM, (g + 1) * HEAD_DIM)
            w = jnp.where(tril, sw_ref[g], 0.0).astype(ACT)
            gn_g = gn[:, cs]
            mixed = _dot(w, gn_g, NN) + sb_ref[:, g:g + 1]
            du = dd[:, cs] * mixed
            dmix = dd[:, cs] * u[:, cs]
            dmix_a = dmix.astype(ACT)
            du_buf[:, cs] = du
            dw_ref[g] = dw_ref[g] + jnp.where(tril, _dot(dmix_a, gn_g, NT), 0.0)
            dsb_ref[:, g:g + 1] = dsb_ref[:, g:g + 1] + jnp.sum(dmix, axis=-1, keepdims=True)
            dgn_buf[:, cs] = _dot(w, dmix_a, TN)
        dz_ref[:, :D_CH] = (du_buf[...] * _gelu_grad(zr[:, :D_CH])).astype(ACT)
        dgn = dgn_buf[...]
        dg_ref[0:1, :] = dg_ref[0:1, :] + jnp.sum(dgn * xh, axis=0, keepdims=True)
        db_ref[0:1, :] = db_ref[0:1, :] + jnp.sum(dgn, axis=0, keepdims=True)
        dxh = dgn * lg_ref[...]
        dgate = rstd * (dxh - jnp.mean(dxh, axis=-1, keepdims=True)
                        - xh * jnp.mean(dxh * xh, axis=-1, keepdims=True))
        dz_ref[:, D_CH:] = (dgate * _gelu_grad(zr[:, D_CH:])).astype(ACT)

    vec = _fullspec((1, D_CH))
    acc = _fullspec((8, D_CH))
    return pl.pallas_call(
        body, grid=(s // BLOCK,), name=name,
        in_specs=[_rowspec(BLOCK, 2 * D_CH), _rowspec(BLOCK, D_CH, dcol), vec, vec,
                  _fullspec((D_GROUPS, BLOCK, BLOCK)), _fullspec((BLOCK, D_GROUPS))],
        out_specs=[_rowspec(BLOCK, 2 * D_CH), _fullspec((D_GROUPS, BLOCK, BLOCK)),
                   _fullspec((BLOCK, D_GROUPS)), acc, acc],
        out_shape=[jax.ShapeDtypeStruct((s, 2 * D_CH), ACT),
                   jax.ShapeDtypeStruct((D_GROUPS, BLOCK, BLOCK), F32),
                   jax.ShapeDtypeStruct((BLOCK, D_GROUPS), F32),
                   jax.ShapeDtypeStruct((8, D_CH), F32), jax.ShapeDtypeStruct((8, D_CH), F32)],
        scratch_shapes=[pltpu.VMEM((BLOCK, D_CH), F32)] * 2,
        compiler_params=_cp("arbitrary"))(z, dsrc, lg, lb, sw, sbt)


def _rms_bwd(dn, x, g):
    r = lax.rsqrt(jnp.mean(x * x, axis=-1, keepdims=True) + RMS_EPS)
    u = dn * g
    dx = r * u - x * (r * r * r) * jnp.mean(x * u, axis=-1, keepdims=True)
    return dx, dn * x * r


def final_loss(h, g, target, name):
    s = h.shape[0]
    tm = _tile(s, (256,))

    def body(h_ref, g_ref, t_ref, loss_ref, dh_ref, dg_ref):
        @pl.when(pl.program_id(0) == 0)
        def _():
            loss_ref[...] = jnp.zeros_like(loss_ref)
            dg_ref[...] = jnp.zeros_like(dg_ref)

        x = h_ref[...]
        r = lax.rsqrt(jnp.mean(x * x, axis=-1, keepdims=True) + RMS_EPS)
        diff = x * r * g_ref[...] - t_ref[...]
        part = jnp.sum(jnp.sum(diff * diff, axis=-1, keepdims=True), axis=0, keepdims=True)
        loss_ref[...] = loss_ref[...] + part * (0.5 / D_MODEL)
        dx, dgt = _rms_bwd(diff * (1.0 / D_MODEL), x, g_ref[...])
        dh_ref[...] = dx
        dg_ref[0:1, :] = dg_ref[0:1, :] + jnp.sum(dgt, axis=0, keepdims=True)

    spec = _rowspec(tm, D_MODEL)
    return pl.pallas_call(
        body, grid=(s // tm,), name=name,
        in_specs=[spec, _fullspec((1, D_MODEL)), spec],
        out_specs=[_fullspec((8, LANES)), spec, _fullspec((8, D_MODEL))],
        out_shape=[jax.ShapeDtypeStruct((8, LANES), F32), jax.ShapeDtypeStruct((s, D_MODEL), F32),
                   jax.ShapeDtypeStruct((8, D_MODEL), F32)],
        compiler_params=_cp("arbitrary"))(h, g, target)


def mm_nt(dy, w, name):
    s, n = dy.shape
    k = w.shape[0]
    tm = _tile(s, (512,))
    tk = _tile(k, (512,))

    def body(d_ref, w_ref, o_ref):
        o_ref[...] = _dot(d_ref[...].astype(ACT), w_ref[...], NT).astype(ACT)

    return pl.pallas_call(
        body, grid=(k // tk, s // tm), name=name,
        in_specs=[pl.BlockSpec((tm, n), lambda j, i: (i, 0)), pl.BlockSpec((tk, n), lambda j, i: (j, 0))],
        out_specs=pl.BlockSpec((tm, tk), lambda j, i: (i, j)),
        out_shape=jax.ShapeDtypeStruct((s, k), ACT),
        compiler_params=_cp("parallel", "parallel"))(dy, w)


def ffn_down_bwd(dh, wd, gate, up, name):
    s = dh.shape[0]
    f = wd.shape[0]
    tm = _tile(s, (512,))
    tf = _tile(f, (1408, 512, 256, 128))

    def body(d_ref, w_ref, g_ref, u_ref, dg_ref, du_ref):
        dact = _dot(d_ref[...].astype(ACT), w_ref[...], NT)
        g = g_ref[...].astype(F32)
        sg = _sigmoid(g)
        dg_ref[...] = (dact * u_ref[...].astype(F32) * (sg * (1.0 + g * (1.0 - sg)))).astype(ACT)
        du_ref[...] = (dact * g * sg).astype(ACT)

    tile = pl.BlockSpec((tm, tf), lambda j, i: (i, j))
    return pl.pallas_call(
        body, grid=(f // tf, s // tm), name=name,
        in_specs=[pl.BlockSpec((tm, D_MODEL), lambda j, i: (i, 0)),
                  pl.BlockSpec((tf, D_MODEL), lambda j, i: (j, 0)), tile, tile],
        out_specs=[tile, tile], out_shape=[jax.ShapeDtypeStruct((s, f), ACT)] * 2,
        compiler_params=_cp("parallel", "parallel"))(dh, wd, gate, up)


def mm_nt_rms(parts, h, g, dh, name):
    s = h.shape[0]
    tm = _tile(s, (256,))
    np_ = len(parts)

    def body(*refs):
        d_refs = refs[:np_]
        w_refs = refs[np_:2 * np_]
        h_ref, g_ref, dh_ref, o_ref, dg_ref = refs[2 * np_:]

        @pl.when(pl.program_id(0) == 0)
        def _():
            dg_ref[...] = jnp.zeros_like(dg_ref)

        dn = None
        for d_ref, w_ref in zip(d_refs, w_refs):
            t = _dot(d_ref[...], w_ref[...], NT)
            dn = t if dn is None else dn + t
        dx, dgt = _rms_bwd(dn, h_ref[...], g_ref[...])
        o_ref[...] = dh_ref[...] + dx
        dg_ref[0:1, :] = dg_ref[0:1, :] + jnp.sum(dgt, axis=0, keepdims=True)

    spec = _rowspec(tm, D_MODEL)
    return pl.pallas_call(
        body, grid=(s // tm,), name=name,
        in_specs=[_rowspec(tm, d.shape[1]) for d, _ in parts] + [_fullspec(w.shape) for _, w in parts]
        + [spec, _fullspec((1, D_MODEL)), spec],
        out_specs=[spec, _fullspec((8, D_MODEL))],
        out_shape=[jax.ShapeDtypeStruct((s, D_MODEL), F32), jax.ShapeDtypeStruct((8, D_MODEL), F32)],
        compiler_params=_cp("arbitrary"))(*[d for d, _ in parts], *[w for _, w in parts], h, g, dh)


def mm_tn(a, b, name):
    s, k = a.shape
    n = b.shape[1]
    tk = _tile(k, (512, 1408, 256, 128))
    tn = _tile(n, (1408, 1280, 1024, 896, 512, 256, 128))
    ts = _tile(s, (2048, 512) if b.dtype == ACT else (1024, 512))

    def body(a_ref, b_ref, o_ref):
        t = _dot(a_ref[...].astype(ACT), b_ref[...].astype(ACT), TN)

        @pl.when(pl.program_id(2) == 0)
        def _():
            o_ref[...] = t

        @pl.when(pl.program_id(2) > 0)
        def _():
            o_ref[...] = o_ref[...] + t

    return pl.pallas_call(
        body, grid=(k // tk, n // tn, s // ts), name=name,
        in_specs=[pl.BlockSpec((ts, tk), lambda i, j, t: (t, i)), pl.BlockSpec((ts, tn), lambda i, j, t: (t, j))],
        out_specs=pl.BlockSpec((tk, tn), lambda i, j, t: (i, j)),
        out_shape=jax.ShapeDtypeStruct((k, n), F32),
        compiler_params=_cp("parallel", "parallel", "arbitrary"))(a, b)


def _adamw_math(w, g, m, v):
    m = ADAM_B1 * m + (1.0 - ADAM_B1) * g
    v = ADAM_B2 * v + (1.0 - ADAM_B2) * (g * g)
    m_hat = m / (1.0 - ADAM_B1 ** ADAM_STEP)
    v_hat = v / (1.0 - ADAM_B2 ** ADAM_STEP)
    delta = -ADAM_LR * (m_hat / (jnp.sqrt(v_hat) + ADAM_EPS) + ADAM_WD * w)
    return delta, m, v


def sum_adamw(parts, w, m, v, layer, others, name):
    nl, r, c = w.shape
    tr = _tile(r, (256, 128, 64, 32, 16, 8))

    def body(p_ref, w_ref, m_ref, v_ref, *rest):
        g_ref, d_ref, mo_ref, vo_ref = rest[-4:]
        g = p_ref[0].astype(F32)
        for i in range(1, N_DEV):
            g = g + p_ref[i].astype(F32)
        d, mm, vv = _adamw_math(w_ref[...], g, m_ref[...], v_ref[...])
        g_ref[...] = g
        d_ref[...] = d
        mo_ref[...] = mm
        vo_ref[...] = vv

    spec = pl.BlockSpec((None, tr, c), lambda i: (layer, i, 0))
    in_specs = [pl.BlockSpec((N_DEV, tr, c), lambda i: (0, i, 0))] + [spec] * 3
    args = [parts, w, m, v]
    aliases = {}
    if others is not None:
        in_specs += [pl.BlockSpec(memory_space=pl.ANY)] * 4
        args += list(others)
        aliases = {4 + j: j for j in range(4)}
    return pl.pallas_call(
        body, grid=(r // tr,), name=name, in_specs=in_specs, out_specs=[spec] * 4,
        out_shape=[jax.ShapeDtypeStruct((nl, r, c), F32)] * 4, input_output_aliases=aliases,
        compiler_params=_cp("parallel"))(*args)


def adamw_small(ws, gs, ms, vs, name):
    k = len(ws)

    def body(*refs):
        for i in range(k):
            w_ref, g_ref, m_ref, v_ref = (refs[j * k + i] for j in range(4))
            d, mm, vv = _adamw_math(w_ref[...], g_ref[...], m_ref[...], v_ref[...])
            refs[4 * k + i][...] = d
            refs[5 * k + i][...] = mm
            refs[6 * k + i][...] = vv

    shapes = [jax.ShapeDtypeStruct(w.shape, F32) for w in ws]
    specs = [_fullspec(w.shape) for w in ws]
    out = pl.pallas_call(
        body, grid=(1,), name=name, in_specs=specs * 4, out_specs=specs * 3, out_shape=shapes * 3,
        compiler_params=_cp("arbitrary"))(*ws, *gs, *ms, *vs)
    return out[:k], out[k:2 * k], out[2 * k:]


def sum_slots(x, out_dtype, name):
    g, r, c = x.shape
    tr = r if x.size * x.dtype.itemsize <= SMALL_BLOCK_BYTES else _tile(r, (256, 128, 64, 32, 16, 8))

    def body(x_ref, o_ref):
        acc = x_ref[0].astype(F32)
        for i in range(1, g):
            acc = acc + x_ref[i].astype(F32)
        o_ref[...] = acc.astype(o_ref.dtype)

    return pl.pallas_call(
        body, grid=(r // tr,), name=name,
        in_specs=[pl.BlockSpec((g, tr, c), lambda i: (0, i, 0))], out_specs=_rowspec(tr, c),
        out_shape=jax.ShapeDtypeStruct((r, c), out_dtype),
        compiler_params=_cp("parallel"))(x)


HBM_SPEC = pl.BlockSpec(memory_space=pltpu.HBM)
SEM_SPEC = pl.BlockSpec(memory_space=pltpu.SEMAPHORE)
DATAFLOW = pltpu.SideEffectType.DATAFLOW_SIDE_EFFECTING


def _my_rank():
    return 4 * lax.axis_index("x") + 2 * lax.axis_index("y") + lax.axis_index("c")


def _exchange_copies(x_refs, land_refs, send, recv, a2a):
    pos = [lax.axis_index(a) for a in AXES]
    me = _my_rank()
    copies = []
    for x_ref, land_ref, s_ref, r_ref in zip(x_refs, land_refs, send, recv):
        for k in range(N_DEV - 1):
            bits = ((k + 1) >> 2 & 1, (k + 1) >> 1 & 1, (k + 1) & 1)
            peer = tuple(1 - p if b else p for p, b in zip(pos, bits))
            prank = 4 * peer[0] + 2 * peer[1] + peer[2]
            copies.append(pltpu.make_async_remote_copy(
                src_ref=x_ref.at[prank] if a2a else x_ref, dst_ref=land_ref.at[me],
                send_sem=s_ref.at[k], recv_sem=r_ref.at[k], device_id=peer, device_id_type=MESH))
    return copies


def exchange_start(xs, a2a, name):
    n = len(xs)
    me = _my_rank()
    lands = []
    for x in xs:
        own = lax.dynamic_index_in_dim(x, me, 0, keepdims=True) if a2a else x[None]
        shape = x.shape if a2a else (N_DEV,) + x.shape
        lands.append(lax.dynamic_update_slice(lax.empty(shape, x.dtype), own, (me,) + (0,) * (len(shape) - 1)))

    def body(*refs):
        x_refs, land_refs = refs[:n], refs[n:2 * n]
        outs = refs[2 * n:]
        for cp in _exchange_copies(x_refs, land_refs, outs[:n], outs[n:2 * n], a2a):
            cp.start()
        token = outs[4 * n]
        token[...] = jnp.zeros_like(token)

    sems = [pltpu.SemaphoreType.DMA((N_DEV - 1,))] * n
    out = pl.pallas_call(
        body, name=name,
        out_shape=tuple(sems + sems + [pltpu.HBM(x.shape, x.dtype) for x in xs]
                        + [pltpu.HBM(l.shape, l.dtype) for l in lands] + [jax.ShapeDtypeStruct((8, LANES), F32)]),
        in_specs=[HBM_SPEC] * (2 * n),
        out_specs=tuple([SEM_SPEC] * (2 * n) + [HBM_SPEC] * (2 * n) + [pl.BlockSpec(memory_space=pltpu.VMEM)]),
        input_output_aliases={i: 2 * n + i for i in range(2 * n)},
        compiler_params=pltpu.CompilerParams(has_side_effects=DATAFLOW),
    )(*[pltpu.with_memory_space_constraint(a, pltpu.HBM) for a in list(xs) + lands])
    return (out[:n], out[n:2 * n], out[2 * n:3 * n], out[3 * n:4 * n]), out[4 * n]


def exchange_wait(handles, after, a2a, name):
    send, recv, x_thru, land_thru = handles
    n = len(x_thru)

    def body(*refs):
        x_refs, land_refs = refs[:n], refs[n:2 * n]
        s_refs, r_refs = refs[2 * n:3 * n], refs[3 * n:4 * n]
        for cp in _exchange_copies(x_refs, land_refs, s_refs, r_refs, a2a):
            cp.wait_send()
            cp.wait_recv()

    out = pl.pallas_call(
        body, name=name,
        out_shape=tuple([pltpu.HBM(a.shape, a.dtype) for a in list(x_thru) + list(land_thru)]),
        in_specs=[HBM_SPEC] * (2 * n) + [SEM_SPEC] * (2 * n) + [pl.BlockSpec(memory_space=pl.ANY)],
        out_specs=tuple([HBM_SPEC] * (2 * n)),
        input_output_aliases={i: i for i in range(2 * n)},
        compiler_params=pltpu.CompilerParams(has_side_effects=DATAFLOW),
    )(*x_thru, *land_thru, *send, *recv, after)
    return out[n:2 * n]


def _local_step(x, target, weight, emit, P):
    s = x.shape[0]
    tabs = _rope_tables(s)
    sinkb = jnp.broadcast_to(P["ev_sinks"].reshape(N_HEADS, 1), (N_HEADS, LANES))
    sbt = P["od_spatial_b"].reshape(D_GROUPS, BLOCK).T
    sw = P["od_spatial_w"].reshape(D_GROUPS, BLOCK, BLOCK)
    fg = P["ffn_norm_g"]
    latest = [None]

    def out(name, layer, grad):
        tok = emit(name, layer, grad)
        if tok is not None:
            latest[0] = tok

    def dep(a):
        return a if latest[0] is None else a + latest[0][0:1, 0:1]

    n0 = rmsnorm(x, P["ev_norm_g"], "rms_in")
    qk_e, v_e, ga, gb = inproj(n0, weight("ev_w_in", 0, n0), tabs, ATT_W + 128,
                               (ATT_W + 128, 128, CONV_CH, CONV_CH), "ev_inproj")
    a_e, lse_e = attn_fwd(qk_e, v_e, d=1, hkv=A_KV_HEADS, max_dist=BLOCK - 1, sink=sinkb, out_dtype=ACT,
                          name="ev_attn")
    c_act, c1 = conv_fwd(ga, gb, P["ev_conv_w"], P["ev_conv_b"], P["ev_conv_ln_g"], P["ev_conv_ln_b"], "ev_conv")
    h1, n1 = mm_res([a_e, c_act], weight("ev_w_out", 0, c_act), x, fg[0:1], "ev_outproj")
    gate0, up0, act0 = ffn_up(n1, weight("ffn_w_gate", 0, n1), weight("ffn_w_up", 0, n1), "ffn0_up")
    h2, n2 = mm_res([act0], weight("ffn_w_down", 0, act0), h1, P["od_norm_g"], "ffn0_down")
    qk_o, v_o, z = inproj(n2, weight("od_w_in", 0, n2), tabs, 2 * ATT_W, (2 * ATT_W, ATT_W, 2 * D_CH), "od_inproj")
    outs, lses = [], []
    for window, dil in DILATED:
        assert window // dil == BLOCK
        o_r, l_r = attn_fwd(qk_o, v_o, d=dil, hkv=N_HEADS, max_dist=BLOCK, sink=None, out_dtype=F32,
                            name="od_attn_d%d" % dil)
        outs.append(o_r)
        lses.append(l_r)
    c_out, lse_o = combine_fwd(outs, lses, "od_combine")
    d_out = gmlp_fwd(z, P["od_sgu_ln_g"], P["od_sgu_ln_b"], sw, sbt, "od_gmlp")
    h3, n3 = mm_res([c_out, d_out], weight("od_w_out", 0, d_out), h2, fg[1:2], "od_outproj")
    gate1, up1, act1 = ffn_up(n3, weight("ffn_w_gate", 1, n3), weight("ffn_w_up", 1, n3), "ffn1_up")
    h4, _ = mm_res([act1], weight("ffn_w_down", 1, act1), h3, None, "ffn1_down")
    loss_part, dh4, dg_final = final_loss(h4, P["final_norm_g"], target, "loss_head")

    def ffn_bwd(layer, dh_out, h_in, n_in, gate, up, act):
        wg, wu, wd = (weight(n, layer, dh_out) for n in ("ffn_w_gate", "ffn_w_up", "ffn_w_down"))
        tag = "ffn%d" % layer
        dgate, dup = ffn_down_bwd(dh_out, wd, gate, up, tag + "_down_bwd")
        g_wd = mm_tn(act, dh_out, tag + "_dwd")
        dh_in, dgn = mm_nt_rms([(dgate, wg), (dup, wu)], h_in, dep(fg[layer:layer + 1]), dh_out, tag + "_up_bwd")
        out("ffn_w_down", layer, g_wd)
        out("ffn_w_gate", layer, mm_tn(n_in, dgate, tag + "_dwg"))
        out("ffn_w_up", layer, mm_tn(n_in, dup, tag + "_dwu"))
        return dh_in, dgn[0:1]

    dh3, dgn_f1 = ffn_bwd(1, dh4, h3, n3, gate1, up1, act1)

    dcd = mm_nt(dh3, weight("od_w_out", 0, dh3), "od_outproj_bwd")
    dz, g_sw, g_sbt, g_slg, g_slb = gmlp_bwd(z, dcd, 1, dep(P["od_sgu_ln_g"]), P["od_sgu_ln_b"], sw, sbt,
                                             "od_gmlp_bwd")
    out("od_spatial", 0, jnp.concatenate([g_sw.reshape(D_GROUPS * BLOCK, BLOCK), g_sbt.T], axis=0))
    out("od_w_out", 0, jnp.concatenate([mm_tn(c_out, dh3, "od_dwo_c"), mm_tn(d_out, dh3, "od_dwo_d")], axis=0))
    dqkv = [attn_bwd(qk_o, v_o, dcd if dil == 1 else dcd[:, :ATT_W], 0, c_out, lse_o, d=dil, hkv=N_HEADS,
                     max_dist=BLOCK, sink=None, out_dtype=ACT, name="od_attn_bwd_d%d" % dil)
            for window, dil in DILATED]
    dproj_o = assemble([([b[0] for b in dqkv], True), ([b[1] for b in dqkv], True), ([b[2] for b in dqkv], False),
                        ([dz], False)], tabs, "od_dproj")
    dh2, dgn_od = mm_nt_rms([(dproj_o, weight("od_w_in", 0, dproj_o))], h2, dep(P["od_norm_g"]), dh3,
                            "od_inproj_bwd")
    out("od_w_in", 0, mm_tn(n2, dproj_o, "od_dwi"))

    dh1, dgn_f0 = ffn_bwd(0, dh2, h1, n1, gate0, up0, act0)

    dac = mm_nt(dh1, weight("ev_w_out", 0, dh1), "ev_outproj_bwd")
    dc1, g_clg, g_clb = conv_bwd_ln(c1, dac, 1, dep(P["ev_conv_ln_g"]), P["ev_conv_ln_b"], "ev_conv_bwd_ln")
    out("ev_w_out", 0, jnp.concatenate([mm_tn(a_e, dh1, "ev_dwo_a"), mm_tn(c_act, dh1, "ev_dwo_c")], axis=0))
    dga, dgb, g_cw, g_cb = conv_bwd_conv(dc1, ga, gb, P["ev_conv_w"], "ev_conv_bwd")
    dq, dk, dv, dsink = attn_bwd(qk_e, v_e, dac, 0, a_e, lse_e, d=1, hkv=A_KV_HEADS, max_dist=BLOCK - 1,
                                 sink=sinkb, out_dtype=F32, name="ev_attn_bwd")
    dproj_e = assemble([([dq], True), ([dk], True), ([dv], False), ([dga], False), ([dgb], False)], tabs,
                       "ev_dproj")
    out("ev_w_in", 0, mm_tn(n0, dproj_e, "ev_dwi"))
    dx, dgn_ev = mm_nt_rms([(dproj_e, weight("ev_w_in", 0, dproj_e))], x, dep(P["ev_norm_g"]), dh1,
                           "ev_inproj_bwd")

    small = {
        "ev_norm_g": dgn_ev[0:1],
        "ev_sinks": dsink[:, 0:1].reshape(1, N_HEADS),
        "ev_conv_w": jnp.sum(g_cw.reshape(CONV_HALO, SUBLANES, CONV_CH), axis=1)[:CONV_WIDTH],
        "ev_conv_b": jnp.sum(g_cb, axis=0, keepdims=True),
        "ev_conv_ln_g": g_clg[0:1],
        "ev_conv_ln_b": g_clb[0:1],
        "od_norm_g": dgn_od[0:1],
        "od_sgu_ln_g": g_slg[0:1],
        "od_sgu_ln_b": g_slb[0:1],
        "od_spatial_w": g_sw.reshape(D_GROUPS * BLOCK, BLOCK),
        "od_spatial_b": g_sbt.T,
        "ffn_norm_g": jnp.concatenate([dgn_f0, dgn_f1], axis=0),
        "final_norm_g": dg_final[0:1],
    }
    return loss_part, dx, small


BIG = ("ev_w_in", "ev_w_out", "od_w_in", "od_w_out", "ffn_w_gate", "ffn_w_up", "ffn_w_down")
COL_SHARDED = ("ev_w_in", "od_w_in", "ffn_w_gate", "ffn_w_up")
GATHER_GROUPS = (
    (("ev_w_in", 0),),
    (("ev_w_out", 0), ("ffn_w_gate", 0), ("ffn_w_up", 0), ("ffn_w_down", 0)),
    (("od_w_in", 0),),
    (("od_w_out", 0), ("ffn_w_gate", 1), ("ffn_w_up", 1), ("ffn_w_down", 1)),
)
REDUCE_GROUPS = (
    (("ffn_w_down", 1), ("ffn_w_gate", 1), ("ffn_w_up", 1)),
    (("od_w_out", 0),),
    (("od_w_in", 0),),
    (("ffn_w_down", 0), ("ffn_w_gate", 0), ("ffn_w_up", 0)),
    (("ev_w_out", 0),),
    (("ev_w_in", 0),),
)


def _unshard(name, g):
    if name in COL_SHARDED:
        return jnp.moveaxis(g, 0, 1).reshape(g.shape[1], N_DEV * g.shape[2])
    return g.reshape(N_DEV * g.shape[1], g.shape[2])


def _shard_slots(name, full):
    r, c = full.shape
    if name in COL_SHARDED:
        return jnp.moveaxis(full.reshape(r, N_DEV, c // N_DEV), 1, 0)
    return full.reshape(N_DEV, r // N_DEV, c)


def kernel(x, ev_norm_g, ev_w_in, ev_sinks, ev_conv_w, ev_conv_b, ev_conv_ln_g, ev_conv_ln_b, ev_w_out, od_norm_g, od_w_in, od_sgu_ln_g, od_sgu_ln_b, od_spatial_w, od_spatial_b, od_w_out, ffn_norm_g, ffn_w_gate, ffn_w_up, ffn_w_down, final_norm_g, loss_target, m_ev_norm_g, m_ev_w_in, m_ev_sinks, m_ev_conv_w, m_ev_conv_b, m_ev_conv_ln_g, m_ev_conv_ln_b, m_ev_w_out, m_od_norm_g, m_od_w_in, m_od_sgu_ln_g, m_od_sgu_ln_b, m_od_spatial_w, m_od_spatial_b, m_od_w_out, m_ffn_norm_g, m_ffn_w_gate, m_ffn_w_up, m_ffn_w_down, m_final_norm_g, v_ev_norm_g, v_ev_w_in, v_ev_sinks, v_ev_conv_w, v_ev_conv_b, v_ev_conv_ln_g, v_ev_conv_ln_b, v_ev_w_out, v_od_norm_g, v_od_w_in, v_od_sgu_ln_g, v_od_sgu_ln_b, v_od_spatial_w, v_od_spatial_b, v_od_w_out, v_ffn_norm_g, v_ffn_w_gate, v_ffn_w_up, v_ffn_w_down, v_final_norm_g):
    names = ["ev_norm_g", "ev_w_in", "ev_sinks", "ev_conv_w", "ev_conv_b", "ev_conv_ln_g", "ev_conv_ln_b", "ev_w_out",
             "od_norm_g", "od_w_in", "od_sgu_ln_g", "od_sgu_ln_b", "od_spatial_w", "od_spatial_b", "od_w_out",
             "ffn_norm_g", "ffn_w_gate", "ffn_w_up", "ffn_w_down", "final_norm_g"]
    wts = dict(zip(names, [ev_norm_g, ev_w_in, ev_sinks, ev_conv_w, ev_conv_b, ev_conv_ln_g, ev_conv_ln_b, ev_w_out,
                           od_norm_g, od_w_in, od_sgu_ln_g, od_sgu_ln_b, od_spatial_w, od_spatial_b, od_w_out,
                           ffn_norm_g, ffn_w_gate, ffn_w_up, ffn_w_down, final_norm_g]))
    mom = dict(zip(names, [m_ev_norm_g, m_ev_w_in, m_ev_sinks, m_ev_conv_w, m_ev_conv_b, m_ev_conv_ln_g, m_ev_conv_ln_b,
                           m_ev_w_out, m_od_norm_g, m_od_w_in, m_od_sgu_ln_g, m_od_sgu_ln_b, m_od_spatial_w,
                           m_od_spatial_b, m_od_w_out, m_ffn_norm_g, m_ffn_w_gate, m_ffn_w_up, m_ffn_w_down,
                           m_final_norm_g]))
    vel = dict(zip(names, [v_ev_norm_g, v_ev_w_in, v_ev_sinks, v_ev_conv_w, v_ev_conv_b, v_ev_conv_ln_g, v_ev_conv_ln_b,
                           v_ev_w_out, v_od_norm_g, v_od_w_in, v_od_sgu_ln_g, v_od_sgu_ln_b, v_od_spatial_w,
                           v_od_spatial_b, v_od_w_out, v_ffn_norm_g, v_ffn_w_gate, v_ffn_w_up, v_ffn_w_down,
                           v_final_norm_g]))
    me = _my_rank()

    sp = jnp.zeros((40, LANES), F32)
    sp = sp.at[0:CONV_WIDTH, 0:64].set(ev_conv_w[0])
    sp = sp.at[32, :].set(od_norm_g[0])
    sp = sp.at[33, 0:64].set(od_sgu_ln_g[0])
    sp = sp.at[34, 0:64].set(od_sgu_ln_b[0])

    gather_order = [k for grp in GATHER_GROUPS for k in grp]
    ag_handles, ag_token = exchange_start([sp] + [wts[n][l].astype(ACT) for n, l in gather_order], False, "ag_start")
    full_w = {}
    P = {
        "ev_norm_g": ev_norm_g + ag_token[0:1, 0:1], "ev_sinks": ev_sinks, "ev_conv_b": ev_conv_b,
        "ev_conv_ln_g": ev_conv_ln_g, "ev_conv_ln_b": ev_conv_ln_b,
        "od_spatial_w": od_spatial_w, "od_spatial_b": od_spatial_b, "ffn_norm_g": ffn_norm_g,
        "final_norm_g": final_norm_g.reshape(1, D_MODEL),
    }

    def weight(name, layer, after):
        if (name, layer) not in full_w:
            gi = [i for i, grp in enumerate(GATHER_GROUPS) if (name, layer) in grp][0]
            idx = [1 + gather_order.index(k) for k in GATHER_GROUPS[gi]]
            if gi == 0:
                idx = [0] + idx
            lands = exchange_wait(tuple([h[i] for i in idx] for h in ag_handles), after, False, "ag_wait%d" % gi)
            if gi == 0:
                spg, lands = lands[0], lands[1:]
                P["ev_conv_w"] = jnp.moveaxis(spg[:, 0:CONV_WIDTH, 0:64], 0, 1).reshape(CONV_WIDTH, CONV_CH)
                P["od_norm_g"] = spg[:, 32, :].reshape(1, D_MODEL)
                P["od_sgu_ln_g"] = spg[:, 33, 0:64].reshape(1, D_CH)
                P["od_sgu_ln_b"] = spg[:, 34, 0:64].reshape(1, D_CH)
            for k, land in zip(GATHER_GROUPS[gi], lands):
                full_w[k] = _unshard(k[0], land)
        return full_w[(name, layer)]

    pending, rs_started, spatial = {}, [], []

    def emit(name, layer, grad):
        if name == "od_spatial":
            handles, token = exchange_start([grad], False, "ar_start_b")
            spatial.append(handles)
            return token
        pending[(name, layer)] = grad
        for gi, grp in enumerate(REDUCE_GROUPS):
            if (name, layer) in grp and all(k in pending for k in grp):
                handles, token = exchange_start([_shard_slots(k[0], pending[k]).astype(ACT) for k in grp], True,
                                                "rs_start%d" % gi)
                rs_started.append((gi, handles))
                return token
        return None

    loss_part, dx, small = _local_step(x[0], loss_target[0], weight, emit, P)
    loss = lax.psum(loss_part[0, 0], AXES)

    wide = ["ev_norm_g", "ev_sinks", "ev_conv_w", "ev_conv_b", "ev_conv_ln_g", "ev_conv_ln_b", "od_norm_g",
            "od_sgu_ln_g", "od_sgu_ln_b", "ffn_norm_g", "final_norm_g"]
    blk_a = jnp.concatenate(
        [jnp.pad(small[n], ((0, 0), (0, D_MODEL - small[n].shape[1]))) for n in wide], axis=0)
    blk_a = jnp.pad(blk_a, ((0, 48 - blk_a.shape[0]), (0, 0)))
    ar_a, ar_token = exchange_start([blk_a], False, "ar_start_a")

    results = {}
    for gi, handles in rs_started:
        lands = exchange_wait(handles, ar_token, True, "rs_wait%d" % gi)
        for (n, l), land in zip(REDUCE_GROUPS[gi], lands):
            results[n] = sum_adamw(land, wts[n], mom[n], vel[n], l, results.get(n), "adamw_%s%d" % (n, l))
    out_g, out_d, out_m, out_v = {}, {}, {}, {}
    for n in BIG:
        out_g[n], out_d[n], out_m[n], out_v[n] = results[n]

    last = results[REDUCE_GROUPS[-1][-1][0]][0]
    sum_b = sum_slots(exchange_wait(spatial[0], last, False, "ar_wait_b")[0], F32, "ar_sum_b")
    sum_a = sum_slots(exchange_wait(ar_a, last, False, "ar_wait_a")[0], F32, "ar_sum_a")
    full = {}
    off = 0
    for n in wide:
        r_, c_ = small[n].shape
        full[n] = sum_a[off:off + r_, 0:c_]
        off += r_
    full["od_spatial_w"] = sum_b[0:D_GROUPS * BLOCK]
    full["od_spatial_b"] = sum_b[D_GROUPS * BLOCK:D_GROUPS * BLOCK + D_GROUPS]
    full["ev_conv_w"] = lax.dynamic_slice_in_dim(full["ev_conv_w"], me * 64, 64, axis=1)
    full["od_norm_g"] = lax.dynamic_slice_in_dim(full["od_norm_g"], me * 128, 128, axis=1)
    full["od_sgu_ln_g"] = lax.dynamic_slice_in_dim(full["od_sgu_ln_g"], me * 64, 64, axis=1)
    full["od_sgu_ln_b"] = lax.dynamic_slice_in_dim(full["od_sgu_ln_b"], me * 64, 64, axis=1)

    small_names = [n for n in names if n not in BIG]
    view = {n: ((-1, wts[n].shape[-1]) if wts[n].ndim > 1 else (1, -1)) for n in small_names}
    ds, ms, vs = adamw_small([wts[n].reshape(view[n]) for n in small_names],
                             [full[n].reshape(view[n]) for n in small_names],
                             [mom[n].reshape(view[n]) for n in small_names],
                             [vel[n].reshape(view[n]) for n in small_names], "adamw_small")
    for i, n in enumerate(small_names):
        shp = wts[n].shape
        out_g[n], out_d[n], out_m[n], out_v[n] = (full[n].reshape(shp), ds[i].reshape(shp), ms[i].reshape(shp),
                                                  vs[i].reshape(shp))

    return (loss, dx[None], *[out_g[n] for n in names], *[out_d[n] for n in names],
            *[out_m[n] for n in names], *[out_v[n] for n in names])
```

```python
import functools
import math

import jax
import jax.numpy as jnp
from jax import lax
from jax.experimental import pallas as pl
from jax.experimental.pallas import tpu as pltpu

F32 = jnp.float32
ACT = jnp.bfloat16

D_MODEL = 1024
HEAD_DIM = 64
N_HEADS = 8
ATT_W = N_HEADS * HEAD_DIM
A_KV_HEADS = 2
CONV_CH = 512
CONV_WIDTH = 31
CONV_HALO = 32
D_CH = 512
D_GROUPS = 8
BLOCK = 128
D_FF = 2816
ROT_DIM = 16
ROPE_THETA = 500000.0
RMS_EPS = 1e-6
LN_EPS = 1e-5
DILATED = ((128, 1), (512, 4), (2048, 16))
NEG = -1e30
LANES = 128

ADAM_LR = 0.001
ADAM_B1 = 0.9
ADAM_B2 = 0.999
ADAM_EPS = 1e-08
ADAM_WD = 0.01
ADAM_STEP = 10

V7X_VMEM_LIMIT = 56 * 1024 * 1024
SMALL_BLOCK_BYTES = 6 * 1024 * 1024
N_DEV = 8

NN = (((1,), (0,)), ((), ()))
NT = (((1,), (1,)), ((), ()))
TN = (((0,), (0,)), ((), ()))
MESH = pl.DeviceIdType.MESH
AXES = ("x", "y", "c")


def _dot(a, b, dims):
    return lax.dot_general(a, b, dims, preferred_element_type=F32)


def _cp(*sem):
    return pltpu.CompilerParams(dimension_semantics=sem if sem else None,
                                vmem_limit_bytes=V7X_VMEM_LIMIT)


def _tile(n, prefs):
    for p in prefs:
        if n % p == 0:
            return p
    return n


def _sigmoid(x):
    return 1.0 / (1.0 + jnp.exp(-x))


def _rowspec(tm, w, col=0):
    return pl.BlockSpec((tm, w), lambda i, col=col: (i, col))


def _fullspec(shape):
    nd = len(shape)
    return pl.BlockSpec(shape, lambda *a, nd=nd: (0,) * nd)


def _rope_tables(seq):
    half = ROT_DIM // 2
    inv_freq = ROPE_THETA ** (-jnp.arange(half, dtype=F32) * (2.0 / ROT_DIM))
    ang = jnp.arange(seq, dtype=jnp.int32).astype(F32)[:, None] * inv_freq[None, :]
    cos, sin = jnp.cos(ang), jnp.sin(ang)
    lane = jnp.arange(LANES)
    jm = lane % HEAD_DIM
    idx = jm % half
    c = jnp.where(jm[None, :] < ROT_DIM, cos[:, idx], 1.0)
    sa = jnp.where(jm[None, :] < half, -sin[:, idx], 0.0)
    sb = jnp.where((jm[None, :] >= half) & (jm[None, :] < ROT_DIM), sin[:, idx], 0.0)
    return c.astype(F32), sa.astype(F32), sb.astype(F32)


def _rope(x, c, sa, sb):
    return x * c + pltpu.roll(x, LANES - 8, 1) * sa + pltpu.roll(x, 8, 1) * sb


def _rope_t(d, c, sa, sb):
    return d * c + pltpu.roll(d * sa, 8, 1) + pltpu.roll(d * sb, LANES - 8, 1)


def rmsnorm(h, g, name):
    s = h.shape[0]
    tm = _tile(s, (512,))

    def body(h_ref, g_ref, o_ref):
        x = h_ref[...]
        r = lax.rsqrt(jnp.mean(x * x, axis=-1, keepdims=True) + RMS_EPS)
        o_ref[...] = (x * r * g_ref[...]).astype(o_ref.dtype)

    return pl.pallas_call(
        body, grid=(s // tm,), name=name,
        in_specs=[_rowspec(tm, D_MODEL), _fullspec((1, D_MODEL))],
        out_specs=_rowspec(tm, D_MODEL),
        out_shape=jax.ShapeDtypeStruct((s, D_MODEL), ACT),
        compiler_params=_cp("parallel"))(h, g)


def inproj(n, w, tabs, nqk, splits, name):
    s = n.shape[0]
    ntot = w.shape[1]
    assert sum(splits) == ntot and splits[0] == nqk
    tm = _tile(s, (256,))

    def body(n_ref, w_ref, c_ref, sa_ref, sb_ref, *outs):
        res = _dot(n_ref[...], w_ref[...], NN)
        c, sa, sb = c_ref[...], sa_ref[...], sb_ref[...]
        for g in range(nqk // LANES):
            x = res[:, g * LANES:(g + 1) * LANES]
            outs[0][:, g * LANES:(g + 1) * LANES] = _rope(x, c, sa, sb).astype(ACT)
        off = nqk
        for o_ref, wd in zip(outs[1:], splits[1:]):
            o_ref[...] = res[:, off:off + wd].astype(ACT)
            off += wd

    return pl.pallas_call(
        body, grid=(s // tm,), name=name,
        in_specs=[_rowspec(tm, D_MODEL), _fullspec((D_MODEL, ntot))] + [_rowspec(tm, LANES)] * 3,
        out_specs=[_rowspec(tm, wd) for wd in splits],
        out_shape=[jax.ShapeDtypeStruct((s, wd), ACT) for wd in splits],
        compiler_params=_cp("parallel"))(n, w, *tabs)


def ffn_up(n, wg, wu, name):
    s = n.shape[0]
    f = wg.shape[1]
    tm = _tile(s, (512,))
    tf = _tile(f, (1408, 512, 256, 128))

    def body(n_ref, wg_ref, wu_ref, g_ref, u_ref, a_ref):
        a = n_ref[...]
        g = _dot(a, wg_ref[...], NN)
        u = _dot(a, wu_ref[...], NN)
        g_ref[...] = g.astype(ACT)
        u_ref[...] = u.astype(ACT)
        a_ref[...] = (g * _sigmoid(g) * u).astype(ACT)

    wspec = pl.BlockSpec((D_MODEL, tf), lambda j, i: (0, j))
    ospec = pl.BlockSpec((tm, tf), lambda j, i: (i, j))
    return pl.pallas_call(
        body, grid=(f // tf, s // tm), name=name,
        in_specs=[pl.BlockSpec((tm, D_MODEL), lambda j, i: (i, 0)), wspec, wspec],
        out_specs=[ospec] * 3,
        out_shape=[jax.ShapeDtypeStruct((s, f), ACT)] * 3,
        compiler_params=_cp("parallel", "parallel"))(n, wg, wu)


def mm_res(parts, w, h, gnext, name):
    s = h.shape[0]
    tm = _tile(s, (256,))
    widths = [p.shape[1] for p in parts]
    assert sum(widths) == w.shape[0]
    np_ = len(parts)

    def body(*refs):
        p_refs = refs[:np_]
        w_ref, h_ref = refs[np_], refs[np_ + 1]
        rest = refs[np_ + 2:]
        acc = h_ref[...]
        off = 0
        for p_ref, wd in zip(p_refs, widths):
            acc = acc + _dot(p_ref[...], w_ref[off:off + wd, :], NN)
            off += wd
        if gnext is None:
            rest[0][...] = acc
        else:
            g_ref, ho_ref, no_ref = rest
            ho_ref[...] = acc
            r = lax.rsqrt(jnp.mean(acc * acc, axis=-1, keepdims=True) + RMS_EPS)
            no_ref[...] = (acc * r * g_ref[...]).astype(ACT)

    in_specs = [_rowspec(tm, wd) for wd in widths] + [_fullspec(w.shape), _rowspec(tm, D_MODEL)]
    args = list(parts) + [w, h]
    out_specs = [_rowspec(tm, D_MODEL)]
    out_shape = [jax.ShapeDtypeStruct((s, D_MODEL), F32)]
    if gnext is not None:
        in_specs.append(_fullspec((1, D_MODEL)))
        args.append(gnext)
        out_specs.append(_rowspec(tm, D_MODEL))
        out_shape.append(jax.ShapeDtypeStruct((s, D_MODEL), ACT))
    out = pl.pallas_call(
        body, grid=(s // tm,), name=name, in_specs=in_specs, out_specs=out_specs,
        out_shape=out_shape, compiler_params=_cp("parallel"))(*args)
    return (out[0], None) if gnext is None else (out[0], out[1])


def _band_mask(n, max_dist):
    qi = lax.broadcasted_iota(jnp.int32, (BLOCK, 2 * BLOCK), 0)
    kj = lax.broadcasted_iota(jnp.int32, (BLOCK, 2 * BLOCK), 1)
    dist = qi + BLOCK - kj
    valid = jnp.logical_and(dist >= 0, dist <= max_dist)
    return jnp.logical_and(valid, jnp.logical_or(kj >= BLOCK, n > 0))


def _fold(a, d):
    return a.reshape(a.shape[0] // d, d * a.shape[1])


def attn_fwd(qk, v, *, d, hkv, max_dist, sink, out_dtype, name):
    s = qk.shape[0]
    kvw = hkv * HEAD_DIM
    wqk = ATT_W + kvw
    assert qk.shape[1] == wqk and (d == 1 or (wqk % ATT_W == 0 and wqk % kvw == 0))
    assert sink is None or max_dist == BLOCK - 1
    nb = s // d // BLOCK
    grp = N_HEADS // hkv
    qpb, kpb, koff = wqk // ATT_W, wqk // kvw, ATT_W // kvw

    def body(*refs):
        if sink is None:
            q_ref, kc_ref, kp_ref, vc_ref, vp_ref, o_ref, l_ref, o_buf = refs
        else:
            q_ref, kc_ref, kp_ref, vc_ref, vp_ref, s_ref, o_ref, l_ref, o_buf = refs
        n = pl.program_id(1)
        valid = _band_mask(n, max_dist)
        qsl = [slice(h * HEAD_DIM, (h + 1) * HEAD_DIM) for h in range(N_HEADS)]
        ksl = [slice((h // grp) * HEAD_DIM, (h // grp + 1) * HEAD_DIM) for h in range(N_HEADS)]
        kk = jnp.concatenate([kp_ref[...], kc_ref[...]], axis=0)
        vv = jnp.concatenate([vp_ref[...], vc_ref[...]], axis=0)
        scores = []
        for h in range(N_HEADS):
            q = q_ref[:, qsl[h]] * 0.125
            scores.append(_dot(q, kk[:, ksl[h]], NT))
        probs = []
        lane = lax.broadcasted_iota(jnp.int32, (BLOCK, LANES), 1)
        lse_tile = jnp.zeros((BLOCK, LANES), F32)
        if sink is not None:
            sink_col = lax.broadcasted_iota(jnp.int32, (BLOCK, 2 * BLOCK), 1) == 0
        for h in range(N_HEADS):
            sc = jnp.where(valid, scores[h], NEG)
            if sink is not None:
                sc = jnp.where(sink_col, s_ref[h:h + 1, 0:1], sc)
            m = jnp.max(sc, axis=-1, keepdims=True)
            p = jnp.exp(sc - m)
            l = jnp.sum(p, axis=-1, keepdims=True)
            if sink is not None:
                p = jnp.where(sink_col, 0.0, p)
            lse_tile = jnp.where(lane == h, m + jnp.log(l), lse_tile)
            probs.append((p.astype(ACT), 1.0 / l))
        l_ref[...] = lse_tile
        for h in range(N_HEADS):
            p, rl = probs[h]
            o_buf[:, qsl[h]] = _dot(p, vv[:, ksl[h]], NN) * rl
        o_ref[...] = o_buf[...].astype(o_ref.dtype)

    prev = lambda n: jnp.maximum(n - 1, 0)
    in_specs = [
        pl.BlockSpec((BLOCK, ATT_W), lambda r, n: (n, r * qpb)),
        pl.BlockSpec((BLOCK, kvw), lambda r, n: (n, r * kpb + koff)),
        pl.BlockSpec((BLOCK, kvw), lambda r, n: (prev(n), r * kpb + koff)),
        pl.BlockSpec((BLOCK, kvw), lambda r, n: (n, r)),
        pl.BlockSpec((BLOCK, kvw), lambda r, n: (prev(n), r)),
    ]
    qkf, vf = _fold(qk, d), _fold(v, d)
    args = [qkf, qkf, qkf, vf, vf]
    if sink is not None:
        in_specs.append(_fullspec((N_HEADS, LANES)))
        args.append(sink)
    ospec = pl.BlockSpec((BLOCK, ATT_W), lambda r, n: (n, r))
    lspec = pl.BlockSpec((BLOCK, LANES), lambda r, n: (n, r))
    o, lse = pl.pallas_call(
        body, grid=(d, nb), name=name, in_specs=in_specs, out_specs=[ospec, lspec],
        out_shape=[jax.ShapeDtypeStruct((s // d, d * ATT_W), out_dtype),
                   jax.ShapeDtypeStruct((s // d, d * LANES), F32)],
        scratch_shapes=[pltpu.VMEM((BLOCK, ATT_W), F32)],
        compiler_params=_cp("parallel", "parallel"))(*args)
    return o.reshape(s, ATT_W), lse.reshape(s, LANES)


def attn_bwd(qk, v, do_src, do_col, o, lse, *, d, hkv, max_dist, sink, out_dtype, name):
    s = qk.shape[0]
    kvw = hkv * HEAD_DIM
    wqk = ATT_W + kvw
    nb = s // d // BLOCK
    grp = N_HEADS // hkv
    qpb, kpb, koff = wqk // ATT_W, wqk // kvw, ATT_W // kvw
    dob = do_src.shape[1] // ATT_W
    has_sink = sink is not None

    def body(*refs):
        refs = list(refs)
        q_ref, kc_ref, kp_ref, vc_ref, vp_ref, do_ref, o_ref, l_ref = refs[:8]
        pos = 8
        if has_sink:
            s_ref = refs[pos]
            pos += 1
        dq_ref, dk_ref, dv_ref = refs[pos:pos + 3]
        pos += 3
        if has_sink:
            ds_ref = refs[pos]
            pos += 1
        ck_ref, cv_ref, dq_buf, dk_buf, dv_buf = refs[pos:pos + 5]
        r_id = pl.program_id(0)
        n = pl.program_id(1)

        @pl.when(n == 0)
        def _():
            ck_ref[...] = jnp.zeros_like(ck_ref)
            cv_ref[...] = jnp.zeros_like(cv_ref)

        if has_sink:
            @pl.when(jnp.logical_and(n == 0, r_id == 0))
            def _():
                ds_ref[...] = jnp.zeros_like(ds_ref)

        @pl.when(n < nb)
        def _():
            valid = _band_mask(n, max_dist)
            qsl = [slice(h * HEAD_DIM, (h + 1) * HEAD_DIM) for h in range(N_HEADS)]
            ksl = [slice((h // grp) * HEAD_DIM, (h // grp + 1) * HEAD_DIM) for h in range(N_HEADS)]
            kk = jnp.concatenate([kp_ref[...], kc_ref[...]], axis=0)
            vv = jnp.concatenate([vp_ref[...], vc_ref[...]], axis=0)
            qs, first = [], []
            for h in range(N_HEADS):
                q = q_ref[:, qsl[h]] * 0.125
                qs.append(q)
                first.append((_dot(q, kk[:, ksl[h]], NT), _dot(do_ref[:, qsl[h]], vv[:, ksl[h]], NT)))
            mid = []
            for h in range(N_HEADS):
                sc, dp = first[h]
                lse_h = l_ref[:, h:h + 1]
                p = jnp.exp(jnp.where(valid, sc, NEG) - lse_h)
                e = jnp.sum(do_ref[:, qsl[h]].astype(F32) * o_ref[:, qsl[h]].astype(F32), axis=-1, keepdims=True)
                mid.append(((p * (dp - e)).astype(ACT), p.astype(ACT)))
                if has_sink:
                    sk = s_ref[h:h + 1, 0:1]
                    dsk = -jnp.sum(jnp.exp(sk - lse_h) * e, axis=0, keepdims=True)
                    ds_ref[h:h + 1, :] = ds_ref[h:h + 1, :] + dsk
            dkk = [None] * hkv
            dvv = [None] * hkv
            for h in range(N_HEADS):
                kh = h // grp
                ds, p = mid[h]
                dq_buf[:, qsl[h]] = _dot(ds, kk[:, ksl[h]], NN) * 0.125
                for lst, val in ((dkk, _dot(ds, qs[h], TN)), (dvv, _dot(p, do_ref[:, qsl[h]], TN))):
                    lst[kh] = val if lst[kh] is None else lst[kh] + val
            for kh in range(hkv):
                ks = slice(kh * HEAD_DIM, (kh + 1) * HEAD_DIM)
                dk_buf[:, ks] = ck_ref[:, ks] + dkk[kh][:BLOCK]
                dv_buf[:, ks] = cv_ref[:, ks] + dvv[kh][:BLOCK]
                ck_ref[:, ks] = dkk[kh][BLOCK:]
                cv_ref[:, ks] = dvv[kh][BLOCK:]
            dq_ref[...] = dq_buf[...].astype(dq_ref.dtype)
            dk_ref[...] = dk_buf[...].astype(dk_ref.dtype)
            dv_ref[...] = dv_buf[...].astype(dv_ref.dtype)

        @pl.when(n == nb)
        def _():
            dk_ref[...] = ck_ref[...].astype(dk_ref.dtype)
            dv_ref[...] = cv_ref[...].astype(dv_ref.dtype)

    qrow = lambda n: jnp.minimum(n, nb - 1)
    prow = lambda n: jnp.maximum(jnp.minimum(n, nb - 1) - 1, 0)
    krow = lambda n: jnp.maximum(n - 1, 0)
    in_specs = [
        pl.BlockSpec((BLOCK, ATT_W), lambda r, n: (qrow(n), r * qpb)),
        pl.BlockSpec((BLOCK, kvw), lambda r, n: (qrow(n), r * kpb + koff)),
        pl.BlockSpec((BLOCK, kvw), lambda r, n: (prow(n), r * kpb + koff)),
        pl.BlockSpec((BLOCK, kvw), lambda r, n: (qrow(n), r)),
        pl.BlockSpec((BLOCK, kvw), lambda r, n: (prow(n), r)),
        pl.BlockSpec((BLOCK, ATT_W), lambda r, n: (qrow(n), r * dob + do_col)),
        pl.BlockSpec((BLOCK, ATT_W), lambda r, n: (qrow(n), r)),
        pl.BlockSpec((BLOCK, LANES), lambda r, n: (qrow(n), r)),
    ]
    qkf, vf = _fold(qk, d), _fold(v, d)
    args = [qkf, qkf, qkf, vf, vf, _fold(do_src, d), _fold(o, d), _fold(lse, d)]
    if has_sink:
        in_specs.append(_fullspec((N_HEADS, LANES)))
        args.append(sink)
    qspec = pl.BlockSpec((BLOCK, ATT_W), lambda r, n: (qrow(n), r))
    kspec = pl.BlockSpec((BLOCK, kvw), lambda r, n: (krow(n), r))
    out_specs = [qspec, kspec, kspec]
    out_shape = [jax.ShapeDtypeStruct((s // d, d * ATT_W), out_dtype),
                 jax.ShapeDtypeStruct((s // d, d * kvw), out_dtype),
                 jax.ShapeDtypeStruct((s // d, d * kvw), out_dtype)]
    if has_sink:
        out_specs.append(_fullspec((N_HEADS, LANES)))
        out_shape.append(jax.ShapeDtypeStruct((N_HEADS, LANES), F32))
    out = pl.pallas_call(
        body, grid=(d, nb + 1), name=name, in_specs=in_specs, out_specs=out_specs,
        out_shape=out_shape,
        scratch_shapes=[pltpu.VMEM((BLOCK, kvw), F32), pltpu.VMEM((BLOCK, kvw), F32),
                        pltpu.VMEM((BLOCK, ATT_W), F32), pltpu.VMEM((BLOCK, kvw), F32), pltpu.VMEM((BLOCK, kvw), F32)],
        compiler_params=_cp("arbitrary", "arbitrary"))(*args)
    res = [out[0].reshape(s, ATT_W), out[1].reshape(s, kvw), out[2].reshape(s, kvw)]
    if has_sink:
        res.append(out[3])
    return res


def combine_fwd(os_, lses, name):
    s = os_[0].shape[0]
    tm = _tile(s, (512,))

    def body(o1, o2, o3, l1, l2, l3, c_ref, l_ref, c_buf):
        a, b, c = l1[...], l2[...], l3[...]
        m = jnp.maximum(jnp.maximum(a, b), c)
        wa, wb, wc = jnp.exp(a - m), jnp.exp(b - m), jnp.exp(c - m)
        tot = wa + wb + wc
        l_ref[...] = m + jnp.log(tot)
        rt = 1.0 / tot
        wa, wb, wc = wa * rt, wb * rt, wc * rt
        for h in range(N_HEADS):
            cs = slice(h * HEAD_DIM, (h + 1) * HEAD_DIM)
            c_buf[:, cs] = (wa[:, h:h + 1] * o1[:, cs] + wb[:, h:h + 1] * o2[:, cs] + wc[:, h:h + 1] * o3[:, cs])
        c_ref[...] = c_buf[...].astype(ACT)

    spec = _rowspec(tm, ATT_W)
    lspec = _rowspec(tm, LANES)
    return pl.pallas_call(
        body, grid=(s // tm,), name=name, in_specs=[spec] * 3 + [lspec] * 3, out_specs=[spec, lspec],
        out_shape=[jax.ShapeDtypeStruct((s, ATT_W), ACT), jax.ShapeDtypeStruct((s, LANES), F32)],
        scratch_shapes=[pltpu.VMEM((tm, ATT_W), F32)],
        compiler_params=_cp("parallel"))(*os_, *lses)


def assemble(parts, tabs, name):
    s = parts[0][0][0].shape[0]
    tm = _tile(s, (256,))
    widths = [terms[0].shape[1] for terms, _ in parts]
    counts = [len(terms) for terms, _ in parts]
    flags = [f for _, f in parts]
    flat = [t for terms, _ in parts for t in terms]

    def body(*refs):
        c_ref, sa_ref, sb_ref, o_ref = refs[len(flat):]
        c, sa, sb = c_ref[...], sa_ref[...], sb_ref[...]
        off = 0
        first = 0
        for wd, cnt, fl in zip(widths, counts, flags):
            t_refs = refs[first:first + cnt]
            first += cnt
            for g in range(wd // LANES):
                cols = slice(g * LANES, (g + 1) * LANES)
                x = t_refs[0][:, cols].astype(F32)
                for t_ref in t_refs[1:]:
                    x = x + t_ref[:, cols].astype(F32)
                if fl:
                    x = _rope_t(x, c, sa, sb)
                o_ref[:, off + g * LANES:off + (g + 1) * LANES] = x.astype(ACT)
            off += wd

    tot = sum(widths)
    return pl.pallas_call(
        body, grid=(s // tm,), name=name,
        in_specs=[_rowspec(tm, wd) for wd, cnt in zip(widths, counts) for _ in range(cnt)]
        + [_rowspec(tm, LANES)] * 3,
        out_specs=_rowspec(tm, tot), out_shape=jax.ShapeDtypeStruct((s, tot), ACT),
        compiler_params=_cp("parallel"))(*flat, *tabs)


def _ln_stats(x):
    mu = jnp.mean(x, axis=-1, keepdims=True)
    xc = x - mu
    var = jnp.mean(xc * xc, axis=-1, keepdims=True)
    rstd = lax.rsqrt(var + LN_EPS)
    return xc * rstd, rstd


SUBLANES = 8


TAP_ROWS = 32


def _tap_sum(buf, cw_ref, offsets, tm, res_ref):
    for r0 in range(0, tm, TAP_ROWS):
        acc = None
        for ph in range(SUBLANES):
            taps = [j for j, off in enumerate(offsets) if off % SUBLANES == ph]
            if not taps:
                continue
            rows = TAP_ROWS if ph == 0 else TAP_ROWS + SUBLANES
            part = None
            for j in taps:
                term = cw_ref[j:j + 1, :] * buf[pl.ds(offsets[j] - ph + r0, rows), :]
                part = term if part is None else part + term
            part = part[ph:ph + TAP_ROWS]
            acc = part if acc is None else acc + part
        res_ref[pl.ds(r0, TAP_ROWS), :] = acc


def conv_fwd(ga, gb, cw, cb, lg, lb, name):
    s = ga.shape[0]
    tm = _tile(s, (256,))
    hb = tm // CONV_HALO

    def body(ga_ref, gb_ref, gah_ref, gbh_ref, cw_ref, cb_ref, lg_ref, lb_ref, c_ref, c1_ref, buf):
        i = pl.program_id(0)
        halo = gah_ref[...].astype(F32) * _sigmoid(gbh_ref[...].astype(F32))
        buf[0:CONV_HALO, :] = jnp.where(i > 0, halo, 0.0)
        buf[CONV_HALO:, :] = ga_ref[...].astype(F32) * _sigmoid(gb_ref[...].astype(F32))
        first = CONV_HALO - (CONV_WIDTH - 1)
        _tap_sum(buf, cw_ref, [first + j for j in range(CONV_WIDTH)], tm, c1_ref)
        acc = c1_ref[...] + cb_ref[...]
        c1_ref[...] = acc
        xh, _ = _ln_stats(acc)
        y = xh * lg_ref[...] + lb_ref[...]
        c_ref[...] = (y * _sigmoid(y)).astype(ACT)

    hspec = pl.BlockSpec((CONV_HALO, CONV_CH), lambda i: (jnp.maximum(i * hb - 1, 0), 0))
    vec = _fullspec((1, CONV_CH))
    spec = _rowspec(tm, CONV_CH)
    return pl.pallas_call(
        body, grid=(s // tm,), name=name,
        in_specs=[spec, spec, hspec, hspec, _fullspec((CONV_WIDTH, CONV_CH)), vec, vec, vec],
        out_specs=[spec, spec],
        out_shape=[jax.ShapeDtypeStruct((s, CONV_CH), ACT), jax.ShapeDtypeStruct((s, CONV_CH), F32)],
        scratch_shapes=[pltpu.VMEM((tm + CONV_HALO, CONV_CH), F32)],
        compiler_params=_cp("parallel"))(ga, gb, ga, gb, cw, cb, lg, lb)


def conv_bwd_ln(c1, dsrc, dcol, lg, lb, name):
    s = c1.shape[0]
    tm = _tile(s, (256,))

    def body(c1_ref, d_ref, lg_ref, lb_ref, o_ref, dg_ref, db_ref):
        @pl.when(pl.program_id(0) == 0)
        def _():
            dg_ref[...] = jnp.zeros_like(dg_ref)
            db_ref[...] = jnp.zeros_like(db_ref)

        xh, rstd = _ln_stats(c1_ref[...].astype(F32))
        y = xh * lg_ref[...] + lb_ref[...]
        sg = _sigmoid(y)
        dy = d_ref[...].astype(F32) * (sg * (1.0 + y * (1.0 - sg)))
        dg_ref[0:1, :] = dg_ref[0:1, :] + jnp.sum(dy * xh, axis=0, keepdims=True)
        db_ref[0:1, :] = db_ref[0:1, :] + jnp.sum(dy, axis=0, keepdims=True)
        dxh = dy * lg_ref[...]
        o_ref[...] = rstd * (dxh - jnp.mean(dxh, axis=-1, keepdims=True)
                             - xh * jnp.mean(dxh * xh, axis=-1, keepdims=True))

    vec = _fullspec((1, CONV_CH))
    acc = _fullspec((8, CONV_CH))
    return pl.pallas_call(
        body, grid=(s // tm,), name=name,
        in_specs=[_rowspec(tm, CONV_CH), _rowspec(tm, CONV_CH, dcol), vec, vec],
        out_specs=[_rowspec(tm, CONV_CH), acc, acc],
        out_shape=[jax.ShapeDtypeStruct((s, CONV_CH), F32)] + [jax.ShapeDtypeStruct((8, CONV_CH), F32)] * 2,
        compiler_params=_cp("arbitrary"))(c1, dsrc, lg, lb)


def conv_bwd_conv(dc1, ga, gb, cw, name):
    s = ga.shape[0]
    tm = _tile(s, (256,))
    hb = tm // CONV_HALO
    nt = s // tm
    last_h = s // CONV_HALO - 1
    first = CONV_HALO - (CONV_WIDTH - 1)


    def rows8(x):
        return jnp.sum(x.reshape(x.shape[0] // SUBLANES, SUBLANES, CONV_CH), axis=0)

    def body(d_ref, dn_ref, ga_ref, gb_ref, gah_ref, gbh_ref, cw_ref,
             dga_ref, dgb_ref, dw_ref, db_ref, dbuf, cbuf, sbuf):
        i = pl.program_id(0)

        @pl.when(i == 0)
        def _():
            dw_ref[...] = jnp.zeros_like(dw_ref)
            db_ref[...] = jnp.zeros_like(db_ref)

        d = d_ref[...]
        dbuf[0:tm, :] = d
        dbuf[tm:, :] = jnp.where(i < nt - 1, dn_ref[...], 0.0)
        halo = gah_ref[...].astype(F32) * _sigmoid(gbh_ref[...].astype(F32))
        cbuf[0:CONV_HALO, :] = jnp.where(i > 0, halo, 0.0)
        a = ga_ref[...].astype(F32)
        sg = _sigmoid(gb_ref[...].astype(F32))
        cbuf[CONV_HALO:, :] = a * sg
        for ph in range(SUBLANES):
            taps = [j for j in range(CONV_WIDTH) if (first + j) % SUBLANES == ph]
            if ph:
                sbuf[0:tm + CONV_HALO - SUBLANES, :] = cbuf[pl.ds(ph, tm + CONV_HALO - SUBLANES), :]
            src = sbuf if ph else cbuf
            for r0 in range(0, tm, TAP_ROWS):
                d_blk = dbuf[pl.ds(r0, TAP_ROWS), :]
                for j in taps:
                    tap = src[pl.ds(first + j - ph + r0, TAP_ROWS), :]
                    rows = slice(j * SUBLANES, (j + 1) * SUBLANES)
                    dw_ref[rows, :] = dw_ref[rows, :] + rows8(d_blk * tap)
        db_ref[...] = db_ref[...] + rows8(d)
        _tap_sum(dbuf, cw_ref, [CONV_WIDTH - 1 - j for j in range(CONV_WIDTH)], tm, sbuf)
        dc0 = sbuf[0:tm, :]
        dga_ref[...] = (dc0 * sg).astype(ACT)
        dgb_ref[...] = (dc0 * a * sg * (1.0 - sg)).astype(ACT)

    spec = _rowspec(tm, CONV_CH)
    hprev = pl.BlockSpec((CONV_HALO, CONV_CH), lambda i: (jnp.maximum(i * hb - 1, 0), 0))
    hnext = pl.BlockSpec((CONV_HALO, CONV_CH), lambda i: (jnp.minimum((i + 1) * hb, last_h), 0))
    return pl.pallas_call(
        body, grid=(nt,), name=name,
        in_specs=[spec, hnext, spec, spec, hprev, hprev, _fullspec((CONV_WIDTH, CONV_CH))],
        out_specs=[spec, spec, _fullspec((CONV_HALO * SUBLANES, CONV_CH)), _fullspec((SUBLANES, CONV_CH))],
        out_shape=[jax.ShapeDtypeStruct((s, CONV_CH), ACT)] * 2
        + [jax.ShapeDtypeStruct((CONV_HALO * SUBLANES, CONV_CH), F32), jax.ShapeDtypeStruct((SUBLANES, CONV_CH), F32)],
        scratch_shapes=[pltpu.VMEM((tm + CONV_HALO, CONV_CH), F32)] * 3,
        compiler_params=_cp("arbitrary"))(dc1, dc1, ga, gb, ga, gb, cw)


_GELU_K = math.sqrt(2.0 / math.pi)
_GELU_C = 0.044715


def _gelu(x):
    return 0.5 * x * (1.0 + jnp.tanh(_GELU_K * (x + _GELU_C * x * x * x)))


def _gelu_grad(x):
    t = jnp.tanh(_GELU_K * (x + _GELU_C * x * x * x))
    return 0.5 * (1.0 + t) + 0.5 * x * (1.0 - t * t) * _GELU_K * (1.0 + 3.0 * _GELU_C * x * x)


def _tril():
    qi = lax.broadcasted_iota(jnp.int32, (BLOCK, BLOCK), 0)
    kj = lax.broadcasted_iota(jnp.int32, (BLOCK, BLOCK), 1)
    return kj <= qi


GMLP_CHUNKS = 2


def _gmlp_weights(sw_ref, w_buf, wt_buf):
    tril = _tril()
    for g in range(D_GROUPS):
        w = jnp.where(tril, sw_ref[g], 0.0)
        w_buf[g] = w.astype(ACT)
        if wt_buf is not None:
            wt_buf[g] = w.T.astype(ACT)


def _gmlp_mix(w_buf, gn, sb_ref, m_buf):
    for c in range(GMLP_CHUNKS):
        rows = slice(c * BLOCK, (c + 1) * BLOCK)
        for g in range(D_GROUPS):
            cs = slice(g * HEAD_DIM, (g + 1) * HEAD_DIM)
            m_buf[rows, cs] = _dot(w_buf[g], gn[rows, cs], NN) + sb_ref[:, cs]


def gmlp_fwd(z, lg, lb, sw, sbx, name):
    s = z.shape[0]
    tm = GMLP_CHUNKS * BLOCK

    def body(z_ref, lg_ref, lb_ref, sw_ref, sb_ref, o_ref, w_buf, m_buf):
        @pl.when(pl.program_id(0) == 0)
        def _():
            _gmlp_weights(sw_ref, w_buf, None)

        zz = _gelu(z_ref[...].astype(F32))
        xh, _ = _ln_stats(zz[:, D_CH:])
        gn = (xh * lg_ref[...] + lb_ref[...]).astype(ACT)
        _gmlp_mix(w_buf, gn, sb_ref, m_buf)
        o_ref[...] = (zz[:, :D_CH] * m_buf[...]).astype(ACT)

    return pl.pallas_call(
        body, grid=(s // tm,), name=name,
        in_specs=[_rowspec(tm, 2 * D_CH), _fullspec((1, D_CH)), _fullspec((1, D_CH)),
                  _fullspec((D_GROUPS, BLOCK, BLOCK)), _fullspec((BLOCK, D_CH))],
        out_specs=_rowspec(tm, D_CH), out_shape=jax.ShapeDtypeStruct((s, D_CH), ACT),
        scratch_shapes=[pltpu.VMEM((D_GROUPS, BLOCK, BLOCK), ACT), pltpu.VMEM((tm, D_CH), F32)],
        compiler_params=_cp("arbitrary"))(z, lg, lb, sw, sbx)


def gmlp_bwd(z, dsrc, dcol, lg, lb, sw, sbx, name):
    s = z.shape[0]
    tm = GMLP_CHUNKS * BLOCK

    def body(z_ref, d_ref, lg_ref, lb_ref, sw_ref, sb_ref, dz_ref, dw_ref, dsb_ref, dg_ref, db_ref,
             w_buf, wt_buf, m_buf, dgn_buf):
        @pl.when(pl.program_id(0) == 0)
        def _():
            _gmlp_weights(sw_ref, w_buf, wt_buf)
            dw_ref[...] = jnp.zeros_like(dw_ref)
            dsb_ref[...] = jnp.zeros_like(dsb_ref)
            dg_ref[...] = jnp.zeros_like(dg_ref)
            db_ref[...] = jnp.zeros_like(db_ref)

        zr = z_ref[...].astype(F32)
        zz = _gelu(zr)
        u = zz[:, :D_CH]
        xh, rstd = _ln_stats(zz[:, D_CH:])
        gn = (xh * lg_ref[...] + lb_ref[...]).astype(ACT)
        dd = d_ref[...].astype(F32)
        _gmlp_mix(w_buf, gn, sb_ref, m_buf)
        dz_ref[:, :D_CH] = (dd * m_buf[...] * _gelu_grad(zr[:, :D_CH])).astype(ACT)
        dmix = dd * u
        dmix_a = dmix.astype(ACT)
        dsb = dmix[0:BLOCK]
        for c in range(1, GMLP_CHUNKS):
            dsb = dsb + dmix[c * BLOCK:(c + 1) * BLOCK]
        dsb_ref[...] = dsb_ref[...] + dsb
        tril = _tril()
        for g in range(D_GROUPS):
            cs = slice(g * HEAD_DIM, (g + 1) * HEAD_DIM)
            dw = None
            for c in range(GMLP_CHUNKS):
                rows = slice(c * BLOCK, (c + 1) * BLOCK)
                t = _dot(dmix_a[rows, cs], gn[rows, cs], NT)
                dw = t if dw is None else dw + t
                dgn_buf[rows, cs] = _dot(wt_buf[g], dmix_a[rows, cs], NN)
            dw_ref[g] = dw_ref[g] + jnp.where(tril, dw, 0.0)
        dgn = dgn_buf[...]
        dg_ref[0:1, :] = dg_ref[0:1, :] + jnp.sum(dgn * xh, axis=0, keepdims=True)
        db_ref[0:1, :] = db_ref[0:1, :] + jnp.sum(dgn, axis=0, keepdims=True)
        dxh = dgn * lg_ref[...]
        dgate = rstd * (dxh - jnp.mean(dxh, axis=-1, keepdims=True)
                        - xh * jnp.mean(dxh * xh, axis=-1, keepdims=True))
        dz_ref[:, D_CH:] = (dgate * _gelu_grad(zr[:, D_CH:])).astype(ACT)

    vec = _fullspec((1, D_CH))
    acc = _fullspec((8, D_CH))
    wshape = (D_GROUPS, BLOCK, BLOCK)
    return pl.pallas_call(
        body, grid=(s // tm,), name=name,
        in_specs=[_rowspec(tm, 2 * D_CH), _rowspec(tm, D_CH, dcol), vec, vec, _fullspec(wshape),
                  _fullspec((BLOCK, D_CH))],
        out_specs=[_rowspec(tm, 2 * D_CH), _fullspec(wshape), _fullspec((BLOCK, D_CH)), acc, acc],
        out_shape=[jax.ShapeDtypeStruct((s, 2 * D_CH), ACT), jax.ShapeDtypeStruct(wshape, F32),
                   jax.ShapeDtypeStruct((BLOCK, D_CH), F32),
                   jax.ShapeDtypeStruct((8, D_CH), F32), jax.ShapeDtypeStruct((8, D_CH), F32)],
        scratch_shapes=[pltpu.VMEM(wshape, ACT), pltpu.VMEM(wshape, ACT),
                        pltpu.VMEM((tm, D_CH), F32), pltpu.VMEM((tm, D_CH), F32)],
        compiler_params=_cp("arbitrary"))(z, dsrc, lg, lb, sw, sbx)


def _rms_bwd(dn, x, g):
    r = lax.rsqrt(jnp.mean(x * x, axis=-1, keepdims=True) + RMS_EPS)
    u = dn * g
    dx = r * u - x * (r * r * r) * jnp.mean(x * u, axis=-1, keepdims=True)
    return dx, dn * x * r


def final_loss(h, g, target, name):
    s = h.shape[0]
    tm = _tile(s, (256,))

    def body(h_ref, g_ref, t_ref, loss_ref, dh_ref, dg_ref):
        @pl.when(pl.program_id(0) == 0)
        def _():
            loss_ref[...] = jnp.zeros_like(loss_ref)
            dg_ref[...] = jnp.zeros_like(dg_ref)

        x = h_ref[...]
        r = lax.rsqrt(jnp.mean(x * x, axis=-1, keepdims=True) + RMS_EPS)
        diff = x * r * g_ref[...] - t_ref[...]
        part = jnp.sum(jnp.sum(diff * diff, axis=-1, keepdims=True), axis=0, keepdims=True)
        loss_ref[...] = loss_ref[...] + part * (0.5 / D_MODEL)
        dx, dgt = _rms_bwd(diff * (1.0 / D_MODEL), x, g_ref[...])
        dh_ref[...] = dx
        dg_ref[0:1, :] = dg_ref[0:1, :] + jnp.sum(dgt, axis=0, keepdims=True)

    spec = _rowspec(tm, D_MODEL)
    return pl.pallas_call(
        body, grid=(s // tm,), name=name,
        in_specs=[spec, _fullspec((1, D_MODEL)), spec],
        out_specs=[_fullspec((8, LANES)), spec, _fullspec((8, D_MODEL))],
        out_shape=[jax.ShapeDtypeStruct((8, LANES), F32), jax.ShapeDtypeStruct((s, D_MODEL), F32),
                   jax.ShapeDtypeStruct((8, D_MODEL), F32)],
        compiler_params=_cp("arbitrary"))(h, g, target)


def mm_nt(dy, w, name):
    s, n = dy.shape
    k = w.shape[0]
    tm = _tile(s, (512,))
    tk = _tile(k, (512,))

    def body(d_ref, w_ref, o_ref):
        o_ref[...] = _dot(d_ref[...].astype(ACT), w_ref[...], NT).astype(ACT)

    return pl.pallas_call(
        body, grid=(k // tk, s // tm), name=name,
        in_specs=[pl.BlockSpec((tm, n), lambda j, i: (i, 0)), pl.BlockSpec((tk, n), lambda j, i: (j, 0))],
        out_specs=pl.BlockSpec((tm, tk), lambda j, i: (i, j)),
        out_shape=jax.ShapeDtypeStruct((s, k), ACT),
        compiler_params=_cp("parallel", "parallel"))(dy, w)


def ffn_down_bwd(dh, wd, gate, up, name):
    s = dh.shape[0]
    f = wd.shape[0]
    tm = _tile(s, (512,))
    tf = _tile(f, (1408, 512, 256, 128))

    def body(d_ref, w_ref, g_ref, u_ref, dg_ref, du_ref):
        dact = _dot(d_ref[...].astype(ACT), w_ref[...], NT)
        g = g_ref[...].astype(F32)
        sg = _sigmoid(g)
        dg_ref[...] = (dact * u_ref[...].astype(F32) * (sg * (1.0 + g * (1.0 - sg)))).astype(ACT)
        du_ref[...] = (dact * g * sg).astype(ACT)

    tile = pl.BlockSpec((tm, tf), lambda j, i: (i, j))
    return pl.pallas_call(
        body, grid=(f // tf, s // tm), name=name,
        in_specs=[pl.BlockSpec((tm, D_MODEL), lambda j, i: (i, 0)),
                  pl.BlockSpec((tf, D_MODEL), lambda j, i: (j, 0)), tile, tile],
        out_specs=[tile, tile], out_shape=[jax.ShapeDtypeStruct((s, f), ACT)] * 2,
        compiler_params=_cp("parallel", "parallel"))(dh, wd, gate, up)


def mm_nt_rms(parts, h, g, dh, name):
    s = h.shape[0]
    tm = _tile(s, (256,))
    np_ = len(parts)

    def body(*refs):
        d_refs = refs[:np_]
        w_refs = refs[np_:2 * np_]
        h_ref, g_ref, dh_ref, o_ref, dg_ref = refs[2 * np_:]

        @pl.when(pl.program_id(0) == 0)
        def _():
            dg_ref[...] = jnp.zeros_like(dg_ref)

        dn = None
        for d_ref, w_ref in zip(d_refs, w_refs):
            t = _dot(d_ref[...], w_ref[...], NT)
            dn = t if dn is None else dn + t
        dx, dgt = _rms_bwd(dn, h_ref[...], g_ref[...])
        o_ref[...] = dh_ref[...] + dx
        dg_ref[0:1, :] = dg_ref[0:1, :] + jnp.sum(dgt, axis=0, keepdims=True)

    spec = _rowspec(tm, D_MODEL)
    return pl.pallas_call(
        body, grid=(s // tm,), name=name,
        in_specs=[_rowspec(tm, d.shape[1]) for d, _ in parts] + [_fullspec(w.shape) for _, w in parts]
        + [spec, _fullspec((1, D_MODEL)), spec],
        out_specs=[spec, _fullspec((8, D_MODEL))],
        out_shape=[jax.ShapeDtypeStruct((s, D_MODEL), F32), jax.ShapeDtypeStruct((8, D_MODEL), F32)],
        compiler_params=_cp("arbitrary"))(*[d for d, _ in parts], *[w for _, w in parts], h, g, dh)


def mm_tn(a, b, name):
    s, k = a.shape
    n = b.shape[1]
    tk = _tile(k, (512, 1408, 256, 128))
    tn = _tile(n, (1408, 1280, 1024, 896, 512, 256, 128))
    ts = _tile(s, (2048, 512) if b.dtype == ACT else (1024, 512))

    def body(a_ref, b_ref, o_ref):
        t = _dot(a_ref[...].astype(ACT), b_ref[...].astype(ACT), TN)

        @pl.when(pl.program_id(2) == 0)
        def _():
            o_ref[...] = t

        @pl.when(pl.program_id(2) > 0)
        def _():
            o_ref[...] = o_ref[...] + t

    return pl.pallas_call(
        body, grid=(k // tk, n // tn, s // ts), name=name,
        in_specs=[pl.BlockSpec((ts, tk), lambda i, j, t: (t, i)), pl.BlockSpec((ts, tn), lambda i, j, t: (t, j))],
        out_specs=pl.BlockSpec((tk, tn), lambda i, j, t: (i, j)),
        out_shape=jax.ShapeDtypeStruct((k, n), F32),
        compiler_params=_cp("parallel", "parallel", "arbitrary"))(a, b)


def _adamw_math(w, g, m, v):
    m = ADAM_B1 * m + (1.0 - ADAM_B1) * g
    v = ADAM_B2 * v + (1.0 - ADAM_B2) * (g * g)
    m_hat = m / (1.0 - ADAM_B1 ** ADAM_STEP)
    v_hat = v / (1.0 - ADAM_B2 ** ADAM_STEP)
    delta = -ADAM_LR * (m_hat / (jnp.sqrt(v_hat) + ADAM_EPS) + ADAM_WD * w)
    return delta, m, v


def sum_adamw(parts, w, m, v, layer, others, name):
    nl, r, c = w.shape
    tr = _tile(r, (256, 128, 64, 32, 16, 8))

    def body(p_ref, w_ref, m_ref, v_ref, *rest):
        g_ref, d_ref, mo_ref, vo_ref = rest[-4:]
        g = p_ref[0].astype(F32)
        for i in range(1, N_DEV):
            g = g + p_ref[i].astype(F32)
        d, mm, vv = _adamw_math(w_ref[...], g, m_ref[...], v_ref[...])
        g_ref[...] = g
        d_ref[...] = d
        mo_ref[...] = mm
        vo_ref[...] = vv

    spec = pl.BlockSpec((None, tr, c), lambda i: (layer, i, 0))
    in_specs = [pl.BlockSpec((N_DEV, tr, c), lambda i: (0, i, 0))] + [spec] * 3
    args = [parts, w, m, v]
    aliases = {}
    if others is not None:
        in_specs += [pl.BlockSpec(memory_space=pl.ANY)] * 4
        args += list(others)
        aliases = {4 + j: j for j in range(4)}
    return pl.pallas_call(
        body, grid=(r // tr,), name=name, in_specs=in_specs, out_specs=[spec] * 4,
        out_shape=[jax.ShapeDtypeStruct((nl, r, c), F32)] * 4, input_output_aliases=aliases,
        compiler_params=_cp("parallel"))(*args)


def cast_layers(items, name):
    n = len(items)

    def body(*refs):
        for i in range(n):
            refs[n + i][...] = refs[i][...].astype(ACT)

    return pl.pallas_call(
        body, grid=(1,), name=name,
        in_specs=[pl.BlockSpec((None,) + w.shape[1:], lambda i, l=l: (l, 0, 0)) for w, l in items],
        out_specs=[_fullspec(w.shape[1:]) for w, _ in items],
        out_shape=[jax.ShapeDtypeStruct(w.shape[1:], ACT) for w, _ in items],
        compiler_params=_cp("arbitrary"))(*[w for w, _ in items])


def adamw_small(ws, gs, ms, vs, name):
    k = len(ws)

    def body(*refs):
        for i in range(k):
            w_ref, g_ref, m_ref, v_ref = (refs[j * k + i] for j in range(4))
            d, mm, vv = _adamw_math(w_ref[...], g_ref[...], m_ref[...], v_ref[...])
            refs[4 * k + i][...] = d
            refs[5 * k + i][...] = mm
            refs[6 * k + i][...] = vv

    shapes = [jax.ShapeDtypeStruct(w.shape, F32) for w in ws]
    specs = [_fullspec(w.shape) for w in ws]
    out = pl.pallas_call(
        body, grid=(1,), name=name, in_specs=specs * 4, out_specs=specs * 3, out_shape=shapes * 3,
        compiler_params=_cp("arbitrary"))(*ws, *gs, *ms, *vs)
    return out[:k], out[k:2 * k], out[2 * k:]


def sum_slots(x, out_dtype, name):
    g, r, c = x.shape
    tr = r if x.size * x.dtype.itemsize <= SMALL_BLOCK_BYTES else _tile(r, (256, 128, 64, 32, 16, 8))

    def body(x_ref, o_ref):
        acc = x_ref[0].astype(F32)
        for i in range(1, g):
            acc = acc + x_ref[i].astype(F32)
        o_ref[...] = acc.astype(o_ref.dtype)

    return pl.pallas_call(
        body, grid=(r // tr,), name=name,
        in_specs=[pl.BlockSpec((g, tr, c), lambda i: (0, i, 0))], out_specs=_rowspec(tr, c),
        out_shape=jax.ShapeDtypeStruct((r, c), out_dtype),
        compiler_params=_cp("parallel"))(x)


HBM_SPEC = pl.BlockSpec(memory_space=pltpu.HBM)
SEM_SPEC = pl.BlockSpec(memory_space=pltpu.SEMAPHORE)
DATAFLOW = pltpu.SideEffectType.DATAFLOW_SIDE_EFFECTING


def _my_rank():
    return 4 * lax.axis_index("x") + 2 * lax.axis_index("y") + lax.axis_index("c")


def _exchange_copies(x_refs, land_refs, send, recv, a2a):
    pos = [lax.axis_index(a) for a in AXES]
    me = _my_rank()
    copies = []
    for x_ref, land_ref, s_ref, r_ref in zip(x_refs, land_refs, send, recv):
        for k in range(N_DEV - 1):
            bits = ((k + 1) >> 2 & 1, (k + 1) >> 1 & 1, (k + 1) & 1)
            peer = tuple(1 - p if b else p for p, b in zip(pos, bits))
            prank = 4 * peer[0] + 2 * peer[1] + peer[2]
            copies.append(pltpu.make_async_remote_copy(
                src_ref=x_ref.at[prank] if a2a else x_ref, dst_ref=land_ref.at[me],
                send_sem=s_ref.at[k], recv_sem=r_ref.at[k], device_id=peer, device_id_type=MESH))
    return copies


def exchange_start(xs, a2a, name, after=None):
    n = len(xs)
    me = _my_rank()
    lands = []
    for x in xs:
        own = lax.dynamic_index_in_dim(x, me, 0, keepdims=True) if a2a else x[None]
        shape = x.shape if a2a else (N_DEV,) + x.shape
        lands.append(lax.dynamic_update_slice(lax.empty(shape, x.dtype), own, (me,) + (0,) * (len(shape) - 1)))

    def body(*refs):
        x_refs, land_refs = refs[:n], refs[n:2 * n]
        outs = refs[len(refs) - 4 * n - 1:]
        for cp in _exchange_copies(x_refs, land_refs, outs[:n], outs[n:2 * n], a2a):
            cp.start()
        token = outs[4 * n]
        token[...] = jnp.zeros_like(token)

    sems = [pltpu.SemaphoreType.DMA((N_DEV - 1,))] * n
    out = pl.pallas_call(
        body, name=name,
        out_shape=tuple(sems + sems + [pltpu.HBM(x.shape, x.dtype) for x in xs]
                        + [pltpu.HBM(l.shape, l.dtype) for l in lands] + [jax.ShapeDtypeStruct((8, LANES), F32)]),
        in_specs=[HBM_SPEC] * (2 * n) + ([] if after is None else [pl.BlockSpec(memory_space=pl.ANY)]),
        out_specs=tuple([SEM_SPEC] * (2 * n) + [HBM_SPEC] * (2 * n) + [pl.BlockSpec(memory_space=pltpu.VMEM)]),
        input_output_aliases={i: 2 * n + i for i in range(2 * n)},
        compiler_params=pltpu.CompilerParams(has_side_effects=DATAFLOW),
    )(*[pltpu.with_memory_space_constraint(a, pltpu.HBM) for a in list(xs) + lands], *([] if after is None else [after]))
    return (out[:n], out[n:2 * n], out[2 * n:3 * n], out[3 * n:4 * n]), out[4 * n]


def exchange_wait(handles, after, a2a, name):
    send, recv, x_thru, land_thru = handles
    n = len(x_thru)

    def body(*refs):
        x_refs, land_refs = refs[:n], refs[n:2 * n]
        s_refs, r_refs = refs[2 * n:3 * n], refs[3 * n:4 * n]
        for cp in _exchange_copies(x_refs, land_refs, s_refs, r_refs, a2a):
            cp.wait_send()
            cp.wait_recv()

    out = pl.pallas_call(
        body, name=name,
        out_shape=tuple([pltpu.HBM(a.shape, a.dtype) for a in list(x_thru) + list(land_thru)]),
        in_specs=[HBM_SPEC] * (2 * n) + [SEM_SPEC] * (2 * n) + [pl.BlockSpec(memory_space=pl.ANY)],
        out_specs=tuple([HBM_SPEC] * (2 * n)),
        input_output_aliases={i: i for i in range(2 * n)},
        compiler_params=pltpu.CompilerParams(has_side_effects=DATAFLOW),
    )(*x_thru, *land_thru, *send, *recv, after)
    return out[n:2 * n]


def _local_step(x, target, weight, emit, P):
    s = x.shape[0]
    tabs = _rope_tables(s)
    sinkb = jnp.broadcast_to(P["ev_sinks"].reshape(N_HEADS, 1), (N_HEADS, LANES))
    sbx = jnp.repeat(P["od_spatial_b"].reshape(D_GROUPS, BLOCK).T, HEAD_DIM, axis=1)
    sw = P["od_spatial_w"].reshape(D_GROUPS, BLOCK, BLOCK)
    fg = P["ffn_norm_g"]
    latest = [None]

    def out(name, layer, grad):
        tok = emit(name, layer, grad)
        if tok is not None:
            latest[0] = tok

    def dep(a):
        return a if latest[0] is None else a + latest[0][0:1, 0:1]

    n0 = rmsnorm(x, P["ev_norm_g"], "rms_in")
    qk_e, v_e, ga, gb = inproj(n0, weight("ev_w_in", 0, n0), tabs, ATT_W + 128,
                               (ATT_W + 128, 128, CONV_CH, CONV_CH), "ev_inproj")
    a_e, lse_e = attn_fwd(qk_e, v_e, d=1, hkv=A_KV_HEADS, max_dist=BLOCK - 1, sink=sinkb, out_dtype=ACT,
                          name="ev_attn")
    c_act, c1 = conv_fwd(ga, gb, P["ev_conv_w"], P["ev_conv_b"], P["ev_conv_ln_g"], P["ev_conv_ln_b"], "ev_conv")
    h1, n1 = mm_res([a_e, c_act], weight("ev_w_out", 0, c_act), x, fg[0:1], "ev_outproj")
    gate0, up0, act0 = ffn_up(n1, weight("ffn_w_gate", 0, n1), weight("ffn_w_up", 0, n1), "ffn0_up")
    h2, n2 = mm_res([act0], weight("ffn_w_down", 0, act0), h1, P["od_norm_g"], "ffn0_down")
    qk_o, v_o, z = inproj(n2, weight("od_w_in", 0, n2), tabs, 2 * ATT_W, (2 * ATT_W, ATT_W, 2 * D_CH), "od_inproj")
    outs, lses = [], []
    for window, dil in DILATED:
        assert window // dil == BLOCK
        o_r, l_r = attn_fwd(qk_o, v_o, d=dil, hkv=N_HEADS, max_dist=BLOCK, sink=None, out_dtype=F32,
                            name="od_attn_d%d" % dil)
        outs.append(o_r)
        lses.append(l_r)
    c_out, lse_o = combine_fwd(outs, lses, "od_combine")
    d_out = gmlp_fwd(z, P["od_sgu_ln_g"], P["od_sgu_ln_b"], sw, sbx, "od_gmlp")
    h3, n3 = mm_res([c_out, d_out], weight("od_w_out", 0, d_out), h2, fg[1:2], "od_outproj")
    gate1, up1, act1 = ffn_up(n3, weight("ffn_w_gate", 1, n3), weight("ffn_w_up", 1, n3), "ffn1_up")
    h4, _ = mm_res([act1], weight("ffn_w_down", 1, act1), h3, None, "ffn1_down")
    loss_part, dh4, dg_final = final_loss(h4, P["final_norm_g"], target, "loss_head")

    def ffn_bwd(layer, dh_out, h_in, n_in, gate, up, act):
        wg, wu, wd = (weight(n, layer, dh_out) for n in ("ffn_w_gate", "ffn_w_up", "ffn_w_down"))
        tag = "ffn%d" % layer
        dgate, dup = ffn_down_bwd(dh_out, wd, gate, up, tag + "_down_bwd")
        g_wd = mm_tn(act, dh_out, tag + "_dwd")
        dh_in, dgn = mm_nt_rms([(dgate, wg), (dup, wu)], h_in, dep(fg[layer:layer + 1]), dh_out, tag + "_up_bwd")
        out("ffn_w_down", layer, g_wd)
        out("ffn_w_gate", layer, mm_tn(n_in, dgate, tag + "_dwg"))
        out("ffn_w_up", layer, mm_tn(n_in, dup, tag + "_dwu"))
        return dh_in, dgn[0:1]

    dh3, dgn_f1 = ffn_bwd(1, dh4, h3, n3, gate1, up1, act1)

    dcd = mm_nt(dh3, weight("od_w_out", 0, dh3), "od_outproj_bwd")
    dz, g_sw, g_sbx, g_slg, g_slb = gmlp_bwd(z, dcd, 1, dep(P["od_sgu_ln_g"]), P["od_sgu_ln_b"], sw, sbx,
                                             "od_gmlp_bwd")
    g_sb = jnp.sum(g_sbx.reshape(BLOCK, D_GROUPS, HEAD_DIM), axis=-1).T
    out("od_spatial", 0, jnp.concatenate([g_sw.reshape(D_GROUPS * BLOCK, BLOCK), g_sb], axis=0))
    out("od_w_out", 0, jnp.concatenate([mm_tn(c_out, dh3, "od_dwo_c"), mm_tn(d_out, dh3, "od_dwo_d")], axis=0))
    dqkv = [attn_bwd(qk_o, v_o, dcd if dil == 1 else dcd[:, :ATT_W], 0, c_out, lse_o, d=dil, hkv=N_HEADS,
                     max_dist=BLOCK, sink=None, out_dtype=ACT, name="od_attn_bwd_d%d" % dil)
            for window, dil in DILATED]
    dproj_o = assemble([([b[0] for b in dqkv], True), ([b[1] for b in dqkv], True), ([b[2] for b in dqkv], False),
                        ([dz], False)], tabs, "od_dproj")
    dh2, dgn_od = mm_nt_rms([(dproj_o, weight("od_w_in", 0, dproj_o))], h2, dep(P["od_norm_g"]), dh3,
                            "od_inproj_bwd")
    out("od_w_in", 0, mm_tn(n2, dproj_o, "od_dwi"))

    dh1, dgn_f0 = ffn_bwd(0, dh2, h1, n1, gate0, up0, act0)

    dac = mm_nt(dh1, weight("ev_w_out", 0, dh1), "ev_outproj_bwd")
    dc1, g_clg, g_clb = conv_bwd_ln(c1, dac, 1, dep(P["ev_conv_ln_g"]), P["ev_conv_ln_b"], "ev_conv_bwd_ln")
    out("ev_w_out", 0, jnp.concatenate([mm_tn(a_e, dh1, "ev_dwo_a"), mm_tn(c_act, dh1, "ev_dwo_c")], axis=0))
    dga, dgb, g_cw, g_cb = conv_bwd_conv(dc1, ga, gb, P["ev_conv_w"], "ev_conv_bwd")
    dq, dk, dv, dsink = attn_bwd(qk_e, v_e, dac, 0, a_e, lse_e, d=1, hkv=A_KV_HEADS, max_dist=BLOCK - 1,
                                 sink=sinkb, out_dtype=F32, name="ev_attn_bwd")
    dproj_e = assemble([([dq], True), ([dk], True), ([dv], False), ([dga], False), ([dgb], False)], tabs,
                       "ev_dproj")
    out("ev_w_in", 0, mm_tn(n0, dproj_e, "ev_dwi"))
    dx, dgn_ev = mm_nt_rms([(dproj_e, weight("ev_w_in", 0, dproj_e))], x, dep(P["ev_norm_g"]), dh1,
                           "ev_inproj_bwd")

    small = {
        "ev_norm_g": dgn_ev[0:1],
        "ev_sinks": dsink[:, 0:1].reshape(1, N_HEADS),
        "ev_conv_w": jnp.sum(g_cw.reshape(CONV_HALO, SUBLANES, CONV_CH), axis=1)[:CONV_WIDTH],
        "ev_conv_b": jnp.sum(g_cb, axis=0, keepdims=True),
        "ev_conv_ln_g": g_clg[0:1],
        "ev_conv_ln_b": g_clb[0:1],
        "od_norm_g": dgn_od[0:1],
        "od_sgu_ln_g": g_slg[0:1],
        "od_sgu_ln_b": g_slb[0:1],
        "od_spatial_w": g_sw.reshape(D_GROUPS * BLOCK, BLOCK),
        "od_spatial_b": g_sb,
        "ffn_norm_g": jnp.concatenate([dgn_f0, dgn_f1], axis=0),
        "final_norm_g": dg_final[0:1],
    }
    return loss_part, dx, small


BIG = ("ev_w_in", "ev_w_out", "od_w_in", "od_w_out", "ffn_w_gate", "ffn_w_up", "ffn_w_down")
COL_SHARDED = ("ev_w_in", "od_w_in", "ffn_w_gate", "ffn_w_up")
GATHER_GROUPS = (
    (("ev_w_in", 0),),
    (("ev_w_out", 0), ("ffn_w_gate", 0), ("ffn_w_up", 0), ("ffn_w_down", 0)),
    (("od_w_in", 0),),
    (("od_w_out", 0), ("ffn_w_gate", 1), ("ffn_w_up", 1), ("ffn_w_down", 1)),
)
REDUCE_GROUPS = (
    (("ffn_w_down", 1), ("ffn_w_gate", 1), ("ffn_w_up", 1)),
    (("od_w_out", 0),),
    (("od_w_in", 0),),
    (("ffn_w_down", 0), ("ffn_w_gate", 0), ("ffn_w_up", 0)),
    (("ev_w_out", 0),),
    (("ev_w_in", 0),),
)


def _unshard(name, g):
    if name in COL_SHARDED:
        return jnp.moveaxis(g, 0, 1).reshape(g.shape[1], N_DEV * g.shape[2])
    return g.reshape(N_DEV * g.shape[1], g.shape[2])


def _shard_slots(name, full):
    r, c = full.shape
    if name in COL_SHARDED:
        return jnp.moveaxis(full.reshape(r, N_DEV, c // N_DEV), 1, 0)
    return full.reshape(N_DEV, r // N_DEV, c)


def kernel(x, ev_norm_g, ev_w_in, ev_sinks, ev_conv_w, ev_conv_b, ev_conv_ln_g, ev_conv_ln_b, ev_w_out, od_norm_g, od_w_in, od_sgu_ln_g, od_sgu_ln_b, od_spatial_w, od_spatial_b, od_w_out, ffn_norm_g, ffn_w_gate, ffn_w_up, ffn_w_down, final_norm_g, loss_target, m_ev_norm_g, m_ev_w_in, m_ev_sinks, m_ev_conv_w, m_ev_conv_b, m_ev_conv_ln_g, m_ev_conv_ln_b, m_ev_w_out, m_od_norm_g, m_od_w_in, m_od_sgu_ln_g, m_od_sgu_ln_b, m_od_spatial_w, m_od_spatial_b, m_od_w_out, m_ffn_norm_g, m_ffn_w_gate, m_ffn_w_up, m_ffn_w_down, m_final_norm_g, v_ev_norm_g, v_ev_w_in, v_ev_sinks, v_ev_conv_w, v_ev_conv_b, v_ev_conv_ln_g, v_ev_conv_ln_b, v_ev_w_out, v_od_norm_g, v_od_w_in, v_od_sgu_ln_g, v_od_sgu_ln_b, v_od_spatial_w, v_od_spatial_b, v_od_w_out, v_ffn_norm_g, v_ffn_w_gate, v_ffn_w_up, v_ffn_w_down, v_final_norm_g):
    names = ["ev_norm_g", "ev_w_in", "ev_sinks", "ev_conv_w", "ev_conv_b", "ev_conv_ln_g", "ev_conv_ln_b", "ev_w_out",
             "od_norm_g", "od_w_in", "od_sgu_ln_g", "od_sgu_ln_b", "od_spatial_w", "od_spatial_b", "od_w_out",
             "ffn_norm_g", "ffn_w_gate", "ffn_w_up", "ffn_w_down", "final_norm_g"]
    wts = dict(zip(names, [ev_norm_g, ev_w_in, ev_sinks, ev_conv_w, ev_conv_b, ev_conv_ln_g, ev_conv_ln_b, ev_w_out,
                           od_norm_g, od_w_in, od_sgu_ln_g, od_sgu_ln_b, od_spatial_w, od_spatial_b, od_w_out,
                           ffn_norm_g, ffn_w_gate, ffn_w_up, ffn_w_down, final_norm_g]))
    mom = dict(zip(names, [m_ev_norm_g, m_ev_w_in, m_ev_sinks, m_ev_conv_w, m_ev_conv_b, m_ev_conv_ln_g, m_ev_conv_ln_b,
                           m_ev_w_out, m_od_norm_g, m_od_w_in, m_od_sgu_ln_g, m_od_sgu_ln_b, m_od_spatial_w,
                           m_od_spatial_b, m_od_w_out, m_ffn_norm_g, m_ffn_w_gate, m_ffn_w_up, m_ffn_w_down,
                           m_final_norm_g]))
    vel = dict(zip(names, [v_ev_norm_g, v_ev_w_in, v_ev_sinks, v_ev_conv_w, v_ev_conv_b, v_ev_conv_ln_g, v_ev_conv_ln_b,
                           v_ev_w_out, v_od_norm_g, v_od_w_in, v_od_sgu_ln_g, v_od_sgu_ln_b, v_od_spatial_w,
                           v_od_spatial_b, v_od_w_out, v_ffn_norm_g, v_ffn_w_gate, v_ffn_w_up, v_ffn_w_down,
                           v_final_norm_g]))
    me = _my_rank()

    sp = jnp.zeros((40, LANES), F32)
    sp = sp.at[0:CONV_WIDTH, 0:64].set(ev_conv_w[0])
    sp = sp.at[32, :].set(od_norm_g[0])
    sp = sp.at[33, 0:64].set(od_sgu_ln_g[0])
    sp = sp.at[34, 0:64].set(od_sgu_ln_b[0])

    early = [k for grp in GATHER_GROUPS[:2] for k in grp]
    late = [k for grp in GATHER_GROUPS[2:] for k in grp]
    early_act = dict(zip(early, cast_layers([(wts[n], l) for n, l in early], "cast_early")))
    late_act = dict(zip(late, cast_layers([(wts[n], l) for n, l in late], "cast_late")))
    ag_early, ag_token = exchange_start([sp] + [early_act[k] for k in early], False, "ag_start")
    ag_late = []
    full_w = {}
    P = {
        "ev_norm_g": ev_norm_g + ag_token[0:1, 0:1], "ev_sinks": ev_sinks, "ev_conv_b": ev_conv_b,
        "ev_conv_ln_g": ev_conv_ln_g, "ev_conv_ln_b": ev_conv_ln_b,
        "od_spatial_w": od_spatial_w, "od_spatial_b": od_spatial_b, "ffn_norm_g": ffn_norm_g,
        "final_norm_g": final_norm_g.reshape(1, D_MODEL),
    }

    def weight(name, layer, after):
        if (name, layer) not in full_w:
            gi = [i for i, grp in enumerate(GATHER_GROUPS) if (name, layer) in grp][0]
            if gi < 2:
                idx = [1 + early.index(k) for k in GATHER_GROUPS[gi]]
                handles = ag_early
            else:
                idx = [late.index(k) for k in GATHER_GROUPS[gi]]
                handles = ag_late[0]
            if gi == 0:
                idx = [0] + idx
            lands = exchange_wait(tuple([h[i] for i in idx] for h in handles), after, False, "ag_wait%d" % gi)
            if gi == 0:
                spg, lands = lands[0], lands[1:]
                P["ev_conv_w"] = jnp.moveaxis(spg[:, 0:CONV_WIDTH, 0:64], 0, 1).reshape(CONV_WIDTH, CONV_CH)
                P["od_norm_g"] = spg[:, 32, :].reshape(1, D_MODEL)
                P["od_sgu_ln_g"] = spg[:, 33, 0:64].reshape(1, D_CH)
                P["od_sgu_ln_b"] = spg[:, 34, 0:64].reshape(1, D_CH)
            if gi == 1:
                ag_late.append(exchange_start([late_act[k] for k in late], False, "ag_start_late", after=lands[0])[0])
            for k, land in zip(GATHER_GROUPS[gi], lands):
                full_w[k] = _unshard(k[0], land)
        return full_w[(name, layer)]

    pending, rs_started, spatial = {}, [], []

    def emit(name, layer, grad):
        if name == "od_spatial":
            handles, token = exchange_start([grad], False, "ar_start_b")
            spatial.append(handles)
            return token
        pending[(name, layer)] = grad
        for gi, grp in enumerate(REDUCE_GROUPS):
            if (name, layer) in grp and all(k in pending for k in grp):
                handles, token = exchange_start([_shard_slots(k[0], pending[k]).astype(ACT) for k in grp], True,
                                                "rs_start%d" % gi)
                rs_started.append((gi, handles))
                return token
        return None

    loss_part, dx, small = _local_step(x[0], loss_target[0], weight, emit, P)
    loss = lax.psum(loss_part[0, 0], AXES)

    wide = ["ev_norm_g", "ev_sinks", "ev_conv_w", "ev_conv_b", "ev_conv_ln_g", "ev_conv_ln_b", "od_norm_g",
            "od_sgu_ln_g", "od_sgu_ln_b", "ffn_norm_g", "final_norm_g"]
    blk_a = jnp.concatenate(
        [jnp.pad(small[n], ((0, 0), (0, D_MODEL - small[n].shape[1]))) for n in wide], axis=0)
    blk_a = jnp.pad(blk_a, ((0, 48 - blk_a.shape[0]), (0, 0)))
    ar_a, ar_token = exchange_start([blk_a], False, "ar_start_a")

    results = {}
    for gi, handles in rs_started:
        lands = exchange_wait(handles, ar_token, True, "rs_wait%d" % gi)
        for (n, l), land in zip(REDUCE_GROUPS[gi], lands):
            results[n] = sum_adamw(land, wts[n], mom[n], vel[n], l, results.get(n), "adamw_%s%d" % (n, l))
    out_g, out_d, out_m, out_v = {}, {}, {}, {}
    for n in BIG:
        out_g[n], out_d[n], out_m[n], out_v[n] = results[n]

    last = results[REDUCE_GROUPS[-1][-1][0]][0]
    sum_b = sum_slots(exchange_wait(spatial[0], last, False, "ar_wait_b")[0], F32, "ar_sum_b")
    sum_a = sum_slots(exchange_wait(ar_a, last, False, "ar_wait_a")[0], F32, "ar_sum_a")
    full = {}
    off = 0
    for n in wide:
        r_, c_ = small[n].shape
        full[n] = sum_a[off:off + r_, 0:c_]
        off += r_
    full["od_spatial_w"] = sum_b[0:D_GROUPS * BLOCK]
    full["od_spatial_b"] = sum_b[D_GROUPS * BLOCK:D_GROUPS * BLOCK + D_GROUPS]
    full["ev_conv_w"] = lax.dynamic_slice_in_dim(full["ev_conv_w"], me * 64, 64, axis=1)
    full["od_norm_g"] = lax.dynamic_slice_in_dim(full["od_norm_g"], me * 128, 128, axis=1)
    full["od_sgu_ln_g"] = lax.dynamic_slice_in_dim(full["od_sgu_ln_g"], me * 64, 64, axis=1)
    full["od_sgu_ln_b"] = lax.dynamic_slice_in_dim(full["od_sgu_ln_b"], me * 64, 64, axis=1)

    small_names = [n for n in names if n not in BIG]
    view = {n: ((-1, wts[n].shape[-1]) if wts[n].ndim > 1 else (1, -1)) for n in small_names}
    ds, ms, vs = adamw_small([wts[n].reshape(view[n]) for n in small_names],
                             [full[n].reshape(view[n]) for n in small_names],
                             [mom[n].reshape(view[n]) for n in small_names],
                             [vel[n].reshape(view[n]) for n in small_names], "adamw_small")
    for i, n in enumerate(small_names):
        shp = wts[n].shape
        out_g[n], out_d[n], out_m[n], out_v[n] = (full[n].reshape(shp), ds[i].reshape(shp), ms[i].reshape(shp),
                                                  vs[i].reshape(shp))

    return (loss, dx[None], *[out_g[n] for n in names], *[out_d[n] for n in names],
            *[out_m[n] for n in names], *[out_v[n] for n in names])
```

```python
import functools
import math

import jax
import jax.numpy as jnp
from jax import lax
from jax.experimental import pallas as pl
from jax.experimental.pallas import tpu as pltpu

F32 = jnp.float32
ACT = jnp.bfloat16

D_MODEL = 1024
HEAD_DIM = 64
N_HEADS = 8
ATT_W = N_HEADS * HEAD_DIM
A_KV_HEADS = 2
CONV_CH = 512
CONV_WIDTH = 31
CONV_HALO = 32
D_CH = 512
D_GROUPS = 8
BLOCK = 128
D_FF = 2816
ROT_DIM = 16
ROPE_THETA = 500000.0
RMS_EPS = 1e-6
LN_EPS = 1e-5
DILATED = ((128, 1), (512, 4), (2048, 16))
NEG = -1e30
LANES = 128

ADAM_LR = 0.001
ADAM_B1 = 0.9
ADAM_B2 = 0.999
ADAM_EPS = 1e-08
ADAM_WD = 0.01
ADAM_STEP = 10

V7X_VMEM_LIMIT = 56 * 1024 * 1024
SMALL_BLOCK_BYTES = 6 * 1024 * 1024
N_DEV = 8

NN = (((1,), (0,)), ((), ()))
NT = (((1,), (1,)), ((), ()))
TN = (((0,), (0,)), ((), ()))
MESH = pl.DeviceIdType.MESH
AXES = ("x", "y", "c")


def _dot(a, b, dims):
    return lax.dot_general(a, b, dims, preferred_element_type=F32)


def _cp(*sem):
    return pltpu.CompilerParams(dimension_semantics=sem if sem else None,
                                vmem_limit_bytes=V7X_VMEM_LIMIT)


def _tile(n, prefs):
    for p in prefs:
        if n % p == 0:
            return p
    return n


def _sigmoid(x):
    return 1.0 / (1.0 + jnp.exp(-x))


def _rowspec(tm, w, col=0):
    return pl.BlockSpec((tm, w), lambda i, col=col: (i, col))


def _fullspec(shape):
    nd = len(shape)
    return pl.BlockSpec(shape, lambda *a, nd=nd: (0,) * nd)


def _rope_tables(seq):
    half = ROT_DIM // 2
    inv_freq = ROPE_THETA ** (-jnp.arange(half, dtype=F32) * (2.0 / ROT_DIM))
    ang = jnp.arange(seq, dtype=jnp.int32).astype(F32)[:, None] * inv_freq[None, :]
    cos, sin = jnp.cos(ang), jnp.sin(ang)
    lane = jnp.arange(LANES)
    jm = lane % HEAD_DIM
    idx = jm % half
    c = jnp.where(jm[None, :] < ROT_DIM, cos[:, idx], 1.0)
    sa = jnp.where(jm[None, :] < half, -sin[:, idx], 0.0)
    sb = jnp.where((jm[None, :] >= half) & (jm[None, :] < ROT_DIM), sin[:, idx], 0.0)
    return c.astype(F32), sa.astype(F32), sb.astype(F32)


def _rope(x, c, sa, sb):
    return x * c + pltpu.roll(x, LANES - 8, 1) * sa + pltpu.roll(x, 8, 1) * sb


def _rope_t(d, c, sa, sb):
    return d * c + pltpu.roll(d * sa, 8, 1) + pltpu.roll(d * sb, LANES - 8, 1)


def rmsnorm(h, g, name):
    s = h.shape[0]
    tm = _tile(s, (512,))

    def body(h_ref, g_ref, o_ref):
        x = h_ref[...]
        r = lax.rsqrt(jnp.mean(x * x, axis=-1, keepdims=True) + RMS_EPS)
        o_ref[...] = (x * r * g_ref[...]).astype(o_ref.dtype)

    return pl.pallas_call(
        body, grid=(s // tm,), name=name,
        in_specs=[_rowspec(tm, D_MODEL), _fullspec((1, D_MODEL))],
        out_specs=_rowspec(tm, D_MODEL),
        out_shape=jax.ShapeDtypeStruct((s, D_MODEL), ACT),
        compiler_params=_cp("parallel"))(h, g)


def inproj(n, w, tabs, nqk, splits, name):
    s = n.shape[0]
    ntot = w.shape[1]
    assert sum(splits) == ntot and splits[0] == nqk
    tm = _tile(s, (256,))

    def body(n_ref, w_ref, c_ref, sa_ref, sb_ref, *outs):
        res = _dot(n_ref[...], w_ref[...], NN)
        c, sa, sb = c_ref[...], sa_ref[...], sb_ref[...]
        for g in range(nqk // LANES):
            x = res[:, g * LANES:(g + 1) * LANES]
            outs[0][:, g * LANES:(g + 1) * LANES] = _rope(x, c, sa, sb).astype(ACT)
        off = nqk
        for o_ref, wd in zip(outs[1:], splits[1:]):
            o_ref[...] = res[:, off:off + wd].astype(ACT)
            off += wd

    return pl.pallas_call(
        body, grid=(s // tm,), name=name,
        in_specs=[_rowspec(tm, D_MODEL), _fullspec((D_MODEL, ntot))] + [_rowspec(tm, LANES)] * 3,
        out_specs=[_rowspec(tm, wd) for wd in splits],
        out_shape=[jax.ShapeDtypeStruct((s, wd), ACT) for wd in splits],
        compiler_params=_cp("parallel"))(n, w, *tabs)


def ffn_up(n, wg, wu, name):
    s = n.shape[0]
    f = wg.shape[1]
    tm = _tile(s, (512,))
    tf = _tile(f, (1408, 512, 256, 128))

    def body(n_ref, wg_ref, wu_ref, g_ref, u_ref, a_ref):
        a = n_ref[...]
        g = _dot(a, wg_ref[...], NN)
        u = _dot(a, wu_ref[...], NN)
        g_ref[...] = g.astype(ACT)
        u_ref[...] = u.astype(ACT)
        a_ref[...] = (g * _sigmoid(g) * u).astype(ACT)

    wspec = pl.BlockSpec((D_MODEL, tf), lambda j, i: (0, j))
    ospec = pl.BlockSpec((tm, tf), lambda j, i: (i, j))
    return pl.pallas_call(
        body, grid=(f // tf, s // tm), name=name,
        in_specs=[pl.BlockSpec((tm, D_MODEL), lambda j, i: (i, 0)), wspec, wspec],
        out_specs=[ospec] * 3,
        out_shape=[jax.ShapeDtypeStruct((s, f), ACT)] * 3,
        compiler_params=_cp("parallel", "parallel"))(n, wg, wu)


def mm_res(parts, w, h, gnext, name):
    s = h.shape[0]
    tm = _tile(s, (256,))
    widths = [p.shape[1] for p in parts]
    assert sum(widths) == w.shape[0]
    np_ = len(parts)

    def body(*refs):
        p_refs = refs[:np_]
        w_ref, h_ref = refs[np_], refs[np_ + 1]
        rest = refs[np_ + 2:]
        acc = h_ref[...]
        off = 0
        for p_ref, wd in zip(p_refs, widths):
            acc = acc + _dot(p_ref[...], w_ref[off:off + wd, :], NN)
            off += wd
        if gnext is None:
            rest[0][...] = acc
        else:
            g_ref, ho_ref, no_ref = rest
            ho_ref[...] = acc
            r = lax.rsqrt(jnp.mean(acc * acc, axis=-1, keepdims=True) + RMS_EPS)
            no_ref[...] = (acc * r * g_ref[...]).astype(ACT)

    in_specs = [_rowspec(tm, wd) for wd in widths] + [_fullspec(w.shape), _rowspec(tm, D_MODEL)]
    args = list(parts) + [w, h]
    out_specs = [_rowspec(tm, D_MODEL)]
    out_shape = [jax.ShapeDtypeStruct((s, D_MODEL), F32)]
    if gnext is not None:
        in_specs.append(_fullspec((1, D_MODEL)))
        args.append(gnext)
        out_specs.append(_rowspec(tm, D_MODEL))
        out_shape.append(jax.ShapeDtypeStruct((s, D_MODEL), ACT))
    out = pl.pallas_call(
        body, grid=(s // tm,), name=name, in_specs=in_specs, out_specs=out_specs,
        out_shape=out_shape, compiler_params=_cp("parallel"))(*args)
    return (out[0], None) if gnext is None else (out[0], out[1])


def _band_mask(n, max_dist):
    qi = lax.broadcasted_iota(jnp.int32, (BLOCK, 2 * BLOCK), 0)
    kj = lax.broadcasted_iota(jnp.int32, (BLOCK, 2 * BLOCK), 1)
    dist = qi + BLOCK - kj
    valid = jnp.logical_and(dist >= 0, dist <= max_dist)
    return jnp.logical_and(valid, jnp.logical_or(kj >= BLOCK, n > 0))


def _band_mask_t(n, max_dist):
    kj = lax.broadcasted_iota(jnp.int32, (2 * BLOCK, BLOCK), 0)
    qi = lax.broadcasted_iota(jnp.int32, (2 * BLOCK, BLOCK), 1)
    dist = qi + BLOCK - kj
    valid = jnp.logical_and(dist >= 0, dist <= max_dist)
    return jnp.logical_and(valid, jnp.logical_or(kj >= BLOCK, n > 0))


def _fold(a, d):
    return a.reshape(a.shape[0] // d, d * a.shape[1])


def attn_fwd(qk, v, *, d, hkv, max_dist, sink, out_dtype, name):
    s = qk.shape[0]
    kvw = hkv * HEAD_DIM
    wqk = ATT_W + kvw
    assert qk.shape[1] == wqk and (d == 1 or (wqk % ATT_W == 0 and wqk % kvw == 0))
    assert sink is None or max_dist == BLOCK - 1
    nb = s // d // BLOCK
    grp = N_HEADS // hkv
    qpb, kpb, koff = wqk // ATT_W, wqk // kvw, ATT_W // kvw

    def body(*refs):
        if sink is None:
            q_ref, kc_ref, kp_ref, vc_ref, vp_ref, o_ref, l_ref, o_buf = refs
        else:
            q_ref, kc_ref, kp_ref, vc_ref, vp_ref, s_ref, o_ref, l_ref, o_buf = refs
        n = pl.program_id(1)
        valid = _band_mask_t(n, max_dist)
        qsl = [slice(h * HEAD_DIM, (h + 1) * HEAD_DIM) for h in range(N_HEADS)]
        ksl = [slice((h // grp) * HEAD_DIM, (h // grp + 1) * HEAD_DIM) for h in range(N_HEADS)]
        kk = jnp.concatenate([kp_ref[...], kc_ref[...]], axis=0)
        vv = jnp.concatenate([vp_ref[...], vc_ref[...]], axis=0)
        scores = []
        for h in range(N_HEADS):
            q = q_ref[:, qsl[h]] * 0.125
            scores.append(_dot(kk[:, ksl[h]], q, NT))
        probs = []
        head_row = lax.broadcasted_iota(jnp.int32, (SUBLANES, BLOCK), 0)
        lse8 = jnp.zeros((SUBLANES, BLOCK), F32)
        if sink is not None:
            sink_row = lax.broadcasted_iota(jnp.int32, (2 * BLOCK, BLOCK), 0) == 0
        for h in range(N_HEADS):
            sc = jnp.where(valid, scores[h], NEG)
            if sink is not None:
                sc = jnp.where(sink_row, s_ref[h:h + 1, 0:1], sc)
            m = jnp.max(sc, axis=0, keepdims=True)
            p = jnp.exp(sc - m)
            l = jnp.sum(p, axis=0, keepdims=True)
            if sink is not None:
                p = jnp.where(sink_row, 0.0, p)
            lse8 = jnp.where(head_row == h, m + jnp.log(l), lse8)
            probs.append((p * (1.0 / l)).astype(ACT))
        l_ref[...] = jnp.concatenate([lse8, jnp.zeros((BLOCK - SUBLANES, BLOCK), F32)], axis=0).T
        for h in range(N_HEADS):
            o_buf[:, qsl[h]] = _dot(probs[h], vv[:, ksl[h]], TN)
        o_ref[...] = o_buf[...].astype(o_ref.dtype)

    prev = lambda n: jnp.maximum(n - 1, 0)
    in_specs = [
        pl.BlockSpec((BLOCK, ATT_W), lambda r, n: (n, r * qpb)),
        pl.BlockSpec((BLOCK, kvw), lambda r, n: (n, r * kpb + koff)),
        pl.BlockSpec((BLOCK, kvw), lambda r, n: (prev(n), r * kpb + koff)),
        pl.BlockSpec((BLOCK, kvw), lambda r, n: (n, r)),
        pl.BlockSpec((BLOCK, kvw), lambda r, n: (prev(n), r)),
    ]
    qkf, vf = _fold(qk, d), _fold(v, d)
    args = [qkf, qkf, qkf, vf, vf]
    if sink is not None:
        in_specs.append(_fullspec((N_HEADS, LANES)))
        args.append(sink)
    ospec = pl.BlockSpec((BLOCK, ATT_W), lambda r, n: (n, r))
    lspec = pl.BlockSpec((BLOCK, LANES), lambda r, n: (n, r))
    o, lse = pl.pallas_call(
        body, grid=(d, nb), name=name, in_specs=in_specs, out_specs=[ospec, lspec],
        out_shape=[jax.ShapeDtypeStruct((s // d, d * ATT_W), out_dtype),
                   jax.ShapeDtypeStruct((s // d, d * LANES), F32)],
        scratch_shapes=[pltpu.VMEM((BLOCK, ATT_W), F32)],
        compiler_params=_cp("parallel", "parallel"))(*args)
    return o.reshape(s, ATT_W), lse.reshape(s, LANES)


def attn_bwd(qk, v, do_src, do_col, o, lse, *, d, hkv, max_dist, sink, out_dtype, name):
    s = qk.shape[0]
    kvw = hkv * HEAD_DIM
    wqk = ATT_W + kvw
    nb = s // d // BLOCK
    grp = N_HEADS // hkv
    qpb, kpb, koff = wqk // ATT_W, wqk // kvw, ATT_W // kvw
    dob = do_src.shape[1] // ATT_W
    has_sink = sink is not None

    def body(*refs):
        refs = list(refs)
        q_ref, kc_ref, kp_ref, vc_ref, vp_ref, do_ref, o_ref, l_ref = refs[:8]
        pos = 8
        if has_sink:
            s_ref = refs[pos]
            pos += 1
        dq_ref, dk_ref, dv_ref = refs[pos:pos + 3]
        pos += 3
        if has_sink:
            ds_ref = refs[pos]
            pos += 1
        ck_ref, cv_ref, dq_buf, dk_buf, dv_buf = refs[pos:pos + 5]
        r_id = pl.program_id(0)
        n = pl.program_id(1)

        @pl.when(n == 0)
        def _():
            ck_ref[...] = jnp.zeros_like(ck_ref)
            cv_ref[...] = jnp.zeros_like(cv_ref)

        if has_sink:
            @pl.when(jnp.logical_and(n == 0, r_id == 0))
            def _():
                ds_ref[...] = jnp.zeros_like(ds_ref)

        @pl.when(n < nb)
        def _():
            valid = _band_mask_t(n, max_dist)
            qsl = [slice(h * HEAD_DIM, (h + 1) * HEAD_DIM) for h in range(N_HEADS)]
            ksl = [slice((h // grp) * HEAD_DIM, (h // grp + 1) * HEAD_DIM) for h in range(N_HEADS)]
            kk = jnp.concatenate([kp_ref[...], kc_ref[...]], axis=0)
            vv = jnp.concatenate([vp_ref[...], vc_ref[...]], axis=0)
            qs, first = [], []
            for h in range(N_HEADS):
                q = q_ref[:, qsl[h]] * 0.125
                qs.append(q)
                first.append((_dot(kk[:, ksl[h]], q, NT), _dot(vv[:, ksl[h]], do_ref[:, qsl[h]], NT)))
            lse_t = l_ref[...].T
            prod = do_ref[...].astype(F32) * o_ref[...].astype(F32)
            hi = prod.astype(ACT)
            lo = (prod - hi.astype(F32)).astype(ACT)
            col = lax.broadcasted_iota(jnp.int32, (LANES, ATT_W), 1)
            row = lax.broadcasted_iota(jnp.int32, (LANES, ATT_W), 0)
            head_of = jnp.logical_and(col >= row * HEAD_DIM, col < (row + 1) * HEAD_DIM).astype(ACT)
            e_t = _dot(head_of, hi, NT) + _dot(head_of, lo, NT)
            mid = []
            for h in range(N_HEADS):
                s_t, dp_t = first[h]
                lse_h, e_h = lse_t[h:h + 1, :], e_t[h:h + 1, :]
                p_t = jnp.exp(jnp.where(valid, s_t, NEG) - lse_h)
                mid.append(((p_t * (dp_t - e_h)).astype(ACT), p_t.astype(ACT)))
                if has_sink:
                    sk = s_ref[h:h + 1, 0:1]
                    dsk = -jnp.sum(jnp.exp(sk - lse_h) * e_h, axis=1, keepdims=True)
                    ds_ref[h:h + 1, :] = ds_ref[h:h + 1, :] + dsk
            dkk = [None] * hkv
            dvv = [None] * hkv
            for h in range(N_HEADS):
                kh = h // grp
                ds_t, p_t = mid[h]
                dq_buf[:, qsl[h]] = _dot(ds_t, kk[:, ksl[h]], TN) * 0.125
                for lst, val in ((dkk, _dot(ds_t, qs[h], NN)), (dvv, _dot(p_t, do_ref[:, qsl[h]], NN))):
                    lst[kh] = val if lst[kh] is None else lst[kh] + val
            for kh in range(hkv):
                ks = slice(kh * HEAD_DIM, (kh + 1) * HEAD_DIM)
                dk_buf[:, ks] = ck_ref[:, ks] + dkk[kh][:BLOCK]
                dv_buf[:, ks] = cv_ref[:, ks] + dvv[kh][:BLOCK]
                ck_ref[:, ks] = dkk[kh][BLOCK:]
                cv_ref[:, ks] = dvv[kh][BLOCK:]
            dq_ref[...] = dq_buf[...].astype(dq_ref.dtype)
            dk_ref[...] = dk_buf[...].astype(dk_ref.dtype)
            dv_ref[...] = dv_buf[...].astype(dv_ref.dtype)

        @pl.when(n == nb)
        def _():
            dk_ref[...] = ck_ref[...].astype(dk_ref.dtype)
            dv_ref[...] = cv_ref[...].astype(dv_ref.dtype)

    qrow = lambda n: jnp.minimum(n, nb - 1)
    prow = lambda n: jnp.maximum(jnp.minimum(n, nb - 1) - 1, 0)
    krow = lambda n: jnp.maximum(n - 1, 0)
    in_specs = [
        pl.BlockSpec((BLOCK, ATT_W), lambda r, n: (qrow(n), r * qpb)),
        pl.BlockSpec((BLOCK, kvw), lambda r, n: (qrow(n), r * kpb + koff)),
        pl.BlockSpec((BLOCK, kvw), lambda r, n: (prow(n), r * kpb + koff)),
        pl.BlockSpec((BLOCK, kvw), lambda r, n: (qrow(n), r)),
        pl.BlockSpec((BLOCK, kvw), lambda r, n: (prow(n), r)),
        pl.BlockSpec((BLOCK, ATT_W), lambda r, n: (qrow(n), r * dob + do_col)),
        pl.BlockSpec((BLOCK, ATT_W), lambda r, n: (qrow(n), r)),
        pl.BlockSpec((BLOCK, LANES), lambda r, n: (qrow(n), r)),
    ]
    qkf, vf = _fold(qk, d), _fold(v, d)
    args = [qkf, qkf, qkf, vf, vf, _fold(do_src, d), _fold(o, d), _fold(lse, d)]
    if has_sink:
        in_specs.append(_fullspec((N_HEADS, LANES)))
        args.append(sink)
    qspec = pl.BlockSpec((BLOCK, ATT_W), lambda r, n: (qrow(n), r))
    kspec = pl.BlockSpec((BLOCK, kvw), lambda r, n: (krow(n), r))
    out_specs = [qspec, kspec, kspec]
    out_shape = [jax.ShapeDtypeStruct((s // d, d * ATT_W), out_dtype),
                 jax.ShapeDtypeStruct((s // d, d * kvw), out_dtype),
                 jax.ShapeDtypeStruct((s // d, d * kvw), out_dtype)]
    if has_sink:
        out_specs.append(_fullspec((N_HEADS, LANES)))
        out_shape.append(jax.ShapeDtypeStruct((N_HEADS, LANES), F32))
    out = pl.pallas_call(
        body, grid=(d, nb + 1), name=name, in_specs=in_specs, out_specs=out_specs,
        out_shape=out_shape,
        scratch_shapes=[pltpu.VMEM((BLOCK, kvw), F32), pltpu.VMEM((BLOCK, kvw), F32),
                        pltpu.VMEM((BLOCK, ATT_W), F32), pltpu.VMEM((BLOCK, kvw), F32), pltpu.VMEM((BLOCK, kvw), F32)],
        compiler_params=_cp("arbitrary", "arbitrary"))(*args)
    res = [out[0].reshape(s, ATT_W), out[1].reshape(s, kvw), out[2].reshape(s, kvw)]
    if has_sink:
        res.append(out[3])
    return res


def combine_fwd(os_, lses, name):
    s = os_[0].shape[0]
    tm = _tile(s, (512,))

    def body(o1, o2, o3, l1, l2, l3, c_ref, l_ref, c_buf):
        a, b, c = l1[...], l2[...], l3[...]
        m = jnp.maximum(jnp.maximum(a, b), c)
        wa, wb, wc = jnp.exp(a - m), jnp.exp(b - m), jnp.exp(c - m)
        tot = wa + wb + wc
        l_ref[...] = m + jnp.log(tot)
        rt = 1.0 / tot
        wa, wb, wc = wa * rt, wb * rt, wc * rt
        for h in range(N_HEADS):
            cs = slice(h * HEAD_DIM, (h + 1) * HEAD_DIM)
            c_buf[:, cs] = (wa[:, h:h + 1] * o1[:, cs] + wb[:, h:h + 1] * o2[:, cs] + wc[:, h:h + 1] * o3[:, cs])
        c_ref[...] = c_buf[...].astype(ACT)

    spec = _rowspec(tm, ATT_W)
    lspec = _rowspec(tm, LANES)
    return pl.pallas_call(
        body, grid=(s // tm,), name=name, in_specs=[spec] * 3 + [lspec] * 3, out_specs=[spec, lspec],
        out_shape=[jax.ShapeDtypeStruct((s, ATT_W), ACT), jax.ShapeDtypeStruct((s, LANES), F32)],
        scratch_shapes=[pltpu.VMEM((tm, ATT_W), F32)],
        compiler_params=_cp("parallel"))(*os_, *lses)


def assemble(parts, tabs, name):
    s = parts[0][0][0].shape[0]
    tm = _tile(s, (256,))
    widths = [terms[0].shape[1] for terms, _ in parts]
    counts = [len(terms) for terms, _ in parts]
    flags = [f for _, f in parts]
    flat = [t for terms, _ in parts for t in terms]

    def body(*refs):
        c_ref, sa_ref, sb_ref, o_ref = refs[len(flat):]
        c, sa, sb = c_ref[...], sa_ref[...], sb_ref[...]
        off = 0
        first = 0
        for wd, cnt, fl in zip(widths, counts, flags):
            t_refs = refs[first:first + cnt]
            first += cnt
            for g in range(wd // LANES):
                cols = slice(g * LANES, (g + 1) * LANES)
                x = t_refs[0][:, cols].astype(F32)
                for t_ref in t_refs[1:]:
                    x = x + t_ref[:, cols].astype(F32)
                if fl:
                    x = _rope_t(x, c, sa, sb)
                o_ref[:, off + g * LANES:off + (g + 1) * LANES] = x.astype(ACT)
            off += wd

    tot = sum(widths)
    return pl.pallas_call(
        body, grid=(s // tm,), name=name,
        in_specs=[_rowspec(tm, wd) for wd, cnt in zip(widths, counts) for _ in range(cnt)]
        + [_rowspec(tm, LANES)] * 3,
        out_specs=_rowspec(tm, tot), out_shape=jax.ShapeDtypeStruct((s, tot), ACT),
        compiler_params=_cp("parallel"))(*flat, *tabs)


def _ln_stats(x):
    mu = jnp.mean(x, axis=-1, keepdims=True)
    xc = x - mu
    var = jnp.mean(xc * xc, axis=-1, keepdims=True)
    rstd = lax.rsqrt(var + LN_EPS)
    return xc * rstd, rstd


SUBLANES = 8


TAP_ROWS = 32


def _tap_sum(buf, cw_ref, offsets, tm, res_ref):
    for r0 in range(0, tm, TAP_ROWS):
        acc = None
        for ph in range(SUBLANES):
            taps = [j for j, off in enumerate(offsets) if off % SUBLANES == ph]
            if not taps:
                continue
            rows = TAP_ROWS if ph == 0 else TAP_ROWS + SUBLANES
            part = None
            for j in taps:
                term = cw_ref[j:j + 1, :] * buf[pl.ds(offsets[j] - ph + r0, rows), :]
                part = term if part is None else part + term
            part = part[ph:ph + TAP_ROWS]
            acc = part if acc is None else acc + part
        res_ref[pl.ds(r0, TAP_ROWS), :] = acc


def conv_fwd(ga, gb, cw, cb, lg, lb, name):
    s = ga.shape[0]
    tm = _tile(s, (256,))
    hb = tm // CONV_HALO

    def body(ga_ref, gb_ref, gah_ref, gbh_ref, cw_ref, cb_ref, lg_ref, lb_ref, c_ref, c1_ref, buf):
        i = pl.program_id(0)
        halo = gah_ref[...].astype(F32) * _sigmoid(gbh_ref[...].astype(F32))
        buf[0:CONV_HALO, :] = jnp.where(i > 0, halo, 0.0)
        buf[CONV_HALO:, :] = ga_ref[...].astype(F32) * _sigmoid(gb_ref[...].astype(F32))
        first = CONV_HALO - (CONV_WIDTH - 1)
        _tap_sum(buf, cw_ref, [first + j for j in range(CONV_WIDTH)], tm, c1_ref)
        acc = c1_ref[...] + cb_ref[...]
        c1_ref[...] = acc
        xh, _ = _ln_stats(acc)
        y = xh * lg_ref[...] + lb_ref[...]
        c_ref[...] = (y * _sigmoid(y)).astype(ACT)

    hspec = pl.BlockSpec((CONV_HALO, CONV_CH), lambda i: (jnp.maximum(i * hb - 1, 0), 0))
    vec = _fullspec((1, CONV_CH))
    spec = _rowspec(tm, CONV_CH)
    return pl.pallas_call(
        body, grid=(s // tm,), name=name,
        in_specs=[spec, spec, hspec, hspec, _fullspec((CONV_WIDTH, CONV_CH)), vec, vec, vec],
        out_specs=[spec, spec],
        out_shape=[jax.ShapeDtypeStruct((s, CONV_CH), ACT), jax.ShapeDtypeStruct((s, CONV_CH), F32)],
        scratch_shapes=[pltpu.VMEM((tm + CONV_HALO, CONV_CH), F32)],
        compiler_params=_cp("parallel"))(ga, gb, ga, gb, cw, cb, lg, lb)


def conv_bwd_ln(c1, dsrc, dcol, lg, lb, name):
    s = c1.shape[0]
    tm = _tile(s, (256,))

    def body(c1_ref, d_ref, lg_ref, lb_ref, o_ref, dg_ref, db_ref):
        @pl.when(pl.program_id(0) == 0)
        def _():
            dg_ref[...] = jnp.zeros_like(dg_ref)
            db_ref[...] = jnp.zeros_like(db_ref)

        xh, rstd = _ln_stats(c1_ref[...].astype(F32))
        y = xh * lg_ref[...] + lb_ref[...]
        sg = _sigmoid(y)
        dy = d_ref[...].astype(F32) * (sg * (1.0 + y * (1.0 - sg)))
        dg_ref[0:1, :] = dg_ref[0:1, :] + jnp.sum(dy * xh, axis=0, keepdims=True)
        db_ref[0:1, :] = db_ref[0:1, :] + jnp.sum(dy, axis=0, keepdims=True)
        dxh = dy * lg_ref[...]
        o_ref[...] = rstd * (dxh - jnp.mean(dxh, axis=-1, keepdims=True)
                             - xh * jnp.mean(dxh * xh, axis=-1, keepdims=True))

    vec = _fullspec((1, CONV_CH))
    acc = _fullspec((8, CONV_CH))
    return pl.pallas_call(
        body, grid=(s // tm,), name=name,
        in_specs=[_rowspec(tm, CONV_CH), _rowspec(tm, CONV_CH, dcol), vec, vec],
        out_specs=[_rowspec(tm, CONV_CH), acc, acc],
        out_shape=[jax.ShapeDtypeStruct((s, CONV_CH), F32)] + [jax.ShapeDtypeStruct((8, CONV_CH), F32)] * 2,
        compiler_params=_cp("arbitrary"))(c1, dsrc, lg, lb)


def conv_bwd_conv(dc1, ga, gb, cw, name):
    s = ga.shape[0]
    tm = _tile(s, (256,))
    hb = tm // CONV_HALO
    nt = s // tm
    last_h = s // CONV_HALO - 1
    first = CONV_HALO - (CONV_WIDTH - 1)


    def rows8(x):
        return jnp.sum(x.reshape(x.shape[0] // SUBLANES, SUBLANES, CONV_CH), axis=0)

    def body(d_ref, dn_ref, ga_ref, gb_ref, gah_ref, gbh_ref, cw_ref,
             dga_ref, dgb_ref, dw_ref, db_ref, dbuf, cbuf, sbuf):
        i = pl.program_id(0)

        @pl.when(i == 0)
        def _():
            dw_ref[...] = jnp.zeros_like(dw_ref)
            db_ref[...] = jnp.zeros_like(db_ref)

        d = d_ref[...]
        dbuf[0:tm, :] = d
        dbuf[tm:, :] = jnp.where(i < nt - 1, dn_ref[...], 0.0)
        halo = gah_ref[...].astype(F32) * _sigmoid(gbh_ref[...].astype(F32))
        cbuf[0:CONV_HALO, :] = jnp.where(i > 0, halo, 0.0)
        a = ga_ref[...].astype(F32)
        sg = _sigmoid(gb_ref[...].astype(F32))
        cbuf[CONV_HALO:, :] = a * sg
        for ph in range(SUBLANES):
            taps = [j for j in range(CONV_WIDTH) if (first + j) % SUBLANES == ph]
            if ph:
                sbuf[0:tm + CONV_HALO - SUBLANES, :] = cbuf[pl.ds(ph, tm + CONV_HALO - SUBLANES), :]
            src = sbuf if ph else cbuf
            for r0 in range(0, tm, TAP_ROWS):
                d_blk = dbuf[pl.ds(r0, TAP_ROWS), :]
                for j in taps:
                    tap = src[pl.ds(first + j - ph + r0, TAP_ROWS), :]
                    rows = slice(j * SUBLANES, (j + 1) * SUBLANES)
                    dw_ref[rows, :] = dw_ref[rows, :] + rows8(d_blk * tap)
        db_ref[...] = db_ref[...] + rows8(d)
        _tap_sum(dbuf, cw_ref, [CONV_WIDTH - 1 - j for j in range(CONV_WIDTH)], tm, sbuf)
        dc0 = sbuf[0:tm, :]
        dga_ref[...] = (dc0 * sg).astype(ACT)
        dgb_ref[...] = (dc0 * a * sg * (1.0 - sg)).astype(ACT)

    spec = _rowspec(tm, CONV_CH)
    hprev = pl.BlockSpec((CONV_HALO, CONV_CH), lambda i: (jnp.maximum(i * hb - 1, 0), 0))
    hnext = pl.BlockSpec((CONV_HALO, CONV_CH), lambda i: (jnp.minimum((i + 1) * hb, last_h), 0))
    return pl.pallas_call(
        body, grid=(nt,), name=name,
        in_specs=[spec, hnext, spec, spec, hprev, hprev, _fullspec((CONV_WIDTH, CONV_CH))],
        out_specs=[spec, spec, _fullspec((CONV_HALO * SUBLANES, CONV_CH)), _fullspec((SUBLANES, CONV_CH))],
        out_shape=[jax.ShapeDtypeStruct((s, CONV_CH), ACT)] * 2
        + [jax.ShapeDtypeStruct((CONV_HALO * SUBLANES, CONV_CH), F32), jax.ShapeDtypeStruct((SUBLANES, CONV_CH), F32)],
        scratch_shapes=[pltpu.VMEM((tm + CONV_HALO, CONV_CH), F32)] * 3,
        compiler_params=_cp("arbitrary"))(dc1, dc1, ga, gb, ga, gb, cw)


_GELU_K = math.sqrt(2.0 / math.pi)
_GELU_C = 0.044715


def _gelu(x):
    return 0.5 * x * (1.0 + jnp.tanh(_GELU_K * (x + _GELU_C * x * x * x)))


def _gelu_grad(x):
    t = jnp.tanh(_GELU_K * (x + _GELU_C * x * x * x))
    return 0.5 * (1.0 + t) + 0.5 * x * (1.0 - t * t) * _GELU_K * (1.0 + 3.0 * _GELU_C * x * x)


def _tril():
    qi = lax.broadcasted_iota(jnp.int32, (BLOCK, BLOCK), 0)
    kj = lax.broadcasted_iota(jnp.int32, (BLOCK, BLOCK), 1)
    return kj <= qi


GMLP_CHUNKS = 2


def _gmlp_weights(sw_ref, w_buf, wt_buf):
    tril = _tril()
    for g in range(D_GROUPS):
        w = jnp.where(tril, sw_ref[g], 0.0)
        w_buf[g] = w.astype(ACT)
        if wt_buf is not None:
            wt_buf[g] = w.T.astype(ACT)


def _gmlp_mix(w_buf, gn, sb_ref, m_buf):
    for c in range(GMLP_CHUNKS):
        rows = slice(c * BLOCK, (c + 1) * BLOCK)
        for g in range(D_GROUPS):
            cs = slice(g * HEAD_DIM, (g + 1) * HEAD_DIM)
            m_buf[rows, cs] = _dot(w_buf[g], gn[rows, cs], NN) + sb_ref[:, cs]


def gmlp_fwd(z, lg, lb, sw, sbx, name):
    s = z.shape[0]
    tm = GMLP_CHUNKS * BLOCK

    def body(z_ref, lg_ref, lb_ref, sw_ref, sb_ref, o_ref, w_buf, m_buf):
        @pl.when(pl.program_id(0) == 0)
        def _():
            _gmlp_weights(sw_ref, w_buf, None)

        zz = _gelu(z_ref[...].astype(F32))
        xh, _ = _ln_stats(zz[:, D_CH:])
        gn = (xh * lg_ref[...] + lb_ref[...]).astype(ACT)
        _gmlp_mix(w_buf, gn, sb_ref, m_buf)
        o_ref[...] = (zz[:, :D_CH] * m_buf[...]).astype(ACT)

    return pl.pallas_call(
        body, grid=(s // tm,), name=name,
        in_specs=[_rowspec(tm, 2 * D_CH), _fullspec((1, D_CH)), _fullspec((1, D_CH)),
                  _fullspec((D_GROUPS, BLOCK, BLOCK)), _fullspec((BLOCK, D_CH))],
        out_specs=_rowspec(tm, D_CH), out_shape=jax.ShapeDtypeStruct((s, D_CH), ACT),
        scratch_shapes=[pltpu.VMEM((D_GROUPS, BLOCK, BLOCK), ACT), pltpu.VMEM((tm, D_CH), F32)],
        compiler_params=_cp("arbitrary"))(z, lg, lb, sw, sbx)


def gmlp_bwd(z, dsrc, dcol, lg, lb, sw, sbx, name):
    s = z.shape[0]
    tm = GMLP_CHUNKS * BLOCK

    def body(z_ref, d_ref, lg_ref, lb_ref, sw_ref, sb_ref, dz_ref, dw_ref, dsb_ref, dg_ref, db_ref,
             w_buf, wt_buf, m_buf, dgn_buf):
        @pl.when(pl.program_id(0) == 0)
        def _():
            _gmlp_weights(sw_ref, w_buf, wt_buf)
            dw_ref[...] = jnp.zeros_like(dw_ref)
            dsb_ref[...] = jnp.zeros_like(dsb_ref)
            dg_ref[...] = jnp.zeros_like(dg_ref)
            db_ref[...] = jnp.zeros_like(db_ref)

        zr = z_ref[...].astype(F32)
        zz = _gelu(zr)
        u = zz[:, :D_CH]
        xh, rstd = _ln_stats(zz[:, D_CH:])
        gn = (xh * lg_ref[...] + lb_ref[...]).astype(ACT)
        dd = d_ref[...].astype(F32)
        _gmlp_mix(w_buf, gn, sb_ref, m_buf)
        dz_ref[:, :D_CH] = (dd * m_buf[...] * _gelu_grad(zr[:, :D_CH])).astype(ACT)
        dmix = dd * u
        dmix_a = dmix.astype(ACT)
        dsb = dmix[0:BLOCK]
        for c in range(1, GMLP_CHUNKS):
            dsb = dsb + dmix[c * BLOCK:(c + 1) * BLOCK]
        dsb_ref[...] = dsb_ref[...] + dsb
        tril = _tril()
        for g in range(D_GROUPS):
            cs = slice(g * HEAD_DIM, (g + 1) * HEAD_DIM)
            dw = None
            for c in range(GMLP_CHUNKS):
                rows = slice(c * BLOCK, (c + 1) * BLOCK)
                t = _dot(dmix_a[rows, cs], gn[rows, cs], NT)
                dw = t if dw is None else dw + t
                dgn_buf[rows, cs] = _dot(wt_buf[g], dmix_a[rows, cs], NN)
            dw_ref[g] = dw_ref[g] + jnp.where(tril, dw, 0.0)
        dgn = dgn_buf[...]
        dg_ref[0:1, :] = dg_ref[0:1, :] + jnp.sum(dgn * xh, axis=0, keepdims=True)
        db_ref[0:1, :] = db_ref[0:1, :] + jnp.sum(dgn, axis=0, keepdims=True)
        dxh = dgn * lg_ref[...]
        dgate = rstd * (dxh - jnp.mean(dxh, axis=-1, keepdims=True)
                        - xh * jnp.mean(dxh * xh, axis=-1, keepdims=True))
        dz_ref[:, D_CH:] = (dgate * _gelu_grad(zr[:, D_CH:])).astype(ACT)

    vec = _fullspec((1, D_CH))
    acc = _fullspec((8, D_CH))
    wshape = (D_GROUPS, BLOCK, BLOCK)
    return pl.pallas_call(
        body, grid=(s // tm,), name=name,
        in_specs=[_rowspec(tm, 2 * D_CH), _rowspec(tm, D_CH, dcol), vec, vec, _fullspec(wshape),
                  _fullspec((BLOCK, D_CH))],
        out_specs=[_rowspec(tm, 2 * D_CH), _fullspec(wshape), _fullspec((BLOCK, D_CH)), acc, acc],
        out_shape=[jax.ShapeDtypeStruct((s, 2 * D_CH), ACT), jax.ShapeDtypeStruct(wshape, F32),
                   jax.ShapeDtypeStruct((BLOCK, D_CH), F32),
                   jax.ShapeDtypeStruct((8, D_CH), F32), jax.ShapeDtypeStruct((8, D_CH), F32)],
        scratch_shapes=[pltpu.VMEM(wshape, ACT), pltpu.VMEM(wshape, ACT),
                        pltpu.VMEM((tm, D_CH), F32), pltpu.VMEM((tm, D_CH), F32)],
        compiler_params=_cp("arbitrary"))(z, dsrc, lg, lb, sw, sbx)


def _rms_bwd(dn, x, g):
    r = lax.rsqrt(jnp.mean(x * x, axis=-1, keepdims=True) + RMS_EPS)
    u = dn * g
    dx = r * u - x * (r * r * r) * jnp.mean(x * u, axis=-1, keepdims=True)
    return dx, dn * x * r


def final_loss(h, g, target, name):
    s = h.shape[0]
    tm = _tile(s, (256,))

    def body(h_ref, g_ref, t_ref, loss_ref, dh_ref, dg_ref):
        @pl.when(pl.program_id(0) == 0)
        def _():
            loss_ref[...] = jnp.zeros_like(loss_ref)
            dg_ref[...] = jnp.zeros_like(dg_ref)

        x = h_ref[...]
        r = lax.rsqrt(jnp.mean(x * x, axis=-1, keepdims=True) + RMS_EPS)
        diff = x * r * g_ref[...] - t_ref[...]
        part = jnp.sum(jnp.sum(diff * diff, axis=-1, keepdims=True), axis=0, keepdims=True)
        loss_ref[...] = loss_ref[...] + part * (0.5 / D_MODEL)
        dx, dgt = _rms_bwd(diff * (1.0 / D_MODEL), x, g_ref[...])
        dh_ref[...] = dx
        dg_ref[0:1, :] = dg_ref[0:1, :] + jnp.sum(dgt, axis=0, keepdims=True)

    spec = _rowspec(tm, D_MODEL)
    return pl.pallas_call(
        body, grid=(s // tm,), name=name,
        in_specs=[spec, _fullspec((1, D_MODEL)), spec],
        out_specs=[_fullspec((8, LANES)), spec, _fullspec((8, D_MODEL))],
        out_shape=[jax.ShapeDtypeStruct((8, LANES), F32), jax.ShapeDtypeStruct((s, D_MODEL), F32),
                   jax.ShapeDtypeStruct((8, D_MODEL), F32)],
        compiler_params=_cp("arbitrary"))(h, g, target)


def mm_nt(dy, w, name):
    s, n = dy.shape
    k = w.shape[0]
    tm = _tile(s, (512,))
    tk = _tile(k, (512,))

    def body(d_ref, w_ref, o_ref):
        o_ref[...] = _dot(d_ref[...].astype(ACT), w_ref[...], NT).astype(ACT)

    return pl.pallas_call(
        body, grid=(k // tk, s // tm), name=name,
        in_specs=[pl.BlockSpec((tm, n), lambda j, i: (i, 0)), pl.BlockSpec((tk, n), lambda j, i: (j, 0))],
        out_specs=pl.BlockSpec((tm, tk), lambda j, i: (i, j)),
        out_shape=jax.ShapeDtypeStruct((s, k), ACT),
        compiler_params=_cp("parallel", "parallel"))(dy, w)


def ffn_down_bwd(dh, wd, gate, up, name):
    s = dh.shape[0]
    f = wd.shape[0]
    tm = _tile(s, (512,))
    tf = _tile(f, (1408, 512, 256, 128))

    def body(d_ref, w_ref, g_ref, u_ref, dg_ref, du_ref):
        dact = _dot(d_ref[...].astype(ACT), w_ref[...], NT)
        g = g_ref[...].astype(F32)
        sg = _sigmoid(g)
        dg_ref[...] = (dact * u_ref[...].astype(F32) * (sg * (1.0 + g * (1.0 - sg)))).astype(ACT)
        du_ref[...] = (dact * g * sg).astype(ACT)

    tile = pl.BlockSpec((tm, tf), lambda j, i: (i, j))
    return pl.pallas_call(
        body, grid=(f // tf, s // tm), name=name,
        in_specs=[pl.BlockSpec((tm, D_MODEL), lambda j, i: (i, 0)),
                  pl.BlockSpec((tf, D_MODEL), lambda j, i: (j, 0)), tile, tile],
        out_specs=[tile, tile], out_shape=[jax.ShapeDtypeStruct((s, f), ACT)] * 2,
        compiler_params=_cp("parallel", "parallel"))(dh, wd, gate, up)


def mm_nt_rms(parts, h, g, dh, name):
    s = h.shape[0]
    tm = _tile(s, (256,))
    np_ = len(parts)

    def body(*refs):
        d_refs = refs[:np_]
        w_refs = refs[np_:2 * np_]
        h_ref, g_ref, dh_ref, o_ref, dg_ref = refs[2 * np_:]

        @pl.when(pl.program_id(0) == 0)
        def _():
            dg_ref[...] = jnp.zeros_like(dg_ref)

        dn = None
        for d_ref, w_ref in zip(d_refs, w_refs):
            t = _dot(d_ref[...], w_ref[...], NT)
            dn = t if dn is None else dn + t
        dx, dgt = _rms_bwd(dn, h_ref[...], g_ref[...])
        o_ref[...] = dh_ref[...] + dx
        dg_ref[0:1, :] = dg_ref[0:1, :] + jnp.sum(dgt, axis=0, keepdims=True)

    spec = _rowspec(tm, D_MODEL)
    return pl.pallas_call(
        body, grid=(s // tm,), name=name,
        in_specs=[_rowspec(tm, d.shape[1]) for d, _ in parts] + [_fullspec(w.shape) for _, w in parts]
        + [spec, _fullspec((1, D_MODEL)), spec],
        out_specs=[spec, _fullspec((8, D_MODEL))],
        out_shape=[jax.ShapeDtypeStruct((s, D_MODEL), F32), jax.ShapeDtypeStruct((8, D_MODEL), F32)],
        compiler_params=_cp("arbitrary"))(*[d for d, _ in parts], *[w for _, w in parts], h, g, dh)


def mm_tn(a, b, name):
    s, k = a.shape
    n = b.shape[1]
    tk = _tile(k, (512, 1408, 256, 128))
    tn = _tile(n, (1408, 1280, 1024, 896, 512, 256, 128))
    ts = _tile(s, (2048, 512) if b.dtype == ACT else (1024, 512))

    def body(a_ref, b_ref, o_ref):
        t = _dot(a_ref[...].astype(ACT), b_ref[...].astype(ACT), TN)

        @pl.when(pl.program_id(2) == 0)
        def _():
            o_ref[...] = t

        @pl.when(pl.program_id(2) > 0)
        def _():
            o_ref[...] = o_ref[...] + t

    return pl.pallas_call(
        body, grid=(k // tk, n // tn, s // ts), name=name,
        in_specs=[pl.BlockSpec((ts, tk), lambda i, j, t: (t, i)), pl.BlockSpec((ts, tn), lambda i, j, t: (t, j))],
        out_specs=pl.BlockSpec((tk, tn), lambda i, j, t: (i, j)),
        out_shape=jax.ShapeDtypeStruct((k, n), F32),
        compiler_params=_cp("parallel", "parallel", "arbitrary"))(a, b)


def _adamw_math(w, g, m, v):
    m = ADAM_B1 * m + (1.0 - ADAM_B1) * g
    v = ADAM_B2 * v + (1.0 - ADAM_B2) * (g * g)
    m_hat = m / (1.0 - ADAM_B1 ** ADAM_STEP)
    v_hat = v / (1.0 - ADAM_B2 ** ADAM_STEP)
    delta = -ADAM_LR * (m_hat / (jnp.sqrt(v_hat) + ADAM_EPS) + ADAM_WD * w)
    return delta, m, v


def sum_adamw(parts, w, m, v, layer, others, name):
    nl, r, c = w.shape
    tr = _tile(r, (256, 128, 64, 32, 16, 8))

    def body(p_ref, w_ref, m_ref, v_ref, *rest):
        g_ref, d_ref, mo_ref, vo_ref = rest[-4:]
        g = p_ref[0].astype(F32)
        for i in range(1, N_DEV):
            g = g + p_ref[i].astype(F32)
        d, mm, vv = _adamw_math(w_ref[...], g, m_ref[...], v_ref[...])
        g_ref[...] = g
        d_ref[...] = d
        mo_ref[...] = mm
        vo_ref[...] = vv

    spec = pl.BlockSpec((None, tr, c), lambda i: (layer, i, 0))
    in_specs = [pl.BlockSpec((N_DEV, tr, c), lambda i: (0, i, 0))] + [spec] * 3
    args = [parts, w, m, v]
    aliases = {}
    if others is not None:
        in_specs += [pl.BlockSpec(memory_space=pl.ANY)] * 4
        args += list(others)
        aliases = {4 + j: j for j in range(4)}
    return pl.pallas_call(
        body, grid=(r // tr,), name=name, in_specs=in_specs, out_specs=[spec] * 4,
        out_shape=[jax.ShapeDtypeStruct((nl, r, c), F32)] * 4, input_output_aliases=aliases,
        compiler_params=_cp("parallel"))(*args)


def cast_layers(items, name):
    n = len(items)

    def body(*refs):
        for i in range(n):
            refs[n + i][...] = refs[i][...].astype(ACT)

    return pl.pallas_call(
        body, grid=(1,), name=name,
        in_specs=[pl.BlockSpec((None,) + w.shape[1:], lambda i, l=l: (l, 0, 0)) for w, l in items],
        out_specs=[_fullspec(w.shape[1:]) for w, _ in items],
        out_shape=[jax.ShapeDtypeStruct(w.shape[1:], ACT) for w, _ in items],
        compiler_params=_cp("arbitrary"))(*[w for w, _ in items])


def adamw_small(ws, gs, ms, vs, name):
    k = len(ws)

    def body(*refs):
        for i in range(k):
            w_ref, g_ref, m_ref, v_ref = (refs[j * k + i] for j in range(4))
            d, mm, vv = _adamw_math(w_ref[...], g_ref[...], m_ref[...], v_ref[...])
            refs[4 * k + i][...] = d
            refs[5 * k + i][...] = mm
            refs[6 * k + i][...] = vv

    shapes = [jax.ShapeDtypeStruct(w.shape, F32) for w in ws]
    specs = [_fullspec(w.shape) for w in ws]
    out = pl.pallas_call(
        body, grid=(1,), name=name, in_specs=specs * 4, out_specs=specs * 3, out_shape=shapes * 3,
        compiler_params=_cp("arbitrary"))(*ws, *gs, *ms, *vs)
    return out[:k], out[k:2 * k], out[2 * k:]


def sum_slots(x, out_dtype, name):
    g, r, c = x.shape
    tr = r if x.size * x.dtype.itemsize <= SMALL_BLOCK_BYTES else _tile(r, (256, 128, 64, 32, 16, 8))

    def body(x_ref, o_ref):
        acc = x_ref[0].astype(F32)
        for i in range(1, g):
            acc = acc + x_ref[i].astype(F32)
        o_ref[...] = acc.astype(o_ref.dtype)

    return pl.pallas_call(
        body, grid=(r // tr,), name=name,
        in_specs=[pl.BlockSpec((g, tr, c), lambda i: (0, i, 0))], out_specs=_rowspec(tr, c),
        out_shape=jax.ShapeDtypeStruct((r, c), out_dtype),
        compiler_params=_cp("parallel"))(x)


HBM_SPEC = pl.BlockSpec(memory_space=pltpu.HBM)
SEM_SPEC = pl.BlockSpec(memory_space=pltpu.SEMAPHORE)
DATAFLOW = pltpu.SideEffectType.DATAFLOW_SIDE_EFFECTING


def _my_rank():
    return 4 * lax.axis_index("x") + 2 * lax.axis_index("y") + lax.axis_index("c")


def _exchange_copies(x_refs, land_refs, send, recv, a2a):
    pos = [lax.axis_index(a) for a in AXES]
    me = _my_rank()
    copies = []
    for x_ref, land_ref, s_ref, r_ref in zip(x_refs, land_refs, send, recv):
        for k in range(N_DEV - 1):
            bits = ((k + 1) >> 2 & 1, (k + 1) >> 1 & 1, (k + 1) & 1)
            peer = tuple(1 - p if b else p for p, b in zip(pos, bits))
            prank = 4 * peer[0] + 2 * peer[1] + peer[2]
            copies.append(pltpu.make_async_remote_copy(
                src_ref=x_ref.at[prank] if a2a else x_ref, dst_ref=land_ref.at[me],
                send_sem=s_ref.at[k], recv_sem=r_ref.at[k], device_id=peer, device_id_type=MESH))
    return copies


def exchange_start(xs, a2a, name, after=None):
    n = len(xs)
    me = _my_rank()
    lands = []
    for x in xs:
        own = lax.dynamic_index_in_dim(x, me, 0, keepdims=True) if a2a else x[None]
        shape = x.shape if a2a else (N_DEV,) + x.shape
        lands.append(lax.dynamic_update_slice(lax.empty(shape, x.dtype), own, (me,) + (0,) * (len(shape) - 1)))

    def body(*refs):
        x_refs, land_refs = refs[:n], refs[n:2 * n]
        outs = refs[len(refs) - 4 * n - 1:]
        for cp in _exchange_copies(x_refs, land_refs, outs[:n], outs[n:2 * n], a2a):
            cp.start()
        token = outs[4 * n]
        token[...] = jnp.zeros_like(token)

    sems = [pltpu.SemaphoreType.DMA((N_DEV - 1,))] * n
    out = pl.pallas_call(
        body, name=name,
        out_shape=tuple(sems + sems + [pltpu.HBM(x.shape, x.dtype) for x in xs]
                        + [pltpu.HBM(l.shape, l.dtype) for l in lands] + [jax.ShapeDtypeStruct((8, LANES), F32)]),
        in_specs=[HBM_SPEC] * (2 * n) + ([] if after is None else [pl.BlockSpec(memory_space=pl.ANY)]),
        out_specs=tuple([SEM_SPEC] * (2 * n) + [HBM_SPEC] * (2 * n) + [pl.BlockSpec(memory_space=pltpu.VMEM)]),
        input_output_aliases={i: 2 * n + i for i in range(2 * n)},
        compiler_params=pltpu.CompilerParams(has_side_effects=DATAFLOW),
    )(*[pltpu.with_memory_space_constraint(a, pltpu.HBM) for a in list(xs) + lands], *([] if after is None else [after]))
    return (out[:n], out[n:2 * n], out[2 * n:3 * n], out[3 * n:4 * n]), out[4 * n]


def exchange_wait(handles, after, a2a, name):
    send, recv, x_thru, land_thru = handles
    n = len(x_thru)

    def body(*refs):
        x_refs, land_refs = refs[:n], refs[n:2 * n]
        s_refs, r_refs = refs[2 * n:3 * n], refs[3 * n:4 * n]
        for cp in _exchange_copies(x_refs, land_refs, s_refs, r_refs, a2a):
            cp.wait_send()
            cp.wait_recv()

    out = pl.pallas_call(
        body, name=name,
        out_shape=tuple([pltpu.HBM(a.shape, a.dtype) for a in list(x_thru) + list(land_thru)]),
        in_specs=[HBM_SPEC] * (2 * n) + [SEM_SPEC] * (2 * n) + [pl.BlockSpec(memory_space=pl.ANY)],
        out_specs=tuple([HBM_SPEC] * (2 * n)),
        input_output_aliases={i: i for i in range(2 * n)},
        compiler_params=pltpu.CompilerParams(has_side_effects=DATAFLOW),
    )(*x_thru, *land_thru, *send, *recv, after)
    return out[n:2 * n]


def _local_step(x, target, weight, emit, P):
    s = x.shape[0]
    tabs = _rope_tables(s)
    sinkb = jnp.broadcast_to(P["ev_sinks"].reshape(N_HEADS, 1), (N_HEADS, LANES))
    sbx = jnp.repeat(P["od_spatial_b"].reshape(D_GROUPS, BLOCK).T, HEAD_DIM, axis=1)
    sw = P["od_spatial_w"].reshape(D_GROUPS, BLOCK, BLOCK)
    fg = P["ffn_norm_g"]
    latest = [None]

    def out(name, layer, grad):
        tok = emit(name, layer, grad)
        if tok is not None:
            latest[0] = tok

    def dep(a):
        return a if latest[0] is None else a + latest[0][0:1, 0:1]

    n0 = rmsnorm(x, P["ev_norm_g"], "rms_in")
    qk_e, v_e, ga, gb = inproj(n0, weight("ev_w_in", 0, n0), tabs, ATT_W + 128,
                               (ATT_W + 128, 128, CONV_CH, CONV_CH), "ev_inproj")
    a_e, lse_e = attn_fwd(qk_e, v_e, d=1, hkv=A_KV_HEADS, max_dist=BLOCK - 1, sink=sinkb, out_dtype=ACT,
                          name="ev_attn")
    c_act, c1 = conv_fwd(ga, gb, P["ev_conv_w"], P["ev_conv_b"], P["ev_conv_ln_g"], P["ev_conv_ln_b"], "ev_conv")
    h1, n1 = mm_res([a_e, c_act], weight("ev_w_out", 0, c_act), x, fg[0:1], "ev_outproj")
    gate0, up0, act0 = ffn_up(n1, weight("ffn_w_gate", 0, n1), weight("ffn_w_up", 0, n1), "ffn0_up")
    h2, n2 = mm_res([act0], weight("ffn_w_down", 0, act0), h1, P["od_norm_g"], "ffn0_down")
    qk_o, v_o, z = inproj(n2, weight("od_w_in", 0, n2), tabs, 2 * ATT_W, (2 * ATT_W, ATT_W, 2 * D_CH), "od_inproj")
    outs, lses = [], []
    for window, dil in DILATED:
        assert window // dil == BLOCK
        o_r, l_r = attn_fwd(qk_o, v_o, d=dil, hkv=N_HEADS, max_dist=BLOCK, sink=None, out_dtype=F32,
                            name="od_attn_d%d" % dil)
        outs.append(o_r)
        lses.append(l_r)
    c_out, lse_o = combine_fwd(outs, lses, "od_combine")
    d_out = gmlp_fwd(z, P["od_sgu_ln_g"], P["od_sgu_ln_b"], sw, sbx, "od_gmlp")
    h3, n3 = mm_res([c_out, d_out], weight("od_w_out", 0, d_out), h2, fg[1:2], "od_outproj")
    gate1, up1, act1 = ffn_up(n3, weight("ffn_w_gate", 1, n3), weight("ffn_w_up", 1, n3), "ffn1_up")
    h4, _ = mm_res([act1], weight("ffn_w_down", 1, act1), h3, None, "ffn1_down")
    loss_part, dh4, dg_final = final_loss(h4, P["final_norm_g"], target, "loss_head")

    def ffn_bwd(layer, dh_out, h_in, n_in, gate, up, act):
        wg, wu, wd = (weight(n, layer, dh_out) for n in ("ffn_w_gate", "ffn_w_up", "ffn_w_down"))
        tag = "ffn%d" % layer
        dgate, dup = ffn_down_bwd(dh_out, wd, gate, up, tag + "_down_bwd")
        g_wd = mm_tn(act, dh_out, tag + "_dwd")
        dh_in, dgn = mm_nt_rms([(dgate, wg), (dup, wu)], h_in, dep(fg[layer:layer + 1]), dh_out, tag + "_up_bwd")
        out("ffn_w_down", layer, g_wd)
        out("ffn_w_gate", layer, mm_tn(n_in, dgate, tag + "_dwg"))
        out("ffn_w_up", layer, mm_tn(n_in, dup, tag + "_dwu"))
        return dh_in, dgn[0:1]

    dh3, dgn_f1 = ffn_bwd(1, dh4, h3, n3, gate1, up1, act1)

    dcd = mm_nt(dh3, weight("od_w_out", 0, dh3), "od_outproj_bwd")
    dz, g_sw, g_sbx, g_slg, g_slb = gmlp_bwd(z, dcd, 1, dep(P["od_sgu_ln_g"]), P["od_sgu_ln_b"], sw, sbx,
                                             "od_gmlp_bwd")
    g_sb = jnp.sum(g_sbx.reshape(BLOCK, D_GROUPS, HEAD_DIM), axis=-1).T
    out("od_spatial", 0, jnp.concatenate([g_sw.reshape(D_GROUPS * BLOCK, BLOCK), g_sb], axis=0))
    out("od_w_out", 0, jnp.concatenate([mm_tn(c_out, dh3, "od_dwo_c"), mm_tn(d_out, dh3, "od_dwo_d")], axis=0))
    dqkv = [attn_bwd(qk_o, v_o, dcd if dil == 1 else dcd[:, :ATT_W], 0, c_out, lse_o, d=dil, hkv=N_HEADS,
                     max_dist=BLOCK, sink=None, out_dtype=ACT, name="od_attn_bwd_d%d" % dil)
            for window, dil in DILATED]
    dproj_o = assemble([([b[0] for b in dqkv], True), ([b[1] for b in dqkv], True), ([b[2] for b in dqkv], False),
                        ([dz], False)], tabs, "od_dproj")
    dh2, dgn_od = mm_nt_rms([(dproj_o, weight("od_w_in", 0, dproj_o))], h2, dep(P["od_norm_g"]), dh3,
                            "od_inproj_bwd")
    out("od_w_in", 0, mm_tn(n2, dproj_o, "od_dwi"))

    dh1, dgn_f0 = ffn_bwd(0, dh2, h1, n1, gate0, up0, act0)

    dac = mm_nt(dh1, weight("ev_w_out", 0, dh1), "ev_outproj_bwd")
    dc1, g_clg, g_clb = conv_bwd_ln(c1, dac, 1, dep(P["ev_conv_ln_g"]), P["ev_conv_ln_b"], "ev_conv_bwd_ln")
    out("ev_w_out", 0, jnp.concatenate([mm_tn(a_e, dh1, "ev_dwo_a"), mm_tn(c_act, dh1, "ev_dwo_c")], axis=0))
    dga, dgb, g_cw, g_cb = conv_bwd_conv(dc1, ga, gb, P["ev_conv_w"], "ev_conv_bwd")
    dq, dk, dv, dsink = attn_bwd(qk_e, v_e, dac, 0, a_e, lse_e, d=1, hkv=A_KV_HEADS, max_dist=BLOCK - 1,
                                 sink=sinkb, out_dtype=F32, name="ev_attn_bwd")
    dproj_e = assemble([([dq], True), ([dk], True), ([dv], False), ([dga], False), ([dgb], False)], tabs,
                       "ev_dproj")
    out("ev_w_in", 0, mm_tn(n0, dproj_e, "ev_dwi"))
    dx, dgn_ev = mm_nt_rms([(dproj_e, weight("ev_w_in", 0, dproj_e))], x, dep(P["ev_norm_g"]), dh1,
                           "ev_inproj_bwd")

    small = {
        "ev_norm_g": dgn_ev[0:1],
        "ev_sinks": dsink[:, 0:1].reshape(1, N_HEADS),
        "ev_conv_w": jnp.sum(g_cw.reshape(CONV_HALO, SUBLANES, CONV_CH), axis=1)[:CONV_WIDTH],
        "ev_conv_b": jnp.sum(g_cb, axis=0, keepdims=True),
        "ev_conv_ln_g": g_clg[0:1],
        "ev_conv_ln_b": g_clb[0:1],
        "od_norm_g": dgn_od[0:1],
        "od_sgu_ln_g": g_slg[0:1],
        "od_sgu_ln_b": g_slb[0:1],
        "od_spatial_w": g_sw.reshape(D_GROUPS * BLOCK, BLOCK),
        "od_spatial_b": g_sb,
        "ffn_norm_g": jnp.concatenate([dgn_f0, dgn_f1], axis=0),
        "final_norm_g": dg_final[0:1],
    }
    return loss_part, dx, small


BIG = ("ev_w_in", "ev_w_out", "od_w_in", "od_w_out", "ffn_w_gate", "ffn_w_up", "ffn_w_down")
COL_SHARDED = ("ev_w_in", "od_w_in", "ffn_w_gate", "ffn_w_up")
GATHER_GROUPS = (
    (("ev_w_in", 0),),
    (("ev_w_out", 0),),
    (("ffn_w_gate", 0), ("ffn_w_up", 0)),
    (("ffn_w_down", 0),),
    (("od_w_in", 0),),
    (("od_w_out", 0),),
    (("ffn_w_gate", 1), ("ffn_w_up", 1)),
    (("ffn_w_down", 1),),
)
GATHER_EARLY = 4
GATHER_LATE_AT = 2
REDUCE_GROUPS = (
    (("ffn_w_down", 1), ("ffn_w_gate", 1), ("ffn_w_up", 1)),
    (("od_w_out", 0),),
    (("od_w_in", 0),),
    (("ffn_w_down", 0), ("ffn_w_gate", 0), ("ffn_w_up", 0)),
    (("ev_w_out", 0),),
    (("ev_w_in", 0),),
)


def _unshard(name, g):
    if name in COL_SHARDED:
        return jnp.moveaxis(g, 0, 1).reshape(g.shape[1], N_DEV * g.shape[2])
    return g.reshape(N_DEV * g.shape[1], g.shape[2])


def _shard_slots(name, full):
    r, c = full.shape
    if name in COL_SHARDED:
        return jnp.moveaxis(full.reshape(r, N_DEV, c // N_DEV), 1, 0)
    return full.reshape(N_DEV, r // N_DEV, c)


def kernel(x, ev_norm_g, ev_w_in, ev_sinks, ev_conv_w, ev_conv_b, ev_conv_ln_g, ev_conv_ln_b, ev_w_out, od_norm_g, od_w_in, od_sgu_ln_g, od_sgu_ln_b, od_spatial_w, od_spatial_b, od_w_out, ffn_norm_g, ffn_w_gate, ffn_w_up, ffn_w_down, final_norm_g, loss_target, m_ev_norm_g, m_ev_w_in, m_ev_sinks, m_ev_conv_w, m_ev_conv_b, m_ev_conv_ln_g, m_ev_conv_ln_b, m_ev_w_out, m_od_norm_g, m_od_w_in, m_od_sgu_ln_g, m_od_sgu_ln_b, m_od_spatial_w, m_od_spatial_b, m_od_w_out, m_ffn_norm_g, m_ffn_w_gate, m_ffn_w_up, m_ffn_w_down, m_final_norm_g, v_ev_norm_g, v_ev_w_in, v_ev_sinks, v_ev_conv_w, v_ev_conv_b, v_ev_conv_ln_g, v_ev_conv_ln_b, v_ev_w_out, v_od_norm_g, v_od_w_in, v_od_sgu_ln_g, v_od_sgu_ln_b, v_od_spatial_w, v_od_spatial_b, v_od_w_out, v_ffn_norm_g, v_ffn_w_gate, v_ffn_w_up, v_ffn_w_down, v_final_norm_g):
    names = ["ev_norm_g", "ev_w_in", "ev_sinks", "ev_conv_w", "ev_conv_b", "ev_conv_ln_g", "ev_conv_ln_b", "ev_w_out",
             "od_norm_g", "od_w_in", "od_sgu_ln_g", "od_sgu_ln_b", "od_spatial_w", "od_spatial_b", "od_w_out",
             "ffn_norm_g", "ffn_w_gate", "ffn_w_up", "ffn_w_down", "final_norm_g"]
    wts = dict(zip(names, [ev_norm_g, ev_w_in, ev_sinks, ev_conv_w, ev_conv_b, ev_conv_ln_g, ev_conv_ln_b, ev_w_out,
                           od_norm_g, od_w_in, od_sgu_ln_g, od_sgu_ln_b, od_spatial_w, od_spatial_b, od_w_out,
                           ffn_norm_g, ffn_w_gate, ffn_w_up, ffn_w_down, final_norm_g]))
    mom = dict(zip(names, [m_ev_norm_g, m_ev_w_in, m_ev_sinks, m_ev_conv_w, m_ev_conv_b, m_ev_conv_ln_g, m_ev_conv_ln_b,
                           m_ev_w_out, m_od_norm_g, m_od_w_in, m_od_sgu_ln_g, m_od_sgu_ln_b, m_od_spatial_w,
                           m_od_spatial_b, m_od_w_out, m_ffn_norm_g, m_ffn_w_gate, m_ffn_w_up, m_ffn_w_down,
                           m_final_norm_g]))
    vel = dict(zip(names, [v_ev_norm_g, v_ev_w_in, v_ev_sinks, v_ev_conv_w, v_ev_conv_b, v_ev_conv_ln_g, v_ev_conv_ln_b,
                           v_ev_w_out, v_od_norm_g, v_od_w_in, v_od_sgu_ln_g, v_od_sgu_ln_b, v_od_spatial_w,
                           v_od_spatial_b, v_od_w_out, v_ffn_norm_g, v_ffn_w_gate, v_ffn_w_up, v_ffn_w_down,
                           v_final_norm_g]))
    me = _my_rank()

    sp = jnp.zeros((40, LANES), F32)
    sp = sp.at[0:CONV_WIDTH, 0:64].set(ev_conv_w[0])
    sp = sp.at[32, :].set(od_norm_g[0])
    sp = sp.at[33, 0:64].set(od_sgu_ln_g[0])
    sp = sp.at[34, 0:64].set(od_sgu_ln_b[0])

    early = [k for grp in GATHER_GROUPS[:GATHER_EARLY] for k in grp]
    late = [k for grp in GATHER_GROUPS[GATHER_EARLY:] for k in grp]
    early_act = dict(zip(early, cast_layers([(wts[n], l) for n, l in early], "cast_early")))
    late_act = dict(zip(late, cast_layers([(wts[n], l) for n, l in late], "cast_late")))
    ag_early, ag_token = exchange_start([sp] + [early_act[k] for k in early], False, "ag_start")
    ag_late = []
    full_w = {}
    P = {
        "ev_norm_g": ev_norm_g + ag_token[0:1, 0:1], "ev_sinks": ev_sinks, "ev_conv_b": ev_conv_b,
        "ev_conv_ln_g": ev_conv_ln_g, "ev_conv_ln_b": ev_conv_ln_b,
        "od_spatial_w": od_spatial_w, "od_spatial_b": od_spatial_b, "ffn_norm_g": ffn_norm_g,
        "final_norm_g": final_norm_g.reshape(1, D_MODEL),
    }

    def weight(name, layer, after):
        if (name, layer) not in full_w:
            gi = [i for i, grp in enumerate(GATHER_GROUPS) if (name, layer) in grp][0]
            if gi < GATHER_EARLY:
                idx = [1 + early.index(k) for k in GATHER_GROUPS[gi]]
                handles = ag_early
            else:
                idx = [late.index(k) for k in GATHER_GROUPS[gi]]
                handles = ag_late[0]
            if gi == 0:
                idx = [0] + idx
            lands = exchange_wait(tuple([h[i] for i in idx] for h in handles), after, False, "ag_wait%d" % gi)
            if gi == 0:
                spg, lands = lands[0], lands[1:]
                P["ev_conv_w"] = jnp.moveaxis(spg[:, 0:CONV_WIDTH, 0:64], 0, 1).reshape(CONV_WIDTH, CONV_CH)
                P["od_norm_g"] = spg[:, 32, :].reshape(1, D_MODEL)
                P["od_sgu_ln_g"] = spg[:, 33, 0:64].reshape(1, D_CH)
                P["od_sgu_ln_b"] = spg[:, 34, 0:64].reshape(1, D_CH)
            if gi == GATHER_LATE_AT:
                ag_late.append(exchange_start([late_act[k] for k in late], False, "ag_start_late", after=lands[0])[0])
            for k, land in zip(GATHER_GROUPS[gi], lands):
                full_w[k] = _unshard(k[0], land)
        return full_w[(name, layer)]

    pending, rs_started, spatial = {}, [], []

    def emit(name, layer, grad):
        if name == "od_spatial":
            handles, token = exchange_start([grad], False, "ar_start_b")
            spatial.append(handles)
            return token
        pending[(name, layer)] = grad
        for gi, grp in enumerate(REDUCE_GROUPS):
            if (name, layer) in grp and all(k in pending for k in grp):
                handles, token = exchange_start([_shard_slots(k[0], pending[k]).astype(ACT) for k in grp], True,
                                                "rs_start%d" % gi)
                rs_started.append((gi, handles))
                return token
        return None

    loss_part, dx, small = _local_step(x[0], loss_target[0], weight, emit, P)
    loss = lax.psum(loss_part[0, 0], AXES)

    wide = ["ev_norm_g", "ev_sinks", "ev_conv_w", "ev_conv_b", "ev_conv_ln_g", "ev_conv_ln_b", "od_norm_g",
            "od_sgu_ln_g", "od_sgu_ln_b", "ffn_norm_g", "final_norm_g"]
    blk_a = jnp.concatenate(
        [jnp.pad(small[n], ((0, 0), (0, D_MODEL - small[n].shape[1]))) for n in wide], axis=0)
    blk_a = jnp.pad(blk_a, ((0, 48 - blk_a.shape[0]), (0, 0)))
    ar_a, ar_token = exchange_start([blk_a], False, "ar_start_a")

    results = {}
    for gi, handles in rs_started:
        lands = exchange_wait(handles, ar_token, True, "rs_wait%d" % gi)
        for (n, l), land in zip(REDUCE_GROUPS[gi], lands):
            results[n] = sum_adamw(land, wts[n], mom[n], vel[n], l, results.get(n), "adamw_%s%d" % (n, l))
    out_g, out_d, out_m, out_v = {}, {}, {}, {}
    for n in BIG:
        out_g[n], out_d[n], out_m[n], out_v[n] = results[n]

    last = results[REDUCE_GROUPS[-1][-1][0]][0]
    sum_b = sum_slots(exchange_wait(spatial[0], last, False, "ar_wait_b")[0], F32, "ar_sum_b")
    sum_a = sum_slots(exchange_wait(ar_a, last, False, "ar_wait_a")[0], F32, "ar_sum_a")
    full = {}
    off = 0
    for n in wide:
        r_, c_ = small[n].shape
        full[n] = sum_a[off:off + r_, 0:c_]
        off += r_
    full["od_spatial_w"] = sum_b[0:D_GROUPS * BLOCK]
    full["od_spatial_b"] = sum_b[D_GROUPS * BLOCK:D_GROUPS * BLOCK + D_GROUPS]
    full["ev_conv_w"] = lax.dynamic_slice_in_dim(full["ev_conv_w"], me * 64, 64, axis=1)
    full["od_norm_g"] = lax.dynamic_slice_in_dim(full["od_norm_g"], me * 128, 128, axis=1)
    full["od_sgu_ln_g"] = lax.dynamic_slice_in_dim(full["od_sgu_ln_g"], me * 64, 64, axis=1)
    full["od_sgu_ln_b"] = lax.dynamic_slice_in_dim(full["od_sgu_ln_b"], me * 64, 64, axis=1)

    small_names = [n for n in names if n not in BIG]
    view = {n: ((-1, wts[n].shape[-1]) if wts[n].ndim > 1 else (1, -1)) for n in small_names}
    ds, ms, vs = adamw_small([wts[n].reshape(view[n]) for n in small_names],
                             [full[n].reshape(view[n]) for n in small_names],
                             [mom[n].reshape(view[n]) for n in small_names],
                             [vel[n].reshape(view[n]) for n in small_names], "adamw_small")
    for i, n in enumerate(small_names):
        shp = wts[n].shape
        out_g[n], out_d[n], out_m[n], out_v[n] = (full[n].reshape(shp), ds[i].reshape(shp), ms[i].reshape(shp),
                                                  vs[i].reshape(shp))

    return (loss, dx[None], *[out_g[n] for n in names], *[out_d[n] for n in names],
            *[out_m[n] for n in names], *[out_v[n] for n in names])
```

```python
import functools
import math

import jax
import jax.numpy as jnp
from jax import lax
from jax.experimental import pallas as pl
from jax.experimental.pallas import tpu as pltpu

F32 = jnp.float32
ACT = jnp.bfloat16

D_MODEL = 1024
HEAD_DIM = 64
N_HEADS = 8
ATT_W = N_HEADS * HEAD_DIM
A_KV_HEADS = 2
CONV_CH = 512
CONV_WIDTH = 31
CONV_HALO = 32
D_CH = 512
D_GROUPS = 8
BLOCK = 128
D_FF = 2816
ROT_DIM = 16
ROPE_THETA = 500000.0
RMS_EPS = 1e-6
LN_EPS = 1e-5
DILATED = ((128, 1), (512, 4), (2048, 16))
NEG = -1e30
LANES = 128

ADAM_LR = 0.001
ADAM_B1 = 0.9
ADAM_B2 = 0.999
ADAM_EPS = 1e-08
ADAM_WD = 0.01
ADAM_STEP = 10

V7X_VMEM_LIMIT = 56 * 1024 * 1024
SMALL_BLOCK_BYTES = 6 * 1024 * 1024
N_DEV = 8

NN = (((1,), (0,)), ((), ()))
NT = (((1,), (1,)), ((), ()))
TN = (((0,), (0,)), ((), ()))
MESH = pl.DeviceIdType.MESH
AXES = ("x", "y", "c")


def _dot(a, b, dims):
    return lax.dot_general(a, b, dims, preferred_element_type=F32)


def _cp(*sem):
    return pltpu.CompilerParams(dimension_semantics=sem if sem else None,
                                vmem_limit_bytes=V7X_VMEM_LIMIT)


def _tile(n, prefs):
    for p in prefs:
        if n % p == 0:
            return p
    return n


def _sigmoid(x):
    return 1.0 / (1.0 + jnp.exp(-x))


def _rowspec(tm, w, col=0):
    return pl.BlockSpec((tm, w), lambda i, col=col: (i, col))


def _fullspec(shape):
    nd = len(shape)
    return pl.BlockSpec(shape, lambda *a, nd=nd: (0,) * nd)


def _rope_tables(seq):
    half = ROT_DIM // 2
    inv_freq = ROPE_THETA ** (-jnp.arange(half, dtype=F32) * (2.0 / ROT_DIM))
    ang = jnp.arange(seq, dtype=jnp.int32).astype(F32)[:, None] * inv_freq[None, :]
    cos, sin = jnp.cos(ang), jnp.sin(ang)
    lane = jnp.arange(LANES)
    jm = lane % HEAD_DIM
    idx = jm % half
    c = jnp.where(jm[None, :] < ROT_DIM, cos[:, idx], 1.0)
    sa = jnp.where(jm[None, :] < half, -sin[:, idx], 0.0)
    sb = jnp.where((jm[None, :] >= half) & (jm[None, :] < ROT_DIM), sin[:, idx], 0.0)
    return c.astype(F32), sa.astype(F32), sb.astype(F32)


def _rope(x, c, sa, sb):
    return x * c + pltpu.roll(x, LANES - 8, 1) * sa + pltpu.roll(x, 8, 1) * sb


def _rope_t(d, c, sa, sb):
    return d * c + pltpu.roll(d * sa, 8, 1) + pltpu.roll(d * sb, LANES - 8, 1)


def rmsnorm(h, g, name):
    s = h.shape[0]
    tm = _tile(s, (512,))

    def body(h_ref, g_ref, o_ref):
        x = h_ref[...]
        r = lax.rsqrt(jnp.mean(x * x, axis=-1, keepdims=True) + RMS_EPS)
        o_ref[...] = (x * r * g_ref[...]).astype(o_ref.dtype)

    return pl.pallas_call(
        body, grid=(s // tm,), name=name,
        in_specs=[_rowspec(tm, D_MODEL), _fullspec((1, D_MODEL))],
        out_specs=_rowspec(tm, D_MODEL),
        out_shape=jax.ShapeDtypeStruct((s, D_MODEL), ACT),
        compiler_params=_cp("parallel"))(h, g)


def _fold_perm(tm, d, inverse):
    m = tm // d
    a = lax.broadcasted_iota(jnp.int32, (tm, tm), 1 if inverse else 0)
    b = lax.broadcasted_iota(jnp.int32, (tm, tm), 0 if inverse else 1)
    src = (a & (m - 1)) * d + (a >> (m.bit_length() - 1))
    return (b == src).astype(ACT)


def _fold_rows(x, d):
    tm = x.shape[0]
    m = tm // d
    p = _dot(_fold_perm(tm, d, False), x, NN).astype(ACT)
    return jnp.concatenate([p[r * m:(r + 1) * m] for r in range(d)], axis=1)


def _unfold_rows(blk, d):
    w = blk.shape[1] // d
    stacked = jnp.concatenate([blk[:, r * w:(r + 1) * w] for r in range(d)], axis=0)
    return _dot(_fold_perm(stacked.shape[0], d, True), stacked, NN)


def inproj(n, w, tabs, nqk, splits, fold_dils, name):
    s = n.shape[0]
    ntot = w.shape[1]
    assert sum(splits) == ntot and splits[0] == nqk
    tm = _tile(s, (256,))
    ns = len(splits)

    def body(n_ref, w_ref, c_ref, sa_ref, sb_ref, *outs):
        res = _dot(n_ref[...], w_ref[...], NN)
        c, sa, sb = c_ref[...], sa_ref[...], sb_ref[...]
        for g in range(nqk // LANES):
            x = res[:, g * LANES:(g + 1) * LANES]
            outs[0][:, g * LANES:(g + 1) * LANES] = _rope(x, c, sa, sb).astype(ACT)
        off = nqk
        for o_ref, wd in zip(outs[1:ns], splits[1:]):
            o_ref[...] = res[:, off:off + wd].astype(ACT)
            off += wd
        for i, d in enumerate(fold_dils):
            outs[ns + 2 * i][...] = _fold_rows(outs[0][...], d)
            outs[ns + 2 * i + 1][...] = _fold_rows(outs[1][...], d)

    out_specs = [_rowspec(tm, wd) for wd in splits]
    out_shape = [jax.ShapeDtypeStruct((s, wd), ACT) for wd in splits]
    for d in fold_dils:
        for wd in splits[:2]:
            out_specs.append(_rowspec(tm // d, d * wd))
            out_shape.append(jax.ShapeDtypeStruct((s // d, d * wd), ACT))
    return pl.pallas_call(
        body, grid=(s // tm,), name=name,
        in_specs=[_rowspec(tm, D_MODEL), _fullspec((D_MODEL, ntot))] + [_rowspec(tm, LANES)] * 3,
        out_specs=out_specs, out_shape=out_shape,
        compiler_params=_cp("parallel"))(n, w, *tabs)


def ffn_up(n, wg, wu, name):
    s = n.shape[0]
    f = wg.shape[1]
    tm = _tile(s, (512,))
    tf = _tile(f, (1408, 512, 256, 128))

    def body(n_ref, wg_ref, wu_ref, g_ref, u_ref, a_ref):
        a = n_ref[...]
        g = _dot(a, wg_ref[...], NN)
        u = _dot(a, wu_ref[...], NN)
        sg = _sigmoid(g)
        silu = g * sg
        g_ref[...] = (u * (sg * (1.0 + g * (1.0 - sg)))).astype(ACT)
        u_ref[...] = silu.astype(ACT)
        a_ref[...] = (silu * u).astype(ACT)

    wspec = pl.BlockSpec((D_MODEL, tf), lambda j, i: (0, j))
    ospec = pl.BlockSpec((tm, tf), lambda j, i: (i, j))
    return pl.pallas_call(
        body, grid=(f // tf, s // tm), name=name,
        in_specs=[pl.BlockSpec((tm, D_MODEL), lambda j, i: (i, 0)), wspec, wspec],
        out_specs=[ospec] * 3,
        out_shape=[jax.ShapeDtypeStruct((s, f), ACT)] * 3,
        compiler_params=_cp("parallel", "parallel"))(n, wg, wu)


def mm_res(parts, w, h, gnext, name):
    s = h.shape[0]
    tm = _tile(s, (256,))
    widths = [p.shape[1] for p in parts]
    assert sum(widths) == w.shape[0]
    np_ = len(parts)

    def body(*refs):
        p_refs = refs[:np_]
        w_ref, h_ref = refs[np_], refs[np_ + 1]
        rest = refs[np_ + 2:]
        acc = h_ref[...]
        off = 0
        for p_ref, wd in zip(p_refs, widths):
            acc = acc + _dot(p_ref[...], w_ref[off:off + wd, :], NN)
            off += wd
        if gnext is None:
            rest[0][...] = acc
        else:
            g_ref, ho_ref, no_ref = rest
            ho_ref[...] = acc
            r = lax.rsqrt(jnp.mean(acc * acc, axis=-1, keepdims=True) + RMS_EPS)
            no_ref[...] = (acc * r * g_ref[...]).astype(ACT)

    in_specs = [_rowspec(tm, wd) for wd in widths] + [_fullspec(w.shape), _rowspec(tm, D_MODEL)]
    args = list(parts) + [w, h]
    out_specs = [_rowspec(tm, D_MODEL)]
    out_shape = [jax.ShapeDtypeStruct((s, D_MODEL), F32)]
    if gnext is not None:
        in_specs.append(_fullspec((1, D_MODEL)))
        args.append(gnext)
        out_specs.append(_rowspec(tm, D_MODEL))
        out_shape.append(jax.ShapeDtypeStruct((s, D_MODEL), ACT))
    out = pl.pallas_call(
        body, grid=(s // tm,), name=name, in_specs=in_specs, out_specs=out_specs,
        out_shape=out_shape, compiler_params=_cp("parallel"))(*args)
    return (out[0], None) if gnext is None else (out[0], out[1])


def _band_mask(n, max_dist):
    qi = lax.broadcasted_iota(jnp.int32, (BLOCK, 2 * BLOCK), 0)
    kj = lax.broadcasted_iota(jnp.int32, (BLOCK, 2 * BLOCK), 1)
    dist = qi + BLOCK - kj
    valid = jnp.logical_and(dist >= 0, dist <= max_dist)
    return jnp.logical_and(valid, jnp.logical_or(kj >= BLOCK, n > 0))


def _band_mask_t(n, max_dist):
    kj = lax.broadcasted_iota(jnp.int32, (2 * BLOCK, BLOCK), 0)
    qi = lax.broadcasted_iota(jnp.int32, (2 * BLOCK, BLOCK), 1)
    dist = qi + BLOCK - kj
    valid = jnp.logical_and(dist >= 0, dist <= max_dist)
    return jnp.logical_and(valid, jnp.logical_or(kj >= BLOCK, n > 0))


def _fold(a, d):
    return a.reshape(a.shape[0] // d, d * a.shape[1])


def attn_fwd(qk, v, *, d, hkv, max_dist, sink, out_dtype, name, folded=None):
    s = qk.shape[0]
    kvw = hkv * HEAD_DIM
    wqk = ATT_W + kvw
    assert qk.shape[1] == wqk and (d == 1 or (wqk % ATT_W == 0 and wqk % kvw == 0))
    assert sink is None or max_dist == BLOCK - 1
    nb = s // d // BLOCK
    grp = N_HEADS // hkv
    qpb, kpb, koff = wqk // ATT_W, wqk // kvw, ATT_W // kvw

    def body(*refs):
        if sink is None:
            q_ref, kc_ref, kp_ref, vc_ref, vp_ref, o_ref, l_ref, o_buf = refs
        else:
            q_ref, kc_ref, kp_ref, vc_ref, vp_ref, s_ref, o_ref, l_ref, o_buf = refs
        n = pl.program_id(1)
        valid = _band_mask_t(n, max_dist)
        qsl = [slice(h * HEAD_DIM, (h + 1) * HEAD_DIM) for h in range(N_HEADS)]
        ksl = [slice((h // grp) * HEAD_DIM, (h // grp + 1) * HEAD_DIM) for h in range(N_HEADS)]
        kk = jnp.concatenate([kp_ref[...], kc_ref[...]], axis=0)
        vv = jnp.concatenate([vp_ref[...], vc_ref[...]], axis=0)
        scores = []
        for h in range(N_HEADS):
            q = q_ref[:, qsl[h]] * 0.125
            scores.append(_dot(kk[:, ksl[h]], q, NT))
        probs = []
        head_row = lax.broadcasted_iota(jnp.int32, (SUBLANES, BLOCK), 0)
        lse8 = jnp.zeros((SUBLANES, BLOCK), F32)
        if sink is not None:
            sink_row = lax.broadcasted_iota(jnp.int32, (2 * BLOCK, BLOCK), 0) == 0
        for h in range(N_HEADS):
            sc = jnp.where(valid, scores[h], NEG)
            if sink is not None:
                sc = jnp.where(sink_row, s_ref[h:h + 1, 0:1], sc)
            m = jnp.max(sc, axis=0, keepdims=True)
            p = jnp.exp(sc - m)
            l = jnp.sum(p, axis=0, keepdims=True)
            if sink is not None:
                p = jnp.where(sink_row, 0.0, p)
            lse8 = jnp.where(head_row == h, m + jnp.log(l), lse8)
            probs.append((p * (1.0 / l)).astype(ACT))
        l_ref[...] = jnp.concatenate([lse8, jnp.zeros((BLOCK - SUBLANES, BLOCK), F32)], axis=0).T
        for h in range(N_HEADS):
            o_buf[:, qsl[h]] = _dot(probs[h], vv[:, ksl[h]], TN)
        o_ref[...] = o_buf[...].astype(o_ref.dtype)

    prev = lambda n: jnp.maximum(n - 1, 0)
    in_specs = [
        pl.BlockSpec((BLOCK, ATT_W), lambda r, n: (n, r * qpb)),
        pl.BlockSpec((BLOCK, kvw), lambda r, n: (n, r * kpb + koff)),
        pl.BlockSpec((BLOCK, kvw), lambda r, n: (prev(n), r * kpb + koff)),
        pl.BlockSpec((BLOCK, kvw), lambda r, n: (n, r)),
        pl.BlockSpec((BLOCK, kvw), lambda r, n: (prev(n), r)),
    ]
    qkf, vf = (_fold(qk, d), _fold(v, d)) if folded is None else folded
    args = [qkf, qkf, qkf, vf, vf]
    if sink is not None:
        in_specs.append(_fullspec((N_HEADS, LANES)))
        args.append(sink)
    ospec = pl.BlockSpec((BLOCK, ATT_W), lambda r, n: (n, r))
    lspec = pl.BlockSpec((BLOCK, LANES), lambda r, n: (n, r))
    o, lse = pl.pallas_call(
        body, grid=(d, nb), name=name, in_specs=in_specs, out_specs=[ospec, lspec],
        out_shape=[jax.ShapeDtypeStruct((s // d, d * ATT_W), out_dtype),
                   jax.ShapeDtypeStruct((s // d, d * LANES), F32)],
        scratch_shapes=[pltpu.VMEM((BLOCK, ATT_W), F32)],
        compiler_params=_cp("parallel", "parallel"))(*args)
    return o.reshape(s, ATT_W), lse.reshape(s, LANES)


def attn_bwd(qk, v, do_src, do_col, o, lse, *, d, hkv, max_dist, sink, out_dtype, name, folded=None,
             keep_folded=False):
    s = qk.shape[0]
    kvw = hkv * HEAD_DIM
    wqk = ATT_W + kvw
    nb = s // d // BLOCK
    grp = N_HEADS // hkv
    qpb, kpb, koff = wqk // ATT_W, wqk // kvw, ATT_W // kvw
    dob = do_src.shape[1] // ATT_W
    has_sink = sink is not None

    def body(*refs):
        refs = list(refs)
        q_ref, kc_ref, kp_ref, vc_ref, vp_ref, do_ref, o_ref, l_ref = refs[:8]
        pos = 8
        if has_sink:
            s_ref = refs[pos]
            pos += 1
        dq_ref, dk_ref, dv_ref = refs[pos:pos + 3]
        pos += 3
        if has_sink:
            ds_ref = refs[pos]
            pos += 1
        ck_ref, cv_ref, dq_buf, dk_buf, dv_buf = refs[pos:pos + 5]
        r_id = pl.program_id(0)
        n = pl.program_id(1)

        @pl.when(n == 0)
        def _():
            ck_ref[...] = jnp.zeros_like(ck_ref)
            cv_ref[...] = jnp.zeros_like(cv_ref)

        if has_sink:
            @pl.when(jnp.logical_and(n == 0, r_id == 0))
            def _():
                ds_ref[...] = jnp.zeros_like(ds_ref)

        @pl.when(n < nb)
        def _():
            valid = _band_mask_t(n, max_dist)
            qsl = [slice(h * HEAD_DIM, (h + 1) * HEAD_DIM) for h in range(N_HEADS)]
            ksl = [slice((h // grp) * HEAD_DIM, (h // grp + 1) * HEAD_DIM) for h in range(N_HEADS)]
            kk = jnp.concatenate([kp_ref[...], kc_ref[...]], axis=0)
            vv = jnp.concatenate([vp_ref[...], vc_ref[...]], axis=0)
            qs, first = [], []
            for h in range(N_HEADS):
                q = q_ref[:, qsl[h]] * 0.125
                qs.append(q)
                first.append((_dot(kk[:, ksl[h]], q, NT), _dot(vv[:, ksl[h]], do_ref[:, qsl[h]], NT)))
            lse_t = l_ref[...].T
            prod = do_ref[...].astype(F32) * o_ref[...].astype(F32)
            hi = prod.astype(ACT)
            lo = (prod - hi.astype(F32)).astype(ACT)
            col = lax.broadcasted_iota(jnp.int32, (LANES, ATT_W), 1)
            row = lax.broadcasted_iota(jnp.int32, (LANES, ATT_W), 0)
            head_of = jnp.logical_and(col >= row * HEAD_DIM, col < (row + 1) * HEAD_DIM).astype(ACT)
            e_t = _dot(head_of, hi, NT) + _dot(head_of, lo, NT)
            mid = []
            for h in range(N_HEADS):
                s_t, dp_t = first[h]
                lse_h, e_h = lse_t[h:h + 1, :], e_t[h:h + 1, :]
                p_t = jnp.exp(jnp.where(valid, s_t, NEG) - lse_h)
                mid.append(((p_t * (dp_t - e_h)).astype(ACT), p_t.astype(ACT)))
                if has_sink:
                    sk = s_ref[h:h + 1, 0:1]
                    dsk = -jnp.sum(jnp.exp(sk - lse_h) * e_h, axis=1, keepdims=True)
                    ds_ref[h:h + 1, :] = ds_ref[h:h + 1, :] + dsk
            dkk = [None] * hkv
            dvv = [None] * hkv
            for h in range(N_HEADS):
                kh = h // grp
                ds_t, p_t = mid[h]
                dq_buf[:, qsl[h]] = _dot(ds_t, kk[:, ksl[h]], TN) * 0.125
                for lst, val in ((dkk, _dot(ds_t, qs[h], NN)), (dvv, _dot(p_t, do_ref[:, qsl[h]], NN))):
                    lst[kh] = val if lst[kh] is None else lst[kh] + val
            for kh in range(hkv):
                ks = slice(kh * HEAD_DIM, (kh + 1) * HEAD_DIM)
                dk_buf[:, ks] = ck_ref[:, ks] + dkk[kh][:BLOCK]
                dv_buf[:, ks] = cv_ref[:, ks] + dvv[kh][:BLOCK]
                ck_ref[:, ks] = dkk[kh][BLOCK:]
                cv_ref[:, ks] = dvv[kh][BLOCK:]
            dq_ref[...] = dq_buf[...].astype(dq_ref.dtype)
            dk_ref[...] = dk_buf[...].astype(dk_ref.dtype)
            dv_ref[...] = dv_buf[...].astype(dv_ref.dtype)

        @pl.when(n == nb)
        def _():
            dk_ref[...] = ck_ref[...].astype(dk_ref.dtype)
            dv_ref[...] = cv_ref[...].astype(dv_ref.dtype)

    qrow = lambda n: jnp.minimum(n, nb - 1)
    prow = lambda n: jnp.maximum(jnp.minimum(n, nb - 1) - 1, 0)
    krow = lambda n: jnp.maximum(n - 1, 0)
    in_specs = [
        pl.BlockSpec((BLOCK, ATT_W), lambda r, n: (qrow(n), r * qpb)),
        pl.BlockSpec((BLOCK, kvw), lambda r, n: (qrow(n), r * kpb + koff)),
        pl.BlockSpec((BLOCK, kvw), lambda r, n: (prow(n), r * kpb + koff)),
        pl.BlockSpec((BLOCK, kvw), lambda r, n: (qrow(n), r)),
        pl.BlockSpec((BLOCK, kvw), lambda r, n: (prow(n), r)),
        pl.BlockSpec((BLOCK, ATT_W), lambda r, n: (qrow(n), r * dob + do_col)),
        pl.BlockSpec((BLOCK, ATT_W), lambda r, n: (qrow(n), r)),
        pl.BlockSpec((BLOCK, LANES), lambda r, n: (qrow(n), r)),
    ]
    qkf, vf = (_fold(qk, d), _fold(v, d)) if folded is None else folded
    args = [qkf, qkf, qkf, vf, vf, _fold(do_src, d), _fold(o, d), _fold(lse, d)]
    if has_sink:
        in_specs.append(_fullspec((N_HEADS, LANES)))
        args.append(sink)
    qspec = pl.BlockSpec((BLOCK, ATT_W), lambda r, n: (qrow(n), r))
    kspec = pl.BlockSpec((BLOCK, kvw), lambda r, n: (krow(n), r))
    out_specs = [qspec, kspec, kspec]
    out_shape = [jax.ShapeDtypeStruct((s // d, d * ATT_W), out_dtype),
                 jax.ShapeDtypeStruct((s // d, d * kvw), out_dtype),
                 jax.ShapeDtypeStruct((s // d, d * kvw), out_dtype)]
    if has_sink:
        out_specs.append(_fullspec((N_HEADS, LANES)))
        out_shape.append(jax.ShapeDtypeStruct((N_HEADS, LANES), F32))
    out = pl.pallas_call(
        body, grid=(d, nb + 1), name=name, in_specs=in_specs, out_specs=out_specs,
        out_shape=out_shape,
        scratch_shapes=[pltpu.VMEM((BLOCK, kvw), F32), pltpu.VMEM((BLOCK, kvw), F32),
                        pltpu.VMEM((BLOCK, ATT_W), F32), pltpu.VMEM((BLOCK, kvw), F32), pltpu.VMEM((BLOCK, kvw), F32)],
        compiler_params=_cp("arbitrary", "arbitrary"))(*args)
    res = list(out[:3]) if keep_folded else [out[0].reshape(s, ATT_W), out[1].reshape(s, kvw), out[2].reshape(s, kvw)]
    if has_sink:
        res.append(out[3])
    return res


def combine_fwd(os_, lses, name):
    s = os_[0].shape[0]
    tm = _tile(s, (512,))

    def body(o1, o2, o3, l1, l2, l3, c_ref, l_ref, c_buf):
        a, b, c = l1[...], l2[...], l3[...]
        m = jnp.maximum(jnp.maximum(a, b), c)
        wa, wb, wc = jnp.exp(a - m), jnp.exp(b - m), jnp.exp(c - m)
        tot = wa + wb + wc
        l_ref[...] = m + jnp.log(tot)
        rt = 1.0 / tot
        wa, wb, wc = wa * rt, wb * rt, wc * rt
        for h in range(N_HEADS):
            cs = slice(h * HEAD_DIM, (h + 1) * HEAD_DIM)
            c_buf[:, cs] = (wa[:, h:h + 1] * o1[:, cs] + wb[:, h:h + 1] * o2[:, cs] + wc[:, h:h + 1] * o3[:, cs])
        c_ref[...] = c_buf[...].astype(ACT)

    spec = _rowspec(tm, ATT_W)
    lspec = _rowspec(tm, LANES)
    return pl.pallas_call(
        body, grid=(s // tm,), name=name, in_specs=[spec] * 3 + [lspec] * 3, out_specs=[spec, lspec],
        out_shape=[jax.ShapeDtypeStruct((s, ATT_W), ACT), jax.ShapeDtypeStruct((s, LANES), F32)],
        scratch_shapes=[pltpu.VMEM((tm, ATT_W), F32)],
        compiler_params=_cp("parallel"))(*os_, *lses)


def assemble(parts, tabs, name):
    terms_of = [[t if isinstance(t, tuple) else (t, 1) for t in terms] for terms, _ in parts]
    s = terms_of[0][0][0].shape[0] * terms_of[0][0][1]
    tm = _tile(s, (256,))
    widths = [ts[0][0].shape[1] // ts[0][1] for ts in terms_of]
    flags = [f for _, f in parts]
    flat = [t for ts in terms_of for t in ts]

    def body(*refs):
        c_ref, sa_ref, sb_ref, o_ref = refs[len(flat):]
        c, sa, sb = c_ref[...], sa_ref[...], sb_ref[...]
        off = 0
        first = 0
        for wd, ts, fl in zip(widths, terms_of, flags):
            t_refs = refs[first:first + len(ts)]
            first += len(ts)
            x = None
            for t_ref, (_, d) in zip(t_refs, ts):
                t = t_ref[...].astype(F32) if d == 1 else _unfold_rows(t_ref[...], d)
                x = t if x is None else x + t
            for g in range(wd // LANES):
                cols = slice(g * LANES, (g + 1) * LANES)
                y = _rope_t(x[:, cols], c, sa, sb) if fl else x[:, cols]
                o_ref[:, off + g * LANES:off + (g + 1) * LANES] = y.astype(ACT)
            off += wd

    tot = sum(widths)
    return pl.pallas_call(
        body, grid=(s // tm,), name=name,
        in_specs=[_rowspec(tm // d, a.shape[1]) for a, d in flat] + [_rowspec(tm, LANES)] * 3,
        out_specs=_rowspec(tm, tot), out_shape=jax.ShapeDtypeStruct((s, tot), ACT),
        compiler_params=_cp("parallel"))(*[a for a, _ in flat], *tabs)


def _ln_stats(x):
    mu = jnp.mean(x, axis=-1, keepdims=True)
    xc = x - mu
    var = jnp.mean(xc * xc, axis=-1, keepdims=True)
    rstd = lax.rsqrt(var + LN_EPS)
    return xc * rstd, rstd


SUBLANES = 8


TAP_ROWS = 32


def _tap_sum(buf, cw_ref, offsets, tm, res_ref):
    for r0 in range(0, tm, TAP_ROWS):
        acc = None
        for ph in range(SUBLANES):
            taps = [j for j, off in enumerate(offsets) if off % SUBLANES == ph]
            if not taps:
                continue
            rows = TAP_ROWS if ph == 0 else TAP_ROWS + SUBLANES
            part = None
            for j in taps:
                term = cw_ref[j:j + 1, :] * buf[pl.ds(offsets[j] - ph + r0, rows), :]
                part = term if part is None else part + term
            part = part[ph:ph + TAP_ROWS]
            acc = part if acc is None else acc + part
        res_ref[pl.ds(r0, TAP_ROWS), :] = acc


def conv_fwd(ga, gb, cw, cb, lg, lb, name):
    s = ga.shape[0]
    tm = _tile(s, (256,))
    hb = tm // CONV_HALO

    def body(ga_ref, gb_ref, gah_ref, gbh_ref, cw_ref, cb_ref, lg_ref, lb_ref, c_ref, c1_ref, buf):
        i = pl.program_id(0)
        halo = gah_ref[...].astype(F32) * _sigmoid(gbh_ref[...].astype(F32))
        buf[0:CONV_HALO, :] = jnp.where(i > 0, halo, 0.0)
        buf[CONV_HALO:, :] = ga_ref[...].astype(F32) * _sigmoid(gb_ref[...].astype(F32))
        first = CONV_HALO - (CONV_WIDTH - 1)
        _tap_sum(buf, cw_ref, [first + j for j in range(CONV_WIDTH)], tm, c1_ref)
        acc = c1_ref[...] + cb_ref[...]
        c1_ref[...] = acc
        xh, _ = _ln_stats(acc)
        y = xh * lg_ref[...] + lb_ref[...]
        c_ref[...] = (y * _sigmoid(y)).astype(ACT)

    hspec = pl.BlockSpec((CONV_HALO, CONV_CH), lambda i: (jnp.maximum(i * hb - 1, 0), 0))
    vec = _fullspec((1, CONV_CH))
    spec = _rowspec(tm, CONV_CH)
    return pl.pallas_call(
        body, grid=(s // tm,), name=name,
        in_specs=[spec, spec, hspec, hspec, _fullspec((CONV_WIDTH, CONV_CH)), vec, vec, vec],
        out_specs=[spec, spec],
        out_shape=[jax.ShapeDtypeStruct((s, CONV_CH), ACT), jax.ShapeDtypeStruct((s, CONV_CH), F32)],
        scratch_shapes=[pltpu.VMEM((tm + CONV_HALO, CONV_CH), F32)],
        compiler_params=_cp("parallel"))(ga, gb, ga, gb, cw, cb, lg, lb)


def conv_bwd_ln(c1, dsrc, dcol, lg, lb, name):
    s = c1.shape[0]
    tm = _tile(s, (256,))

    def body(c1_ref, d_ref, lg_ref, lb_ref, o_ref, dg_ref, db_ref):
        @pl.when(pl.program_id(0) == 0)
        def _():
            dg_ref[...] = jnp.zeros_like(dg_ref)
            db_ref[...] = jnp.zeros_like(db_ref)

        xh, rstd = _ln_stats(c1_ref[...].astype(F32))
        y = xh * lg_ref[...] + lb_ref[...]
        sg = _sigmoid(y)
        dy = d_ref[...].astype(F32) * (sg * (1.0 + y * (1.0 - sg)))
        dg_ref[0:1, :] = dg_ref[0:1, :] + jnp.sum(dy * xh, axis=0, keepdims=True)
        db_ref[0:1, :] = db_ref[0:1, :] + jnp.sum(dy, axis=0, keepdims=True)
        dxh = dy * lg_ref[...]
        o_ref[...] = rstd * (dxh - jnp.mean(dxh, axis=-1, keepdims=True)
                             - xh * jnp.mean(dxh * xh, axis=-1, keepdims=True))

    vec = _fullspec((1, CONV_CH))
    acc = _fullspec((8, CONV_CH))
    return pl.pallas_call(
        body, grid=(s // tm,), name=name,
        in_specs=[_rowspec(tm, CONV_CH), _rowspec(tm, CONV_CH, dcol), vec, vec],
        out_specs=[_rowspec(tm, CONV_CH), acc, acc],
        out_shape=[jax.ShapeDtypeStruct((s, CONV_CH), F32)] + [jax.ShapeDtypeStruct((8, CONV_CH), F32)] * 2,
        compiler_params=_cp("arbitrary"))(c1, dsrc, lg, lb)


def conv_bwd_conv(dc1, ga, gb, cw, name):
    s = ga.shape[0]
    tm = _tile(s, (256,))
    hb = tm // CONV_HALO
    nt = s // tm
    last_h = s // CONV_HALO - 1
    first = CONV_HALO - (CONV_WIDTH - 1)


    def rows8(x):
        return jnp.sum(x.reshape(x.shape[0] // SUBLANES, SUBLANES, CONV_CH), axis=0)

    def body(d_ref, dn_ref, ga_ref, gb_ref, gah_ref, gbh_ref, cw_ref,
             dga_ref, dgb_ref, dw_ref, db_ref, dbuf, cbuf, sbuf):
        i = pl.program_id(0)

        @pl.when(i == 0)
        def _():
            dw_ref[...] = jnp.zeros_like(dw_ref)
            db_ref[...] = jnp.zeros_like(db_ref)

        d = d_ref[...]
        dbuf[0:tm, :] = d
        dbuf[tm:, :] = jnp.where(i < nt - 1, dn_ref[...], 0.0)
        halo = gah_ref[...].astype(F32) * _sigmoid(gbh_ref[...].astype(F32))
        cbuf[0:CONV_HALO, :] = jnp.where(i > 0, halo, 0.0)
        a = ga_ref[...].astype(F32)
        sg = _sigmoid(gb_ref[...].astype(F32))
        cbuf[CONV_HALO:, :] = a * sg
        for ph in range(SUBLANES):
            taps = [j for j in range(CONV_WIDTH) if (first + j) % SUBLANES == ph]
            if ph:
                sbuf[0:tm + CONV_HALO - SUBLANES, :] = cbuf[pl.ds(ph, tm + CONV_HALO - SUBLANES), :]
            src = sbuf if ph else cbuf
            for r0 in range(0, tm, TAP_ROWS):
                d_blk = dbuf[pl.ds(r0, TAP_ROWS), :]
                for j in taps:
                    tap = src[pl.ds(first + j - ph + r0, TAP_ROWS), :]
                    rows = slice(j * SUBLANES, (j + 1) * SUBLANES)
                    dw_ref[rows, :] = dw_ref[rows, :] + rows8(d_blk * tap)
        db_ref[...] = db_ref[...] + rows8(d)
        _tap_sum(dbuf, cw_ref, [CONV_WIDTH - 1 - j for j in range(CONV_WIDTH)], tm, sbuf)
        dc0 = sbuf[0:tm, :]
        dga_ref[...] = (dc0 * sg).astype(ACT)
        dgb_ref[...] = (dc0 * a * sg * (1.0 - sg)).astype(ACT)

    spec = _rowspec(tm, CONV_CH)
    hprev = pl.BlockSpec((CONV_HALO, CONV_CH), lambda i: (jnp.maximum(i * hb - 1, 0), 0))
    hnext = pl.BlockSpec((CONV_HALO, CONV_CH), lambda i: (jnp.minimum((i + 1) * hb, last_h), 0))
    return pl.pallas_call(
        body, grid=(nt,), name=name,
        in_specs=[spec, hnext, spec, spec, hprev, hprev, _fullspec((CONV_WIDTH, CONV_CH))],
        out_specs=[spec, spec, _fullspec((CONV_HALO * SUBLANES, CONV_CH)), _fullspec((SUBLANES, CONV_CH))],
        out_shape=[jax.ShapeDtypeStruct((s, CONV_CH), ACT)] * 2
        + [jax.ShapeDtypeStruct((CONV_HALO * SUBLANES, CONV_CH), F32), jax.ShapeDtypeStruct((SUBLANES, CONV_CH), F32)],
        scratch_shapes=[pltpu.VMEM((tm + CONV_HALO, CONV_CH), F32)] * 3,
        compiler_params=_cp("arbitrary"))(dc1, dc1, ga, gb, ga, gb, cw)


_GELU_K = math.sqrt(2.0 / math.pi)
_GELU_C = 0.044715


def _gelu(x):
    return 0.5 * x * (1.0 + jnp.tanh(_GELU_K * (x + _GELU_C * x * x * x)))


def _gelu_grad(x):
    t = jnp.tanh(_GELU_K * (x + _GELU_C * x * x * x))
    return 0.5 * (1.0 + t) + 0.5 * x * (1.0 - t * t) * _GELU_K * (1.0 + 3.0 * _GELU_C * x * x)


def _tril():
    qi = lax.broadcasted_iota(jnp.int32, (BLOCK, BLOCK), 0)
    kj = lax.broadcasted_iota(jnp.int32, (BLOCK, BLOCK), 1)
    return kj <= qi


GMLP_CHUNKS = 2


def _gmlp_weights(sw_ref, w_buf, wt_buf):
    tril = _tril()
    for g in range(D_GROUPS):
        w = jnp.where(tril, sw_ref[g], 0.0)
        w_buf[g] = w.astype(ACT)
        if wt_buf is not None:
            wt_buf[g] = w.T.astype(ACT)


def _gmlp_mix(w_buf, gn, sb_ref, m_buf):
    for c in range(GMLP_CHUNKS):
        rows = slice(c * BLOCK, (c + 1) * BLOCK)
        for g in range(D_GROUPS):
            cs = slice(g * HEAD_DIM, (g + 1) * HEAD_DIM)
            m_buf[rows, cs] = _dot(w_buf[g], gn[rows, cs], NN) + sb_ref[:, cs]


def gmlp_fwd(z, lg, lb, sw, sbx, name):
    s = z.shape[0]
    tm = GMLP_CHUNKS * BLOCK

    def body(z_ref, lg_ref, lb_ref, sw_ref, sb_ref, o_ref, w_buf, m_buf):
        @pl.when(pl.program_id(0) == 0)
        def _():
            _gmlp_weights(sw_ref, w_buf, None)

        zz = _gelu(z_ref[...].astype(F32))
        xh, _ = _ln_stats(zz[:, D_CH:])
        gn = (xh * lg_ref[...] + lb_ref[...]).astype(ACT)
        _gmlp_mix(w_buf, gn, sb_ref, m_buf)
        o_ref[...] = (zz[:, :D_CH] * m_buf[...]).astype(ACT)

    return pl.pallas_call(
        body, grid=(s // tm,), name=name,
        in_specs=[_rowspec(tm, 2 * D_CH), _fullspec((1, D_CH)), _fullspec((1, D_CH)),
                  _fullspec((D_GROUPS, BLOCK, BLOCK)), _fullspec((BLOCK, D_CH))],
        out_specs=_rowspec(tm, D_CH), out_shape=jax.ShapeDtypeStruct((s, D_CH), ACT),
        scratch_shapes=[pltpu.VMEM((D_GROUPS, BLOCK, BLOCK), ACT), pltpu.VMEM((tm, D_CH), F32)],
        compiler_params=_cp("arbitrary"))(z, lg, lb, sw, sbx)


def gmlp_bwd(z, dsrc, dcol, lg, lb, sw, sbx, name):
    s = z.shape[0]
    tm = GMLP_CHUNKS * BLOCK

    def body(z_ref, d_ref, lg_ref, lb_ref, sw_ref, sb_ref, dz_ref, dw_ref, dsb_ref, dg_ref, db_ref,
             w_buf, wt_buf, m_buf, dgn_buf):
        @pl.when(pl.program_id(0) == 0)
        def _():
            _gmlp_weights(sw_ref, w_buf, wt_buf)
            dw_ref[...] = jnp.zeros_like(dw_ref)
            dsb_ref[...] = jnp.zeros_like(dsb_ref)
            dg_ref[...] = jnp.zeros_like(dg_ref)
            db_ref[...] = jnp.zeros_like(db_ref)

        zr = z_ref[...].astype(F32)
        zz = _gelu(zr)
        u = zz[:, :D_CH]
        xh, rstd = _ln_stats(zz[:, D_CH:])
        gn = (xh * lg_ref[...] + lb_ref[...]).astype(ACT)
        dd = d_ref[...].astype(F32)
        _gmlp_mix(w_buf, gn, sb_ref, m_buf)
        dz_ref[:, :D_CH] = (dd * m_buf[...] * _gelu_grad(zr[:, :D_CH])).astype(ACT)
        dmix = dd * u
        dmix_a = dmix.astype(ACT)
        dsb = dmix[0:BLOCK]
        for c in range(1, GMLP_CHUNKS):
            dsb = dsb + dmix[c * BLOCK:(c + 1) * BLOCK]
        dsb_ref[...] = dsb_ref[...] + dsb
        tril = _tril()
        for g in range(D_GROUPS):
            cs = slice(g * HEAD_DIM, (g + 1) * HEAD_DIM)
            dw = None
            for c in range(GMLP_CHUNKS):
                rows = slice(c * BLOCK, (c + 1) * BLOCK)
                t = _dot(dmix_a[rows, cs], gn[rows, cs], NT)
                dw = t if dw is None else dw + t
                dgn_buf[rows, cs] = _dot(wt_buf[g], dmix_a[rows, cs], NN)
            dw_ref[g] = dw_ref[g] + jnp.where(tril, dw, 0.0)
        dgn = dgn_buf[...]
        dg_ref[0:1, :] = dg_ref[0:1, :] + jnp.sum(dgn * xh, axis=0, keepdims=True)
        db_ref[0:1, :] = db_ref[0:1, :] + jnp.sum(dgn, axis=0, keepdims=True)
        dxh = dgn * lg_ref[...]
        dgate = rstd * (dxh - jnp.mean(dxh, axis=-1, keepdims=True)
                        - xh * jnp.mean(dxh * xh, axis=-1, keepdims=True))
        dz_ref[:, D_CH:] = (dgate * _gelu_grad(zr[:, D_CH:])).astype(ACT)

    vec = _fullspec((1, D_CH))
    acc = _fullspec((8, D_CH))
    wshape = (D_GROUPS, BLOCK, BLOCK)
    return pl.pallas_call(
        body, grid=(s // tm,), name=name,
        in_specs=[_rowspec(tm, 2 * D_CH), _rowspec(tm, D_CH, dcol), vec, vec, _fullspec(wshape),
                  _fullspec((BLOCK, D_CH))],
        out_specs=[_rowspec(tm, 2 * D_CH), _fullspec(wshape), _fullspec((BLOCK, D_CH)), acc, acc],
        out_shape=[jax.ShapeDtypeStruct((s, 2 * D_CH), ACT), jax.ShapeDtypeStruct(wshape, F32),
                   jax.ShapeDtypeStruct((BLOCK, D_CH), F32),
                   jax.ShapeDtypeStruct((8, D_CH), F32), jax.ShapeDtypeStruct((8, D_CH), F32)],
        scratch_shapes=[pltpu.VMEM(wshape, ACT), pltpu.VMEM(wshape, ACT),
                        pltpu.VMEM((tm, D_CH), F32), pltpu.VMEM((tm, D_CH), F32)],
        compiler_params=_cp("arbitrary"))(z, dsrc, lg, lb, sw, sbx)


def _rms_bwd(dn, x, g):
    r = lax.rsqrt(jnp.mean(x * x, axis=-1, keepdims=True) + RMS_EPS)
    u = dn * g
    dx = r * u - x * (r * r * r) * jnp.mean(x * u, axis=-1, keepdims=True)
    return dx, dn * x * r


def final_loss(h, g, target, name):
    s = h.shape[0]
    tm = _tile(s, (256,))

    def body(h_ref, g_ref, t_ref, loss_ref, dh_ref, dg_ref):
        @pl.when(pl.program_id(0) == 0)
        def _():
            loss_ref[...] = jnp.zeros_like(loss_ref)
            dg_ref[...] = jnp.zeros_like(dg_ref)

        x = h_ref[...]
        r = lax.rsqrt(jnp.mean(x * x, axis=-1, keepdims=True) + RMS_EPS)
        diff = x * r * g_ref[...] - t_ref[...]
        part = jnp.sum(jnp.sum(diff * diff, axis=-1, keepdims=True), axis=0, keepdims=True)
        loss_ref[...] = loss_ref[...] + part * (0.5 / D_MODEL)
        dx, dgt = _rms_bwd(diff * (1.0 / D_MODEL), x, g_ref[...])
        dh_ref[...] = dx
        dg_ref[0:1, :] = dg_ref[0:1, :] + jnp.sum(dgt, axis=0, keepdims=True)

    spec = _rowspec(tm, D_MODEL)
    return pl.pallas_call(
        body, grid=(s // tm,), name=name,
        in_specs=[spec, _fullspec((1, D_MODEL)), spec],
        out_specs=[_fullspec((8, LANES)), spec, _fullspec((8, D_MODEL))],
        out_shape=[jax.ShapeDtypeStruct((8, LANES), F32), jax.ShapeDtypeStruct((s, D_MODEL), F32),
                   jax.ShapeDtypeStruct((8, D_MODEL), F32)],
        compiler_params=_cp("arbitrary"))(h, g, target)


def mm_nt(dy, w, name):
    s, n = dy.shape
    k = w.shape[0]
    tm = _tile(s, (512,))
    tk = _tile(k, (512,))

    def body(d_ref, w_ref, o_ref):
        o_ref[...] = _dot(d_ref[...].astype(ACT), w_ref[...], NT).astype(ACT)

    return pl.pallas_call(
        body, grid=(k // tk, s // tm), name=name,
        in_specs=[pl.BlockSpec((tm, n), lambda j, i: (i, 0)), pl.BlockSpec((tk, n), lambda j, i: (j, 0))],
        out_specs=pl.BlockSpec((tm, tk), lambda j, i: (i, j)),
        out_shape=jax.ShapeDtypeStruct((s, k), ACT),
        compiler_params=_cp("parallel", "parallel"))(dy, w)


def ffn_down_bwd(dh, wd, dact_dgate, dact_dup, name):
    s = dh.shape[0]
    f = wd.shape[0]
    tm = _tile(s, (512,))
    tf = _tile(f, (1408, 512, 256, 128))

    def body(d_ref, w_ref, g_ref, u_ref, dg_ref, du_ref):
        dact = _dot(d_ref[...].astype(ACT), w_ref[...], NT)
        dg_ref[...] = (dact * g_ref[...].astype(F32)).astype(ACT)
        du_ref[...] = (dact * u_ref[...].astype(F32)).astype(ACT)

    tile = pl.BlockSpec((tm, tf), lambda j, i: (i, j))
    return pl.pallas_call(
        body, grid=(f // tf, s // tm), name=name,
        in_specs=[pl.BlockSpec((tm, D_MODEL), lambda j, i: (i, 0)),
                  pl.BlockSpec((tf, D_MODEL), lambda j, i: (j, 0)), tile, tile],
        out_specs=[tile, tile], out_shape=[jax.ShapeDtypeStruct((s, f), ACT)] * 2,
        compiler_params=_cp("parallel", "parallel"))(dh, wd, dact_dgate, dact_dup)


def mm_nt_rms(parts, h, g, dh, name):
    s = h.shape[0]
    tm = _tile(s, (256,))
    np_ = len(parts)

    def body(*refs):
        d_refs = refs[:np_]
        w_refs = refs[np_:2 * np_]
        h_ref, g_ref, dh_ref, o_ref, dg_ref = refs[2 * np_:]

        @pl.when(pl.program_id(0) == 0)
        def _():
            dg_ref[...] = jnp.zeros_like(dg_ref)

        dn = None
        for d_ref, w_ref in zip(d_refs, w_refs):
            t = _dot(d_ref[...], w_ref[...], NT)
            dn = t if dn is None else dn + t
        dx, dgt = _rms_bwd(dn, h_ref[...], g_ref[...])
        o_ref[...] = dh_ref[...] + dx
        dg_ref[0:1, :] = dg_ref[0:1, :] + jnp.sum(dgt, axis=0, keepdims=True)

    spec = _rowspec(tm, D_MODEL)
    return pl.pallas_call(
        body, grid=(s // tm,), name=name,
        in_specs=[_rowspec(tm, d.shape[1]) for d, _ in parts] + [_fullspec(w.shape) for _, w in parts]
        + [spec, _fullspec((1, D_MODEL)), spec],
        out_specs=[spec, _fullspec((8, D_MODEL))],
        out_shape=[jax.ShapeDtypeStruct((s, D_MODEL), F32), jax.ShapeDtypeStruct((8, D_MODEL), F32)],
        compiler_params=_cp("arbitrary"))(*[d for d, _ in parts], *[w for _, w in parts], h, g, dh)


def mm_tn(a, b, name):
    s, k = a.shape
    n = b.shape[1]
    tk = _tile(k, (512, 1408, 256, 128))
    tn = _tile(n, (1408, 1280, 1024, 896, 512, 256, 128))
    ts = _tile(s, (2048, 512) if b.dtype == ACT else (1024, 512))

    def body(a_ref, b_ref, o_ref):
        t = _dot(a_ref[...].astype(ACT), b_ref[...].astype(ACT), TN)

        @pl.when(pl.program_id(2) == 0)
        def _():
            o_ref[...] = t

        @pl.when(pl.program_id(2) > 0)
        def _():
            o_ref[...] = o_ref[...] + t

    return pl.pallas_call(
        body, grid=(k // tk, n // tn, s // ts), name=name,
        in_specs=[pl.BlockSpec((ts, tk), lambda i, j, t: (t, i)), pl.BlockSpec((ts, tn), lambda i, j, t: (t, j))],
        out_specs=pl.BlockSpec((tk, tn), lambda i, j, t: (i, j)),
        out_shape=jax.ShapeDtypeStruct((k, n), F32),
        compiler_params=_cp("parallel", "parallel", "arbitrary"))(a, b)


def _adamw_math(w, g, m, v):
    m = ADAM_B1 * m + (1.0 - ADAM_B1) * g
    v = ADAM_B2 * v + (1.0 - ADAM_B2) * (g * g)
    m_hat = m / (1.0 - ADAM_B1 ** ADAM_STEP)
    v_hat = v / (1.0 - ADAM_B2 ** ADAM_STEP)
    delta = -ADAM_LR * (m_hat / (jnp.sqrt(v_hat) + ADAM_EPS) + ADAM_WD * w)
    return delta, m, v


def sum_adamw(parts, w, m, v, layer, others, name):
    nl, r, c = w.shape
    tr = _tile(r, (256, 128, 64, 32, 16, 8))

    def body(p_ref, w_ref, m_ref, v_ref, *rest):
        g_ref, d_ref, mo_ref, vo_ref = rest[-4:]
        g = p_ref[0].astype(F32)
        for i in range(1, N_DEV):
            g = g + p_ref[i].astype(F32)
        d, mm, vv = _adamw_math(w_ref[...], g, m_ref[...], v_ref[...])
        g_ref[...] = g
        d_ref[...] = d
        mo_ref[...] = mm
        vo_ref[...] = vv

    spec = pl.BlockSpec((None, tr, c), lambda i: (layer, i, 0))
    in_specs = [pl.BlockSpec((N_DEV, tr, c), lambda i: (0, i, 0))] + [spec] * 3
    args = [parts, w, m, v]
    aliases = {}
    if others is not None:
        in_specs += [pl.BlockSpec(memory_space=pl.ANY)] * 4
        args += list(others)
        aliases = {4 + j: j for j in range(4)}
    return pl.pallas_call(
        body, grid=(r // tr,), name=name, in_specs=in_specs, out_specs=[spec] * 4,
        out_shape=[jax.ShapeDtypeStruct((nl, r, c), F32)] * 4, input_output_aliases=aliases,
        compiler_params=_cp("parallel"))(*args)


def cast_layers(items, name):
    n = len(items)

    def body(*refs):
        for i in range(n):
            refs[n + i][...] = refs[i][...].astype(ACT)

    return pl.pallas_call(
        body, grid=(1,), name=name,
        in_specs=[pl.BlockSpec((None,) + w.shape[1:], lambda i, l=l: (l, 0, 0)) for w, l in items],
        out_specs=[_fullspec(w.shape[1:]) for w, _ in items],
        out_shape=[jax.ShapeDtypeStruct(w.shape[1:], ACT) for w, _ in items],
        compiler_params=_cp("arbitrary"))(*[w for w, _ in items])


def adamw_small(ws, gs, ms, vs, name):
    k = len(ws)

    def body(*refs):
        for i in range(k):
            w_ref, g_ref, m_ref, v_ref = (refs[j * k + i] for j in range(4))
            d, mm, vv = _adamw_math(w_ref[...], g_ref[...], m_ref[...], v_ref[...])
            refs[4 * k + i][...] = d
            refs[5 * k + i][...] = mm
            refs[6 * k + i][...] = vv

    shapes = [jax.ShapeDtypeStruct(w.shape, F32) for w in ws]
    specs = [_fullspec(w.shape) for w in ws]
    out = pl.pallas_call(
        body, grid=(1,), name=name, in_specs=specs * 4, out_specs=specs * 3, out_shape=shapes * 3,
        compiler_params=_cp("arbitrary"))(*ws, *gs, *ms, *vs)
    return out[:k], out[k:2 * k], out[2 * k:]


def sum_slots(x, out_dtype, name):
    g, r, c = x.shape
    tr = r if x.size * x.dtype.itemsize <= SMALL_BLOCK_BYTES else _tile(r, (256, 128, 64, 32, 16, 8))

    def body(x_ref, o_ref):
        acc = x_ref[0].astype(F32)
        for i in range(1, g):
            acc = acc + x_ref[i].astype(F32)
        o_ref[...] = acc.astype(o_ref.dtype)

    return pl.pallas_call(
        body, grid=(r // tr,), name=name,
        in_specs=[pl.BlockSpec((g, tr, c), lambda i: (0, i, 0))], out_specs=_rowspec(tr, c),
        out_shape=jax.ShapeDtypeStruct((r, c), out_dtype),
        compiler_params=_cp("parallel"))(x)


HBM_SPEC = pl.BlockSpec(memory_space=pltpu.HBM)
SEM_SPEC = pl.BlockSpec(memory_space=pltpu.SEMAPHORE)
DATAFLOW = pltpu.SideEffectType.DATAFLOW_SIDE_EFFECTING


def _my_rank():
    return 4 * lax.axis_index("x") + 2 * lax.axis_index("y") + lax.axis_index("c")


def _exchange_copies(x_refs, land_refs, send, recv, a2a):
    pos = [lax.axis_index(a) for a in AXES]
    me = _my_rank()
    copies = []
    for x_ref, land_ref, s_ref, r_ref in zip(x_refs, land_refs, send, recv):
        for k in range(N_DEV - 1):
            bits = ((k + 1) >> 2 & 1, (k + 1) >> 1 & 1, (k + 1) & 1)
            peer = tuple(1 - p if b else p for p, b in zip(pos, bits))
            prank = 4 * peer[0] + 2 * peer[1] + peer[2]
            copies.append(pltpu.make_async_remote_copy(
                src_ref=x_ref.at[prank] if a2a else x_ref, dst_ref=land_ref.at[me],
                send_sem=s_ref.at[k], recv_sem=r_ref.at[k], device_id=peer, device_id_type=MESH))
    return copies


def exchange_start(xs, a2a, name, after=None):
    n = len(xs)
    me = _my_rank()
    lands = []
    for x in xs:
        own = lax.dynamic_index_in_dim(x, me, 0, keepdims=True) if a2a else x[None]
        shape = x.shape if a2a else (N_DEV,) + x.shape
        lands.append(lax.dynamic_update_slice(lax.empty(shape, x.dtype), own, (me,) + (0,) * (len(shape) - 1)))

    def body(*refs):
        x_refs, land_refs = refs[:n], refs[n:2 * n]
        outs = refs[len(refs) - 4 * n - 1:]
        for cp in _exchange_copies(x_refs, land_refs, outs[:n], outs[n:2 * n], a2a):
            cp.start()
        token = outs[4 * n]
        token[...] = jnp.zeros_like(token)

    sems = [pltpu.SemaphoreType.DMA((N_DEV - 1,))] * n
    out = pl.pallas_call(
        body, name=name,
        out_shape=tuple(sems + sems + [pltpu.HBM(x.shape, x.dtype) for x in xs]
                        + [pltpu.HBM(l.shape, l.dtype) for l in lands] + [jax.ShapeDtypeStruct((8, LANES), F32)]),
        in_specs=[HBM_SPEC] * (2 * n) + ([] if after is None else [pl.BlockSpec(memory_space=pl.ANY)]),
        out_specs=tuple([SEM_SPEC] * (2 * n) + [HBM_SPEC] * (2 * n) + [pl.BlockSpec(memory_space=pltpu.VMEM)]),
        input_output_aliases={i: 2 * n + i for i in range(2 * n)},
        compiler_params=pltpu.CompilerParams(has_side_effects=DATAFLOW),
    )(*[pltpu.with_memory_space_constraint(a, pltpu.HBM) for a in list(xs) + lands], *([] if after is None else [after]))
    return (out[:n], out[n:2 * n], out[2 * n:3 * n], out[3 * n:4 * n]), out[4 * n]


def exchange_wait(handles, after, a2a, name):
    send, recv, x_thru, land_thru = handles
    n = len(x_thru)

    def body(*refs):
        x_refs, land_refs = refs[:n], refs[n:2 * n]
        s_refs, r_refs = refs[2 * n:3 * n], refs[3 * n:4 * n]
        for cp in _exchange_copies(x_refs, land_refs, s_refs, r_refs, a2a):
            cp.wait_send()
            cp.wait_recv()

    out = pl.pallas_call(
        body, name=name,
        out_shape=tuple([pltpu.HBM(a.shape, a.dtype) for a in list(x_thru) + list(land_thru)]),
        in_specs=[HBM_SPEC] * (2 * n) + [SEM_SPEC] * (2 * n) + [pl.BlockSpec(memory_space=pl.ANY)],
        out_specs=tuple([HBM_SPEC] * (2 * n)),
        input_output_aliases={i: i for i in range(2 * n)},
        compiler_params=pltpu.CompilerParams(has_side_effects=DATAFLOW),
    )(*x_thru, *land_thru, *send, *recv, after)
    return out[n:2 * n]


def _local_step(x, target, weight, emit, P):
    s = x.shape[0]
    tabs = _rope_tables(s)
    sinkb = jnp.broadcast_to(P["ev_sinks"].reshape(N_HEADS, 1), (N_HEADS, LANES))
    sbx = jnp.repeat(P["od_spatial_b"].reshape(D_GROUPS, BLOCK).T, HEAD_DIM, axis=1)
    sw = P["od_spatial_w"].reshape(D_GROUPS, BLOCK, BLOCK)
    fg = P["ffn_norm_g"]
    latest = [None]

    def out(name, layer, grad):
        tok = emit(name, layer, grad)
        if tok is not None:
            latest[0] = tok

    def dep(a):
        return a if latest[0] is None else a + latest[0][0:1, 0:1]

    n0 = rmsnorm(x, P["ev_norm_g"], "rms_in")
    qk_e, v_e, ga, gb = inproj(n0, weight("ev_w_in", 0, n0), tabs, ATT_W + 128,
                               (ATT_W + 128, 128, CONV_CH, CONV_CH), (), "ev_inproj")
    a_e, lse_e = attn_fwd(qk_e, v_e, d=1, hkv=A_KV_HEADS, max_dist=BLOCK - 1, sink=sinkb, out_dtype=ACT,
                          name="ev_attn")
    c_act, c1 = conv_fwd(ga, gb, P["ev_conv_w"], P["ev_conv_b"], P["ev_conv_ln_g"], P["ev_conv_ln_b"], "ev_conv")
    h1, n1 = mm_res([a_e, c_act], weight("ev_w_out", 0, c_act), x, fg[0:1], "ev_outproj")
    gate0, up0, act0 = ffn_up(n1, weight("ffn_w_gate", 0, n1), weight("ffn_w_up", 0, n1), "ffn0_up")
    h2, n2 = mm_res([act0], weight("ffn_w_down", 0, act0), h1, P["od_norm_g"], "ffn0_down")
    fold_dils = tuple(dil for _, dil in DILATED if dil > 1)
    qk_o, v_o, z, *pre = inproj(n2, weight("od_w_in", 0, n2), tabs, 2 * ATT_W, (2 * ATT_W, ATT_W, 2 * D_CH),
                                fold_dils, "od_inproj")
    folded = {dil: (pre[2 * i], pre[2 * i + 1]) for i, dil in enumerate(fold_dils)}
    outs, lses = [], []
    for window, dil in DILATED:
        assert window // dil == BLOCK
        o_r, l_r = attn_fwd(qk_o, v_o, d=dil, hkv=N_HEADS, max_dist=BLOCK, sink=None, out_dtype=F32,
                            name="od_attn_d%d" % dil, folded=folded.get(dil))
        outs.append(o_r)
        lses.append(l_r)
    c_out, lse_o = combine_fwd(outs, lses, "od_combine")
    d_out = gmlp_fwd(z, P["od_sgu_ln_g"], P["od_sgu_ln_b"], sw, sbx, "od_gmlp")
    h3, n3 = mm_res([c_out, d_out], weight("od_w_out", 0, d_out), h2, fg[1:2], "od_outproj")
    gate1, up1, act1 = ffn_up(n3, weight("ffn_w_gate", 1, n3), weight("ffn_w_up", 1, n3), "ffn1_up")
    h4, _ = mm_res([act1], weight("ffn_w_down", 1, act1), h3, None, "ffn1_down")
    loss_part, dh4, dg_final = final_loss(h4, P["final_norm_g"], target, "loss_head")

    def ffn_bwd(layer, dh_out, h_in, n_in, gate, up, act):
        wg, wu, wd = (weight(n, layer, dh_out) for n in ("ffn_w_gate", "ffn_w_up", "ffn_w_down"))
        tag = "ffn%d" % layer
        dgate, dup = ffn_down_bwd(dh_out, wd, gate, up, tag + "_down_bwd")
        g_wd = mm_tn(act, dh_out, tag + "_dwd")
        dh_in, dgn = mm_nt_rms([(dgate, wg), (dup, wu)], h_in, dep(fg[layer:layer + 1]), dh_out, tag + "_up_bwd")
        out("ffn_w_down", layer, g_wd)
        out("ffn_w_gate", layer, mm_tn(n_in, dgate, tag + "_dwg"))
        out("ffn_w_up", layer, mm_tn(n_in, dup, tag + "_dwu"))
        return dh_in, dgn[0:1]

    dh3, dgn_f1 = ffn_bwd(1, dh4, h3, n3, gate1, up1, act1)

    dcd = mm_nt(dh3, weight("od_w_out", 0, dh3), "od_outproj_bwd")
    dz, g_sw, g_sbx, g_slg, g_slb = gmlp_bwd(z, dcd, 1, dep(P["od_sgu_ln_g"]), P["od_sgu_ln_b"], sw, sbx,
                                             "od_gmlp_bwd")
    g_sb = jnp.sum(g_sbx.reshape(BLOCK, D_GROUPS, HEAD_DIM), axis=-1).T
    out("od_spatial", 0, jnp.concatenate([g_sw.reshape(D_GROUPS * BLOCK, BLOCK), g_sb], axis=0))
    out("od_w_out", 0, jnp.concatenate([mm_tn(c_out, dh3, "od_dwo_c"), mm_tn(d_out, dh3, "od_dwo_d")], axis=0))
    dqkv = [attn_bwd(qk_o, v_o, dcd if dil == 1 else dcd[:, :ATT_W], 0, c_out, lse_o, d=dil, hkv=N_HEADS,
                     max_dist=BLOCK, sink=None, out_dtype=ACT, name="od_attn_bwd_d%d" % dil,
                     folded=folded.get(dil), keep_folded=True)
            for window, dil in DILATED]
    dils = [dil for _, dil in DILATED]
    dproj_o = assemble([([(b[j], dil) for b, dil in zip(dqkv, dils)], j < 2) for j in range(3)] + [([dz], False)],
                       tabs, "od_dproj")
    dh2, dgn_od = mm_nt_rms([(dproj_o, weight("od_w_in", 0, dproj_o))], h2, dep(P["od_norm_g"]), dh3,
                            "od_inproj_bwd")
    out("od_w_in", 0, mm_tn(n2, dproj_o, "od_dwi"))

    dh1, dgn_f0 = ffn_bwd(0, dh2, h1, n1, gate0, up0, act0)

    dac = mm_nt(dh1, weight("ev_w_out", 0, dh1), "ev_outproj_bwd")
    dc1, g_clg, g_clb = conv_bwd_ln(c1, dac, 1, dep(P["ev_conv_ln_g"]), P["ev_conv_ln_b"], "ev_conv_bwd_ln")
    out("ev_w_out", 0, jnp.concatenate([mm_tn(a_e, dh1, "ev_dwo_a"), mm_tn(c_act, dh1, "ev_dwo_c")], axis=0))
    dga, dgb, g_cw, g_cb = conv_bwd_conv(dc1, ga, gb, P["ev_conv_w"], "ev_conv_bwd")
    dq, dk, dv, dsink = attn_bwd(qk_e, v_e, dac, 0, a_e, lse_e, d=1, hkv=A_KV_HEADS, max_dist=BLOCK - 1,
                                 sink=sinkb, out_dtype=F32, name="ev_attn_bwd")
    dproj_e = assemble([([dq], True), ([dk], True), ([dv], False), ([dga], False), ([dgb], False)], tabs,
                       "ev_dproj")
    out("ev_w_in", 0, mm_tn(n0, dproj_e, "ev_dwi"))
    dx, dgn_ev = mm_nt_rms([(dproj_e, weight("ev_w_in", 0, dproj_e))], x, dep(P["ev_norm_g"]), dh1,
                           "ev_inproj_bwd")

    small = {
        "ev_norm_g": dgn_ev[0:1],
        "ev_sinks": dsink[:, 0:1].reshape(1, N_HEADS),
        "ev_conv_w": jnp.sum(g_cw.reshape(CONV_HALO, SUBLANES, CONV_CH), axis=1)[:CONV_WIDTH],
        "ev_conv_b": jnp.sum(g_cb, axis=0, keepdims=True),
        "ev_conv_ln_g": g_clg[0:1],
        "ev_conv_ln_b": g_clb[0:1],
        "od_norm_g": dgn_od[0:1],
        "od_sgu_ln_g": g_slg[0:1],
        "od_sgu_ln_b": g_slb[0:1],
        "od_spatial_w": g_sw.reshape(D_GROUPS * BLOCK, BLOCK),
        "od_spatial_b": g_sb,
        "ffn_norm_g": jnp.concatenate([dgn_f0, dgn_f1], axis=0),
        "final_norm_g": dg_final[0:1],
    }
    return loss_part, dx, small


BIG = ("ev_w_in", "ev_w_out", "od_w_in", "od_w_out", "ffn_w_gate", "ffn_w_up", "ffn_w_down")
COL_SHARDED = ("ev_w_in", "od_w_in", "ffn_w_gate", "ffn_w_up")
GATHER_GROUPS = (
    (("ev_w_in", 0),),
    (("ev_w_out", 0),),
    (("ffn_w_gate", 0), ("ffn_w_up", 0)),
    (("ffn_w_down", 0),),
    (("od_w_in", 0),),
    (("od_w_out", 0),),
    (("ffn_w_gate", 1), ("ffn_w_up", 1)),
    (("ffn_w_down", 1),),
)
GATHER_EARLY = 4
GATHER_LATE_AT = 2
REDUCE_GROUPS = (
    (("ffn_w_down", 1), ("ffn_w_gate", 1), ("ffn_w_up", 1)),
    (("od_w_out", 0),),
    (("od_w_in", 0),),
    (("ffn_w_down", 0), ("ffn_w_gate", 0), ("ffn_w_up", 0)),
    (("ev_w_out", 0),),
    (("ev_w_in", 0),),
)


def _unshard(name, g):
    if name in COL_SHARDED:
        return jnp.moveaxis(g, 0, 1).reshape(g.shape[1], N_DEV * g.shape[2])
    return g.reshape(N_DEV * g.shape[1], g.shape[2])


def _shard_slots(name, full):
    r, c = full.shape
    if name in COL_SHARDED:
        return jnp.moveaxis(full.reshape(r, N_DEV, c // N_DEV), 1, 0)
    return full.reshape(N_DEV, r // N_DEV, c)


def kernel(x, ev_norm_g, ev_w_in, ev_sinks, ev_conv_w, ev_conv_b, ev_conv_ln_g, ev_conv_ln_b, ev_w_out, od_norm_g, od_w_in, od_sgu_ln_g, od_sgu_ln_b, od_spatial_w, od_spatial_b, od_w_out, ffn_norm_g, ffn_w_gate, ffn_w_up, ffn_w_down, final_norm_g, loss_target, m_ev_norm_g, m_ev_w_in, m_ev_sinks, m_ev_conv_w, m_ev_conv_b, m_ev_conv_ln_g, m_ev_conv_ln_b, m_ev_w_out, m_od_norm_g, m_od_w_in, m_od_sgu_ln_g, m_od_sgu_ln_b, m_od_spatial_w, m_od_spatial_b, m_od_w_out, m_ffn_norm_g, m_ffn_w_gate, m_ffn_w_up, m_ffn_w_down, m_final_norm_g, v_ev_norm_g, v_ev_w_in, v_ev_sinks, v_ev_conv_w, v_ev_conv_b, v_ev_conv_ln_g, v_ev_conv_ln_b, v_ev_w_out, v_od_norm_g, v_od_w_in, v_od_sgu_ln_g, v_od_sgu_ln_b, v_od_spatial_w, v_od_spatial_b, v_od_w_out, v_ffn_norm_g, v_ffn_w_gate, v_ffn_w_up, v_ffn_w_down, v_final_norm_g):
    names = ["ev_norm_g", "ev_w_in", "ev_sinks", "ev_conv_w", "ev_conv_b", "ev_conv_ln_g", "ev_conv_ln_b", "ev_w_out",
             "od_norm_g", "od_w_in", "od_sgu_ln_g", "od_sgu_ln_b", "od_spatial_w", "od_spatial_b", "od_w_out",
             "ffn_norm_g", "ffn_w_gate", "ffn_w_up", "ffn_w_down", "final_norm_g"]
    wts = dict(zip(names, [ev_norm_g, ev_w_in, ev_sinks, ev_conv_w, ev_conv_b, ev_conv_ln_g, ev_conv_ln_b, ev_w_out,
                           od_norm_g, od_w_in, od_sgu_ln_g, od_sgu_ln_b, od_spatial_w, od_spatial_b, od_w_out,
                           ffn_norm_g, ffn_w_gate, ffn_w_up, ffn_w_down, final_norm_g]))
    mom = dict(zip(names, [m_ev_norm_g, m_ev_w_in, m_ev_sinks, m_ev_conv_w, m_ev_conv_b, m_ev_conv_ln_g, m_ev_conv_ln_b,
                           m_ev_w_out, m_od_norm_g, m_od_w_in, m_od_sgu_ln_g, m_od_sgu_ln_b, m_od_spatial_w,
                           m_od_spatial_b, m_od_w_out, m_ffn_norm_g, m_ffn_w_gate, m_ffn_w_up, m_ffn_w_down,
                           m_final_norm_g]))
    vel = dict(zip(names, [v_ev_norm_g, v_ev_w_in, v_ev_sinks, v_ev_conv_w, v_ev_conv_b, v_ev_conv_ln_g, v_ev_conv_ln_b,
                           v_ev_w_out, v_od_norm_g, v_od_w_in, v_od_sgu_ln_g, v_od_sgu_ln_b, v_od_spatial_w,
                           v_od_spatial_b, v_od_w_out, v_ffn_norm_g, v_ffn_w_gate, v_ffn_w_up, v_ffn_w_down,
                           v_final_norm_g]))
    me = _my_rank()

    sp = jnp.zeros((40, LANES), F32)
    sp = sp.at[0:CONV_WIDTH, 0:64].set(ev_conv_w[0])
    sp = sp.at[32, :].set(od_norm_g[0])
    sp = sp.at[33, 0:64].set(od_sgu_ln_g[0])
    sp = sp.at[34, 0:64].set(od_sgu_ln_b[0])

    early = [k for grp in GATHER_GROUPS[:GATHER_EARLY] for k in grp]
    late = [k for grp in GATHER_GROUPS[GATHER_EARLY:] for k in grp]
    early_act = dict(zip(early, cast_layers([(wts[n], l) for n, l in early], "cast_early")))
    late_act = dict(zip(late, cast_layers([(wts[n], l) for n, l in late], "cast_late")))
    ag_early, ag_token = exchange_start([sp] + [early_act[k] for k in early], False, "ag_start")
    ag_late = []
    full_w = {}
    P = {
        "ev_norm_g": ev_norm_g + ag_token[0:1, 0:1], "ev_sinks": ev_sinks, "ev_conv_b": ev_conv_b,
        "ev_conv_ln_g": ev_conv_ln_g, "ev_conv_ln_b": ev_conv_ln_b,
        "od_spatial_w": od_spatial_w, "od_spatial_b": od_spatial_b, "ffn_norm_g": ffn_norm_g,
        "final_norm_g": final_norm_g.reshape(1, D_MODEL),
    }

    def weight(name, layer, after):
        if (name, layer) not in full_w:
            gi = [i for i, grp in enumerate(GATHER_GROUPS) if (name, layer) in grp][0]
            if gi < GATHER_EARLY:
                idx = [1 + early.index(k) for k in GATHER_GROUPS[gi]]
                handles = ag_early
            else:
                idx = [late.index(k) for k in GATHER_GROUPS[gi]]
                handles = ag_late[0]
            if gi == 0:
                idx = [0] + idx
            lands = exchange_wait(tuple([h[i] for i in idx] for h in handles), after, False, "ag_wait%d" % gi)
            if gi == 0:
                spg, lands = lands[0], lands[1:]
                P["ev_conv_w"] = jnp.moveaxis(spg[:, 0:CONV_WIDTH, 0:64], 0, 1).reshape(CONV_WIDTH, CONV_CH)
                P["od_norm_g"] = spg[:, 32, :].reshape(1, D_MODEL)
                P["od_sgu_ln_g"] = spg[:, 33, 0:64].reshape(1, D_CH)
                P["od_sgu_ln_b"] = spg[:, 34, 0:64].reshape(1, D_CH)
            if gi == GATHER_LATE_AT:
                ag_late.append(exchange_start([late_act[k] for k in late], False, "ag_start_late", after=lands[0])[0])
            for k, land in zip(GATHER_GROUPS[gi], lands):
                full_w[k] = _unshard(k[0], land)
        return full_w[(name, layer)]

    pending, rs_started, spatial = {}, [], []

    def emit(name, layer, grad):
        if name == "od_spatial":
            handles, token = exchange_start([grad], False, "ar_start_b")
            spatial.append(handles)
            return token
        pending[(name, layer)] = grad
        for gi, grp in enumerate(REDUCE_GROUPS):
            if (name, layer) in grp and all(k in pending for k in grp):
                handles, token = exchange_start([_shard_slots(k[0], pending[k]).astype(ACT) for k in grp], True,
                                                "rs_start%d" % gi)
                rs_started.append((gi, handles))
                return token
        return None

    loss_part, dx, small = _local_step(x[0], loss_target[0], weight, emit, P)
    loss = lax.psum(loss_part[0, 0], AXES)

    wide = ["ev_norm_g", "ev_sinks", "ev_conv_w", "ev_conv_b", "ev_conv_ln_g", "ev_conv_ln_b", "od_norm_g",
            "od_sgu_ln_g", "od_sgu_ln_b", "ffn_norm_g", "final_norm_g"]
    blk_a = jnp.concatenate(
        [jnp.pad(small[n], ((0, 0), (0, D_MODEL - small[n].shape[1]))) for n in wide], axis=0)
    blk_a = jnp.pad(blk_a, ((0, 48 - blk_a.shape[0]), (0, 0)))
    ar_a, ar_token = exchange_start([blk_a], False, "ar_start_a")

    results = {}
    for gi, handles in rs_started:
        lands = exchange_wait(handles, ar_token, True, "rs_wait%d" % gi)
        for (n, l), land in zip(REDUCE_GROUPS[gi], lands):
            results[n] = sum_adamw(land, wts[n], mom[n], vel[n], l, results.get(n), "adamw_%s%d" % (n, l))
    out_g, out_d, out_m, out_v = {}, {}, {}, {}
    for n in BIG:
        out_g[n], out_d[n], out_m[n], out_v[n] = results[n]

    last = results[REDUCE_GROUPS[-1][-1][0]][0]
    sum_b = sum_slots(exchange_wait(spatial[0], last, False, "ar_wait_b")[0], F32, "ar_sum_b")
    sum_a = sum_slots(exchange_wait(ar_a, last, False, "ar_wait_a")[0], F32, "ar_sum_a")
    full = {}
    off = 0
    for n in wide:
        r_, c_ = small[n].shape
        full[n] = sum_a[off:off + r_, 0:c_]
        off += r_
    full["od_spatial_w"] = sum_b[0:D_GROUPS * BLOCK]
    full["od_spatial_b"] = sum_b[D_GROUPS * BLOCK:D_GROUPS * BLOCK + D_GROUPS]
    full["ev_conv_w"] = lax.dynamic_slice_in_dim(full["ev_conv_w"], me * 64, 64, axis=1)
    full["od_norm_g"] = lax.dynamic_slice_in_dim(full["od_norm_g"], me * 128, 128, axis=1)
    full["od_sgu_ln_g"] = lax.dynamic_slice_in_dim(full["od_sgu_ln_g"], me * 64, 64, axis=1)
    full["od_sgu_ln_b"] = lax.dynamic_slice_in_dim(full["od_sgu_ln_b"], me * 64, 64, axis=1)

    small_names = [n for n in names if n not in BIG]
    view = {n: ((-1, wts[n].shape[-1]) if wts[n].ndim > 1 else (1, -1)) for n in small_names}
    ds, ms, vs = adamw_small([wts[n].reshape(view[n]) for n in small_names],
                             [full[n].reshape(view[n]) for n in small_names],
                             [mom[n].reshape(view[n]) for n in small_names],
                             [vel[n].reshape(view[n]) for n in small_names], "adamw_small")
    for i, n in enumerate(small_names):
        shp = wts[n].shape
        out_g[n], out_d[n], out_m[n], out_v[n] = (full[n].reshape(shp), ds[i].reshape(shp), ms[i].reshape(shp),
                                                  vs[i].reshape(shp))

    return (loss, dx[None], *[out_g[n] for n in names], *[out_d[n] for n in names],
            *[out_m[n] for n in names], *[out_v[n] for n in names])
```

```python
import functools
import math

import jax
import jax.numpy as jnp
from jax import lax
from jax.experimental import pallas as pl
from jax.experimental.pallas import tpu as pltpu

F32 = jnp.float32
ACT = jnp.bfloat16

D_MODEL = 1024
HEAD_DIM = 64
N_HEADS = 8
ATT_W = N_HEADS * HEAD_DIM
A_KV_HEADS = 2
CONV_CH = 512
CONV_WIDTH = 31
CONV_HALO = 32
D_CH = 512
D_GROUPS = 8
BLOCK = 128
D_FF = 2816
ROT_DIM = 16
ROPE_THETA = 500000.0
RMS_EPS = 1e-6
LN_EPS = 1e-5
DILATED = ((128, 1), (512, 4), (2048, 16))
NEG = -1e30
LANES = 128

ADAM_LR = 0.001
ADAM_B1 = 0.9
ADAM_B2 = 0.999
ADAM_EPS = 1e-08
ADAM_WD = 0.01
ADAM_STEP = 10

V7X_VMEM_LIMIT = 56 * 1024 * 1024
SMALL_BLOCK_BYTES = 6 * 1024 * 1024
N_DEV = 8

NN = (((1,), (0,)), ((), ()))
NT = (((1,), (1,)), ((), ()))
TN = (((0,), (0,)), ((), ()))
MESH = pl.DeviceIdType.MESH
AXES = ("x", "y", "c")


def _dot(a, b, dims):
    return lax.dot_general(a, b, dims, preferred_element_type=F32)


def _cp(*sem):
    return pltpu.CompilerParams(dimension_semantics=sem if sem else None,
                                vmem_limit_bytes=V7X_VMEM_LIMIT)


def _tile(n, prefs):
    for p in prefs:
        if n % p == 0:
            return p
    return n


def _sigmoid(x):
    return 1.0 / (1.0 + jnp.exp(-x))


def _rowspec(tm, w, col=0):
    return pl.BlockSpec((tm, w), lambda i, col=col: (i, col))


def _fullspec(shape):
    nd = len(shape)
    return pl.BlockSpec(shape, lambda *a, nd=nd: (0,) * nd)


def _rope_tables(seq):
    half = ROT_DIM // 2
    inv_freq = ROPE_THETA ** (-jnp.arange(half, dtype=F32) * (2.0 / ROT_DIM))
    ang = jnp.arange(seq, dtype=jnp.int32).astype(F32)[:, None] * inv_freq[None, :]
    cos, sin = jnp.cos(ang), jnp.sin(ang)
    lane = jnp.arange(LANES)
    jm = lane % HEAD_DIM
    idx = jm % half
    c = jnp.where(jm[None, :] < ROT_DIM, cos[:, idx], 1.0)
    sa = jnp.where(jm[None, :] < half, -sin[:, idx], 0.0)
    sb = jnp.where((jm[None, :] >= half) & (jm[None, :] < ROT_DIM), sin[:, idx], 0.0)
    return c.astype(F32), sa.astype(F32), sb.astype(F32)


def _rope(x, c, sa, sb):
    return x * c + pltpu.roll(x, LANES - 8, 1) * sa + pltpu.roll(x, 8, 1) * sb


def _rope_t(d, c, sa, sb):
    return d * c + pltpu.roll(d * sa, 8, 1) + pltpu.roll(d * sb, LANES - 8, 1)


def rmsnorm(h, g, name):
    s = h.shape[0]
    tm = _tile(s, (512,))

    def body(h_ref, g_ref, o_ref):
        x = h_ref[...]
        r = lax.rsqrt(jnp.mean(x * x, axis=-1, keepdims=True) + RMS_EPS)
        o_ref[...] = (x * r * g_ref[...]).astype(o_ref.dtype)

    return pl.pallas_call(
        body, grid=(s // tm,), name=name,
        in_specs=[_rowspec(tm, D_MODEL), _fullspec((1, D_MODEL))],
        out_specs=_rowspec(tm, D_MODEL),
        out_shape=jax.ShapeDtypeStruct((s, D_MODEL), ACT),
        compiler_params=_cp("parallel"))(h, g)


def _fold_perm(tm, d, inverse, cache):
    key = (tm, d, inverse)
    if key not in cache:
        m = tm // d
        a = lax.broadcasted_iota(jnp.int32, (tm, tm), 1 if inverse else 0)
        b = lax.broadcasted_iota(jnp.int32, (tm, tm), 0 if inverse else 1)
        src = (a & (m - 1)) * d + (a >> (m.bit_length() - 1))
        cache[key] = (b == src).astype(ACT)
    return cache[key]


def _fold_rows(x, d, cache):
    tm = x.shape[0]
    m = tm // d
    p = _dot(_fold_perm(tm, d, False, cache), x, NN).astype(ACT)
    return jnp.concatenate([p[r * m:(r + 1) * m] for r in range(d)], axis=1)


def _unfold_rows(blk, d, cache):
    w = blk.shape[1] // d
    stacked = jnp.concatenate([blk[:, r * w:(r + 1) * w] for r in range(d)], axis=0)
    return _dot(_fold_perm(stacked.shape[0], d, True, cache), stacked, NN)


def _unfold_rows_f32(blk, d, cache):
    hi = blk.astype(ACT)
    r1 = blk - hi.astype(F32)
    mid = r1.astype(ACT)
    lo = (r1 - mid.astype(F32)).astype(ACT)
    return _unfold_rows(hi, d, cache) + _unfold_rows(mid, d, cache) + _unfold_rows(lo, d, cache)


def inproj(n, w, tabs, nqk, splits, fold_dils, name):
    s = n.shape[0]
    ntot = w.shape[1]
    assert sum(splits) == ntot and splits[0] == nqk
    tm = _tile(s, (256,))
    ns = len(splits)

    def body(n_ref, w_ref, c_ref, sa_ref, sb_ref, *outs):
        res = _dot(n_ref[...], w_ref[...], NN)
        c, sa, sb = c_ref[...], sa_ref[...], sb_ref[...]
        for g in range(nqk // LANES):
            x = res[:, g * LANES:(g + 1) * LANES]
            outs[0][:, g * LANES:(g + 1) * LANES] = _rope(x, c, sa, sb).astype(ACT)
        off = nqk
        for o_ref, wd in zip(outs[1:ns], splits[1:]):
            o_ref[...] = res[:, off:off + wd].astype(ACT)
            off += wd
        cache = {}
        for i, d in enumerate(fold_dils):
            outs[ns + 2 * i][...] = _fold_rows(outs[0][...], d, cache)
            outs[ns + 2 * i + 1][...] = _fold_rows(outs[1][...], d, cache)

    out_specs = [_rowspec(tm, wd) for wd in splits]
    out_shape = [jax.ShapeDtypeStruct((s, wd), ACT) for wd in splits]
    for d in fold_dils:
        for wd in splits[:2]:
            out_specs.append(_rowspec(tm // d, d * wd))
            out_shape.append(jax.ShapeDtypeStruct((s // d, d * wd), ACT))
    return pl.pallas_call(
        body, grid=(s // tm,), name=name,
        in_specs=[_rowspec(tm, D_MODEL), _fullspec((D_MODEL, ntot))] + [_rowspec(tm, LANES)] * 3,
        out_specs=out_specs, out_shape=out_shape,
        compiler_params=_cp("parallel"))(n, w, *tabs)


def ffn_up(n, wg, wu, name):
    s = n.shape[0]
    f = wg.shape[1]
    tm = _tile(s, (512,))
    tf = _tile(f, (1408, 512, 256, 128))

    def body(n_ref, wg_ref, wu_ref, g_ref, u_ref, a_ref):
        a = n_ref[...]
        g = _dot(a, wg_ref[...], NN)
        u = _dot(a, wu_ref[...], NN)
        sg = _sigmoid(g)
        silu = g * sg
        g_ref[...] = (u * (sg * (1.0 + g * (1.0 - sg)))).astype(ACT)
        u_ref[...] = silu.astype(ACT)
        a_ref[...] = (silu * u).astype(ACT)

    wspec = pl.BlockSpec((D_MODEL, tf), lambda j, i: (0, j))
    ospec = pl.BlockSpec((tm, tf), lambda j, i: (i, j))
    return pl.pallas_call(
        body, grid=(f // tf, s // tm), name=name,
        in_specs=[pl.BlockSpec((tm, D_MODEL), lambda j, i: (i, 0)), wspec, wspec],
        out_specs=[ospec] * 3,
        out_shape=[jax.ShapeDtypeStruct((s, f), ACT)] * 3,
        compiler_params=_cp("parallel", "parallel"))(n, wg, wu)


def mm_res(parts, w, h, gnext, name):
    s = h.shape[0]
    tm = _tile(s, (256,))
    widths = [p.shape[1] for p in parts]
    assert sum(widths) == w.shape[0]
    np_ = len(parts)

    def body(*refs):
        p_refs = refs[:np_]
        w_ref, h_ref = refs[np_], refs[np_ + 1]
        rest = refs[np_ + 2:]
        acc = h_ref[...]
        off = 0
        for p_ref, wd in zip(p_refs, widths):
            acc = acc + _dot(p_ref[...], w_ref[off:off + wd, :], NN)
            off += wd
        if gnext is None:
            rest[0][...] = acc
        else:
            g_ref, ho_ref, no_ref = rest
            ho_ref[...] = acc
            r = lax.rsqrt(jnp.mean(acc * acc, axis=-1, keepdims=True) + RMS_EPS)
            no_ref[...] = (acc * r * g_ref[...]).astype(ACT)

    in_specs = [_rowspec(tm, wd) for wd in widths] + [_fullspec(w.shape), _rowspec(tm, D_MODEL)]
    args = list(parts) + [w, h]
    out_specs = [_rowspec(tm, D_MODEL)]
    out_shape = [jax.ShapeDtypeStruct((s, D_MODEL), F32)]
    if gnext is not None:
        in_specs.append(_fullspec((1, D_MODEL)))
        args.append(gnext)
        out_specs.append(_rowspec(tm, D_MODEL))
        out_shape.append(jax.ShapeDtypeStruct((s, D_MODEL), ACT))
    out = pl.pallas_call(
        body, grid=(s // tm,), name=name, in_specs=in_specs, out_specs=out_specs,
        out_shape=out_shape, compiler_params=_cp("parallel"))(*args)
    return (out[0], None) if gnext is None else (out[0], out[1])


def _band_mask(n, max_dist):
    qi = lax.broadcasted_iota(jnp.int32, (BLOCK, 2 * BLOCK), 0)
    kj = lax.broadcasted_iota(jnp.int32, (BLOCK, 2 * BLOCK), 1)
    dist = qi + BLOCK - kj
    valid = jnp.logical_and(dist >= 0, dist <= max_dist)
    return jnp.logical_and(valid, jnp.logical_or(kj >= BLOCK, n > 0))


def _band_mask_t(n, max_dist):
    kj = lax.broadcasted_iota(jnp.int32, (2 * BLOCK, BLOCK), 0)
    qi = lax.broadcasted_iota(jnp.int32, (2 * BLOCK, BLOCK), 1)
    dist = qi + BLOCK - kj
    valid = jnp.logical_and(dist >= 0, dist <= max_dist)
    return jnp.logical_and(valid, jnp.logical_or(kj >= BLOCK, n > 0))


def _fold(a, d):
    return a.reshape(a.shape[0] // d, d * a.shape[1])


def attn_fwd(qk, v, *, d, hkv, max_dist, sink, out_dtype, name, folded=None, keep_folded=False):
    s = qk.shape[0]
    kvw = hkv * HEAD_DIM
    wqk = ATT_W + kvw
    assert qk.shape[1] == wqk and (d == 1 or (wqk % ATT_W == 0 and wqk % kvw == 0))
    assert sink is None or max_dist == BLOCK - 1
    nb = s // d // BLOCK
    grp = N_HEADS // hkv
    qpb, kpb, koff = wqk // ATT_W, wqk // kvw, ATT_W // kvw

    def body(*refs):
        if sink is None:
            q_ref, kc_ref, kp_ref, vc_ref, vp_ref, o_ref, l_ref, o_buf = refs
        else:
            q_ref, kc_ref, kp_ref, vc_ref, vp_ref, s_ref, o_ref, l_ref, o_buf = refs
        n = pl.program_id(1)
        valid = _band_mask_t(n, max_dist)
        qsl = [slice(h * HEAD_DIM, (h + 1) * HEAD_DIM) for h in range(N_HEADS)]
        ksl = [slice((h // grp) * HEAD_DIM, (h // grp + 1) * HEAD_DIM) for h in range(N_HEADS)]
        kk = jnp.concatenate([kp_ref[...], kc_ref[...]], axis=0)
        vv = jnp.concatenate([vp_ref[...], vc_ref[...]], axis=0)
        scores = []
        for h in range(N_HEADS):
            q = q_ref[:, qsl[h]] * 0.125
            scores.append(_dot(kk[:, ksl[h]], q, NT))
        probs = []
        head_row = lax.broadcasted_iota(jnp.int32, (SUBLANES, BLOCK), 0)
        lse8 = jnp.zeros((SUBLANES, BLOCK), F32)
        if sink is not None:
            sink_row = lax.broadcasted_iota(jnp.int32, (2 * BLOCK, BLOCK), 0) == 0
        for h in range(N_HEADS):
            sc = jnp.where(valid, scores[h], NEG)
            if sink is not None:
                sc = jnp.where(sink_row, s_ref[h:h + 1, 0:1], sc)
            m = jnp.max(sc, axis=0, keepdims=True)
            p = jnp.exp(sc - m)
            l = jnp.sum(p, axis=0, keepdims=True)
            if sink is not None:
                p = jnp.where(sink_row, 0.0, p)
            lse8 = jnp.where(head_row == h, m + jnp.log(l), lse8)
            probs.append((p * (1.0 / l)).astype(ACT))
        l_ref[...] = jnp.concatenate([lse8, jnp.zeros((BLOCK - SUBLANES, BLOCK), F32)], axis=0).T
        for h in range(N_HEADS):
            o_buf[:, qsl[h]] = _dot(probs[h], vv[:, ksl[h]], TN)
        o_ref[...] = o_buf[...].astype(o_ref.dtype)

    prev = lambda n: jnp.maximum(n - 1, 0)
    in_specs = [
        pl.BlockSpec((BLOCK, ATT_W), lambda r, n: (n, r * qpb)),
        pl.BlockSpec((BLOCK, kvw), lambda r, n: (n, r * kpb + koff)),
        pl.BlockSpec((BLOCK, kvw), lambda r, n: (prev(n), r * kpb + koff)),
        pl.BlockSpec((BLOCK, kvw), lambda r, n: (n, r)),
        pl.BlockSpec((BLOCK, kvw), lambda r, n: (prev(n), r)),
    ]
    qkf, vf = (_fold(qk, d), _fold(v, d)) if folded is None else folded
    args = [qkf, qkf, qkf, vf, vf]
    if sink is not None:
        in_specs.append(_fullspec((N_HEADS, LANES)))
        args.append(sink)
    ospec = pl.BlockSpec((BLOCK, ATT_W), lambda r, n: (n, r))
    lspec = pl.BlockSpec((BLOCK, LANES), lambda r, n: (n, r))
    o, lse = pl.pallas_call(
        body, grid=(d, nb), name=name, in_specs=in_specs, out_specs=[ospec, lspec],
        out_shape=[jax.ShapeDtypeStruct((s // d, d * ATT_W), out_dtype),
                   jax.ShapeDtypeStruct((s // d, d * LANES), F32)],
        scratch_shapes=[pltpu.VMEM((BLOCK, ATT_W), F32)],
        compiler_params=_cp("parallel", "parallel"))(*args)
    return (o, lse) if keep_folded else (o.reshape(s, ATT_W), lse.reshape(s, LANES))


def attn_bwd(qk, v, do_src, do_col, o, lse, *, d, hkv, max_dist, sink, out_dtype, name, folded=None,
             folded_do_o=None, keep_folded=False):
    s = qk.shape[0]
    kvw = hkv * HEAD_DIM
    wqk = ATT_W + kvw
    nb = s // d // BLOCK
    grp = N_HEADS // hkv
    qpb, kpb, koff = wqk // ATT_W, wqk // kvw, ATT_W // kvw
    dob = do_src.shape[1] // ATT_W
    has_sink = sink is not None

    def body(*refs):
        refs = list(refs)
        q_ref, kc_ref, kp_ref, vc_ref, vp_ref, do_ref, o_ref, l_ref = refs[:8]
        pos = 8
        if has_sink:
            s_ref = refs[pos]
            pos += 1
        dq_ref, dk_ref, dv_ref = refs[pos:pos + 3]
        pos += 3
        if has_sink:
            ds_ref = refs[pos]
            pos += 1
        ck_ref, cv_ref, dq_buf, dk_buf, dv_buf = refs[pos:pos + 5]
        r_id = pl.program_id(0)
        n = pl.program_id(1)

        @pl.when(n == 0)
        def _():
            ck_ref[...] = jnp.zeros_like(ck_ref)
            cv_ref[...] = jnp.zeros_like(cv_ref)

        if has_sink:
            @pl.when(jnp.logical_and(n == 0, r_id == 0))
            def _():
                ds_ref[...] = jnp.zeros_like(ds_ref)

        @pl.when(n < nb)
        def _():
            valid = _band_mask_t(n, max_dist)
            qsl = [slice(h * HEAD_DIM, (h + 1) * HEAD_DIM) for h in range(N_HEADS)]
            ksl = [slice((h // grp) * HEAD_DIM, (h // grp + 1) * HEAD_DIM) for h in range(N_HEADS)]
            kk = jnp.concatenate([kp_ref[...], kc_ref[...]], axis=0)
            vv = jnp.concatenate([vp_ref[...], vc_ref[...]], axis=0)
            qs, first = [], []
            for h in range(N_HEADS):
                q = q_ref[:, qsl[h]] * 0.125
                qs.append(q)
                first.append((_dot(kk[:, ksl[h]], q, NT), _dot(vv[:, ksl[h]], do_ref[:, qsl[h]], NT)))
            lse_t = l_ref[...].T
            prod = do_ref[...].astype(F32) * o_ref[...].astype(F32)
            hi = prod.astype(ACT)
            lo = (prod - hi.astype(F32)).astype(ACT)
            col = lax.broadcasted_iota(jnp.int32, (LANES, ATT_W), 1)
            row = lax.broadcasted_iota(jnp.int32, (LANES, ATT_W), 0)
            head_of = jnp.logical_and(col >= row * HEAD_DIM, col < (row + 1) * HEAD_DIM).astype(ACT)
            e_t = _dot(head_of, hi, NT) + _dot(head_of, lo, NT)
            mid = []
            for h in range(N_HEADS):
                s_t, dp_t = first[h]
                lse_h, e_h = lse_t[h:h + 1, :], e_t[h:h + 1, :]
                p_t = jnp.exp(jnp.where(valid, s_t, NEG) - lse_h)
                mid.append(((p_t * (dp_t - e_h)).astype(ACT), p_t.astype(ACT)))
                if has_sink:
                    sk = s_ref[h:h + 1, 0:1]
                    dsk = -jnp.sum(jnp.exp(sk - lse_h) * e_h, axis=1, keepdims=True)
                    ds_ref[h:h + 1, :] = ds_ref[h:h + 1, :] + dsk
            dkk = [None] * hkv
            dvv = [None] * hkv
            for h in range(N_HEADS):
                kh = h // grp
                ds_t, p_t = mid[h]
                dq_buf[:, qsl[h]] = _dot(ds_t, kk[:, ksl[h]], TN) * 0.125
                for lst, val in ((dkk, _dot(ds_t, qs[h], NN)), (dvv, _dot(p_t, do_ref[:, qsl[h]], NN))):
                    lst[kh] = val if lst[kh] is None else lst[kh] + val
            for kh in range(hkv):
                ks = slice(kh * HEAD_DIM, (kh + 1) * HEAD_DIM)
                dk_buf[:, ks] = ck_ref[:, ks] + dkk[kh][:BLOCK]
                dv_buf[:, ks] = cv_ref[:, ks] + dvv[kh][:BLOCK]
                ck_ref[:, ks] = dkk[kh][BLOCK:]
                cv_ref[:, ks] = dvv[kh][BLOCK:]
            dq_ref[...] = dq_buf[...].astype(dq_ref.dtype)
            dk_ref[...] = dk_buf[...].astype(dk_ref.dtype)
            dv_ref[...] = dv_buf[...].astype(dv_ref.dtype)

        @pl.when(n == nb)
        def _():
            dk_ref[...] = ck_ref[...].astype(dk_ref.dtype)
            dv_ref[...] = cv_ref[...].astype(dv_ref.dtype)

    qrow = lambda n: jnp.minimum(n, nb - 1)
    prow = lambda n: jnp.maximum(jnp.minimum(n, nb - 1) - 1, 0)
    krow = lambda n: jnp.maximum(n - 1, 0)
    in_specs = [
        pl.BlockSpec((BLOCK, ATT_W), lambda r, n: (qrow(n), r * qpb)),
        pl.BlockSpec((BLOCK, kvw), lambda r, n: (qrow(n), r * kpb + koff)),
        pl.BlockSpec((BLOCK, kvw), lambda r, n: (prow(n), r * kpb + koff)),
        pl.BlockSpec((BLOCK, kvw), lambda r, n: (qrow(n), r)),
        pl.BlockSpec((BLOCK, kvw), lambda r, n: (prow(n), r)),
        pl.BlockSpec((BLOCK, ATT_W), lambda r, n: (qrow(n), r * dob + do_col)),
        pl.BlockSpec((BLOCK, ATT_W), lambda r, n: (qrow(n), r)),
        pl.BlockSpec((BLOCK, LANES), lambda r, n: (qrow(n), r)),
    ]
    qkf, vf = (_fold(qk, d), _fold(v, d)) if folded is None else folded
    dof, of = (_fold(do_src, d), _fold(o, d)) if folded_do_o is None else folded_do_o
    args = [qkf, qkf, qkf, vf, vf, dof, of, _fold(lse, d)]
    if has_sink:
        in_specs.append(_fullspec((N_HEADS, LANES)))
        args.append(sink)
    qspec = pl.BlockSpec((BLOCK, ATT_W), lambda r, n: (qrow(n), r))
    kspec = pl.BlockSpec((BLOCK, kvw), lambda r, n: (krow(n), r))
    out_specs = [qspec, kspec, kspec]
    out_shape = [jax.ShapeDtypeStruct((s // d, d * ATT_W), out_dtype),
                 jax.ShapeDtypeStruct((s // d, d * kvw), out_dtype),
                 jax.ShapeDtypeStruct((s // d, d * kvw), out_dtype)]
    if has_sink:
        out_specs.append(_fullspec((N_HEADS, LANES)))
        out_shape.append(jax.ShapeDtypeStruct((N_HEADS, LANES), F32))
    out = pl.pallas_call(
        body, grid=(d, nb + 1), name=name, in_specs=in_specs, out_specs=out_specs,
        out_shape=out_shape,
        scratch_shapes=[pltpu.VMEM((BLOCK, kvw), F32), pltpu.VMEM((BLOCK, kvw), F32),
                        pltpu.VMEM((BLOCK, ATT_W), F32), pltpu.VMEM((BLOCK, kvw), F32), pltpu.VMEM((BLOCK, kvw), F32)],
        compiler_params=_cp("arbitrary", "arbitrary"))(*args)
    res = list(out[:3]) if keep_folded else [out[0].reshape(s, ATT_W), out[1].reshape(s, kvw), out[2].reshape(s, kvw)]
    if has_sink:
        res.append(out[3])
    return res


def combine_fwd(os_, lses, dils, name):
    s = os_[0].shape[0] * dils[0]
    tm = _tile(s, (256,))
    fold_dils = [d for d in dils if d > 1]

    def body(o1, o2, o3, l1, l2, l3, c_ref, l_ref, *rest):
        c_buf = rest[-1]
        cache = {}
        o = [r[...].astype(F32) if d == 1 else _unfold_rows(r[...], d, cache) for r, d in zip((o1, o2, o3), dils)]
        a, b, c = [r[...] if d == 1 else _unfold_rows_f32(r[...], d, cache) for r, d in zip((l1, l2, l3), dils)]
        m = jnp.maximum(jnp.maximum(a, b), c)
        wa, wb, wc = jnp.exp(a - m), jnp.exp(b - m), jnp.exp(c - m)
        tot = wa + wb + wc
        l_ref[...] = m + jnp.log(tot)
        rt = 1.0 / tot
        wa, wb, wc = wa * rt, wb * rt, wc * rt
        for h in range(N_HEADS):
            cs = slice(h * HEAD_DIM, (h + 1) * HEAD_DIM)
            c_buf[:, cs] = (wa[:, h:h + 1] * o[0][:, cs] + wb[:, h:h + 1] * o[1][:, cs] + wc[:, h:h + 1] * o[2][:, cs])
        mix = c_buf[...].astype(ACT)
        c_ref[...] = mix
        for f_ref, d in zip(rest[:-1], fold_dils):
            f_ref[...] = _fold_rows(mix, d, cache)

    return pl.pallas_call(
        body, grid=(s // tm,), name=name,
        in_specs=[_rowspec(tm // d, d * ATT_W) for d in dils] + [_rowspec(tm // d, d * LANES) for d in dils],
        out_specs=[_rowspec(tm, ATT_W), _rowspec(tm, LANES)] + [_rowspec(tm // d, d * ATT_W) for d in fold_dils],
        out_shape=[jax.ShapeDtypeStruct((s, ATT_W), ACT), jax.ShapeDtypeStruct((s, LANES), F32)]
        + [jax.ShapeDtypeStruct((s // d, d * ATT_W), ACT) for d in fold_dils],
        scratch_shapes=[pltpu.VMEM((tm, ATT_W), F32)],
        compiler_params=_cp("parallel"))(*os_, *lses)


def assemble(parts, tabs, name):
    terms_of = [[t if isinstance(t, tuple) else (t, 1) for t in terms] for terms, _ in parts]
    s = terms_of[0][0][0].shape[0] * terms_of[0][0][1]
    tm = _tile(s, (256,))
    widths = [ts[0][0].shape[1] // ts[0][1] for ts in terms_of]
    flags = [f for _, f in parts]
    flat = [t for ts in terms_of for t in ts]

    def body(*refs):
        c_ref, sa_ref, sb_ref, o_ref = refs[len(flat):]
        c, sa, sb = c_ref[...], sa_ref[...], sb_ref[...]
        off = 0
        first = 0
        cache = {}
        for wd, ts, fl in zip(widths, terms_of, flags):
            t_refs = refs[first:first + len(ts)]
            first += len(ts)
            x = None
            for t_ref, (_, d) in zip(t_refs, ts):
                t = t_ref[...].astype(F32) if d == 1 else _unfold_rows(t_ref[...], d, cache)
                x = t if x is None else x + t
            for g in range(wd // LANES):
                cols = slice(g * LANES, (g + 1) * LANES)
                y = _rope_t(x[:, cols], c, sa, sb) if fl else x[:, cols]
                o_ref[:, off + g * LANES:off + (g + 1) * LANES] = y.astype(ACT)
            off += wd

    tot = sum(widths)
    return pl.pallas_call(
        body, grid=(s // tm,), name=name,
        in_specs=[_rowspec(tm // d, a.shape[1]) for a, d in flat] + [_rowspec(tm, LANES)] * 3,
        out_specs=_rowspec(tm, tot), out_shape=jax.ShapeDtypeStruct((s, tot), ACT),
        compiler_params=_cp("parallel"))(*[a for a, _ in flat], *tabs)


def _ln_stats(x):
    mu = jnp.mean(x, axis=-1, keepdims=True)
    xc = x - mu
    var = jnp.mean(xc * xc, axis=-1, keepdims=True)
    rstd = lax.rsqrt(var + LN_EPS)
    return xc * rstd, rstd


SUBLANES = 8


TAP_ROWS = 32


def _tap_sum(buf, cw_ref, offsets, tm, res_ref):
    for r0 in range(0, tm, TAP_ROWS):
        acc = None
        for ph in range(SUBLANES):
            taps = [j for j, off in enumerate(offsets) if off % SUBLANES == ph]
            if not taps:
                continue
            rows = TAP_ROWS if ph == 0 else TAP_ROWS + SUBLANES
            part = None
            for j in taps:
                term = cw_ref[j:j + 1, :] * buf[pl.ds(offsets[j] - ph + r0, rows), :]
                part = term if part is None else part + term
            part = part[ph:ph + TAP_ROWS]
            acc = part if acc is None else acc + part
        res_ref[pl.ds(r0, TAP_ROWS), :] = acc


def conv_fwd(ga, gb, cw, cb, lg, lb, name):
    s = ga.shape[0]
    tm = _tile(s, (256,))
    hb = tm // CONV_HALO

    def body(ga_ref, gb_ref, gah_ref, gbh_ref, cw_ref, cb_ref, lg_ref, lb_ref, c_ref, c1_ref, buf):
        i = pl.program_id(0)
        halo = gah_ref[...].astype(F32) * _sigmoid(gbh_ref[...].astype(F32))
        buf[0:CONV_HALO, :] = jnp.where(i > 0, halo, 0.0)
        buf[CONV_HALO:, :] = ga_ref[...].astype(F32) * _sigmoid(gb_ref[...].astype(F32))
        first = CONV_HALO - (CONV_WIDTH - 1)
        _tap_sum(buf, cw_ref, [first + j for j in range(CONV_WIDTH)], tm, c1_ref)
        acc = c1_ref[...] + cb_ref[...]
        c1_ref[...] = acc
        xh, _ = _ln_stats(acc)
        y = xh * lg_ref[...] + lb_ref[...]
        c_ref[...] = (y * _sigmoid(y)).astype(ACT)

    hspec = pl.BlockSpec((CONV_HALO, CONV_CH), lambda i: (jnp.maximum(i * hb - 1, 0), 0))
    vec = _fullspec((1, CONV_CH))
    spec = _rowspec(tm, CONV_CH)
    return pl.pallas_call(
        body, grid=(s // tm,), name=name,
        in_specs=[spec, spec, hspec, hspec, _fullspec((CONV_WIDTH, CONV_CH)), vec, vec, vec],
        out_specs=[spec, spec],
        out_shape=[jax.ShapeDtypeStruct((s, CONV_CH), ACT), jax.ShapeDtypeStruct((s, CONV_CH), F32)],
        scratch_shapes=[pltpu.VMEM((tm + CONV_HALO, CONV_CH), F32)],
        compiler_params=_cp("parallel"))(ga, gb, ga, gb, cw, cb, lg, lb)


def conv_bwd_ln(c1, dsrc, dcol, lg, lb, name):
    s = c1.shape[0]
    tm = _tile(s, (256,))

    def body(c1_ref, d_ref, lg_ref, lb_ref, o_ref, dg_ref, db_ref):
        @pl.when(pl.program_id(0) == 0)
        def _():
            dg_ref[...] = jnp.zeros_like(dg_ref)
            db_ref[...] = jnp.zeros_like(db_ref)

        xh, rstd = _ln_stats(c1_ref[...].astype(F32))
        y = xh * lg_ref[...] + lb_ref[...]
        sg = _sigmoid(y)
        dy = d_ref[...].astype(F32) * (sg * (1.0 + y * (1.0 - sg)))
        dg_ref[0:1, :] = dg_ref[0:1, :] + jnp.sum(dy * xh, axis=0, keepdims=True)
        db_ref[0:1, :] = db_ref[0:1, :] + jnp.sum(dy, axis=0, keepdims=True)
        dxh = dy * lg_ref[...]
        o_ref[...] = rstd * (dxh - jnp.mean(dxh, axis=-1, keepdims=True)
                             - xh * jnp.mean(dxh * xh, axis=-1, keepdims=True))

    vec = _fullspec((1, CONV_CH))
    acc = _fullspec((8, CONV_CH))
    return pl.pallas_call(
        body, grid=(s // tm,), name=name,
        in_specs=[_rowspec(tm, CONV_CH), _rowspec(tm, CONV_CH, dcol), vec, vec],
        out_specs=[_rowspec(tm, CONV_CH), acc, acc],
        out_shape=[jax.ShapeDtypeStruct((s, CONV_CH), F32)] + [jax.ShapeDtypeStruct((8, CONV_CH), F32)] * 2,
        compiler_params=_cp("arbitrary"))(c1, dsrc, lg, lb)


def conv_bwd_conv(dc1, ga, gb, cw, name):
    s = ga.shape[0]
    tm = _tile(s, (256,))
    hb = tm // CONV_HALO
    nt = s // tm
    last_h = s // CONV_HALO - 1
    first = CONV_HALO - (CONV_WIDTH - 1)


    def rows8(x):
        return jnp.sum(x.reshape(x.shape[0] // SUBLANES, SUBLANES, CONV_CH), axis=0)

    def body(d_ref, dn_ref, ga_ref, gb_ref, gah_ref, gbh_ref, cw_ref,
             dga_ref, dgb_ref, dw_ref, db_ref, dbuf, cbuf, sbuf):
        i = pl.program_id(0)

        @pl.when(i == 0)
        def _():
            dw_ref[...] = jnp.zeros_like(dw_ref)
            db_ref[...] = jnp.zeros_like(db_ref)

        d = d_ref[...]
        dbuf[0:tm, :] = d
        dbuf[tm:, :] = jnp.where(i < nt - 1, dn_ref[...], 0.0)
        halo = gah_ref[...].astype(F32) * _sigmoid(gbh_ref[...].astype(F32))
        cbuf[0:CONV_HALO, :] = jnp.where(i > 0, halo, 0.0)
        a = ga_ref[...].astype(F32)
        sg = _sigmoid(gb_ref[...].astype(F32))
        cbuf[CONV_HALO:, :] = a * sg
        for ph in range(SUBLANES):
            taps = [j for j in range(CONV_WIDTH) if (first + j) % SUBLANES == ph]
            if ph:
                sbuf[0:tm + CONV_HALO - SUBLANES, :] = cbuf[pl.ds(ph, tm + CONV_HALO - SUBLANES), :]
            src = sbuf if ph else cbuf
            for r0 in range(0, tm, TAP_ROWS):
                d_blk = dbuf[pl.ds(r0, TAP_ROWS), :]
                for j in taps:
                    tap = src[pl.ds(first + j - ph + r0, TAP_ROWS), :]
                    rows = slice(j * SUBLANES, (j + 1) * SUBLANES)
                    dw_ref[rows, :] = dw_ref[rows, :] + rows8(d_blk * tap)
        db_ref[...] = db_ref[...] + rows8(d)
        _tap_sum(dbuf, cw_ref, [CONV_WIDTH - 1 - j for j in range(CONV_WIDTH)], tm, sbuf)
        dc0 = sbuf[0:tm, :]
        dga_ref[...] = (dc0 * sg).astype(ACT)
        dgb_ref[...] = (dc0 * a * sg * (1.0 - sg)).astype(ACT)

    spec = _rowspec(tm, CONV_CH)
    hprev = pl.BlockSpec((CONV_HALO, CONV_CH), lambda i: (jnp.maximum(i * hb - 1, 0), 0))
    hnext = pl.BlockSpec((CONV_HALO, CONV_CH), lambda i: (jnp.minimum((i + 1) * hb, last_h), 0))
    return pl.pallas_call(
        body, grid=(nt,), name=name,
        in_specs=[spec, hnext, spec, spec, hprev, hprev, _fullspec((CONV_WIDTH, CONV_CH))],
        out_specs=[spec, spec, _fullspec((CONV_HALO * SUBLANES, CONV_CH)), _fullspec((SUBLANES, CONV_CH))],
        out_shape=[jax.ShapeDtypeStruct((s, CONV_CH), ACT)] * 2
        + [jax.ShapeDtypeStruct((CONV_HALO * SUBLANES, CONV_CH), F32), jax.ShapeDtypeStruct((SUBLANES, CONV_CH), F32)],
        scratch_shapes=[pltpu.VMEM((tm + CONV_HALO, CONV_CH), F32)] * 3,
        compiler_params=_cp("arbitrary"))(dc1, dc1, ga, gb, ga, gb, cw)


_GELU_K = math.sqrt(2.0 / math.pi)
_GELU_C = 0.044715


def _gelu(x):
    return 0.5 * x * (1.0 + jnp.tanh(_GELU_K * (x + _GELU_C * x * x * x)))


def _gelu_grad(x):
    t = jnp.tanh(_GELU_K * (x + _GELU_C * x * x * x))
    return 0.5 * (1.0 + t) + 0.5 * x * (1.0 - t * t) * _GELU_K * (1.0 + 3.0 * _GELU_C * x * x)


def _tril():
    qi = lax.broadcasted_iota(jnp.int32, (BLOCK, BLOCK), 0)
    kj = lax.broadcasted_iota(jnp.int32, (BLOCK, BLOCK), 1)
    return kj <= qi


GMLP_CHUNKS = 2


def _gmlp_weights(sw_ref, w_buf, wt_buf):
    tril = _tril()
    for g in range(D_GROUPS):
        w = jnp.where(tril, sw_ref[g], 0.0)
        w_buf[g] = w.astype(ACT)
        if wt_buf is not None:
            wt_buf[g] = w.T.astype(ACT)


def _gmlp_mix(w_buf, gn, sb_ref, m_buf):
    for c in range(GMLP_CHUNKS):
        rows = slice(c * BLOCK, (c + 1) * BLOCK)
        for g in range(D_GROUPS):
            cs = slice(g * HEAD_DIM, (g + 1) * HEAD_DIM)
            m_buf[rows, cs] = _dot(w_buf[g], gn[rows, cs], NN) + sb_ref[:, cs]


def gmlp_fwd(z, lg, lb, sw, sbx, name):
    s = z.shape[0]
    tm = GMLP_CHUNKS * BLOCK

    def body(z_ref, lg_ref, lb_ref, sw_ref, sb_ref, o_ref, w_buf, m_buf):
        @pl.when(pl.program_id(0) == 0)
        def _():
            _gmlp_weights(sw_ref, w_buf, None)

        zz = _gelu(z_ref[...].astype(F32))
        xh, _ = _ln_stats(zz[:, D_CH:])
        gn = (xh * lg_ref[...] + lb_ref[...]).astype(ACT)
        _gmlp_mix(w_buf, gn, sb_ref, m_buf)
        o_ref[...] = (zz[:, :D_CH] * m_buf[...]).astype(ACT)

    return pl.pallas_call(
        body, grid=(s // tm,), name=name,
        in_specs=[_rowspec(tm, 2 * D_CH), _fullspec((1, D_CH)), _fullspec((1, D_CH)),
                  _fullspec((D_GROUPS, BLOCK, BLOCK)), _fullspec((BLOCK, D_CH))],
        out_specs=_rowspec(tm, D_CH), out_shape=jax.ShapeDtypeStruct((s, D_CH), ACT),
        scratch_shapes=[pltpu.VMEM((D_GROUPS, BLOCK, BLOCK), ACT), pltpu.VMEM((tm, D_CH), F32)],
        compiler_params=_cp("arbitrary"))(z, lg, lb, sw, sbx)


def gmlp_bwd(z, dsrc, dcol, lg, lb, sw, sbx, name):
    s = z.shape[0]
    tm = GMLP_CHUNKS * BLOCK

    def body(z_ref, d_ref, lg_ref, lb_ref, sw_ref, sb_ref, dz_ref, dw_ref, dsb_ref, dg_ref, db_ref,
             w_buf, wt_buf, m_buf, dgn_buf):
        @pl.when(pl.program_id(0) == 0)
        def _():
            _gmlp_weights(sw_ref, w_buf, wt_buf)
            dw_ref[...] = jnp.zeros_like(dw_ref)
            dsb_ref[...] = jnp.zeros_like(dsb_ref)
            dg_ref[...] = jnp.zeros_like(dg_ref)
            db_ref[...] = jnp.zeros_like(db_ref)

        zr = z_ref[...].astype(F32)
        zz = _gelu(zr)
        u = zz[:, :D_CH]
        xh, rstd = _ln_stats(zz[:, D_CH:])
        gn = (xh * lg_ref[...] + lb_ref[...]).astype(ACT)
        dd = d_ref[...].astype(F32)
        _gmlp_mix(w_buf, gn, sb_ref, m_buf)
        dz_ref[:, :D_CH] = (dd * m_buf[...] * _gelu_grad(zr[:, :D_CH])).astype(ACT)
        dmix = dd * u
        dmix_a = dmix.astype(ACT)
        dsb = dmix[0:BLOCK]
        for c in range(1, GMLP_CHUNKS):
            dsb = dsb + dmix[c * BLOCK:(c + 1) * BLOCK]
        dsb_ref[...] = dsb_ref[...] + dsb
        tril = _tril()
        for g in range(D_GROUPS):
            cs = slice(g * HEAD_DIM, (g + 1) * HEAD_DIM)
            dw = None
            for c in range(GMLP_CHUNKS):
                rows = slice(c * BLOCK, (c + 1) * BLOCK)
                t = _dot(dmix_a[rows, cs], gn[rows, cs], NT)
                dw = t if dw is None else dw + t
                dgn_buf[rows, cs] = _dot(wt_buf[g], dmix_a[rows, cs], NN)
            dw_ref[g] = dw_ref[g] + jnp.where(tril, dw, 0.0)
        dgn = dgn_buf[...]
        dg_ref[0:1, :] = dg_ref[0:1, :] + jnp.sum(dgn * xh, axis=0, keepdims=True)
        db_ref[0:1, :] = db_ref[0:1, :] + jnp.sum(dgn, axis=0, keepdims=True)
        dxh = dgn * lg_ref[...]
        dgate = rstd * (dxh - jnp.mean(dxh, axis=-1, keepdims=True)
                        - xh * jnp.mean(dxh * xh, axis=-1, keepdims=True))
        dz_ref[:, D_CH:] = (dgate * _gelu_grad(zr[:, D_CH:])).astype(ACT)

    vec = _fullspec((1, D_CH))
    acc = _fullspec((8, D_CH))
    wshape = (D_GROUPS, BLOCK, BLOCK)
    return pl.pallas_call(
        body, grid=(s // tm,), name=name,
        in_specs=[_rowspec(tm, 2 * D_CH), _rowspec(tm, D_CH, dcol), vec, vec, _fullspec(wshape),
                  _fullspec((BLOCK, D_CH))],
        out_specs=[_rowspec(tm, 2 * D_CH), _fullspec(wshape), _fullspec((BLOCK, D_CH)), acc, acc],
        out_shape=[jax.ShapeDtypeStruct((s, 2 * D_CH), ACT), jax.ShapeDtypeStruct(wshape, F32),
                   jax.ShapeDtypeStruct((BLOCK, D_CH), F32),
                   jax.ShapeDtypeStruct((8, D_CH), F32), jax.ShapeDtypeStruct((8, D_CH), F32)],
        scratch_shapes=[pltpu.VMEM(wshape, ACT), pltpu.VMEM(wshape, ACT),
                        pltpu.VMEM((tm, D_CH), F32), pltpu.VMEM((tm, D_CH), F32)],
        compiler_params=_cp("arbitrary"))(z, dsrc, lg, lb, sw, sbx)


def _rms_bwd(dn, x, g):
    r = lax.rsqrt(jnp.mean(x * x, axis=-1, keepdims=True) + RMS_EPS)
    u = dn * g
    dx = r * u - x * (r * r * r) * jnp.mean(x * u, axis=-1, keepdims=True)
    return dx, dn * x * r


def final_loss(h, g, target, name):
    s = h.shape[0]
    tm = _tile(s, (256,))

    def body(h_ref, g_ref, t_ref, loss_ref, dh_ref, dg_ref):
        @pl.when(pl.program_id(0) == 0)
        def _():
            loss_ref[...] = jnp.zeros_like(loss_ref)
            dg_ref[...] = jnp.zeros_like(dg_ref)

        x = h_ref[...]
        r = lax.rsqrt(jnp.mean(x * x, axis=-1, keepdims=True) + RMS_EPS)
        diff = x * r * g_ref[...] - t_ref[...]
        part = jnp.sum(jnp.sum(diff * diff, axis=-1, keepdims=True), axis=0, keepdims=True)
        loss_ref[...] = loss_ref[...] + part * (0.5 / D_MODEL)
        dx, dgt = _rms_bwd(diff * (1.0 / D_MODEL), x, g_ref[...])
        dh_ref[...] = dx
        dg_ref[0:1, :] = dg_ref[0:1, :] + jnp.sum(dgt, axis=0, keepdims=True)

    spec = _rowspec(tm, D_MODEL)
    return pl.pallas_call(
        body, grid=(s // tm,), name=name,
        in_specs=[spec, _fullspec((1, D_MODEL)), spec],
        out_specs=[_fullspec((8, LANES)), spec, _fullspec((8, D_MODEL))],
        out_shape=[jax.ShapeDtypeStruct((8, LANES), F32), jax.ShapeDtypeStruct((s, D_MODEL), F32),
                   jax.ShapeDtypeStruct((8, D_MODEL), F32)],
        compiler_params=_cp("arbitrary"))(h, g, target)


def mm_nt(dy, w, name, fold_dils=(), fold_cols=0):
    s, n = dy.shape
    k = w.shape[0]
    tm = _tile(s, (256,) if fold_dils else (512,))
    tk = k if fold_dils else _tile(k, (512,))

    def body(d_ref, w_ref, o_ref, *f_refs):
        res = _dot(d_ref[...].astype(ACT), w_ref[...], NT).astype(ACT)
        o_ref[...] = res
        cache = {}
        for f_ref, d in zip(f_refs, fold_dils):
            f_ref[...] = _fold_rows(res[:, :fold_cols], d, cache)

    out = pl.pallas_call(
        body, grid=(k // tk, s // tm), name=name,
        in_specs=[pl.BlockSpec((tm, n), lambda j, i: (i, 0)), pl.BlockSpec((tk, n), lambda j, i: (j, 0))],
        out_specs=[pl.BlockSpec((tm, tk), lambda j, i: (i, j))]
        + [pl.BlockSpec((tm // d, d * fold_cols), lambda j, i: (i, 0)) for d in fold_dils],
        out_shape=[jax.ShapeDtypeStruct((s, k), ACT)]
        + [jax.ShapeDtypeStruct((s // d, d * fold_cols), ACT) for d in fold_dils],
        compiler_params=_cp("parallel", "parallel"))(dy, w)
    return out if fold_dils else out[0]


def ffn_down_bwd(dh, wd, dact_dgate, dact_dup, name):
    s = dh.shape[0]
    f = wd.shape[0]
    tm = _tile(s, (512,))
    tf = _tile(f, (1408, 512, 256, 128))

    def body(d_ref, w_ref, g_ref, u_ref, dg_ref, du_ref):
        dact = _dot(d_ref[...].astype(ACT), w_ref[...], NT)
        dg_ref[...] = (dact * g_ref[...].astype(F32)).astype(ACT)
        du_ref[...] = (dact * u_ref[...].astype(F32)).astype(ACT)

    tile = pl.BlockSpec((tm, tf), lambda j, i: (i, j))
    return pl.pallas_call(
        body, grid=(f // tf, s // tm), name=name,
        in_specs=[pl.BlockSpec((tm, D_MODEL), lambda j, i: (i, 0)),
                  pl.BlockSpec((tf, D_MODEL), lambda j, i: (j, 0)), tile, tile],
        out_specs=[tile, tile], out_shape=[jax.ShapeDtypeStruct((s, f), ACT)] * 2,
        compiler_params=_cp("parallel", "parallel"))(dh, wd, dact_dgate, dact_dup)


def mm_nt_rms(parts, h, g, dh, name):
    s = h.shape[0]
    tm = _tile(s, (256,))
    np_ = len(parts)

    def body(*refs):
        d_refs = refs[:np_]
        w_refs = refs[np_:2 * np_]
        h_ref, g_ref, dh_ref, o_ref, dg_ref = refs[2 * np_:]

        @pl.when(pl.program_id(0) == 0)
        def _():
            dg_ref[...] = jnp.zeros_like(dg_ref)

        dn = None
        for d_ref, w_ref in zip(d_refs, w_refs):
            t = _dot(d_ref[...], w_ref[...], NT)
            dn = t if dn is None else dn + t
        dx, dgt = _rms_bwd(dn, h_ref[...], g_ref[...])
        o_ref[...] = dh_ref[...] + dx
        dg_ref[0:1, :] = dg_ref[0:1, :] + jnp.sum(dgt, axis=0, keepdims=True)

    spec = _rowspec(tm, D_MODEL)
    return pl.pallas_call(
        body, grid=(s // tm,), name=name,
        in_specs=[_rowspec(tm, d.shape[1]) for d, _ in parts] + [_fullspec(w.shape) for _, w in parts]
        + [spec, _fullspec((1, D_MODEL)), spec],
        out_specs=[spec, _fullspec((8, D_MODEL))],
        out_shape=[jax.ShapeDtypeStruct((s, D_MODEL), F32), jax.ShapeDtypeStruct((8, D_MODEL), F32)],
        compiler_params=_cp("arbitrary"))(*[d for d, _ in parts], *[w for _, w in parts], h, g, dh)


def mm_tn(a, b, name):
    s, k = a.shape
    n = b.shape[1]
    tk = _tile(k, (512, 1408, 256, 128))
    tn = _tile(n, (1408, 1280, 1024, 896, 512, 256, 128))
    ts = _tile(s, (2048, 512) if b.dtype == ACT else (1024, 512))

    def body(a_ref, b_ref, o_ref):
        t = _dot(a_ref[...].astype(ACT), b_ref[...].astype(ACT), TN)

        @pl.when(pl.program_id(2) == 0)
        def _():
            o_ref[...] = t

        @pl.when(pl.program_id(2) > 0)
        def _():
            o_ref[...] = o_ref[...] + t

    return pl.pallas_call(
        body, grid=(k // tk, n // tn, s // ts), name=name,
        in_specs=[pl.BlockSpec((ts, tk), lambda i, j, t: (t, i)), pl.BlockSpec((ts, tn), lambda i, j, t: (t, j))],
        out_specs=pl.BlockSpec((tk, tn), lambda i, j, t: (i, j)),
        out_shape=jax.ShapeDtypeStruct((k, n), F32),
        compiler_params=_cp("parallel", "parallel", "arbitrary"))(a, b)


def _adamw_math(w, g, m, v):
    m = ADAM_B1 * m + (1.0 - ADAM_B1) * g
    v = ADAM_B2 * v + (1.0 - ADAM_B2) * (g * g)
    m_hat = m / (1.0 - ADAM_B1 ** ADAM_STEP)
    v_hat = v / (1.0 - ADAM_B2 ** ADAM_STEP)
    delta = -ADAM_LR * (m_hat / (jnp.sqrt(v_hat) + ADAM_EPS) + ADAM_WD * w)
    return delta, m, v


def sum_adamw(parts, w, m, v, layer, others, name):
    nl, r, c = w.shape
    tr = _tile(r, (256, 128, 64, 32, 16, 8))

    def body(p_ref, w_ref, m_ref, v_ref, *rest):
        g_ref, d_ref, mo_ref, vo_ref = rest[-4:]
        g = p_ref[0].astype(F32)
        for i in range(1, N_DEV):
            g = g + p_ref[i].astype(F32)
        d, mm, vv = _adamw_math(w_ref[...], g, m_ref[...], v_ref[...])
        g_ref[...] = g
        d_ref[...] = d
        mo_ref[...] = mm
        vo_ref[...] = vv

    spec = pl.BlockSpec((None, tr, c), lambda i: (layer, i, 0))
    in_specs = [pl.BlockSpec((N_DEV, tr, c), lambda i: (0, i, 0))] + [spec] * 3
    args = [parts, w, m, v]
    aliases = {}
    if others is not None:
        in_specs += [pl.BlockSpec(memory_space=pl.ANY)] * 4
        args += list(others)
        aliases = {4 + j: j for j in range(4)}
    return pl.pallas_call(
        body, grid=(r // tr,), name=name, in_specs=in_specs, out_specs=[spec] * 4,
        out_shape=[jax.ShapeDtypeStruct((nl, r, c), F32)] * 4, input_output_aliases=aliases,
        compiler_params=_cp("parallel"))(*args)


def cast_layers(items, name):
    n = len(items)

    def body(*refs):
        for i in range(n):
            refs[n + i][...] = refs[i][...].astype(ACT)

    return pl.pallas_call(
        body, grid=(1,), name=name,
        in_specs=[pl.BlockSpec((None,) + w.shape[1:], lambda i, l=l: (l, 0, 0)) for w, l in items],
        out_specs=[_fullspec(w.shape[1:]) for w, _ in items],
        out_shape=[jax.ShapeDtypeStruct(w.shape[1:], ACT) for w, _ in items],
        compiler_params=_cp("arbitrary"))(*[w for w, _ in items])


def adamw_small(ws, gs, ms, vs, name):
    k = len(ws)

    def body(*refs):
        for i in range(k):
            w_ref, g_ref, m_ref, v_ref = (refs[j * k + i] for j in range(4))
            d, mm, vv = _adamw_math(w_ref[...], g_ref[...], m_ref[...], v_ref[...])
            refs[4 * k + i][...] = d
            refs[5 * k + i][...] = mm
            refs[6 * k + i][...] = vv

    shapes = [jax.ShapeDtypeStruct(w.shape, F32) for w in ws]
    specs = [_fullspec(w.shape) for w in ws]
    out = pl.pallas_call(
        body, grid=(1,), name=name, in_specs=specs * 4, out_specs=specs * 3, out_shape=shapes * 3,
        compiler_params=_cp("arbitrary"))(*ws, *gs, *ms, *vs)
    return out[:k], out[k:2 * k], out[2 * k:]


def sum_slots(x, out_dtype, name):
    g, r, c = x.shape
    tr = r if x.size * x.dtype.itemsize <= SMALL_BLOCK_BYTES else _tile(r, (256, 128, 64, 32, 16, 8))

    def body(x_ref, o_ref):
        acc = x_ref[0].astype(F32)
        for i in range(1, g):
            acc = acc + x_ref[i].astype(F32)
        o_ref[...] = acc.astype(o_ref.dtype)

    return pl.pallas_call(
        body, grid=(r // tr,), name=name,
        in_specs=[pl.BlockSpec((g, tr, c), lambda i: (0, i, 0))], out_specs=_rowspec(tr, c),
        out_shape=jax.ShapeDtypeStruct((r, c), out_dtype),
        compiler_params=_cp("parallel"))(x)


HBM_SPEC = pl.BlockSpec(memory_space=pltpu.HBM)
SEM_SPEC = pl.BlockSpec(memory_space=pltpu.SEMAPHORE)
DATAFLOW = pltpu.SideEffectType.DATAFLOW_SIDE_EFFECTING


def _my_rank():
    return 4 * lax.axis_index("x") + 2 * lax.axis_index("y") + lax.axis_index("c")


def _exchange_copies(x_refs, land_refs, send, recv, a2a):
    pos = [lax.axis_index(a) for a in AXES]
    me = _my_rank()
    copies = []
    for x_ref, land_ref, s_ref, r_ref in zip(x_refs, land_refs, send, recv):
        for k in range(N_DEV - 1):
            bits = ((k + 1) >> 2 & 1, (k + 1) >> 1 & 1, (k + 1) & 1)
            peer = tuple(1 - p if b else p for p, b in zip(pos, bits))
            prank = 4 * peer[0] + 2 * peer[1] + peer[2]
            copies.append(pltpu.make_async_remote_copy(
                src_ref=x_ref.at[prank] if a2a else x_ref, dst_ref=land_ref.at[me],
                send_sem=s_ref.at[k], recv_sem=r_ref.at[k], device_id=peer, device_id_type=MESH))
    return copies


def exchange_start(xs, a2a, name, after=None):
    n = len(xs)
    me = _my_rank()
    lands = []
    for x in xs:
        own = lax.dynamic_index_in_dim(x, me, 0, keepdims=True) if a2a else x[None]
        shape = x.shape if a2a else (N_DEV,) + x.shape
        lands.append(lax.dynamic_update_slice(lax.empty(shape, x.dtype), own, (me,) + (0,) * (len(shape) - 1)))

    def body(*refs):
        x_refs, land_refs = refs[:n], refs[n:2 * n]
        outs = refs[len(refs) - 4 * n - 1:]
        for cp in _exchange_copies(x_refs, land_refs, outs[:n], outs[n:2 * n], a2a):
            cp.start()
        token = outs[4 * n]
        token[...] = jnp.zeros_like(token)

    sems = [pltpu.SemaphoreType.DMA((N_DEV - 1,))] * n
    out = pl.pallas_call(
        body, name=name,
        out_shape=tuple(sems + sems + [pltpu.HBM(x.shape, x.dtype) for x in xs]
                        + [pltpu.HBM(l.shape, l.dtype) for l in lands] + [jax.ShapeDtypeStruct((8, LANES), F32)]),
        in_specs=[HBM_SPEC] * (2 * n) + ([] if after is None else [pl.BlockSpec(memory_space=pl.ANY)]),
        out_specs=tuple([SEM_SPEC] * (2 * n) + [HBM_SPEC] * (2 * n) + [pl.BlockSpec(memory_space=pltpu.VMEM)]),
        input_output_aliases={i: 2 * n + i for i in range(2 * n)},
        compiler_params=pltpu.CompilerParams(has_side_effects=DATAFLOW),
    )(*[pltpu.with_memory_space_constraint(a, pltpu.HBM) for a in list(xs) + lands], *([] if after is None else [after]))
    return (out[:n], out[n:2 * n], out[2 * n:3 * n], out[3 * n:4 * n]), out[4 * n]


def exchange_wait(handles, after, a2a, name):
    send, recv, x_thru, land_thru = handles
    n = len(x_thru)

    def body(*refs):
        x_refs, land_refs = refs[:n], refs[n:2 * n]
        s_refs, r_refs = refs[2 * n:3 * n], refs[3 * n:4 * n]
        for cp in _exchange_copies(x_refs, land_refs, s_refs, r_refs, a2a):
            cp.wait_send()
            cp.wait_recv()

    out = pl.pallas_call(
        body, name=name,
        out_shape=tuple([pltpu.HBM(a.shape, a.dtype) for a in list(x_thru) + list(land_thru)]),
        in_specs=[HBM_SPEC] * (2 * n) + [SEM_SPEC] * (2 * n) + [pl.BlockSpec(memory_space=pl.ANY)],
        out_specs=tuple([HBM_SPEC] * (2 * n)),
        input_output_aliases={i: i for i in range(2 * n)},
        compiler_params=pltpu.CompilerParams(has_side_effects=DATAFLOW),
    )(*x_thru, *land_thru, *send, *recv, after)
    return out[n:2 * n]


def _local_step(x, target, weight, emit, P):
    s = x.shape[0]
    tabs = _rope_tables(s)
    sinkb = jnp.broadcast_to(P["ev_sinks"].reshape(N_HEADS, 1), (N_HEADS, LANES))
    sbx = jnp.repeat(P["od_spatial_b"].reshape(D_GROUPS, BLOCK).T, HEAD_DIM, axis=1)
    sw = P["od_spatial_w"].reshape(D_GROUPS, BLOCK, BLOCK)
    fg = P["ffn_norm_g"]
    latest = [None]

    def out(name, layer, grad):
        tok = emit(name, layer, grad)
        if tok is not None:
            latest[0] = tok

    def dep(a):
        return a if latest[0] is None else a + latest[0][0:1, 0:1]

    n0 = rmsnorm(x, P["ev_norm_g"], "rms_in")
    qk_e, v_e, ga, gb = inproj(n0, weight("ev_w_in", 0, n0), tabs, ATT_W + 128,
                               (ATT_W + 128, 128, CONV_CH, CONV_CH), (), "ev_inproj")
    a_e, lse_e = attn_fwd(qk_e, v_e, d=1, hkv=A_KV_HEADS, max_dist=BLOCK - 1, sink=sinkb, out_dtype=ACT,
                          name="ev_attn")
    c_act, c1 = conv_fwd(ga, gb, P["ev_conv_w"], P["ev_conv_b"], P["ev_conv_ln_g"], P["ev_conv_ln_b"], "ev_conv")
    h1, n1 = mm_res([a_e, c_act], weight("ev_w_out", 0, c_act), x, fg[0:1], "ev_outproj")
    gate0, up0, act0 = ffn_up(n1, weight("ffn_w_gate", 0, n1), weight("ffn_w_up", 0, n1), "ffn0_up")
    h2, n2 = mm_res([act0], weight("ffn_w_down", 0, act0), h1, P["od_norm_g"], "ffn0_down")
    fold_dils = tuple(dil for _, dil in DILATED if dil > 1)
    qk_o, v_o, z, *pre = inproj(n2, weight("od_w_in", 0, n2), tabs, 2 * ATT_W, (2 * ATT_W, ATT_W, 2 * D_CH),
                                fold_dils, "od_inproj")
    folded = {dil: (pre[2 * i], pre[2 * i + 1]) for i, dil in enumerate(fold_dils)}
    outs, lses = [], []
    for window, dil in DILATED:
        assert window // dil == BLOCK
        o_r, l_r = attn_fwd(qk_o, v_o, d=dil, hkv=N_HEADS, max_dist=BLOCK, sink=None, out_dtype=ACT,
                            name="od_attn_d%d" % dil, folded=folded.get(dil), keep_folded=True)
        outs.append(o_r)
        lses.append(l_r)
    dils = [dil for _, dil in DILATED]
    c_out, lse_o, *c_folded = combine_fwd(outs, lses, dils, "od_combine")
    d_out = gmlp_fwd(z, P["od_sgu_ln_g"], P["od_sgu_ln_b"], sw, sbx, "od_gmlp")
    h3, n3 = mm_res([c_out, d_out], weight("od_w_out", 0, d_out), h2, fg[1:2], "od_outproj")
    gate1, up1, act1 = ffn_up(n3, weight("ffn_w_gate", 1, n3), weight("ffn_w_up", 1, n3), "ffn1_up")
    h4, _ = mm_res([act1], weight("ffn_w_down", 1, act1), h3, None, "ffn1_down")
    loss_part, dh4, dg_final = final_loss(h4, P["final_norm_g"], target, "loss_head")

    def ffn_bwd(layer, dh_out, h_in, n_in, gate, up, act):
        wg, wu, wd = (weight(n, layer, dh_out) for n in ("ffn_w_gate", "ffn_w_up", "ffn_w_down"))
        tag = "ffn%d" % layer
        dgate, dup = ffn_down_bwd(dh_out, wd, gate, up, tag + "_down_bwd")
        g_wd = mm_tn(act, dh_out, tag + "_dwd")
        dh_in, dgn = mm_nt_rms([(dgate, wg), (dup, wu)], h_in, dep(fg[layer:layer + 1]), dh_out, tag + "_up_bwd")
        out("ffn_w_down", layer, g_wd)
        out("ffn_w_gate", layer, mm_tn(n_in, dgate, tag + "_dwg"))
        out("ffn_w_up", layer, mm_tn(n_in, dup, tag + "_dwu"))
        return dh_in, dgn[0:1]

    dh3, dgn_f1 = ffn_bwd(1, dh4, h3, n3, gate1, up1, act1)

    dcd, *dc_folded = mm_nt(dh3, weight("od_w_out", 0, dh3), "od_outproj_bwd", fold_dils, ATT_W)
    do_o = {dil: (dc_folded[i], c_folded[i]) for i, dil in enumerate(fold_dils)}
    dz, g_sw, g_sbx, g_slg, g_slb = gmlp_bwd(z, dcd, 1, dep(P["od_sgu_ln_g"]), P["od_sgu_ln_b"], sw, sbx,
                                             "od_gmlp_bwd")
    g_sb = jnp.sum(g_sbx.reshape(BLOCK, D_GROUPS, HEAD_DIM), axis=-1).T
    out("od_spatial", 0, jnp.concatenate([g_sw.reshape(D_GROUPS * BLOCK, BLOCK), g_sb], axis=0))
    out("od_w_out", 0, jnp.concatenate([mm_tn(c_out, dh3, "od_dwo_c"), mm_tn(d_out, dh3, "od_dwo_d")], axis=0))
    dqkv = [attn_bwd(qk_o, v_o, dcd if dil == 1 else dcd[:, :ATT_W], 0, c_out, lse_o, d=dil, hkv=N_HEADS,
                     max_dist=BLOCK, sink=None, out_dtype=ACT, name="od_attn_bwd_d%d" % dil,
                     folded=folded.get(dil), folded_do_o=do_o.get(dil), keep_folded=True)
            for window, dil in DILATED]
    dproj_o = assemble([([(b[j], dil) for b, dil in zip(dqkv, dils)], j < 2) for j in range(3)] + [([dz], False)],
                       tabs, "od_dproj")
    dh2, dgn_od = mm_nt_rms([(dproj_o, weight("od_w_in", 0, dproj_o))], h2, dep(P["od_norm_g"]), dh3,
                            "od_inproj_bwd")
    out("od_w_in", 0, mm_tn(n2, dproj_o, "od_dwi"))

    dh1, dgn_f0 = ffn_bwd(0, dh2, h1, n1, gate0, up0, act0)

    dac = mm_nt(dh1, weight("ev_w_out", 0, dh1), "ev_outproj_bwd")
    dc1, g_clg, g_clb = conv_bwd_ln(c1, dac, 1, dep(P["ev_conv_ln_g"]), P["ev_conv_ln_b"], "ev_conv_bwd_ln")
    out("ev_w_out", 0, jnp.concatenate([mm_tn(a_e, dh1, "ev_dwo_a"), mm_tn(c_act, dh1, "ev_dwo_c")], axis=0))
    dga, dgb, g_cw, g_cb = conv_bwd_conv(dc1, ga, gb, P["ev_conv_w"], "ev_conv_bwd")
    dq, dk, dv, dsink = attn_bwd(qk_e, v_e, dac, 0, a_e, lse_e, d=1, hkv=A_KV_HEADS, max_dist=BLOCK - 1,
                                 sink=sinkb, out_dtype=F32, name="ev_attn_bwd")
    dproj_e = assemble([([dq], True), ([dk], True), ([dv], False), ([dga], False), ([dgb], False)], tabs,
                       "ev_dproj")
    out("ev_w_in", 0, mm_tn(n0, dproj_e, "ev_dwi"))
    dx, dgn_ev = mm_nt_rms([(dproj_e, weight("ev_w_in", 0, dproj_e))], x, dep(P["ev_norm_g"]), dh1,
                           "ev_inproj_bwd")

    small = {
        "ev_norm_g": dgn_ev[0:1],
        "ev_sinks": dsink[:, 0:1].reshape(1, N_HEADS),
        "ev_conv_w": jnp.sum(g_cw.reshape(CONV_HALO, SUBLANES, CONV_CH), axis=1)[:CONV_WIDTH],
        "ev_conv_b": jnp.sum(g_cb, axis=0, keepdims=True),
        "ev_conv_ln_g": g_clg[0:1],
        "ev_conv_ln_b": g_clb[0:1],
        "od_norm_g": dgn_od[0:1],
        "od_sgu_ln_g": g_slg[0:1],
        "od_sgu_ln_b": g_slb[0:1],
        "od_spatial_w": g_sw.reshape(D_GROUPS * BLOCK, BLOCK),
        "od_spatial_b": g_sb,
        "ffn_norm_g": jnp.concatenate([dgn_f0, dgn_f1], axis=0),
        "final_norm_g": dg_final[0:1],
    }
    return loss_part, dx, small


BIG = ("ev_w_in", "ev_w_out", "od_w_in", "od_w_out", "ffn_w_gate", "ffn_w_up", "ffn_w_down")
COL_SHARDED = ("ev_w_in", "od_w_in", "ffn_w_gate", "ffn_w_up")
GATHER_GROUPS = (
    (("ev_w_in", 0),),
    (("ev_w_out", 0),),
    (("ffn_w_gate", 0), ("ffn_w_up", 0)),
    (("ffn_w_down", 0),),
    (("od_w_in", 0),),
    (("od_w_out", 0),),
    (("ffn_w_gate", 1), ("ffn_w_up", 1)),
    (("ffn_w_down", 1),),
)
GATHER_EARLY = 4
GATHER_LATE_AT = 2
REDUCE_GROUPS = (
    (("ffn_w_down", 1), ("ffn_w_gate", 1), ("ffn_w_up", 1)),
    (("od_w_out", 0),),
    (("od_w_in", 0),),
    (("ffn_w_down", 0), ("ffn_w_gate", 0), ("ffn_w_up", 0)),
    (("ev_w_out", 0),),
    (("ev_w_in", 0),),
)


def _unshard(name, g):
    if name in COL_SHARDED:
        return jnp.moveaxis(g, 0, 1).reshape(g.shape[1], N_DEV * g.shape[2])
    return g.reshape(N_DEV * g.shape[1], g.shape[2])


def _shard_slots(name, full):
    r, c = full.shape
    if name in COL_SHARDED:
        return jnp.moveaxis(full.reshape(r, N_DEV, c // N_DEV), 1, 0)
    return full.reshape(N_DEV, r // N_DEV, c)


def kernel(x, ev_norm_g, ev_w_in, ev_sinks, ev_conv_w, ev_conv_b, ev_conv_ln_g, ev_conv_ln_b, ev_w_out, od_norm_g, od_w_in, od_sgu_ln_g, od_sgu_ln_b, od_spatial_w, od_spatial_b, od_w_out, ffn_norm_g, ffn_w_gate, ffn_w_up, ffn_w_down, final_norm_g, loss_target, m_ev_norm_g, m_ev_w_in, m_ev_sinks, m_ev_conv_w, m_ev_conv_b, m_ev_conv_ln_g, m_ev_conv_ln_b, m_ev_w_out, m_od_norm_g, m_od_w_in, m_od_sgu_ln_g, m_od_sgu_ln_b, m_od_spatial_w, m_od_spatial_b, m_od_w_out, m_ffn_norm_g, m_ffn_w_gate, m_ffn_w_up, m_ffn_w_down, m_final_norm_g, v_ev_norm_g, v_ev_w_in, v_ev_sinks, v_ev_conv_w, v_ev_conv_b, v_ev_conv_ln_g, v_ev_conv_ln_b, v_ev_w_out, v_od_norm_g, v_od_w_in, v_od_sgu_ln_g, v_od_sgu_ln_b, v_od_spatial_w, v_od_spatial_b, v_od_w_out, v_ffn_norm_g, v_ffn_w_gate, v_ffn_w_up, v_ffn_w_down, v_final_norm_g):
    names = ["ev_norm_g", "ev_w_in", "ev_sinks", "ev_conv_w", "ev_conv_b", "ev_conv_ln_g", "ev_conv_ln_b", "ev_w_out",
             "od_norm_g", "od_w_in", "od_sgu_ln_g", "od_sgu_ln_b", "od_spatial_w", "od_spatial_b", "od_w_out",
             "ffn_norm_g", "ffn_w_gate", "ffn_w_up", "ffn_w_down", "final_norm_g"]
    wts = dict(zip(names, [ev_norm_g, ev_w_in, ev_sinks, ev_conv_w, ev_conv_b, ev_conv_ln_g, ev_conv_ln_b, ev_w_out,
                           od_norm_g, od_w_in, od_sgu_ln_g, od_sgu_ln_b, od_spatial_w, od_spatial_b, od_w_out,
                           ffn_norm_g, ffn_w_gate, ffn_w_up, ffn_w_down, final_norm_g]))
    mom = dict(zip(names, [m_ev_norm_g, m_ev_w_in, m_ev_sinks, m_ev_conv_w, m_ev_conv_b, m_ev_conv_ln_g, m_ev_conv_ln_b,
                           m_ev_w_out, m_od_norm_g, m_od_w_in, m_od_sgu_ln_g, m_od_sgu_ln_b, m_od_spatial_w,
                           m_od_spatial_b, m_od_w_out, m_ffn_norm_g, m_ffn_w_gate, m_ffn_w_up, m_ffn_w_down,
                           m_final_norm_g]))
    vel = dict(zip(names, [v_ev_norm_g, v_ev_w_in, v_ev_sinks, v_ev_conv_w, v_ev_conv_b, v_ev_conv_ln_g, v_ev_conv_ln_b,
                           v_ev_w_out, v_od_norm_g, v_od_w_in, v_od_sgu_ln_g, v_od_sgu_ln_b, v_od_spatial_w,
                           v_od_spatial_b, v_od_w_out, v_ffn_norm_g, v_ffn_w_gate, v_ffn_w_up, v_ffn_w_down,
                           v_final_norm_g]))
    me = _my_rank()

    sp = jnp.zeros((40, LANES), F32)
    sp = sp.at[0:CONV_WIDTH, 0:64].set(ev_conv_w[0])
    sp = sp.at[32, :].set(od_norm_g[0])
    sp = sp.at[33, 0:64].set(od_sgu_ln_g[0])
    sp = sp.at[34, 0:64].set(od_sgu_ln_b[0])

    early = [k for grp in GATHER_GROUPS[:GATHER_EARLY] for k in grp]
    late = [k for grp in GATHER_GROUPS[GATHER_EARLY:] for k in grp]
    early_act = dict(zip(early, cast_layers([(wts[n], l) for n, l in early], "cast_early")))
    late_act = dict(zip(late, cast_layers([(wts[n], l) for n, l in late], "cast_late")))
    ag_early, ag_token = exchange_start([sp] + [early_act[k] for k in early], False, "ag_start")
    ag_late = []
    full_w = {}
    P = {
        "ev_norm_g": ev_norm_g + ag_token[0:1, 0:1], "ev_sinks": ev_sinks, "ev_conv_b": ev_conv_b,
        "ev_conv_ln_g": ev_conv_ln_g, "ev_conv_ln_b": ev_conv_ln_b,
        "od_spatial_w": od_spatial_w, "od_spatial_b": od_spatial_b, "ffn_norm_g": ffn_norm_g,
        "final_norm_g": final_norm_g.reshape(1, D_MODEL),
    }

    def weight(name, layer, after):
        if (name, layer) not in full_w:
            gi = [i for i, grp in enumerate(GATHER_GROUPS) if (name, layer) in grp][0]
            if gi < GATHER_EARLY:
                idx = [1 + early.index(k) for k in GATHER_GROUPS[gi]]
                handles = ag_early
            else:
                idx = [late.index(k) for k in GATHER_GROUPS[gi]]
                handles = ag_late[0]
            if gi == 0:
                idx = [0] + idx
            lands = exchange_wait(tuple([h[i] for i in idx] for h in handles), after, False, "ag_wait%d" % gi)
            if gi == 0:
                spg, lands = lands[0], lands[1:]
                P["ev_conv_w"] = jnp.moveaxis(spg[:, 0:CONV_WIDTH, 0:64], 0, 1).reshape(CONV_WIDTH, CONV_CH)
                P["od_norm_g"] = spg[:, 32, :].reshape(1, D_MODEL)
                P["od_sgu_ln_g"] = spg[:, 33, 0:64].reshape(1, D_CH)
                P["od_sgu_ln_b"] = spg[:, 34, 0:64].reshape(1, D_CH)
            if gi == GATHER_LATE_AT:
                ag_late.append(exchange_start([late_act[k] for k in late], False, "ag_start_late", after=lands[0])[0])
            for k, land in zip(GATHER_GROUPS[gi], lands):
                full_w[k] = _unshard(k[0], land)
        return full_w[(name, layer)]

    pending, rs_started, spatial = {}, [], []

    def emit(name, layer, grad):
        if name == "od_spatial":
            handles, token = exchange_start([grad], False, "ar_start_b")
            spatial.append(handles)
            return token
        pending[(name, layer)] = grad
        for gi, grp in enumerate(REDUCE_GROUPS):
            if (name, layer) in grp and all(k in pending for k in grp):
                handles, token = exchange_start([_shard_slots(k[0], pending[k]).astype(ACT) for k in grp], True,
                                                "rs_start%d" % gi)
                rs_started.append((gi, handles))
                return token
        return None

    loss_part, dx, small = _local_step(x[0], loss_target[0], weight, emit, P)
    loss = lax.psum(loss_part[0, 0], AXES)

    wide = ["ev_norm_g", "ev_sinks", "ev_conv_w", "ev_conv_b", "ev_conv_ln_g", "ev_conv_ln_b", "od_norm_g",
            "od_sgu_ln_g", "od_sgu_ln_b", "ffn_norm_g", "final_norm_g"]
    blk_a = jnp.concatenate(
        [jnp.pad(small[n], ((0, 0), (0, D_MODEL - small[n].shape[1]))) for n in wide], axis=0)
    blk_a = jnp.pad(blk_a, ((0, 48 - blk_a.shape[0]), (0, 0)))
    ar_a, ar_token = exchange_start([blk_a], False, "ar_start_a")

    results = {}
    for gi, handles in rs_started:
        lands = exchange_wait(handles, ar_token, True, "rs_wait%d" % gi)
        for (n, l), land in zip(REDUCE_GROUPS[gi], lands):
            results[n] = sum_adamw(land, wts[n], mom[n], vel[n], l, results.get(n), "adamw_%s%d" % (n, l))
    out_g, out_d, out_m, out_v = {}, {}, {}, {}
    for n in BIG:
        out_g[n], out_d[n], out_m[n], out_v[n] = results[n]

    last = results[REDUCE_GROUPS[-1][-1][0]][0]
    sum_b = sum_slots(exchange_wait(spatial[0], last, False, "ar_wait_b")[0], F32, "ar_sum_b")
    sum_a = sum_slots(exchange_wait(ar_a, last, False, "ar_wait_a")[0], F32, "ar_sum_a")
    full = {}
    off = 0
    for n in wide:
        r_, c_ = small[n].shape
        full[n] = sum_a[off:off + r_, 0:c_]
        off += r_
    full["od_spatial_w"] = sum_b[0:D_GROUPS * BLOCK]
    full["od_spatial_b"] = sum_b[D_GROUPS * BLOCK:D_GROUPS * BLOCK + D_GROUPS]
    full["ev_conv_w"] = lax.dynamic_slice_in_dim(full["ev_conv_w"], me * 64, 64, axis=1)
    full["od_norm_g"] = lax.dynamic_slice_in_dim(full["od_norm_g"], me * 128, 128, axis=1)
    full["od_sgu_ln_g"] = lax.dynamic_slice_in_dim(full["od_sgu_ln_g"], me * 64, 64, axis=1)
    full["od_sgu_ln_b"] = lax.dynamic_slice_in_dim(full["od_sgu_ln_b"], me * 64, 64, axis=1)

    small_names = [n for n in names if n not in BIG]
    view = {n: ((-1, wts[n].shape[-1]) if wts[n].ndim > 1 else (1, -1)) for n in small_names}
    ds, ms, vs = adamw_small([wts[n].reshape(view[n]) for n in small_names],
                             [full[n].reshape(view[n]) for n in small_names],
                             [mom[n].reshape(view[n]) for n in small_names],
                             [vel[n].reshape(view[n]) for n in small_names], "adamw_small")
    for i, n in enumerate(small_names):
        shp = wts[n].shape
        out_g[n], out_d[n], out_m[n], out_v[n] = (full[n].reshape(shp), ds[i].reshape(shp), ms[i].reshape(shp),
                                                  vs[i].reshape(shp))

    return (loss, dx[None], *[out_g[n] for n in names], *[out_d[n] for n in names],
            *[out_m[n] for n in names], *[out_v[n] for n in names])
```

```python
import functools
import math

import jax
import jax.numpy as jnp
from jax import lax
from jax.experimental import pallas as pl
from jax.experimental.pallas import tpu as pltpu

F32 = jnp.float32
ACT = jnp.bfloat16

D_MODEL = 1024
HEAD_DIM = 64
N_HEADS = 8
ATT_W = N_HEADS * HEAD_DIM
A_KV_HEADS = 2
CONV_CH = 512
CONV_WIDTH = 31
CONV_HALO = 32
D_CH = 512
D_GROUPS = 8
BLOCK = 128
D_FF = 2816
ROT_DIM = 16
ROPE_THETA = 500000.0
RMS_EPS = 1e-6
LN_EPS = 1e-5
DILATED = ((128, 1), (512, 4), (2048, 16))
NEG = -1e30
LANES = 128

ADAM_LR = 0.001
ADAM_B1 = 0.9
ADAM_B2 = 0.999
ADAM_EPS = 1e-08
ADAM_WD = 0.01
ADAM_STEP = 10

V7X_VMEM_LIMIT = 56 * 1024 * 1024
SMALL_BLOCK_BYTES = 6 * 1024 * 1024
N_DEV = 8

NN = (((1,), (0,)), ((), ()))
NT = (((1,), (1,)), ((), ()))
TN = (((0,), (0,)), ((), ()))
MESH = pl.DeviceIdType.MESH
AXES = ("x", "y", "c")


def _dot(a, b, dims):
    return lax.dot_general(a, b, dims, preferred_element_type=F32)


def _cp(*sem):
    return pltpu.CompilerParams(dimension_semantics=sem if sem else None,
                                vmem_limit_bytes=V7X_VMEM_LIMIT)


def _tile(n, prefs):
    for p in prefs:
        if n % p == 0:
            return p
    return n


def _sigmoid(x):
    return 1.0 / (1.0 + jnp.exp(-x))


def _rowspec(tm, w, col=0):
    return pl.BlockSpec((tm, w), lambda i, col=col: (i, col))


def _fullspec(shape):
    nd = len(shape)
    return pl.BlockSpec(shape, lambda *a, nd=nd: (0,) * nd)


def _rope_tables(seq):
    half = ROT_DIM // 2
    inv_freq = ROPE_THETA ** (-jnp.arange(half, dtype=F32) * (2.0 / ROT_DIM))
    ang = jnp.arange(seq, dtype=jnp.int32).astype(F32)[:, None] * inv_freq[None, :]
    cos, sin = jnp.cos(ang), jnp.sin(ang)
    lane = jnp.arange(LANES)
    jm = lane % HEAD_DIM
    idx = jm % half
    c = jnp.where(jm[None, :] < ROT_DIM, cos[:, idx], 1.0)
    sa = jnp.where(jm[None, :] < half, -sin[:, idx], 0.0)
    sb = jnp.where((jm[None, :] >= half) & (jm[None, :] < ROT_DIM), sin[:, idx], 0.0)
    return c.astype(F32), sa.astype(F32), sb.astype(F32)


def _rope(x, c, sa, sb):
    return x * c + pltpu.roll(x, LANES - 8, 1) * sa + pltpu.roll(x, 8, 1) * sb


def _rope_t(d, c, sa, sb):
    return d * c + pltpu.roll(d * sa, 8, 1) + pltpu.roll(d * sb, LANES - 8, 1)


def rmsnorm(h, g, name):
    s = h.shape[0]
    tm = _tile(s, (512,))

    def body(h_ref, g_ref, o_ref):
        x = h_ref[...]
        r = lax.rsqrt(jnp.mean(x * x, axis=-1, keepdims=True) + RMS_EPS)
        o_ref[...] = (x * r * g_ref[...]).astype(o_ref.dtype)

    return pl.pallas_call(
        body, grid=(s // tm,), name=name,
        in_specs=[_rowspec(tm, D_MODEL), _fullspec((1, D_MODEL))],
        out_specs=_rowspec(tm, D_MODEL),
        out_shape=jax.ShapeDtypeStruct((s, D_MODEL), ACT),
        compiler_params=_cp("parallel"))(h, g)


def _fold_perm(tm, d, inverse, cache):
    key = (tm, d, inverse)
    if key not in cache:
        m = tm // d
        a = lax.broadcasted_iota(jnp.int32, (tm, tm), 1 if inverse else 0)
        b = lax.broadcasted_iota(jnp.int32, (tm, tm), 0 if inverse else 1)
        src = (a & (m - 1)) * d + (a >> (m.bit_length() - 1))
        cache[key] = (b == src).astype(ACT)
    return cache[key]


def _fold_rows(x, d, cache):
    tm = x.shape[0]
    m = tm // d
    p = _dot(_fold_perm(tm, d, False, cache), x, NN).astype(ACT)
    return jnp.concatenate([p[r * m:(r + 1) * m] for r in range(d)], axis=1)


def _unfold_rows(blk, d, cache):
    w = blk.shape[1] // d
    stacked = jnp.concatenate([blk[:, r * w:(r + 1) * w] for r in range(d)], axis=0)
    return _dot(_fold_perm(stacked.shape[0], d, True, cache), stacked, NN)


def _unfold_rows_f32(blk, d, cache):
    hi = blk.astype(ACT)
    r1 = blk - hi.astype(F32)
    mid = r1.astype(ACT)
    lo = (r1 - mid.astype(F32)).astype(ACT)
    return _unfold_rows(hi, d, cache) + _unfold_rows(mid, d, cache) + _unfold_rows(lo, d, cache)


def inproj(n, w, tabs, nqk, splits, fold_dils, name):
    s = n.shape[0]
    ntot = w.shape[1]
    assert sum(splits) == ntot and splits[0] == nqk
    tm = _tile(s, (256,) if fold_dils else (512,))
    ns = len(splits)

    def body(n_ref, w_ref, c_ref, sa_ref, sb_ref, *outs):
        res = _dot(n_ref[...], w_ref[...], NN)
        c, sa, sb = c_ref[...], sa_ref[...], sb_ref[...]
        for g in range(nqk // LANES):
            x = res[:, g * LANES:(g + 1) * LANES]
            outs[0][:, g * LANES:(g + 1) * LANES] = _rope(x, c, sa, sb).astype(ACT)
        off = nqk
        for o_ref, wd in zip(outs[1:ns], splits[1:]):
            o_ref[...] = res[:, off:off + wd].astype(ACT)
            off += wd
        cache = {}
        for i, d in enumerate(fold_dils):
            outs[ns + 2 * i][...] = _fold_rows(outs[0][...], d, cache)
            outs[ns + 2 * i + 1][...] = _fold_rows(outs[1][...], d, cache)

    out_specs = [_rowspec(tm, wd) for wd in splits]
    out_shape = [jax.ShapeDtypeStruct((s, wd), ACT) for wd in splits]
    for d in fold_dils:
        for wd in splits[:2]:
            out_specs.append(_rowspec(tm // d, d * wd))
            out_shape.append(jax.ShapeDtypeStruct((s // d, d * wd), ACT))
    return pl.pallas_call(
        body, grid=(s // tm,), name=name,
        in_specs=[_rowspec(tm, D_MODEL), _fullspec((D_MODEL, ntot))] + [_rowspec(tm, LANES)] * 3,
        out_specs=out_specs, out_shape=out_shape,
        compiler_params=_cp("parallel"))(n, w, *tabs)


def ffn_up(n, wg, wu, name):
    s = n.shape[0]
    f = wg.shape[1]
    tm = _tile(s, (512,))
    tf = _tile(f, (1408, 512, 256, 128))

    def body(n_ref, wg_ref, wu_ref, g_ref, u_ref, a_ref):
        a = n_ref[...]
        g = _dot(a, wg_ref[...], NN)
        u = _dot(a, wu_ref[...], NN)
        sg = _sigmoid(g)
        silu = g * sg
        g_ref[...] = (u * (sg * (1.0 + g * (1.0 - sg)))).astype(ACT)
        u_ref[...] = silu.astype(ACT)
        a_ref[...] = (silu * u).astype(ACT)

    wspec = pl.BlockSpec((D_MODEL, tf), lambda j, i: (0, j))
    ospec = pl.BlockSpec((tm, tf), lambda j, i: (i, j))
    return pl.pallas_call(
        body, grid=(f // tf, s // tm), name=name,
        in_specs=[pl.BlockSpec((tm, D_MODEL), lambda j, i: (i, 0)), wspec, wspec],
        out_specs=[ospec] * 3,
        out_shape=[jax.ShapeDtypeStruct((s, f), ACT)] * 3,
        compiler_params=_cp("parallel", "parallel"))(n, wg, wu)


def mm_res(parts, w, h, gnext, name):
    s = h.shape[0]
    tm = _tile(s, (512,))
    widths = [p.shape[1] for p in parts]
    assert sum(widths) == w.shape[0]
    np_ = len(parts)

    def body(*refs):
        p_refs = refs[:np_]
        w_ref, h_ref = refs[np_], refs[np_ + 1]
        rest = refs[np_ + 2:]
        acc = h_ref[...]
        off = 0
        for p_ref, wd in zip(p_refs, widths):
            acc = acc + _dot(p_ref[...], w_ref[off:off + wd, :], NN)
            off += wd
        if gnext is None:
            rest[0][...] = acc
        else:
            g_ref, ho_ref, no_ref = rest
            ho_ref[...] = acc
            r = lax.rsqrt(jnp.mean(acc * acc, axis=-1, keepdims=True) + RMS_EPS)
            no_ref[...] = (acc * r * g_ref[...]).astype(ACT)

    in_specs = [_rowspec(tm, wd) for wd in widths] + [_fullspec(w.shape), _rowspec(tm, D_MODEL)]
    args = list(parts) + [w, h]
    out_specs = [_rowspec(tm, D_MODEL)]
    out_shape = [jax.ShapeDtypeStruct((s, D_MODEL), F32)]
    if gnext is not None:
        in_specs.append(_fullspec((1, D_MODEL)))
        args.append(gnext)
        out_specs.append(_rowspec(tm, D_MODEL))
        out_shape.append(jax.ShapeDtypeStruct((s, D_MODEL), ACT))
    out = pl.pallas_call(
        body, grid=(s // tm,), name=name, in_specs=in_specs, out_specs=out_specs,
        out_shape=out_shape, compiler_params=_cp("parallel"))(*args)
    return (out[0], None) if gnext is None else (out[0], out[1])


def _band_mask(n, max_dist):
    qi = lax.broadcasted_iota(jnp.int32, (BLOCK, 2 * BLOCK), 0)
    kj = lax.broadcasted_iota(jnp.int32, (BLOCK, 2 * BLOCK), 1)
    dist = qi + BLOCK - kj
    valid = jnp.logical_and(dist >= 0, dist <= max_dist)
    return jnp.logical_and(valid, jnp.logical_or(kj >= BLOCK, n > 0))


def _band_mask_t(n, max_dist):
    kj = lax.broadcasted_iota(jnp.int32, (2 * BLOCK, BLOCK), 0)
    qi = lax.broadcasted_iota(jnp.int32, (2 * BLOCK, BLOCK), 1)
    dist = qi + BLOCK - kj
    valid = jnp.logical_and(dist >= 0, dist <= max_dist)
    return jnp.logical_and(valid, jnp.logical_or(kj >= BLOCK, n > 0))


def _fold(a, d):
    return a.reshape(a.shape[0] // d, d * a.shape[1])


def attn_fwd(qk, v, *, d, hkv, max_dist, sink, out_dtype, name, folded=None, keep_folded=False):
    s = qk.shape[0]
    kvw = hkv * HEAD_DIM
    wqk = ATT_W + kvw
    assert qk.shape[1] == wqk and (d == 1 or (wqk % ATT_W == 0 and wqk % kvw == 0))
    assert sink is None or max_dist == BLOCK - 1
    nb = s // d // BLOCK
    grp = N_HEADS // hkv
    qpb, kpb, koff = wqk // ATT_W, wqk // kvw, ATT_W // kvw

    def body(*refs):
        if sink is None:
            q_ref, kc_ref, kp_ref, vc_ref, vp_ref, o_ref, l_ref, o_buf = refs
        else:
            q_ref, kc_ref, kp_ref, vc_ref, vp_ref, s_ref, o_ref, l_ref, o_buf = refs
        n = pl.program_id(1)
        valid = _band_mask_t(n, max_dist)
        qsl = [slice(h * HEAD_DIM, (h + 1) * HEAD_DIM) for h in range(N_HEADS)]
        ksl = [slice((h // grp) * HEAD_DIM, (h // grp + 1) * HEAD_DIM) for h in range(N_HEADS)]
        kk = jnp.concatenate([kp_ref[...], kc_ref[...]], axis=0)
        vv = jnp.concatenate([vp_ref[...], vc_ref[...]], axis=0)
        scores = []
        for h in range(N_HEADS):
            q = q_ref[:, qsl[h]] * 0.125
            scores.append(_dot(kk[:, ksl[h]], q, NT))
        probs = []
        head_row = lax.broadcasted_iota(jnp.int32, (SUBLANES, BLOCK), 0)
        lse8 = jnp.zeros((SUBLANES, BLOCK), F32)
        if sink is not None:
            sink_row = lax.broadcasted_iota(jnp.int32, (2 * BLOCK, BLOCK), 0) == 0
        for h in range(N_HEADS):
            sc = jnp.where(valid, scores[h], NEG)
            if sink is not None:
                sc = jnp.where(sink_row, s_ref[h:h + 1, 0:1], sc)
            m = jnp.max(sc, axis=0, keepdims=True)
            p = jnp.exp(sc - m)
            l = jnp.sum(p, axis=0, keepdims=True)
            if sink is not None:
                p = jnp.where(sink_row, 0.0, p)
            lse8 = jnp.where(head_row == h, m + jnp.log(l), lse8)
            probs.append((p * (1.0 / l)).astype(ACT))
        l_ref[...] = jnp.concatenate([lse8, jnp.zeros((BLOCK - SUBLANES, BLOCK), F32)], axis=0).T
        for h in range(N_HEADS):
            o_buf[:, qsl[h]] = _dot(probs[h], vv[:, ksl[h]], TN)
        o_ref[...] = o_buf[...].astype(o_ref.dtype)

    prev = lambda n: jnp.maximum(n - 1, 0)
    in_specs = [
        pl.BlockSpec((BLOCK, ATT_W), lambda r, n: (n, r * qpb)),
        pl.BlockSpec((BLOCK, kvw), lambda r, n: (n, r * kpb + koff)),
        pl.BlockSpec((BLOCK, kvw), lambda r, n: (prev(n), r * kpb + koff)),
        pl.BlockSpec((BLOCK, kvw), lambda r, n: (n, r)),
        pl.BlockSpec((BLOCK, kvw), lambda r, n: (prev(n), r)),
    ]
    qkf, vf = (_fold(qk, d), _fold(v, d)) if folded is None else folded
    args = [qkf, qkf, qkf, vf, vf]
    if sink is not None:
        in_specs.append(_fullspec((N_HEADS, LANES)))
        args.append(sink)
    ospec = pl.BlockSpec((BLOCK, ATT_W), lambda r, n: (n, r))
    lspec = pl.BlockSpec((BLOCK, LANES), lambda r, n: (n, r))
    o, lse = pl.pallas_call(
        body, grid=(d, nb), name=name, in_specs=in_specs, out_specs=[ospec, lspec],
        out_shape=[jax.ShapeDtypeStruct((s // d, d * ATT_W), out_dtype),
                   jax.ShapeDtypeStruct((s // d, d * LANES), F32)],
        scratch_shapes=[pltpu.VMEM((BLOCK, ATT_W), F32)],
        compiler_params=_cp("parallel", "parallel"))(*args)
    return (o, lse) if keep_folded else (o.reshape(s, ATT_W), lse.reshape(s, LANES))


def attn_bwd(qk, v, do_src, do_col, o, lse, *, d, hkv, max_dist, sink, out_dtype, name, folded=None,
             folded_do_o=None, keep_folded=False):
    s = qk.shape[0]
    kvw = hkv * HEAD_DIM
    wqk = ATT_W + kvw
    nb = s // d // BLOCK
    grp = N_HEADS // hkv
    qpb, kpb, koff = wqk // ATT_W, wqk // kvw, ATT_W // kvw
    dob = do_src.shape[1] // ATT_W
    has_sink = sink is not None

    def body(*refs):
        refs = list(refs)
        q_ref, kc_ref, kp_ref, vc_ref, vp_ref, do_ref, o_ref, l_ref = refs[:8]
        pos = 8
        if has_sink:
            s_ref = refs[pos]
            pos += 1
        dq_ref, dk_ref, dv_ref = refs[pos:pos + 3]
        pos += 3
        if has_sink:
            ds_ref = refs[pos]
            pos += 1
        ck_ref, cv_ref, dq_buf, dk_buf, dv_buf = refs[pos:pos + 5]
        r_id = pl.program_id(0)
        n = pl.program_id(1)

        @pl.when(n == 0)
        def _():
            ck_ref[...] = jnp.zeros_like(ck_ref)
            cv_ref[...] = jnp.zeros_like(cv_ref)

        if has_sink:
            @pl.when(jnp.logical_and(n == 0, r_id == 0))
            def _():
                ds_ref[...] = jnp.zeros_like(ds_ref)

        @pl.when(n < nb)
        def _():
            valid = _band_mask_t(n, max_dist)
            qsl = [slice(h * HEAD_DIM, (h + 1) * HEAD_DIM) for h in range(N_HEADS)]
            ksl = [slice((h // grp) * HEAD_DIM, (h // grp + 1) * HEAD_DIM) for h in range(N_HEADS)]
            kk = jnp.concatenate([kp_ref[...], kc_ref[...]], axis=0)
            vv = jnp.concatenate([vp_ref[...], vc_ref[...]], axis=0)
            qs, first = [], []
            for h in range(N_HEADS):
                q = q_ref[:, qsl[h]] * 0.125
                qs.append(q)
                first.append((_dot(kk[:, ksl[h]], q, NT), _dot(vv[:, ksl[h]], do_ref[:, qsl[h]], NT)))
            lse_t = l_ref[...].T
            prod = do_ref[...].astype(F32) * o_ref[...].astype(F32)
            hi = prod.astype(ACT)
            lo = (prod - hi.astype(F32)).astype(ACT)
            col = lax.broadcasted_iota(jnp.int32, (LANES, ATT_W), 1)
            row = lax.broadcasted_iota(jnp.int32, (LANES, ATT_W), 0)
            head_of = jnp.logical_and(col >= row * HEAD_DIM, col < (row + 1) * HEAD_DIM).astype(ACT)
            e_t = _dot(head_of, hi, NT) + _dot(head_of, lo, NT)
            mid = []
            for h in range(N_HEADS):
                s_t, dp_t = first[h]
                lse_h, e_h = lse_t[h:h + 1, :], e_t[h:h + 1, :]
                p_t = jnp.exp(jnp.where(valid, s_t, NEG) - lse_h)
                mid.append(((p_t * (dp_t - e_h)).astype(ACT), p_t.astype(ACT)))
                if has_sink:
                    sk = s_ref[h:h + 1, 0:1]
                    dsk = -jnp.sum(jnp.exp(sk - lse_h) * e_h, axis=1, keepdims=True)
                    ds_ref[h:h + 1, :] = ds_ref[h:h + 1, :] + dsk
            dkk = [None] * hkv
            dvv = [None] * hkv
            for h in range(N_HEADS):
                kh = h // grp
                ds_t, p_t = mid[h]
                dq_buf[:, qsl[h]] = _dot(ds_t, kk[:, ksl[h]], TN) * 0.125
                for lst, val in ((dkk, _dot(ds_t, qs[h], NN)), (dvv, _dot(p_t, do_ref[:, qsl[h]], NN))):
                    lst[kh] = val if lst[kh] is None else lst[kh] + val
            for kh in range(hkv):
                ks = slice(kh * HEAD_DIM, (kh + 1) * HEAD_DIM)
                dk_buf[:, ks] = ck_ref[:, ks] + dkk[kh][:BLOCK]
                dv_buf[:, ks] = cv_ref[:, ks] + dvv[kh][:BLOCK]
                ck_ref[:, ks] = dkk[kh][BLOCK:]
                cv_ref[:, ks] = dvv[kh][BLOCK:]
            dq_ref[...] = dq_buf[...].astype(dq_ref.dtype)
            dk_ref[...] = dk_buf[...].astype(dk_ref.dtype)
            dv_ref[...] = dv_buf[...].astype(dv_ref.dtype)

        @pl.when(n == nb)
        def _():
            dk_ref[...] = ck_ref[...].astype(dk_ref.dtype)
            dv_ref[...] = cv_ref[...].astype(dv_ref.dtype)

    qrow = lambda n: jnp.minimum(n, nb - 1)
    prow = lambda n: jnp.maximum(jnp.minimum(n, nb - 1) - 1, 0)
    krow = lambda n: jnp.maximum(n - 1, 0)
    in_specs = [
        pl.BlockSpec((BLOCK, ATT_W), lambda r, n: (qrow(n), r * qpb)),
        pl.BlockSpec((BLOCK, kvw), lambda r, n: (qrow(n), r * kpb + koff)),
        pl.BlockSpec((BLOCK, kvw), lambda r, n: (prow(n), r * kpb + koff)),
        pl.BlockSpec((BLOCK, kvw), lambda r, n: (qrow(n), r)),
        pl.BlockSpec((BLOCK, kvw), lambda r, n: (prow(n), r)),
        pl.BlockSpec((BLOCK, ATT_W), lambda r, n: (qrow(n), r * dob + do_col)),
        pl.BlockSpec((BLOCK, ATT_W), lambda r, n: (qrow(n), r)),
        pl.BlockSpec((BLOCK, LANES), lambda r, n: (qrow(n), r)),
    ]
    qkf, vf = (_fold(qk, d), _fold(v, d)) if folded is None else folded
    dof, of = (_fold(do_src, d), _fold(o, d)) if folded_do_o is None else folded_do_o
    args = [qkf, qkf, qkf, vf, vf, dof, of, _fold(lse, d)]
    if has_sink:
        in_specs.append(_fullspec((N_HEADS, LANES)))
        args.append(sink)
    qspec = pl.BlockSpec((BLOCK, ATT_W), lambda r, n: (qrow(n), r))
    kspec = pl.BlockSpec((BLOCK, kvw), lambda r, n: (krow(n), r))
    out_specs = [qspec, kspec, kspec]
    out_shape = [jax.ShapeDtypeStruct((s // d, d * ATT_W), out_dtype),
                 jax.ShapeDtypeStruct((s // d, d * kvw), out_dtype),
                 jax.ShapeDtypeStruct((s // d, d * kvw), out_dtype)]
    if has_sink:
        out_specs.append(_fullspec((N_HEADS, LANES)))
        out_shape.append(jax.ShapeDtypeStruct((N_HEADS, LANES), F32))
    out = pl.pallas_call(
        body, grid=(d, nb + 1), name=name, in_specs=in_specs, out_specs=out_specs,
        out_shape=out_shape,
        scratch_shapes=[pltpu.VMEM((BLOCK, kvw), F32), pltpu.VMEM((BLOCK, kvw), F32),
                        pltpu.VMEM((BLOCK, ATT_W), F32), pltpu.VMEM((BLOCK, kvw), F32), pltpu.VMEM((BLOCK, kvw), F32)],
        compiler_params=_cp("arbitrary", "arbitrary"))(*args)
    res = list(out[:3]) if keep_folded else [out[0].reshape(s, ATT_W), out[1].reshape(s, kvw), out[2].reshape(s, kvw)]
    if has_sink:
        res.append(out[3])
    return res


def combine_fwd(os_, lses, dils, name):
    s = os_[0].shape[0] * dils[0]
    tm = _tile(s, (256,))
    fold_dils = [d for d in dils if d > 1]

    def body(o1, o2, o3, l1, l2, l3, c_ref, l_ref, *rest):
        c_buf = rest[-1]
        cache = {}
        o = [r[...].astype(F32) if d == 1 else _unfold_rows(r[...], d, cache) for r, d in zip((o1, o2, o3), dils)]
        a, b, c = [r[...] if d == 1 else _unfold_rows_f32(r[...], d, cache) for r, d in zip((l1, l2, l3), dils)]
        m = jnp.maximum(jnp.maximum(a, b), c)
        wa, wb, wc = jnp.exp(a - m), jnp.exp(b - m), jnp.exp(c - m)
        tot = wa + wb + wc
        l_ref[...] = m + jnp.log(tot)
        rt = 1.0 / tot
        wa, wb, wc = wa * rt, wb * rt, wc * rt
        for h in range(N_HEADS):
            cs = slice(h * HEAD_DIM, (h + 1) * HEAD_DIM)
            c_buf[:, cs] = (wa[:, h:h + 1] * o[0][:, cs] + wb[:, h:h + 1] * o[1][:, cs] + wc[:, h:h + 1] * o[2][:, cs])
        mix = c_buf[...].astype(ACT)
        c_ref[...] = mix
        for f_ref, d in zip(rest[:-1], fold_dils):
            f_ref[...] = _fold_rows(mix, d, cache)

    return pl.pallas_call(
        body, grid=(s // tm,), name=name,
        in_specs=[_rowspec(tm // d, d * ATT_W) for d in dils] + [_rowspec(tm // d, d * LANES) for d in dils],
        out_specs=[_rowspec(tm, ATT_W), _rowspec(tm, LANES)] + [_rowspec(tm // d, d * ATT_W) for d in fold_dils],
        out_shape=[jax.ShapeDtypeStruct((s, ATT_W), ACT), jax.ShapeDtypeStruct((s, LANES), F32)]
        + [jax.ShapeDtypeStruct((s // d, d * ATT_W), ACT) for d in fold_dils],
        scratch_shapes=[pltpu.VMEM((tm, ATT_W), F32)],
        compiler_params=_cp("parallel"))(*os_, *lses)


def assemble(parts, tabs, name):
    terms_of = [[t if isinstance(t, tuple) else (t, 1) for t in terms] for terms, _ in parts]
    flat = [t for ts in terms_of for t in ts]
    s = terms_of[0][0][0].shape[0] * terms_of[0][0][1]
    tm = _tile(s, (256,) if any(d > 1 for _, d in flat) else (512,))
    widths = [ts[0][0].shape[1] // ts[0][1] for ts in terms_of]
    flags = [f for _, f in parts]

    def body(*refs):
        c_ref, sa_ref, sb_ref, o_ref = refs[len(flat):]
        c, sa, sb = c_ref[...], sa_ref[...], sb_ref[...]
        off = 0
        first = 0
        cache = {}
        for wd, ts, fl in zip(widths, terms_of, flags):
            t_refs = refs[first:first + len(ts)]
            first += len(ts)
            x = None
            for t_ref, (_, d) in zip(t_refs, ts):
                t = t_ref[...].astype(F32) if d == 1 else _unfold_rows(t_ref[...], d, cache)
                x = t if x is None else x + t
            for g in range(wd // LANES):
                cols = slice(g * LANES, (g + 1) * LANES)
                y = _rope_t(x[:, cols], c, sa, sb) if fl else x[:, cols]
                o_ref[:, off + g * LANES:off + (g + 1) * LANES] = y.astype(ACT)
            off += wd

    tot = sum(widths)
    return pl.pallas_call(
        body, grid=(s // tm,), name=name,
        in_specs=[_rowspec(tm // d, a.shape[1]) for a, d in flat] + [_rowspec(tm, LANES)] * 3,
        out_specs=_rowspec(tm, tot), out_shape=jax.ShapeDtypeStruct((s, tot), ACT),
        compiler_params=_cp("parallel"))(*[a for a, _ in flat], *tabs)


def _ln_stats(x):
    mu = jnp.mean(x, axis=-1, keepdims=True)
    xc = x - mu
    var = jnp.mean(xc * xc, axis=-1, keepdims=True)
    rstd = lax.rsqrt(var + LN_EPS)
    return xc * rstd, rstd


SUBLANES = 8


TAP_ROWS = 32


def _tap_sum(buf, cw_ref, offsets, tm, res_ref):
    for r0 in range(0, tm, TAP_ROWS):
        acc = None
        for ph in range(SUBLANES):
            taps = [j for j, off in enumerate(offsets) if off % SUBLANES == ph]
            if not taps:
                continue
            rows = TAP_ROWS if ph == 0 else TAP_ROWS + SUBLANES
            part = None
            for j in taps:
                term = cw_ref[j:j + 1, :] * buf[pl.ds(offsets[j] - ph + r0, rows), :]
                part = term if part is None else part + term
            part = part[ph:ph + TAP_ROWS]
            acc = part if acc is None else acc + part
        res_ref[pl.ds(r0, TAP_ROWS), :] = acc


def conv_fwd(ga, gb, cw, cb, lg, lb, name):
    s = ga.shape[0]
    tm = _tile(s, (512,))
    hb = tm // CONV_HALO

    def body(ga_ref, gb_ref, gah_ref, gbh_ref, cw_ref, cb_ref, lg_ref, lb_ref, c_ref, c1_ref, buf):
        i = pl.program_id(0)
        halo = gah_ref[...].astype(F32) * _sigmoid(gbh_ref[...].astype(F32))
        buf[0:CONV_HALO, :] = jnp.where(i > 0, halo, 0.0)
        buf[CONV_HALO:, :] = ga_ref[...].astype(F32) * _sigmoid(gb_ref[...].astype(F32))
        first = CONV_HALO - (CONV_WIDTH - 1)
        _tap_sum(buf, cw_ref, [first + j for j in range(CONV_WIDTH)], tm, c1_ref)
        acc = c1_ref[...] + cb_ref[...]
        c1_ref[...] = acc
        xh, _ = _ln_stats(acc)
        y = xh * lg_ref[...] + lb_ref[...]
        c_ref[...] = (y * _sigmoid(y)).astype(ACT)

    hspec = pl.BlockSpec((CONV_HALO, CONV_CH), lambda i: (jnp.maximum(i * hb - 1, 0), 0))
    vec = _fullspec((1, CONV_CH))
    spec = _rowspec(tm, CONV_CH)
    return pl.pallas_call(
        body, grid=(s // tm,), name=name,
        in_specs=[spec, spec, hspec, hspec, _fullspec((CONV_WIDTH, CONV_CH)), vec, vec, vec],
        out_specs=[spec, spec],
        out_shape=[jax.ShapeDtypeStruct((s, CONV_CH), ACT), jax.ShapeDtypeStruct((s, CONV_CH), F32)],
        scratch_shapes=[pltpu.VMEM((tm + CONV_HALO, CONV_CH), F32)],
        compiler_params=_cp("parallel"))(ga, gb, ga, gb, cw, cb, lg, lb)


def conv_bwd_ln(c1, dsrc, dcol, lg, lb, name):
    s = c1.shape[0]
    tm = _tile(s, (512,))

    def body(c1_ref, d_ref, lg_ref, lb_ref, o_ref, dg_ref, db_ref):
        @pl.when(pl.program_id(0) == 0)
        def _():
            dg_ref[...] = jnp.zeros_like(dg_ref)
            db_ref[...] = jnp.zeros_like(db_ref)

        xh, rstd = _ln_stats(c1_ref[...].astype(F32))
        y = xh * lg_ref[...] + lb_ref[...]
        sg = _sigmoid(y)
        dy = d_ref[...].astype(F32) * (sg * (1.0 + y * (1.0 - sg)))
        dg_ref[0:1, :] = dg_ref[0:1, :] + jnp.sum(dy * xh, axis=0, keepdims=True)
        db_ref[0:1, :] = db_ref[0:1, :] + jnp.sum(dy, axis=0, keepdims=True)
        dxh = dy * lg_ref[...]
        o_ref[...] = rstd * (dxh - jnp.mean(dxh, axis=-1, keepdims=True)
                             - xh * jnp.mean(dxh * xh, axis=-1, keepdims=True))

    vec = _fullspec((1, CONV_CH))
    acc = _fullspec((8, CONV_CH))
    return pl.pallas_call(
        body, grid=(s // tm,), name=name,
        in_specs=[_rowspec(tm, CONV_CH), _rowspec(tm, CONV_CH, dcol), vec, vec],
        out_specs=[_rowspec(tm, CONV_CH), acc, acc],
        out_shape=[jax.ShapeDtypeStruct((s, CONV_CH), F32)] + [jax.ShapeDtypeStruct((8, CONV_CH), F32)] * 2,
        compiler_params=_cp("arbitrary"))(c1, dsrc, lg, lb)


def conv_bwd_conv(dc1, ga, gb, cw, name):
    s = ga.shape[0]
    tm = _tile(s, (256,))
    hb = tm // CONV_HALO
    nt = s // tm
    last_h = s // CONV_HALO - 1
    first = CONV_HALO - (CONV_WIDTH - 1)


    def rows8(x):
        return jnp.sum(x.reshape(x.shape[0] // SUBLANES, SUBLANES, CONV_CH), axis=0)

    def body(d_ref, dn_ref, ga_ref, gb_ref, gah_ref, gbh_ref, cw_ref,
             dga_ref, dgb_ref, dw_ref, db_ref, dbuf, cbuf, sbuf):
        i = pl.program_id(0)

        @pl.when(i == 0)
        def _():
            dw_ref[...] = jnp.zeros_like(dw_ref)
            db_ref[...] = jnp.zeros_like(db_ref)

        d = d_ref[...]
        dbuf[0:tm, :] = d
        dbuf[tm:, :] = jnp.where(i < nt - 1, dn_ref[...], 0.0)
        halo = gah_ref[...].astype(F32) * _sigmoid(gbh_ref[...].astype(F32))
        cbuf[0:CONV_HALO, :] = jnp.where(i > 0, halo, 0.0)
        a = ga_ref[...].astype(F32)
        sg = _sigmoid(gb_ref[...].astype(F32))
        cbuf[CONV_HALO:, :] = a * sg
        for ph in range(SUBLANES):
            taps = [j for j in range(CONV_WIDTH) if (first + j) % SUBLANES == ph]
            if ph:
                sbuf[0:tm + CONV_HALO - SUBLANES, :] = cbuf[pl.ds(ph, tm + CONV_HALO - SUBLANES), :]
            src = sbuf if ph else cbuf
            for r0 in range(0, tm, TAP_ROWS):
                d_blk = dbuf[pl.ds(r0, TAP_ROWS), :]
                for j in taps:
                    tap = src[pl.ds(first + j - ph + r0, TAP_ROWS), :]
                    rows = slice(j * SUBLANES, (j + 1) * SUBLANES)
                    dw_ref[rows, :] = dw_ref[rows, :] + rows8(d_blk * tap)
        db_ref[...] = db_ref[...] + rows8(d)
        _tap_sum(dbuf, cw_ref, [CONV_WIDTH - 1 - j for j in range(CONV_WIDTH)], tm, sbuf)
        dc0 = sbuf[0:tm, :]
        dga_ref[...] = (dc0 * sg).astype(ACT)
        dgb_ref[...] = (dc0 * a * sg * (1.0 - sg)).astype(ACT)

    spec = _rowspec(tm, CONV_CH)
    hprev = pl.BlockSpec((CONV_HALO, CONV_CH), lambda i: (jnp.maximum(i * hb - 1, 0), 0))
    hnext = pl.BlockSpec((CONV_HALO, CONV_CH), lambda i: (jnp.minimum((i + 1) * hb, last_h), 0))
    return pl.pallas_call(
        body, grid=(nt,), name=name,
        in_specs=[spec, hnext, spec, spec, hprev, hprev, _fullspec((CONV_WIDTH, CONV_CH))],
        out_specs=[spec, spec, _fullspec((CONV_HALO * SUBLANES, CONV_CH)), _fullspec((SUBLANES, CONV_CH))],
        out_shape=[jax.ShapeDtypeStruct((s, CONV_CH), ACT)] * 2
        + [jax.ShapeDtypeStruct((CONV_HALO * SUBLANES, CONV_CH), F32), jax.ShapeDtypeStruct((SUBLANES, CONV_CH), F32)],
        scratch_shapes=[pltpu.VMEM((tm + CONV_HALO, CONV_CH), F32)] * 3,
        compiler_params=_cp("arbitrary"))(dc1, dc1, ga, gb, ga, gb, cw)


_GELU_K = math.sqrt(2.0 / math.pi)
_GELU_C = 0.044715


def _gelu(x):
    return 0.5 * x * (1.0 + jnp.tanh(_GELU_K * (x + _GELU_C * x * x * x)))


def _gelu_grad(x):
    t = jnp.tanh(_GELU_K * (x + _GELU_C * x * x * x))
    return 0.5 * (1.0 + t) + 0.5 * x * (1.0 - t * t) * _GELU_K * (1.0 + 3.0 * _GELU_C * x * x)


def _tril():
    qi = lax.broadcasted_iota(jnp.int32, (BLOCK, BLOCK), 0)
    kj = lax.broadcasted_iota(jnp.int32, (BLOCK, BLOCK), 1)
    return kj <= qi


GMLP_CHUNKS = 2


def _gmlp_weights(sw_ref, w_buf, wt_buf):
    tril = _tril()
    for g in range(D_GROUPS):
        w = jnp.where(tril, sw_ref[g], 0.0)
        w_buf[g] = w.astype(ACT)
        if wt_buf is not None:
            wt_buf[g] = w.T.astype(ACT)


def _gmlp_mix(w_buf, gn, sb_ref, m_buf):
    for c in range(GMLP_CHUNKS):
        rows = slice(c * BLOCK, (c + 1) * BLOCK)
        for g in range(D_GROUPS):
            cs = slice(g * HEAD_DIM, (g + 1) * HEAD_DIM)
            m_buf[rows, cs] = _dot(w_buf[g], gn[rows, cs], NN) + sb_ref[:, cs]


def gmlp_fwd(z, lg, lb, sw, sbx, name):
    s = z.shape[0]
    tm = GMLP_CHUNKS * BLOCK

    def body(z_ref, lg_ref, lb_ref, sw_ref, sb_ref, o_ref, w_buf, m_buf):
        @pl.when(pl.program_id(0) == 0)
        def _():
            _gmlp_weights(sw_ref, w_buf, None)

        zz = _gelu(z_ref[...].astype(F32))
        xh, _ = _ln_stats(zz[:, D_CH:])
        gn = (xh * lg_ref[...] + lb_ref[...]).astype(ACT)
        _gmlp_mix(w_buf, gn, sb_ref, m_buf)
        o_ref[...] = (zz[:, :D_CH] * m_buf[...]).astype(ACT)

    return pl.pallas_call(
        body, grid=(s // tm,), name=name,
        in_specs=[_rowspec(tm, 2 * D_CH), _fullspec((1, D_CH)), _fullspec((1, D_CH)),
                  _fullspec((D_GROUPS, BLOCK, BLOCK)), _fullspec((BLOCK, D_CH))],
        out_specs=_rowspec(tm, D_CH), out_shape=jax.ShapeDtypeStruct((s, D_CH), ACT),
        scratch_shapes=[pltpu.VMEM((D_GROUPS, BLOCK, BLOCK), ACT), pltpu.VMEM((tm, D_CH), F32)],
        compiler_params=_cp("arbitrary"))(z, lg, lb, sw, sbx)


def gmlp_bwd(z, dsrc, dcol, lg, lb, sw, sbx, name):
    s = z.shape[0]
    tm = GMLP_CHUNKS * BLOCK

    def body(z_ref, d_ref, lg_ref, lb_ref, sw_ref, sb_ref, dz_ref, dw_ref, dsb_ref, dg_ref, db_ref,
             w_buf, wt_buf, m_buf, dgn_buf):
        @pl.when(pl.program_id(0) == 0)
        def _():
            _gmlp_weights(sw_ref, w_buf, wt_buf)
            dw_ref[...] = jnp.zeros_like(dw_ref)
            dsb_ref[...] = jnp.zeros_like(dsb_ref)
            dg_ref[...] = jnp.zeros_like(dg_ref)
            db_ref[...] = jnp.zeros_like(db_ref)

        zr = z_ref[...].astype(F32)
        zz = _gelu(zr)
        u = zz[:, :D_CH]
        xh, rstd = _ln_stats(zz[:, D_CH:])
        gn = (xh * lg_ref[...] + lb_ref[...]).astype(ACT)
        dd = d_ref[...].astype(F32)
        _gmlp_mix(w_buf, gn, sb_ref, m_buf)
        dz_ref[:, :D_CH] = (dd * m_buf[...] * _gelu_grad(zr[:, :D_CH])).astype(ACT)
        dmix = dd * u
        dmix_a = dmix.astype(ACT)
        dsb = dmix[0:BLOCK]
        for c in range(1, GMLP_CHUNKS):
            dsb = dsb + dmix[c * BLOCK:(c + 1) * BLOCK]
        dsb_ref[...] = dsb_ref[...] + dsb
        tril = _tril()
        for g in range(D_GROUPS):
            cs = slice(g * HEAD_DIM, (g + 1) * HEAD_DIM)
            dw = None
            for c in range(GMLP_CHUNKS):
                rows = slice(c * BLOCK, (c + 1) * BLOCK)
                t = _dot(dmix_a[rows, cs], gn[rows, cs], NT)
                dw = t if dw is None else dw + t
                dgn_buf[rows, cs] = _dot(wt_buf[g], dmix_a[rows, cs], NN)
            dw_ref[g] = dw_ref[g] + jnp.where(tril, dw, 0.0)
        dgn = dgn_buf[...]
        dg_ref[0:1, :] = dg_ref[0:1, :] + jnp.sum(dgn * xh, axis=0, keepdims=True)
        db_ref[0:1, :] = db_ref[0:1, :] + jnp.sum(dgn, axis=0, keepdims=True)
        dxh = dgn * lg_ref[...]
        dgate = rstd * (dxh - jnp.mean(dxh, axis=-1, keepdims=True)
                        - xh * jnp.mean(dxh * xh, axis=-1, keepdims=True))
        dz_ref[:, D_CH:] = (dgate * _gelu_grad(zr[:, D_CH:])).astype(ACT)

    vec = _fullspec((1, D_CH))
    acc = _fullspec((8, D_CH))
    wshape = (D_GROUPS, BLOCK, BLOCK)
    return pl.pallas_call(
        body, grid=(s // tm,), name=name,
        in_specs=[_rowspec(tm, 2 * D_CH), _rowspec(tm, D_CH, dcol), vec, vec, _fullspec(wshape),
                  _fullspec((BLOCK, D_CH))],
        out_specs=[_rowspec(tm, 2 * D_CH), _fullspec(wshape), _fullspec((BLOCK, D_CH)), acc, acc],
        out_shape=[jax.ShapeDtypeStruct((s, 2 * D_CH), ACT), jax.ShapeDtypeStruct(wshape, F32),
                   jax.ShapeDtypeStruct((BLOCK, D_CH), F32),
                   jax.ShapeDtypeStruct((8, D_CH), F32), jax.ShapeDtypeStruct((8, D_CH), F32)],
        scratch_shapes=[pltpu.VMEM(wshape, ACT), pltpu.VMEM(wshape, ACT),
                        pltpu.VMEM((tm, D_CH), F32), pltpu.VMEM((tm, D_CH), F32)],
        compiler_params=_cp("arbitrary"))(z, dsrc, lg, lb, sw, sbx)


def _rms_bwd(dn, x, g):
    r = lax.rsqrt(jnp.mean(x * x, axis=-1, keepdims=True) + RMS_EPS)
    u = dn * g
    dx = r * u - x * (r * r * r) * jnp.mean(x * u, axis=-1, keepdims=True)
    return dx, dn * x * r


def final_loss(h, g, target, name):
    s = h.shape[0]
    tm = _tile(s, (512,))

    def body(h_ref, g_ref, t_ref, loss_ref, dh_ref, dg_ref):
        @pl.when(pl.program_id(0) == 0)
        def _():
            loss_ref[...] = jnp.zeros_like(loss_ref)
            dg_ref[...] = jnp.zeros_like(dg_ref)

        x = h_ref[...]
        r = lax.rsqrt(jnp.mean(x * x, axis=-1, keepdims=True) + RMS_EPS)
        diff = x * r * g_ref[...] - t_ref[...]
        part = jnp.sum(jnp.sum(diff * diff, axis=-1, keepdims=True), axis=0, keepdims=True)
        loss_ref[...] = loss_ref[...] + part * (0.5 / D_MODEL)
        dx, dgt = _rms_bwd(diff * (1.0 / D_MODEL), x, g_ref[...])
        dh_ref[...] = dx
        dg_ref[0:1, :] = dg_ref[0:1, :] + jnp.sum(dgt, axis=0, keepdims=True)

    spec = _rowspec(tm, D_MODEL)
    return pl.pallas_call(
        body, grid=(s // tm,), name=name,
        in_specs=[spec, _fullspec((1, D_MODEL)), spec],
        out_specs=[_fullspec((8, LANES)), spec, _fullspec((8, D_MODEL))],
        out_shape=[jax.ShapeDtypeStruct((8, LANES), F32), jax.ShapeDtypeStruct((s, D_MODEL), F32),
                   jax.ShapeDtypeStruct((8, D_MODEL), F32)],
        compiler_params=_cp("arbitrary"))(h, g, target)


def mm_nt(dy, w, name, fold_dils=(), fold_cols=0):
    s, n = dy.shape
    k = w.shape[0]
    tm = _tile(s, (256,) if fold_dils else (512,))
    tk = k if fold_dils else _tile(k, (512,))

    def body(d_ref, w_ref, o_ref, *f_refs):
        res = _dot(d_ref[...].astype(ACT), w_ref[...], NT).astype(ACT)
        o_ref[...] = res
        cache = {}
        for f_ref, d in zip(f_refs, fold_dils):
            f_ref[...] = _fold_rows(res[:, :fold_cols], d, cache)

    out = pl.pallas_call(
        body, grid=(k // tk, s // tm), name=name,
        in_specs=[pl.BlockSpec((tm, n), lambda j, i: (i, 0)), pl.BlockSpec((tk, n), lambda j, i: (j, 0))],
        out_specs=[pl.BlockSpec((tm, tk), lambda j, i: (i, j))]
        + [pl.BlockSpec((tm // d, d * fold_cols), lambda j, i: (i, 0)) for d in fold_dils],
        out_shape=[jax.ShapeDtypeStruct((s, k), ACT)]
        + [jax.ShapeDtypeStruct((s // d, d * fold_cols), ACT) for d in fold_dils],
        compiler_params=_cp("parallel", "parallel"))(dy, w)
    return out if fold_dils else out[0]


def ffn_down_bwd(dh, wd, dact_dgate, dact_dup, name):
    s = dh.shape[0]
    f = wd.shape[0]
    tm = _tile(s, (512,))
    tf = _tile(f, (1408, 512, 256, 128))

    def body(d_ref, w_ref, g_ref, u_ref, dg_ref, du_ref):
        dact = _dot(d_ref[...].astype(ACT), w_ref[...], NT)
        dg_ref[...] = (dact * g_ref[...].astype(F32)).astype(ACT)
        du_ref[...] = (dact * u_ref[...].astype(F32)).astype(ACT)

    tile = pl.BlockSpec((tm, tf), lambda j, i: (i, j))
    return pl.pallas_call(
        body, grid=(f // tf, s // tm), name=name,
        in_specs=[pl.BlockSpec((tm, D_MODEL), lambda j, i: (i, 0)),
                  pl.BlockSpec((tf, D_MODEL), lambda j, i: (j, 0)), tile, tile],
        out_specs=[tile, tile], out_shape=[jax.ShapeDtypeStruct((s, f), ACT)] * 2,
        compiler_params=_cp("parallel", "parallel"))(dh, wd, dact_dgate, dact_dup)


def mm_nt_rms(parts, h, g, dh, name):
    s = h.shape[0]
    tm = _tile(s, (512,) if sum(w.size for _, w in parts) * 2 <= SMALL_BLOCK_BYTES else (256,))
    np_ = len(parts)

    def body(*refs):
        d_refs = refs[:np_]
        w_refs = refs[np_:2 * np_]
        h_ref, g_ref, dh_ref, o_ref, dg_ref = refs[2 * np_:]

        @pl.when(pl.program_id(0) == 0)
        def _():
            dg_ref[...] = jnp.zeros_like(dg_ref)

        dn = None
        for d_ref, w_ref in zip(d_refs, w_refs):
            t = _dot(d_ref[...], w_ref[...], NT)
            dn = t if dn is None else dn + t
        dx, dgt = _rms_bwd(dn, h_ref[...], g_ref[...])
        o_ref[...] = dh_ref[...] + dx
        dg_ref[0:1, :] = dg_ref[0:1, :] + jnp.sum(dgt, axis=0, keepdims=True)

    spec = _rowspec(tm, D_MODEL)
    return pl.pallas_call(
        body, grid=(s // tm,), name=name,
        in_specs=[_rowspec(tm, d.shape[1]) for d, _ in parts] + [_fullspec(w.shape) for _, w in parts]
        + [spec, _fullspec((1, D_MODEL)), spec],
        out_specs=[spec, _fullspec((8, D_MODEL))],
        out_shape=[jax.ShapeDtypeStruct((s, D_MODEL), F32), jax.ShapeDtypeStruct((8, D_MODEL), F32)],
        compiler_params=_cp("arbitrary"))(*[d for d, _ in parts], *[w for _, w in parts], h, g, dh)


def mm_tn(a, b, name):
    s, k = a.shape
    n = b.shape[1]
    tk = _tile(k, (512, 1408, 256, 128))
    tn = _tile(n, (1408, 1280, 1024, 896, 512, 256, 128))
    ts = _tile(s, (2048, 512) if b.dtype == ACT else (1024, 512))
    nt = s // ts

    def body(a_ref, b_ref, o_ref, acc):
        t = _dot(a_ref[...].astype(ACT), b_ref[...].astype(ACT), TN)
        step = pl.program_id(2)

        @pl.when(step == 0)
        def _():
            acc[...] = t

        @pl.when(step > 0)
        def _():
            acc[...] = acc[...] + t

        @pl.when(step == nt - 1)
        def _():
            o_ref[...] = acc[...].astype(ACT)

    return pl.pallas_call(
        body, grid=(k // tk, n // tn, nt), name=name,
        in_specs=[pl.BlockSpec((ts, tk), lambda i, j, t: (t, i)), pl.BlockSpec((ts, tn), lambda i, j, t: (t, j))],
        out_specs=pl.BlockSpec((tk, tn), lambda i, j, t: (i, j)),
        out_shape=jax.ShapeDtypeStruct((k, n), ACT),
        scratch_shapes=[pltpu.VMEM((tk, tn), F32)],
        compiler_params=_cp("parallel", "parallel", "arbitrary"))(a, b)


def _adamw_math(w, g, m, v):
    m = ADAM_B1 * m + (1.0 - ADAM_B1) * g
    v = ADAM_B2 * v + (1.0 - ADAM_B2) * (g * g)
    m_hat = m / (1.0 - ADAM_B1 ** ADAM_STEP)
    v_hat = v / (1.0 - ADAM_B2 ** ADAM_STEP)
    delta = -ADAM_LR * (m_hat / (jnp.sqrt(v_hat) + ADAM_EPS) + ADAM_WD * w)
    return delta, m, v


def sum_adamw(parts, w, m, v, layer, others, name):
    nl, r, c = w.shape
    tr = _tile(r, (256, 128, 64, 32, 16, 8))

    def body(p_ref, w_ref, m_ref, v_ref, *rest):
        g_ref, d_ref, mo_ref, vo_ref = rest[-4:]
        g = p_ref[0].astype(F32)
        for i in range(1, N_DEV):
            g = g + p_ref[i].astype(F32)
        d, mm, vv = _adamw_math(w_ref[...], g, m_ref[...], v_ref[...])
        g_ref[...] = g
        d_ref[...] = d
        mo_ref[...] = mm
        vo_ref[...] = vv

    spec = pl.BlockSpec((None, tr, c), lambda i: (layer, i, 0))
    in_specs = [pl.BlockSpec((N_DEV, tr, c), lambda i: (0, i, 0))] + [spec] * 3
    args = [parts, w, m, v]
    aliases = {}
    if others is not None:
        in_specs += [pl.BlockSpec(memory_space=pl.ANY)] * 4
        args += list(others)
        aliases = {4 + j: j for j in range(4)}
    return pl.pallas_call(
        body, grid=(r // tr,), name=name, in_specs=in_specs, out_specs=[spec] * 4,
        out_shape=[jax.ShapeDtypeStruct((nl, r, c), F32)] * 4, input_output_aliases=aliases,
        compiler_params=_cp("parallel"))(*args)


def cast_layers(items, name):
    n = len(items)

    def body(*refs):
        for i in range(n):
            refs[n + i][...] = refs[i][...].astype(ACT)

    return pl.pallas_call(
        body, grid=(1,), name=name,
        in_specs=[pl.BlockSpec((None,) + w.shape[1:], lambda i, l=l: (l, 0, 0)) for w, l in items],
        out_specs=[_fullspec(w.shape[1:]) for w, _ in items],
        out_shape=[jax.ShapeDtypeStruct(w.shape[1:], ACT) for w, _ in items],
        compiler_params=_cp("arbitrary"))(*[w for w, _ in items])


def adamw_small(ws, gs, ms, vs, name):
    k = len(ws)

    def body(*refs):
        for i in range(k):
            w_ref, g_ref, m_ref, v_ref = (refs[j * k + i] for j in range(4))
            d, mm, vv = _adamw_math(w_ref[...], g_ref[...], m_ref[...], v_ref[...])
            refs[4 * k + i][...] = d
            refs[5 * k + i][...] = mm
            refs[6 * k + i][...] = vv

    shapes = [jax.ShapeDtypeStruct(w.shape, F32) for w in ws]
    specs = [_fullspec(w.shape) for w in ws]
    out = pl.pallas_call(
        body, grid=(1,), name=name, in_specs=specs * 4, out_specs=specs * 3, out_shape=shapes * 3,
        compiler_params=_cp("arbitrary"))(*ws, *gs, *ms, *vs)
    return out[:k], out[k:2 * k], out[2 * k:]


def sum_slots(x, out_dtype, name):
    g, r, c = x.shape
    tr = r if x.size * x.dtype.itemsize <= SMALL_BLOCK_BYTES else _tile(r, (256, 128, 64, 32, 16, 8))

    def body(x_ref, o_ref):
        acc = x_ref[0].astype(F32)
        for i in range(1, g):
            acc = acc + x_ref[i].astype(F32)
        o_ref[...] = acc.astype(o_ref.dtype)

    return pl.pallas_call(
        body, grid=(r // tr,), name=name,
        in_specs=[pl.BlockSpec((g, tr, c), lambda i: (0, i, 0))], out_specs=_rowspec(tr, c),
        out_shape=jax.ShapeDtypeStruct((r, c), out_dtype),
        compiler_params=_cp("parallel"))(x)


HBM_SPEC = pl.BlockSpec(memory_space=pltpu.HBM)
SEM_SPEC = pl.BlockSpec(memory_space=pltpu.SEMAPHORE)
DATAFLOW = pltpu.SideEffectType.DATAFLOW_SIDE_EFFECTING


def _my_rank():
    return 4 * lax.axis_index("x") + 2 * lax.axis_index("y") + lax.axis_index("c")


def _exchange_copies(x_refs, land_refs, send, recv, a2a):
    pos = [lax.axis_index(a) for a in AXES]
    me = _my_rank()
    copies = []
    for x_ref, land_ref, s_ref, r_ref in zip(x_refs, land_refs, send, recv):
        for k in range(N_DEV - 1):
            bits = ((k + 1) >> 2 & 1, (k + 1) >> 1 & 1, (k + 1) & 1)
            peer = tuple(1 - p if b else p for p, b in zip(pos, bits))
            prank = 4 * peer[0] + 2 * peer[1] + peer[2]
            copies.append(pltpu.make_async_remote_copy(
                src_ref=x_ref.at[prank] if a2a else x_ref, dst_ref=land_ref.at[me],
                send_sem=s_ref.at[k], recv_sem=r_ref.at[k], device_id=peer, device_id_type=MESH))
    return copies


def exchange_start(xs, a2a, name, after=None):
    n = len(xs)
    me = _my_rank()
    lands = []
    for x in xs:
        own = lax.dynamic_index_in_dim(x, me, 0, keepdims=True) if a2a else x[None]
        shape = x.shape if a2a else (N_DEV,) + x.shape
        lands.append(lax.dynamic_update_slice(lax.empty(shape, x.dtype), own, (me,) + (0,) * (len(shape) - 1)))

    def body(*refs):
        x_refs, land_refs = refs[:n], refs[n:2 * n]
        outs = refs[len(refs) - 4 * n - 1:]
        for cp in _exchange_copies(x_refs, land_refs, outs[:n], outs[n:2 * n], a2a):
            cp.start()
        token = outs[4 * n]
        token[...] = jnp.zeros_like(token)

    sems = [pltpu.SemaphoreType.DMA((N_DEV - 1,))] * n
    out = pl.pallas_call(
        body, name=name,
        out_shape=tuple(sems + sems + [pltpu.HBM(x.shape, x.dtype) for x in xs]
                        + [pltpu.HBM(l.shape, l.dtype) for l in lands] + [jax.ShapeDtypeStruct((8, LANES), F32)]),
        in_specs=[HBM_SPEC] * (2 * n) + ([] if after is None else [pl.BlockSpec(memory_space=pl.ANY)]),
        out_specs=tuple([SEM_SPEC] * (2 * n) + [HBM_SPEC] * (2 * n) + [pl.BlockSpec(memory_space=pltpu.VMEM)]),
        input_output_aliases={i: 2 * n + i for i in range(2 * n)},
        compiler_params=pltpu.CompilerParams(has_side_effects=DATAFLOW),
    )(*[pltpu.with_memory_space_constraint(a, pltpu.HBM) for a in list(xs) + lands], *([] if after is None else [after]))
    return (out[:n], out[n:2 * n], out[2 * n:3 * n], out[3 * n:4 * n]), out[4 * n]


def exchange_wait(handles, after, a2a, name):
    send, recv, x_thru, land_thru = handles
    n = len(x_thru)

    def body(*refs):
        x_refs, land_refs = refs[:n], refs[n:2 * n]
        s_refs, r_refs = refs[2 * n:3 * n], refs[3 * n:4 * n]
        for cp in _exchange_copies(x_refs, land_refs, s_refs, r_refs, a2a):
            cp.wait_send()
            cp.wait_recv()

    out = pl.pallas_call(
        body, name=name,
        out_shape=tuple([pltpu.HBM(a.shape, a.dtype) for a in list(x_thru) + list(land_thru)]),
        in_specs=[HBM_SPEC] * (2 * n) + [SEM_SPEC] * (2 * n) + [pl.BlockSpec(memory_space=pl.ANY)],
        out_specs=tuple([HBM_SPEC] * (2 * n)),
        input_output_aliases={i: i for i in range(2 * n)},
        compiler_params=pltpu.CompilerParams(has_side_effects=DATAFLOW),
    )(*x_thru, *land_thru, *send, *recv, after)
    return out[n:2 * n]


def _local_step(x, target, weight, emit, P):
    s = x.shape[0]
    tabs = _rope_tables(s)
    sinkb = jnp.broadcast_to(P["ev_sinks"].reshape(N_HEADS, 1), (N_HEADS, LANES))
    sbx = jnp.repeat(P["od_spatial_b"].reshape(D_GROUPS, BLOCK).T, HEAD_DIM, axis=1)
    sw = P["od_spatial_w"].reshape(D_GROUPS, BLOCK, BLOCK)
    fg = P["ffn_norm_g"]
    latest = [None]

    def out(name, layer, grad):
        tok = emit(name, layer, grad)
        if tok is not None:
            latest[0] = tok

    def dep(a):
        return a if latest[0] is None else a + latest[0][0:1, 0:1]

    n0 = rmsnorm(x, P["ev_norm_g"], "rms_in")
    qk_e, v_e, ga, gb = inproj(n0, weight("ev_w_in", 0, n0), tabs, ATT_W + 128,
                               (ATT_W + 128, 128, CONV_CH, CONV_CH), (), "ev_inproj")
    a_e, lse_e = attn_fwd(qk_e, v_e, d=1, hkv=A_KV_HEADS, max_dist=BLOCK - 1, sink=sinkb, out_dtype=ACT,
                          name="ev_attn")
    c_act, c1 = conv_fwd(ga, gb, P["ev_conv_w"], P["ev_conv_b"], P["ev_conv_ln_g"], P["ev_conv_ln_b"], "ev_conv")
    h1, n1 = mm_res([a_e, c_act], weight("ev_w_out", 0, c_act), x, fg[0:1], "ev_outproj")
    gate0, up0, act0 = ffn_up(n1, weight("ffn_w_gate", 0, n1), weight("ffn_w_up", 0, n1), "ffn0_up")
    h2, n2 = mm_res([act0], weight("ffn_w_down", 0, act0), h1, P["od_norm_g"], "ffn0_down")
    fold_dils = tuple(dil for _, dil in DILATED if dil > 1)
    qk_o, v_o, z, *pre = inproj(n2, weight("od_w_in", 0, n2), tabs, 2 * ATT_W, (2 * ATT_W, ATT_W, 2 * D_CH),
                                fold_dils, "od_inproj")
    folded = {dil: (pre[2 * i], pre[2 * i + 1]) for i, dil in enumerate(fold_dils)}
    outs, lses = [], []
    for window, dil in DILATED:
        assert window // dil == BLOCK
        o_r, l_r = attn_fwd(qk_o, v_o, d=dil, hkv=N_HEADS, max_dist=BLOCK, sink=None, out_dtype=ACT,
                            name="od_attn_d%d" % dil, folded=folded.get(dil), keep_folded=True)
        outs.append(o_r)
        lses.append(l_r)
    dils = [dil for _, dil in DILATED]
    c_out, lse_o, *c_folded = combine_fwd(outs, lses, dils, "od_combine")
    d_out = gmlp_fwd(z, P["od_sgu_ln_g"], P["od_sgu_ln_b"], sw, sbx, "od_gmlp")
    h3, n3 = mm_res([c_out, d_out], weight("od_w_out", 0, d_out), h2, fg[1:2], "od_outproj")
    gate1, up1, act1 = ffn_up(n3, weight("ffn_w_gate", 1, n3), weight("ffn_w_up", 1, n3), "ffn1_up")
    h4, _ = mm_res([act1], weight("ffn_w_down", 1, act1), h3, None, "ffn1_down")
    loss_part, dh4, dg_final = final_loss(h4, P["final_norm_g"], target, "loss_head")

    def ffn_bwd(layer, dh_out, h_in, n_in, gate, up, act):
        wg, wu, wd = (weight(n, layer, dh_out) for n in ("ffn_w_gate", "ffn_w_up", "ffn_w_down"))
        tag = "ffn%d" % layer
        dgate, dup = ffn_down_bwd(dh_out, wd, gate, up, tag + "_down_bwd")
        g_wd = mm_tn(act, dh_out, tag + "_dwd")
        dh_in, dgn = mm_nt_rms([(dgate, wg), (dup, wu)], h_in, dep(fg[layer:layer + 1]), dh_out, tag + "_up_bwd")
        out("ffn_w_down", layer, g_wd)
        out("ffn_w_gate", layer, mm_tn(n_in, dgate, tag + "_dwg"))
        out("ffn_w_up", layer, mm_tn(n_in, dup, tag + "_dwu"))
        return dh_in, dgn[0:1]

    dh3, dgn_f1 = ffn_bwd(1, dh4, h3, n3, gate1, up1, act1)

    dcd, *dc_folded = mm_nt(dh3, weight("od_w_out", 0, dh3), "od_outproj_bwd", fold_dils, ATT_W)
    do_o = {dil: (dc_folded[i], c_folded[i]) for i, dil in enumerate(fold_dils)}
    dz, g_sw, g_sbx, g_slg, g_slb = gmlp_bwd(z, dcd, 1, dep(P["od_sgu_ln_g"]), P["od_sgu_ln_b"], sw, sbx,
                                             "od_gmlp_bwd")
    g_sb = jnp.sum(g_sbx.reshape(BLOCK, D_GROUPS, HEAD_DIM), axis=-1).T
    out("od_spatial", 0, jnp.concatenate([g_sw.reshape(D_GROUPS * BLOCK, BLOCK), g_sb], axis=0))
    out("od_w_out", 0, jnp.concatenate([mm_tn(c_out, dh3, "od_dwo_c"), mm_tn(d_out, dh3, "od_dwo_d")], axis=0))
    dqkv = [attn_bwd(qk_o, v_o, dcd if dil == 1 else dcd[:, :ATT_W], 0, c_out, lse_o, d=dil, hkv=N_HEADS,
                     max_dist=BLOCK, sink=None, out_dtype=ACT, name="od_attn_bwd_d%d" % dil,
                     folded=folded.get(dil), folded_do_o=do_o.get(dil), keep_folded=True)
            for window, dil in DILATED]
    dproj_o = assemble([([(b[j], dil) for b, dil in zip(dqkv, dils)], j < 2) for j in range(3)] + [([dz], False)],
                       tabs, "od_dproj")
    dh2, dgn_od = mm_nt_rms([(dproj_o, weight("od_w_in", 0, dproj_o))], h2, dep(P["od_norm_g"]), dh3,
                            "od_inproj_bwd")
    out("od_w_in", 0, mm_tn(n2, dproj_o, "od_dwi"))

    dh1, dgn_f0 = ffn_bwd(0, dh2, h1, n1, gate0, up0, act0)

    dac = mm_nt(dh1, weight("ev_w_out", 0, dh1), "ev_outproj_bwd")
    dc1, g_clg, g_clb = conv_bwd_ln(c1, dac, 1, dep(P["ev_conv_ln_g"]), P["ev_conv_ln_b"], "ev_conv_bwd_ln")
    out("ev_w_out", 0, jnp.concatenate([mm_tn(a_e, dh1, "ev_dwo_a"), mm_tn(c_act, dh1, "ev_dwo_c")], axis=0))
    dga, dgb, g_cw, g_cb = conv_bwd_conv(dc1, ga, gb, P["ev_conv_w"], "ev_conv_bwd")
    dq, dk, dv, dsink = attn_bwd(qk_e, v_e, dac, 0, a_e, lse_e, d=1, hkv=A_KV_HEADS, max_dist=BLOCK - 1,
                                 sink=sinkb, out_dtype=F32, name="ev_attn_bwd")
    dproj_e = assemble([([dq], True), ([dk], True), ([dv], False), ([dga], False), ([dgb], False)], tabs,
                       "ev_dproj")
    out("ev_w_in", 0, mm_tn(n0, dproj_e, "ev_dwi"))
    dx, dgn_ev = mm_nt_rms([(dproj_e, weight("ev_w_in", 0, dproj_e))], x, dep(P["ev_norm_g"]), dh1,
                           "ev_inproj_bwd")

    small = {
        "ev_norm_g": dgn_ev[0:1],
        "ev_sinks": dsink[:, 0:1].reshape(1, N_HEADS),
        "ev_conv_w": jnp.sum(g_cw.reshape(CONV_HALO, SUBLANES, CONV_CH), axis=1)[:CONV_WIDTH],
        "ev_conv_b": jnp.sum(g_cb, axis=0, keepdims=True),
        "ev_conv_ln_g": g_clg[0:1],
        "ev_conv_ln_b": g_clb[0:1],
        "od_norm_g": dgn_od[0:1],
        "od_sgu_ln_g": g_slg[0:1],
        "od_sgu_ln_b": g_slb[0:1],
        "od_spatial_w": g_sw.reshape(D_GROUPS * BLOCK, BLOCK),
        "od_spatial_b": g_sb,
        "ffn_norm_g": jnp.concatenate([dgn_f0, dgn_f1], axis=0),
        "final_norm_g": dg_final[0:1],
    }
    return loss_part, dx, small


BIG = ("ev_w_in", "ev_w_out", "od_w_in", "od_w_out", "ffn_w_gate", "ffn_w_up", "ffn_w_down")
COL_SHARDED = ("ev_w_in", "od_w_in", "ffn_w_gate", "ffn_w_up")
GATHER_GROUPS = (
    (("ev_w_in", 0),),
    (("ev_w_out", 0),),
    (("ffn_w_gate", 0), ("ffn_w_up", 0)),
    (("ffn_w_down", 0),),
    (("od_w_in", 0),),
    (("od_w_out", 0),),
    (("ffn_w_gate", 1), ("ffn_w_up", 1)),
    (("ffn_w_down", 1),),
)
GATHER_EARLY = 4
GATHER_LATE_AT = 2
REDUCE_GROUPS = (
    (("ffn_w_down", 1), ("ffn_w_gate", 1), ("ffn_w_up", 1)),
    (("od_w_out", 0),),
    (("od_w_in", 0),),
    (("ffn_w_down", 0), ("ffn_w_gate", 0), ("ffn_w_up", 0)),
    (("ev_w_out", 0),),
    (("ev_w_in", 0),),
)


def _unshard(name, g):
    if name in COL_SHARDED:
        return jnp.moveaxis(g, 0, 1).reshape(g.shape[1], N_DEV * g.shape[2])
    return g.reshape(N_DEV * g.shape[1], g.shape[2])


def _shard_slots(name, full):
    r, c = full.shape
    if name in COL_SHARDED:
        return jnp.moveaxis(full.reshape(r, N_DEV, c // N_DEV), 1, 0)
    return full.reshape(N_DEV, r // N_DEV, c)


def kernel(x, ev_norm_g, ev_w_in, ev_sinks, ev_conv_w, ev_conv_b, ev_conv_ln_g, ev_conv_ln_b, ev_w_out, od_norm_g, od_w_in, od_sgu_ln_g, od_sgu_ln_b, od_spatial_w, od_spatial_b, od_w_out, ffn_norm_g, ffn_w_gate, ffn_w_up, ffn_w_down, final_norm_g, loss_target, m_ev_norm_g, m_ev_w_in, m_ev_sinks, m_ev_conv_w, m_ev_conv_b, m_ev_conv_ln_g, m_ev_conv_ln_b, m_ev_w_out, m_od_norm_g, m_od_w_in, m_od_sgu_ln_g, m_od_sgu_ln_b, m_od_spatial_w, m_od_spatial_b, m_od_w_out, m_ffn_norm_g, m_ffn_w_gate, m_ffn_w_up, m_ffn_w_down, m_final_norm_g, v_ev_norm_g, v_ev_w_in, v_ev_sinks, v_ev_conv_w, v_ev_conv_b, v_ev_conv_ln_g, v_ev_conv_ln_b, v_ev_w_out, v_od_norm_g, v_od_w_in, v_od_sgu_ln_g, v_od_sgu_ln_b, v_od_spatial_w, v_od_spatial_b, v_od_w_out, v_ffn_norm_g, v_ffn_w_gate, v_ffn_w_up, v_ffn_w_down, v_final_norm_g):
    names = ["ev_norm_g", "ev_w_in", "ev_sinks", "ev_conv_w", "ev_conv_b", "ev_conv_ln_g", "ev_conv_ln_b", "ev_w_out",
             "od_norm_g", "od_w_in", "od_sgu_ln_g", "od_sgu_ln_b", "od_spatial_w", "od_spatial_b", "od_w_out",
             "ffn_norm_g", "ffn_w_gate", "ffn_w_up", "ffn_w_down", "final_norm_g"]
    wts = dict(zip(names, [ev_norm_g, ev_w_in, ev_sinks, ev_conv_w, ev_conv_b, ev_conv_ln_g, ev_conv_ln_b, ev_w_out,
                           od_norm_g, od_w_in, od_sgu_ln_g, od_sgu_ln_b, od_spatial_w, od_spatial_b, od_w_out,
                           ffn_norm_g, ffn_w_gate, ffn_w_up, ffn_w_down, final_norm_g]))
    mom = dict(zip(names, [m_ev_norm_g, m_ev_w_in, m_ev_sinks, m_ev_conv_w, m_ev_conv_b, m_ev_conv_ln_g, m_ev_conv_ln_b,
                           m_ev_w_out, m_od_norm_g, m_od_w_in, m_od_sgu_ln_g, m_od_sgu_ln_b, m_od_spatial_w,
                           m_od_spatial_b, m_od_w_out, m_ffn_norm_g, m_ffn_w_gate, m_ffn_w_up, m_ffn_w_down,
                           m_final_norm_g]))
    vel = dict(zip(names, [v_ev_norm_g, v_ev_w_in, v_ev_sinks, v_ev_conv_w, v_ev_conv_b, v_ev_conv_ln_g, v_ev_conv_ln_b,
                           v_ev_w_out, v_od_norm_g, v_od_w_in, v_od_sgu_ln_g, v_od_sgu_ln_b, v_od_spatial_w,
                           v_od_spatial_b, v_od_w_out, v_ffn_norm_g, v_ffn_w_gate, v_ffn_w_up, v_ffn_w_down,
                           v_final_norm_g]))
    me = _my_rank()

    sp = jnp.zeros((40, LANES), F32)
    sp = sp.at[0:CONV_WIDTH, 0:64].set(ev_conv_w[0])
    sp = sp.at[32, :].set(od_norm_g[0])
    sp = sp.at[33, 0:64].set(od_sgu_ln_g[0])
    sp = sp.at[34, 0:64].set(od_sgu_ln_b[0])

    early = [k for grp in GATHER_GROUPS[:GATHER_EARLY] for k in grp]
    late = [k for grp in GATHER_GROUPS[GATHER_EARLY:] for k in grp]
    early_act = dict(zip(early, cast_layers([(wts[n], l) for n, l in early], "cast_early")))
    late_act = dict(zip(late, cast_layers([(wts[n], l) for n, l in late], "cast_late")))
    ag_early, ag_token = exchange_start([sp] + [early_act[k] for k in early], False, "ag_start")
    ag_late = []
    full_w = {}
    P = {
        "ev_norm_g": ev_norm_g + ag_token[0:1, 0:1], "ev_sinks": ev_sinks, "ev_conv_b": ev_conv_b,
        "ev_conv_ln_g": ev_conv_ln_g, "ev_conv_ln_b": ev_conv_ln_b,
        "od_spatial_w": od_spatial_w, "od_spatial_b": od_spatial_b, "ffn_norm_g": ffn_norm_g,
        "final_norm_g": final_norm_g.reshape(1, D_MODEL),
    }

    def weight(name, layer, after):
        if (name, layer) not in full_w:
            gi = [i for i, grp in enumerate(GATHER_GROUPS) if (name, layer) in grp][0]
            if gi < GATHER_EARLY:
                idx = [1 + early.index(k) for k in GATHER_GROUPS[gi]]
                handles = ag_early
            else:
                idx = [late.index(k) for k in GATHER_GROUPS[gi]]
                handles = ag_late[0]
            if gi == 0:
                idx = [0] + idx
            lands = exchange_wait(tuple([h[i] for i in idx] for h in handles), after, False, "ag_wait%d" % gi)
            if gi == 0:
                spg, lands = lands[0], lands[1:]
                P["ev_conv_w"] = jnp.moveaxis(spg[:, 0:CONV_WIDTH, 0:64], 0, 1).reshape(CONV_WIDTH, CONV_CH)
                P["od_norm_g"] = spg[:, 32, :].reshape(1, D_MODEL)
                P["od_sgu_ln_g"] = spg[:, 33, 0:64].reshape(1, D_CH)
                P["od_sgu_ln_b"] = spg[:, 34, 0:64].reshape(1, D_CH)
            if gi == GATHER_LATE_AT:
                ag_late.append(exchange_start([late_act[k] for k in late], False, "ag_start_late", after=lands[0])[0])
            for k, land in zip(GATHER_GROUPS[gi], lands):
                full_w[k] = _unshard(k[0], land)
        return full_w[(name, layer)]

    pending, rs_started, spatial = {}, [], []

    def emit(name, layer, grad):
        if name == "od_spatial":
            handles, token = exchange_start([grad], False, "ar_start_b")
            spatial.append(handles)
            return token
        pending[(name, layer)] = grad
        for gi, grp in enumerate(REDUCE_GROUPS):
            if (name, layer) in grp and all(k in pending for k in grp):
                handles, token = exchange_start([_shard_slots(k[0], pending[k]).astype(ACT) for k in grp], True,
                                                "rs_start%d" % gi)
                rs_started.append((gi, handles))
                return token
        return None

    loss_part, dx, small = _local_step(x[0], loss_target[0], weight, emit, P)
    loss = lax.psum(loss_part[0, 0], AXES)

    wide = ["ev_norm_g", "ev_sinks", "ev_conv_w", "ev_conv_b", "ev_conv_ln_g", "ev_conv_ln_b", "od_norm_g",
            "od_sgu_ln_g", "od_sgu_ln_b", "ffn_norm_g", "final_norm_g"]
    blk_a = jnp.concatenate(
        [jnp.pad(small[n], ((0, 0), (0, D_MODEL - small[n].shape[1]))) for n in wide], axis=0)
    blk_a = jnp.pad(blk_a, ((0, 48 - blk_a.shape[0]), (0, 0)))
    ar_a, ar_token = exchange_start([blk_a], False, "ar_start_a")

    results = {}
    for gi, handles in rs_started:
        lands = exchange_wait(handles, ar_token, True, "rs_wait%d" % gi)
        for (n, l), land in zip(REDUCE_GROUPS[gi], lands):
            results[n] = sum_adamw(land, wts[n], mom[n], vel[n], l, results.get(n), "adamw_%s%d" % (n, l))
    out_g, out_d, out_m, out_v = {}, {}, {}, {}
    for n in BIG:
        out_g[n], out_d[n], out_m[n], out_v[n] = results[n]

    last = results[REDUCE_GROUPS[-1][-1][0]][0]
    sum_b = sum_slots(exchange_wait(spatial[0], last, False, "ar_wait_b")[0], F32, "ar_sum_b")
    sum_a = sum_slots(exchange_wait(ar_a, last, False, "ar_wait_a")[0], F32, "ar_sum_a")
    full = {}
    off = 0
    for n in wide:
        r_, c_ = small[n].shape
        full[n] = sum_a[off:off + r_, 0:c_]
        off += r_
    full["od_spatial_w"] = sum_b[0:D_GROUPS * BLOCK]
    full["od_spatial_b"] = sum_b[D_GROUPS * BLOCK:D_GROUPS * BLOCK + D_GROUPS]
    full["ev_conv_w"] = lax.dynamic_slice_in_dim(full["ev_conv_w"], me * 64, 64, axis=1)
    full["od_norm_g"] = lax.dynamic_slice_in_dim(full["od_norm_g"], me * 128, 128, axis=1)
    full["od_sgu_ln_g"] = lax.dynamic_slice_in_dim(full["od_sgu_ln_g"], me * 64, 64, axis=1)
    full["od_sgu_ln_b"] = lax.dynamic_slice_in_dim(full["od_sgu_ln_b"], me * 64, 64, axis=1)

    small_names = [n for n in names if n not in BIG]
    view = {n: ((-1, wts[n].shape[-1]) if wts[n].ndim > 1 else (1, -1)) for n in small_names}
    ds, ms, vs = adamw_small([wts[n].reshape(view[n]) for n in small_names],
                             [full[n].reshape(view[n]) for n in small_names],
                             [mom[n].reshape(view[n]) for n in small_names],
                             [vel[n].reshape(view[n]) for n in small_names], "adamw_small")
    for i, n in enumerate(small_names):
        shp = wts[n].shape
        out_g[n], out_d[n], out_m[n], out_v[n] = (full[n].reshape(shp), ds[i].reshape(shp), ms[i].reshape(shp),
                                                  vs[i].reshape(shp))

    return (loss, dx[None], *[out_g[n] for n in names], *[out_d[n] for n in names],
            *[out_m[n] for n in names], *[out_v[n] for n in names])
```

```python
import functools
import math

import jax
import jax.numpy as jnp
from jax import lax
from jax.experimental import pallas as pl
from jax.experimental.pallas import tpu as pltpu

F32 = jnp.float32
ACT = jnp.bfloat16

D_MODEL = 1024
HEAD_DIM = 64
N_HEADS = 8
ATT_W = N_HEADS * HEAD_DIM
A_KV_HEADS = 2
CONV_CH = 512
CONV_WIDTH = 31
CONV_HALO = 32
D_CH = 512
D_GROUPS = 8
BLOCK = 128
D_FF = 2816
ROT_DIM = 16
ROPE_THETA = 500000.0
RMS_EPS = 1e-6
LN_EPS = 1e-5
DILATED = ((128, 1), (512, 4), (2048, 16))
NEG = -1e30
LANES = 128

ADAM_LR = 0.001
ADAM_B1 = 0.9
ADAM_B2 = 0.999
ADAM_EPS = 1e-08
ADAM_WD = 0.01
ADAM_STEP = 10

V7X_VMEM_LIMIT = 56 * 1024 * 1024
SMALL_BLOCK_BYTES = 6 * 1024 * 1024
N_DEV = 8

NN = (((1,), (0,)), ((), ()))
NT = (((1,), (1,)), ((), ()))
TN = (((0,), (0,)), ((), ()))
MESH = pl.DeviceIdType.MESH
AXES = ("x", "y", "c")


def _dot(a, b, dims):
    return lax.dot_general(a, b, dims, preferred_element_type=F32)


def _cp(*sem):
    return pltpu.CompilerParams(dimension_semantics=sem if sem else None,
                                vmem_limit_bytes=V7X_VMEM_LIMIT)


def _tile(n, prefs):
    for p in prefs:
        if n % p == 0:
            return p
    return n


def _sigmoid(x):
    return 1.0 / (1.0 + jnp.exp(-x))


def _rowspec(tm, w, col=0):
    return pl.BlockSpec((tm, w), lambda i, col=col: (i, col))


def _fullspec(shape):
    nd = len(shape)
    return pl.BlockSpec(shape, lambda *a, nd=nd: (0,) * nd)


def _rope_tables(seq):
    half = ROT_DIM // 2
    inv_freq = ROPE_THETA ** (-jnp.arange(half, dtype=F32) * (2.0 / ROT_DIM))
    ang = jnp.arange(seq, dtype=jnp.int32).astype(F32)[:, None] * inv_freq[None, :]
    cos, sin = jnp.cos(ang), jnp.sin(ang)
    lane = jnp.arange(LANES)
    jm = lane % HEAD_DIM
    idx = jm % half
    c = jnp.where(jm[None, :] < ROT_DIM, cos[:, idx], 1.0)
    sa = jnp.where(jm[None, :] < half, -sin[:, idx], 0.0)
    sb = jnp.where((jm[None, :] >= half) & (jm[None, :] < ROT_DIM), sin[:, idx], 0.0)
    return c.astype(F32), sa.astype(F32), sb.astype(F32)


def _rope(x, c, sa, sb):
    return x * c + pltpu.roll(x, LANES - 8, 1) * sa + pltpu.roll(x, 8, 1) * sb


def _rope_t(d, c, sa, sb):
    return d * c + pltpu.roll(d * sa, 8, 1) + pltpu.roll(d * sb, LANES - 8, 1)


def rmsnorm(h, g, name):
    s = h.shape[0]
    tm = _tile(s, (512,))

    def body(h_ref, g_ref, o_ref):
        x = h_ref[...]
        r = lax.rsqrt(jnp.mean(x * x, axis=-1, keepdims=True) + RMS_EPS)
        o_ref[...] = (x * r * g_ref[...]).astype(o_ref.dtype)

    return pl.pallas_call(
        body, grid=(s // tm,), name=name,
        in_specs=[_rowspec(tm, D_MODEL), _fullspec((1, D_MODEL))],
        out_specs=_rowspec(tm, D_MODEL),
        out_shape=jax.ShapeDtypeStruct((s, D_MODEL), ACT),
        compiler_params=_cp("parallel"))(h, g)


def _fold_perm(tm, d, inverse, cache):
    key = (tm, d, inverse)
    if key not in cache:
        m = tm // d
        a = lax.broadcasted_iota(jnp.int32, (tm, tm), 1 if inverse else 0)
        b = lax.broadcasted_iota(jnp.int32, (tm, tm), 0 if inverse else 1)
        src = (a & (m - 1)) * d + (a >> (m.bit_length() - 1))
        cache[key] = (b == src).astype(ACT)
    return cache[key]


def _fold_rows(x, d, cache):
    tm = x.shape[0]
    m = tm // d
    p = _dot(_fold_perm(tm, d, False, cache), x, NN).astype(ACT)
    return jnp.concatenate([p[r * m:(r + 1) * m] for r in range(d)], axis=1)


def _unfold_rows(blk, d, cache):
    w = blk.shape[1] // d
    stacked = jnp.concatenate([blk[:, r * w:(r + 1) * w] for r in range(d)], axis=0)
    return _dot(_fold_perm(stacked.shape[0], d, True, cache), stacked, NN)


def _unfold_rows_f32(blk, d, cache):
    hi = blk.astype(ACT)
    r1 = blk - hi.astype(F32)
    mid = r1.astype(ACT)
    lo = (r1 - mid.astype(F32)).astype(ACT)
    return _unfold_rows(hi, d, cache) + _unfold_rows(mid, d, cache) + _unfold_rows(lo, d, cache)


def inproj(n, w, tabs, nqk, splits, fold_dils, name):
    s = n.shape[0]
    ntot = w.shape[1]
    assert sum(splits) == ntot and splits[0] == nqk
    tm = _tile(s, (256,) if fold_dils else (512,))
    ns = len(splits)

    def body(n_ref, w_ref, c_ref, sa_ref, sb_ref, *outs):
        res = _dot(n_ref[...], w_ref[...], NN)
        c, sa, sb = c_ref[...], sa_ref[...], sb_ref[...]
        for g in range(nqk // LANES):
            x = res[:, g * LANES:(g + 1) * LANES]
            outs[0][:, g * LANES:(g + 1) * LANES] = _rope(x, c, sa, sb).astype(ACT)
        off = nqk
        for o_ref, wd in zip(outs[1:ns], splits[1:]):
            o_ref[...] = res[:, off:off + wd].astype(ACT)
            off += wd
        cache = {}
        for i, d in enumerate(fold_dils):
            outs[ns + 2 * i][...] = _fold_rows(outs[0][...], d, cache)
            outs[ns + 2 * i + 1][...] = _fold_rows(outs[1][...], d, cache)

    out_specs = [_rowspec(tm, wd) for wd in splits]
    out_shape = [jax.ShapeDtypeStruct((s, wd), ACT) for wd in splits]
    for d in fold_dils:
        for wd in splits[:2]:
            out_specs.append(_rowspec(tm // d, d * wd))
            out_shape.append(jax.ShapeDtypeStruct((s // d, d * wd), ACT))
    return pl.pallas_call(
        body, grid=(s // tm,), name=name,
        in_specs=[_rowspec(tm, D_MODEL), _fullspec((D_MODEL, ntot))] + [_rowspec(tm, LANES)] * 3,
        out_specs=out_specs, out_shape=out_shape,
        compiler_params=_cp("parallel"))(n, w, *tabs)


def ffn_up(n, wg, wu, name):
    s = n.shape[0]
    f = wg.shape[1]
    tm = _tile(s, (1024, 512))
    tf = _tile(f, (1408, 512, 256, 128))

    def body(n_ref, wg_ref, wu_ref, g_ref, u_ref, a_ref):
        a = n_ref[...]
        g = _dot(a, wg_ref[...], NN)
        u = _dot(a, wu_ref[...], NN)
        sg = _sigmoid(g)
        silu = g * sg
        g_ref[...] = (u * (sg * (1.0 + g * (1.0 - sg)))).astype(ACT)
        u_ref[...] = silu.astype(ACT)
        a_ref[...] = (silu * u).astype(ACT)

    wspec = pl.BlockSpec((D_MODEL, tf), lambda j, i: (0, j))
    ospec = pl.BlockSpec((tm, tf), lambda j, i: (i, j))
    return pl.pallas_call(
        body, grid=(f // tf, s // tm), name=name,
        in_specs=[pl.BlockSpec((tm, D_MODEL), lambda j, i: (i, 0)), wspec, wspec],
        out_specs=[ospec] * 3,
        out_shape=[jax.ShapeDtypeStruct((s, f), ACT)] * 3,
        compiler_params=_cp("parallel", "parallel"))(n, wg, wu)


def mm_res(parts, w, h, gnext, name):
    s = h.shape[0]
    tm = _tile(s, (512,))
    widths = [p.shape[1] for p in parts]
    assert sum(widths) == w.shape[0]
    np_ = len(parts)

    def body(*refs):
        p_refs = refs[:np_]
        w_ref, h_ref = refs[np_], refs[np_ + 1]
        rest = refs[np_ + 2:]
        acc = h_ref[...]
        off = 0
        for p_ref, wd in zip(p_refs, widths):
            acc = acc + _dot(p_ref[...], w_ref[off:off + wd, :], NN)
            off += wd
        if gnext is None:
            rest[0][...] = acc
        else:
            g_ref, ho_ref, no_ref = rest
            ho_ref[...] = acc
            r = lax.rsqrt(jnp.mean(acc * acc, axis=-1, keepdims=True) + RMS_EPS)
            no_ref[...] = (acc * r * g_ref[...]).astype(ACT)

    in_specs = [_rowspec(tm, wd) for wd in widths] + [_fullspec(w.shape), _rowspec(tm, D_MODEL)]
    args = list(parts) + [w, h]
    out_specs = [_rowspec(tm, D_MODEL)]
    out_shape = [jax.ShapeDtypeStruct((s, D_MODEL), F32)]
    if gnext is not None:
        in_specs.append(_fullspec((1, D_MODEL)))
        args.append(gnext)
        out_specs.append(_rowspec(tm, D_MODEL))
        out_shape.append(jax.ShapeDtypeStruct((s, D_MODEL), ACT))
    out = pl.pallas_call(
        body, grid=(s // tm,), name=name, in_specs=in_specs, out_specs=out_specs,
        out_shape=out_shape, compiler_params=_cp("parallel"))(*args)
    return (out[0], None) if gnext is None else (out[0], out[1])


def _band_mask(n, max_dist):
    qi = lax.broadcasted_iota(jnp.int32, (BLOCK, 2 * BLOCK), 0)
    kj = lax.broadcasted_iota(jnp.int32, (BLOCK, 2 * BLOCK), 1)
    dist = qi + BLOCK - kj
    valid = jnp.logical_and(dist >= 0, dist <= max_dist)
    return jnp.logical_and(valid, jnp.logical_or(kj >= BLOCK, n > 0))


def _band_mask_t(n, max_dist):
    kj = lax.broadcasted_iota(jnp.int32, (2 * BLOCK, BLOCK), 0)
    qi = lax.broadcasted_iota(jnp.int32, (2 * BLOCK, BLOCK), 1)
    dist = qi + BLOCK - kj
    valid = jnp.logical_and(dist >= 0, dist <= max_dist)
    return jnp.logical_and(valid, jnp.logical_or(kj >= BLOCK, n > 0))


ATT_FWD_BLOCKS = 2


def _fold(a, d):
    return a.reshape(a.shape[0] // d, d * a.shape[1])


def attn_fwd(qk, v, *, d, hkv, max_dist, sink, out_dtype, name, folded=None, keep_folded=False):
    s = qk.shape[0]
    kvw = hkv * HEAD_DIM
    wqk = ATT_W + kvw
    assert qk.shape[1] == wqk and (d == 1 or (wqk % ATT_W == 0 and wqk % kvw == 0))
    assert sink is None or max_dist == BLOCK - 1
    nb = s // d // BLOCK
    grp = N_HEADS // hkv
    qpb, kpb, koff = wqk // ATT_W, wqk // kvw, ATT_W // kvw

    qb = ATT_FWD_BLOCKS if nb % ATT_FWD_BLOCKS == 0 else 1

    def body(*refs):
        if sink is None:
            q_ref, kc_ref, kp_ref, vc_ref, vp_ref, o_ref, l_ref, o_buf = refs
        else:
            q_ref, kc_ref, kp_ref, vc_ref, vp_ref, s_ref, o_ref, l_ref, o_buf = refs
        n = pl.program_id(1)
        qsl = [slice(h * HEAD_DIM, (h + 1) * HEAD_DIM) for h in range(N_HEADS)]
        ksl = [slice((h // grp) * HEAD_DIM, (h // grp + 1) * HEAD_DIM) for h in range(N_HEADS)]
        head_row = lax.broadcasted_iota(jnp.int32, (SUBLANES, BLOCK), 0)
        if sink is not None:
            sink_row = lax.broadcasted_iota(jnp.int32, (2 * BLOCK, BLOCK), 0) == 0
        for j in range(qb):
            rows = slice(j * BLOCK, (j + 1) * BLOCK)
            before = slice((j - 1) * BLOCK, j * BLOCK)
            valid = _band_mask_t(n * qb + j, max_dist)
            kk = jnp.concatenate([kp_ref[...] if j == 0 else kc_ref[before, :], kc_ref[rows, :]], axis=0)
            vv = jnp.concatenate([vp_ref[...] if j == 0 else vc_ref[before, :], vc_ref[rows, :]], axis=0)
            scores = []
            for h in range(N_HEADS):
                q = q_ref[rows, qsl[h]] * 0.125
                scores.append(_dot(kk[:, ksl[h]], q, NT))
            probs = []
            lse8 = jnp.zeros((SUBLANES, BLOCK), F32)
            for h in range(N_HEADS):
                sc = jnp.where(valid, scores[h], NEG)
                if sink is not None:
                    sc = jnp.where(sink_row, s_ref[h:h + 1, 0:1], sc)
                m = jnp.max(sc, axis=0, keepdims=True)
                p = jnp.exp(sc - m)
                l = jnp.sum(p, axis=0, keepdims=True)
                if sink is not None:
                    p = jnp.where(sink_row, 0.0, p)
                lse8 = jnp.where(head_row == h, m + jnp.log(l), lse8)
                probs.append((p * (1.0 / l)).astype(ACT))
            l_ref[rows, :] = jnp.concatenate([lse8, jnp.zeros((BLOCK - SUBLANES, BLOCK), F32)], axis=0).T
            for h in range(N_HEADS):
                o_buf[rows, qsl[h]] = _dot(probs[h], vv[:, ksl[h]], TN)
        o_ref[...] = o_buf[...].astype(o_ref.dtype)

    prev = lambda n: jnp.maximum(n * qb - 1, 0)
    in_specs = [
        pl.BlockSpec((qb * BLOCK, ATT_W), lambda r, n: (n, r * qpb)),
        pl.BlockSpec((qb * BLOCK, kvw), lambda r, n: (n, r * kpb + koff)),
        pl.BlockSpec((BLOCK, kvw), lambda r, n: (prev(n), r * kpb + koff)),
        pl.BlockSpec((qb * BLOCK, kvw), lambda r, n: (n, r)),
        pl.BlockSpec((BLOCK, kvw), lambda r, n: (prev(n), r)),
    ]
    qkf, vf = (_fold(qk, d), _fold(v, d)) if folded is None else folded
    args = [qkf, qkf, qkf, vf, vf]
    if sink is not None:
        in_specs.append(_fullspec((N_HEADS, LANES)))
        args.append(sink)
    ospec = pl.BlockSpec((qb * BLOCK, ATT_W), lambda r, n: (n, r))
    lspec = pl.BlockSpec((qb * BLOCK, LANES), lambda r, n: (n, r))
    o, lse = pl.pallas_call(
        body, grid=(d, nb // qb), name=name, in_specs=in_specs, out_specs=[ospec, lspec],
        out_shape=[jax.ShapeDtypeStruct((s // d, d * ATT_W), out_dtype),
                   jax.ShapeDtypeStruct((s // d, d * LANES), F32)],
        scratch_shapes=[pltpu.VMEM((qb * BLOCK, ATT_W), F32)],
        compiler_params=_cp("parallel", "parallel"))(*args)
    return (o, lse) if keep_folded else (o.reshape(s, ATT_W), lse.reshape(s, LANES))


def attn_bwd(qk, v, do_src, do_col, o, lse, *, d, hkv, max_dist, sink, out_dtype, name, folded=None,
             folded_do_o=None, keep_folded=False):
    s = qk.shape[0]
    kvw = hkv * HEAD_DIM
    wqk = ATT_W + kvw
    nb = s // d // BLOCK
    grp = N_HEADS // hkv
    qpb, kpb, koff = wqk // ATT_W, wqk // kvw, ATT_W // kvw
    dob = do_src.shape[1] // ATT_W
    has_sink = sink is not None

    def body(*refs):
        refs = list(refs)
        q_ref, kc_ref, kp_ref, vc_ref, vp_ref, do_ref, o_ref, l_ref = refs[:8]
        pos = 8
        if has_sink:
            s_ref = refs[pos]
            pos += 1
        dq_ref, dk_ref, dv_ref = refs[pos:pos + 3]
        pos += 3
        if has_sink:
            ds_ref = refs[pos]
            pos += 1
        ck_ref, cv_ref, dq_buf, dk_buf, dv_buf = refs[pos:pos + 5]
        r_id = pl.program_id(0)
        n = pl.program_id(1)

        @pl.when(n == 0)
        def _():
            ck_ref[...] = jnp.zeros_like(ck_ref)
            cv_ref[...] = jnp.zeros_like(cv_ref)

        if has_sink:
            @pl.when(jnp.logical_and(n == 0, r_id == 0))
            def _():
                ds_ref[...] = jnp.zeros_like(ds_ref)

        @pl.when(n < nb)
        def _():
            valid = _band_mask_t(n, max_dist)
            qsl = [slice(h * HEAD_DIM, (h + 1) * HEAD_DIM) for h in range(N_HEADS)]
            ksl = [slice((h // grp) * HEAD_DIM, (h // grp + 1) * HEAD_DIM) for h in range(N_HEADS)]
            kk = jnp.concatenate([kp_ref[...], kc_ref[...]], axis=0)
            vv = jnp.concatenate([vp_ref[...], vc_ref[...]], axis=0)
            qs, first = [], []
            for h in range(N_HEADS):
                q = q_ref[:, qsl[h]] * 0.125
                qs.append(q)
                first.append((_dot(kk[:, ksl[h]], q, NT), _dot(vv[:, ksl[h]], do_ref[:, qsl[h]], NT)))
            lse_t = l_ref[...].T
            prod = do_ref[...].astype(F32) * o_ref[...].astype(F32)
            hi = prod.astype(ACT)
            lo = (prod - hi.astype(F32)).astype(ACT)
            col = lax.broadcasted_iota(jnp.int32, (LANES, ATT_W), 1)
            row = lax.broadcasted_iota(jnp.int32, (LANES, ATT_W), 0)
            head_of = jnp.logical_and(col >= row * HEAD_DIM, col < (row + 1) * HEAD_DIM).astype(ACT)
            e_t = _dot(head_of, hi, NT) + _dot(head_of, lo, NT)
            mid = []
            for h in range(N_HEADS):
                s_t, dp_t = first[h]
                lse_h, e_h = lse_t[h:h + 1, :], e_t[h:h + 1, :]
                p_t = jnp.exp(jnp.where(valid, s_t, NEG) - lse_h)
                mid.append(((p_t * (dp_t - e_h)).astype(ACT), p_t.astype(ACT)))
                if has_sink:
                    sk = s_ref[h:h + 1, 0:1]
                    dsk = -jnp.sum(jnp.exp(sk - lse_h) * e_h, axis=1, keepdims=True)
                    ds_ref[h:h + 1, :] = ds_ref[h:h + 1, :] + dsk
            dkk = [None] * hkv
            dvv = [None] * hkv
            for h in range(N_HEADS):
                kh = h // grp
                ds_t, p_t = mid[h]
                dq_buf[:, qsl[h]] = _dot(ds_t, kk[:, ksl[h]], TN) * 0.125
                for lst, val in ((dkk, _dot(ds_t, qs[h], NN)), (dvv, _dot(p_t, do_ref[:, qsl[h]], NN))):
                    lst[kh] = val if lst[kh] is None else lst[kh] + val
            for kh in range(hkv):
                ks = slice(kh * HEAD_DIM, (kh + 1) * HEAD_DIM)
                dk_buf[:, ks] = ck_ref[:, ks] + dkk[kh][:BLOCK]
                dv_buf[:, ks] = cv_ref[:, ks] + dvv[kh][:BLOCK]
                ck_ref[:, ks] = dkk[kh][BLOCK:]
                cv_ref[:, ks] = dvv[kh][BLOCK:]
            dq_ref[...] = dq_buf[...].astype(dq_ref.dtype)
            dk_ref[...] = dk_buf[...].astype(dk_ref.dtype)
            dv_ref[...] = dv_buf[...].astype(dv_ref.dtype)

        @pl.when(n == nb)
        def _():
            dk_ref[...] = ck_ref[...].astype(dk_ref.dtype)
            dv_ref[...] = cv_ref[...].astype(dv_ref.dtype)

    qrow = lambda n: jnp.minimum(n, nb - 1)
    prow = lambda n: jnp.maximum(jnp.minimum(n, nb - 1) - 1, 0)
    krow = lambda n: jnp.maximum(n - 1, 0)
    in_specs = [
        pl.BlockSpec((BLOCK, ATT_W), lambda r, n: (qrow(n), r * qpb)),
        pl.BlockSpec((BLOCK, kvw), lambda r, n: (qrow(n), r * kpb + koff)),
        pl.BlockSpec((BLOCK, kvw), lambda r, n: (prow(n), r * kpb + koff)),
        pl.BlockSpec((BLOCK, kvw), lambda r, n: (qrow(n), r)),
        pl.BlockSpec((BLOCK, kvw), lambda r, n: (prow(n), r)),
        pl.BlockSpec((BLOCK, ATT_W), lambda r, n: (qrow(n), r * dob + do_col)),
        pl.BlockSpec((BLOCK, ATT_W), lambda r, n: (qrow(n), r)),
        pl.BlockSpec((BLOCK, LANES), lambda r, n: (qrow(n), r)),
    ]
    qkf, vf = (_fold(qk, d), _fold(v, d)) if folded is None else folded
    dof, of = (_fold(do_src, d), _fold(o, d)) if folded_do_o is None else folded_do_o
    args = [qkf, qkf, qkf, vf, vf, dof, of, _fold(lse, d)]
    if has_sink:
        in_specs.append(_fullspec((N_HEADS, LANES)))
        args.append(sink)
    qspec = pl.BlockSpec((BLOCK, ATT_W), lambda r, n: (qrow(n), r))
    kspec = pl.BlockSpec((BLOCK, kvw), lambda r, n: (krow(n), r))
    out_specs = [qspec, kspec, kspec]
    out_shape = [jax.ShapeDtypeStruct((s // d, d * ATT_W), out_dtype),
                 jax.ShapeDtypeStruct((s // d, d * kvw), out_dtype),
                 jax.ShapeDtypeStruct((s // d, d * kvw), out_dtype)]
    if has_sink:
        out_specs.append(_fullspec((N_HEADS, LANES)))
        out_shape.append(jax.ShapeDtypeStruct((N_HEADS, LANES), F32))
    out = pl.pallas_call(
        body, grid=(d, nb + 1), name=name, in_specs=in_specs, out_specs=out_specs,
        out_shape=out_shape,
        scratch_shapes=[pltpu.VMEM((BLOCK, kvw), F32), pltpu.VMEM((BLOCK, kvw), F32),
                        pltpu.VMEM((BLOCK, ATT_W), F32), pltpu.VMEM((BLOCK, kvw), F32), pltpu.VMEM((BLOCK, kvw), F32)],
        compiler_params=_cp("arbitrary", "arbitrary"))(*args)
    res = list(out[:3]) if keep_folded else [out[0].reshape(s, ATT_W), out[1].reshape(s, kvw), out[2].reshape(s, kvw)]
    if has_sink:
        res.append(out[3])
    return res


def combine_fwd(os_, lses, dils, name):
    s = os_[0].shape[0] * dils[0]
    tm = _tile(s, (256,))
    fold_dils = [d for d in dils if d > 1]

    def body(o1, o2, o3, l1, l2, l3, c_ref, l_ref, *rest):
        c_buf = rest[-1]
        cache = {}
        o = [r[...].astype(F32) if d == 1 else _unfold_rows(r[...], d, cache) for r, d in zip((o1, o2, o3), dils)]
        a, b, c = [r[...] if d == 1 else _unfold_rows_f32(r[...], d, cache) for r, d in zip((l1, l2, l3), dils)]
        m = jnp.maximum(jnp.maximum(a, b), c)
        wa, wb, wc = jnp.exp(a - m), jnp.exp(b - m), jnp.exp(c - m)
        tot = wa + wb + wc
        l_ref[...] = m + jnp.log(tot)
        rt = 1.0 / tot
        wa, wb, wc = wa * rt, wb * rt, wc * rt
        for h in range(N_HEADS):
            cs = slice(h * HEAD_DIM, (h + 1) * HEAD_DIM)
            c_buf[:, cs] = (wa[:, h:h + 1] * o[0][:, cs] + wb[:, h:h + 1] * o[1][:, cs] + wc[:, h:h + 1] * o[2][:, cs])
        mix = c_buf[...].astype(ACT)
        c_ref[...] = mix
        for f_ref, d in zip(rest[:-1], fold_dils):
            f_ref[...] = _fold_rows(mix, d, cache)

    return pl.pallas_call(
        body, grid=(s // tm,), name=name,
        in_specs=[_rowspec(tm // d, d * ATT_W) for d in dils] + [_rowspec(tm // d, d * LANES) for d in dils],
        out_specs=[_rowspec(tm, ATT_W), _rowspec(tm, LANES)] + [_rowspec(tm // d, d * ATT_W) for d in fold_dils],
        out_shape=[jax.ShapeDtypeStruct((s, ATT_W), ACT), jax.ShapeDtypeStruct((s, LANES), F32)]
        + [jax.ShapeDtypeStruct((s // d, d * ATT_W), ACT) for d in fold_dils],
        scratch_shapes=[pltpu.VMEM((tm, ATT_W), F32)],
        compiler_params=_cp("parallel"))(*os_, *lses)


def assemble(parts, tabs, name):
    terms_of = [[t if isinstance(t, tuple) else (t, 1) for t in terms] for terms, _ in parts]
    flat = [t for ts in terms_of for t in ts]
    s = terms_of[0][0][0].shape[0] * terms_of[0][0][1]
    tm = _tile(s, (256,) if any(d > 1 for _, d in flat) else (512,))
    widths = [ts[0][0].shape[1] // ts[0][1] for ts in terms_of]
    flags = [f for _, f in parts]

    def body(*refs):
        c_ref, sa_ref, sb_ref, o_ref = refs[len(flat):]
        c, sa, sb = c_ref[...], sa_ref[...], sb_ref[...]
        off = 0
        first = 0
        cache = {}
        for wd, ts, fl in zip(widths, terms_of, flags):
            t_refs = refs[first:first + len(ts)]
            first += len(ts)
            x = None
            for t_ref, (_, d) in zip(t_refs, ts):
                t = t_ref[...].astype(F32) if d == 1 else _unfold_rows(t_ref[...], d, cache)
                x = t if x is None else x + t
            for g in range(wd // LANES):
                cols = slice(g * LANES, (g + 1) * LANES)
                y = _rope_t(x[:, cols], c, sa, sb) if fl else x[:, cols]
                o_ref[:, off + g * LANES:off + (g + 1) * LANES] = y.astype(ACT)
            off += wd

    tot = sum(widths)
    return pl.pallas_call(
        body, grid=(s // tm,), name=name,
        in_specs=[_rowspec(tm // d, a.shape[1]) for a, d in flat] + [_rowspec(tm, LANES)] * 3,
        out_specs=_rowspec(tm, tot), out_shape=jax.ShapeDtypeStruct((s, tot), ACT),
        compiler_params=_cp("parallel"))(*[a for a, _ in flat], *tabs)


def _ln_stats(x):
    mu = jnp.mean(x, axis=-1, keepdims=True)
    xc = x - mu
    var = jnp.mean(xc * xc, axis=-1, keepdims=True)
    rstd = lax.rsqrt(var + LN_EPS)
    return xc * rstd, rstd


SUBLANES = 8


TAP_ROWS = 32


def _tap_sum(buf, cw_ref, offsets, tm, res_ref):
    for r0 in range(0, tm, TAP_ROWS):
        acc = None
        for ph in range(SUBLANES):
            taps = [j for j, off in enumerate(offsets) if off % SUBLANES == ph]
            if not taps:
                continue
            rows = TAP_ROWS if ph == 0 else TAP_ROWS + SUBLANES
            part = None
            for j in taps:
                term = cw_ref[j:j + 1, :] * buf[pl.ds(offsets[j] - ph + r0, rows), :]
                part = term if part is None else part + term
            part = part[ph:ph + TAP_ROWS]
            acc = part if acc is None else acc + part
        res_ref[pl.ds(r0, TAP_ROWS), :] = acc


def conv_fwd(ga, gb, cw, cb, lg, lb, name):
    s = ga.shape[0]
    tm = _tile(s, (512,))
    hb = tm // CONV_HALO

    def body(ga_ref, gb_ref, gah_ref, gbh_ref, cw_ref, cb_ref, lg_ref, lb_ref, c_ref, c1_ref, buf):
        i = pl.program_id(0)
        halo = gah_ref[...].astype(F32) * _sigmoid(gbh_ref[...].astype(F32))
        buf[0:CONV_HALO, :] = jnp.where(i > 0, halo, 0.0)
        buf[CONV_HALO:, :] = ga_ref[...].astype(F32) * _sigmoid(gb_ref[...].astype(F32))
        first = CONV_HALO - (CONV_WIDTH - 1)
        _tap_sum(buf, cw_ref, [first + j for j in range(CONV_WIDTH)], tm, c1_ref)
        acc = c1_ref[...] + cb_ref[...]
        c1_ref[...] = acc
        xh, _ = _ln_stats(acc)
        y = xh * lg_ref[...] + lb_ref[...]
        c_ref[...] = (y * _sigmoid(y)).astype(ACT)

    hspec = pl.BlockSpec((CONV_HALO, CONV_CH), lambda i: (jnp.maximum(i * hb - 1, 0), 0))
    vec = _fullspec((1, CONV_CH))
    spec = _rowspec(tm, CONV_CH)
    return pl.pallas_call(
        body, grid=(s // tm,), name=name,
        in_specs=[spec, spec, hspec, hspec, _fullspec((CONV_WIDTH, CONV_CH)), vec, vec, vec],
        out_specs=[spec, spec],
        out_shape=[jax.ShapeDtypeStruct((s, CONV_CH), ACT), jax.ShapeDtypeStruct((s, CONV_CH), F32)],
        scratch_shapes=[pltpu.VMEM((tm + CONV_HALO, CONV_CH), F32)],
        compiler_params=_cp("parallel"))(ga, gb, ga, gb, cw, cb, lg, lb)


def conv_bwd_ln(c1, dsrc, dcol, lg, lb, name):
    s = c1.shape[0]
    tm = _tile(s, (512,))

    def body(c1_ref, d_ref, lg_ref, lb_ref, o_ref, dg_ref, db_ref):
        @pl.when(pl.program_id(0) == 0)
        def _():
            dg_ref[...] = jnp.zeros_like(dg_ref)
            db_ref[...] = jnp.zeros_like(db_ref)

        xh, rstd = _ln_stats(c1_ref[...].astype(F32))
        y = xh * lg_ref[...] + lb_ref[...]
        sg = _sigmoid(y)
        dy = d_ref[...].astype(F32) * (sg * (1.0 + y * (1.0 - sg)))
        dg_ref[0:1, :] = dg_ref[0:1, :] + jnp.sum(dy * xh, axis=0, keepdims=True)
        db_ref[0:1, :] = db_ref[0:1, :] + jnp.sum(dy, axis=0, keepdims=True)
        dxh = dy * lg_ref[...]
        o_ref[...] = rstd * (dxh - jnp.mean(dxh, axis=-1, keepdims=True)
                             - xh * jnp.mean(dxh * xh, axis=-1, keepdims=True))

    vec = _fullspec((1, CONV_CH))
    acc = _fullspec((8, CONV_CH))
    return pl.pallas_call(
        body, grid=(s // tm,), name=name,
        in_specs=[_rowspec(tm, CONV_CH), _rowspec(tm, CONV_CH, dcol), vec, vec],
        out_specs=[_rowspec(tm, CONV_CH), acc, acc],
        out_shape=[jax.ShapeDtypeStruct((s, CONV_CH), F32)] + [jax.ShapeDtypeStruct((8, CONV_CH), F32)] * 2,
        compiler_params=_cp("arbitrary"))(c1, dsrc, lg, lb)


def conv_bwd_conv(dc1, ga, gb, cw, name):
    s = ga.shape[0]
    tm = _tile(s, (256,))
    hb = tm // CONV_HALO
    nt = s // tm
    last_h = s // CONV_HALO - 1
    first = CONV_HALO - (CONV_WIDTH - 1)


    def rows8(x):
        return jnp.sum(x.reshape(x.shape[0] // SUBLANES, SUBLANES, CONV_CH), axis=0)

    def body(d_ref, dn_ref, ga_ref, gb_ref, gah_ref, gbh_ref, cw_ref,
             dga_ref, dgb_ref, dw_ref, db_ref, dbuf, cbuf, sbuf):
        i = pl.program_id(0)

        @pl.when(i == 0)
        def _():
            dw_ref[...] = jnp.zeros_like(dw_ref)
            db_ref[...] = jnp.zeros_like(db_ref)

        d = d_ref[...]
        dbuf[0:tm, :] = d
        dbuf[tm:, :] = jnp.where(i < nt - 1, dn_ref[...], 0.0)
        halo = gah_ref[...].astype(F32) * _sigmoid(gbh_ref[...].astype(F32))
        cbuf[0:CONV_HALO, :] = jnp.where(i > 0, halo, 0.0)
        a = ga_ref[...].astype(F32)
        sg = _sigmoid(gb_ref[...].astype(F32))
        cbuf[CONV_HALO:, :] = a * sg
        for ph in range(SUBLANES):
            taps = [j for j in range(CONV_WIDTH) if (first + j) % SUBLANES == ph]
            if ph:
                sbuf[0:tm + CONV_HALO - SUBLANES, :] = cbuf[pl.ds(ph, tm + CONV_HALO - SUBLANES), :]
            src = sbuf if ph else cbuf
            for r0 in range(0, tm, TAP_ROWS):
                d_blk = dbuf[pl.ds(r0, TAP_ROWS), :]
                for j in taps:
                    tap = src[pl.ds(first + j - ph + r0, TAP_ROWS), :]
                    rows = slice(j * SUBLANES, (j + 1) * SUBLANES)
                    dw_ref[rows, :] = dw_ref[rows, :] + rows8(d_blk * tap)
        db_ref[...] = db_ref[...] + rows8(d)
        _tap_sum(dbuf, cw_ref, [CONV_WIDTH - 1 - j for j in range(CONV_WIDTH)], tm, sbuf)
        dc0 = sbuf[0:tm, :]
        dga_ref[...] = (dc0 * sg).astype(ACT)
        dgb_ref[...] = (dc0 * a * sg * (1.0 - sg)).astype(ACT)

    spec = _rowspec(tm, CONV_CH)
    hprev = pl.BlockSpec((CONV_HALO, CONV_CH), lambda i: (jnp.maximum(i * hb - 1, 0), 0))
    hnext = pl.BlockSpec((CONV_HALO, CONV_CH), lambda i: (jnp.minimum((i + 1) * hb, last_h), 0))
    return pl.pallas_call(
        body, grid=(nt,), name=name,
        in_specs=[spec, hnext, spec, spec, hprev, hprev, _fullspec((CONV_WIDTH, CONV_CH))],
        out_specs=[spec, spec, _fullspec((CONV_HALO * SUBLANES, CONV_CH)), _fullspec((SUBLANES, CONV_CH))],
        out_shape=[jax.ShapeDtypeStruct((s, CONV_CH), ACT)] * 2
        + [jax.ShapeDtypeStruct((CONV_HALO * SUBLANES, CONV_CH), F32), jax.ShapeDtypeStruct((SUBLANES, CONV_CH), F32)],
        scratch_shapes=[pltpu.VMEM((tm + CONV_HALO, CONV_CH), F32)] * 3,
        compiler_params=_cp("arbitrary"))(dc1, dc1, ga, gb, ga, gb, cw)


_GELU_K = math.sqrt(2.0 / math.pi)
_GELU_C = 0.044715


def _gelu(x):
    return 0.5 * x * (1.0 + jnp.tanh(_GELU_K * (x + _GELU_C * x * x * x)))


def _gelu_grad(x):
    t = jnp.tanh(_GELU_K * (x + _GELU_C * x * x * x))
    return 0.5 * (1.0 + t) + 0.5 * x * (1.0 - t * t) * _GELU_K * (1.0 + 3.0 * _GELU_C * x * x)


def _tril():
    qi = lax.broadcasted_iota(jnp.int32, (BLOCK, BLOCK), 0)
    kj = lax.broadcasted_iota(jnp.int32, (BLOCK, BLOCK), 1)
    return kj <= qi


GMLP_CHUNKS = 4


def _gmlp_weights(sw_ref, w_buf, wt_buf):
    tril = _tril()
    for g in range(D_GROUPS):
        w = jnp.where(tril, sw_ref[g], 0.0)
        w_buf[g] = w.astype(ACT)
        if wt_buf is not None:
            wt_buf[g] = w.T.astype(ACT)


def _gmlp_mix(w_buf, gn, sb_ref, m_buf):
    for c in range(GMLP_CHUNKS):
        rows = slice(c * BLOCK, (c + 1) * BLOCK)
        for g in range(D_GROUPS):
            cs = slice(g * HEAD_DIM, (g + 1) * HEAD_DIM)
            m_buf[rows, cs] = _dot(w_buf[g], gn[rows, cs], NN) + sb_ref[:, cs]


def gmlp_fwd(z, lg, lb, sw, sbx, name):
    s = z.shape[0]
    tm = GMLP_CHUNKS * BLOCK

    def body(z_ref, lg_ref, lb_ref, sw_ref, sb_ref, o_ref, w_buf, m_buf):
        @pl.when(pl.program_id(0) == 0)
        def _():
            _gmlp_weights(sw_ref, w_buf, None)

        zz = _gelu(z_ref[...].astype(F32))
        xh, _ = _ln_stats(zz[:, D_CH:])
        gn = (xh * lg_ref[...] + lb_ref[...]).astype(ACT)
        _gmlp_mix(w_buf, gn, sb_ref, m_buf)
        o_ref[...] = (zz[:, :D_CH] * m_buf[...]).astype(ACT)

    return pl.pallas_call(
        body, grid=(s // tm,), name=name,
        in_specs=[_rowspec(tm, 2 * D_CH), _fullspec((1, D_CH)), _fullspec((1, D_CH)),
                  _fullspec((D_GROUPS, BLOCK, BLOCK)), _fullspec((BLOCK, D_CH))],
        out_specs=_rowspec(tm, D_CH), out_shape=jax.ShapeDtypeStruct((s, D_CH), ACT),
        scratch_shapes=[pltpu.VMEM((D_GROUPS, BLOCK, BLOCK), ACT), pltpu.VMEM((tm, D_CH), F32)],
        compiler_params=_cp("arbitrary"))(z, lg, lb, sw, sbx)


def gmlp_bwd(z, dsrc, dcol, lg, lb, sw, sbx, name):
    s = z.shape[0]
    tm = GMLP_CHUNKS * BLOCK

    def body(z_ref, d_ref, lg_ref, lb_ref, sw_ref, sb_ref, dz_ref, dw_ref, dsb_ref, dg_ref, db_ref,
             w_buf, wt_buf, m_buf, dgn_buf):
        @pl.when(pl.program_id(0) == 0)
        def _():
            _gmlp_weights(sw_ref, w_buf, wt_buf)
            dw_ref[...] = jnp.zeros_like(dw_ref)
            dsb_ref[...] = jnp.zeros_like(dsb_ref)
            dg_ref[...] = jnp.zeros_like(dg_ref)
            db_ref[...] = jnp.zeros_like(db_ref)

        zr = z_ref[...].astype(F32)
        zz = _gelu(zr)
        u = zz[:, :D_CH]
        xh, rstd = _ln_stats(zz[:, D_CH:])
        gn = (xh * lg_ref[...] + lb_ref[...]).astype(ACT)
        dd = d_ref[...].astype(F32)
        _gmlp_mix(w_buf, gn, sb_ref, m_buf)
        dz_ref[:, :D_CH] = (dd * m_buf[...] * _gelu_grad(zr[:, :D_CH])).astype(ACT)
        dmix = dd * u
        dmix_a = dmix.astype(ACT)
        dsb = dmix[0:BLOCK]
        for c in range(1, GMLP_CHUNKS):
            dsb = dsb + dmix[c * BLOCK:(c + 1) * BLOCK]
        dsb_ref[...] = dsb_ref[...] + dsb
        tril = _tril()
        for g in range(D_GROUPS):
            cs = slice(g * HEAD_DIM, (g + 1) * HEAD_DIM)
            dw = None
            for c in range(GMLP_CHUNKS):
                rows = slice(c * BLOCK, (c + 1) * BLOCK)
                t = _dot(dmix_a[rows, cs], gn[rows, cs], NT)
                dw = t if dw is None else dw + t
                dgn_buf[rows, cs] = _dot(wt_buf[g], dmix_a[rows, cs], NN)
            dw_ref[g] = dw_ref[g] + jnp.where(tril, dw, 0.0)
        dgn = dgn_buf[...]
        dg_ref[0:1, :] = dg_ref[0:1, :] + jnp.sum(dgn * xh, axis=0, keepdims=True)
        db_ref[0:1, :] = db_ref[0:1, :] + jnp.sum(dgn, axis=0, keepdims=True)
        dxh = dgn * lg_ref[...]
        dgate = rstd * (dxh - jnp.mean(dxh, axis=-1, keepdims=True)
                        - xh * jnp.mean(dxh * xh, axis=-1, keepdims=True))
        dz_ref[:, D_CH:] = (dgate * _gelu_grad(zr[:, D_CH:])).astype(ACT)

    vec = _fullspec((1, D_CH))
    acc = _fullspec((8, D_CH))
    wshape = (D_GROUPS, BLOCK, BLOCK)
    return pl.pallas_call(
        body, grid=(s // tm,), name=name,
        in_specs=[_rowspec(tm, 2 * D_CH), _rowspec(tm, D_CH, dcol), vec, vec, _fullspec(wshape),
                  _fullspec((BLOCK, D_CH))],
        out_specs=[_rowspec(tm, 2 * D_CH), _fullspec(wshape), _fullspec((BLOCK, D_CH)), acc, acc],
        out_shape=[jax.ShapeDtypeStruct((s, 2 * D_CH), ACT), jax.ShapeDtypeStruct(wshape, F32),
                   jax.ShapeDtypeStruct((BLOCK, D_CH), F32),
                   jax.ShapeDtypeStruct((8, D_CH), F32), jax.ShapeDtypeStruct((8, D_CH), F32)],
        scratch_shapes=[pltpu.VMEM(wshape, ACT), pltpu.VMEM(wshape, ACT),
                        pltpu.VMEM((tm, D_CH), F32), pltpu.VMEM((tm, D_CH), F32)],
        compiler_params=_cp("arbitrary"))(z, dsrc, lg, lb, sw, sbx)


def _rms_bwd(dn, x, g):
    r = lax.rsqrt(jnp.mean(x * x, axis=-1, keepdims=True) + RMS_EPS)
    u = dn * g
    dx = r * u - x * (r * r * r) * jnp.mean(x * u, axis=-1, keepdims=True)
    return dx, dn * x * r


def final_loss(h, g, target, name):
    s = h.shape[0]
    tm = _tile(s, (512,))

    def body(h_ref, g_ref, t_ref, loss_ref, dh_ref, dg_ref):
        @pl.when(pl.program_id(0) == 0)
        def _():
            loss_ref[...] = jnp.zeros_like(loss_ref)
            dg_ref[...] = jnp.zeros_like(dg_ref)

        x = h_ref[...]
        r = lax.rsqrt(jnp.mean(x * x, axis=-1, keepdims=True) + RMS_EPS)
        diff = x * r * g_ref[...] - t_ref[...]
        part = jnp.sum(jnp.sum(diff * diff, axis=-1, keepdims=True), axis=0, keepdims=True)
        loss_ref[...] = loss_ref[...] + part * (0.5 / D_MODEL)
        dx, dgt = _rms_bwd(diff * (1.0 / D_MODEL), x, g_ref[...])
        dh_ref[...] = dx
        dg_ref[0:1, :] = dg_ref[0:1, :] + jnp.sum(dgt, axis=0, keepdims=True)

    spec = _rowspec(tm, D_MODEL)
    return pl.pallas_call(
        body, grid=(s // tm,), name=name,
        in_specs=[spec, _fullspec((1, D_MODEL)), spec],
        out_specs=[_fullspec((8, LANES)), spec, _fullspec((8, D_MODEL))],
        out_shape=[jax.ShapeDtypeStruct((8, LANES), F32), jax.ShapeDtypeStruct((s, D_MODEL), F32),
                   jax.ShapeDtypeStruct((8, D_MODEL), F32)],
        compiler_params=_cp("arbitrary"))(h, g, target)


def mm_nt(dy, w, name, fold_dils=(), fold_cols=0):
    s, n = dy.shape
    k = w.shape[0]
    tm = _tile(s, (256,) if fold_dils else (512,))
    tk = k if fold_dils else _tile(k, (1024, 512))

    def body(d_ref, w_ref, o_ref, *f_refs):
        res = _dot(d_ref[...].astype(ACT), w_ref[...], NT).astype(ACT)
        o_ref[...] = res
        cache = {}
        for f_ref, d in zip(f_refs, fold_dils):
            f_ref[...] = _fold_rows(res[:, :fold_cols], d, cache)

    out = pl.pallas_call(
        body, grid=(k // tk, s // tm), name=name,
        in_specs=[pl.BlockSpec((tm, n), lambda j, i: (i, 0)), pl.BlockSpec((tk, n), lambda j, i: (j, 0))],
        out_specs=[pl.BlockSpec((tm, tk), lambda j, i: (i, j))]
        + [pl.BlockSpec((tm // d, d * fold_cols), lambda j, i: (i, 0)) for d in fold_dils],
        out_shape=[jax.ShapeDtypeStruct((s, k), ACT)]
        + [jax.ShapeDtypeStruct((s // d, d * fold_cols), ACT) for d in fold_dils],
        compiler_params=_cp("parallel", "parallel"))(dy, w)
    return out if fold_dils else out[0]


def ffn_down_bwd(dh, wd, dact_dgate, dact_dup, name):
    s = dh.shape[0]
    f = wd.shape[0]
    tm = _tile(s, (1024, 512))
    tf = _tile(f, (1408, 512, 256, 128))

    def body(d_ref, w_ref, g_ref, u_ref, dg_ref, du_ref):
        dact = _dot(d_ref[...].astype(ACT), w_ref[...], NT)
        dg_ref[...] = (dact * g_ref[...].astype(F32)).astype(ACT)
        du_ref[...] = (dact * u_ref[...].astype(F32)).astype(ACT)

    tile = pl.BlockSpec((tm, tf), lambda j, i: (i, j))
    return pl.pallas_call(
        body, grid=(f // tf, s // tm), name=name,
        in_specs=[pl.BlockSpec((tm, D_MODEL), lambda j, i: (i, 0)),
                  pl.BlockSpec((tf, D_MODEL), lambda j, i: (j, 0)), tile, tile],
        out_specs=[tile, tile], out_shape=[jax.ShapeDtypeStruct((s, f), ACT)] * 2,
        compiler_params=_cp("parallel", "parallel"))(dh, wd, dact_dgate, dact_dup)


def mm_nt_rms(parts, h, g, dh, name):
    s = h.shape[0]
    tm = _tile(s, (512,))
    wspec = lambda w: pl.BlockSpec(w.shape, lambda i: (0, 0), pipeline_mode=pl.Buffered(1))
    np_ = len(parts)

    def body(*refs):
        d_refs = refs[:np_]
        w_refs = refs[np_:2 * np_]
        h_ref, g_ref, dh_ref, o_ref, dg_ref = refs[2 * np_:]

        @pl.when(pl.program_id(0) == 0)
        def _():
            dg_ref[...] = jnp.zeros_like(dg_ref)

        dn = None
        for d_ref, w_ref in zip(d_refs, w_refs):
            t = _dot(d_ref[...], w_ref[...], NT)
            dn = t if dn is None else dn + t
        dx, dgt = _rms_bwd(dn, h_ref[...], g_ref[...])
        o_ref[...] = dh_ref[...] + dx
        dg_ref[0:1, :] = dg_ref[0:1, :] + jnp.sum(dgt, axis=0, keepdims=True)

    spec = _rowspec(tm, D_MODEL)
    return pl.pallas_call(
        body, grid=(s // tm,), name=name,
        in_specs=[_rowspec(tm, d.shape[1]) for d, _ in parts] + [wspec(w) for _, w in parts]
        + [spec, _fullspec((1, D_MODEL)), spec],
        out_specs=[spec, _fullspec((8, D_MODEL))],
        out_shape=[jax.ShapeDtypeStruct((s, D_MODEL), F32), jax.ShapeDtypeStruct((8, D_MODEL), F32)],
        compiler_params=_cp("arbitrary"))(*[d for d, _ in parts], *[w for _, w in parts], h, g, dh)


def mm_tn(a, b, name):
    s, k = a.shape
    n = b.shape[1]
    tk = _tile(k, (512, 1408, 256, 128))
    tn = _tile(n, (1408, 1280, 1024, 896, 512, 256, 128))
    ts = _tile(s, (2048, 512) if b.dtype == ACT else (1024, 512))
    nt = s // ts

    def body(a_ref, b_ref, o_ref, acc):
        t = _dot(a_ref[...].astype(ACT), b_ref[...].astype(ACT), TN)
        step = pl.program_id(2)

        @pl.when(step == 0)
        def _():
            acc[...] = t

        @pl.when(step > 0)
        def _():
            acc[...] = acc[...] + t

        @pl.when(step == nt - 1)
        def _():
            o_ref[...] = acc[...].astype(ACT)

    return pl.pallas_call(
        body, grid=(k // tk, n // tn, nt), name=name,
        in_specs=[pl.BlockSpec((ts, tk), lambda i, j, t: (t, i)), pl.BlockSpec((ts, tn), lambda i, j, t: (t, j))],
        out_specs=pl.BlockSpec((tk, tn), lambda i, j, t: (i, j)),
        out_shape=jax.ShapeDtypeStruct((k, n), ACT),
        scratch_shapes=[pltpu.VMEM((tk, tn), F32)],
        compiler_params=_cp("parallel", "parallel", "arbitrary"))(a, b)


def _adamw_math(w, g, m, v):
    m = ADAM_B1 * m + (1.0 - ADAM_B1) * g
    v = ADAM_B2 * v + (1.0 - ADAM_B2) * (g * g)
    m_hat = m / (1.0 - ADAM_B1 ** ADAM_STEP)
    v_hat = v / (1.0 - ADAM_B2 ** ADAM_STEP)
    delta = -ADAM_LR * (m_hat / (jnp.sqrt(v_hat) + ADAM_EPS) + ADAM_WD * w)
    return delta, m, v


def sum_adamw(parts, w, m, v, layer, others, name):
    nl, r, c = w.shape
    tr = _tile(r, (256, 128, 64, 32, 16, 8))

    def body(p_ref, w_ref, m_ref, v_ref, *rest):
        g_ref, d_ref, mo_ref, vo_ref = rest[-4:]
        g = p_ref[0].astype(F32)
        for i in range(1, N_DEV):
            g = g + p_ref[i].astype(F32)
        d, mm, vv = _adamw_math(w_ref[...], g, m_ref[...], v_ref[...])
        g_ref[...] = g
        d_ref[...] = d
        mo_ref[...] = mm
        vo_ref[...] = vv

    spec = pl.BlockSpec((None, tr, c), lambda i: (layer, i, 0))
    in_specs = [pl.BlockSpec((N_DEV, tr, c), lambda i: (0, i, 0))] + [spec] * 3
    args = [parts, w, m, v]
    aliases = {}
    if others is not None:
        in_specs += [pl.BlockSpec(memory_space=pl.ANY)] * 4
        args += list(others)
        aliases = {4 + j: j for j in range(4)}
    return pl.pallas_call(
        body, grid=(r // tr,), name=name, in_specs=in_specs, out_specs=[spec] * 4,
        out_shape=[jax.ShapeDtypeStruct((nl, r, c), F32)] * 4, input_output_aliases=aliases,
        compiler_params=_cp("parallel"))(*args)


def cast_layers(items, name):
    n = len(items)

    def body(*refs):
        for i in range(n):
            refs[n + i][...] = refs[i][...].astype(ACT)

    return pl.pallas_call(
        body, grid=(1,), name=name,
        in_specs=[pl.BlockSpec((None,) + w.shape[1:], lambda i, l=l: (l, 0, 0)) for w, l in items],
        out_specs=[_fullspec(w.shape[1:]) for w, _ in items],
        out_shape=[jax.ShapeDtypeStruct(w.shape[1:], ACT) for w, _ in items],
        compiler_params=_cp("arbitrary"))(*[w for w, _ in items])


def adamw_small(ws, gs, ms, vs, name):
    k = len(ws)

    def body(*refs):
        for i in range(k):
            w_ref, g_ref, m_ref, v_ref = (refs[j * k + i] for j in range(4))
            d, mm, vv = _adamw_math(w_ref[...], g_ref[...], m_ref[...], v_ref[...])
            refs[4 * k + i][...] = d
            refs[5 * k + i][...] = mm
            refs[6 * k + i][...] = vv

    shapes = [jax.ShapeDtypeStruct(w.shape, F32) for w in ws]
    specs = [_fullspec(w.shape) for w in ws]
    out = pl.pallas_call(
        body, grid=(1,), name=name, in_specs=specs * 4, out_specs=specs * 3, out_shape=shapes * 3,
        compiler_params=_cp("arbitrary"))(*ws, *gs, *ms, *vs)
    return out[:k], out[k:2 * k], out[2 * k:]


def sum_slots(x, out_dtype, name):
    g, r, c = x.shape
    tr = r if x.size * x.dtype.itemsize <= SMALL_BLOCK_BYTES else _tile(r, (256, 128, 64, 32, 16, 8))

    def body(x_ref, o_ref):
        acc = x_ref[0].astype(F32)
        for i in range(1, g):
            acc = acc + x_ref[i].astype(F32)
        o_ref[...] = acc.astype(o_ref.dtype)

    return pl.pallas_call(
        body, grid=(r // tr,), name=name,
        in_specs=[pl.BlockSpec((g, tr, c), lambda i: (0, i, 0))], out_specs=_rowspec(tr, c),
        out_shape=jax.ShapeDtypeStruct((r, c), out_dtype),
        compiler_params=_cp("parallel"))(x)


HBM_SPEC = pl.BlockSpec(memory_space=pltpu.HBM)
SEM_SPEC = pl.BlockSpec(memory_space=pltpu.SEMAPHORE)
DATAFLOW = pltpu.SideEffectType.DATAFLOW_SIDE_EFFECTING


def _my_rank():
    return 4 * lax.axis_index("x") + 2 * lax.axis_index("y") + lax.axis_index("c")


def _exchange_copies(x_refs, land_refs, send, recv, a2a):
    pos = [lax.axis_index(a) for a in AXES]
    me = _my_rank()
    copies = []
    for x_ref, land_ref, s_ref, r_ref in zip(x_refs, land_refs, send, recv):
        for k in range(N_DEV - 1):
            bits = ((k + 1) >> 2 & 1, (k + 1) >> 1 & 1, (k + 1) & 1)
            peer = tuple(1 - p if b else p for p, b in zip(pos, bits))
            prank = 4 * peer[0] + 2 * peer[1] + peer[2]
            copies.append(pltpu.make_async_remote_copy(
                src_ref=x_ref.at[prank] if a2a else x_ref, dst_ref=land_ref.at[me],
                send_sem=s_ref.at[k], recv_sem=r_ref.at[k], device_id=peer, device_id_type=MESH))
    return copies


def exchange_start(xs, a2a, name, after=None):
    n = len(xs)
    me = _my_rank()
    lands = []
    for x in xs:
        own = lax.dynamic_index_in_dim(x, me, 0, keepdims=True) if a2a else x[None]
        shape = x.shape if a2a else (N_DEV,) + x.shape
        lands.append(lax.dynamic_update_slice(lax.empty(shape, x.dtype), own, (me,) + (0,) * (len(shape) - 1)))

    def body(*refs):
        x_refs, land_refs = refs[:n], refs[n:2 * n]
        outs = refs[len(refs) - 4 * n - 1:]
        for cp in _exchange_copies(x_refs, land_refs, outs[:n], outs[n:2 * n], a2a):
            cp.start()
        token = outs[4 * n]
        token[...] = jnp.zeros_like(token)

    sems = [pltpu.SemaphoreType.DMA((N_DEV - 1,))] * n
    out = pl.pallas_call(
        body, name=name,
        out_shape=tuple(sems + sems + [pltpu.HBM(x.shape, x.dtype) for x in xs]
                        + [pltpu.HBM(l.shape, l.dtype) for l in lands] + [jax.ShapeDtypeStruct((8, LANES), F32)]),
        in_specs=[HBM_SPEC] * (2 * n) + ([] if after is None else [pl.BlockSpec(memory_space=pl.ANY)]),
        out_specs=tuple([SEM_SPEC] * (2 * n) + [HBM_SPEC] * (2 * n) + [pl.BlockSpec(memory_space=pltpu.VMEM)]),
        input_output_aliases={i: 2 * n + i for i in range(2 * n)},
        compiler_params=pltpu.CompilerParams(has_side_effects=DATAFLOW),
    )(*[pltpu.with_memory_space_constraint(a, pltpu.HBM) for a in list(xs) + lands], *([] if after is None else [after]))
    return (out[:n], out[n:2 * n], out[2 * n:3 * n], out[3 * n:4 * n]), out[4 * n]


def exchange_wait(handles, after, a2a, name):
    send, recv, x_thru, land_thru = handles
    n = len(x_thru)

    def body(*refs):
        x_refs, land_refs = refs[:n], refs[n:2 * n]
        s_refs, r_refs = refs[2 * n:3 * n], refs[3 * n:4 * n]
        for cp in _exchange_copies(x_refs, land_refs, s_refs, r_refs, a2a):
            cp.wait_send()
            cp.wait_recv()

    out = pl.pallas_call(
        body, name=name,
        out_shape=tuple([pltpu.HBM(a.shape, a.dtype) for a in list(x_thru) + list(land_thru)]),
        in_specs=[HBM_SPEC] * (2 * n) + [SEM_SPEC] * (2 * n) + [pl.BlockSpec(memory_space=pl.ANY)],
        out_specs=tuple([HBM_SPEC] * (2 * n)),
        input_output_aliases={i: i for i in range(2 * n)},
        compiler_params=pltpu.CompilerParams(has_side_effects=DATAFLOW),
    )(*x_thru, *land_thru, *send, *recv, after)
    return out[n:2 * n]


def _local_step(x, target, weight, emit, P):
    s = x.shape[0]
    tabs = _rope_tables(s)
    sinkb = jnp.broadcast_to(P["ev_sinks"].reshape(N_HEADS, 1), (N_HEADS, LANES))
    sbx = jnp.repeat(P["od_spatial_b"].reshape(D_GROUPS, BLOCK).T, HEAD_DIM, axis=1)
    sw = P["od_spatial_w"].reshape(D_GROUPS, BLOCK, BLOCK)
    fg = P["ffn_norm_g"]
    latest = [None]

    def out(name, layer, grad):
        tok = emit(name, layer, grad)
        if tok is not None:
            latest[0] = tok

    def dep(a):
        return a if latest[0] is None else a + latest[0][0:1, 0:1]

    n0 = rmsnorm(x, P["ev_norm_g"], "rms_in")
    qk_e, v_e, ga, gb = inproj(n0, weight("ev_w_in", 0, n0), tabs, ATT_W + 128,
                               (ATT_W + 128, 128, CONV_CH, CONV_CH), (), "ev_inproj")
    a_e, lse_e = attn_fwd(qk_e, v_e, d=1, hkv=A_KV_HEADS, max_dist=BLOCK - 1, sink=sinkb, out_dtype=ACT,
                          name="ev_attn")
    c_act, c1 = conv_fwd(ga, gb, P["ev_conv_w"], P["ev_conv_b"], P["ev_conv_ln_g"], P["ev_conv_ln_b"], "ev_conv")
    h1, n1 = mm_res([a_e, c_act], weight("ev_w_out", 0, c_act), x, fg[0:1], "ev_outproj")
    gate0, up0, act0 = ffn_up(n1, weight("ffn_w_gate", 0, n1), weight("ffn_w_up", 0, n1), "ffn0_up")
    h2, n2 = mm_res([act0], weight("ffn_w_down", 0, act0), h1, P["od_norm_g"], "ffn0_down")
    fold_dils = tuple(dil for _, dil in DILATED if dil > 1)
    qk_o, v_o, z, *pre = inproj(n2, weight("od_w_in", 0, n2), tabs, 2 * ATT_W, (2 * ATT_W, ATT_W, 2 * D_CH),
                                fold_dils, "od_inproj")
    folded = {dil: (pre[2 * i], pre[2 * i + 1]) for i, dil in enumerate(fold_dils)}
    outs, lses = [], []
    for window, dil in DILATED:
        assert window // dil == BLOCK
        o_r, l_r = attn_fwd(qk_o, v_o, d=dil, hkv=N_HEADS, max_dist=BLOCK, sink=None, out_dtype=ACT,
                            name="od_attn_d%d" % dil, folded=folded.get(dil), keep_folded=True)
        outs.append(o_r)
        lses.append(l_r)
    dils = [dil for _, dil in DILATED]
    c_out, lse_o, *c_folded = combine_fwd(outs, lses, dils, "od_combine")
    d_out = gmlp_fwd(z, P["od_sgu_ln_g"], P["od_sgu_ln_b"], sw, sbx, "od_gmlp")
    h3, n3 = mm_res([c_out, d_out], weight("od_w_out", 0, d_out), h2, fg[1:2], "od_outproj")
    gate1, up1, act1 = ffn_up(n3, weight("ffn_w_gate", 1, n3), weight("ffn_w_up", 1, n3), "ffn1_up")
    h4, _ = mm_res([act1], weight("ffn_w_down", 1, act1), h3, None, "ffn1_down")
    loss_part, dh4, dg_final = final_loss(h4, P["final_norm_g"], target, "loss_head")

    def ffn_bwd(layer, dh_out, h_in, n_in, gate, up, act):
        wg, wu, wd = (weight(n, layer, dh_out) for n in ("ffn_w_gate", "ffn_w_up", "ffn_w_down"))
        tag = "ffn%d" % layer
        dgate, dup = ffn_down_bwd(dh_out, wd, gate, up, tag + "_down_bwd")
        g_wd = mm_tn(act, dh_out, tag + "_dwd")
        dh_in, dgn = mm_nt_rms([(dgate, wg), (dup, wu)], h_in, dep(fg[layer:layer + 1]), dh_out, tag + "_up_bwd")
        out("ffn_w_down", layer, g_wd)
        out("ffn_w_gate", layer, mm_tn(n_in, dgate, tag + "_dwg"))
        out("ffn_w_up", layer, mm_tn(n_in, dup, tag + "_dwu"))
        return dh_in, dgn[0:1]

    dh3, dgn_f1 = ffn_bwd(1, dh4, h3, n3, gate1, up1, act1)

    dcd, *dc_folded = mm_nt(dh3, weight("od_w_out", 0, dh3), "od_outproj_bwd", fold_dils, ATT_W)
    do_o = {dil: (dc_folded[i], c_folded[i]) for i, dil in enumerate(fold_dils)}
    dz, g_sw, g_sbx, g_slg, g_slb = gmlp_bwd(z, dcd, 1, dep(P["od_sgu_ln_g"]), P["od_sgu_ln_b"], sw, sbx,
                                             "od_gmlp_bwd")
    g_sb = jnp.sum(g_sbx.reshape(BLOCK, D_GROUPS, HEAD_DIM), axis=-1).T
    out("od_spatial", 0, jnp.concatenate([g_sw.reshape(D_GROUPS * BLOCK, BLOCK), g_sb], axis=0))
    out("od_w_out", 0, jnp.concatenate([mm_tn(c_out, dh3, "od_dwo_c"), mm_tn(d_out, dh3, "od_dwo_d")], axis=0))
    dqkv = [attn_bwd(qk_o, v_o, dcd if dil == 1 else dcd[:, :ATT_W], 0, c_out, lse_o, d=dil, hkv=N_HEADS,
                     max_dist=BLOCK, sink=None, out_dtype=ACT, name="od_attn_bwd_d%d" % dil,
                     folded=folded.get(dil), folded_do_o=do_o.get(dil), keep_folded=True)
            for window, dil in DILATED]
    dproj_o = assemble([([(b[j], dil) for b, dil in zip(dqkv, dils)], j < 2) for j in range(3)] + [([dz], False)],
                       tabs, "od_dproj")
    dh2, dgn_od = mm_nt_rms([(dproj_o, weight("od_w_in", 0, dproj_o))], h2, dep(P["od_norm_g"]), dh3,
                            "od_inproj_bwd")
    out("od_w_in", 0, mm_tn(n2, dproj_o, "od_dwi"))

    dh1, dgn_f0 = ffn_bwd(0, dh2, h1, n1, gate0, up0, act0)

    dac = mm_nt(dh1, weight("ev_w_out", 0, dh1), "ev_outproj_bwd")
    dc1, g_clg, g_clb = conv_bwd_ln(c1, dac, 1, dep(P["ev_conv_ln_g"]), P["ev_conv_ln_b"], "ev_conv_bwd_ln")
    out("ev_w_out", 0, jnp.concatenate([mm_tn(a_e, dh1, "ev_dwo_a"), mm_tn(c_act, dh1, "ev_dwo_c")], axis=0))
    dga, dgb, g_cw, g_cb = conv_bwd_conv(dc1, ga, gb, P["ev_conv_w"], "ev_conv_bwd")
    dq, dk, dv, dsink = attn_bwd(qk_e, v_e, dac, 0, a_e, lse_e, d=1, hkv=A_KV_HEADS, max_dist=BLOCK - 1,
                                 sink=sinkb, out_dtype=F32, name="ev_attn_bwd")
    dproj_e = assemble([([dq], True), ([dk], True), ([dv], False), ([dga], False), ([dgb], False)], tabs,
                       "ev_dproj")
    out("ev_w_in", 0, mm_tn(n0, dproj_e, "ev_dwi"))
    dx, dgn_ev = mm_nt_rms([(dproj_e, weight("ev_w_in", 0, dproj_e))], x, dep(P["ev_norm_g"]), dh1,
                           "ev_inproj_bwd")

    small = {
        "ev_norm_g": dgn_ev[0:1],
        "ev_sinks": dsink[:, 0:1].reshape(1, N_HEADS),
        "ev_conv_w": jnp.sum(g_cw.reshape(CONV_HALO, SUBLANES, CONV_CH), axis=1)[:CONV_WIDTH],
        "ev_conv_b": jnp.sum(g_cb, axis=0, keepdims=True),
        "ev_conv_ln_g": g_clg[0:1],
        "ev_conv_ln_b": g_clb[0:1],
        "od_norm_g": dgn_od[0:1],
        "od_sgu_ln_g": g_slg[0:1],
        "od_sgu_ln_b": g_slb[0:1],
        "od_spatial_w": g_sw.reshape(D_GROUPS * BLOCK, BLOCK),
        "od_spatial_b": g_sb,
        "ffn_norm_g": jnp.concatenate([dgn_f0, dgn_f1], axis=0),
        "final_norm_g": dg_final[0:1],
    }
    return loss_part, dx, small


BIG = ("ev_w_in", "ev_w_out", "od_w_in", "od_w_out", "ffn_w_gate", "ffn_w_up", "ffn_w_down")
COL_SHARDED = ("ev_w_in", "od_w_in", "ffn_w_gate", "ffn_w_up")
GATHER_GROUPS = (
    (("ev_w_in", 0),),
    (("ev_w_out", 0),),
    (("ffn_w_gate", 0), ("ffn_w_up", 0)),
    (("ffn_w_down", 0),),
    (("od_w_in", 0),),
    (("od_w_out", 0),),
    (("ffn_w_gate", 1), ("ffn_w_up", 1)),
    (("ffn_w_down", 1),),
)
GATHER_EARLY = 4
GATHER_LATE_AT = 2
REDUCE_GROUPS = (
    (("ffn_w_down", 1), ("ffn_w_gate", 1), ("ffn_w_up", 1)),
    (("od_w_out", 0),),
    (("od_w_in", 0),),
    (("ffn_w_down", 0), ("ffn_w_gate", 0), ("ffn_w_up", 0)),
    (("ev_w_out", 0),),
    (("ev_w_in", 0),),
)


def _unshard(name, g):
    if name in COL_SHARDED:
        return jnp.moveaxis(g, 0, 1).reshape(g.shape[1], N_DEV * g.shape[2])
    return g.reshape(N_DEV * g.shape[1], g.shape[2])


def _shard_slots(name, full):
    r, c = full.shape
    if name in COL_SHARDED:
        return jnp.moveaxis(full.reshape(r, N_DEV, c // N_DEV), 1, 0)
    return full.reshape(N_DEV, r // N_DEV, c)


def kernel(x, ev_norm_g, ev_w_in, ev_sinks, ev_conv_w, ev_conv_b, ev_conv_ln_g, ev_conv_ln_b, ev_w_out, od_norm_g, od_w_in, od_sgu_ln_g, od_sgu_ln_b, od_spatial_w, od_spatial_b, od_w_out, ffn_norm_g, ffn_w_gate, ffn_w_up, ffn_w_down, final_norm_g, loss_target, m_ev_norm_g, m_ev_w_in, m_ev_sinks, m_ev_conv_w, m_ev_conv_b, m_ev_conv_ln_g, m_ev_conv_ln_b, m_ev_w_out, m_od_norm_g, m_od_w_in, m_od_sgu_ln_g, m_od_sgu_ln_b, m_od_spatial_w, m_od_spatial_b, m_od_w_out, m_ffn_norm_g, m_ffn_w_gate, m_ffn_w_up, m_ffn_w_down, m_final_norm_g, v_ev_norm_g, v_ev_w_in, v_ev_sinks, v_ev_conv_w, v_ev_conv_b, v_ev_conv_ln_g, v_ev_conv_ln_b, v_ev_w_out, v_od_norm_g, v_od_w_in, v_od_sgu_ln_g, v_od_sgu_ln_b, v_od_spatial_w, v_od_spatial_b, v_od_w_out, v_ffn_norm_g, v_ffn_w_gate, v_ffn_w_up, v_ffn_w_down, v_final_norm_g):
    names = ["ev_norm_g", "ev_w_in", "ev_sinks", "ev_conv_w", "ev_conv_b", "ev_conv_ln_g", "ev_conv_ln_b", "ev_w_out",
             "od_norm_g", "od_w_in", "od_sgu_ln_g", "od_sgu_ln_b", "od_spatial_w", "od_spatial_b", "od_w_out",
             "ffn_norm_g", "ffn_w_gate", "ffn_w_up", "ffn_w_down", "final_norm_g"]
    wts = dict(zip(names, [ev_norm_g, ev_w_in, ev_sinks, ev_conv_w, ev_conv_b, ev_conv_ln_g, ev_conv_ln_b, ev_w_out,
                           od_norm_g, od_w_in, od_sgu_ln_g, od_sgu_ln_b, od_spatial_w, od_spatial_b, od_w_out,
                           ffn_norm_g, ffn_w_gate, ffn_w_up, ffn_w_down, final_norm_g]))
    mom = dict(zip(names, [m_ev_norm_g, m_ev_w_in, m_ev_sinks, m_ev_conv_w, m_ev_conv_b, m_ev_conv_ln_g, m_ev_conv_ln_b,
                           m_ev_w_out, m_od_norm_g, m_od_w_in, m_od_sgu_ln_g, m_od_sgu_ln_b, m_od_spatial_w,
                           m_od_spatial_b, m_od_w_out, m_ffn_norm_g, m_ffn_w_gate, m_ffn_w_up, m_ffn_w_down,
                           m_final_norm_g]))
    vel = dict(zip(names, [v_ev_norm_g, v_ev_w_in, v_ev_sinks, v_ev_conv_w, v_ev_conv_b, v_ev_conv_ln_g, v_ev_conv_ln_b,
                           v_ev_w_out, v_od_norm_g, v_od_w_in, v_od_sgu_ln_g, v_od_sgu_ln_b, v_od_spatial_w,
                           v_od_spatial_b, v_od_w_out, v_ffn_norm_g, v_ffn_w_gate, v_ffn_w_up, v_ffn_w_down,
                           v_final_norm_g]))
    me = _my_rank()

    sp = jnp.zeros((40, LANES), F32)
    sp = sp.at[0:CONV_WIDTH, 0:64].set(ev_conv_w[0])
    sp = sp.at[32, :].set(od_norm_g[0])
    sp = sp.at[33, 0:64].set(od_sgu_ln_g[0])
    sp = sp.at[34, 0:64].set(od_sgu_ln_b[0])

    early = [k for grp in GATHER_GROUPS[:GATHER_EARLY] for k in grp]
    late = [k for grp in GATHER_GROUPS[GATHER_EARLY:] for k in grp]
    early_act = dict(zip(early, cast_layers([(wts[n], l) for n, l in early], "cast_early")))
    late_act = dict(zip(late, cast_layers([(wts[n], l) for n, l in late], "cast_late")))
    ag_early, ag_token = exchange_start([sp] + [early_act[k] for k in early], False, "ag_start")
    ag_late = []
    full_w = {}
    P = {
        "ev_norm_g": ev_norm_g + ag_token[0:1, 0:1], "ev_sinks": ev_sinks, "ev_conv_b": ev_conv_b,
        "ev_conv_ln_g": ev_conv_ln_g, "ev_conv_ln_b": ev_conv_ln_b,
        "od_spatial_w": od_spatial_w, "od_spatial_b": od_spatial_b, "ffn_norm_g": ffn_norm_g,
        "final_norm_g": final_norm_g.reshape(1, D_MODEL),
    }

    def weight(name, layer, after):
        if (name, layer) not in full_w:
            gi = [i for i, grp in enumerate(GATHER_GROUPS) if (name, layer) in grp][0]
            if gi < GATHER_EARLY:
                idx = [1 + early.index(k) for k in GATHER_GROUPS[gi]]
                handles = ag_early
            else:
                idx = [late.index(k) for k in GATHER_GROUPS[gi]]
                handles = ag_late[0]
            if gi == 0:
                idx = [0] + idx
            lands = exchange_wait(tuple([h[i] for i in idx] for h in handles), after, False, "ag_wait%d" % gi)
            if gi == 0:
                spg, lands = lands[0], lands[1:]
                P["ev_conv_w"] = jnp.moveaxis(spg[:, 0:CONV_WIDTH, 0:64], 0, 1).reshape(CONV_WIDTH, CONV_CH)
                P["od_norm_g"] = spg[:, 32, :].reshape(1, D_MODEL)
                P["od_sgu_ln_g"] = spg[:, 33, 0:64].reshape(1, D_CH)
                P["od_sgu_ln_b"] = spg[:, 34, 0:64].reshape(1, D_CH)
            if gi == GATHER_LATE_AT:
                ag_late.append(exchange_start([late_act[k] for k in late], False, "ag_start_late", after=lands[0])[0])
            for k, land in zip(GATHER_GROUPS[gi], lands):
                full_w[k] = _unshard(k[0], land)
        return full_w[(name, layer)]

    pending, rs_started, spatial = {}, [], []

    def emit(name, layer, grad):
        if name == "od_spatial":
            handles, token = exchange_start([grad], False, "ar_start_b")
            spatial.append(handles)
            return token
        pending[(name, layer)] = grad
        for gi, grp in enumerate(REDUCE_GROUPS):
            if (name, layer) in grp and all(k in pending for k in grp):
                handles, token = exchange_start([_shard_slots(k[0], pending[k]).astype(ACT) for k in grp], True,
                                                "rs_start%d" % gi)
                rs_started.append((gi, handles))
                return token
        return None

    loss_part, dx, small = _local_step(x[0], loss_target[0], weight, emit, P)
    loss = lax.psum(loss_part[0, 0], AXES)

    wide = ["ev_norm_g", "ev_sinks", "ev_conv_w", "ev_conv_b", "ev_conv_ln_g", "ev_conv_ln_b", "od_norm_g",
            "od_sgu_ln_g", "od_sgu_ln_b", "ffn_norm_g", "final_norm_g"]
    blk_a = jnp.concatenate(
        [jnp.pad(small[n], ((0, 0), (0, D_MODEL - small[n].shape[1]))) for n in wide], axis=0)
    blk_a = jnp.pad(blk_a, ((0, 48 - blk_a.shape[0]), (0, 0)))
    ar_a, ar_token = exchange_start([blk_a], False, "ar_start_a")

    results = {}
    for gi, handles in rs_started:
        lands = exchange_wait(handles, ar_token, True, "rs_wait%d" % gi)
        for (n, l), land in zip(REDUCE_GROUPS[gi], lands):
            results[n] = sum_adamw(land, wts[n], mom[n], vel[n], l, results.get(n), "adamw_%s%d" % (n, l))
    out_g, out_d, out_m, out_v = {}, {}, {}, {}
    for n in BIG:
        out_g[n], out_d[n], out_m[n], out_v[n] = results[n]

    last = results[REDUCE_GROUPS[-1][-1][0]][0]
    sum_b = sum_slots(exchange_wait(spatial[0], last, False, "ar_wait_b")[0], F32, "ar_sum_b")
    sum_a = sum_slots(exchange_wait(ar_a, last, False, "ar_wait_a")[0], F32, "ar_sum_a")
    full = {}
    off = 0
    for n in wide:
        r_, c_ = small[n].shape
        full[n] = sum_a[off:off + r_, 0:c_]
        off += r_
    full["od_spatial_w"] = sum_b[0:D_GROUPS * BLOCK]
    full["od_spatial_b"] = sum_b[D_GROUPS * BLOCK:D_GROUPS * BLOCK + D_GROUPS]
    full["ev_conv_w"] = lax.dynamic_slice_in_dim(full["ev_conv_w"], me * 64, 64, axis=1)
    full["od_norm_g"] = lax.dynamic_slice_in_dim(full["od_norm_g"], me * 128, 128, axis=1)
    full["od_sgu_ln_g"] = lax.dynamic_slice_in_dim(full["od_sgu_ln_g"], me * 64, 64, axis=1)
    full["od_sgu_ln_b"] = lax.dynamic_slice_in_dim(full["od_sgu_ln_b"], me * 64, 64, axis=1)

    small_names = [n for n in names if n not in BIG]
    view = {n: ((-1, wts[n].shape[-1]) if wts[n].ndim > 1 else (1, -1)) for n in small_names}
    ds, ms, vs = adamw_small([wts[n].reshape(view[n]) for n in small_names],
                             [full[n].reshape(view[n]) for n in small_names],
                             [mom[n].reshape(view[n]) for n in small_names],
                             [vel[n].reshape(view[n]) for n in small_names], "adamw_small")
    for i, n in enumerate(small_names):
        shp = wts[n].shape
        out_g[n], out_d[n], out_m[n], out_v[n] = (full[n].reshape(shp), ds[i].reshape(shp), ms[i].reshape(shp),
                                                  vs[i].reshape(shp))

    return (loss, dx[None], *[out_g[n] for n in names], *[out_d[n] for n in names],
            *[out_m[n] for n in names], *[out_v[n] for n in names])
```

```python
import functools
import math

import jax
import jax.numpy as jnp
from jax import lax
from jax.experimental import pallas as pl
from jax.experimental.pallas import tpu as pltpu

F32 = jnp.float32
ACT = jnp.bfloat16

D_MODEL = 1024
HEAD_DIM = 64
N_HEADS = 8
ATT_W = N_HEADS * HEAD_DIM
A_KV_HEADS = 2
CONV_CH = 512
CONV_WIDTH = 31
CONV_HALO = 32
D_CH = 512
D_GROUPS = 8
BLOCK = 128
D_FF = 2816
ROT_DIM = 16
ROPE_THETA = 500000.0
RMS_EPS = 1e-6
LN_EPS = 1e-5
DILATED = ((128, 1), (512, 4), (2048, 16))
NEG = -1e30
LANES = 128

ADAM_LR = 0.001
ADAM_B1 = 0.9
ADAM_B2 = 0.999
ADAM_EPS = 1e-08
ADAM_WD = 0.01
ADAM_STEP = 10

V7X_VMEM_LIMIT = 56 * 1024 * 1024
SMALL_BLOCK_BYTES = 6 * 1024 * 1024
N_DEV = 8

NN = (((1,), (0,)), ((), ()))
NT = (((1,), (1,)), ((), ()))
TN = (((0,), (0,)), ((), ()))
MESH = pl.DeviceIdType.MESH
AXES = ("x", "y", "c")


def _dot(a, b, dims):
    return lax.dot_general(a, b, dims, preferred_element_type=F32)


def _cp(*sem):
    return pltpu.CompilerParams(dimension_semantics=sem if sem else None,
                                vmem_limit_bytes=V7X_VMEM_LIMIT)


def _tile(n, prefs):
    for p in prefs:
        if n % p == 0:
            return p
    return n


def _sigmoid(x):
    return 1.0 / (1.0 + jnp.exp(-x))


def _rowspec(tm, w, col=0):
    return pl.BlockSpec((tm, w), lambda i, col=col: (i, col))


def _fullspec(shape):
    nd = len(shape)
    return pl.BlockSpec(shape, lambda *a, nd=nd: (0,) * nd)


def _rope_tables(seq):
    half = ROT_DIM // 2
    inv_freq = ROPE_THETA ** (-jnp.arange(half, dtype=F32) * (2.0 / ROT_DIM))
    ang = jnp.arange(seq, dtype=jnp.int32).astype(F32)[:, None] * inv_freq[None, :]
    cos, sin = jnp.cos(ang), jnp.sin(ang)
    lane = jnp.arange(LANES)
    jm = lane % HEAD_DIM
    idx = jm % half
    c = jnp.where(jm[None, :] < ROT_DIM, cos[:, idx], 1.0)
    sa = jnp.where(jm[None, :] < half, -sin[:, idx], 0.0)
    sb = jnp.where((jm[None, :] >= half) & (jm[None, :] < ROT_DIM), sin[:, idx], 0.0)
    return c.astype(F32), sa.astype(F32), sb.astype(F32)


def _rope(x, c, sa, sb):
    return x * c + pltpu.roll(x, LANES - 8, 1) * sa + pltpu.roll(x, 8, 1) * sb


def _rope_t(d, c, sa, sb):
    return d * c + pltpu.roll(d * sa, 8, 1) + pltpu.roll(d * sb, LANES - 8, 1)


def rmsnorm(h, g, name):
    s = h.shape[0]
    tm = _tile(s, (512,))

    def body(h_ref, g_ref, o_ref):
        x = h_ref[...]
        r = lax.rsqrt(jnp.mean(x * x, axis=-1, keepdims=True) + RMS_EPS)
        o_ref[...] = (x * r * g_ref[...]).astype(o_ref.dtype)

    return pl.pallas_call(
        body, grid=(s // tm,), name=name,
        in_specs=[_rowspec(tm, D_MODEL), _fullspec((1, D_MODEL))],
        out_specs=_rowspec(tm, D_MODEL),
        out_shape=jax.ShapeDtypeStruct((s, D_MODEL), ACT),
        compiler_params=_cp("parallel"))(h, g)


def _fold_perm(tm, d, inverse, cache):
    key = (tm, d, inverse)
    if key not in cache:
        m = tm // d
        a = lax.broadcasted_iota(jnp.int32, (tm, tm), 1 if inverse else 0)
        b = lax.broadcasted_iota(jnp.int32, (tm, tm), 0 if inverse else 1)
        src = (a & (m - 1)) * d + (a >> (m.bit_length() - 1))
        cache[key] = (b == src).astype(ACT)
    return cache[key]


def _fold_rows(x, d, cache):
    tm = x.shape[0]
    m = tm // d
    p = _dot(_fold_perm(tm, d, False, cache), x, NN).astype(ACT)
    return jnp.concatenate([p[r * m:(r + 1) * m] for r in range(d)], axis=1)


def _unfold_rows(blk, d, cache):
    w = blk.shape[1] // d
    stacked = jnp.concatenate([blk[:, r * w:(r + 1) * w] for r in range(d)], axis=0)
    return _dot(_fold_perm(stacked.shape[0], d, True, cache), stacked, NN)


def _unfold_rows_f32(blk, d, cache):
    hi = blk.astype(ACT)
    r1 = blk - hi.astype(F32)
    mid = r1.astype(ACT)
    lo = (r1 - mid.astype(F32)).astype(ACT)
    return _unfold_rows(hi, d, cache) + _unfold_rows(mid, d, cache) + _unfold_rows(lo, d, cache)


def inproj(n, w, tabs, nqk, splits, fold_dils, name):
    s = n.shape[0]
    ntot = w.shape[1]
    assert sum(splits) == ntot and splits[0] == nqk
    tm = _tile(s, (256,) if fold_dils else (512,))
    ns = len(splits)

    def body(n_ref, w_ref, c_ref, sa_ref, sb_ref, *outs):
        res = _dot(n_ref[...], w_ref[...], NN)
        c, sa, sb = c_ref[...], sa_ref[...], sb_ref[...]
        for g in range(nqk // LANES):
            x = res[:, g * LANES:(g + 1) * LANES]
            outs[0][:, g * LANES:(g + 1) * LANES] = _rope(x, c, sa, sb).astype(ACT)
        off = nqk
        for o_ref, wd in zip(outs[1:ns], splits[1:]):
            o_ref[...] = res[:, off:off + wd].astype(ACT)
            off += wd
        cache = {}
        for i, d in enumerate(fold_dils):
            outs[ns + 2 * i][...] = _fold_rows(outs[0][...], d, cache)
            outs[ns + 2 * i + 1][...] = _fold_rows(outs[1][...], d, cache)

    out_specs = [_rowspec(tm, wd) for wd in splits]
    out_shape = [jax.ShapeDtypeStruct((s, wd), ACT) for wd in splits]
    for d in fold_dils:
        for wd in splits[:2]:
            out_specs.append(_rowspec(tm // d, d * wd))
            out_shape.append(jax.ShapeDtypeStruct((s // d, d * wd), ACT))
    return pl.pallas_call(
        body, grid=(s // tm,), name=name,
        in_specs=[_rowspec(tm, D_MODEL), _fullspec((D_MODEL, ntot))] + [_rowspec(tm, LANES)] * 3,
        out_specs=out_specs, out_shape=out_shape,
        compiler_params=_cp("parallel"))(n, w, *tabs)


def ffn_up(n, wg, wu, name):
    s = n.shape[0]
    f = wg.shape[1]
    tm = _tile(s, (1024, 512))
    tf = _tile(f, (1408, 512, 256, 128))

    def body(n_ref, wg_ref, wu_ref, g_ref, u_ref, a_ref):
        a = n_ref[...]
        g = _dot(a, wg_ref[...], NN)
        u = _dot(a, wu_ref[...], NN)
        sg = _sigmoid(g)
        silu = g * sg
        g_ref[...] = (u * (sg * (1.0 + g * (1.0 - sg)))).astype(ACT)
        u_ref[...] = silu.astype(ACT)
        a_ref[...] = (silu * u).astype(ACT)

    wspec = pl.BlockSpec((D_MODEL, tf), lambda j, i: (0, j))
    ospec = pl.BlockSpec((tm, tf), lambda j, i: (i, j))
    return pl.pallas_call(
        body, grid=(f // tf, s // tm), name=name,
        in_specs=[pl.BlockSpec((tm, D_MODEL), lambda j, i: (i, 0)), wspec, wspec],
        out_specs=[ospec] * 3,
        out_shape=[jax.ShapeDtypeStruct((s, f), ACT)] * 3,
        compiler_params=_cp("parallel", "parallel"))(n, wg, wu)


def mm_res(parts, w, h, gnext, name):
    s = h.shape[0]
    tm = _tile(s, (512,))
    widths = [p.shape[1] for p in parts]
    assert sum(widths) == w.shape[0]
    np_ = len(parts)

    def body(*refs):
        p_refs = refs[:np_]
        w_ref, h_ref = refs[np_], refs[np_ + 1]
        rest = refs[np_ + 2:]
        acc = h_ref[...]
        off = 0
        for p_ref, wd in zip(p_refs, widths):
            acc = acc + _dot(p_ref[...], w_ref[off:off + wd, :], NN)
            off += wd
        if gnext is None:
            rest[0][...] = acc
        else:
            g_ref, ho_ref, no_ref = rest
            ho_ref[...] = acc
            r = lax.rsqrt(jnp.mean(acc * acc, axis=-1, keepdims=True) + RMS_EPS)
            no_ref[...] = (acc * r * g_ref[...]).astype(ACT)

    in_specs = [_rowspec(tm, wd) for wd in widths] + [_fullspec(w.shape), _rowspec(tm, D_MODEL)]
    args = list(parts) + [w, h]
    out_specs = [_rowspec(tm, D_MODEL)]
    out_shape = [jax.ShapeDtypeStruct((s, D_MODEL), F32)]
    if gnext is not None:
        in_specs.append(_fullspec((1, D_MODEL)))
        args.append(gnext)
        out_specs.append(_rowspec(tm, D_MODEL))
        out_shape.append(jax.ShapeDtypeStruct((s, D_MODEL), ACT))
    out = pl.pallas_call(
        body, grid=(s // tm,), name=name, in_specs=in_specs, out_specs=out_specs,
        out_shape=out_shape, compiler_params=_cp("parallel"))(*args)
    return (out[0], None) if gnext is None else (out[0], out[1])


def _band_mask(n, max_dist):
    qi = lax.broadcasted_iota(jnp.int32, (BLOCK, 2 * BLOCK), 0)
    kj = lax.broadcasted_iota(jnp.int32, (BLOCK, 2 * BLOCK), 1)
    dist = qi + BLOCK - kj
    valid = jnp.logical_and(dist >= 0, dist <= max_dist)
    return jnp.logical_and(valid, jnp.logical_or(kj >= BLOCK, n > 0))


def _band_mask_t(n, max_dist):
    kj = lax.broadcasted_iota(jnp.int32, (2 * BLOCK, BLOCK), 0)
    qi = lax.broadcasted_iota(jnp.int32, (2 * BLOCK, BLOCK), 1)
    dist = qi + BLOCK - kj
    valid = jnp.logical_and(dist >= 0, dist <= max_dist)
    return jnp.logical_and(valid, jnp.logical_or(kj >= BLOCK, n > 0))


ATT_FWD_BLOCKS = 4


def _fold(a, d):
    return a.reshape(a.shape[0] // d, d * a.shape[1])


def attn_fwd(qk, v, *, d, hkv, max_dist, sink, out_dtype, name, folded=None, keep_folded=False):
    s = qk.shape[0]
    kvw = hkv * HEAD_DIM
    wqk = ATT_W + kvw
    assert qk.shape[1] == wqk and (d == 1 or (wqk % ATT_W == 0 and wqk % kvw == 0))
    assert sink is None or max_dist == BLOCK - 1
    nb = s // d // BLOCK
    grp = N_HEADS // hkv
    qpb, kpb, koff = wqk // ATT_W, wqk // kvw, ATT_W // kvw

    qb = ATT_FWD_BLOCKS if nb % ATT_FWD_BLOCKS == 0 else 1

    def body(*refs):
        if sink is None:
            q_ref, kc_ref, kp_ref, vc_ref, vp_ref, o_ref, l_ref, o_buf = refs
        else:
            q_ref, kc_ref, kp_ref, vc_ref, vp_ref, s_ref, o_ref, l_ref, o_buf = refs
        n = pl.program_id(1)
        qsl = [slice(h * HEAD_DIM, (h + 1) * HEAD_DIM) for h in range(N_HEADS)]
        ksl = [slice((h // grp) * HEAD_DIM, (h // grp + 1) * HEAD_DIM) for h in range(N_HEADS)]
        head_row = lax.broadcasted_iota(jnp.int32, (SUBLANES, BLOCK), 0)
        if sink is not None:
            sink_row = lax.broadcasted_iota(jnp.int32, (2 * BLOCK, BLOCK), 0) == 0
        for j in range(qb):
            rows = slice(j * BLOCK, (j + 1) * BLOCK)
            before = slice((j - 1) * BLOCK, j * BLOCK)
            valid = _band_mask_t(n * qb + j, max_dist)
            kk = jnp.concatenate([kp_ref[...] if j == 0 else kc_ref[before, :], kc_ref[rows, :]], axis=0)
            vv = jnp.concatenate([vp_ref[...] if j == 0 else vc_ref[before, :], vc_ref[rows, :]], axis=0)
            scores = []
            for h in range(N_HEADS):
                q = q_ref[rows, qsl[h]] * 0.125
                scores.append(_dot(kk[:, ksl[h]], q, NT))
            probs = []
            lse8 = jnp.zeros((SUBLANES, BLOCK), F32)
            for h in range(N_HEADS):
                sc = jnp.where(valid, scores[h], NEG)
                if sink is not None:
                    sc = jnp.where(sink_row, s_ref[h:h + 1, 0:1], sc)
                m = jnp.max(sc, axis=0, keepdims=True)
                p = jnp.exp(sc - m)
                l = jnp.sum(p, axis=0, keepdims=True)
                if sink is not None:
                    p = jnp.where(sink_row, 0.0, p)
                lse8 = jnp.where(head_row == h, m + jnp.log(l), lse8)
                probs.append((p * (1.0 / l)).astype(ACT))
            l_ref[rows, :] = jnp.concatenate([lse8, jnp.zeros((BLOCK - SUBLANES, BLOCK), F32)], axis=0).T
            for h in range(N_HEADS):
                o_buf[rows, qsl[h]] = _dot(probs[h], vv[:, ksl[h]], TN)
        o_ref[...] = o_buf[...].astype(o_ref.dtype)

    prev = lambda n: jnp.maximum(n * qb - 1, 0)
    in_specs = [
        pl.BlockSpec((qb * BLOCK, ATT_W), lambda r, n: (n, r * qpb)),
        pl.BlockSpec((qb * BLOCK, kvw), lambda r, n: (n, r * kpb + koff)),
        pl.BlockSpec((BLOCK, kvw), lambda r, n: (prev(n), r * kpb + koff)),
        pl.BlockSpec((qb * BLOCK, kvw), lambda r, n: (n, r)),
        pl.BlockSpec((BLOCK, kvw), lambda r, n: (prev(n), r)),
    ]
    qkf, vf = (_fold(qk, d), _fold(v, d)) if folded is None else folded
    args = [qkf, qkf, qkf, vf, vf]
    if sink is not None:
        in_specs.append(_fullspec((N_HEADS, LANES)))
        args.append(sink)
    ospec = pl.BlockSpec((qb * BLOCK, ATT_W), lambda r, n: (n, r))
    lspec = pl.BlockSpec((qb * BLOCK, LANES), lambda r, n: (n, r))
    o, lse = pl.pallas_call(
        body, grid=(d, nb // qb), name=name, in_specs=in_specs, out_specs=[ospec, lspec],
        out_shape=[jax.ShapeDtypeStruct((s // d, d * ATT_W), out_dtype),
                   jax.ShapeDtypeStruct((s // d, d * LANES), F32)],
        scratch_shapes=[pltpu.VMEM((qb * BLOCK, ATT_W), F32)],
        compiler_params=_cp("parallel", "parallel"))(*args)
    return (o, lse) if keep_folded else (o.reshape(s, ATT_W), lse.reshape(s, LANES))


def attn_bwd(qk, v, do_src, do_col, o, lse, *, d, hkv, max_dist, sink, out_dtype, name, folded=None,
             folded_do_o=None, keep_folded=False):
    s = qk.shape[0]
    kvw = hkv * HEAD_DIM
    wqk = ATT_W + kvw
    nb = s // d // BLOCK
    grp = N_HEADS // hkv
    qpb, kpb, koff = wqk // ATT_W, wqk // kvw, ATT_W // kvw
    dob = do_src.shape[1] // ATT_W
    has_sink = sink is not None

    def body(*refs):
        refs = list(refs)
        q_ref, kc_ref, kp_ref, vc_ref, vp_ref, do_ref, o_ref, l_ref = refs[:8]
        pos = 8
        if has_sink:
            s_ref = refs[pos]
            pos += 1
        dq_ref, dk_ref, dv_ref = refs[pos:pos + 3]
        pos += 3
        if has_sink:
            ds_ref = refs[pos]
            pos += 1
        ck_ref, cv_ref, dq_buf, dk_buf, dv_buf = refs[pos:pos + 5]
        r_id = pl.program_id(0)
        n = pl.program_id(1)

        @pl.when(n == 0)
        def _():
            ck_ref[...] = jnp.zeros_like(ck_ref)
            cv_ref[...] = jnp.zeros_like(cv_ref)

        if has_sink:
            @pl.when(jnp.logical_and(n == 0, r_id == 0))
            def _():
                ds_ref[...] = jnp.zeros_like(ds_ref)

        @pl.when(n < nb)
        def _():
            valid = _band_mask_t(n, max_dist)
            qsl = [slice(h * HEAD_DIM, (h + 1) * HEAD_DIM) for h in range(N_HEADS)]
            ksl = [slice((h // grp) * HEAD_DIM, (h // grp + 1) * HEAD_DIM) for h in range(N_HEADS)]
            kk = jnp.concatenate([kp_ref[...], kc_ref[...]], axis=0)
            vv = jnp.concatenate([vp_ref[...], vc_ref[...]], axis=0)
            qs, first = [], []
            for h in range(N_HEADS):
                q = q_ref[:, qsl[h]] * 0.125
                qs.append(q)
                first.append((_dot(kk[:, ksl[h]], q, NT), _dot(vv[:, ksl[h]], do_ref[:, qsl[h]], NT)))
            lse_t = l_ref[...].T
            prod = do_ref[...].astype(F32) * o_ref[...].astype(F32)
            hi = prod.astype(ACT)
            lo = (prod - hi.astype(F32)).astype(ACT)
            col = lax.broadcasted_iota(jnp.int32, (LANES, ATT_W), 1)
            row = lax.broadcasted_iota(jnp.int32, (LANES, ATT_W), 0)
            head_of = jnp.logical_and(col >= row * HEAD_DIM, col < (row + 1) * HEAD_DIM).astype(ACT)
            e_t = _dot(head_of, hi, NT) + _dot(head_of, lo, NT)
            mid = []
            for h in range(N_HEADS):
                s_t, dp_t = first[h]
                lse_h, e_h = lse_t[h:h + 1, :], e_t[h:h + 1, :]
                p_t = jnp.exp(jnp.where(valid, s_t, NEG) - lse_h)
                mid.append(((p_t * (dp_t - e_h)).astype(ACT), p_t.astype(ACT)))
                if has_sink:
                    sk = s_ref[h:h + 1, 0:1]
                    dsk = -jnp.sum(jnp.exp(sk - lse_h) * e_h, axis=1, keepdims=True)
                    ds_ref[h:h + 1, :] = ds_ref[h:h + 1, :] + dsk
            dkk = [None] * hkv
            dvv = [None] * hkv
            for h in range(N_HEADS):
                kh = h // grp
                ds_t, p_t = mid[h]
                dq_buf[:, qsl[h]] = _dot(ds_t, kk[:, ksl[h]], TN) * 0.125
                for lst, val in ((dkk, _dot(ds_t, qs[h], NN)), (dvv, _dot(p_t, do_ref[:, qsl[h]], NN))):
                    lst[kh] = val if lst[kh] is None else lst[kh] + val
            for kh in range(hkv):
                ks = slice(kh * HEAD_DIM, (kh + 1) * HEAD_DIM)
                dk_buf[:, ks] = ck_ref[:, ks] + dkk[kh][:BLOCK]
                dv_buf[:, ks] = cv_ref[:, ks] + dvv[kh][:BLOCK]
                ck_ref[:, ks] = dkk[kh][BLOCK:]
                cv_ref[:, ks] = dvv[kh][BLOCK:]
            dq_ref[...] = dq_buf[...].astype(dq_ref.dtype)
            dk_ref[...] = dk_buf[...].astype(dk_ref.dtype)
            dv_ref[...] = dv_buf[...].astype(dv_ref.dtype)

        @pl.when(n == nb)
        def _():
            dk_ref[...] = ck_ref[...].astype(dk_ref.dtype)
            dv_ref[...] = cv_ref[...].astype(dv_ref.dtype)

    qrow = lambda n: jnp.minimum(n, nb - 1)
    prow = lambda n: jnp.maximum(jnp.minimum(n, nb - 1) - 1, 0)
    krow = lambda n: jnp.maximum(n - 1, 0)
    in_specs = [
        pl.BlockSpec((BLOCK, ATT_W), lambda r, n: (qrow(n), r * qpb)),
        pl.BlockSpec((BLOCK, kvw), lambda r, n: (qrow(n), r * kpb + koff)),
        pl.BlockSpec((BLOCK, kvw), lambda r, n: (prow(n), r * kpb + koff)),
        pl.BlockSpec((BLOCK, kvw), lambda r, n: (qrow(n), r)),
        pl.BlockSpec((BLOCK, kvw), lambda r, n: (prow(n), r)),
        pl.BlockSpec((BLOCK, ATT_W), lambda r, n: (qrow(n), r * dob + do_col)),
        pl.BlockSpec((BLOCK, ATT_W), lambda r, n: (qrow(n), r)),
        pl.BlockSpec((BLOCK, LANES), lambda r, n: (qrow(n), r)),
    ]
    qkf, vf = (_fold(qk, d), _fold(v, d)) if folded is None else folded
    dof, of = (_fold(do_src, d), _fold(o, d)) if folded_do_o is None else folded_do_o
    args = [qkf, qkf, qkf, vf, vf, dof, of, _fold(lse, d)]
    if has_sink:
        in_specs.append(_fullspec((N_HEADS, LANES)))
        args.append(sink)
    qspec = pl.BlockSpec((BLOCK, ATT_W), lambda r, n: (qrow(n), r))
    kspec = pl.BlockSpec((BLOCK, kvw), lambda r, n: (krow(n), r))
    out_specs = [qspec, kspec, kspec]
    out_shape = [jax.ShapeDtypeStruct((s // d, d * ATT_W), out_dtype),
                 jax.ShapeDtypeStruct((s // d, d * kvw), out_dtype),
                 jax.ShapeDtypeStruct((s // d, d * kvw), out_dtype)]
    if has_sink:
        out_specs.append(_fullspec((N_HEADS, LANES)))
        out_shape.append(jax.ShapeDtypeStruct((N_HEADS, LANES), F32))
    out = pl.pallas_call(
        body, grid=(d, nb + 1), name=name, in_specs=in_specs, out_specs=out_specs,
        out_shape=out_shape,
        scratch_shapes=[pltpu.VMEM((BLOCK, kvw), F32), pltpu.VMEM((BLOCK, kvw), F32),
                        pltpu.VMEM((BLOCK, ATT_W), F32), pltpu.VMEM((BLOCK, kvw), F32), pltpu.VMEM((BLOCK, kvw), F32)],
        compiler_params=_cp("arbitrary", "arbitrary"))(*args)
    res = list(out[:3]) if keep_folded else [out[0].reshape(s, ATT_W), out[1].reshape(s, kvw), out[2].reshape(s, kvw)]
    if has_sink:
        res.append(out[3])
    return res


def combine_fwd(os_, lses, dils, name):
    s = os_[0].shape[0] * dils[0]
    tm = _tile(s, (256,))
    fold_dils = [d for d in dils if d > 1]

    def body(o1, o2, o3, l1, l2, l3, c_ref, l_ref, *rest):
        c_buf = rest[-1]
        cache = {}
        o = [r[...].astype(F32) if d == 1 else _unfold_rows(r[...], d, cache) for r, d in zip((o1, o2, o3), dils)]
        a, b, c = [r[...] if d == 1 else _unfold_rows_f32(r[...], d, cache) for r, d in zip((l1, l2, l3), dils)]
        m = jnp.maximum(jnp.maximum(a, b), c)
        wa, wb, wc = jnp.exp(a - m), jnp.exp(b - m), jnp.exp(c - m)
        tot = wa + wb + wc
        l_ref[...] = m + jnp.log(tot)
        rt = 1.0 / tot
        wa, wb, wc = wa * rt, wb * rt, wc * rt
        for h in range(N_HEADS):
            cs = slice(h * HEAD_DIM, (h + 1) * HEAD_DIM)
            c_buf[:, cs] = (wa[:, h:h + 1] * o[0][:, cs] + wb[:, h:h + 1] * o[1][:, cs] + wc[:, h:h + 1] * o[2][:, cs])
        mix = c_buf[...].astype(ACT)
        c_ref[...] = mix
        for f_ref, d in zip(rest[:-1], fold_dils):
            f_ref[...] = _fold_rows(mix, d, cache)

    return pl.pallas_call(
        body, grid=(s // tm,), name=name,
        in_specs=[_rowspec(tm // d, d * ATT_W) for d in dils] + [_rowspec(tm // d, d * LANES) for d in dils],
        out_specs=[_rowspec(tm, ATT_W), _rowspec(tm, LANES)] + [_rowspec(tm // d, d * ATT_W) for d in fold_dils],
        out_shape=[jax.ShapeDtypeStruct((s, ATT_W), ACT), jax.ShapeDtypeStruct((s, LANES), F32)]
        + [jax.ShapeDtypeStruct((s // d, d * ATT_W), ACT) for d in fold_dils],
        scratch_shapes=[pltpu.VMEM((tm, ATT_W), F32)],
        compiler_params=_cp("parallel"))(*os_, *lses)


def assemble(parts, tabs, name):
    terms_of = [[t if isinstance(t, tuple) else (t, 1) for t in terms] for terms, _ in parts]
    flat = [t for ts in terms_of for t in ts]
    s = terms_of[0][0][0].shape[0] * terms_of[0][0][1]
    tm = _tile(s, (256,) if any(d > 1 for _, d in flat) else (512,))
    widths = [ts[0][0].shape[1] // ts[0][1] for ts in terms_of]
    flags = [f for _, f in parts]

    def body(*refs):
        c_ref, sa_ref, sb_ref, o_ref = refs[len(flat):]
        c, sa, sb = c_ref[...], sa_ref[...], sb_ref[...]
        off = 0
        first = 0
        cache = {}
        for wd, ts, fl in zip(widths, terms_of, flags):
            t_refs = refs[first:first + len(ts)]
            first += len(ts)
            x = None
            for t_ref, (_, d) in zip(t_refs, ts):
                t = t_ref[...].astype(F32) if d == 1 else _unfold_rows(t_ref[...], d, cache)
                x = t if x is None else x + t
            for g in range(wd // LANES):
                cols = slice(g * LANES, (g + 1) * LANES)
                y = _rope_t(x[:, cols], c, sa, sb) if fl else x[:, cols]
                o_ref[:, off + g * LANES:off + (g + 1) * LANES] = y.astype(ACT)
            off += wd

    tot = sum(widths)
    return pl.pallas_call(
        body, grid=(s // tm,), name=name,
        in_specs=[_rowspec(tm // d, a.shape[1]) for a, d in flat] + [_rowspec(tm, LANES)] * 3,
        out_specs=_rowspec(tm, tot), out_shape=jax.ShapeDtypeStruct((s, tot), ACT),
        compiler_params=_cp("parallel"))(*[a for a, _ in flat], *tabs)


def _ln_stats(x):
    mu = jnp.mean(x, axis=-1, keepdims=True)
    xc = x - mu
    var = jnp.mean(xc * xc, axis=-1, keepdims=True)
    rstd = lax.rsqrt(var + LN_EPS)
    return xc * rstd, rstd


SUBLANES = 8


TAP_ROWS = 32


def _tap_sum(buf, cw_ref, offsets, tm, res_ref):
    for r0 in range(0, tm, TAP_ROWS):
        acc = None
        for ph in range(SUBLANES):
            taps = [j for j, off in enumerate(offsets) if off % SUBLANES == ph]
            if not taps:
                continue
            rows = TAP_ROWS if ph == 0 else TAP_ROWS + SUBLANES
            part = None
            for j in taps:
                term = cw_ref[j:j + 1, :] * buf[pl.ds(offsets[j] - ph + r0, rows), :]
                part = term if part is None else part + term
            part = part[ph:ph + TAP_ROWS]
            acc = part if acc is None else acc + part
        res_ref[pl.ds(r0, TAP_ROWS), :] = acc


def conv_fwd(ga, gb, cw, cb, lg, lb, name):
    s = ga.shape[0]
    tm = _tile(s, (512,))
    hb = tm // CONV_HALO

    def body(ga_ref, gb_ref, gah_ref, gbh_ref, cw_ref, cb_ref, lg_ref, lb_ref, c_ref, c1_ref, buf):
        i = pl.program_id(0)
        halo = gah_ref[...].astype(F32) * _sigmoid(gbh_ref[...].astype(F32))
        buf[0:CONV_HALO, :] = jnp.where(i > 0, halo, 0.0)
        buf[CONV_HALO:, :] = ga_ref[...].astype(F32) * _sigmoid(gb_ref[...].astype(F32))
        first = CONV_HALO - (CONV_WIDTH - 1)
        _tap_sum(buf, cw_ref, [first + j for j in range(CONV_WIDTH)], tm, c1_ref)
        acc = c1_ref[...] + cb_ref[...]
        c1_ref[...] = acc
        xh, _ = _ln_stats(acc)
        y = xh * lg_ref[...] + lb_ref[...]
        c_ref[...] = (y * _sigmoid(y)).astype(ACT)

    hspec = pl.BlockSpec((CONV_HALO, CONV_CH), lambda i: (jnp.maximum(i * hb - 1, 0), 0))
    vec = _fullspec((1, CONV_CH))
    spec = _rowspec(tm, CONV_CH)
    return pl.pallas_call(
        body, grid=(s // tm,), name=name,
        in_specs=[spec, spec, hspec, hspec, _fullspec((CONV_WIDTH, CONV_CH)), vec, vec, vec],
        out_specs=[spec, spec],
        out_shape=[jax.ShapeDtypeStruct((s, CONV_CH), ACT), jax.ShapeDtypeStruct((s, CONV_CH), F32)],
        scratch_shapes=[pltpu.VMEM((tm + CONV_HALO, CONV_CH), F32)],
        compiler_params=_cp("parallel"))(ga, gb, ga, gb, cw, cb, lg, lb)


def conv_bwd_ln(c1, dsrc, dcol, lg, lb, name):
    s = c1.shape[0]
    tm = _tile(s, (512,))

    def body(c1_ref, d_ref, lg_ref, lb_ref, o_ref, dg_ref, db_ref):
        @pl.when(pl.program_id(0) == 0)
        def _():
            dg_ref[...] = jnp.zeros_like(dg_ref)
            db_ref[...] = jnp.zeros_like(db_ref)

        xh, rstd = _ln_stats(c1_ref[...].astype(F32))
        y = xh * lg_ref[...] + lb_ref[...]
        sg = _sigmoid(y)
        dy = d_ref[...].astype(F32) * (sg * (1.0 + y * (1.0 - sg)))
        dg_ref[0:1, :] = dg_ref[0:1, :] + jnp.sum(dy * xh, axis=0, keepdims=True)
        db_ref[0:1, :] = db_ref[0:1, :] + jnp.sum(dy, axis=0, keepdims=True)
        dxh = dy * lg_ref[...]
        o_ref[...] = rstd * (dxh - jnp.mean(dxh, axis=-1, keepdims=True)
                             - xh * jnp.mean(dxh * xh, axis=-1, keepdims=True))

    vec = _fullspec((1, CONV_CH))
    acc = _fullspec((8, CONV_CH))
    return pl.pallas_call(
        body, grid=(s // tm,), name=name,
        in_specs=[_rowspec(tm, CONV_CH), _rowspec(tm, CONV_CH, dcol), vec, vec],
        out_specs=[_rowspec(tm, CONV_CH), acc, acc],
        out_shape=[jax.ShapeDtypeStruct((s, CONV_CH), F32)] + [jax.ShapeDtypeStruct((8, CONV_CH), F32)] * 2,
        compiler_params=_cp("arbitrary"))(c1, dsrc, lg, lb)


def conv_bwd_conv(dc1, ga, gb, cw, name):
    s = ga.shape[0]
    tm = _tile(s, (512,))
    hb = tm // CONV_HALO
    nt = s // tm
    last_h = s // CONV_HALO - 1
    first = CONV_HALO - (CONV_WIDTH - 1)


    def rows8(x):
        return jnp.sum(x.reshape(x.shape[0] // SUBLANES, SUBLANES, CONV_CH), axis=0)

    def body(d_ref, dn_ref, ga_ref, gb_ref, gah_ref, gbh_ref, cw_ref,
             dga_ref, dgb_ref, dw_ref, db_ref, dbuf, cbuf, sbuf):
        i = pl.program_id(0)

        @pl.when(i == 0)
        def _():
            dw_ref[...] = jnp.zeros_like(dw_ref)
            db_ref[...] = jnp.zeros_like(db_ref)

        d = d_ref[...]
        dbuf[0:tm, :] = d
        dbuf[tm:, :] = jnp.where(i < nt - 1, dn_ref[...], 0.0)
        halo = gah_ref[...].astype(F32) * _sigmoid(gbh_ref[...].astype(F32))
        cbuf[0:CONV_HALO, :] = jnp.where(i > 0, halo, 0.0)
        a = ga_ref[...].astype(F32)
        sg = _sigmoid(gb_ref[...].astype(F32))
        cbuf[CONV_HALO:, :] = a * sg
        for ph in range(SUBLANES):
            taps = [j for j in range(CONV_WIDTH) if (first + j) % SUBLANES == ph]
            if ph:
                sbuf[0:tm + CONV_HALO - SUBLANES, :] = cbuf[pl.ds(ph, tm + CONV_HALO - SUBLANES), :]
            src = sbuf if ph else cbuf
            for r0 in range(0, tm, TAP_ROWS):
                d_blk = dbuf[pl.ds(r0, TAP_ROWS), :]
                for j in taps:
                    tap = src[pl.ds(first + j - ph + r0, TAP_ROWS), :]
                    rows = slice(j * SUBLANES, (j + 1) * SUBLANES)
                    dw_ref[rows, :] = dw_ref[rows, :] + rows8(d_blk * tap)
        db_ref[...] = db_ref[...] + rows8(d)
        _tap_sum(dbuf, cw_ref, [CONV_WIDTH - 1 - j for j in range(CONV_WIDTH)], tm, sbuf)
        dc0 = sbuf[0:tm, :]
        dga_ref[...] = (dc0 * sg).astype(ACT)
        dgb_ref[...] = (dc0 * a * sg * (1.0 - sg)).astype(ACT)

    spec = _rowspec(tm, CONV_CH)
    hprev = pl.BlockSpec((CONV_HALO, CONV_CH), lambda i: (jnp.maximum(i * hb - 1, 0), 0))
    hnext = pl.BlockSpec((CONV_HALO, CONV_CH), lambda i: (jnp.minimum((i + 1) * hb, last_h), 0))
    return pl.pallas_call(
        body, grid=(nt,), name=name,
        in_specs=[spec, hnext, spec, spec, hprev, hprev, _fullspec((CONV_WIDTH, CONV_CH))],
        out_specs=[spec, spec, _fullspec((CONV_HALO * SUBLANES, CONV_CH)), _fullspec((SUBLANES, CONV_CH))],
        out_shape=[jax.ShapeDtypeStruct((s, CONV_CH), ACT)] * 2
        + [jax.ShapeDtypeStruct((CONV_HALO * SUBLANES, CONV_CH), F32), jax.ShapeDtypeStruct((SUBLANES, CONV_CH), F32)],
        scratch_shapes=[pltpu.VMEM((tm + CONV_HALO, CONV_CH), F32)] * 3,
        compiler_params=_cp("arbitrary"))(dc1, dc1, ga, gb, ga, gb, cw)


_GELU_K = math.sqrt(2.0 / math.pi)
_GELU_C = 0.044715


def _gelu(x):
    return 0.5 * x * (1.0 + jnp.tanh(_GELU_K * (x + _GELU_C * x * x * x)))


def _gelu_grad(x):
    t = jnp.tanh(_GELU_K * (x + _GELU_C * x * x * x))
    return 0.5 * (1.0 + t) + 0.5 * x * (1.0 - t * t) * _GELU_K * (1.0 + 3.0 * _GELU_C * x * x)


def _tril():
    qi = lax.broadcasted_iota(jnp.int32, (BLOCK, BLOCK), 0)
    kj = lax.broadcasted_iota(jnp.int32, (BLOCK, BLOCK), 1)
    return kj <= qi


GMLP_CHUNKS = 4


def _gmlp_weights(sw_ref, w_buf, wt_buf):
    tril = _tril()
    for g in range(D_GROUPS):
        w = jnp.where(tril, sw_ref[g], 0.0)
        w_buf[g] = w.astype(ACT)
        if wt_buf is not None:
            wt_buf[g] = w.T.astype(ACT)


def _gmlp_mix(w_buf, gn, sb_ref, m_buf):
    for c in range(GMLP_CHUNKS):
        rows = slice(c * BLOCK, (c + 1) * BLOCK)
        for g in range(D_GROUPS):
            cs = slice(g * HEAD_DIM, (g + 1) * HEAD_DIM)
            m_buf[rows, cs] = _dot(w_buf[g], gn[rows, cs], NN) + sb_ref[:, cs]


def gmlp_fwd(z, lg, lb, sw, sbx, name):
    s = z.shape[0]
    tm = GMLP_CHUNKS * BLOCK

    def body(z_ref, lg_ref, lb_ref, sw_ref, sb_ref, o_ref, w_buf, m_buf):
        @pl.when(pl.program_id(0) == 0)
        def _():
            _gmlp_weights(sw_ref, w_buf, None)

        zz = _gelu(z_ref[...].astype(F32))
        xh, _ = _ln_stats(zz[:, D_CH:])
        gn = (xh * lg_ref[...] + lb_ref[...]).astype(ACT)
        _gmlp_mix(w_buf, gn, sb_ref, m_buf)
        o_ref[...] = (zz[:, :D_CH] * m_buf[...]).astype(ACT)

    return pl.pallas_call(
        body, grid=(s // tm,), name=name,
        in_specs=[_rowspec(tm, 2 * D_CH), _fullspec((1, D_CH)), _fullspec((1, D_CH)),
                  _fullspec((D_GROUPS, BLOCK, BLOCK)), _fullspec((BLOCK, D_CH))],
        out_specs=_rowspec(tm, D_CH), out_shape=jax.ShapeDtypeStruct((s, D_CH), ACT),
        scratch_shapes=[pltpu.VMEM((D_GROUPS, BLOCK, BLOCK), ACT), pltpu.VMEM((tm, D_CH), F32)],
        compiler_params=_cp("arbitrary"))(z, lg, lb, sw, sbx)


def gmlp_bwd(z, dsrc, dcol, lg, lb, sw, sbx, name):
    s = z.shape[0]
    tm = GMLP_CHUNKS * BLOCK

    def body(z_ref, d_ref, lg_ref, lb_ref, sw_ref, sb_ref, dz_ref, dw_ref, dsb_ref, dg_ref, db_ref,
             w_buf, wt_buf, m_buf, dgn_buf):
        @pl.when(pl.program_id(0) == 0)
        def _():
            _gmlp_weights(sw_ref, w_buf, wt_buf)
            dw_ref[...] = jnp.zeros_like(dw_ref)
            dsb_ref[...] = jnp.zeros_like(dsb_ref)
            dg_ref[...] = jnp.zeros_like(dg_ref)
            db_ref[...] = jnp.zeros_like(db_ref)

        zr = z_ref[...].astype(F32)
        zz = _gelu(zr)
        u = zz[:, :D_CH]
        xh, rstd = _ln_stats(zz[:, D_CH:])
        gn = (xh * lg_ref[...] + lb_ref[...]).astype(ACT)
        dd = d_ref[...].astype(F32)
        _gmlp_mix(w_buf, gn, sb_ref, m_buf)
        dz_ref[:, :D_CH] = (dd * m_buf[...] * _gelu_grad(zr[:, :D_CH])).astype(ACT)
        dmix = dd * u
        dmix_a = dmix.astype(ACT)
        dsb = dmix[0:BLOCK]
        for c in range(1, GMLP_CHUNKS):
            dsb = dsb + dmix[c * BLOCK:(c + 1) * BLOCK]
        dsb_ref[...] = dsb_ref[...] + dsb
        tril = _tril()
        for g in range(D_GROUPS):
            cs = slice(g * HEAD_DIM, (g + 1) * HEAD_DIM)
            dw = None
            for c in range(GMLP_CHUNKS):
                rows = slice(c * BLOCK, (c + 1) * BLOCK)
                t = _dot(dmix_a[rows, cs], gn[rows, cs], NT)
                dw = t if dw is None else dw + t
                dgn_buf[rows, cs] = _dot(wt_buf[g], dmix_a[rows, cs], NN)
            dw_ref[g] = dw_ref[g] + jnp.where(tril, dw, 0.0)
        dgn = dgn_buf[...]
        dg_ref[0:1, :] = dg_ref[0:1, :] + jnp.sum(dgn * xh, axis=0, keepdims=True)
        db_ref[0:1, :] = db_ref[0:1, :] + jnp.sum(dgn, axis=0, keepdims=True)
        dxh = dgn * lg_ref[...]
        dgate = rstd * (dxh - jnp.mean(dxh, axis=-1, keepdims=True)
                        - xh * jnp.mean(dxh * xh, axis=-1, keepdims=True))
        dz_ref[:, D_CH:] = (dgate * _gelu_grad(zr[:, D_CH:])).astype(ACT)

    vec = _fullspec((1, D_CH))
    acc = _fullspec((8, D_CH))
    wshape = (D_GROUPS, BLOCK, BLOCK)
    return pl.pallas_call(
        body, grid=(s // tm,), name=name,
        in_specs=[_rowspec(tm, 2 * D_CH), _rowspec(tm, D_CH, dcol), vec, vec, _fullspec(wshape),
                  _fullspec((BLOCK, D_CH))],
        out_specs=[_rowspec(tm, 2 * D_CH), _fullspec(wshape), _fullspec((BLOCK, D_CH)), acc, acc],
        out_shape=[jax.ShapeDtypeStruct((s, 2 * D_CH), ACT), jax.ShapeDtypeStruct(wshape, F32),
                   jax.ShapeDtypeStruct((BLOCK, D_CH), F32),
                   jax.ShapeDtypeStruct((8, D_CH), F32), jax.ShapeDtypeStruct((8, D_CH), F32)],
        scratch_shapes=[pltpu.VMEM(wshape, ACT), pltpu.VMEM(wshape, ACT),
                        pltpu.VMEM((tm, D_CH), F32), pltpu.VMEM((tm, D_CH), F32)],
        compiler_params=_cp("arbitrary"))(z, dsrc, lg, lb, sw, sbx)


def _rms_bwd(dn, x, g):
    r = lax.rsqrt(jnp.mean(x * x, axis=-1, keepdims=True) + RMS_EPS)
    u = dn * g
    dx = r * u - x * (r * r * r) * jnp.mean(x * u, axis=-1, keepdims=True)
    return dx, dn * x * r


def final_loss(h, g, target, name):
    s = h.shape[0]
    tm = _tile(s, (512,))

    def body(h_ref, g_ref, t_ref, loss_ref, dh_ref, dg_ref):
        @pl.when(pl.program_id(0) == 0)
        def _():
            loss_ref[...] = jnp.zeros_like(loss_ref)
            dg_ref[...] = jnp.zeros_like(dg_ref)

        x = h_ref[...]
        r = lax.rsqrt(jnp.mean(x * x, axis=-1, keepdims=True) + RMS_EPS)
        diff = x * r * g_ref[...] - t_ref[...]
        part = jnp.sum(jnp.sum(diff * diff, axis=-1, keepdims=True), axis=0, keepdims=True)
        loss_ref[...] = loss_ref[...] + part * (0.5 / D_MODEL)
        dx, dgt = _rms_bwd(diff * (1.0 / D_MODEL), x, g_ref[...])
        dh_ref[...] = dx
        dg_ref[0:1, :] = dg_ref[0:1, :] + jnp.sum(dgt, axis=0, keepdims=True)

    spec = _rowspec(tm, D_MODEL)
    return pl.pallas_call(
        body, grid=(s // tm,), name=name,
        in_specs=[spec, _fullspec((1, D_MODEL)), spec],
        out_specs=[_fullspec((8, LANES)), spec, _fullspec((8, D_MODEL))],
        out_shape=[jax.ShapeDtypeStruct((8, LANES), F32), jax.ShapeDtypeStruct((s, D_MODEL), F32),
                   jax.ShapeDtypeStruct((8, D_MODEL), F32)],
        compiler_params=_cp("arbitrary"))(h, g, target)


def mm_nt(dy, w, name, fold_dils=(), fold_cols=0):
    s, n = dy.shape
    k = w.shape[0]
    tm = _tile(s, (256,) if fold_dils else (512,))
    tk = k if fold_dils else _tile(k, (1024, 512))

    def body(d_ref, w_ref, o_ref, *f_refs):
        res = _dot(d_ref[...].astype(ACT), w_ref[...], NT).astype(ACT)
        o_ref[...] = res
        cache = {}
        for f_ref, d in zip(f_refs, fold_dils):
            f_ref[...] = _fold_rows(res[:, :fold_cols], d, cache)

    out = pl.pallas_call(
        body, grid=(k // tk, s // tm), name=name,
        in_specs=[pl.BlockSpec((tm, n), lambda j, i: (i, 0)), pl.BlockSpec((tk, n), lambda j, i: (j, 0))],
        out_specs=[pl.BlockSpec((tm, tk), lambda j, i: (i, j))]
        + [pl.BlockSpec((tm // d, d * fold_cols), lambda j, i: (i, 0)) for d in fold_dils],
        out_shape=[jax.ShapeDtypeStruct((s, k), ACT)]
        + [jax.ShapeDtypeStruct((s // d, d * fold_cols), ACT) for d in fold_dils],
        compiler_params=_cp("parallel", "parallel"))(dy, w)
    return out if fold_dils else out[0]


def ffn_down_bwd(dh, wd, dact_dgate, dact_dup, name):
    s = dh.shape[0]
    f = wd.shape[0]
    tm = _tile(s, (1024, 512))
    tf = _tile(f, (1408, 512, 256, 128))

    def body(d_ref, w_ref, g_ref, u_ref, dg_ref, du_ref):
        dact = _dot(d_ref[...].astype(ACT), w_ref[...], NT)
        dg_ref[...] = (dact * g_ref[...].astype(F32)).astype(ACT)
        du_ref[...] = (dact * u_ref[...].astype(F32)).astype(ACT)

    tile = pl.BlockSpec((tm, tf), lambda j, i: (i, j))
    return pl.pallas_call(
        body, grid=(f // tf, s // tm), name=name,
        in_specs=[pl.BlockSpec((tm, D_MODEL), lambda j, i: (i, 0)),
                  pl.BlockSpec((tf, D_MODEL), lambda j, i: (j, 0)), tile, tile],
        out_specs=[tile, tile], out_shape=[jax.ShapeDtypeStruct((s, f), ACT)] * 2,
        compiler_params=_cp("parallel", "parallel"))(dh, wd, dact_dgate, dact_dup)


def mm_nt_rms(parts, h, g, dh, name):
    s = h.shape[0]
    tm = _tile(s, (512,))
    wspec = lambda w: pl.BlockSpec(w.shape, lambda i: (0, 0), pipeline_mode=pl.Buffered(1))
    np_ = len(parts)

    def body(*refs):
        d_refs = refs[:np_]
        w_refs = refs[np_:2 * np_]
        h_ref, g_ref, dh_ref, o_ref, dg_ref = refs[2 * np_:]

        @pl.when(pl.program_id(0) == 0)
        def _():
            dg_ref[...] = jnp.zeros_like(dg_ref)

        dn = None
        for d_ref, w_ref in zip(d_refs, w_refs):
            t = _dot(d_ref[...], w_ref[...], NT)
            dn = t if dn is None else dn + t
        dx, dgt = _rms_bwd(dn, h_ref[...], g_ref[...])
        o_ref[...] = dh_ref[...] + dx
        dg_ref[0:1, :] = dg_ref[0:1, :] + jnp.sum(dgt, axis=0, keepdims=True)

    spec = _rowspec(tm, D_MODEL)
    return pl.pallas_call(
        body, grid=(s // tm,), name=name,
        in_specs=[_rowspec(tm, d.shape[1]) for d, _ in parts] + [wspec(w) for _, w in parts]
        + [spec, _fullspec((1, D_MODEL)), spec],
        out_specs=[spec, _fullspec((8, D_MODEL))],
        out_shape=[jax.ShapeDtypeStruct((s, D_MODEL), F32), jax.ShapeDtypeStruct((8, D_MODEL), F32)],
        compiler_params=_cp("arbitrary"))(*[d for d, _ in parts], *[w for _, w in parts], h, g, dh)


def mm_tn(a, b, name):
    s, k = a.shape
    n = b.shape[1]
    tk = _tile(k, (1024, 1408, 512, 256, 128))
    tn = _tile(n, (1408, 1280, 1024, 896, 512, 256, 128))
    ts = _tile(s, (2048, 512) if b.dtype == ACT else (1024, 512))
    nt = s // ts

    def body(a_ref, b_ref, o_ref, acc):
        t = _dot(a_ref[...].astype(ACT), b_ref[...].astype(ACT), TN)
        step = pl.program_id(2)

        @pl.when(step == 0)
        def _():
            acc[...] = t

        @pl.when(step > 0)
        def _():
            acc[...] = acc[...] + t

        @pl.when(step == nt - 1)
        def _():
            o_ref[...] = acc[...].astype(ACT)

    return pl.pallas_call(
        body, grid=(k // tk, n // tn, nt), name=name,
        in_specs=[pl.BlockSpec((ts, tk), lambda i, j, t: (t, i)), pl.BlockSpec((ts, tn), lambda i, j, t: (t, j))],
        out_specs=pl.BlockSpec((tk, tn), lambda i, j, t: (i, j)),
        out_shape=jax.ShapeDtypeStruct((k, n), ACT),
        scratch_shapes=[pltpu.VMEM((tk, tn), F32)],
        compiler_params=_cp("parallel", "parallel", "arbitrary"))(a, b)


def _adamw_math(w, g, m, v):
    m = ADAM_B1 * m + (1.0 - ADAM_B1) * g
    v = ADAM_B2 * v + (1.0 - ADAM_B2) * (g * g)
    m_hat = m / (1.0 - ADAM_B1 ** ADAM_STEP)
    v_hat = v / (1.0 - ADAM_B2 ** ADAM_STEP)
    delta = -ADAM_LR * (m_hat / (jnp.sqrt(v_hat) + ADAM_EPS) + ADAM_WD * w)
    return delta, m, v


def sum_adamw(parts, w, m, v, layer, others, name):
    nl, r, c = w.shape
    tr = _tile(r, (256, 128, 64, 32, 16, 8))

    def body(p_ref, w_ref, m_ref, v_ref, *rest):
        g_ref, d_ref, mo_ref, vo_ref = rest[-4:]
        g = p_ref[0].astype(F32)
        for i in range(1, N_DEV):
            g = g + p_ref[i].astype(F32)
        d, mm, vv = _adamw_math(w_ref[...], g, m_ref[...], v_ref[...])
        g_ref[...] = g
        d_ref[...] = d
        mo_ref[...] = mm
        vo_ref[...] = vv

    spec = pl.BlockSpec((None, tr, c), lambda i: (layer, i, 0))
    in_specs = [pl.BlockSpec((N_DEV, tr, c), lambda i: (0, i, 0))] + [spec] * 3
    args = [parts, w, m, v]
    aliases = {}
    if others is not None:
        in_specs += [pl.BlockSpec(memory_space=pl.ANY)] * 4
        args += list(others)
        aliases = {4 + j: j for j in range(4)}
    return pl.pallas_call(
        body, grid=(r // tr,), name=name, in_specs=in_specs, out_specs=[spec] * 4,
        out_shape=[jax.ShapeDtypeStruct((nl, r, c), F32)] * 4, input_output_aliases=aliases,
        compiler_params=_cp("parallel"))(*args)


def cast_layers(items, name):
    n = len(items)

    def body(*refs):
        for i in range(n):
            refs[n + i][...] = refs[i][...].astype(ACT)

    return pl.pallas_call(
        body, grid=(1,), name=name,
        in_specs=[pl.BlockSpec((None,) + w.shape[1:], lambda i, l=l: (l, 0, 0)) for w, l in items],
        out_specs=[_fullspec(w.shape[1:]) for w, _ in items],
        out_shape=[jax.ShapeDtypeStruct(w.shape[1:], ACT) for w, _ in items],
        compiler_params=_cp("arbitrary"))(*[w for w, _ in items])


def adamw_small(ws, gs, ms, vs, name):
    k = len(ws)

    def body(*refs):
        for i in range(k):
            w_ref, g_ref, m_ref, v_ref = (refs[j * k + i] for j in range(4))
            d, mm, vv = _adamw_math(w_ref[...], g_ref[...], m_ref[...], v_ref[...])
            refs[4 * k + i][...] = d
            refs[5 * k + i][...] = mm
            refs[6 * k + i][...] = vv

    shapes = [jax.ShapeDtypeStruct(w.shape, F32) for w in ws]
    specs = [_fullspec(w.shape) for w in ws]
    out = pl.pallas_call(
        body, grid=(1,), name=name, in_specs=specs * 4, out_specs=specs * 3, out_shape=shapes * 3,
        compiler_params=_cp("arbitrary"))(*ws, *gs, *ms, *vs)
    return out[:k], out[k:2 * k], out[2 * k:]


def sum_slots(x, out_dtype, name):
    g, r, c = x.shape
    tr = r if x.size * x.dtype.itemsize <= SMALL_BLOCK_BYTES else _tile(r, (256, 128, 64, 32, 16, 8))

    def body(x_ref, o_ref):
        acc = x_ref[0].astype(F32)
        for i in range(1, g):
            acc = acc + x_ref[i].astype(F32)
        o_ref[...] = acc.astype(o_ref.dtype)

    return pl.pallas_call(
        body, grid=(r // tr,), name=name,
        in_specs=[pl.BlockSpec((g, tr, c), lambda i: (0, i, 0))], out_specs=_rowspec(tr, c),
        out_shape=jax.ShapeDtypeStruct((r, c), out_dtype),
        compiler_params=_cp("parallel"))(x)


HBM_SPEC = pl.BlockSpec(memory_space=pltpu.HBM)
SEM_SPEC = pl.BlockSpec(memory_space=pltpu.SEMAPHORE)
DATAFLOW = pltpu.SideEffectType.DATAFLOW_SIDE_EFFECTING


def _my_rank():
    return 4 * lax.axis_index("x") + 2 * lax.axis_index("y") + lax.axis_index("c")


def _exchange_copies(x_refs, land_refs, send, recv, a2a):
    pos = [lax.axis_index(a) for a in AXES]
    me = _my_rank()
    copies = []
    for x_ref, land_ref, s_ref, r_ref in zip(x_refs, land_refs, send, recv):
        for k in range(N_DEV - 1):
            bits = ((k + 1) >> 2 & 1, (k + 1) >> 1 & 1, (k + 1) & 1)
            peer = tuple(1 - p if b else p for p, b in zip(pos, bits))
            prank = 4 * peer[0] + 2 * peer[1] + peer[2]
            copies.append(pltpu.make_async_remote_copy(
                src_ref=x_ref.at[prank] if a2a else x_ref, dst_ref=land_ref.at[me],
                send_sem=s_ref.at[k], recv_sem=r_ref.at[k], device_id=peer, device_id_type=MESH))
    return copies


def exchange_start(xs, a2a, name, after=None):
    n = len(xs)
    me = _my_rank()
    lands = []
    for x in xs:
        own = lax.dynamic_index_in_dim(x, me, 0, keepdims=True) if a2a else x[None]
        shape = x.shape if a2a else (N_DEV,) + x.shape
        lands.append(lax.dynamic_update_slice(lax.empty(shape, x.dtype), own, (me,) + (0,) * (len(shape) - 1)))

    def body(*refs):
        x_refs, land_refs = refs[:n], refs[n:2 * n]
        outs = refs[len(refs) - 4 * n - 1:]
        for cp in _exchange_copies(x_refs, land_refs, outs[:n], outs[n:2 * n], a2a):
            cp.start()
        token = outs[4 * n]
        token[...] = jnp.zeros_like(token)

    sems = [pltpu.SemaphoreType.DMA((N_DEV - 1,))] * n
    out = pl.pallas_call(
        body, name=name,
        out_shape=tuple(sems + sems + [pltpu.HBM(x.shape, x.dtype) for x in xs]
                        + [pltpu.HBM(l.shape, l.dtype) for l in lands] + [jax.ShapeDtypeStruct((8, LANES), F32)]),
        in_specs=[HBM_SPEC] * (2 * n) + ([] if after is None else [pl.BlockSpec(memory_space=pl.ANY)]),
        out_specs=tuple([SEM_SPEC] * (2 * n) + [HBM_SPEC] * (2 * n) + [pl.BlockSpec(memory_space=pltpu.VMEM)]),
        input_output_aliases={i: 2 * n + i for i in range(2 * n)},
        compiler_params=pltpu.CompilerParams(has_side_effects=DATAFLOW),
    )(*[pltpu.with_memory_space_constraint(a, pltpu.HBM) for a in list(xs) + lands], *([] if after is None else [after]))
    return (out[:n], out[n:2 * n], out[2 * n:3 * n], out[3 * n:4 * n]), out[4 * n]


def exchange_wait(handles, after, a2a, name):
    send, recv, x_thru, land_thru = handles
    n = len(x_thru)

    def body(*refs):
        x_refs, land_refs = refs[:n], refs[n:2 * n]
        s_refs, r_refs = refs[2 * n:3 * n], refs[3 * n:4 * n]
        for cp in _exchange_copies(x_refs, land_refs, s_refs, r_refs, a2a):
            cp.wait_send()
            cp.wait_recv()

    out = pl.pallas_call(
        body, name=name,
        out_shape=tuple([pltpu.HBM(a.shape, a.dtype) for a in list(x_thru) + list(land_thru)]),
        in_specs=[HBM_SPEC] * (2 * n) + [SEM_SPEC] * (2 * n) + [pl.BlockSpec(memory_space=pl.ANY)],
        out_specs=tuple([HBM_SPEC] * (2 * n)),
        input_output_aliases={i: i for i in range(2 * n)},
        compiler_params=pltpu.CompilerParams(has_side_effects=DATAFLOW),
    )(*x_thru, *land_thru, *send, *recv, after)
    return out[n:2 * n]


def _local_step(x, target, weight, emit, P):
    s = x.shape[0]
    tabs = _rope_tables(s)
    sinkb = jnp.broadcast_to(P["ev_sinks"].reshape(N_HEADS, 1), (N_HEADS, LANES))
    sbx = jnp.repeat(P["od_spatial_b"].reshape(D_GROUPS, BLOCK).T, HEAD_DIM, axis=1)
    sw = P["od_spatial_w"].reshape(D_GROUPS, BLOCK, BLOCK)
    fg = P["ffn_norm_g"]
    latest = [None]

    def out(name, layer, grad):
        tok = emit(name, layer, grad)
        if tok is not None:
            latest[0] = tok

    def dep(a):
        return a if latest[0] is None else a + latest[0][0:1, 0:1]

    n0 = rmsnorm(x, P["ev_norm_g"], "rms_in")
    qk_e, v_e, ga, gb = inproj(n0, weight("ev_w_in", 0, n0), tabs, ATT_W + 128,
                               (ATT_W + 128, 128, CONV_CH, CONV_CH), (), "ev_inproj")
    a_e, lse_e = attn_fwd(qk_e, v_e, d=1, hkv=A_KV_HEADS, max_dist=BLOCK - 1, sink=sinkb, out_dtype=ACT,
                          name="ev_attn")
    c_act, c1 = conv_fwd(ga, gb, P["ev_conv_w"], P["ev_conv_b"], P["ev_conv_ln_g"], P["ev_conv_ln_b"], "ev_conv")
    h1, n1 = mm_res([a_e, c_act], weight("ev_w_out", 0, c_act), x, fg[0:1], "ev_outproj")
    gate0, up0, act0 = ffn_up(n1, weight("ffn_w_gate", 0, n1), weight("ffn_w_up", 0, n1), "ffn0_up")
    h2, n2 = mm_res([act0], weight("ffn_w_down", 0, act0), h1, P["od_norm_g"], "ffn0_down")
    fold_dils = tuple(dil for _, dil in DILATED if dil > 1)
    qk_o, v_o, z, *pre = inproj(n2, weight("od_w_in", 0, n2), tabs, 2 * ATT_W, (2 * ATT_W, ATT_W, 2 * D_CH),
                                fold_dils, "od_inproj")
    folded = {dil: (pre[2 * i], pre[2 * i + 1]) for i, dil in enumerate(fold_dils)}
    outs, lses = [], []
    for window, dil in DILATED:
        assert window // dil == BLOCK
        o_r, l_r = attn_fwd(qk_o, v_o, d=dil, hkv=N_HEADS, max_dist=BLOCK, sink=None, out_dtype=ACT,
                            name="od_attn_d%d" % dil, folded=folded.get(dil), keep_folded=True)
        outs.append(o_r)
        lses.append(l_r)
    dils = [dil for _, dil in DILATED]
    c_out, lse_o, *c_folded = combine_fwd(outs, lses, dils, "od_combine")
    d_out = gmlp_fwd(z, P["od_sgu_ln_g"], P["od_sgu_ln_b"], sw, sbx, "od_gmlp")
    h3, n3 = mm_res([c_out, d_out], weight("od_w_out", 0, d_out), h2, fg[1:2], "od_outproj")
    gate1, up1, act1 = ffn_up(n3, weight("ffn_w_gate", 1, n3), weight("ffn_w_up", 1, n3), "ffn1_up")
    h4, _ = mm_res([act1], weight("ffn_w_down", 1, act1), h3, None, "ffn1_down")
    loss_part, dh4, dg_final = final_loss(h4, P["final_norm_g"], target, "loss_head")

    def ffn_bwd(layer, dh_out, h_in, n_in, gate, up, act):
        wg, wu, wd = (weight(n, layer, dh_out) for n in ("ffn_w_gate", "ffn_w_up", "ffn_w_down"))
        tag = "ffn%d" % layer
        dgate, dup = ffn_down_bwd(dh_out, wd, gate, up, tag + "_down_bwd")
        g_wd = mm_tn(act, dh_out, tag + "_dwd")
        dh_in, dgn = mm_nt_rms([(dgate, wg), (dup, wu)], h_in, dep(fg[layer:layer + 1]), dh_out, tag + "_up_bwd")
        out("ffn_w_down", layer, g_wd)
        out("ffn_w_gate", layer, mm_tn(n_in, dgate, tag + "_dwg"))
        out("ffn_w_up", layer, mm_tn(n_in, dup, tag + "_dwu"))
        return dh_in, dgn[0:1]

    dh3, dgn_f1 = ffn_bwd(1, dh4, h3, n3, gate1, up1, act1)

    dcd, *dc_folded = mm_nt(dh3, weight("od_w_out", 0, dh3), "od_outproj_bwd", fold_dils, ATT_W)
    do_o = {dil: (dc_folded[i], c_folded[i]) for i, dil in enumerate(fold_dils)}
    dz, g_sw, g_sbx, g_slg, g_slb = gmlp_bwd(z, dcd, 1, dep(P["od_sgu_ln_g"]), P["od_sgu_ln_b"], sw, sbx,
                                             "od_gmlp_bwd")
    g_sb = jnp.sum(g_sbx.reshape(BLOCK, D_GROUPS, HEAD_DIM), axis=-1).T
    out("od_spatial", 0, jnp.concatenate([g_sw.reshape(D_GROUPS * BLOCK, BLOCK), g_sb], axis=0))
    out("od_w_out", 0, jnp.concatenate([mm_tn(c_out, dh3, "od_dwo_c"), mm_tn(d_out, dh3, "od_dwo_d")], axis=0))
    dqkv = [attn_bwd(qk_o, v_o, dcd if dil == 1 else dcd[:, :ATT_W], 0, c_out, lse_o, d=dil, hkv=N_HEADS,
                     max_dist=BLOCK, sink=None, out_dtype=ACT, name="od_attn_bwd_d%d" % dil,
                     folded=folded.get(dil), folded_do_o=do_o.get(dil), keep_folded=True)
            for window, dil in DILATED]
    dproj_o = assemble([([(b[j], dil) for b, dil in zip(dqkv, dils)], j < 2) for j in range(3)] + [([dz], False)],
                       tabs, "od_dproj")
    dh2, dgn_od = mm_nt_rms([(dproj_o, weight("od_w_in", 0, dproj_o))], h2, dep(P["od_norm_g"]), dh3,
                            "od_inproj_bwd")
    out("od_w_in", 0, mm_tn(n2, dproj_o, "od_dwi"))

    dh1, dgn_f0 = ffn_bwd(0, dh2, h1, n1, gate0, up0, act0)

    dac = mm_nt(dh1, weight("ev_w_out", 0, dh1), "ev_outproj_bwd")
    dc1, g_clg, g_clb = conv_bwd_ln(c1, dac, 1, dep(P["ev_conv_ln_g"]), P["ev_conv_ln_b"], "ev_conv_bwd_ln")
    out("ev_w_out", 0, jnp.concatenate([mm_tn(a_e, dh1, "ev_dwo_a"), mm_tn(c_act, dh1, "ev_dwo_c")], axis=0))
    dga, dgb, g_cw, g_cb = conv_bwd_conv(dc1, ga, gb, P["ev_conv_w"], "ev_conv_bwd")
    dq, dk, dv, dsink = attn_bwd(qk_e, v_e, dac, 0, a_e, lse_e, d=1, hkv=A_KV_HEADS, max_dist=BLOCK - 1,
                                 sink=sinkb, out_dtype=F32, name="ev_attn_bwd")
    dproj_e = assemble([([dq], True), ([dk], True), ([dv], False), ([dga], False), ([dgb], False)], tabs,
                       "ev_dproj")
    out("ev_w_in", 0, mm_tn(n0, dproj_e, "ev_dwi"))
    dx, dgn_ev = mm_nt_rms([(dproj_e, weight("ev_w_in", 0, dproj_e))], x, dep(P["ev_norm_g"]), dh1,
                           "ev_inproj_bwd")

    small = {
        "ev_norm_g": dgn_ev[0:1],
        "ev_sinks": dsink[:, 0:1].reshape(1, N_HEADS),
        "ev_conv_w": jnp.sum(g_cw.reshape(CONV_HALO, SUBLANES, CONV_CH), axis=1)[:CONV_WIDTH],
        "ev_conv_b": jnp.sum(g_cb, axis=0, keepdims=True),
        "ev_conv_ln_g": g_clg[0:1],
        "ev_conv_ln_b": g_clb[0:1],
        "od_norm_g": dgn_od[0:1],
        "od_sgu_ln_g": g_slg[0:1],
        "od_sgu_ln_b": g_slb[0:1],
        "od_spatial_w": g_sw.reshape(D_GROUPS * BLOCK, BLOCK),
        "od_spatial_b": g_sb,
        "ffn_norm_g": jnp.concatenate([dgn_f0, dgn_f1], axis=0),
        "final_norm_g": dg_final[0:1],
    }
    return loss_part, dx, small


BIG = ("ev_w_in", "ev_w_out", "od_w_in", "od_w_out", "ffn_w_gate", "ffn_w_up", "ffn_w_down")
COL_SHARDED = ("ev_w_in", "od_w_in", "ffn_w_gate", "ffn_w_up")
GATHER_GROUPS = (
    (("ev_w_in", 0),),
    (("ev_w_out", 0),),
    (("ffn_w_gate", 0), ("ffn_w_up", 0)),
    (("ffn_w_down", 0),),
    (("od_w_in", 0),),
    (("od_w_out", 0),),
    (("ffn_w_gate", 1), ("ffn_w_up", 1)),
    (("ffn_w_down", 1),),
)
GATHER_EARLY = 4
GATHER_LATE_AT = 2
REDUCE_GROUPS = (
    (("ffn_w_down", 1), ("ffn_w_gate", 1), ("ffn_w_up", 1)),
    (("od_w_out", 0),),
    (("od_w_in", 0),),
    (("ffn_w_down", 0), ("ffn_w_gate", 0), ("ffn_w_up", 0)),
    (("ev_w_out", 0),),
    (("ev_w_in", 0),),
)


def _unshard(name, g):
    if name in COL_SHARDED:
        return jnp.moveaxis(g, 0, 1).reshape(g.shape[1], N_DEV * g.shape[2])
    return g.reshape(N_DEV * g.shape[1], g.shape[2])


def _shard_slots(name, full):
    r, c = full.shape
    if name in COL_SHARDED:
        return jnp.moveaxis(full.reshape(r, N_DEV, c // N_DEV), 1, 0)
    return full.reshape(N_DEV, r // N_DEV, c)


def kernel(x, ev_norm_g, ev_w_in, ev_sinks, ev_conv_w, ev_conv_b, ev_conv_ln_g, ev_conv_ln_b, ev_w_out, od_norm_g, od_w_in, od_sgu_ln_g, od_sgu_ln_b, od_spatial_w, od_spatial_b, od_w_out, ffn_norm_g, ffn_w_gate, ffn_w_up, ffn_w_down, final_norm_g, loss_target, m_ev_norm_g, m_ev_w_in, m_ev_sinks, m_ev_conv_w, m_ev_conv_b, m_ev_conv_ln_g, m_ev_conv_ln_b, m_ev_w_out, m_od_norm_g, m_od_w_in, m_od_sgu_ln_g, m_od_sgu_ln_b, m_od_spatial_w, m_od_spatial_b, m_od_w_out, m_ffn_norm_g, m_ffn_w_gate, m_ffn_w_up, m_ffn_w_down, m_final_norm_g, v_ev_norm_g, v_ev_w_in, v_ev_sinks, v_ev_conv_w, v_ev_conv_b, v_ev_conv_ln_g, v_ev_conv_ln_b, v_ev_w_out, v_od_norm_g, v_od_w_in, v_od_sgu_ln_g, v_od_sgu_ln_b, v_od_spatial_w, v_od_spatial_b, v_od_w_out, v_ffn_norm_g, v_ffn_w_gate, v_ffn_w_up, v_ffn_w_down, v_final_norm_g):
    names = ["ev_norm_g", "ev_w_in", "ev_sinks", "ev_conv_w", "ev_conv_b", "ev_conv_ln_g", "ev_conv_ln_b", "ev_w_out",
             "od_norm_g", "od_w_in", "od_sgu_ln_g", "od_sgu_ln_b", "od_spatial_w", "od_spatial_b", "od_w_out",
             "ffn_norm_g", "ffn_w_gate", "ffn_w_up", "ffn_w_down", "final_norm_g"]
    wts = dict(zip(names, [ev_norm_g, ev_w_in, ev_sinks, ev_conv_w, ev_conv_b, ev_conv_ln_g, ev_conv_ln_b, ev_w_out,
                           od_norm_g, od_w_in, od_sgu_ln_g, od_sgu_ln_b, od_spatial_w, od_spatial_b, od_w_out,
                           ffn_norm_g, ffn_w_gate, ffn_w_up, ffn_w_down, final_norm_g]))
    mom = dict(zip(names, [m_ev_norm_g, m_ev_w_in, m_ev_sinks, m_ev_conv_w, m_ev_conv_b, m_ev_conv_ln_g, m_ev_conv_ln_b,
                           m_ev_w_out, m_od_norm_g, m_od_w_in, m_od_sgu_ln_g, m_od_sgu_ln_b, m_od_spatial_w,
                           m_od_spatial_b, m_od_w_out, m_ffn_norm_g, m_ffn_w_gate, m_ffn_w_up, m_ffn_w_down,
                           m_final_norm_g]))
    vel = dict(zip(names, [v_ev_norm_g, v_ev_w_in, v_ev_sinks, v_ev_conv_w, v_ev_conv_b, v_ev_conv_ln_g, v_ev_conv_ln_b,
                           v_ev_w_out, v_od_norm_g, v_od_w_in, v_od_sgu_ln_g, v_od_sgu_ln_b, v_od_spatial_w,
                           v_od_spatial_b, v_od_w_out, v_ffn_norm_g, v_ffn_w_gate, v_ffn_w_up, v_ffn_w_down,
                           v_final_norm_g]))
    me = _my_rank()

    sp = jnp.zeros((40, LANES), F32)
    sp = sp.at[0:CONV_WIDTH, 0:64].set(ev_conv_w[0])
    sp = sp.at[32, :].set(od_norm_g[0])
    sp = sp.at[33, 0:64].set(od_sgu_ln_g[0])
    sp = sp.at[34, 0:64].set(od_sgu_ln_b[0])

    early = [k for grp in GATHER_GROUPS[:GATHER_EARLY] for k in grp]
    late = [k for grp in GATHER_GROUPS[GATHER_EARLY:] for k in grp]
    early_act = dict(zip(early, cast_layers([(wts[n], l) for n, l in early], "cast_early")))
    late_act = dict(zip(late, cast_layers([(wts[n], l) for n, l in late], "cast_late")))
    ag_early, ag_token = exchange_start([sp] + [early_act[k] for k in early], False, "ag_start")
    ag_late = []
    full_w = {}
    P = {
        "ev_norm_g": ev_norm_g + ag_token[0:1, 0:1], "ev_sinks": ev_sinks, "ev_conv_b": ev_conv_b,
        "ev_conv_ln_g": ev_conv_ln_g, "ev_conv_ln_b": ev_conv_ln_b,
        "od_spatial_w": od_spatial_w, "od_spatial_b": od_spatial_b, "ffn_norm_g": ffn_norm_g,
        "final_norm_g": final_norm_g.reshape(1, D_MODEL),
    }

    def weight(name, layer, after):
        if (name, layer) not in full_w:
            gi = [i for i, grp in enumerate(GATHER_GROUPS) if (name, layer) in grp][0]
            if gi < GATHER_EARLY:
                idx = [1 + early.index(k) for k in GATHER_GROUPS[gi]]
                handles = ag_early
            else:
                idx = [late.index(k) for k in GATHER_GROUPS[gi]]
                handles = ag_late[0]
            if gi == 0:
                idx = [0] + idx
            lands = exchange_wait(tuple([h[i] for i in idx] for h in handles), after, False, "ag_wait%d" % gi)
            if gi == 0:
                spg, lands = lands[0], lands[1:]
                P["ev_conv_w"] = jnp.moveaxis(spg[:, 0:CONV_WIDTH, 0:64], 0, 1).reshape(CONV_WIDTH, CONV_CH)
                P["od_norm_g"] = spg[:, 32, :].reshape(1, D_MODEL)
                P["od_sgu_ln_g"] = spg[:, 33, 0:64].reshape(1, D_CH)
                P["od_sgu_ln_b"] = spg[:, 34, 0:64].reshape(1, D_CH)
            if gi == GATHER_LATE_AT:
                ag_late.append(exchange_start([late_act[k] for k in late], False, "ag_start_late", after=lands[0])[0])
            for k, land in zip(GATHER_GROUPS[gi], lands):
                full_w[k] = _unshard(k[0], land)
        return full_w[(name, layer)]

    pending, rs_started, spatial = {}, [], []

    def emit(name, layer, grad):
        if name == "od_spatial":
            handles, token = exchange_start([grad], False, "ar_start_b")
            spatial.append(handles)
            return token
        pending[(name, layer)] = grad
        for gi, grp in enumerate(REDUCE_GROUPS):
            if (name, layer) in grp and all(k in pending for k in grp):
                handles, token = exchange_start([_shard_slots(k[0], pending[k]).astype(ACT) for k in grp], True,
                                                "rs_start%d" % gi)
                rs_started.append((gi, handles))
                return token
        return None

    loss_part, dx, small = _local_step(x[0], loss_target[0], weight, emit, P)
    loss = lax.psum(loss_part[0, 0], AXES)

    wide = ["ev_norm_g", "ev_sinks", "ev_conv_w", "ev_conv_b", "ev_conv_ln_g", "ev_conv_ln_b", "od_norm_g",
            "od_sgu_ln_g", "od_sgu_ln_b", "ffn_norm_g", "final_norm_g"]
    blk_a = jnp.concatenate(
        [jnp.pad(small[n], ((0, 0), (0, D_MODEL - small[n].shape[1]))) for n in wide], axis=0)
    blk_a = jnp.pad(blk_a, ((0, 48 - blk_a.shape[0]), (0, 0)))
    ar_a, ar_token = exchange_start([blk_a], False, "ar_start_a")

    results = {}
    for gi, handles in rs_started:
        lands = exchange_wait(handles, ar_token, True, "rs_wait%d" % gi)
        for (n, l), land in zip(REDUCE_GROUPS[gi], lands):
            results[n] = sum_adamw(land, wts[n], mom[n], vel[n], l, results.get(n), "adamw_%s%d" % (n, l))
    out_g, out_d, out_m, out_v = {}, {}, {}, {}
    for n in BIG:
        out_g[n], out_d[n], out_m[n], out_v[n] = results[n]

    last = results[REDUCE_GROUPS[-1][-1][0]][0]
    sum_b = sum_slots(exchange_wait(spatial[0], last, False, "ar_wait_b")[0], F32, "ar_sum_b")
    sum_a = sum_slots(exchange_wait(ar_a, last, False, "ar_wait_a")[0], F32, "ar_sum_a")
    full = {}
    off = 0
    for n in wide:
        r_, c_ = small[n].shape
        full[n] = sum_a[off:off + r_, 0:c_]
        off += r_
    full["od_spatial_w"] = sum_b[0:D_GROUPS * BLOCK]
    full["od_spatial_b"] = sum_b[D_GROUPS * BLOCK:D_GROUPS * BLOCK + D_GROUPS]
    full["ev_conv_w"] = lax.dynamic_slice_in_dim(full["ev_conv_w"], me * 64, 64, axis=1)
    full["od_norm_g"] = lax.dynamic_slice_in_dim(full["od_norm_g"], me * 128, 128, axis=1)
    full["od_sgu_ln_g"] = lax.dynamic_slice_in_dim(full["od_sgu_ln_g"], me * 64, 64, axis=1)
    full["od_sgu_ln_b"] = lax.dynamic_slice_in_dim(full["od_sgu_ln_b"], me * 64, 64, axis=1)

    small_names = [n for n in names if n not in BIG]
    view = {n: ((-1, wts[n].shape[-1]) if wts[n].ndim > 1 else (1, -1)) for n in small_names}
    ds, ms, vs = adamw_small([wts[n].reshape(view[n]) for n in small_names],
                             [full[n].reshape(view[n]) for n in small_names],
                             [mom[n].reshape(view[n]) for n in small_names],
                             [vel[n].reshape(view[n]) for n in small_names], "adamw_small")
    for i, n in enumerate(small_names):
        shp = wts[n].shape
        out_g[n], out_d[n], out_m[n], out_v[n] = (full[n].reshape(shp), ds[i].reshape(shp), ms[i].reshape(shp),
                                                  vs[i].reshape(shp))

    return (loss, dx[None], *[out_g[n] for n in names], *[out_d[n] for n in names],
            *[out_m[n] for n in names], *[out_v[n] for n in names])
```

```python
import functools
import math

import jax
import jax.numpy as jnp
from jax import lax
from jax.experimental import pallas as pl
from jax.experimental.pallas import tpu as pltpu

F32 = jnp.float32
ACT = jnp.bfloat16

D_MODEL = 1024
HEAD_DIM = 64
N_HEADS = 8
ATT_W = N_HEADS * HEAD_DIM
A_KV_HEADS = 2
CONV_CH = 512
CONV_WIDTH = 31
CONV_HALO = 32
D_CH = 512
D_GROUPS = 8
BLOCK = 128
D_FF = 2816
ROT_DIM = 16
ROPE_THETA = 500000.0
RMS_EPS = 1e-6
LN_EPS = 1e-5
DILATED = ((128, 1), (512, 4), (2048, 16))
NEG = -1e30
LANES = 128

ADAM_LR = 0.001
ADAM_B1 = 0.9
ADAM_B2 = 0.999
ADAM_EPS = 1e-08
ADAM_WD = 0.01
ADAM_STEP = 10

V7X_VMEM_LIMIT = 56 * 1024 * 1024
SMALL_BLOCK_BYTES = 6 * 1024 * 1024
N_DEV = 8

NN = (((1,), (0,)), ((), ()))
NT = (((1,), (1,)), ((), ()))
TN = (((0,), (0,)), ((), ()))
MESH = pl.DeviceIdType.MESH
AXES = ("x", "y", "c")


def _dot(a, b, dims):
    return lax.dot_general(a, b, dims, preferred_element_type=F32)


def _cp(*sem):
    return pltpu.CompilerParams(dimension_semantics=sem if sem else None,
                                vmem_limit_bytes=V7X_VMEM_LIMIT)


def _tile(n, prefs):
    for p in prefs:
        if n % p == 0:
            return p
    return n


def _sigmoid(x):
    return 1.0 / (1.0 + jnp.exp(-x))


def _rowspec(tm, w, col=0):
    return pl.BlockSpec((tm, w), lambda i, col=col: (i, col))


def _fullspec(shape):
    nd = len(shape)
    return pl.BlockSpec(shape, lambda *a, nd=nd: (0,) * nd)


def _rope_tables(seq):
    half = ROT_DIM // 2
    inv_freq = ROPE_THETA ** (-jnp.arange(half, dtype=F32) * (2.0 / ROT_DIM))
    ang = jnp.arange(seq, dtype=jnp.int32).astype(F32)[:, None] * inv_freq[None, :]
    cos, sin = jnp.cos(ang), jnp.sin(ang)
    lane = jnp.arange(LANES)
    jm = lane % HEAD_DIM
    idx = jm % half
    c = jnp.where(jm[None, :] < ROT_DIM, cos[:, idx], 1.0)
    sa = jnp.where(jm[None, :] < half, -sin[:, idx], 0.0)
    sb = jnp.where((jm[None, :] >= half) & (jm[None, :] < ROT_DIM), sin[:, idx], 0.0)
    return c.astype(F32), sa.astype(F32), sb.astype(F32)


def _rope(x, c, sa, sb):
    return x * c + pltpu.roll(x, LANES - 8, 1) * sa + pltpu.roll(x, 8, 1) * sb


def _rope_t(d, c, sa, sb):
    return d * c + pltpu.roll(d * sa, 8, 1) + pltpu.roll(d * sb, LANES - 8, 1)


def rmsnorm(h, g, name):
    s = h.shape[0]
    tm = _tile(s, (512,))

    def body(h_ref, g_ref, o_ref):
        x = h_ref[...]
        r = lax.rsqrt(jnp.mean(x * x, axis=-1, keepdims=True) + RMS_EPS)
        o_ref[...] = (x * r * g_ref[...]).astype(o_ref.dtype)

    return pl.pallas_call(
        body, grid=(s // tm,), name=name,
        in_specs=[_rowspec(tm, D_MODEL), _fullspec((1, D_MODEL))],
        out_specs=_rowspec(tm, D_MODEL),
        out_shape=jax.ShapeDtypeStruct((s, D_MODEL), ACT),
        compiler_params=_cp("parallel"))(h, g)


def _fold_perm(tm, d, inverse, cache):
    key = (tm, d, inverse)
    if key not in cache:
        m = tm // d
        a = lax.broadcasted_iota(jnp.int32, (tm, tm), 1 if inverse else 0)
        b = lax.broadcasted_iota(jnp.int32, (tm, tm), 0 if inverse else 1)
        src = (a & (m - 1)) * d + (a >> (m.bit_length() - 1))
        cache[key] = (b == src).astype(ACT)
    return cache[key]


FOLD_ROWS = 256


def _fold_rows(x, d, cache):
    m = FOLD_ROWS // d
    out = []
    for r0 in range(0, x.shape[0], FOLD_ROWS):
        p = _dot(_fold_perm(FOLD_ROWS, d, False, cache), x[r0:r0 + FOLD_ROWS], NN).astype(ACT)
        out.append(jnp.concatenate([p[r * m:(r + 1) * m] for r in range(d)], axis=1))
    return out[0] if len(out) == 1 else jnp.concatenate(out, axis=0)


def _unfold_rows(blk, d, cache):
    w = blk.shape[1] // d
    m = FOLD_ROWS // d
    out = []
    for r0 in range(0, blk.shape[0], m):
        stacked = jnp.concatenate([blk[r0:r0 + m, r * w:(r + 1) * w] for r in range(d)], axis=0)
        out.append(_dot(_fold_perm(FOLD_ROWS, d, True, cache), stacked, NN))
    return out[0] if len(out) == 1 else jnp.concatenate(out, axis=0)


def _unfold_rows_f32(blk, d, cache):
    hi = blk.astype(ACT)
    r1 = blk - hi.astype(F32)
    mid = r1.astype(ACT)
    lo = (r1 - mid.astype(F32)).astype(ACT)
    return _unfold_rows(hi, d, cache) + _unfold_rows(mid, d, cache) + _unfold_rows(lo, d, cache)


def inproj(n, w, tabs, nqk, splits, fold_dils, name):
    s = n.shape[0]
    ntot = w.shape[1]
    assert sum(splits) == ntot and splits[0] == nqk
    tm = _tile(s, (512,))
    ns = len(splits)

    def body(n_ref, w_ref, c_ref, sa_ref, sb_ref, *outs):
        res = _dot(n_ref[...], w_ref[...], NN)
        c, sa, sb = c_ref[...], sa_ref[...], sb_ref[...]
        for g in range(nqk // LANES):
            x = res[:, g * LANES:(g + 1) * LANES]
            outs[0][:, g * LANES:(g + 1) * LANES] = _rope(x, c, sa, sb).astype(ACT)
        off = nqk
        for o_ref, wd in zip(outs[1:ns], splits[1:]):
            o_ref[...] = res[:, off:off + wd].astype(ACT)
            off += wd
        cache = {}
        for i, d in enumerate(fold_dils):
            outs[ns + 2 * i][...] = _fold_rows(outs[0][...], d, cache)
            outs[ns + 2 * i + 1][...] = _fold_rows(outs[1][...], d, cache)

    out_specs = [_rowspec(tm, wd) for wd in splits]
    out_shape = [jax.ShapeDtypeStruct((s, wd), ACT) for wd in splits]
    for d in fold_dils:
        for wd in splits[:2]:
            out_specs.append(_rowspec(tm // d, d * wd))
            out_shape.append(jax.ShapeDtypeStruct((s // d, d * wd), ACT))
    return pl.pallas_call(
        body, grid=(s // tm,), name=name,
        in_specs=[_rowspec(tm, D_MODEL), _fullspec((D_MODEL, ntot))] + [_rowspec(tm, LANES)] * 3,
        out_specs=out_specs, out_shape=out_shape,
        compiler_params=_cp("parallel"))(n, w, *tabs)


def ffn_up(n, wg, wu, name):
    s = n.shape[0]
    f = wg.shape[1]
    tm = _tile(s, (1024, 512))
    tf = _tile(f, (1408, 512, 256, 128))

    def body(n_ref, wg_ref, wu_ref, g_ref, u_ref, a_ref):
        a = n_ref[...]
        g = _dot(a, wg_ref[...], NN)
        u = _dot(a, wu_ref[...], NN)
        sg = _sigmoid(g)
        silu = g * sg
        g_ref[...] = (u * (sg * (1.0 + g * (1.0 - sg)))).astype(ACT)
        u_ref[...] = silu.astype(ACT)
        a_ref[...] = (silu * u).astype(ACT)

    wspec = pl.BlockSpec((D_MODEL, tf), lambda j, i: (0, j))
    ospec = pl.BlockSpec((tm, tf), lambda j, i: (i, j))
    return pl.pallas_call(
        body, grid=(f // tf, s // tm), name=name,
        in_specs=[pl.BlockSpec((tm, D_MODEL), lambda j, i: (i, 0)), wspec, wspec],
        out_specs=[ospec] * 3,
        out_shape=[jax.ShapeDtypeStruct((s, f), ACT)] * 3,
        compiler_params=_cp("parallel", "parallel"))(n, wg, wu)


def mm_res(parts, w, h, gnext, name):
    s = h.shape[0]
    tm = _tile(s, (512,))
    widths = [p.shape[1] for p in parts]
    assert sum(widths) == w.shape[0]
    np_ = len(parts)

    def body(*refs):
        p_refs = refs[:np_]
        w_ref, h_ref = refs[np_], refs[np_ + 1]
        rest = refs[np_ + 2:]
        acc = h_ref[...]
        off = 0
        for p_ref, wd in zip(p_refs, widths):
            acc = acc + _dot(p_ref[...], w_ref[off:off + wd, :], NN)
            off += wd
        if gnext is None:
            rest[0][...] = acc
        else:
            g_ref, ho_ref, no_ref = rest
            ho_ref[...] = acc
            r = lax.rsqrt(jnp.mean(acc * acc, axis=-1, keepdims=True) + RMS_EPS)
            no_ref[...] = (acc * r * g_ref[...]).astype(ACT)

    in_specs = [_rowspec(tm, wd) for wd in widths] + [_fullspec(w.shape), _rowspec(tm, D_MODEL)]
    args = list(parts) + [w, h]
    out_specs = [_rowspec(tm, D_MODEL)]
    out_shape = [jax.ShapeDtypeStruct((s, D_MODEL), F32)]
    if gnext is not None:
        in_specs.append(_fullspec((1, D_MODEL)))
        args.append(gnext)
        out_specs.append(_rowspec(tm, D_MODEL))
        out_shape.append(jax.ShapeDtypeStruct((s, D_MODEL), ACT))
    out = pl.pallas_call(
        body, grid=(s // tm,), name=name, in_specs=in_specs, out_specs=out_specs,
        out_shape=out_shape, compiler_params=_cp("parallel"))(*args)
    return (out[0], None) if gnext is None else (out[0], out[1])


def _band_mask(n, max_dist):
    qi = lax.broadcasted_iota(jnp.int32, (BLOCK, 2 * BLOCK), 0)
    kj = lax.broadcasted_iota(jnp.int32, (BLOCK, 2 * BLOCK), 1)
    dist = qi + BLOCK - kj
    valid = jnp.logical_and(dist >= 0, dist <= max_dist)
    return jnp.logical_and(valid, jnp.logical_or(kj >= BLOCK, n > 0))


def _band_mask_t(n, max_dist):
    kj = lax.broadcasted_iota(jnp.int32, (2 * BLOCK, BLOCK), 0)
    qi = lax.broadcasted_iota(jnp.int32, (2 * BLOCK, BLOCK), 1)
    dist = qi + BLOCK - kj
    valid = jnp.logical_and(dist >= 0, dist <= max_dist)
    return jnp.logical_and(valid, jnp.logical_or(kj >= BLOCK, n > 0))


ATT_FWD_BLOCKS = 4


def _fold(a, d):
    return a.reshape(a.shape[0] // d, d * a.shape[1])


def attn_fwd(qk, v, *, d, hkv, max_dist, sink, out_dtype, name, folded=None, keep_folded=False):
    s = qk.shape[0]
    kvw = hkv * HEAD_DIM
    wqk = ATT_W + kvw
    assert qk.shape[1] == wqk and (d == 1 or (wqk % ATT_W == 0 and wqk % kvw == 0))
    assert sink is None or max_dist == BLOCK - 1
    nb = s // d // BLOCK
    grp = N_HEADS // hkv
    qpb, kpb, koff = wqk // ATT_W, wqk // kvw, ATT_W // kvw

    qb = ATT_FWD_BLOCKS if nb % ATT_FWD_BLOCKS == 0 else 1

    def body(*refs):
        if sink is None:
            q_ref, kc_ref, kp_ref, vc_ref, vp_ref, o_ref, l_ref, o_buf = refs
        else:
            q_ref, kc_ref, kp_ref, vc_ref, vp_ref, s_ref, o_ref, l_ref, o_buf = refs
        n = pl.program_id(1)
        qsl = [slice(h * HEAD_DIM, (h + 1) * HEAD_DIM) for h in range(N_HEADS)]
        ksl = [slice((h // grp) * HEAD_DIM, (h // grp + 1) * HEAD_DIM) for h in range(N_HEADS)]
        head_row = lax.broadcasted_iota(jnp.int32, (SUBLANES, BLOCK), 0)
        if sink is not None:
            sink_row = lax.broadcasted_iota(jnp.int32, (2 * BLOCK, BLOCK), 0) == 0
        for j in range(qb):
            rows = slice(j * BLOCK, (j + 1) * BLOCK)
            before = slice((j - 1) * BLOCK, j * BLOCK)
            valid = _band_mask_t(n * qb + j, max_dist)
            kk = jnp.concatenate([kp_ref[...] if j == 0 else kc_ref[before, :], kc_ref[rows, :]], axis=0)
            vv = jnp.concatenate([vp_ref[...] if j == 0 else vc_ref[before, :], vc_ref[rows, :]], axis=0)
            scores = []
            for h in range(N_HEADS):
                q = q_ref[rows, qsl[h]] * 0.125
                scores.append(_dot(kk[:, ksl[h]], q, NT))
            probs = []
            lse8 = jnp.zeros((SUBLANES, BLOCK), F32)
            for h in range(N_HEADS):
                sc = jnp.where(valid, scores[h], NEG)
                if sink is not None:
                    sc = jnp.where(sink_row, s_ref[h:h + 1, 0:1], sc)
                m = jnp.max(sc, axis=0, keepdims=True)
                p = jnp.exp(sc - m)
                l = jnp.sum(p, axis=0, keepdims=True)
                if sink is not None:
                    p = jnp.where(sink_row, 0.0, p)
                lse8 = jnp.where(head_row == h, m + jnp.log(l), lse8)
                probs.append((p * (1.0 / l)).astype(ACT))
            l_ref[rows, :] = jnp.concatenate([lse8, jnp.zeros((BLOCK - SUBLANES, BLOCK), F32)], axis=0).T
            for h in range(N_HEADS):
                o_buf[rows, qsl[h]] = _dot(probs[h], vv[:, ksl[h]], TN)
        o_ref[...] = o_buf[...].astype(o_ref.dtype)

    prev = lambda n: jnp.maximum(n * qb - 1, 0)
    in_specs = [
        pl.BlockSpec((qb * BLOCK, ATT_W), lambda r, n: (n, r * qpb)),
        pl.BlockSpec((qb * BLOCK, kvw), lambda r, n: (n, r * kpb + koff)),
        pl.BlockSpec((BLOCK, kvw), lambda r, n: (prev(n), r * kpb + koff)),
        pl.BlockSpec((qb * BLOCK, kvw), lambda r, n: (n, r)),
        pl.BlockSpec((BLOCK, kvw), lambda r, n: (prev(n), r)),
    ]
    qkf, vf = (_fold(qk, d), _fold(v, d)) if folded is None else folded
    args = [qkf, qkf, qkf, vf, vf]
    if sink is not None:
        in_specs.append(_fullspec((N_HEADS, LANES)))
        args.append(sink)
    ospec = pl.BlockSpec((qb * BLOCK, ATT_W), lambda r, n: (n, r))
    lspec = pl.BlockSpec((qb * BLOCK, LANES), lambda r, n: (n, r))
    o, lse = pl.pallas_call(
        body, grid=(d, nb // qb), name=name, in_specs=in_specs, out_specs=[ospec, lspec],
        out_shape=[jax.ShapeDtypeStruct((s // d, d * ATT_W), out_dtype),
                   jax.ShapeDtypeStruct((s // d, d * LANES), F32)],
        scratch_shapes=[pltpu.VMEM((qb * BLOCK, ATT_W), F32)],
        compiler_params=_cp("parallel", "parallel"))(*args)
    return (o, lse) if keep_folded else (o.reshape(s, ATT_W), lse.reshape(s, LANES))


def attn_bwd(qk, v, do_src, do_col, o, lse, *, d, hkv, max_dist, sink, out_dtype, name, folded=None,
             folded_do_o=None, keep_folded=False):
    s = qk.shape[0]
    kvw = hkv * HEAD_DIM
    wqk = ATT_W + kvw
    nb = s // d // BLOCK
    grp = N_HEADS // hkv
    qpb, kpb, koff = wqk // ATT_W, wqk // kvw, ATT_W // kvw
    dob = do_src.shape[1] // ATT_W
    has_sink = sink is not None

    def body(*refs):
        refs = list(refs)
        q_ref, kc_ref, kp_ref, vc_ref, vp_ref, do_ref, o_ref, l_ref = refs[:8]
        pos = 8
        if has_sink:
            s_ref = refs[pos]
            pos += 1
        dq_ref, dk_ref, dv_ref = refs[pos:pos + 3]
        pos += 3
        if has_sink:
            ds_ref = refs[pos]
            pos += 1
        ck_ref, cv_ref, dq_buf, dk_buf, dv_buf = refs[pos:pos + 5]
        r_id = pl.program_id(0)
        n = pl.program_id(1)

        @pl.when(n == 0)
        def _():
            ck_ref[...] = jnp.zeros_like(ck_ref)
            cv_ref[...] = jnp.zeros_like(cv_ref)

        if has_sink:
            @pl.when(jnp.logical_and(n == 0, r_id == 0))
            def _():
                ds_ref[...] = jnp.zeros_like(ds_ref)

        @pl.when(n < nb)
        def _():
            valid = _band_mask_t(n, max_dist)
            qsl = [slice(h * HEAD_DIM, (h + 1) * HEAD_DIM) for h in range(N_HEADS)]
            ksl = [slice((h // grp) * HEAD_DIM, (h // grp + 1) * HEAD_DIM) for h in range(N_HEADS)]
            kk = jnp.concatenate([kp_ref[...], kc_ref[...]], axis=0)
            vv = jnp.concatenate([vp_ref[...], vc_ref[...]], axis=0)
            qs, first = [], []
            for h in range(N_HEADS):
                q = q_ref[:, qsl[h]] * 0.125
                qs.append(q)
                first.append((_dot(kk[:, ksl[h]], q, NT), _dot(vv[:, ksl[h]], do_ref[:, qsl[h]], NT)))
            lse_t = l_ref[...].T
            prod = do_ref[...].astype(F32) * o_ref[...].astype(F32)
            hi = prod.astype(ACT)
            lo = (prod - hi.astype(F32)).astype(ACT)
            col = lax.broadcasted_iota(jnp.int32, (LANES, ATT_W), 1)
            row = lax.broadcasted_iota(jnp.int32, (LANES, ATT_W), 0)
            head_of = jnp.logical_and(col >= row * HEAD_DIM, col < (row + 1) * HEAD_DIM).astype(ACT)
            e_t = _dot(head_of, hi, NT) + _dot(head_of, lo, NT)
            mid = []
            for h in range(N_HEADS):
                s_t, dp_t = first[h]
                lse_h, e_h = lse_t[h:h + 1, :], e_t[h:h + 1, :]
                p_t = jnp.exp(jnp.where(valid, s_t, NEG) - lse_h)
                mid.append(((p_t * (dp_t - e_h)).astype(ACT), p_t.astype(ACT)))
                if has_sink:
                    sk = s_ref[h:h + 1, 0:1]
                    dsk = -jnp.sum(jnp.exp(sk - lse_h) * e_h, axis=1, keepdims=True)
                    ds_ref[h:h + 1, :] = ds_ref[h:h + 1, :] + dsk
            dkk = [None] * hkv
            dvv = [None] * hkv
            for h in range(N_HEADS):
                kh = h // grp
                ds_t, p_t = mid[h]
                dq_buf[:, qsl[h]] = _dot(ds_t, kk[:, ksl[h]], TN) * 0.125
                for lst, val in ((dkk, _dot(ds_t, qs[h], NN)), (dvv, _dot(p_t, do_ref[:, qsl[h]], NN))):
                    lst[kh] = val if lst[kh] is None else lst[kh] + val
            for kh in range(hkv):
                ks = slice(kh * HEAD_DIM, (kh + 1) * HEAD_DIM)
                dk_buf[:, ks] = ck_ref[:, ks] + dkk[kh][:BLOCK]
                dv_buf[:, ks] = cv_ref[:, ks] + dvv[kh][:BLOCK]
                ck_ref[:, ks] = dkk[kh][BLOCK:]
                cv_ref[:, ks] = dvv[kh][BLOCK:]
            dq_ref[...] = dq_buf[...].astype(dq_ref.dtype)
            dk_ref[...] = dk_buf[...].astype(dk_ref.dtype)
            dv_ref[...] = dv_buf[...].astype(dv_ref.dtype)

        @pl.when(n == nb)
        def _():
            dk_ref[...] = ck_ref[...].astype(dk_ref.dtype)
            dv_ref[...] = cv_ref[...].astype(dv_ref.dtype)

    qrow = lambda n: jnp.minimum(n, nb - 1)
    prow = lambda n: jnp.maximum(jnp.minimum(n, nb - 1) - 1, 0)
    krow = lambda n: jnp.maximum(n - 1, 0)
    in_specs = [
        pl.BlockSpec((BLOCK, ATT_W), lambda r, n: (qrow(n), r * qpb)),
        pl.BlockSpec((BLOCK, kvw), lambda r, n: (qrow(n), r * kpb + koff)),
        pl.BlockSpec((BLOCK, kvw), lambda r, n: (prow(n), r * kpb + koff)),
        pl.BlockSpec((BLOCK, kvw), lambda r, n: (qrow(n), r)),
        pl.BlockSpec((BLOCK, kvw), lambda r, n: (prow(n), r)),
        pl.BlockSpec((BLOCK, ATT_W), lambda r, n: (qrow(n), r * dob + do_col)),
        pl.BlockSpec((BLOCK, ATT_W), lambda r, n: (qrow(n), r)),
        pl.BlockSpec((BLOCK, LANES), lambda r, n: (qrow(n), r)),
    ]
    qkf, vf = (_fold(qk, d), _fold(v, d)) if folded is None else folded
    dof, of = (_fold(do_src, d), _fold(o, d)) if folded_do_o is None else folded_do_o
    args = [qkf, qkf, qkf, vf, vf, dof, of, _fold(lse, d)]
    if has_sink:
        in_specs.append(_fullspec((N_HEADS, LANES)))
        args.append(sink)
    qspec = pl.BlockSpec((BLOCK, ATT_W), lambda r, n: (qrow(n), r))
    kspec = pl.BlockSpec((BLOCK, kvw), lambda r, n: (krow(n), r))
    out_specs = [qspec, kspec, kspec]
    out_shape = [jax.ShapeDtypeStruct((s // d, d * ATT_W), out_dtype),
                 jax.ShapeDtypeStruct((s // d, d * kvw), out_dtype),
                 jax.ShapeDtypeStruct((s // d, d * kvw), out_dtype)]
    if has_sink:
        out_specs.append(_fullspec((N_HEADS, LANES)))
        out_shape.append(jax.ShapeDtypeStruct((N_HEADS, LANES), F32))
    out = pl.pallas_call(
        body, grid=(d, nb + 1), name=name, in_specs=in_specs, out_specs=out_specs,
        out_shape=out_shape,
        scratch_shapes=[pltpu.VMEM((BLOCK, kvw), F32), pltpu.VMEM((BLOCK, kvw), F32),
                        pltpu.VMEM((BLOCK, ATT_W), F32), pltpu.VMEM((BLOCK, kvw), F32), pltpu.VMEM((BLOCK, kvw), F32)],
        compiler_params=_cp("arbitrary", "arbitrary"))(*args)
    res = list(out[:3]) if keep_folded else [out[0].reshape(s, ATT_W), out[1].reshape(s, kvw), out[2].reshape(s, kvw)]
    if has_sink:
        res.append(out[3])
    return res


def combine_fwd(os_, lses, dils, name):
    s = os_[0].shape[0] * dils[0]
    tm = _tile(s, (512,))
    fold_dils = [d for d in dils if d > 1]

    def body(o1, o2, o3, l1, l2, l3, c_ref, l_ref, *rest):
        c_buf = rest[-1]
        cache = {}
        o = [r[...].astype(F32) if d == 1 else _unfold_rows(r[...], d, cache) for r, d in zip((o1, o2, o3), dils)]
        a, b, c = [r[...] if d == 1 else _unfold_rows_f32(r[...], d, cache) for r, d in zip((l1, l2, l3), dils)]
        m = jnp.maximum(jnp.maximum(a, b), c)
        wa, wb, wc = jnp.exp(a - m), jnp.exp(b - m), jnp.exp(c - m)
        tot = wa + wb + wc
        l_ref[...] = m + jnp.log(tot)
        rt = 1.0 / tot
        wa, wb, wc = wa * rt, wb * rt, wc * rt
        for h in range(N_HEADS):
            cs = slice(h * HEAD_DIM, (h + 1) * HEAD_DIM)
            c_buf[:, cs] = (wa[:, h:h + 1] * o[0][:, cs] + wb[:, h:h + 1] * o[1][:, cs] + wc[:, h:h + 1] * o[2][:, cs])
        mix = c_buf[...].astype(ACT)
        c_ref[...] = mix
        for f_ref, d in zip(rest[:-1], fold_dils):
            f_ref[...] = _fold_rows(mix, d, cache)

    return pl.pallas_call(
        body, grid=(s // tm,), name=name,
        in_specs=[_rowspec(tm // d, d * ATT_W) for d in dils] + [_rowspec(tm // d, d * LANES) for d in dils],
        out_specs=[_rowspec(tm, ATT_W), _rowspec(tm, LANES)] + [_rowspec(tm // d, d * ATT_W) for d in fold_dils],
        out_shape=[jax.ShapeDtypeStruct((s, ATT_W), ACT), jax.ShapeDtypeStruct((s, LANES), F32)]
        + [jax.ShapeDtypeStruct((s // d, d * ATT_W), ACT) for d in fold_dils],
        scratch_shapes=[pltpu.VMEM((tm, ATT_W), F32)],
        compiler_params=_cp("parallel"))(*os_, *lses)


def assemble(parts, tabs, name):
    terms_of = [[t if isinstance(t, tuple) else (t, 1) for t in terms] for terms, _ in parts]
    flat = [t for ts in terms_of for t in ts]
    s = terms_of[0][0][0].shape[0] * terms_of[0][0][1]
    tm = _tile(s, (512,))
    widths = [ts[0][0].shape[1] // ts[0][1] for ts in terms_of]
    flags = [f for _, f in parts]

    def body(*refs):
        c_ref, sa_ref, sb_ref, o_ref = refs[len(flat):]
        c, sa, sb = c_ref[...], sa_ref[...], sb_ref[...]
        off = 0
        first = 0
        cache = {}
        for wd, ts, fl in zip(widths, terms_of, flags):
            t_refs = refs[first:first + len(ts)]
            first += len(ts)
            x = None
            for t_ref, (_, d) in zip(t_refs, ts):
                t = t_ref[...].astype(F32) if d == 1 else _unfold_rows(t_ref[...], d, cache)
                x = t if x is None else x + t
            for g in range(wd // LANES):
                cols = slice(g * LANES, (g + 1) * LANES)
                y = _rope_t(x[:, cols], c, sa, sb) if fl else x[:, cols]
                o_ref[:, off + g * LANES:off + (g + 1) * LANES] = y.astype(ACT)
            off += wd

    tot = sum(widths)
    return pl.pallas_call(
        body, grid=(s // tm,), name=name,
        in_specs=[_rowspec(tm // d, a.shape[1]) for a, d in flat] + [_rowspec(tm, LANES)] * 3,
        out_specs=_rowspec(tm, tot), out_shape=jax.ShapeDtypeStruct((s, tot), ACT),
        compiler_params=_cp("parallel"))(*[a for a, _ in flat], *tabs)


def _ln_stats(x):
    mu = jnp.mean(x, axis=-1, keepdims=True)
    xc = x - mu
    var = jnp.mean(xc * xc, axis=-1, keepdims=True)
    rstd = lax.rsqrt(var + LN_EPS)
    return xc * rstd, rstd


SUBLANES = 8


TAP_ROWS = 32


def _tap_sum(buf, cw_ref, offsets, tm, res_ref):
    for r0 in range(0, tm, TAP_ROWS):
        acc = None
        for ph in range(SUBLANES):
            taps = [j for j, off in enumerate(offsets) if off % SUBLANES == ph]
            if not taps:
                continue
            rows = TAP_ROWS if ph == 0 else TAP_ROWS + SUBLANES
            part = None
            for j in taps:
                term = cw_ref[j:j + 1, :] * buf[pl.ds(offsets[j] - ph + r0, rows), :]
                part = term if part is None else part + term
            part = part[ph:ph + TAP_ROWS]
            acc = part if acc is None else acc + part
        res_ref[pl.ds(r0, TAP_ROWS), :] = acc


def conv_fwd(ga, gb, cw, cb, lg, lb, name):
    s = ga.shape[0]
    tm = _tile(s, (512,))
    hb = tm // CONV_HALO

    def body(ga_ref, gb_ref, gah_ref, gbh_ref, cw_ref, cb_ref, lg_ref, lb_ref, c_ref, c1_ref, buf):
        i = pl.program_id(0)
        halo = gah_ref[...].astype(F32) * _sigmoid(gbh_ref[...].astype(F32))
        buf[0:CONV_HALO, :] = jnp.where(i > 0, halo, 0.0)
        buf[CONV_HALO:, :] = ga_ref[...].astype(F32) * _sigmoid(gb_ref[...].astype(F32))
        first = CONV_HALO - (CONV_WIDTH - 1)
        _tap_sum(buf, cw_ref, [first + j for j in range(CONV_WIDTH)], tm, c1_ref)
        acc = c1_ref[...] + cb_ref[...]
        c1_ref[...] = acc
        xh, _ = _ln_stats(acc)
        y = xh * lg_ref[...] + lb_ref[...]
        c_ref[...] = (y * _sigmoid(y)).astype(ACT)

    hspec = pl.BlockSpec((CONV_HALO, CONV_CH), lambda i: (jnp.maximum(i * hb - 1, 0), 0))
    vec = _fullspec((1, CONV_CH))
    spec = _rowspec(tm, CONV_CH)
    return pl.pallas_call(
        body, grid=(s // tm,), name=name,
        in_specs=[spec, spec, hspec, hspec, _fullspec((CONV_WIDTH, CONV_CH)), vec, vec, vec],
        out_specs=[spec, spec],
        out_shape=[jax.ShapeDtypeStruct((s, CONV_CH), ACT), jax.ShapeDtypeStruct((s, CONV_CH), F32)],
        scratch_shapes=[pltpu.VMEM((tm + CONV_HALO, CONV_CH), F32)],
        compiler_params=_cp("parallel"))(ga, gb, ga, gb, cw, cb, lg, lb)


def conv_bwd_ln(c1, dsrc, dcol, lg, lb, name):
    s = c1.shape[0]
    tm = _tile(s, (512,))

    def body(c1_ref, d_ref, lg_ref, lb_ref, o_ref, dg_ref, db_ref):
        @pl.when(pl.program_id(0) == 0)
        def _():
            dg_ref[...] = jnp.zeros_like(dg_ref)
            db_ref[...] = jnp.zeros_like(db_ref)

        xh, rstd = _ln_stats(c1_ref[...].astype(F32))
        y = xh * lg_ref[...] + lb_ref[...]
        sg = _sigmoid(y)
        dy = d_ref[...].astype(F32) * (sg * (1.0 + y * (1.0 - sg)))
        dg_ref[0:1, :] = dg_ref[0:1, :] + jnp.sum(dy * xh, axis=0, keepdims=True)
        db_ref[0:1, :] = db_ref[0:1, :] + jnp.sum(dy, axis=0, keepdims=True)
        dxh = dy * lg_ref[...]
        o_ref[...] = rstd * (dxh - jnp.mean(dxh, axis=-1, keepdims=True)
                             - xh * jnp.mean(dxh * xh, axis=-1, keepdims=True))

    vec = _fullspec((1, CONV_CH))
    acc = _fullspec((8, CONV_CH))
    return pl.pallas_call(
        body, grid=(s // tm,), name=name,
        in_specs=[_rowspec(tm, CONV_CH), _rowspec(tm, CONV_CH, dcol), vec, vec],
        out_specs=[_rowspec(tm, CONV_CH), acc, acc],
        out_shape=[jax.ShapeDtypeStruct((s, CONV_CH), F32)] + [jax.ShapeDtypeStruct((8, CONV_CH), F32)] * 2,
        compiler_params=_cp("arbitrary"))(c1, dsrc, lg, lb)


def conv_bwd_conv(dc1, ga, gb, cw, name):
    s = ga.shape[0]
    tm = _tile(s, (512,))
    hb = tm // CONV_HALO
    nt = s // tm
    last_h = s // CONV_HALO - 1
    first = CONV_HALO - (CONV_WIDTH - 1)


    def rows8(x):
        return jnp.sum(x.reshape(x.shape[0] // SUBLANES, SUBLANES, CONV_CH), axis=0)

    def body(d_ref, dn_ref, ga_ref, gb_ref, gah_ref, gbh_ref, cw_ref,
             dga_ref, dgb_ref, dw_ref, db_ref, dbuf, cbuf, sbuf):
        i = pl.program_id(0)

        @pl.when(i == 0)
        def _():
            dw_ref[...] = jnp.zeros_like(dw_ref)
            db_ref[...] = jnp.zeros_like(db_ref)

        d = d_ref[...]
        dbuf[0:tm, :] = d
        dbuf[tm:, :] = jnp.where(i < nt - 1, dn_ref[...], 0.0)
        halo = gah_ref[...].astype(F32) * _sigmoid(gbh_ref[...].astype(F32))
        cbuf[0:CONV_HALO, :] = jnp.where(i > 0, halo, 0.0)
        a = ga_ref[...].astype(F32)
        sg = _sigmoid(gb_ref[...].astype(F32))
        cbuf[CONV_HALO:, :] = a * sg
        for ph in range(SUBLANES):
            taps = [j for j in range(CONV_WIDTH) if (first + j) % SUBLANES == ph]
            if ph:
                sbuf[0:tm + CONV_HALO - SUBLANES, :] = cbuf[pl.ds(ph, tm + CONV_HALO - SUBLANES), :]
            src = sbuf if ph else cbuf
            for r0 in range(0, tm, TAP_ROWS):
                d_blk = dbuf[pl.ds(r0, TAP_ROWS), :]
                for j in taps:
                    tap = src[pl.ds(first + j - ph + r0, TAP_ROWS), :]
                    rows = slice(j * SUBLANES, (j + 1) * SUBLANES)
                    dw_ref[rows, :] = dw_ref[rows, :] + rows8(d_blk * tap)
        db_ref[...] = db_ref[...] + rows8(d)
        _tap_sum(dbuf, cw_ref, [CONV_WIDTH - 1 - j for j in range(CONV_WIDTH)], tm, sbuf)
        dc0 = sbuf[0:tm, :]
        dga_ref[...] = (dc0 * sg).astype(ACT)
        dgb_ref[...] = (dc0 * a * sg * (1.0 - sg)).astype(ACT)

    spec = _rowspec(tm, CONV_CH)
    hprev = pl.BlockSpec((CONV_HALO, CONV_CH), lambda i: (jnp.maximum(i * hb - 1, 0), 0))
    hnext = pl.BlockSpec((CONV_HALO, CONV_CH), lambda i: (jnp.minimum((i + 1) * hb, last_h), 0))
    return pl.pallas_call(
        body, grid=(nt,), name=name,
        in_specs=[spec, hnext, spec, spec, hprev, hprev, _fullspec((CONV_WIDTH, CONV_CH))],
        out_specs=[spec, spec, _fullspec((CONV_HALO * SUBLANES, CONV_CH)), _fullspec((SUBLANES, CONV_CH))],
        out_shape=[jax.ShapeDtypeStruct((s, CONV_CH), ACT)] * 2
        + [jax.ShapeDtypeStruct((CONV_HALO * SUBLANES, CONV_CH), F32), jax.ShapeDtypeStruct((SUBLANES, CONV_CH), F32)],
        scratch_shapes=[pltpu.VMEM((tm + CONV_HALO, CONV_CH), F32)] * 3,
        compiler_params=_cp("arbitrary"))(dc1, dc1, ga, gb, ga, gb, cw)


_GELU_K = math.sqrt(2.0 / math.pi)
_GELU_C = 0.044715


def _gelu(x):
    return 0.5 * x * (1.0 + jnp.tanh(_GELU_K * (x + _GELU_C * x * x * x)))


def _gelu_grad(x):
    t = jnp.tanh(_GELU_K * (x + _GELU_C * x * x * x))
    return 0.5 * (1.0 + t) + 0.5 * x * (1.0 - t * t) * _GELU_K * (1.0 + 3.0 * _GELU_C * x * x)


def _tril():
    qi = lax.broadcasted_iota(jnp.int32, (BLOCK, BLOCK), 0)
    kj = lax.broadcasted_iota(jnp.int32, (BLOCK, BLOCK), 1)
    return kj <= qi


GMLP_CHUNKS = 4


def _gmlp_weights(sw_ref, w_buf, wt_buf):
    tril = _tril()
    for g in range(D_GROUPS):
        w = jnp.where(tril, sw_ref[g], 0.0)
        w_buf[g] = w.astype(ACT)
        if wt_buf is not None:
            wt_buf[g] = w.T.astype(ACT)


def _gmlp_mix(w_buf, gn, sb_ref, m_buf):
    for c in range(GMLP_CHUNKS):
        rows = slice(c * BLOCK, (c + 1) * BLOCK)
        for g in range(D_GROUPS):
            cs = slice(g * HEAD_DIM, (g + 1) * HEAD_DIM)
            m_buf[rows, cs] = _dot(w_buf[g], gn[rows, cs], NN) + sb_ref[:, cs]


def gmlp_fwd(z, lg, lb, sw, sbx, name):
    s = z.shape[0]
    tm = GMLP_CHUNKS * BLOCK

    def body(z_ref, lg_ref, lb_ref, sw_ref, sb_ref, o_ref, w_buf, m_buf):
        @pl.when(pl.program_id(0) == 0)
        def _():
            _gmlp_weights(sw_ref, w_buf, None)

        zz = _gelu(z_ref[...].astype(F32))
        xh, _ = _ln_stats(zz[:, D_CH:])
        gn = (xh * lg_ref[...] + lb_ref[...]).astype(ACT)
        _gmlp_mix(w_buf, gn, sb_ref, m_buf)
        o_ref[...] = (zz[:, :D_CH] * m_buf[...]).astype(ACT)

    return pl.pallas_call(
        body, grid=(s // tm,), name=name,
        in_specs=[_rowspec(tm, 2 * D_CH), _fullspec((1, D_CH)), _fullspec((1, D_CH)),
                  _fullspec((D_GROUPS, BLOCK, BLOCK)), _fullspec((BLOCK, D_CH))],
        out_specs=_rowspec(tm, D_CH), out_shape=jax.ShapeDtypeStruct((s, D_CH), ACT),
        scratch_shapes=[pltpu.VMEM((D_GROUPS, BLOCK, BLOCK), ACT), pltpu.VMEM((tm, D_CH), F32)],
        compiler_params=_cp("arbitrary"))(z, lg, lb, sw, sbx)


def gmlp_bwd(z, dsrc, dcol, lg, lb, sw, sbx, name):
    s = z.shape[0]
    tm = GMLP_CHUNKS * BLOCK

    def body(z_ref, d_ref, lg_ref, lb_ref, sw_ref, sb_ref, dz_ref, dw_ref, dsb_ref, dg_ref, db_ref,
             w_buf, wt_buf, m_buf, dgn_buf):
        @pl.when(pl.program_id(0) == 0)
        def _():
            _gmlp_weights(sw_ref, w_buf, wt_buf)
            dw_ref[...] = jnp.zeros_like(dw_ref)
            dsb_ref[...] = jnp.zeros_like(dsb_ref)
            dg_ref[...] = jnp.zeros_like(dg_ref)
            db_ref[...] = jnp.zeros_like(db_ref)

        zr = z_ref[...].astype(F32)
        zz = _gelu(zr)
        u = zz[:, :D_CH]
        xh, rstd = _ln_stats(zz[:, D_CH:])
        gn = (xh * lg_ref[...] + lb_ref[...]).astype(ACT)
        dd = d_ref[...].astype(F32)
        _gmlp_mix(w_buf, gn, sb_ref, m_buf)
        dz_ref[:, :D_CH] = (dd * m_buf[...] * _gelu_grad(zr[:, :D_CH])).astype(ACT)
        dmix = dd * u
        dmix_a = dmix.astype(ACT)
        dsb = dmix[0:BLOCK]
        for c in range(1, GMLP_CHUNKS):
            dsb = dsb + dmix[c * BLOCK:(c + 1) * BLOCK]
        dsb_ref[...] = dsb_ref[...] + dsb
        tril = _tril()
        for g in range(D_GROUPS):
            cs = slice(g * HEAD_DIM, (g + 1) * HEAD_DIM)
            dw = None
            for c in range(GMLP_CHUNKS):
                rows = slice(c * BLOCK, (c + 1) * BLOCK)
                t = _dot(dmix_a[rows, cs], gn[rows, cs], NT)
                dw = t if dw is None else dw + t
                dgn_buf[rows, cs] = _dot(wt_buf[g], dmix_a[rows, cs], NN)
            dw_ref[g] = dw_ref[g] + jnp.where(tril, dw, 0.0)
        dgn = dgn_buf[...]
        dg_ref[0:1, :] = dg_ref[0:1, :] + jnp.sum(dgn * xh, axis=0, keepdims=True)
        db_ref[0:1, :] = db_ref[0:1, :] + jnp.sum(dgn, axis=0, keepdims=True)
        dxh = dgn * lg_ref[...]
        dgate = rstd * (dxh - jnp.mean(dxh, axis=-1, keepdims=True)
                        - xh * jnp.mean(dxh * xh, axis=-1, keepdims=True))
        dz_ref[:, D_CH:] = (dgate * _gelu_grad(zr[:, D_CH:])).astype(ACT)

    vec = _fullspec((1, D_CH))
    acc = _fullspec((8, D_CH))
    wshape = (D_GROUPS, BLOCK, BLOCK)
    return pl.pallas_call(
        body, grid=(s // tm,), name=name,
        in_specs=[_rowspec(tm, 2 * D_CH), _rowspec(tm, D_CH, dcol), vec, vec, _fullspec(wshape),
                  _fullspec((BLOCK, D_CH))],
        out_specs=[_rowspec(tm, 2 * D_CH), _fullspec(wshape), _fullspec((BLOCK, D_CH)), acc, acc],
        out_shape=[jax.ShapeDtypeStruct((s, 2 * D_CH), ACT), jax.ShapeDtypeStruct(wshape, F32),
                   jax.ShapeDtypeStruct((BLOCK, D_CH), F32),
                   jax.ShapeDtypeStruct((8, D_CH), F32), jax.ShapeDtypeStruct((8, D_CH), F32)],
        scratch_shapes=[pltpu.VMEM(wshape, ACT), pltpu.VMEM(wshape, ACT),
                        pltpu.VMEM((tm, D_CH), F32), pltpu.VMEM((tm, D_CH), F32)],
        compiler_params=_cp("arbitrary"))(z, dsrc, lg, lb, sw, sbx)


def _rms_bwd(dn, x, g):
    r = lax.rsqrt(jnp.mean(x * x, axis=-1, keepdims=True) + RMS_EPS)
    u = dn * g
    dx = r * u - x * (r * r * r) * jnp.mean(x * u, axis=-1, keepdims=True)
    return dx, dn * x * r


def final_loss(h, g, target, name):
    s = h.shape[0]
    tm = _tile(s, (512,))

    def body(h_ref, g_ref, t_ref, loss_ref, dh_ref, dg_ref):
        @pl.when(pl.program_id(0) == 0)
        def _():
            loss_ref[...] = jnp.zeros_like(loss_ref)
            dg_ref[...] = jnp.zeros_like(dg_ref)

        x = h_ref[...]
        r = lax.rsqrt(jnp.mean(x * x, axis=-1, keepdims=True) + RMS_EPS)
        diff = x * r * g_ref[...] - t_ref[...]
        part = jnp.sum(jnp.sum(diff * diff, axis=-1, keepdims=True), axis=0, keepdims=True)
        loss_ref[...] = loss_ref[...] + part * (0.5 / D_MODEL)
        dx, dgt = _rms_bwd(diff * (1.0 / D_MODEL), x, g_ref[...])
        dh_ref[...] = dx
        dg_ref[0:1, :] = dg_ref[0:1, :] + jnp.sum(dgt, axis=0, keepdims=True)

    spec = _rowspec(tm, D_MODEL)
    return pl.pallas_call(
        body, grid=(s // tm,), name=name,
        in_specs=[spec, _fullspec((1, D_MODEL)), spec],
        out_specs=[_fullspec((8, LANES)), spec, _fullspec((8, D_MODEL))],
        out_shape=[jax.ShapeDtypeStruct((8, LANES), F32), jax.ShapeDtypeStruct((s, D_MODEL), F32),
                   jax.ShapeDtypeStruct((8, D_MODEL), F32)],
        compiler_params=_cp("arbitrary"))(h, g, target)


def mm_nt(dy, w, name, fold_dils=(), fold_cols=0):
    s, n = dy.shape
    k = w.shape[0]
    tm = _tile(s, (512,))
    tk = k if fold_dils else _tile(k, (1024, 512))

    def body(d_ref, w_ref, o_ref, *f_refs):
        res = _dot(d_ref[...].astype(ACT), w_ref[...], NT).astype(ACT)
        o_ref[...] = res
        cache = {}
        for f_ref, d in zip(f_refs, fold_dils):
            f_ref[...] = _fold_rows(res[:, :fold_cols], d, cache)

    out = pl.pallas_call(
        body, grid=(k // tk, s // tm), name=name,
        in_specs=[pl.BlockSpec((tm, n), lambda j, i: (i, 0)), pl.BlockSpec((tk, n), lambda j, i: (j, 0))],
        out_specs=[pl.BlockSpec((tm, tk), lambda j, i: (i, j))]
        + [pl.BlockSpec((tm // d, d * fold_cols), lambda j, i: (i, 0)) for d in fold_dils],
        out_shape=[jax.ShapeDtypeStruct((s, k), ACT)]
        + [jax.ShapeDtypeStruct((s // d, d * fold_cols), ACT) for d in fold_dils],
        compiler_params=_cp("parallel", "parallel"))(dy, w)
    return out if fold_dils else out[0]


def ffn_down_bwd(dh, wd, dact_dgate, dact_dup, name):
    s = dh.shape[0]
    f = wd.shape[0]
    tm = _tile(s, (1024, 512))
    tf = _tile(f, (1408, 512, 256, 128))

    def body(d_ref, w_ref, g_ref, u_ref, dg_ref, du_ref):
        dact = _dot(d_ref[...].astype(ACT), w_ref[...], NT)
        dg_ref[...] = (dact * g_ref[...].astype(F32)).astype(ACT)
        du_ref[...] = (dact * u_ref[...].astype(F32)).astype(ACT)

    tile = pl.BlockSpec((tm, tf), lambda j, i: (i, j))
    return pl.pallas_call(
        body, grid=(f // tf, s // tm), name=name,
        in_specs=[pl.BlockSpec((tm, D_MODEL), lambda j, i: (i, 0)),
                  pl.BlockSpec((tf, D_MODEL), lambda j, i: (j, 0)), tile, tile],
        out_specs=[tile, tile], out_shape=[jax.ShapeDtypeStruct((s, f), ACT)] * 2,
        compiler_params=_cp("parallel", "parallel"))(dh, wd, dact_dgate, dact_dup)


def mm_nt_rms(parts, h, g, dh, name):
    s = h.shape[0]
    tm = _tile(s, (512,))
    wspec = lambda w: pl.BlockSpec(w.shape, lambda i: (0, 0), pipeline_mode=pl.Buffered(1))
    np_ = len(parts)

    def body(*refs):
        d_refs = refs[:np_]
        w_refs = refs[np_:2 * np_]
        h_ref, g_ref, dh_ref, o_ref, dg_ref = refs[2 * np_:]

        @pl.when(pl.program_id(0) == 0)
        def _():
            dg_ref[...] = jnp.zeros_like(dg_ref)

        dn = None
        for d_ref, w_ref in zip(d_refs, w_refs):
            t = _dot(d_ref[...], w_ref[...], NT)
            dn = t if dn is None else dn + t
        dx, dgt = _rms_bwd(dn, h_ref[...], g_ref[...])
        o_ref[...] = dh_ref[...] + dx
        dg_ref[0:1, :] = dg_ref[0:1, :] + jnp.sum(dgt, axis=0, keepdims=True)

    spec = _rowspec(tm, D_MODEL)
    return pl.pallas_call(
        body, grid=(s // tm,), name=name,
        in_specs=[_rowspec(tm, d.shape[1]) for d, _ in parts] + [wspec(w) for _, w in parts]
        + [spec, _fullspec((1, D_MODEL)), spec],
        out_specs=[spec, _fullspec((8, D_MODEL))],
        out_shape=[jax.ShapeDtypeStruct((s, D_MODEL), F32), jax.ShapeDtypeStruct((8, D_MODEL), F32)],
        compiler_params=_cp("arbitrary"))(*[d for d, _ in parts], *[w for _, w in parts], h, g, dh)


def mm_tn(a, b, name):
    s, k = a.shape
    n = b.shape[1]
    tk = _tile(k, (1024, 1408, 512, 256, 128))
    tn = _tile(n, (1408, 1280, 1024, 896, 512, 256, 128))
    ts = _tile(s, (2048, 512) if b.dtype == ACT else (1024, 512))
    nt = s // ts

    def body(a_ref, b_ref, o_ref, acc):
        t = _dot(a_ref[...].astype(ACT), b_ref[...].astype(ACT), TN)
        step = pl.program_id(2)

        @pl.when(step == 0)
        def _():
            acc[...] = t

        @pl.when(step > 0)
        def _():
            acc[...] = acc[...] + t

        @pl.when(step == nt - 1)
        def _():
            o_ref[...] = acc[...].astype(ACT)

    return pl.pallas_call(
        body, grid=(k // tk, n // tn, nt), name=name,
        in_specs=[pl.BlockSpec((ts, tk), lambda i, j, t: (t, i)), pl.BlockSpec((ts, tn), lambda i, j, t: (t, j))],
        out_specs=pl.BlockSpec((tk, tn), lambda i, j, t: (i, j)),
        out_shape=jax.ShapeDtypeStruct((k, n), ACT),
        scratch_shapes=[pltpu.VMEM((tk, tn), F32)],
        compiler_params=_cp("parallel", "parallel", "arbitrary"))(a, b)


def _adamw_math(w, g, m, v):
    m = ADAM_B1 * m + (1.0 - ADAM_B1) * g
    v = ADAM_B2 * v + (1.0 - ADAM_B2) * (g * g)
    m_hat = m / (1.0 - ADAM_B1 ** ADAM_STEP)
    v_hat = v / (1.0 - ADAM_B2 ** ADAM_STEP)
    delta = -ADAM_LR * (m_hat / (jnp.sqrt(v_hat) + ADAM_EPS) + ADAM_WD * w)
    return delta, m, v


def sum_adamw(parts, w, m, v, layer, others, name):
    nl, r, c = w.shape
    tr = _tile(r, (256, 128, 64, 32, 16, 8))

    def body(p_ref, w_ref, m_ref, v_ref, *rest):
        g_ref, d_ref, mo_ref, vo_ref = rest[-4:]
        g = p_ref[0].astype(F32)
        for i in range(1, N_DEV):
            g = g + p_ref[i].astype(F32)
        d, mm, vv = _adamw_math(w_ref[...], g, m_ref[...], v_ref[...])
        g_ref[...] = g
        d_ref[...] = d
        mo_ref[...] = mm
        vo_ref[...] = vv

    spec = pl.BlockSpec((None, tr, c), lambda i: (layer, i, 0))
    in_specs = [pl.BlockSpec((N_DEV, tr, c), lambda i: (0, i, 0))] + [spec] * 3
    args = [parts, w, m, v]
    aliases = {}
    if others is not None:
        in_specs += [pl.BlockSpec(memory_space=pl.ANY)] * 4
        args += list(others)
        aliases = {4 + j: j for j in range(4)}
    return pl.pallas_call(
        body, grid=(r // tr,), name=name, in_specs=in_specs, out_specs=[spec] * 4,
        out_shape=[jax.ShapeDtypeStruct((nl, r, c), F32)] * 4, input_output_aliases=aliases,
        compiler_params=_cp("parallel"))(*args)


def cast_layers(items, name):
    n = len(items)

    def body(*refs):
        for i in range(n):
            refs[n + i][...] = refs[i][...].astype(ACT)

    return pl.pallas_call(
        body, grid=(1,), name=name,
        in_specs=[pl.BlockSpec((None,) + w.shape[1:], lambda i, l=l: (l, 0, 0)) for w, l in items],
        out_specs=[_fullspec(w.shape[1:]) for w, _ in items],
        out_shape=[jax.ShapeDtypeStruct(w.shape[1:], ACT) for w, _ in items],
        compiler_params=_cp("arbitrary"))(*[w for w, _ in items])


def adamw_small(ws, gs, ms, vs, name):
    k = len(ws)

    def body(*refs):
        for i in range(k):
            w_ref, g_ref, m_ref, v_ref = (refs[j * k + i] for j in range(4))
            d, mm, vv = _adamw_math(w_ref[...], g_ref[...], m_ref[...], v_ref[...])
            refs[4 * k + i][...] = d
            refs[5 * k + i][...] = mm
            refs[6 * k + i][...] = vv

    shapes = [jax.ShapeDtypeStruct(w.shape, F32) for w in ws]
    specs = [_fullspec(w.shape) for w in ws]
    out = pl.pallas_call(
        body, grid=(1,), name=name, in_specs=specs * 4, out_specs=specs * 3, out_shape=shapes * 3,
        compiler_params=_cp("arbitrary"))(*ws, *gs, *ms, *vs)
    return out[:k], out[k:2 * k], out[2 * k:]


def sum_slots(x, out_dtype, name):
    g, r, c = x.shape
    tr = r if x.size * x.dtype.itemsize <= SMALL_BLOCK_BYTES else _tile(r, (256, 128, 64, 32, 16, 8))

    def body(x_ref, o_ref):
        acc = x_ref[0].astype(F32)
        for i in range(1, g):
            acc = acc + x_ref[i].astype(F32)
        o_ref[...] = acc.astype(o_ref.dtype)

    return pl.pallas_call(
        body, grid=(r // tr,), name=name,
        in_specs=[pl.BlockSpec((g, tr, c), lambda i: (0, i, 0))], out_specs=_rowspec(tr, c),
        out_shape=jax.ShapeDtypeStruct((r, c), out_dtype),
        compiler_params=_cp("parallel"))(x)


HBM_SPEC = pl.BlockSpec(memory_space=pltpu.HBM)
SEM_SPEC = pl.BlockSpec(memory_space=pltpu.SEMAPHORE)
DATAFLOW = pltpu.SideEffectType.DATAFLOW_SIDE_EFFECTING


def _my_rank():
    return 4 * lax.axis_index("x") + 2 * lax.axis_index("y") + lax.axis_index("c")


def _exchange_copies(x_refs, land_refs, send, recv, a2a):
    pos = [lax.axis_index(a) for a in AXES]
    me = _my_rank()
    copies = []
    for x_ref, land_ref, s_ref, r_ref in zip(x_refs, land_refs, send, recv):
        for k in range(N_DEV - 1):
            bits = ((k + 1) >> 2 & 1, (k + 1) >> 1 & 1, (k + 1) & 1)
            peer = tuple(1 - p if b else p for p, b in zip(pos, bits))
            prank = 4 * peer[0] + 2 * peer[1] + peer[2]
            copies.append(pltpu.make_async_remote_copy(
                src_ref=x_ref.at[prank] if a2a else x_ref, dst_ref=land_ref.at[me],
                send_sem=s_ref.at[k], recv_sem=r_ref.at[k], device_id=peer, device_id_type=MESH))
    return copies


def exchange_start(xs, a2a, name, after=None):
    n = len(xs)
    me = _my_rank()
    lands = []
    for x in xs:
        own = lax.dynamic_index_in_dim(x, me, 0, keepdims=True) if a2a else x[None]
        shape = x.shape if a2a else (N_DEV,) + x.shape
        lands.append(lax.dynamic_update_slice(lax.empty(shape, x.dtype), own, (me,) + (0,) * (len(shape) - 1)))

    def body(*refs):
        x_refs, land_refs = refs[:n], refs[n:2 * n]
        outs = refs[len(refs) - 4 * n - 1:]
        for cp in _exchange_copies(x_refs, land_refs, outs[:n], outs[n:2 * n], a2a):
            cp.start()
        token = outs[4 * n]
        token[...] = jnp.zeros_like(token)

    sems = [pltpu.SemaphoreType.DMA((N_DEV - 1,))] * n
    out = pl.pallas_call(
        body, name=name,
        out_shape=tuple(sems + sems + [pltpu.HBM(x.shape, x.dtype) for x in xs]
                        + [pltpu.HBM(l.shape, l.dtype) for l in lands] + [jax.ShapeDtypeStruct((8, LANES), F32)]),
        in_specs=[HBM_SPEC] * (2 * n) + ([] if after is None else [pl.BlockSpec(memory_space=pl.ANY)]),
        out_specs=tuple([SEM_SPEC] * (2 * n) + [HBM_SPEC] * (2 * n) + [pl.BlockSpec(memory_space=pltpu.VMEM)]),
        input_output_aliases={i: 2 * n + i for i in range(2 * n)},
        compiler_params=pltpu.CompilerParams(has_side_effects=DATAFLOW),
    )(*[pltpu.with_memory_space_constraint(a, pltpu.HBM) for a in list(xs) + lands], *([] if after is None else [after]))
    return (out[:n], out[n:2 * n], out[2 * n:3 * n], out[3 * n:4 * n]), out[4 * n]


def exchange_wait(handles, after, a2a, name):
    send, recv, x_thru, land_thru = handles
    n = len(x_thru)

    def body(*refs):
        x_refs, land_refs = refs[:n], refs[n:2 * n]
        s_refs, r_refs = refs[2 * n:3 * n], refs[3 * n:4 * n]
        for cp in _exchange_copies(x_refs, land_refs, s_refs, r_refs, a2a):
            cp.wait_send()
            cp.wait_recv()

    out = pl.pallas_call(
        body, name=name,
        out_shape=tuple([pltpu.HBM(a.shape, a.dtype) for a in list(x_thru) + list(land_thru)]),
        in_specs=[HBM_SPEC] * (2 * n) + [SEM_SPEC] * (2 * n) + [pl.BlockSpec(memory_space=pl.ANY)],
        out_specs=tuple([HBM_SPEC] * (2 * n)),
        input_output_aliases={i: i for i in range(2 * n)},
        compiler_params=pltpu.CompilerParams(has_side_effects=DATAFLOW),
    )(*x_thru, *land_thru, *send, *recv, after)
    return out[n:2 * n]


def _local_step(x, target, weight, emit, P):
    s = x.shape[0]
    tabs = _rope_tables(s)
    sinkb = jnp.broadcast_to(P["ev_sinks"].reshape(N_HEADS, 1), (N_HEADS, LANES))
    sbx = jnp.repeat(P["od_spatial_b"].reshape(D_GROUPS, BLOCK).T, HEAD_DIM, axis=1)
    sw = P["od_spatial_w"].reshape(D_GROUPS, BLOCK, BLOCK)
    fg = P["ffn_norm_g"]
    latest = [None]

    def out(name, layer, grad):
        tok = emit(name, layer, grad)
        if tok is not None:
            latest[0] = tok

    def dep(a):
        return a if latest[0] is None else a + latest[0][0:1, 0:1]

    n0 = rmsnorm(x, P["ev_norm_g"], "rms_in")
    qk_e, v_e, ga, gb = inproj(n0, weight("ev_w_in", 0, n0), tabs, ATT_W + 128,
                               (ATT_W + 128, 128, CONV_CH, CONV_CH), (), "ev_inproj")
    a_e, lse_e = attn_fwd(qk_e, v_e, d=1, hkv=A_KV_HEADS, max_dist=BLOCK - 1, sink=sinkb, out_dtype=ACT,
                          name="ev_attn")
    c_act, c1 = conv_fwd(ga, gb, P["ev_conv_w"], P["ev_conv_b"], P["ev_conv_ln_g"], P["ev_conv_ln_b"], "ev_conv")
    h1, n1 = mm_res([a_e, c_act], weight("ev_w_out", 0, c_act), x, fg[0:1], "ev_outproj")
    gate0, up0, act0 = ffn_up(n1, weight("ffn_w_gate", 0, n1), weight("ffn_w_up", 0, n1), "ffn0_up")
    h2, n2 = mm_res([act0], weight("ffn_w_down", 0, act0), h1, P["od_norm_g"], "ffn0_down")
    fold_dils = tuple(dil for _, dil in DILATED if dil > 1)
    qk_o, v_o, z, *pre = inproj(n2, weight("od_w_in", 0, n2), tabs, 2 * ATT_W, (2 * ATT_W, ATT_W, 2 * D_CH),
                                fold_dils, "od_inproj")
    folded = {dil: (pre[2 * i], pre[2 * i + 1]) for i, dil in enumerate(fold_dils)}
    outs, lses = [], []
    for window, dil in DILATED:
        assert window // dil == BLOCK
        o_r, l_r = attn_fwd(qk_o, v_o, d=dil, hkv=N_HEADS, max_dist=BLOCK, sink=None, out_dtype=ACT,
                            name="od_attn_d%d" % dil, folded=folded.get(dil), keep_folded=True)
        outs.append(o_r)
        lses.append(l_r)
    dils = [dil for _, dil in DILATED]
    c_out, lse_o, *c_folded = combine_fwd(outs, lses, dils, "od_combine")
    d_out = gmlp_fwd(z, P["od_sgu_ln_g"], P["od_sgu_ln_b"], sw, sbx, "od_gmlp")
    h3, n3 = mm_res([c_out, d_out], weight("od_w_out", 0, d_out), h2, fg[1:2], "od_outproj")
    gate1, up1, act1 = ffn_up(n3, weight("ffn_w_gate", 1, n3), weight("ffn_w_up", 1, n3), "ffn1_up")
    h4, _ = mm_res([act1], weight("ffn_w_down", 1, act1), h3, None, "ffn1_down")
    loss_part, dh4, dg_final = final_loss(h4, P["final_norm_g"], target, "loss_head")

    def ffn_bwd(layer, dh_out, h_in, n_in, gate, up, act):
        wg, wu, wd = (weight(n, layer, dh_out) for n in ("ffn_w_gate", "ffn_w_up", "ffn_w_down"))
        tag = "ffn%d" % layer
        dgate, dup = ffn_down_bwd(dh_out, wd, gate, up, tag + "_down_bwd")
        g_wd = mm_tn(act, dh_out, tag + "_dwd")
        dh_in, dgn = mm_nt_rms([(dgate, wg), (dup, wu)], h_in, dep(fg[layer:layer + 1]), dh_out, tag + "_up_bwd")
        out("ffn_w_down", layer, g_wd)
        out("ffn_w_gate", layer, mm_tn(n_in, dgate, tag + "_dwg"))
        out("ffn_w_up", layer, mm_tn(n_in, dup, tag + "_dwu"))
        return dh_in, dgn[0:1]

    dh3, dgn_f1 = ffn_bwd(1, dh4, h3, n3, gate1, up1, act1)

    dcd, *dc_folded = mm_nt(dh3, weight("od_w_out", 0, dh3), "od_outproj_bwd", fold_dils, ATT_W)
    do_o = {dil: (dc_folded[i], c_folded[i]) for i, dil in enumerate(fold_dils)}
    dz, g_sw, g_sbx, g_slg, g_slb = gmlp_bwd(z, dcd, 1, dep(P["od_sgu_ln_g"]), P["od_sgu_ln_b"], sw, sbx,
                                             "od_gmlp_bwd")
    g_sb = jnp.sum(g_sbx.reshape(BLOCK, D_GROUPS, HEAD_DIM), axis=-1).T
    out("od_spatial", 0, jnp.concatenate([g_sw.reshape(D_GROUPS * BLOCK, BLOCK), g_sb], axis=0))
    out("od_w_out", 0, jnp.concatenate([mm_tn(c_out, dh3, "od_dwo_c"), mm_tn(d_out, dh3, "od_dwo_d")], axis=0))
    dqkv = [attn_bwd(qk_o, v_o, dcd if dil == 1 else dcd[:, :ATT_W], 0, c_out, lse_o, d=dil, hkv=N_HEADS,
                     max_dist=BLOCK, sink=None, out_dtype=ACT, name="od_attn_bwd_d%d" % dil,
                     folded=folded.get(dil), folded_do_o=do_o.get(dil), keep_folded=True)
            for window, dil in DILATED]
    dproj_o = assemble([([(b[j], dil) for b, dil in zip(dqkv, dils)], j < 2) for j in range(3)] + [([dz], False)],
                       tabs, "od_dproj")
    dh2, dgn_od = mm_nt_rms([(dproj_o, weight("od_w_in", 0, dproj_o))], h2, dep(P["od_norm_g"]), dh3,
                            "od_inproj_bwd")
    out("od_w_in", 0, mm_tn(n2, dproj_o, "od_dwi"))

    dh1, dgn_f0 = ffn_bwd(0, dh2, h1, n1, gate0, up0, act0)

    dac = mm_nt(dh1, weight("ev_w_out", 0, dh1), "ev_outproj_bwd")
    dc1, g_clg, g_clb = conv_bwd_ln(c1, dac, 1, dep(P["ev_conv_ln_g"]), P["ev_conv_ln_b"], "ev_conv_bwd_ln")
    out("ev_w_out", 0, jnp.concatenate([mm_tn(a_e, dh1, "ev_dwo_a"), mm_tn(c_act, dh1, "ev_dwo_c")], axis=0))
    dga, dgb, g_cw, g_cb = conv_bwd_conv(dc1, ga, gb, P["ev_conv_w"], "ev_conv_bwd")
    dq, dk, dv, dsink = attn_bwd(qk_e, v_e, dac, 0, a_e, lse_e, d=1, hkv=A_KV_HEADS, max_dist=BLOCK - 1,
                                 sink=sinkb, out_dtype=F32, name="ev_attn_bwd")
    dproj_e = assemble([([dq], True), ([dk], True), ([dv], False), ([dga], False), ([dgb], False)], tabs,
                       "ev_dproj")
    out("ev_w_in", 0, mm_tn(n0, dproj_e, "ev_dwi"))
    dx, dgn_ev = mm_nt_rms([(dproj_e, weight("ev_w_in", 0, dproj_e))], x, dep(P["ev_norm_g"]), dh1,
                           "ev_inproj_bwd")

    small = {
        "ev_norm_g": dgn_ev[0:1],
        "ev_sinks": dsink[:, 0:1].reshape(1, N_HEADS),
        "ev_conv_w": jnp.sum(g_cw.reshape(CONV_HALO, SUBLANES, CONV_CH), axis=1)[:CONV_WIDTH],
        "ev_conv_b": jnp.sum(g_cb, axis=0, keepdims=True),
        "ev_conv_ln_g": g_clg[0:1],
        "ev_conv_ln_b": g_clb[0:1],
        "od_norm_g": dgn_od[0:1],
        "od_sgu_ln_g": g_slg[0:1],
        "od_sgu_ln_b": g_slb[0:1],
        "od_spatial_w": g_sw.reshape(D_GROUPS * BLOCK, BLOCK),
        "od_spatial_b": g_sb,
        "ffn_norm_g": jnp.concatenate([dgn_f0, dgn_f1], axis=0),
        "final_norm_g": dg_final[0:1],
    }
    return loss_part, dx, small


BIG = ("ev_w_in", "ev_w_out", "od_w_in", "od_w_out", "ffn_w_gate", "ffn_w_up", "ffn_w_down")
COL_SHARDED = ("ev_w_in", "od_w_in", "ffn_w_gate", "ffn_w_up")
GATHER_GROUPS = (
    (("ev_w_in", 0),),
    (("ev_w_out", 0),),
    (("ffn_w_gate", 0), ("ffn_w_up", 0)),
    (("ffn_w_down", 0),),
    (("od_w_in", 0),),
    (("od_w_out", 0),),
    (("ffn_w_gate", 1), ("ffn_w_up", 1)),
    (("ffn_w_down", 1),),
)
GATHER_EARLY = 4
GATHER_LATE_AT = 2
REDUCE_GROUPS = (
    (("ffn_w_down", 1), ("ffn_w_gate", 1), ("ffn_w_up", 1)),
    (("od_w_out", 0),),
    (("od_w_in", 0),),
    (("ffn_w_down", 0), ("ffn_w_gate", 0), ("ffn_w_up", 0)),
    (("ev_w_out", 0),),
    (("ev_w_in", 0),),
)


def _unshard(name, g):
    if name in COL_SHARDED:
        return jnp.moveaxis(g, 0, 1).reshape(g.shape[1], N_DEV * g.shape[2])
    return g.reshape(N_DEV * g.shape[1], g.shape[2])


def _shard_slots(name, full):
    r, c = full.shape
    if name in COL_SHARDED:
        return jnp.moveaxis(full.reshape(r, N_DEV, c // N_DEV), 1, 0)
    return full.reshape(N_DEV, r // N_DEV, c)


def kernel(x, ev_norm_g, ev_w_in, ev_sinks, ev_conv_w, ev_conv_b, ev_conv_ln_g, ev_conv_ln_b, ev_w_out, od_norm_g, od_w_in, od_sgu_ln_g, od_sgu_ln_b, od_spatial_w, od_spatial_b, od_w_out, ffn_norm_g, ffn_w_gate, ffn_w_up, ffn_w_down, final_norm_g, loss_target, m_ev_norm_g, m_ev_w_in, m_ev_sinks, m_ev_conv_w, m_ev_conv_b, m_ev_conv_ln_g, m_ev_conv_ln_b, m_ev_w_out, m_od_norm_g, m_od_w_in, m_od_sgu_ln_g, m_od_sgu_ln_b, m_od_spatial_w, m_od_spatial_b, m_od_w_out, m_ffn_norm_g, m_ffn_w_gate, m_ffn_w_up, m_ffn_w_down, m_final_norm_g, v_ev_norm_g, v_ev_w_in, v_ev_sinks, v_ev_conv_w, v_ev_conv_b, v_ev_conv_ln_g, v_ev_conv_ln_b, v_ev_w_out, v_od_norm_g, v_od_w_in, v_od_sgu_ln_g, v_od_sgu_ln_b, v_od_spatial_w, v_od_spatial_b, v_od_w_out, v_ffn_norm_g, v_ffn_w_gate, v_ffn_w_up, v_ffn_w_down, v_final_norm_g):
    names = ["ev_norm_g", "ev_w_in", "ev_sinks", "ev_conv_w", "ev_conv_b", "ev_conv_ln_g", "ev_conv_ln_b", "ev_w_out",
             "od_norm_g", "od_w_in", "od_sgu_ln_g", "od_sgu_ln_b", "od_spatial_w", "od_spatial_b", "od_w_out",
             "ffn_norm_g", "ffn_w_gate", "ffn_w_up", "ffn_w_down", "final_norm_g"]
    wts = dict(zip(names, [ev_norm_g, ev_w_in, ev_sinks, ev_conv_w, ev_conv_b, ev_conv_ln_g, ev_conv_ln_b, ev_w_out,
                           od_norm_g, od_w_in, od_sgu_ln_g, od_sgu_ln_b, od_spatial_w, od_spatial_b, od_w_out,
                           ffn_norm_g, ffn_w_gate, ffn_w_up, ffn_w_down, final_norm_g]))
    mom = dict(zip(names, [m_ev_norm_g, m_ev_w_in, m_ev_sinks, m_ev_conv_w, m_ev_conv_b, m_ev_conv_ln_g, m_ev_conv_ln_b,
                           m_ev_w_out, m_od_norm_g, m_od_w_in, m_od_sgu_ln_g, m_od_sgu_ln_b, m_od_spatial_w,
                           m_od_spatial_b, m_od_w_out, m_ffn_norm_g, m_ffn_w_gate, m_ffn_w_up, m_ffn_w_down,
                           m_final_norm_g]))
    vel = dict(zip(names, [v_ev_norm_g, v_ev_w_in, v_ev_sinks, v_ev_conv_w, v_ev_conv_b, v_ev_conv_ln_g, v_ev_conv_ln_b,
                           v_ev_w_out, v_od_norm_g, v_od_w_in, v_od_sgu_ln_g, v_od_sgu_ln_b, v_od_spatial_w,
                           v_od_spatial_b, v_od_w_out, v_ffn_norm_g, v_ffn_w_gate, v_ffn_w_up, v_ffn_w_down,
                           v_final_norm_g]))
    me = _my_rank()

    sp = jnp.zeros((40, LANES), F32)
    sp = sp.at[0:CONV_WIDTH, 0:64].set(ev_conv_w[0])
    sp = sp.at[32, :].set(od_norm_g[0])
    sp = sp.at[33, 0:64].set(od_sgu_ln_g[0])
    sp = sp.at[34, 0:64].set(od_sgu_ln_b[0])

    early = [k for grp in GATHER_GROUPS[:GATHER_EARLY] for k in grp]
    late = [k for grp in GATHER_GROUPS[GATHER_EARLY:] for k in grp]
    early_act = dict(zip(early, cast_layers([(wts[n], l) for n, l in early], "cast_early")))
    late_act = dict(zip(late, cast_layers([(wts[n], l) for n, l in late], "cast_late")))
    ag_early, ag_token = exchange_start([sp] + [early_act[k] for k in early], False, "ag_start")
    ag_late = []
    full_w = {}
    P = {
        "ev_norm_g": ev_norm_g + ag_token[0:1, 0:1], "ev_sinks": ev_sinks, "ev_conv_b": ev_conv_b,
        "ev_conv_ln_g": ev_conv_ln_g, "ev_conv_ln_b": ev_conv_ln_b,
        "od_spatial_w": od_spatial_w, "od_spatial_b": od_spatial_b, "ffn_norm_g": ffn_norm_g,
        "final_norm_g": final_norm_g.reshape(1, D_MODEL),
    }

    def weight(name, layer, after):
        if (name, layer) not in full_w:
            gi = [i for i, grp in enumerate(GATHER_GROUPS) if (name, layer) in grp][0]
            if gi < GATHER_EARLY:
                idx = [1 + early.index(k) for k in GATHER_GROUPS[gi]]
                handles = ag_early
            else:
                idx = [late.index(k) for k in GATHER_GROUPS[gi]]
                handles = ag_late[0]
            if gi == 0:
                idx = [0] + idx
            lands = exchange_wait(tuple([h[i] for i in idx] for h in handles), after, False, "ag_wait%d" % gi)
            if gi == 0:
                spg, lands = lands[0], lands[1:]
                P["ev_conv_w"] = jnp.moveaxis(spg[:, 0:CONV_WIDTH, 0:64], 0, 1).reshape(CONV_WIDTH, CONV_CH)
                P["od_norm_g"] = spg[:, 32, :].reshape(1, D_MODEL)
                P["od_sgu_ln_g"] = spg[:, 33, 0:64].reshape(1, D_CH)
                P["od_sgu_ln_b"] = spg[:, 34, 0:64].reshape(1, D_CH)
            if gi == GATHER_LATE_AT:
                ag_late.append(exchange_start([late_act[k] for k in late], False, "ag_start_late", after=lands[0])[0])
            for k, land in zip(GATHER_GROUPS[gi], lands):
                full_w[k] = _unshard(k[0], land)
        return full_w[(name, layer)]

    pending, rs_started, spatial = {}, [], []

    def emit(name, layer, grad):
        if name == "od_spatial":
            handles, token = exchange_start([grad], False, "ar_start_b")
            spatial.append(handles)
            return token
        pending[(name, layer)] = grad
        for gi, grp in enumerate(REDUCE_GROUPS):
            if (name, layer) in grp and all(k in pending for k in grp):
                handles, token = exchange_start([_shard_slots(k[0], pending[k]).astype(ACT) for k in grp], True,
                                                "rs_start%d" % gi)
                rs_started.append((gi, handles))
                return token
        return None

    loss_part, dx, small = _local_step(x[0], loss_target[0], weight, emit, P)
    loss = lax.psum(loss_part[0, 0], AXES)

    wide = ["ev_norm_g", "ev_sinks", "ev_conv_w", "ev_conv_b", "ev_conv_ln_g", "ev_conv_ln_b", "od_norm_g",
            "od_sgu_ln_g", "od_sgu_ln_b", "ffn_norm_g", "final_norm_g"]
    blk_a = jnp.concatenate(
        [jnp.pad(small[n], ((0, 0), (0, D_MODEL - small[n].shape[1]))) for n in wide], axis=0)
    blk_a = jnp.pad(blk_a, ((0, 48 - blk_a.shape[0]), (0, 0)))
    ar_a, ar_token = exchange_start([blk_a], False, "ar_start_a")

    results = {}
    for gi, handles in rs_started:
        lands = exchange_wait(handles, ar_token, True, "rs_wait%d" % gi)
        for (n, l), land in zip(REDUCE_GROUPS[gi], lands):
            results[n] = sum_adamw(land, wts[n], mom[n], vel[n], l, results.get(n), "adamw_%s%d" % (n, l))
    out_g, out_d, out_m, out_v = {}, {}, {}, {}
    for n in BIG:
        out_g[n], out_d[n], out_m[n], out_v[n] = results[n]

    last = results[REDUCE_GROUPS[-1][-1][0]][0]
    sum_b = sum_slots(exchange_wait(spatial[0], last, False, "ar_wait_b")[0], F32, "ar_sum_b")
    sum_a = sum_slots(exchange_wait(ar_a, last, False, "ar_wait_a")[0], F32, "ar_sum_a")
    full = {}
    off = 0
    for n in wide:
        r_, c_ = small[n].shape
        full[n] = sum_a[off:off + r_, 0:c_]
        off += r_
    full["od_spatial_w"] = sum_b[0:D_GROUPS * BLOCK]
    full["od_spatial_b"] = sum_b[D_GROUPS * BLOCK:D_GROUPS * BLOCK + D_GROUPS]
    full["ev_conv_w"] = lax.dynamic_slice_in_dim(full["ev_conv_w"], me * 64, 64, axis=1)
    full["od_norm_g"] = lax.dynamic_slice_in_dim(full["od_norm_g"], me * 128, 128, axis=1)
    full["od_sgu_ln_g"] = lax.dynamic_slice_in_dim(full["od_sgu_ln_g"], me * 64, 64, axis=1)
    full["od_sgu_ln_b"] = lax.dynamic_slice_in_dim(full["od_sgu_ln_b"], me * 64, 64, axis=1)

    small_names = [n for n in names if n not in BIG]
    view = {n: ((-1, wts[n].shape[-1]) if wts[n].ndim > 1 else (1, -1)) for n in small_names}
    ds, ms, vs = adamw_small([wts[n].reshape(view[n]) for n in small_names],
                             [full[n].reshape(view[n]) for n in small_names],
                             [mom[n].reshape(view[n]) for n in small_names],
                             [vel[n].reshape(view[n]) for n in small_names], "adamw_small")
    for i, n in enumerate(small_names):
        shp = wts[n].shape
        out_g[n], out_d[n], out_m[n], out_v[n] = (full[n].reshape(shp), ds[i].reshape(shp), ms[i].reshape(shp),
                                                  vs[i].reshape(shp))

    return (loss, dx[None], *[out_g[n] for n in names], *[out_d[n] for n in names],
            *[out_m[n] for n in names], *[out_v[n] for n in names])
```

```python
import functools
import math

import jax
import jax.numpy as jnp
from jax import lax
from jax.experimental import pallas as pl
from jax.experimental.pallas import tpu as pltpu

F32 = jnp.float32
ACT = jnp.bfloat16

D_MODEL = 1024
HEAD_DIM = 64
N_HEADS = 8
ATT_W = N_HEADS * HEAD_DIM
A_KV_HEADS = 2
CONV_CH = 512
CONV_WIDTH = 31
CONV_HALO = 32
D_CH = 512
D_GROUPS = 8
BLOCK = 128
D_FF = 2816
ROT_DIM = 16
ROPE_THETA = 500000.0
RMS_EPS = 1e-6
LN_EPS = 1e-5
DILATED = ((128, 1), (512, 4), (2048, 16))
NEG = -1e30
LANES = 128

ADAM_LR = 0.001
ADAM_B1 = 0.9
ADAM_B2 = 0.999
ADAM_EPS = 1e-08
ADAM_WD = 0.01
ADAM_STEP = 10

V7X_VMEM_LIMIT = 56 * 1024 * 1024
SMALL_BLOCK_BYTES = 6 * 1024 * 1024
N_DEV = 8

NN = (((1,), (0,)), ((), ()))
NT = (((1,), (1,)), ((), ()))
TN = (((0,), (0,)), ((), ()))
MESH = pl.DeviceIdType.MESH
AXES = ("x", "y", "c")


def _dot(a, b, dims):
    return lax.dot_general(a, b, dims, preferred_element_type=F32)


def _cp(*sem):
    return pltpu.CompilerParams(dimension_semantics=sem if sem else None,
                                vmem_limit_bytes=V7X_VMEM_LIMIT)


def _tile(n, prefs):
    for p in prefs:
        if n % p == 0:
            return p
    return n


def _sigmoid(x):
    return 1.0 / (1.0 + jnp.exp(-x))


def _rowspec(tm, w, col=0):
    return pl.BlockSpec((tm, w), lambda i, col=col: (i, col))


def _fullspec(shape):
    nd = len(shape)
    return pl.BlockSpec(shape, lambda *a, nd=nd: (0,) * nd)


def _rope_tables(seq):
    half = ROT_DIM // 2
    inv_freq = ROPE_THETA ** (-jnp.arange(half, dtype=F32) * (2.0 / ROT_DIM))
    ang = jnp.arange(seq, dtype=jnp.int32).astype(F32)[:, None] * inv_freq[None, :]
    cos, sin = jnp.cos(ang), jnp.sin(ang)
    lane = jnp.arange(LANES)
    jm = lane % HEAD_DIM
    idx = jm % half
    c = jnp.where(jm[None, :] < ROT_DIM, cos[:, idx], 1.0)
    sa = jnp.where(jm[None, :] < half, -sin[:, idx], 0.0)
    sb = jnp.where((jm[None, :] >= half) & (jm[None, :] < ROT_DIM), sin[:, idx], 0.0)
    return c.astype(F32), sa.astype(F32), sb.astype(F32)


def _rope(x, c, sa, sb):
    return x * c + pltpu.roll(x, LANES - 8, 1) * sa + pltpu.roll(x, 8, 1) * sb


def _rope_t(d, c, sa, sb):
    return d * c + pltpu.roll(d * sa, 8, 1) + pltpu.roll(d * sb, LANES - 8, 1)


def rmsnorm(h, g, name):
    s = h.shape[0]
    tm = _tile(s, (1024, 512))

    def body(h_ref, g_ref, o_ref):
        x = h_ref[...]
        r = lax.rsqrt(jnp.mean(x * x, axis=-1, keepdims=True) + RMS_EPS)
        o_ref[...] = (x * r * g_ref[...]).astype(o_ref.dtype)

    return pl.pallas_call(
        body, grid=(s // tm,), name=name,
        in_specs=[_rowspec(tm, D_MODEL), _fullspec((1, D_MODEL))],
        out_specs=_rowspec(tm, D_MODEL),
        out_shape=jax.ShapeDtypeStruct((s, D_MODEL), ACT),
        compiler_params=_cp("parallel"))(h, g)


def _fold_perm(tm, d, inverse, cache):
    key = (tm, d, inverse)
    if key not in cache:
        m = tm // d
        a = lax.broadcasted_iota(jnp.int32, (tm, tm), 1 if inverse else 0)
        b = lax.broadcasted_iota(jnp.int32, (tm, tm), 0 if inverse else 1)
        src = (a & (m - 1)) * d + (a >> (m.bit_length() - 1))
        cache[key] = (b == src).astype(ACT)
    return cache[key]


FOLD_ROWS = 256


def _fold_rows(x, d, cache):
    m = FOLD_ROWS // d
    out = []
    for r0 in range(0, x.shape[0], FOLD_ROWS):
        p = _dot(_fold_perm(FOLD_ROWS, d, False, cache), x[r0:r0 + FOLD_ROWS], NN).astype(ACT)
        out.append(jnp.concatenate([p[r * m:(r + 1) * m] for r in range(d)], axis=1))
    return out[0] if len(out) == 1 else jnp.concatenate(out, axis=0)


def _unfold_rows(blk, d, cache):
    w = blk.shape[1] // d
    m = FOLD_ROWS // d
    out = []
    for r0 in range(0, blk.shape[0], m):
        stacked = jnp.concatenate([blk[r0:r0 + m, r * w:(r + 1) * w] for r in range(d)], axis=0)
        out.append(_dot(_fold_perm(FOLD_ROWS, d, True, cache), stacked, NN))
    return out[0] if len(out) == 1 else jnp.concatenate(out, axis=0)


def _unfold_rows_f32(blk, d, cache):
    hi = blk.astype(ACT)
    r1 = blk - hi.astype(F32)
    mid = r1.astype(ACT)
    lo = (r1 - mid.astype(F32)).astype(ACT)
    return _unfold_rows(hi, d, cache) + _unfold_rows(mid, d, cache) + _unfold_rows(lo, d, cache)


def inproj(n, w, tabs, nqk, splits, fold_dils, name):
    s = n.shape[0]
    ntot = w.shape[1]
    assert sum(splits) == ntot and splits[0] == nqk
    tm = _tile(s, (512,))
    ns = len(splits)

    def body(n_ref, w_ref, c_ref, sa_ref, sb_ref, *outs):
        res = _dot(n_ref[...], w_ref[...], NN)
        c, sa, sb = c_ref[...], sa_ref[...], sb_ref[...]
        for g in range(nqk // LANES):
            x = res[:, g * LANES:(g + 1) * LANES]
            outs[0][:, g * LANES:(g + 1) * LANES] = _rope(x, c, sa, sb).astype(ACT)
        off = nqk
        for o_ref, wd in zip(outs[1:ns], splits[1:]):
            o_ref[...] = res[:, off:off + wd].astype(ACT)
            off += wd
        cache = {}
        for i, d in enumerate(fold_dils):
            outs[ns + 2 * i][...] = _fold_rows(outs[0][...], d, cache)
            outs[ns + 2 * i + 1][...] = _fold_rows(outs[1][...], d, cache)

    out_specs = [_rowspec(tm, wd) for wd in splits]
    out_shape = [jax.ShapeDtypeStruct((s, wd), ACT) for wd in splits]
    for d in fold_dils:
        for wd in splits[:2]:
            out_specs.append(_rowspec(tm // d, d * wd))
            out_shape.append(jax.ShapeDtypeStruct((s // d, d * wd), ACT))
    return pl.pallas_call(
        body, grid=(s // tm,), name=name,
        in_specs=[_rowspec(tm, D_MODEL), _fullspec((D_MODEL, ntot))] + [_rowspec(tm, LANES)] * 3,
        out_specs=out_specs, out_shape=out_shape,
        compiler_params=_cp("parallel"))(n, w, *tabs)


def ffn_up(n, wg, wu, name):
    s = n.shape[0]
    f = wg.shape[1]
    tm = _tile(s, (1024, 512))
    tf = _tile(f, (1408, 512, 256, 128))

    def body(n_ref, wg_ref, wu_ref, g_ref, u_ref, a_ref):
        a = n_ref[...]
        g = _dot(a, wg_ref[...], NN)
        u = _dot(a, wu_ref[...], NN)
        sg = _sigmoid(g)
        silu = g * sg
        g_ref[...] = (u * (sg * (1.0 + g * (1.0 - sg)))).astype(ACT)
        u_ref[...] = silu.astype(ACT)
        a_ref[...] = (silu * u).astype(ACT)

    wspec = pl.BlockSpec((D_MODEL, tf), lambda j, i: (0, j))
    ospec = pl.BlockSpec((tm, tf), lambda j, i: (i, j))
    return pl.pallas_call(
        body, grid=(f // tf, s // tm), name=name,
        in_specs=[pl.BlockSpec((tm, D_MODEL), lambda j, i: (i, 0)), wspec, wspec],
        out_specs=[ospec] * 3,
        out_shape=[jax.ShapeDtypeStruct((s, f), ACT)] * 3,
        compiler_params=_cp("parallel", "parallel"))(n, wg, wu)


def mm_res(parts, w, h, gnext, name):
    s = h.shape[0]
    tm = _tile(s, (1024, 512))
    widths = [p.shape[1] for p in parts]
    assert sum(widths) == w.shape[0]
    np_ = len(parts)

    def body(*refs):
        p_refs = refs[:np_]
        w_ref, h_ref = refs[np_], refs[np_ + 1]
        rest = refs[np_ + 2:]
        acc = h_ref[...]
        off = 0
        for p_ref, wd in zip(p_refs, widths):
            acc = acc + _dot(p_ref[...], w_ref[off:off + wd, :], NN)
            off += wd
        if gnext is None:
            rest[0][...] = acc
        else:
            g_ref, ho_ref, no_ref = rest
            ho_ref[...] = acc
            r = lax.rsqrt(jnp.mean(acc * acc, axis=-1, keepdims=True) + RMS_EPS)
            no_ref[...] = (acc * r * g_ref[...]).astype(ACT)

    wspec = pl.BlockSpec(w.shape, lambda i: (0, 0), pipeline_mode=pl.Buffered(1))
    in_specs = [_rowspec(tm, wd) for wd in widths] + [wspec, _rowspec(tm, D_MODEL)]
    args = list(parts) + [w, h]
    out_specs = [_rowspec(tm, D_MODEL)]
    out_shape = [jax.ShapeDtypeStruct((s, D_MODEL), F32)]
    if gnext is not None:
        in_specs.append(_fullspec((1, D_MODEL)))
        args.append(gnext)
        out_specs.append(_rowspec(tm, D_MODEL))
        out_shape.append(jax.ShapeDtypeStruct((s, D_MODEL), ACT))
    out = pl.pallas_call(
        body, grid=(s // tm,), name=name, in_specs=in_specs, out_specs=out_specs,
        out_shape=out_shape, compiler_params=_cp("parallel"))(*args)
    return (out[0], None) if gnext is None else (out[0], out[1])


def _band_mask(n, max_dist):
    qi = lax.broadcasted_iota(jnp.int32, (BLOCK, 2 * BLOCK), 0)
    kj = lax.broadcasted_iota(jnp.int32, (BLOCK, 2 * BLOCK), 1)
    dist = qi + BLOCK - kj
    valid = jnp.logical_and(dist >= 0, dist <= max_dist)
    return jnp.logical_and(valid, jnp.logical_or(kj >= BLOCK, n > 0))


def _band_mask_t(n, max_dist):
    kj = lax.broadcasted_iota(jnp.int32, (2 * BLOCK, BLOCK), 0)
    qi = lax.broadcasted_iota(jnp.int32, (2 * BLOCK, BLOCK), 1)
    dist = qi + BLOCK - kj
    valid = jnp.logical_and(dist >= 0, dist <= max_dist)
    return jnp.logical_and(valid, jnp.logical_or(kj >= BLOCK, n > 0))


ATT_FWD_BLOCKS = 4


def _fold(a, d):
    return a.reshape(a.shape[0] // d, d * a.shape[1])


def attn_fwd(qk, v, *, d, hkv, max_dist, sink, out_dtype, name, folded=None, keep_folded=False):
    s = qk.shape[0]
    kvw = hkv * HEAD_DIM
    wqk = ATT_W + kvw
    assert qk.shape[1] == wqk and (d == 1 or (wqk % ATT_W == 0 and wqk % kvw == 0))
    assert sink is None or max_dist == BLOCK - 1
    nb = s // d // BLOCK
    grp = N_HEADS // hkv
    qpb, kpb, koff = wqk // ATT_W, wqk // kvw, ATT_W // kvw

    qb = ATT_FWD_BLOCKS if nb % ATT_FWD_BLOCKS == 0 else 1

    def body(*refs):
        if sink is None:
            q_ref, kc_ref, kp_ref, vc_ref, vp_ref, o_ref, l_ref, o_buf = refs
        else:
            q_ref, kc_ref, kp_ref, vc_ref, vp_ref, s_ref, o_ref, l_ref, o_buf = refs
        n = pl.program_id(1)
        qsl = [slice(h * HEAD_DIM, (h + 1) * HEAD_DIM) for h in range(N_HEADS)]
        ksl = [slice((h // grp) * HEAD_DIM, (h // grp + 1) * HEAD_DIM) for h in range(N_HEADS)]
        head_row = lax.broadcasted_iota(jnp.int32, (SUBLANES, BLOCK), 0)
        if sink is not None:
            sink_row = lax.broadcasted_iota(jnp.int32, (2 * BLOCK, BLOCK), 0) == 0
        for j in range(qb):
            rows = slice(j * BLOCK, (j + 1) * BLOCK)
            before = slice((j - 1) * BLOCK, j * BLOCK)
            valid = _band_mask_t(n * qb + j, max_dist)
            kk = jnp.concatenate([kp_ref[...] if j == 0 else kc_ref[before, :], kc_ref[rows, :]], axis=0)
            vv = jnp.concatenate([vp_ref[...] if j == 0 else vc_ref[before, :], vc_ref[rows, :]], axis=0)
            scores = []
            for h in range(N_HEADS):
                q = q_ref[rows, qsl[h]] * 0.125
                scores.append(_dot(kk[:, ksl[h]], q, NT))
            probs = []
            lse8 = jnp.zeros((SUBLANES, BLOCK), F32)
            for h in range(N_HEADS):
                sc = jnp.where(valid, scores[h], NEG)
                if sink is not None:
                    sc = jnp.where(sink_row, s_ref[h:h + 1, 0:1], sc)
                m = jnp.max(sc, axis=0, keepdims=True)
                p = jnp.exp(sc - m)
                l = jnp.sum(p, axis=0, keepdims=True)
                if sink is not None:
                    p = jnp.where(sink_row, 0.0, p)
                lse8 = jnp.where(head_row == h, m + jnp.log(l), lse8)
                probs.append((p * (1.0 / l)).astype(ACT))
            l_ref[rows, :] = jnp.concatenate([lse8, jnp.zeros((BLOCK - SUBLANES, BLOCK), F32)], axis=0).T
            for h in range(N_HEADS):
                o_buf[rows, qsl[h]] = _dot(probs[h], vv[:, ksl[h]], TN)
        o_ref[...] = o_buf[...].astype(o_ref.dtype)

    prev = lambda n: jnp.maximum(n * qb - 1, 0)
    in_specs = [
        pl.BlockSpec((qb * BLOCK, ATT_W), lambda r, n: (n, r * qpb)),
        pl.BlockSpec((qb * BLOCK, kvw), lambda r, n: (n, r * kpb + koff)),
        pl.BlockSpec((BLOCK, kvw), lambda r, n: (prev(n), r * kpb + koff)),
        pl.BlockSpec((qb * BLOCK, kvw), lambda r, n: (n, r)),
        pl.BlockSpec((BLOCK, kvw), lambda r, n: (prev(n), r)),
    ]
    qkf, vf = (_fold(qk, d), _fold(v, d)) if folded is None else folded
    args = [qkf, qkf, qkf, vf, vf]
    if sink is not None:
        in_specs.append(_fullspec((N_HEADS, LANES)))
        args.append(sink)
    ospec = pl.BlockSpec((qb * BLOCK, ATT_W), lambda r, n: (n, r))
    lspec = pl.BlockSpec((qb * BLOCK, LANES), lambda r, n: (n, r))
    o, lse = pl.pallas_call(
        body, grid=(d, nb // qb), name=name, in_specs=in_specs, out_specs=[ospec, lspec],
        out_shape=[jax.ShapeDtypeStruct((s // d, d * ATT_W), out_dtype),
                   jax.ShapeDtypeStruct((s // d, d * LANES), F32)],
        scratch_shapes=[pltpu.VMEM((qb * BLOCK, ATT_W), F32)],
        compiler_params=_cp("parallel", "parallel"))(*args)
    return (o, lse) if keep_folded else (o.reshape(s, ATT_W), lse.reshape(s, LANES))


def attn_bwd(qk, v, do_src, do_col, o, lse, *, d, hkv, max_dist, sink, out_dtype, name, folded=None,
             folded_do_o=None, keep_folded=False):
    s = qk.shape[0]
    kvw = hkv * HEAD_DIM
    wqk = ATT_W + kvw
    nb = s // d // BLOCK
    grp = N_HEADS // hkv
    qpb, kpb, koff = wqk // ATT_W, wqk // kvw, ATT_W // kvw
    dob = do_src.shape[1] // ATT_W
    has_sink = sink is not None

    def body(*refs):
        refs = list(refs)
        q_ref, kc_ref, kp_ref, vc_ref, vp_ref, do_ref, o_ref, l_ref = refs[:8]
        pos = 8
        if has_sink:
            s_ref = refs[pos]
            pos += 1
        dq_ref, dk_ref, dv_ref = refs[pos:pos + 3]
        pos += 3
        if has_sink:
            ds_ref = refs[pos]
            pos += 1
        ck_ref, cv_ref, dq_buf, dk_buf, dv_buf = refs[pos:pos + 5]
        r_id = pl.program_id(0)
        n = pl.program_id(1)

        @pl.when(n == 0)
        def _():
            ck_ref[...] = jnp.zeros_like(ck_ref)
            cv_ref[...] = jnp.zeros_like(cv_ref)

        if has_sink:
            @pl.when(jnp.logical_and(n == 0, r_id == 0))
            def _():
                ds_ref[...] = jnp.zeros_like(ds_ref)

        @pl.when(n < nb)
        def _():
            valid = _band_mask_t(n, max_dist)
            qsl = [slice(h * HEAD_DIM, (h + 1) * HEAD_DIM) for h in range(N_HEADS)]
            ksl = [slice((h // grp) * HEAD_DIM, (h // grp + 1) * HEAD_DIM) for h in range(N_HEADS)]
            kk = jnp.concatenate([kp_ref[...], kc_ref[...]], axis=0)
            vv = jnp.concatenate([vp_ref[...], vc_ref[...]], axis=0)
            qs, first = [], []
            for h in range(N_HEADS):
                q = q_ref[:, qsl[h]] * 0.125
                qs.append(q)
                first.append((_dot(kk[:, ksl[h]], q, NT), _dot(vv[:, ksl[h]], do_ref[:, qsl[h]], NT)))
            lse_t = l_ref[...].T
            prod = do_ref[...].astype(F32) * o_ref[...].astype(F32)
            hi = prod.astype(ACT)
            lo = (prod - hi.astype(F32)).astype(ACT)
            col = lax.broadcasted_iota(jnp.int32, (LANES, ATT_W), 1)
            row = lax.broadcasted_iota(jnp.int32, (LANES, ATT_W), 0)
            head_of = jnp.logical_and(col >= row * HEAD_DIM, col < (row + 1) * HEAD_DIM).astype(ACT)
            e_t = _dot(head_of, hi, NT) + _dot(head_of, lo, NT)
            mid = []
            for h in range(N_HEADS):
                s_t, dp_t = first[h]
                lse_h, e_h = lse_t[h:h + 1, :], e_t[h:h + 1, :]
                p_t = jnp.exp(jnp.where(valid, s_t, NEG) - lse_h)
                mid.append(((p_t * (dp_t - e_h)).astype(ACT), p_t.astype(ACT)))
                if has_sink:
                    sk = s_ref[h:h + 1, 0:1]
                    dsk = -jnp.sum(jnp.exp(sk - lse_h) * e_h, axis=1, keepdims=True)
                    ds_ref[h:h + 1, :] = ds_ref[h:h + 1, :] + dsk
            dkk = [None] * hkv
            dvv = [None] * hkv
            for h in range(N_HEADS):
                kh = h // grp
                ds_t, p_t = mid[h]
                dq_buf[:, qsl[h]] = _dot(ds_t, kk[:, ksl[h]], TN) * 0.125
                for lst, val in ((dkk, _dot(ds_t, qs[h], NN)), (dvv, _dot(p_t, do_ref[:, qsl[h]], NN))):
                    lst[kh] = val if lst[kh] is None else lst[kh] + val
            for kh in range(hkv):
                ks = slice(kh * HEAD_DIM, (kh + 1) * HEAD_DIM)
                dk_buf[:, ks] = ck_ref[:, ks] + dkk[kh][:BLOCK]
                dv_buf[:, ks] = cv_ref[:, ks] + dvv[kh][:BLOCK]
                ck_ref[:, ks] = dkk[kh][BLOCK:]
                cv_ref[:, ks] = dvv[kh][BLOCK:]
            dq_ref[...] = dq_buf[...].astype(dq_ref.dtype)
            dk_ref[...] = dk_buf[...].astype(dk_ref.dtype)
            dv_ref[...] = dv_buf[...].astype(dv_ref.dtype)

        @pl.when(n == nb)
        def _():
            dk_ref[...] = ck_ref[...].astype(dk_ref.dtype)
            dv_ref[...] = cv_ref[...].astype(dv_ref.dtype)

    qrow = lambda n: jnp.minimum(n, nb - 1)
    prow = lambda n: jnp.maximum(jnp.minimum(n, nb - 1) - 1, 0)
    krow = lambda n: jnp.maximum(n - 1, 0)
    in_specs = [
        pl.BlockSpec((BLOCK, ATT_W), lambda r, n: (qrow(n), r * qpb)),
        pl.BlockSpec((BLOCK, kvw), lambda r, n: (qrow(n), r * kpb + koff)),
        pl.BlockSpec((BLOCK, kvw), lambda r, n: (prow(n), r * kpb + koff)),
        pl.BlockSpec((BLOCK, kvw), lambda r, n: (qrow(n), r)),
        pl.BlockSpec((BLOCK, kvw), lambda r, n: (prow(n), r)),
        pl.BlockSpec((BLOCK, ATT_W), lambda r, n: (qrow(n), r * dob + do_col)),
        pl.BlockSpec((BLOCK, ATT_W), lambda r, n: (qrow(n), r)),
        pl.BlockSpec((BLOCK, LANES), lambda r, n: (qrow(n), r)),
    ]
    qkf, vf = (_fold(qk, d), _fold(v, d)) if folded is None else folded
    dof, of = (_fold(do_src, d), _fold(o, d)) if folded_do_o is None else folded_do_o
    args = [qkf, qkf, qkf, vf, vf, dof, of, _fold(lse, d)]
    if has_sink:
        in_specs.append(_fullspec((N_HEADS, LANES)))
        args.append(sink)
    qspec = pl.BlockSpec((BLOCK, ATT_W), lambda r, n: (qrow(n), r))
    kspec = pl.BlockSpec((BLOCK, kvw), lambda r, n: (krow(n), r))
    out_specs = [qspec, kspec, kspec]
    out_shape = [jax.ShapeDtypeStruct((s // d, d * ATT_W), out_dtype),
                 jax.ShapeDtypeStruct((s // d, d * kvw), out_dtype),
                 jax.ShapeDtypeStruct((s // d, d * kvw), out_dtype)]
    if has_sink:
        out_specs.append(_fullspec((N_HEADS, LANES)))
        out_shape.append(jax.ShapeDtypeStruct((N_HEADS, LANES), F32))
    out = pl.pallas_call(
        body, grid=(d, nb + 1), name=name, in_specs=in_specs, out_specs=out_specs,
        out_shape=out_shape,
        scratch_shapes=[pltpu.VMEM((BLOCK, kvw), F32), pltpu.VMEM((BLOCK, kvw), F32),
                        pltpu.VMEM((BLOCK, ATT_W), F32), pltpu.VMEM((BLOCK, kvw), F32), pltpu.VMEM((BLOCK, kvw), F32)],
        compiler_params=_cp("arbitrary", "arbitrary"))(*args)
    res = list(out[:3]) if keep_folded else [out[0].reshape(s, ATT_W), out[1].reshape(s, kvw), out[2].reshape(s, kvw)]
    if has_sink:
        res.append(out[3])
    return res


def combine_fwd(os_, lses, dils, name):
    s = os_[0].shape[0] * dils[0]
    tm = _tile(s, (512,))
    fold_dils = [d for d in dils if d > 1]

    def body(o1, o2, o3, l1, l2, l3, c_ref, l_ref, *rest):
        c_buf = rest[-1]
        cache = {}
        o = [r[...].astype(F32) if d == 1 else _unfold_rows(r[...], d, cache) for r, d in zip((o1, o2, o3), dils)]
        a, b, c = [r[...] if d == 1 else _unfold_rows_f32(r[...], d, cache) for r, d in zip((l1, l2, l3), dils)]
        m = jnp.maximum(jnp.maximum(a, b), c)
        wa, wb, wc = jnp.exp(a - m), jnp.exp(b - m), jnp.exp(c - m)
        tot = wa + wb + wc
        l_ref[...] = m + jnp.log(tot)
        rt = 1.0 / tot
        wa, wb, wc = wa * rt, wb * rt, wc * rt
        for h in range(N_HEADS):
            cs = slice(h * HEAD_DIM, (h + 1) * HEAD_DIM)
            c_buf[:, cs] = (wa[:, h:h + 1] * o[0][:, cs] + wb[:, h:h + 1] * o[1][:, cs] + wc[:, h:h + 1] * o[2][:, cs])
        mix = c_buf[...].astype(ACT)
        c_ref[...] = mix
        for f_ref, d in zip(rest[:-1], fold_dils):
            f_ref[...] = _fold_rows(mix, d, cache)

    return pl.pallas_call(
        body, grid=(s // tm,), name=name,
        in_specs=[_rowspec(tm // d, d * ATT_W) for d in dils] + [_rowspec(tm // d, d * LANES) for d in dils],
        out_specs=[_rowspec(tm, ATT_W), _rowspec(tm, LANES)] + [_rowspec(tm // d, d * ATT_W) for d in fold_dils],
        out_shape=[jax.ShapeDtypeStruct((s, ATT_W), ACT), jax.ShapeDtypeStruct((s, LANES), F32)]
        + [jax.ShapeDtypeStruct((s // d, d * ATT_W), ACT) for d in fold_dils],
        scratch_shapes=[pltpu.VMEM((tm, ATT_W), F32)],
        compiler_params=_cp("parallel"))(*os_, *lses)


def assemble(parts, tabs, name):
    terms_of = [[t if isinstance(t, tuple) else (t, 1) for t in terms] for terms, _ in parts]
    flat = [t for ts in terms_of for t in ts]
    s = terms_of[0][0][0].shape[0] * terms_of[0][0][1]
    tm = _tile(s, (512,))
    widths = [ts[0][0].shape[1] // ts[0][1] for ts in terms_of]
    flags = [f for _, f in parts]

    def body(*refs):
        c_ref, sa_ref, sb_ref, o_ref = refs[len(flat):]
        c, sa, sb = c_ref[...], sa_ref[...], sb_ref[...]
        off = 0
        first = 0
        cache = {}
        for wd, ts, fl in zip(widths, terms_of, flags):
            t_refs = refs[first:first + len(ts)]
            first += len(ts)
            x = None
            for t_ref, (_, d) in zip(t_refs, ts):
                t = t_ref[...].astype(F32) if d == 1 else _unfold_rows(t_ref[...], d, cache)
                x = t if x is None else x + t
            for g in range(wd // LANES):
                cols = slice(g * LANES, (g + 1) * LANES)
                y = _rope_t(x[:, cols], c, sa, sb) if fl else x[:, cols]
                o_ref[:, off + g * LANES:off + (g + 1) * LANES] = y.astype(ACT)
            off += wd

    tot = sum(widths)
    return pl.pallas_call(
        body, grid=(s // tm,), name=name,
        in_specs=[_rowspec(tm // d, a.shape[1]) for a, d in flat] + [_rowspec(tm, LANES)] * 3,
        out_specs=_rowspec(tm, tot), out_shape=jax.ShapeDtypeStruct((s, tot), ACT),
        compiler_params=_cp("parallel"))(*[a for a, _ in flat], *tabs)


def _ln_stats(x):
    mu = jnp.mean(x, axis=-1, keepdims=True)
    xc = x - mu
    var = jnp.mean(xc * xc, axis=-1, keepdims=True)
    rstd = lax.rsqrt(var + LN_EPS)
    return xc * rstd, rstd


SUBLANES = 8


TAP_ROWS = 32


def _tap_sum(buf, cw_ref, offsets, tm, res_ref):
    for r0 in range(0, tm, TAP_ROWS):
        acc = None
        for ph in range(SUBLANES):
            taps = [j for j, off in enumerate(offsets) if off % SUBLANES == ph]
            if not taps:
                continue
            rows = TAP_ROWS if ph == 0 else TAP_ROWS + SUBLANES
            part = None
            for j in taps:
                term = cw_ref[j:j + 1, :] * buf[pl.ds(offsets[j] - ph + r0, rows), :]
                part = term if part is None else part + term
            part = part[ph:ph + TAP_ROWS]
            acc = part if acc is None else acc + part
        res_ref[pl.ds(r0, TAP_ROWS), :] = acc


def conv_fwd(ga, gb, cw, cb, lg, lb, name):
    s = ga.shape[0]
    tm = _tile(s, (512,))
    hb = tm // CONV_HALO

    def body(ga_ref, gb_ref, gah_ref, gbh_ref, cw_ref, cb_ref, lg_ref, lb_ref, c_ref, c1_ref, buf):
        i = pl.program_id(0)
        halo = gah_ref[...].astype(F32) * _sigmoid(gbh_ref[...].astype(F32))
        buf[0:CONV_HALO, :] = jnp.where(i > 0, halo, 0.0)
        buf[CONV_HALO:, :] = ga_ref[...].astype(F32) * _sigmoid(gb_ref[...].astype(F32))
        first = CONV_HALO - (CONV_WIDTH - 1)
        _tap_sum(buf, cw_ref, [first + j for j in range(CONV_WIDTH)], tm, c1_ref)
        acc = c1_ref[...] + cb_ref[...]
        c1_ref[...] = acc
        xh, _ = _ln_stats(acc)
        y = xh * lg_ref[...] + lb_ref[...]
        c_ref[...] = (y * _sigmoid(y)).astype(ACT)

    hspec = pl.BlockSpec((CONV_HALO, CONV_CH), lambda i: (jnp.maximum(i * hb - 1, 0), 0))
    vec = _fullspec((1, CONV_CH))
    spec = _rowspec(tm, CONV_CH)
    return pl.pallas_call(
        body, grid=(s // tm,), name=name,
        in_specs=[spec, spec, hspec, hspec, _fullspec((CONV_WIDTH, CONV_CH)), vec, vec, vec],
        out_specs=[spec, spec],
        out_shape=[jax.ShapeDtypeStruct((s, CONV_CH), ACT), jax.ShapeDtypeStruct((s, CONV_CH), F32)],
        scratch_shapes=[pltpu.VMEM((tm + CONV_HALO, CONV_CH), F32)],
        compiler_params=_cp("parallel"))(ga, gb, ga, gb, cw, cb, lg, lb)


def conv_bwd_ln(c1, dsrc, dcol, lg, lb, name):
    s = c1.shape[0]
    tm = _tile(s, (512,))

    def body(c1_ref, d_ref, lg_ref, lb_ref, o_ref, dg_ref, db_ref):
        @pl.when(pl.program_id(0) == 0)
        def _():
            dg_ref[...] = jnp.zeros_like(dg_ref)
            db_ref[...] = jnp.zeros_like(db_ref)

        xh, rstd = _ln_stats(c1_ref[...].astype(F32))
        y = xh * lg_ref[...] + lb_ref[...]
        sg = _sigmoid(y)
        dy = d_ref[...].astype(F32) * (sg * (1.0 + y * (1.0 - sg)))
        dg_ref[0:1, :] = dg_ref[0:1, :] + jnp.sum(dy * xh, axis=0, keepdims=True)
        db_ref[0:1, :] = db_ref[0:1, :] + jnp.sum(dy, axis=0, keepdims=True)
        dxh = dy * lg_ref[...]
        o_ref[...] = rstd * (dxh - jnp.mean(dxh, axis=-1, keepdims=True)
                             - xh * jnp.mean(dxh * xh, axis=-1, keepdims=True))

    vec = _fullspec((1, CONV_CH))
    acc = _fullspec((8, CONV_CH))
    return pl.pallas_call(
        body, grid=(s // tm,), name=name,
        in_specs=[_rowspec(tm, CONV_CH), _rowspec(tm, CONV_CH, dcol), vec, vec],
        out_specs=[_rowspec(tm, CONV_CH), acc, acc],
        out_shape=[jax.ShapeDtypeStruct((s, CONV_CH), F32)] + [jax.ShapeDtypeStruct((8, CONV_CH), F32)] * 2,
        compiler_params=_cp("arbitrary"))(c1, dsrc, lg, lb)


def conv_bwd_conv(dc1, ga, gb, cw, name):
    s = ga.shape[0]
    tm = _tile(s, (512,))
    hb = tm // CONV_HALO
    nt = s // tm
    last_h = s // CONV_HALO - 1
    first = CONV_HALO - (CONV_WIDTH - 1)


    def rows8(x):
        return jnp.sum(x.reshape(x.shape[0] // SUBLANES, SUBLANES, CONV_CH), axis=0)

    def body(d_ref, dn_ref, ga_ref, gb_ref, gah_ref, gbh_ref, cw_ref,
             dga_ref, dgb_ref, dw_ref, db_ref, dbuf, cbuf, sbuf):
        i = pl.program_id(0)

        @pl.when(i == 0)
        def _():
            dw_ref[...] = jnp.zeros_like(dw_ref)
            db_ref[...] = jnp.zeros_like(db_ref)

        d = d_ref[...]
        dbuf[0:tm, :] = d
        dbuf[tm:, :] = jnp.where(i < nt - 1, dn_ref[...], 0.0)
        halo = gah_ref[...].astype(F32) * _sigmoid(gbh_ref[...].astype(F32))
        cbuf[0:CONV_HALO, :] = jnp.where(i > 0, halo, 0.0)
        a = ga_ref[...].astype(F32)
        sg = _sigmoid(gb_ref[...].astype(F32))
        cbuf[CONV_HALO:, :] = a * sg
        for ph in range(SUBLANES):
            taps = [j for j in range(CONV_WIDTH) if (first + j) % SUBLANES == ph]
            if ph:
                sbuf[0:tm + CONV_HALO - SUBLANES, :] = cbuf[pl.ds(ph, tm + CONV_HALO - SUBLANES), :]
            src = sbuf if ph else cbuf
            for r0 in range(0, tm, TAP_ROWS):
                d_blk = dbuf[pl.ds(r0, TAP_ROWS), :]
                for j in taps:
                    tap = src[pl.ds(first + j - ph + r0, TAP_ROWS), :]
                    rows = slice(j * SUBLANES, (j + 1) * SUBLANES)
                    dw_ref[rows, :] = dw_ref[rows, :] + rows8(d_blk * tap)
        db_ref[...] = db_ref[...] + rows8(d)
        _tap_sum(dbuf, cw_ref, [CONV_WIDTH - 1 - j for j in range(CONV_WIDTH)], tm, sbuf)
        dc0 = sbuf[0:tm, :]
        dga_ref[...] = (dc0 * sg).astype(ACT)
        dgb_ref[...] = (dc0 * a * sg * (1.0 - sg)).astype(ACT)

    spec = _rowspec(tm, CONV_CH)
    hprev = pl.BlockSpec((CONV_HALO, CONV_CH), lambda i: (jnp.maximum(i * hb - 1, 0), 0))
    hnext = pl.BlockSpec((CONV_HALO, CONV_CH), lambda i: (jnp.minimum((i + 1) * hb, last_h), 0))
    return pl.pallas_call(
        body, grid=(nt,), name=name,
        in_specs=[spec, hnext, spec, spec, hprev, hprev, _fullspec((CONV_WIDTH, CONV_CH))],
        out_specs=[spec, spec, _fullspec((CONV_HALO * SUBLANES, CONV_CH)), _fullspec((SUBLANES, CONV_CH))],
        out_shape=[jax.ShapeDtypeStruct((s, CONV_CH), ACT)] * 2
        + [jax.ShapeDtypeStruct((CONV_HALO * SUBLANES, CONV_CH), F32), jax.ShapeDtypeStruct((SUBLANES, CONV_CH), F32)],
        scratch_shapes=[pltpu.VMEM((tm + CONV_HALO, CONV_CH), F32)] * 3,
        compiler_params=_cp("arbitrary"))(dc1, dc1, ga, gb, ga, gb, cw)


_GELU_K = math.sqrt(2.0 / math.pi)
_GELU_C = 0.044715


def _gelu(x):
    return 0.5 * x * (1.0 + jnp.tanh(_GELU_K * (x + _GELU_C * x * x * x)))


def _gelu_grad(x):
    t = jnp.tanh(_GELU_K * (x + _GELU_C * x * x * x))
    return 0.5 * (1.0 + t) + 0.5 * x * (1.0 - t * t) * _GELU_K * (1.0 + 3.0 * _GELU_C * x * x)


def _tril():
    qi = lax.broadcasted_iota(jnp.int32, (BLOCK, BLOCK), 0)
    kj = lax.broadcasted_iota(jnp.int32, (BLOCK, BLOCK), 1)
    return kj <= qi


GMLP_CHUNKS = 4


def _gmlp_weights(sw_ref, w_buf, wt_buf):
    tril = _tril()
    for g in range(D_GROUPS):
        w = jnp.where(tril, sw_ref[g], 0.0)
        w_buf[g] = w.astype(ACT)
        if wt_buf is not None:
            wt_buf[g] = w.T.astype(ACT)


def _gmlp_mix(w_buf, gn, sb_ref, m_buf):
    for c in range(GMLP_CHUNKS):
        rows = slice(c * BLOCK, (c + 1) * BLOCK)
        for g in range(D_GROUPS):
            cs = slice(g * HEAD_DIM, (g + 1) * HEAD_DIM)
            m_buf[rows, cs] = _dot(w_buf[g], gn[rows, cs], NN) + sb_ref[:, cs]


def gmlp_fwd(z, lg, lb, sw, sbx, name):
    s = z.shape[0]
    tm = GMLP_CHUNKS * BLOCK

    def body(z_ref, lg_ref, lb_ref, sw_ref, sb_ref, o_ref, w_buf, m_buf):
        @pl.when(pl.program_id(0) == 0)
        def _():
            _gmlp_weights(sw_ref, w_buf, None)

        zz = _gelu(z_ref[...].astype(F32))
        xh, _ = _ln_stats(zz[:, D_CH:])
        gn = (xh * lg_ref[...] + lb_ref[...]).astype(ACT)
        _gmlp_mix(w_buf, gn, sb_ref, m_buf)
        o_ref[...] = (zz[:, :D_CH] * m_buf[...]).astype(ACT)

    return pl.pallas_call(
        body, grid=(s // tm,), name=name,
        in_specs=[_rowspec(tm, 2 * D_CH), _fullspec((1, D_CH)), _fullspec((1, D_CH)),
                  _fullspec((D_GROUPS, BLOCK, BLOCK)), _fullspec((BLOCK, D_CH))],
        out_specs=_rowspec(tm, D_CH), out_shape=jax.ShapeDtypeStruct((s, D_CH), ACT),
        scratch_shapes=[pltpu.VMEM((D_GROUPS, BLOCK, BLOCK), ACT), pltpu.VMEM((tm, D_CH), F32)],
        compiler_params=_cp("arbitrary"))(z, lg, lb, sw, sbx)


def gmlp_bwd(z, dsrc, dcol, lg, lb, sw, sbx, name):
    s = z.shape[0]
    tm = GMLP_CHUNKS * BLOCK

    def body(z_ref, d_ref, lg_ref, lb_ref, sw_ref, sb_ref, dz_ref, dw_ref, dsb_ref, dg_ref, db_ref,
             w_buf, wt_buf, m_buf, dgn_buf):
        @pl.when(pl.program_id(0) == 0)
        def _():
            _gmlp_weights(sw_ref, w_buf, wt_buf)
            dw_ref[...] = jnp.zeros_like(dw_ref)
            dsb_ref[...] = jnp.zeros_like(dsb_ref)
            dg_ref[...] = jnp.zeros_like(dg_ref)
            db_ref[...] = jnp.zeros_like(db_ref)

        zr = z_ref[...].astype(F32)
        zz = _gelu(zr)
        u = zz[:, :D_CH]
        xh, rstd = _ln_stats(zz[:, D_CH:])
        gn = (xh * lg_ref[...] + lb_ref[...]).astype(ACT)
        dd = d_ref[...].astype(F32)
        _gmlp_mix(w_buf, gn, sb_ref, m_buf)
        dz_ref[:, :D_CH] = (dd * m_buf[...] * _gelu_grad(zr[:, :D_CH])).astype(ACT)
        dmix = dd * u
        dmix_a = dmix.astype(ACT)
        dsb = dmix[0:BLOCK]
        for c in range(1, GMLP_CHUNKS):
            dsb = dsb + dmix[c * BLOCK:(c + 1) * BLOCK]
        dsb_ref[...] = dsb_ref[...] + dsb
        tril = _tril()
        for g in range(D_GROUPS):
            cs = slice(g * HEAD_DIM, (g + 1) * HEAD_DIM)
            dw = None
            for c in range(GMLP_CHUNKS):
                rows = slice(c * BLOCK, (c + 1) * BLOCK)
                t = _dot(dmix_a[rows, cs], gn[rows, cs], NT)
                dw = t if dw is None else dw + t
                dgn_buf[rows, cs] = _dot(wt_buf[g], dmix_a[rows, cs], NN)
            dw_ref[g] = dw_ref[g] + jnp.where(tril, dw, 0.0)
        dgn = dgn_buf[...]
        dg_ref[0:1, :] = dg_ref[0:1, :] + jnp.sum(dgn * xh, axis=0, keepdims=True)
        db_ref[0:1, :] = db_ref[0:1, :] + jnp.sum(dgn, axis=0, keepdims=True)
        dxh = dgn * lg_ref[...]
        dgate = rstd * (dxh - jnp.mean(dxh, axis=-1, keepdims=True)
                        - xh * jnp.mean(dxh * xh, axis=-1, keepdims=True))
        dz_ref[:, D_CH:] = (dgate * _gelu_grad(zr[:, D_CH:])).astype(ACT)

    vec = _fullspec((1, D_CH))
    acc = _fullspec((8, D_CH))
    wshape = (D_GROUPS, BLOCK, BLOCK)
    return pl.pallas_call(
        body, grid=(s // tm,), name=name,
        in_specs=[_rowspec(tm, 2 * D_CH), _rowspec(tm, D_CH, dcol), vec, vec, _fullspec(wshape),
                  _fullspec((BLOCK, D_CH))],
        out_specs=[_rowspec(tm, 2 * D_CH), _fullspec(wshape), _fullspec((BLOCK, D_CH)), acc, acc],
        out_shape=[jax.ShapeDtypeStruct((s, 2 * D_CH), ACT), jax.ShapeDtypeStruct(wshape, F32),
                   jax.ShapeDtypeStruct((BLOCK, D_CH), F32),
                   jax.ShapeDtypeStruct((8, D_CH), F32), jax.ShapeDtypeStruct((8, D_CH), F32)],
        scratch_shapes=[pltpu.VMEM(wshape, ACT), pltpu.VMEM(wshape, ACT),
                        pltpu.VMEM((tm, D_CH), F32), pltpu.VMEM((tm, D_CH), F32)],
        compiler_params=_cp("arbitrary"))(z, dsrc, lg, lb, sw, sbx)


def _rms_bwd(dn, x, g):
    r = lax.rsqrt(jnp.mean(x * x, axis=-1, keepdims=True) + RMS_EPS)
    u = dn * g
    dx = r * u - x * (r * r * r) * jnp.mean(x * u, axis=-1, keepdims=True)
    return dx, dn * x * r


def final_loss(h, g, target, name):
    s = h.shape[0]
    tm = _tile(s, (1024, 512))

    def body(h_ref, g_ref, t_ref, loss_ref, dh_ref, dg_ref):
        @pl.when(pl.program_id(0) == 0)
        def _():
            loss_ref[...] = jnp.zeros_like(loss_ref)
            dg_ref[...] = jnp.zeros_like(dg_ref)

        x = h_ref[...]
        r = lax.rsqrt(jnp.mean(x * x, axis=-1, keepdims=True) + RMS_EPS)
        diff = x * r * g_ref[...] - t_ref[...]
        part = jnp.sum(jnp.sum(diff * diff, axis=-1, keepdims=True), axis=0, keepdims=True)
        loss_ref[...] = loss_ref[...] + part * (0.5 / D_MODEL)
        dx, dgt = _rms_bwd(diff * (1.0 / D_MODEL), x, g_ref[...])
        dh_ref[...] = dx
        dg_ref[0:1, :] = dg_ref[0:1, :] + jnp.sum(dgt, axis=0, keepdims=True)

    spec = _rowspec(tm, D_MODEL)
    return pl.pallas_call(
        body, grid=(s // tm,), name=name,
        in_specs=[spec, _fullspec((1, D_MODEL)), spec],
        out_specs=[_fullspec((8, LANES)), spec, _fullspec((8, D_MODEL))],
        out_shape=[jax.ShapeDtypeStruct((8, LANES), F32), jax.ShapeDtypeStruct((s, D_MODEL), F32),
                   jax.ShapeDtypeStruct((8, D_MODEL), F32)],
        compiler_params=_cp("arbitrary"))(h, g, target)


def mm_nt(dy, w, name, fold_dils=(), fold_cols=0):
    s, n = dy.shape
    k = w.shape[0]
    tm = _tile(s, (512,) if fold_dils else (1024, 512))
    tk = k if fold_dils else _tile(k, (1024, 512))

    def body(d_ref, w_ref, o_ref, *f_refs):
        res = _dot(d_ref[...].astype(ACT), w_ref[...], NT).astype(ACT)
        o_ref[...] = res
        cache = {}
        for f_ref, d in zip(f_refs, fold_dils):
            f_ref[...] = _fold_rows(res[:, :fold_cols], d, cache)

    out = pl.pallas_call(
        body, grid=(k // tk, s // tm), name=name,
        in_specs=[pl.BlockSpec((tm, n), lambda j, i: (i, 0)), pl.BlockSpec((tk, n), lambda j, i: (j, 0))],
        out_specs=[pl.BlockSpec((tm, tk), lambda j, i: (i, j))]
        + [pl.BlockSpec((tm // d, d * fold_cols), lambda j, i: (i, 0)) for d in fold_dils],
        out_shape=[jax.ShapeDtypeStruct((s, k), ACT)]
        + [jax.ShapeDtypeStruct((s // d, d * fold_cols), ACT) for d in fold_dils],
        compiler_params=_cp("parallel", "parallel"))(dy, w)
    return out if fold_dils else out[0]


def ffn_down_bwd(dh, wd, dact_dgate, dact_dup, name):
    s = dh.shape[0]
    f = wd.shape[0]
    tm = _tile(s, (1024, 512))
    tf = _tile(f, (1408, 512, 256, 128))

    def body(d_ref, w_ref, g_ref, u_ref, dg_ref, du_ref):
        dact = _dot(d_ref[...].astype(ACT), w_ref[...], NT)
        dg_ref[...] = (dact * g_ref[...].astype(F32)).astype(ACT)
        du_ref[...] = (dact * u_ref[...].astype(F32)).astype(ACT)

    tile = pl.BlockSpec((tm, tf), lambda j, i: (i, j))
    return pl.pallas_call(
        body, grid=(f // tf, s // tm), name=name,
        in_specs=[pl.BlockSpec((tm, D_MODEL), lambda j, i: (i, 0)),
                  pl.BlockSpec((tf, D_MODEL), lambda j, i: (j, 0)), tile, tile],
        out_specs=[tile, tile], out_shape=[jax.ShapeDtypeStruct((s, f), ACT)] * 2,
        compiler_params=_cp("parallel", "parallel"))(dh, wd, dact_dgate, dact_dup)


def mm_nt_rms(parts, h, g, dh, name):
    s = h.shape[0]
    tm = _tile(s, (512,))
    wspec = lambda w: pl.BlockSpec(w.shape, lambda i: (0, 0), pipeline_mode=pl.Buffered(1))
    np_ = len(parts)

    def body(*refs):
        d_refs = refs[:np_]
        w_refs = refs[np_:2 * np_]
        h_ref, g_ref, dh_ref, o_ref, dg_ref = refs[2 * np_:]

        @pl.when(pl.program_id(0) == 0)
        def _():
            dg_ref[...] = jnp.zeros_like(dg_ref)

        dn = None
        for d_ref, w_ref in zip(d_refs, w_refs):
            t = _dot(d_ref[...], w_ref[...], NT)
            dn = t if dn is None else dn + t
        dx, dgt = _rms_bwd(dn, h_ref[...], g_ref[...])
        o_ref[...] = dh_ref[...] + dx
        dg_ref[0:1, :] = dg_ref[0:1, :] + jnp.sum(dgt, axis=0, keepdims=True)

    spec = _rowspec(tm, D_MODEL)
    return pl.pallas_call(
        body, grid=(s // tm,), name=name,
        in_specs=[_rowspec(tm, d.shape[1]) for d, _ in parts] + [wspec(w) for _, w in parts]
        + [spec, _fullspec((1, D_MODEL)), spec],
        out_specs=[spec, _fullspec((8, D_MODEL))],
        out_shape=[jax.ShapeDtypeStruct((s, D_MODEL), F32), jax.ShapeDtypeStruct((8, D_MODEL), F32)],
        compiler_params=_cp("arbitrary"))(*[d for d, _ in parts], *[w for _, w in parts], h, g, dh)


def mm_tn(a, b, name):
    s, k = a.shape
    n = b.shape[1]
    tk = _tile(k, (1024, 1408, 512, 256, 128))
    tn = _tile(n, (1408, 1280, 1024, 896, 512, 256, 128))
    ts = _tile(s, (2048, 512) if b.dtype == ACT else (1024, 512))
    nt = s // ts

    def body(a_ref, b_ref, o_ref, acc):
        t = _dot(a_ref[...].astype(ACT), b_ref[...].astype(ACT), TN)
        step = pl.program_id(2)

        @pl.when(step == 0)
        def _():
            acc[...] = t

        @pl.when(step > 0)
        def _():
            acc[...] = acc[...] + t

        @pl.when(step == nt - 1)
        def _():
            o_ref[...] = acc[...].astype(ACT)

    return pl.pallas_call(
        body, grid=(k // tk, n // tn, nt), name=name,
        in_specs=[pl.BlockSpec((ts, tk), lambda i, j, t: (t, i)), pl.BlockSpec((ts, tn), lambda i, j, t: (t, j))],
        out_specs=pl.BlockSpec((tk, tn), lambda i, j, t: (i, j)),
        out_shape=jax.ShapeDtypeStruct((k, n), ACT),
        scratch_shapes=[pltpu.VMEM((tk, tn), F32)],
        compiler_params=_cp("parallel", "parallel", "arbitrary"))(a, b)


def _adamw_math(w, g, m, v):
    m = ADAM_B1 * m + (1.0 - ADAM_B1) * g
    v = ADAM_B2 * v + (1.0 - ADAM_B2) * (g * g)
    m_hat = m / (1.0 - ADAM_B1 ** ADAM_STEP)
    v_hat = v / (1.0 - ADAM_B2 ** ADAM_STEP)
    delta = -ADAM_LR * (m_hat / (jnp.sqrt(v_hat) + ADAM_EPS) + ADAM_WD * w)
    return delta, m, v


def sum_adamw(parts, w, m, v, layer, others, name):
    nl, r, c = w.shape
    tr = _tile(r, (256, 128, 64, 32, 16, 8))

    def body(p_ref, w_ref, m_ref, v_ref, *rest):
        g_ref, d_ref, mo_ref, vo_ref = rest[-4:]
        g = p_ref[0].astype(F32)
        for i in range(1, N_DEV):
            g = g + p_ref[i].astype(F32)
        d, mm, vv = _adamw_math(w_ref[...], g, m_ref[...], v_ref[...])
        g_ref[...] = g
        d_ref[...] = d
        mo_ref[...] = mm
        vo_ref[...] = vv

    spec = pl.BlockSpec((None, tr, c), lambda i: (layer, i, 0))
    in_specs = [pl.BlockSpec((N_DEV, tr, c), lambda i: (0, i, 0))] + [spec] * 3
    args = [parts, w, m, v]
    aliases = {}
    if others is not None:
        in_specs += [pl.BlockSpec(memory_space=pl.ANY)] * 4
        args += list(others)
        aliases = {4 + j: j for j in range(4)}
    return pl.pallas_call(
        body, grid=(r // tr,), name=name, in_specs=in_specs, out_specs=[spec] * 4,
        out_shape=[jax.ShapeDtypeStruct((nl, r, c), F32)] * 4, input_output_aliases=aliases,
        compiler_params=_cp("parallel"))(*args)


def cast_layers(items, name):
    n = len(items)

    def body(*refs):
        for i in range(n):
            refs[n + i][...] = refs[i][...].astype(ACT)

    return pl.pallas_call(
        body, grid=(1,), name=name,
        in_specs=[pl.BlockSpec((None,) + w.shape[1:], lambda i, l=l: (l, 0, 0)) for w, l in items],
        out_specs=[_fullspec(w.shape[1:]) for w, _ in items],
        out_shape=[jax.ShapeDtypeStruct(w.shape[1:], ACT) for w, _ in items],
        compiler_params=_cp("arbitrary"))(*[w for w, _ in items])


def adamw_small(ws, gs, ms, vs, name):
    k = len(ws)

    def body(*refs):
        for i in range(k):
            w_ref, g_ref, m_ref, v_ref = (refs[j * k + i] for j in range(4))
            d, mm, vv = _adamw_math(w_ref[...], g_ref[...], m_ref[...], v_ref[...])
            refs[4 * k + i][...] = d
            refs[5 * k + i][...] = mm
            refs[6 * k + i][...] = vv

    shapes = [jax.ShapeDtypeStruct(w.shape, F32) for w in ws]
    specs = [_fullspec(w.shape) for w in ws]
    out = pl.pallas_call(
        body, grid=(1,), name=name, in_specs=specs * 4, out_specs=specs * 3, out_shape=shapes * 3,
        compiler_params=_cp("arbitrary"))(*ws, *gs, *ms, *vs)
    return out[:k], out[k:2 * k], out[2 * k:]


def sum_slots(x, out_dtype, name):
    g, r, c = x.shape
    tr = r if x.size * x.dtype.itemsize <= SMALL_BLOCK_BYTES else _tile(r, (256, 128, 64, 32, 16, 8))

    def body(x_ref, o_ref):
        acc = x_ref[0].astype(F32)
        for i in range(1, g):
            acc = acc + x_ref[i].astype(F32)
        o_ref[...] = acc.astype(o_ref.dtype)

    return pl.pallas_call(
        body, grid=(r // tr,), name=name,
        in_specs=[pl.BlockSpec((g, tr, c), lambda i: (0, i, 0))], out_specs=_rowspec(tr, c),
        out_shape=jax.ShapeDtypeStruct((r, c), out_dtype),
        compiler_params=_cp("parallel"))(x)


HBM_SPEC = pl.BlockSpec(memory_space=pltpu.HBM)
SEM_SPEC = pl.BlockSpec(memory_space=pltpu.SEMAPHORE)
DATAFLOW = pltpu.SideEffectType.DATAFLOW_SIDE_EFFECTING


def _my_rank():
    return 4 * lax.axis_index("x") + 2 * lax.axis_index("y") + lax.axis_index("c")


def _exchange_copies(x_refs, land_refs, send, recv, a2a):
    pos = [lax.axis_index(a) for a in AXES]
    me = _my_rank()
    copies = []
    for x_ref, land_ref, s_ref, r_ref in zip(x_refs, land_refs, send, recv):
        for k in range(N_DEV - 1):
            bits = ((k + 1) >> 2 & 1, (k + 1) >> 1 & 1, (k + 1) & 1)
            peer = tuple(1 - p if b else p for p, b in zip(pos, bits))
            prank = 4 * peer[0] + 2 * peer[1] + peer[2]
            copies.append(pltpu.make_async_remote_copy(
                src_ref=x_ref.at[prank] if a2a else x_ref, dst_ref=land_ref.at[me],
                send_sem=s_ref.at[k], recv_sem=r_ref.at[k], device_id=peer, device_id_type=MESH))
    return copies


def exchange_start(xs, a2a, name, after=None):
    n = len(xs)
    me = _my_rank()
    lands = []
    for x in xs:
        own = lax.dynamic_index_in_dim(x, me, 0, keepdims=True) if a2a else x[None]
        shape = x.shape if a2a else (N_DEV,) + x.shape
        lands.append(lax.dynamic_update_slice(lax.empty(shape, x.dtype), own, (me,) + (0,) * (len(shape) - 1)))

    def body(*refs):
        x_refs, land_refs = refs[:n], refs[n:2 * n]
        outs = refs[len(refs) - 4 * n - 1:]
        for cp in _exchange_copies(x_refs, land_refs, outs[:n], outs[n:2 * n], a2a):
            cp.start()
        token = outs[4 * n]
        token[...] = jnp.zeros_like(token)

    sems = [pltpu.SemaphoreType.DMA((N_DEV - 1,))] * n
    out = pl.pallas_call(
        body, name=name,
        out_shape=tuple(sems + sems + [pltpu.HBM(x.shape, x.dtype) for x in xs]
                        + [pltpu.HBM(l.shape, l.dtype) for l in lands] + [jax.ShapeDtypeStruct((8, LANES), F32)]),
        in_specs=[HBM_SPEC] * (2 * n) + ([] if after is None else [pl.BlockSpec(memory_space=pl.ANY)]),
        out_specs=tuple([SEM_SPEC] * (2 * n) + [HBM_SPEC] * (2 * n) + [pl.BlockSpec(memory_space=pltpu.VMEM)]),
        input_output_aliases={i: 2 * n + i for i in range(2 * n)},
        compiler_params=pltpu.CompilerParams(has_side_effects=DATAFLOW),
    )(*[pltpu.with_memory_space_constraint(a, pltpu.HBM) for a in list(xs) + lands], *([] if after is None else [after]))
    return (out[:n], out[n:2 * n], out[2 * n:3 * n], out[3 * n:4 * n]), out[4 * n]


def exchange_wait(handles, after, a2a, name):
    send, recv, x_thru, land_thru = handles
    n = len(x_thru)

    def body(*refs):
        x_refs, land_refs = refs[:n], refs[n:2 * n]
        s_refs, r_refs = refs[2 * n:3 * n], refs[3 * n:4 * n]
        for cp in _exchange_copies(x_refs, land_refs, s_refs, r_refs, a2a):
            cp.wait_send()
            cp.wait_recv()

    out = pl.pallas_call(
        body, name=name,
        out_shape=tuple([pltpu.HBM(a.shape, a.dtype) for a in list(x_thru) + list(land_thru)]),
        in_specs=[HBM_SPEC] * (2 * n) + [SEM_SPEC] * (2 * n) + [pl.BlockSpec(memory_space=pl.ANY)],
        out_specs=tuple([HBM_SPEC] * (2 * n)),
        input_output_aliases={i: i for i in range(2 * n)},
        compiler_params=pltpu.CompilerParams(has_side_effects=DATAFLOW),
    )(*x_thru, *land_thru, *send, *recv, after)
    return out[n:2 * n]


def _local_step(x, target, weight, emit, P):
    s = x.shape[0]
    tabs = _rope_tables(s)
    sinkb = jnp.broadcast_to(P["ev_sinks"].reshape(N_HEADS, 1), (N_HEADS, LANES))
    sbx = jnp.repeat(P["od_spatial_b"].reshape(D_GROUPS, BLOCK).T, HEAD_DIM, axis=1)
    sw = P["od_spatial_w"].reshape(D_GROUPS, BLOCK, BLOCK)
    fg = P["ffn_norm_g"]
    latest = [None]

    def out(name, layer, grad):
        tok = emit(name, layer, grad)
        if tok is not None:
            latest[0] = tok

    def dep(a):
        return a if latest[0] is None else a + latest[0][0:1, 0:1]

    n0 = rmsnorm(x, P["ev_norm_g"], "rms_in")
    qk_e, v_e, ga, gb = inproj(n0, weight("ev_w_in", 0, n0), tabs, ATT_W + 128,
                               (ATT_W + 128, 128, CONV_CH, CONV_CH), (), "ev_inproj")
    a_e, lse_e = attn_fwd(qk_e, v_e, d=1, hkv=A_KV_HEADS, max_dist=BLOCK - 1, sink=sinkb, out_dtype=ACT,
                          name="ev_attn")
    c_act, c1 = conv_fwd(ga, gb, P["ev_conv_w"], P["ev_conv_b"], P["ev_conv_ln_g"], P["ev_conv_ln_b"], "ev_conv")
    h1, n1 = mm_res([a_e, c_act], weight("ev_w_out", 0, c_act), x, fg[0:1], "ev_outproj")
    gate0, up0, act0 = ffn_up(n1, weight("ffn_w_gate", 0, n1), weight("ffn_w_up", 0, n1), "ffn0_up")
    h2, n2 = mm_res([act0], weight("ffn_w_down", 0, act0), h1, P["od_norm_g"], "ffn0_down")
    fold_dils = tuple(dil for _, dil in DILATED if dil > 1)
    qk_o, v_o, z, *pre = inproj(n2, weight("od_w_in", 0, n2), tabs, 2 * ATT_W, (2 * ATT_W, ATT_W, 2 * D_CH),
                                fold_dils, "od_inproj")
    folded = {dil: (pre[2 * i], pre[2 * i + 1]) for i, dil in enumerate(fold_dils)}
    outs, lses = [], []
    for window, dil in DILATED:
        assert window // dil == BLOCK
        o_r, l_r = attn_fwd(qk_o, v_o, d=dil, hkv=N_HEADS, max_dist=BLOCK, sink=None, out_dtype=ACT,
                            name="od_attn_d%d" % dil, folded=folded.get(dil), keep_folded=True)
        outs.append(o_r)
        lses.append(l_r)
    dils = [dil for _, dil in DILATED]
    c_out, lse_o, *c_folded = combine_fwd(outs, lses, dils, "od_combine")
    d_out = gmlp_fwd(z, P["od_sgu_ln_g"], P["od_sgu_ln_b"], sw, sbx, "od_gmlp")
    h3, n3 = mm_res([c_out, d_out], weight("od_w_out", 0, d_out), h2, fg[1:2], "od_outproj")
    gate1, up1, act1 = ffn_up(n3, weight("ffn_w_gate", 1, n3), weight("ffn_w_up", 1, n3), "ffn1_up")
    h4, _ = mm_res([act1], weight("ffn_w_down", 1, act1), h3, None, "ffn1_down")
    loss_part, dh4, dg_final = final_loss(h4, P["final_norm_g"], target, "loss_head")

    def ffn_bwd(layer, dh_out, h_in, n_in, gate, up, act):
        wg, wu, wd = (weight(n, layer, dh_out) for n in ("ffn_w_gate", "ffn_w_up", "ffn_w_down"))
        tag = "ffn%d" % layer
        dgate, dup = ffn_down_bwd(dh_out, wd, gate, up, tag + "_down_bwd")
        g_wd = mm_tn(act, dh_out, tag + "_dwd")
        dh_in, dgn = mm_nt_rms([(dgate, wg), (dup, wu)], h_in, dep(fg[layer:layer + 1]), dh_out, tag + "_up_bwd")
        out("ffn_w_down", layer, g_wd)
        out("ffn_w_gate", layer, mm_tn(n_in, dgate, tag + "_dwg"))
        out("ffn_w_up", layer, mm_tn(n_in, dup, tag + "_dwu"))
        return dh_in, dgn[0:1]

    dh3, dgn_f1 = ffn_bwd(1, dh4, h3, n3, gate1, up1, act1)

    dcd, *dc_folded = mm_nt(dh3, weight("od_w_out", 0, dh3), "od_outproj_bwd", fold_dils, ATT_W)
    do_o = {dil: (dc_folded[i], c_folded[i]) for i, dil in enumerate(fold_dils)}
    dz, g_sw, g_sbx, g_slg, g_slb = gmlp_bwd(z, dcd, 1, dep(P["od_sgu_ln_g"]), P["od_sgu_ln_b"], sw, sbx,
                                             "od_gmlp_bwd")
    g_sb = jnp.sum(g_sbx.reshape(BLOCK, D_GROUPS, HEAD_DIM), axis=-1).T
    out("od_spatial", 0, jnp.concatenate([g_sw.reshape(D_GROUPS * BLOCK, BLOCK), g_sb], axis=0))
    out("od_w_out", 0, jnp.concatenate([mm_tn(c_out, dh3, "od_dwo_c"), mm_tn(d_out, dh3, "od_dwo_d")], axis=0))
    dqkv = [attn_bwd(qk_o, v_o, dcd if dil == 1 else dcd[:, :ATT_W], 0, c_out, lse_o, d=dil, hkv=N_HEADS,
                     max_dist=BLOCK, sink=None, out_dtype=ACT, name="od_attn_bwd_d%d" % dil,
                     folded=folded.get(dil), folded_do_o=do_o.get(dil), keep_folded=True)
            for window, dil in DILATED]
    dproj_o = assemble([([(b[j], dil) for b, dil in zip(dqkv, dils)], j < 2) for j in range(3)] + [([dz], False)],
                       tabs, "od_dproj")
    dh2, dgn_od = mm_nt_rms([(dproj_o, weight("od_w_in", 0, dproj_o))], h2, dep(P["od_norm_g"]), dh3,
                            "od_inproj_bwd")
    out("od_w_in", 0, mm_tn(n2, dproj_o, "od_dwi"))

    dh1, dgn_f0 = ffn_bwd(0, dh2, h1, n1, gate0, up0, act0)

    dac = mm_nt(dh1, weight("ev_w_out", 0, dh1), "ev_outproj_bwd")
    dc1, g_clg, g_clb = conv_bwd_ln(c1, dac, 1, dep(P["ev_conv_ln_g"]), P["ev_conv_ln_b"], "ev_conv_bwd_ln")
    out("ev_w_out", 0, jnp.concatenate([mm_tn(a_e, dh1, "ev_dwo_a"), mm_tn(c_act, dh1, "ev_dwo_c")], axis=0))
    dga, dgb, g_cw, g_cb = conv_bwd_conv(dc1, ga, gb, P["ev_conv_w"], "ev_conv_bwd")
    dq, dk, dv, dsink = attn_bwd(qk_e, v_e, dac, 0, a_e, lse_e, d=1, hkv=A_KV_HEADS, max_dist=BLOCK - 1,
                                 sink=sinkb, out_dtype=F32, name="ev_attn_bwd")
    dproj_e = assemble([([dq], True), ([dk], True), ([dv], False), ([dga], False), ([dgb], False)], tabs,
                       "ev_dproj")
    out("ev_w_in", 0, mm_tn(n0, dproj_e, "ev_dwi"))
    dx, dgn_ev = mm_nt_rms([(dproj_e, weight("ev_w_in", 0, dproj_e))], x, dep(P["ev_norm_g"]), dh1,
                           "ev_inproj_bwd")

    small = {
        "ev_norm_g": dgn_ev[0:1],
        "ev_sinks": dsink[:, 0:1].reshape(1, N_HEADS),
        "ev_conv_w": jnp.sum(g_cw.reshape(CONV_HALO, SUBLANES, CONV_CH), axis=1)[:CONV_WIDTH],
        "ev_conv_b": jnp.sum(g_cb, axis=0, keepdims=True),
        "ev_conv_ln_g": g_clg[0:1],
        "ev_conv_ln_b": g_clb[0:1],
        "od_norm_g": dgn_od[0:1],
        "od_sgu_ln_g": g_slg[0:1],
        "od_sgu_ln_b": g_slb[0:1],
        "od_spatial_w": g_sw.reshape(D_GROUPS * BLOCK, BLOCK),
        "od_spatial_b": g_sb,
        "ffn_norm_g": jnp.concatenate([dgn_f0, dgn_f1], axis=0),
        "final_norm_g": dg_final[0:1],
    }
    return loss_part, dx, small


BIG = ("ev_w_in", "ev_w_out", "od_w_in", "od_w_out", "ffn_w_gate", "ffn_w_up", "ffn_w_down")
COL_SHARDED = ("ev_w_in", "od_w_in", "ffn_w_gate", "ffn_w_up")
GATHER_GROUPS = (
    (("ev_w_in", 0),),
    (("ev_w_out", 0),),
    (("ffn_w_gate", 0), ("ffn_w_up", 0)),
    (("ffn_w_down", 0),),
    (("od_w_in", 0),),
    (("od_w_out", 0),),
    (("ffn_w_gate", 1), ("ffn_w_up", 1)),
    (("ffn_w_down", 1),),
)
GATHER_EARLY = 4
GATHER_LATE_AT = 2
REDUCE_GROUPS = (
    (("ffn_w_down", 1), ("ffn_w_gate", 1), ("ffn_w_up", 1)),
    (("od_w_out", 0),),
    (("od_w_in", 0),),
    (("ffn_w_down", 0), ("ffn_w_gate", 0), ("ffn_w_up", 0)),
    (("ev_w_out", 0),),
    (("ev_w_in", 0),),
)


def _unshard(name, g):
    if name in COL_SHARDED:
        return jnp.moveaxis(g, 0, 1).reshape(g.shape[1], N_DEV * g.shape[2])
    return g.reshape(N_DEV * g.shape[1], g.shape[2])


def _shard_slots(name, full):
    r, c = full.shape
    if name in COL_SHARDED:
        return jnp.moveaxis(full.reshape(r, N_DEV, c // N_DEV), 1, 0)
    return full.reshape(N_DEV, r // N_DEV, c)


def kernel(x, ev_norm_g, ev_w_in, ev_sinks, ev_conv_w, ev_conv_b, ev_conv_ln_g, ev_conv_ln_b, ev_w_out, od_norm_g, od_w_in, od_sgu_ln_g, od_sgu_ln_b, od_spatial_w, od_spatial_b, od_w_out, ffn_norm_g, ffn_w_gate, ffn_w_up, ffn_w_down, final_norm_g, loss_target, m_ev_norm_g, m_ev_w_in, m_ev_sinks, m_ev_conv_w, m_ev_conv_b, m_ev_conv_ln_g, m_ev_conv_ln_b, m_ev_w_out, m_od_norm_g, m_od_w_in, m_od_sgu_ln_g, m_od_sgu_ln_b, m_od_spatial_w, m_od_spatial_b, m_od_w_out, m_ffn_norm_g, m_ffn_w_gate, m_ffn_w_up, m_ffn_w_down, m_final_norm_g, v_ev_norm_g, v_ev_w_in, v_ev_sinks, v_ev_conv_w, v_ev_conv_b, v_ev_conv_ln_g, v_ev_conv_ln_b, v_ev_w_out, v_od_norm_g, v_od_w_in, v_od_sgu_ln_g, v_od_sgu_ln_b, v_od_spatial_w, v_od_spatial_b, v_od_w_out, v_ffn_norm_g, v_ffn_w_gate, v_ffn_w_up, v_ffn_w_down, v_final_norm_g):
    names = ["ev_norm_g", "ev_w_in", "ev_sinks", "ev_conv_w", "ev_conv_b", "ev_conv_ln_g", "ev_conv_ln_b", "ev_w_out",
             "od_norm_g", "od_w_in", "od_sgu_ln_g", "od_sgu_ln_b", "od_spatial_w", "od_spatial_b", "od_w_out",
             "ffn_norm_g", "ffn_w_gate", "ffn_w_up", "ffn_w_down", "final_norm_g"]
    wts = dict(zip(names, [ev_norm_g, ev_w_in, ev_sinks, ev_conv_w, ev_conv_b, ev_conv_ln_g, ev_conv_ln_b, ev_w_out,
                           od_norm_g, od_w_in, od_sgu_ln_g, od_sgu_ln_b, od_spatial_w, od_spatial_b, od_w_out,
                           ffn_norm_g, ffn_w_gate, ffn_w_up, ffn_w_down, final_norm_g]))
    mom = dict(zip(names, [m_ev_norm_g, m_ev_w_in, m_ev_sinks, m_ev_conv_w, m_ev_conv_b, m_ev_conv_ln_g, m_ev_conv_ln_b,
                           m_ev_w_out, m_od_norm_g, m_od_w_in, m_od_sgu_ln_g, m_od_sgu_ln_b, m_od_spatial_w,
                           m_od_spatial_b, m_od_w_out, m_ffn_norm_g, m_ffn_w_gate, m_ffn_w_up, m_ffn_w_down,
                           m_final_norm_g]))
    vel = dict(zip(names, [v_ev_norm_g, v_ev_w_in, v_ev_sinks, v_ev_conv_w, v_ev_conv_b, v_ev_conv_ln_g, v_ev_conv_ln_b,
                           v_ev_w_out, v_od_norm_g, v_od_w_in, v_od_sgu_ln_g, v_od_sgu_ln_b, v_od_spatial_w,
                           v_od_spatial_b, v_od_w_out, v_ffn_norm_g, v_ffn_w_gate, v_ffn_w_up, v_ffn_w_down,
                           v_final_norm_g]))
    me = _my_rank()

    sp = jnp.zeros((40, LANES), F32)
    sp = sp.at[0:CONV_WIDTH, 0:64].set(ev_conv_w[0])
    sp = sp.at[32, :].set(od_norm_g[0])
    sp = sp.at[33, 0:64].set(od_sgu_ln_g[0])
    sp = sp.at[34, 0:64].set(od_sgu_ln_b[0])

    early = [k for grp in GATHER_GROUPS[:GATHER_EARLY] for k in grp]
    late = [k for grp in GATHER_GROUPS[GATHER_EARLY:] for k in grp]
    early_act = dict(zip(early, cast_layers([(wts[n], l) for n, l in early], "cast_early")))
    late_act = dict(zip(late, cast_layers([(wts[n], l) for n, l in late], "cast_late")))
    ag_early, ag_token = exchange_start([sp] + [early_act[k] for k in early], False, "ag_start")
    ag_late = []
    full_w = {}
    P = {
        "ev_norm_g": ev_norm_g + ag_token[0:1, 0:1], "ev_sinks": ev_sinks, "ev_conv_b": ev_conv_b,
        "ev_conv_ln_g": ev_conv_ln_g, "ev_conv_ln_b": ev_conv_ln_b,
        "od_spatial_w": od_spatial_w, "od_spatial_b": od_spatial_b, "ffn_norm_g": ffn_norm_g,
        "final_norm_g": final_norm_g.reshape(1, D_MODEL),
    }

    def weight(name, layer, after):
        if (name, layer) not in full_w:
            gi = [i for i, grp in enumerate(GATHER_GROUPS) if (name, layer) in grp][0]
            if gi < GATHER_EARLY:
                idx = [1 + early.index(k) for k in GATHER_GROUPS[gi]]
                handles = ag_early
            else:
                idx = [late.index(k) for k in GATHER_GROUPS[gi]]
                handles = ag_late[0]
            if gi == 0:
                idx = [0] + idx
            lands = exchange_wait(tuple([h[i] for i in idx] for h in handles), after, False, "ag_wait%d" % gi)
            if gi == 0:
                spg, lands = lands[0], lands[1:]
                P["ev_conv_w"] = jnp.moveaxis(spg[:, 0:CONV_WIDTH, 0:64], 0, 1).reshape(CONV_WIDTH, CONV_CH)
                P["od_norm_g"] = spg[:, 32, :].reshape(1, D_MODEL)
                P["od_sgu_ln_g"] = spg[:, 33, 0:64].reshape(1, D_CH)
                P["od_sgu_ln_b"] = spg[:, 34, 0:64].reshape(1, D_CH)
            if gi == GATHER_LATE_AT:
                ag_late.append(exchange_start([late_act[k] for k in late], False, "ag_start_late", after=lands[0])[0])
            for k, land in zip(GATHER_GROUPS[gi], lands):
                full_w[k] = _unshard(k[0], land)
        return full_w[(name, layer)]

    pending, rs_started, spatial = {}, [], []

    def emit(name, layer, grad):
        if name == "od_spatial":
            handles, token = exchange_start([grad], False, "ar_start_b")
            spatial.append(handles)
            return token
        pending[(name, layer)] = grad
        for gi, grp in enumerate(REDUCE_GROUPS):
            if (name, layer) in grp and all(k in pending for k in grp):
                handles, token = exchange_start([_shard_slots(k[0], pending[k]).astype(ACT) for k in grp], True,
                                                "rs_start%d" % gi)
                rs_started.append((gi, handles))
                return token
        return None

    loss_part, dx, small = _local_step(x[0], loss_target[0], weight, emit, P)
    loss = lax.psum(loss_part[0, 0], AXES)

    wide = ["ev_norm_g", "ev_sinks", "ev_conv_w", "ev_conv_b", "ev_conv_ln_g", "ev_conv_ln_b", "od_norm_g",
            "od_sgu_ln_g", "od_sgu_ln_b", "ffn_norm_g", "final_norm_g"]
    blk_a = jnp.concatenate(
        [jnp.pad(small[n], ((0, 0), (0, D_MODEL - small[n].shape[1]))) for n in wide], axis=0)
    blk_a = jnp.pad(blk_a, ((0, 48 - blk_a.shape[0]), (0, 0)))
    ar_a, ar_token = exchange_start([blk_a], False, "ar_start_a")

    results = {}
    for gi, handles in rs_started:
        lands = exchange_wait(handles, ar_token, True, "rs_wait%d" % gi)
        for (n, l), land in zip(REDUCE_GROUPS[gi], lands):
            results[n] = sum_adamw(land, wts[n], mom[n], vel[n], l, results.get(n), "adamw_%s%d" % (n, l))
    out_g, out_d, out_m, out_v = {}, {}, {}, {}
    for n in BIG:
        out_g[n], out_d[n], out_m[n], out_v[n] = results[n]

    last = results[REDUCE_GROUPS[-1][-1][0]][0]
    sum_b = sum_slots(exchange_wait(spatial[0], last, False, "ar_wait_b")[0], F32, "ar_sum_b")
    sum_a = sum_slots(exchange_wait(ar_a, last, False, "ar_wait_a")[0], F32, "ar_sum_a")
    full = {}
    off = 0
    for n in wide:
        r_, c_ = small[n].shape
        full[n] = sum_a[off:off + r_, 0:c_]
        off += r_
    full["od_spatial_w"] = sum_b[0:D_GROUPS * BLOCK]
    full["od_spatial_b"] = sum_b[D_GROUPS * BLOCK:D_GROUPS * BLOCK + D_GROUPS]
    full["ev_conv_w"] = lax.dynamic_slice_in_dim(full["ev_conv_w"], me * 64, 64, axis=1)
    full["od_norm_g"] = lax.dynamic_slice_in_dim(full["od_norm_g"], me * 128, 128, axis=1)
    full["od_sgu_ln_g"] = lax.dynamic_slice_in_dim(full["od_sgu_ln_g"], me * 64, 64, axis=1)
    full["od_sgu_ln_b"] = lax.dynamic_slice_in_dim(full["od_sgu_ln_b"], me * 64, 64, axis=1)

    small_names = [n for n in names if n not in BIG]
    view = {n: ((-1, wts[n].shape[-1]) if wts[n].ndim > 1 else (1, -1)) for n in small_names}
    ds, ms, vs = adamw_small([wts[n].reshape(view[n]) for n in small_names],
                             [full[n].reshape(view[n]) for n in small_names],
                             [mom[n].reshape(view[n]) for n in small_names],
                             [vel[n].reshape(view[n]) for n in small_names], "adamw_small")
    for i, n in enumerate(small_names):
        shp = wts[n].shape
        out_g[n], out_d[n], out_m[n], out_v[n] = (full[n].reshape(shp), ds[i].reshape(shp), ms[i].reshape(shp),
                                                  vs[i].reshape(shp))

    return (loss, dx[None], *[out_g[n] for n in names], *[out_d[n] for n in names],
            *[out_m[n] for n in names], *[out_v[n] for n in names])
```

```python
import functools
import math

import jax
import jax.numpy as jnp
from jax import lax
from jax.experimental import pallas as pl
from jax.experimental.pallas import tpu as pltpu

F32 = jnp.float32
ACT = jnp.bfloat16

D_MODEL = 1024
HEAD_DIM = 64
N_HEADS = 8
ATT_W = N_HEADS * HEAD_DIM
A_KV_HEADS = 2
CONV_CH = 512
CONV_WIDTH = 31
CONV_HALO = 32
D_CH = 512
D_GROUPS = 8
BLOCK = 128
D_FF = 2816
ROT_DIM = 16
ROPE_THETA = 500000.0
RMS_EPS = 1e-6
LN_EPS = 1e-5
DILATED = ((128, 1), (512, 4), (2048, 16))
NEG = -1e30
LANES = 128

ADAM_LR = 0.001
ADAM_B1 = 0.9
ADAM_B2 = 0.999
ADAM_EPS = 1e-08
ADAM_WD = 0.01
ADAM_STEP = 10

V7X_VMEM_LIMIT = 56 * 1024 * 1024
SMALL_BLOCK_BYTES = 6 * 1024 * 1024
N_DEV = 8

NN = (((1,), (0,)), ((), ()))
NT = (((1,), (1,)), ((), ()))
TN = (((0,), (0,)), ((), ()))
MESH = pl.DeviceIdType.MESH
AXES = ("x", "y", "c")


def _dot(a, b, dims):
    return lax.dot_general(a, b, dims, preferred_element_type=F32)


def _cp(*sem):
    return pltpu.CompilerParams(dimension_semantics=sem if sem else None,
                                vmem_limit_bytes=V7X_VMEM_LIMIT)


def _tile(n, prefs):
    for p in prefs:
        if n % p == 0:
            return p
    return n


def _sigmoid(x):
    return 1.0 / (1.0 + jnp.exp(-x))


def _rowspec(tm, w, col=0):
    return pl.BlockSpec((tm, w), lambda i, col=col: (i, col))


def _fullspec(shape):
    nd = len(shape)
    return pl.BlockSpec(shape, lambda *a, nd=nd: (0,) * nd)


def _rope_tables(seq):
    half = ROT_DIM // 2
    inv_freq = ROPE_THETA ** (-jnp.arange(half, dtype=F32) * (2.0 / ROT_DIM))
    ang = jnp.arange(seq, dtype=jnp.int32).astype(F32)[:, None] * inv_freq[None, :]
    cos, sin = jnp.cos(ang), jnp.sin(ang)
    lane = jnp.arange(LANES)
    jm = lane % HEAD_DIM
    idx = jm % half
    c = jnp.where(jm[None, :] < ROT_DIM, cos[:, idx], 1.0)
    sa = jnp.where(jm[None, :] < half, -sin[:, idx], 0.0)
    sb = jnp.where((jm[None, :] >= half) & (jm[None, :] < ROT_DIM), sin[:, idx], 0.0)
    return c.astype(F32), sa.astype(F32), sb.astype(F32)


def _rope(x, c, sa, sb):
    return x * c + pltpu.roll(x, LANES - 8, 1) * sa + pltpu.roll(x, 8, 1) * sb


def _rope_t(d, c, sa, sb):
    return d * c + pltpu.roll(d * sa, 8, 1) + pltpu.roll(d * sb, LANES - 8, 1)


def rmsnorm(h, g, name):
    s = h.shape[0]
    tm = _tile(s, (1024, 512))

    def body(h_ref, g_ref, o_ref):
        x = h_ref[...]
        r = lax.rsqrt(jnp.mean(x * x, axis=-1, keepdims=True) + RMS_EPS)
        o_ref[...] = (x * r * g_ref[...]).astype(o_ref.dtype)

    return pl.pallas_call(
        body, grid=(s // tm,), name=name,
        in_specs=[_rowspec(tm, D_MODEL), _fullspec((1, D_MODEL))],
        out_specs=_rowspec(tm, D_MODEL),
        out_shape=jax.ShapeDtypeStruct((s, D_MODEL), ACT),
        compiler_params=_cp("parallel"))(h, g)


def _fold_perm(tm, d, inverse, cache):
    key = (tm, d, inverse)
    if key not in cache:
        m = tm // d
        a = lax.broadcasted_iota(jnp.int32, (tm, tm), 1 if inverse else 0)
        b = lax.broadcasted_iota(jnp.int32, (tm, tm), 0 if inverse else 1)
        src = (a & (m - 1)) * d + (a >> (m.bit_length() - 1))
        cache[key] = (b == src).astype(ACT)
    return cache[key]


FOLD_ROWS = 256


def _fold_rows(x, d, cache):
    m = FOLD_ROWS // d
    out = []
    for r0 in range(0, x.shape[0], FOLD_ROWS):
        p = _dot(_fold_perm(FOLD_ROWS, d, False, cache), x[r0:r0 + FOLD_ROWS], NN).astype(ACT)
        out.append(jnp.concatenate([p[r * m:(r + 1) * m] for r in range(d)], axis=1))
    return out[0] if len(out) == 1 else jnp.concatenate(out, axis=0)


def _unfold_rows(blk, d, cache):
    w = blk.shape[1] // d
    m = FOLD_ROWS // d
    out = []
    for r0 in range(0, blk.shape[0], m):
        stacked = jnp.concatenate([blk[r0:r0 + m, r * w:(r + 1) * w] for r in range(d)], axis=0)
        out.append(_dot(_fold_perm(FOLD_ROWS, d, True, cache), stacked, NN))
    return out[0] if len(out) == 1 else jnp.concatenate(out, axis=0)


def _unfold_rows_f32(blk, d, cache):
    hi = blk.astype(ACT)
    r1 = blk - hi.astype(F32)
    mid = r1.astype(ACT)
    lo = (r1 - mid.astype(F32)).astype(ACT)
    return _unfold_rows(hi, d, cache) + _unfold_rows(mid, d, cache) + _unfold_rows(lo, d, cache)


def inproj(n, w, tabs, nqk, splits, fold_dils, name):
    s = n.shape[0]
    ntot = w.shape[1]
    assert sum(splits) == ntot and splits[0] == nqk
    tm = _tile(s, (512,))
    ns = len(splits)

    def body(n_ref, w_ref, c_ref, sa_ref, sb_ref, *outs):
        res = _dot(n_ref[...], w_ref[...], NN)
        c, sa, sb = c_ref[...], sa_ref[...], sb_ref[...]
        for g in range(nqk // LANES):
            x = res[:, g * LANES:(g + 1) * LANES]
            outs[0][:, g * LANES:(g + 1) * LANES] = _rope(x, c, sa, sb).astype(ACT)
        off = nqk
        for o_ref, wd in zip(outs[1:ns], splits[1:]):
            o_ref[...] = res[:, off:off + wd].astype(ACT)
            off += wd
        cache = {}
        for i, d in enumerate(fold_dils):
            outs[ns + 2 * i][...] = _fold_rows(outs[0][...], d, cache)
            outs[ns + 2 * i + 1][...] = _fold_rows(outs[1][...], d, cache)

    out_specs = [_rowspec(tm, wd) for wd in splits]
    out_shape = [jax.ShapeDtypeStruct((s, wd), ACT) for wd in splits]
    for d in fold_dils:
        for wd in splits[:2]:
            out_specs.append(_rowspec(tm // d, d * wd))
            out_shape.append(jax.ShapeDtypeStruct((s // d, d * wd), ACT))
    return pl.pallas_call(
        body, grid=(s // tm,), name=name,
        in_specs=[_rowspec(tm, D_MODEL), _fullspec((D_MODEL, ntot))] + [_rowspec(tm, LANES)] * 3,
        out_specs=out_specs, out_shape=out_shape,
        compiler_params=_cp("parallel"))(n, w, *tabs)


def ffn_up(n, wg, wu, name):
    s = n.shape[0]
    f = wg.shape[1]
    tm = _tile(s, (1024, 512))
    tf = _tile(f, (1408, 512, 256, 128))

    def body(n_ref, wg_ref, wu_ref, g_ref, u_ref, a_ref):
        a = n_ref[...]
        g = _dot(a, wg_ref[...], NN)
        u = _dot(a, wu_ref[...], NN)
        sg = _sigmoid(g)
        silu = g * sg
        g_ref[...] = (u * (sg * (1.0 + g * (1.0 - sg)))).astype(ACT)
        u_ref[...] = silu.astype(ACT)
        a_ref[...] = (silu * u).astype(ACT)

    wspec = pl.BlockSpec((D_MODEL, tf), lambda j, i: (0, j))
    ospec = pl.BlockSpec((tm, tf), lambda j, i: (i, j))
    return pl.pallas_call(
        body, grid=(f // tf, s // tm), name=name,
        in_specs=[pl.BlockSpec((tm, D_MODEL), lambda j, i: (i, 0)), wspec, wspec],
        out_specs=[ospec] * 3,
        out_shape=[jax.ShapeDtypeStruct((s, f), ACT)] * 3,
        compiler_params=_cp("parallel", "parallel"))(n, wg, wu)


def mm_res(parts, w, h, gnext, name):
    s = h.shape[0]
    tm = _tile(s, (1024, 512))
    widths = [p.shape[1] for p in parts]
    assert sum(widths) == w.shape[0]
    np_ = len(parts)

    def body(*refs):
        p_refs = refs[:np_]
        w_ref, h_ref = refs[np_], refs[np_ + 1]
        rest = refs[np_ + 2:]
        acc = h_ref[...]
        off = 0
        for p_ref, wd in zip(p_refs, widths):
            acc = acc + _dot(p_ref[...], w_ref[off:off + wd, :], NN)
            off += wd
        if gnext is None:
            rest[0][...] = acc
        else:
            g_ref, ho_ref, no_ref = rest
            ho_ref[...] = acc
            r = lax.rsqrt(jnp.mean(acc * acc, axis=-1, keepdims=True) + RMS_EPS)
            no_ref[...] = (acc * r * g_ref[...]).astype(ACT)

    wspec = pl.BlockSpec(w.shape, lambda i: (0, 0), pipeline_mode=pl.Buffered(1))
    in_specs = [_rowspec(tm, wd) for wd in widths] + [wspec, _rowspec(tm, D_MODEL)]
    args = list(parts) + [w, h]
    out_specs = [_rowspec(tm, D_MODEL)]
    out_shape = [jax.ShapeDtypeStruct((s, D_MODEL), F32)]
    if gnext is not None:
        in_specs.append(_fullspec((1, D_MODEL)))
        args.append(gnext)
        out_specs.append(_rowspec(tm, D_MODEL))
        out_shape.append(jax.ShapeDtypeStruct((s, D_MODEL), ACT))
    out = pl.pallas_call(
        body, grid=(s // tm,), name=name, in_specs=in_specs, out_specs=out_specs,
        out_shape=out_shape, compiler_params=_cp("parallel"))(*args)
    return (out[0], None) if gnext is None else (out[0], out[1])


def _band_mask(n, max_dist):
    qi = lax.broadcasted_iota(jnp.int32, (BLOCK, 2 * BLOCK), 0)
    kj = lax.broadcasted_iota(jnp.int32, (BLOCK, 2 * BLOCK), 1)
    dist = qi + BLOCK - kj
    valid = jnp.logical_and(dist >= 0, dist <= max_dist)
    return jnp.logical_and(valid, jnp.logical_or(kj >= BLOCK, n > 0))


def _band_mask_t(n, max_dist):
    kj = lax.broadcasted_iota(jnp.int32, (2 * BLOCK, BLOCK), 0)
    qi = lax.broadcasted_iota(jnp.int32, (2 * BLOCK, BLOCK), 1)
    dist = qi + BLOCK - kj
    valid = jnp.logical_and(dist >= 0, dist <= max_dist)
    return jnp.logical_and(valid, jnp.logical_or(kj >= BLOCK, n > 0))


ATT_FWD_BLOCKS = 4


def _fold(a, d):
    return a.reshape(a.shape[0] // d, d * a.shape[1])


def attn_fwd(qk, v, *, d, hkv, max_dist, sink, out_dtype, name, folded=None, keep_folded=False):
    s = qk.shape[0]
    kvw = hkv * HEAD_DIM
    wqk = ATT_W + kvw
    assert qk.shape[1] == wqk and (d == 1 or (wqk % ATT_W == 0 and wqk % kvw == 0))
    assert sink is None or max_dist == BLOCK - 1
    nb = s // d // BLOCK
    grp = N_HEADS // hkv
    qpb, kpb, koff = wqk // ATT_W, wqk // kvw, ATT_W // kvw

    qb = ATT_FWD_BLOCKS if nb % ATT_FWD_BLOCKS == 0 else 1

    def body(*refs):
        if sink is None:
            q_ref, kc_ref, kp_ref, vc_ref, vp_ref, o_ref, l_ref, o_buf = refs
        else:
            q_ref, kc_ref, kp_ref, vc_ref, vp_ref, s_ref, o_ref, l_ref, o_buf = refs
        n = pl.program_id(1)
        qsl = [slice(h * HEAD_DIM, (h + 1) * HEAD_DIM) for h in range(N_HEADS)]
        ksl = [slice((h // grp) * HEAD_DIM, (h // grp + 1) * HEAD_DIM) for h in range(N_HEADS)]
        head_row = lax.broadcasted_iota(jnp.int32, (SUBLANES, BLOCK), 0)
        if sink is not None:
            sink_row = lax.broadcasted_iota(jnp.int32, (2 * BLOCK, BLOCK), 0) == 0
        for j in range(qb):
            rows = slice(j * BLOCK, (j + 1) * BLOCK)
            before = slice((j - 1) * BLOCK, j * BLOCK)
            valid = _band_mask_t(n * qb + j, max_dist)
            kk = jnp.concatenate([kp_ref[...] if j == 0 else kc_ref[before, :], kc_ref[rows, :]], axis=0)
            vv = jnp.concatenate([vp_ref[...] if j == 0 else vc_ref[before, :], vc_ref[rows, :]], axis=0)
            scores = []
            for h in range(N_HEADS):
                q = q_ref[rows, qsl[h]] * 0.125
                scores.append(_dot(kk[:, ksl[h]], q, NT))
            probs = []
            lse8 = jnp.zeros((SUBLANES, BLOCK), F32)
            for h in range(N_HEADS):
                sc = jnp.where(valid, scores[h], NEG)
                if sink is not None:
                    sc = jnp.where(sink_row, s_ref[h:h + 1, 0:1], sc)
                m = jnp.max(sc, axis=0, keepdims=True)
                p = jnp.exp(sc - m)
                l = jnp.sum(p, axis=0, keepdims=True)
                if sink is not None:
                    p = jnp.where(sink_row, 0.0, p)
                lse8 = jnp.where(head_row == h, m + jnp.log(l), lse8)
                probs.append((p * (1.0 / l)).astype(ACT))
            l_ref[rows, :] = jnp.concatenate([lse8, jnp.zeros((BLOCK - SUBLANES, BLOCK), F32)], axis=0).T
            for h in range(N_HEADS):
                o_buf[rows, qsl[h]] = _dot(probs[h], vv[:, ksl[h]], TN)
        o_ref[...] = o_buf[...].astype(o_ref.dtype)

    prev = lambda n: jnp.maximum(n * qb - 1, 0)
    in_specs = [
        pl.BlockSpec((qb * BLOCK, ATT_W), lambda r, n: (n, r * qpb)),
        pl.BlockSpec((qb * BLOCK, kvw), lambda r, n: (n, r * kpb + koff)),
        pl.BlockSpec((BLOCK, kvw), lambda r, n: (prev(n), r * kpb + koff)),
        pl.BlockSpec((qb * BLOCK, kvw), lambda r, n: (n, r)),
        pl.BlockSpec((BLOCK, kvw), lambda r, n: (prev(n), r)),
    ]
    qkf, vf = (_fold(qk, d), _fold(v, d)) if folded is None else folded
    args = [qkf, qkf, qkf, vf, vf]
    if sink is not None:
        in_specs.append(_fullspec((N_HEADS, LANES)))
        args.append(sink)
    ospec = pl.BlockSpec((qb * BLOCK, ATT_W), lambda r, n: (n, r))
    lspec = pl.BlockSpec((qb * BLOCK, LANES), lambda r, n: (n, r))
    o, lse = pl.pallas_call(
        body, grid=(d, nb // qb), name=name, in_specs=in_specs, out_specs=[ospec, lspec],
        out_shape=[jax.ShapeDtypeStruct((s // d, d * ATT_W), out_dtype),
                   jax.ShapeDtypeStruct((s // d, d * LANES), F32)],
        scratch_shapes=[pltpu.VMEM((qb * BLOCK, ATT_W), F32)],
        compiler_params=_cp("parallel", "parallel"))(*args)
    return (o, lse) if keep_folded else (o.reshape(s, ATT_W), lse.reshape(s, LANES))


def attn_bwd(qk, v, do_src, do_col, o, lse, *, d, hkv, max_dist, sink, out_dtype, name, folded=None,
             folded_do_o=None, keep_folded=False):
    s = qk.shape[0]
    kvw = hkv * HEAD_DIM
    wqk = ATT_W + kvw
    nb = s // d // BLOCK
    grp = N_HEADS // hkv
    qpb, kpb, koff = wqk // ATT_W, wqk // kvw, ATT_W // kvw
    dob = do_src.shape[1] // ATT_W
    has_sink = sink is not None

    def body(*refs):
        refs = list(refs)
        q_ref, kc_ref, kp_ref, vc_ref, vp_ref, do_ref, o_ref, l_ref = refs[:8]
        pos = 8
        if has_sink:
            s_ref = refs[pos]
            pos += 1
        dq_ref, dk_ref, dv_ref = refs[pos:pos + 3]
        pos += 3
        if has_sink:
            ds_ref = refs[pos]
            pos += 1
        ck_ref, cv_ref, dq_buf, dk_buf, dv_buf = refs[pos:pos + 5]
        r_id = pl.program_id(0)
        n = pl.program_id(1)

        @pl.when(n == 0)
        def _():
            ck_ref[...] = jnp.zeros_like(ck_ref)
            cv_ref[...] = jnp.zeros_like(cv_ref)

        if has_sink:
            @pl.when(jnp.logical_and(n == 0, r_id == 0))
            def _():
                ds_ref[...] = jnp.zeros_like(ds_ref)

        @pl.when(n < nb)
        def _():
            valid = _band_mask_t(n, max_dist)
            qsl = [slice(h * HEAD_DIM, (h + 1) * HEAD_DIM) for h in range(N_HEADS)]
            ksl = [slice((h // grp) * HEAD_DIM, (h // grp + 1) * HEAD_DIM) for h in range(N_HEADS)]
            kk = jnp.concatenate([kp_ref[...], kc_ref[...]], axis=0)
            vv = jnp.concatenate([vp_ref[...], vc_ref[...]], axis=0)
            qs, first = [], []
            for h in range(N_HEADS):
                q = q_ref[:, qsl[h]] * 0.125
                qs.append(q)
                first.append((_dot(kk[:, ksl[h]], q, NT), _dot(vv[:, ksl[h]], do_ref[:, qsl[h]], NT)))
            lse_t = l_ref[...].T
            prod = do_ref[...].astype(F32) * o_ref[...].astype(F32)
            hi = prod.astype(ACT)
            lo = (prod - hi.astype(F32)).astype(ACT)
            col = lax.broadcasted_iota(jnp.int32, (LANES, ATT_W), 1)
            row = lax.broadcasted_iota(jnp.int32, (LANES, ATT_W), 0)
            head_of = jnp.logical_and(col >= row * HEAD_DIM, col < (row + 1) * HEAD_DIM).astype(ACT)
            e_t = _dot(head_of, hi, NT) + _dot(head_of, lo, NT)
            mid = []
            for h in range(N_HEADS):
                s_t, dp_t = first[h]
                lse_h, e_h = lse_t[h:h + 1, :], e_t[h:h + 1, :]
                p_t = jnp.exp(jnp.where(valid, s_t, NEG) - lse_h)
                mid.append(((p_t * (dp_t - e_h)).astype(ACT), p_t.astype(ACT)))
                if has_sink:
                    sk = s_ref[h:h + 1, 0:1]
                    dsk = -jnp.sum(jnp.exp(sk - lse_h) * e_h, axis=1, keepdims=True)
                    ds_ref[h:h + 1, :] = ds_ref[h:h + 1, :] + dsk
            dkk = [None] * hkv
            dvv = [None] * hkv
            for h in range(N_HEADS):
                kh = h // grp
                ds_t, p_t = mid[h]
                dq_buf[:, qsl[h]] = _dot(ds_t, kk[:, ksl[h]], TN) * 0.125
                for lst, val in ((dkk, _dot(ds_t, qs[h], NN)), (dvv, _dot(p_t, do_ref[:, qsl[h]], NN))):
                    lst[kh] = val if lst[kh] is None else lst[kh] + val
            for kh in range(hkv):
                ks = slice(kh * HEAD_DIM, (kh + 1) * HEAD_DIM)
                dk_buf[:, ks] = ck_ref[:, ks] + dkk[kh][:BLOCK]
                dv_buf[:, ks] = cv_ref[:, ks] + dvv[kh][:BLOCK]
                ck_ref[:, ks] = dkk[kh][BLOCK:]
                cv_ref[:, ks] = dvv[kh][BLOCK:]
            dq_ref[...] = dq_buf[...].astype(dq_ref.dtype)
            dk_ref[...] = dk_buf[...].astype(dk_ref.dtype)
            dv_ref[...] = dv_buf[...].astype(dv_ref.dtype)

        @pl.when(n == nb)
        def _():
            dk_ref[...] = ck_ref[...].astype(dk_ref.dtype)
            dv_ref[...] = cv_ref[...].astype(dv_ref.dtype)

    qrow = lambda n: jnp.minimum(n, nb - 1)
    prow = lambda n: jnp.maximum(jnp.minimum(n, nb - 1) - 1, 0)
    krow = lambda n: jnp.maximum(n - 1, 0)
    in_specs = [
        pl.BlockSpec((BLOCK, ATT_W), lambda r, n: (qrow(n), r * qpb)),
        pl.BlockSpec((BLOCK, kvw), lambda r, n: (qrow(n), r * kpb + koff)),
        pl.BlockSpec((BLOCK, kvw), lambda r, n: (prow(n), r * kpb + koff)),
        pl.BlockSpec((BLOCK, kvw), lambda r, n: (qrow(n), r)),
        pl.BlockSpec((BLOCK, kvw), lambda r, n: (prow(n), r)),
        pl.BlockSpec((BLOCK, ATT_W), lambda r, n: (qrow(n), r * dob + do_col)),
        pl.BlockSpec((BLOCK, ATT_W), lambda r, n: (qrow(n), r)),
        pl.BlockSpec((BLOCK, LANES), lambda r, n: (qrow(n), r)),
    ]
    qkf, vf = (_fold(qk, d), _fold(v, d)) if folded is None else folded
    dof, of = (_fold(do_src, d), _fold(o, d)) if folded_do_o is None else folded_do_o
    args = [qkf, qkf, qkf, vf, vf, dof, of, _fold(lse, d)]
    if has_sink:
        in_specs.append(_fullspec((N_HEADS, LANES)))
        args.append(sink)
    qspec = pl.BlockSpec((BLOCK, ATT_W), lambda r, n: (qrow(n), r))
    kspec = pl.BlockSpec((BLOCK, kvw), lambda r, n: (krow(n), r))
    out_specs = [qspec, kspec, kspec]
    out_shape = [jax.ShapeDtypeStruct((s // d, d * ATT_W), out_dtype),
                 jax.ShapeDtypeStruct((s // d, d * kvw), out_dtype),
                 jax.ShapeDtypeStruct((s // d, d * kvw), out_dtype)]
    if has_sink:
        out_specs.append(_fullspec((N_HEADS, LANES)))
        out_shape.append(jax.ShapeDtypeStruct((N_HEADS, LANES), F32))
    out = pl.pallas_call(
        body, grid=(d, nb + 1), name=name, in_specs=in_specs, out_specs=out_specs,
        out_shape=out_shape,
        scratch_shapes=[pltpu.VMEM((BLOCK, kvw), F32), pltpu.VMEM((BLOCK, kvw), F32),
                        pltpu.VMEM((BLOCK, ATT_W), F32), pltpu.VMEM((BLOCK, kvw), F32), pltpu.VMEM((BLOCK, kvw), F32)],
        compiler_params=_cp("arbitrary", "arbitrary"))(*args)
    res = list(out[:3]) if keep_folded else [out[0].reshape(s, ATT_W), out[1].reshape(s, kvw), out[2].reshape(s, kvw)]
    if has_sink:
        res.append(out[3])
    return res


def combine_fwd(os_, lses, dils, name):
    s = os_[0].shape[0] * dils[0]
    tm = _tile(s, (512,))
    fold_dils = [d for d in dils if d > 1]

    def body(o1, o2, o3, l1, l2, l3, c_ref, l_ref, *rest):
        c_buf = rest[-1]
        cache = {}
        o = [r[...].astype(F32) if d == 1 else _unfold_rows(r[...], d, cache) for r, d in zip((o1, o2, o3), dils)]
        a, b, c = [r[...] if d == 1 else _unfold_rows_f32(r[...], d, cache) for r, d in zip((l1, l2, l3), dils)]
        m = jnp.maximum(jnp.maximum(a, b), c)
        wa, wb, wc = jnp.exp(a - m), jnp.exp(b - m), jnp.exp(c - m)
        tot = wa + wb + wc
        l_ref[...] = m + jnp.log(tot)
        rt = 1.0 / tot
        wa, wb, wc = wa * rt, wb * rt, wc * rt
        for h in range(N_HEADS):
            cs = slice(h * HEAD_DIM, (h + 1) * HEAD_DIM)
            c_buf[:, cs] = (wa[:, h:h + 1] * o[0][:, cs] + wb[:, h:h + 1] * o[1][:, cs] + wc[:, h:h + 1] * o[2][:, cs])
        mix = c_buf[...].astype(ACT)
        c_ref[...] = mix
        for f_ref, d in zip(rest[:-1], fold_dils):
            f_ref[...] = _fold_rows(mix, d, cache)

    return pl.pallas_call(
        body, grid=(s // tm,), name=name,
        in_specs=[_rowspec(tm // d, d * ATT_W) for d in dils] + [_rowspec(tm // d, d * LANES) for d in dils],
        out_specs=[_rowspec(tm, ATT_W), _rowspec(tm, LANES)] + [_rowspec(tm // d, d * ATT_W) for d in fold_dils],
        out_shape=[jax.ShapeDtypeStruct((s, ATT_W), ACT), jax.ShapeDtypeStruct((s, LANES), F32)]
        + [jax.ShapeDtypeStruct((s // d, d * ATT_W), ACT) for d in fold_dils],
        scratch_shapes=[pltpu.VMEM((tm, ATT_W), F32)],
        compiler_params=_cp("parallel"))(*os_, *lses)


def assemble(parts, tabs, name):
    terms_of = [[t if isinstance(t, tuple) else (t, 1) for t in terms] for terms, _ in parts]
    flat = [t for ts in terms_of for t in ts]
    s = terms_of[0][0][0].shape[0] * terms_of[0][0][1]
    tm = _tile(s, (512,))
    widths = [ts[0][0].shape[1] // ts[0][1] for ts in terms_of]
    flags = [f for _, f in parts]

    def body(*refs):
        c_ref, sa_ref, sb_ref, o_ref = refs[len(flat):]
        c, sa, sb = c_ref[...], sa_ref[...], sb_ref[...]
        off = 0
        first = 0
        cache = {}
        for wd, ts, fl in zip(widths, terms_of, flags):
            t_refs = refs[first:first + len(ts)]
            first += len(ts)
            x = None
            for t_ref, (_, d) in zip(t_refs, ts):
                t = t_ref[...].astype(F32) if d == 1 else _unfold_rows(t_ref[...], d, cache)
                x = t if x is None else x + t
            for g in range(wd // LANES):
                cols = slice(g * LANES, (g + 1) * LANES)
                y = _rope_t(x[:, cols], c, sa, sb) if fl else x[:, cols]
                o_ref[:, off + g * LANES:off + (g + 1) * LANES] = y.astype(ACT)
            off += wd

    tot = sum(widths)
    return pl.pallas_call(
        body, grid=(s // tm,), name=name,
        in_specs=[_rowspec(tm // d, a.shape[1]) for a, d in flat] + [_rowspec(tm, LANES)] * 3,
        out_specs=_rowspec(tm, tot), out_shape=jax.ShapeDtypeStruct((s, tot), ACT),
        compiler_params=_cp("parallel"))(*[a for a, _ in flat], *tabs)


def _ln_stats(x):
    mu = jnp.mean(x, axis=-1, keepdims=True)
    xc = x - mu
    var = jnp.mean(xc * xc, axis=-1, keepdims=True)
    rstd = lax.rsqrt(var + LN_EPS)
    return xc * rstd, rstd


SUBLANES = 8


TAP_ROWS = 32


def _tap_sum(buf, cw_ref, offsets, tm, res_ref):
    for r0 in range(0, tm, TAP_ROWS):
        acc = None
        for ph in range(SUBLANES):
            taps = [j for j, off in enumerate(offsets) if off % SUBLANES == ph]
            if not taps:
                continue
            rows = TAP_ROWS if ph == 0 else TAP_ROWS + SUBLANES
            part = None
            for j in taps:
                term = cw_ref[j:j + 1, :] * buf[pl.ds(offsets[j] - ph + r0, rows), :]
                part = term if part is None else part + term
            part = part[ph:ph + TAP_ROWS]
            acc = part if acc is None else acc + part
        res_ref[pl.ds(r0, TAP_ROWS), :] = acc


def conv_fwd(ga, gb, cw, cb, lg, lb, name):
    s = ga.shape[0]
    tm = _tile(s, (512,))
    hb = tm // CONV_HALO

    def body(ga_ref, gb_ref, gah_ref, gbh_ref, cw_ref, cb_ref, lg_ref, lb_ref, c_ref, c1_ref, buf):
        i = pl.program_id(0)
        halo = gah_ref[...].astype(F32) * _sigmoid(gbh_ref[...].astype(F32))
        buf[0:CONV_HALO, :] = jnp.where(i > 0, halo, 0.0)
        buf[CONV_HALO:, :] = ga_ref[...].astype(F32) * _sigmoid(gb_ref[...].astype(F32))
        first = CONV_HALO - (CONV_WIDTH - 1)
        _tap_sum(buf, cw_ref, [first + j for j in range(CONV_WIDTH)], tm, c1_ref)
        acc = c1_ref[...] + cb_ref[...]
        c1_ref[...] = acc
        xh, _ = _ln_stats(acc)
        y = xh * lg_ref[...] + lb_ref[...]
        c_ref[...] = (y * _sigmoid(y)).astype(ACT)

    hspec = pl.BlockSpec((CONV_HALO, CONV_CH), lambda i: (jnp.maximum(i * hb - 1, 0), 0))
    vec = _fullspec((1, CONV_CH))
    spec = _rowspec(tm, CONV_CH)
    return pl.pallas_call(
        body, grid=(s // tm,), name=name,
        in_specs=[spec, spec, hspec, hspec, _fullspec((CONV_WIDTH, CONV_CH)), vec, vec, vec],
        out_specs=[spec, spec],
        out_shape=[jax.ShapeDtypeStruct((s, CONV_CH), ACT), jax.ShapeDtypeStruct((s, CONV_CH), F32)],
        scratch_shapes=[pltpu.VMEM((tm + CONV_HALO, CONV_CH), F32)],
        compiler_params=_cp("parallel"))(ga, gb, ga, gb, cw, cb, lg, lb)


def conv_bwd_ln(c1, dsrc, dcol, lg, lb, name):
    s = c1.shape[0]
    tm = _tile(s, (512,))

    def body(c1_ref, d_ref, lg_ref, lb_ref, o_ref, dg_ref, db_ref):
        @pl.when(pl.program_id(0) == 0)
        def _():
            dg_ref[...] = jnp.zeros_like(dg_ref)
            db_ref[...] = jnp.zeros_like(db_ref)

        xh, rstd = _ln_stats(c1_ref[...].astype(F32))
        y = xh * lg_ref[...] + lb_ref[...]
        sg = _sigmoid(y)
        dy = d_ref[...].astype(F32) * (sg * (1.0 + y * (1.0 - sg)))
        dg_ref[0:1, :] = dg_ref[0:1, :] + jnp.sum(dy * xh, axis=0, keepdims=True)
        db_ref[0:1, :] = db_ref[0:1, :] + jnp.sum(dy, axis=0, keepdims=True)
        dxh = dy * lg_ref[...]
        o_ref[...] = rstd * (dxh - jnp.mean(dxh, axis=-1, keepdims=True)
                             - xh * jnp.mean(dxh * xh, axis=-1, keepdims=True))

    vec = _fullspec((1, CONV_CH))
    acc = _fullspec((8, CONV_CH))
    return pl.pallas_call(
        body, grid=(s // tm,), name=name,
        in_specs=[_rowspec(tm, CONV_CH), _rowspec(tm, CONV_CH, dcol), vec, vec],
        out_specs=[_rowspec(tm, CONV_CH), acc, acc],
        out_shape=[jax.ShapeDtypeStruct((s, CONV_CH), F32)] + [jax.ShapeDtypeStruct((8, CONV_CH), F32)] * 2,
        compiler_params=_cp("arbitrary"))(c1, dsrc, lg, lb)


def conv_bwd_conv(dc1, ga, gb, cw, name):
    s = ga.shape[0]
    tm = _tile(s, (512,))
    hb = tm // CONV_HALO
    nt = s // tm
    last_h = s // CONV_HALO - 1
    first = CONV_HALO - (CONV_WIDTH - 1)


    def rows8(x):
        return jnp.sum(x.reshape(x.shape[0] // SUBLANES, SUBLANES, CONV_CH), axis=0)

    def body(d_ref, dn_ref, ga_ref, gb_ref, gah_ref, gbh_ref, cw_ref,
             dga_ref, dgb_ref, dw_ref, db_ref, dbuf, cbuf, sbuf):
        i = pl.program_id(0)

        @pl.when(i == 0)
        def _():
            dw_ref[...] = jnp.zeros_like(dw_ref)
            db_ref[...] = jnp.zeros_like(db_ref)

        d = d_ref[...]
        dbuf[0:tm, :] = d
        dbuf[tm:, :] = jnp.where(i < nt - 1, dn_ref[...], 0.0)
        halo = gah_ref[...].astype(F32) * _sigmoid(gbh_ref[...].astype(F32))
        cbuf[0:CONV_HALO, :] = jnp.where(i > 0, halo, 0.0)
        a = ga_ref[...].astype(F32)
        sg = _sigmoid(gb_ref[...].astype(F32))
        cbuf[CONV_HALO:, :] = a * sg
        for ph in range(SUBLANES):
            taps = [j for j in range(CONV_WIDTH) if (first + j) % SUBLANES == ph]
            if ph:
                sbuf[0:tm + CONV_HALO - SUBLANES, :] = cbuf[pl.ds(ph, tm + CONV_HALO - SUBLANES), :]
            src = sbuf if ph else cbuf
            for r0 in range(0, tm, TAP_ROWS):
                d_blk = dbuf[pl.ds(r0, TAP_ROWS), :]
                for j in taps:
                    tap = src[pl.ds(first + j - ph + r0, TAP_ROWS), :]
                    rows = slice(j * SUBLANES, (j + 1) * SUBLANES)
                    dw_ref[rows, :] = dw_ref[rows, :] + rows8(d_blk * tap)
        db_ref[...] = db_ref[...] + rows8(d)
        _tap_sum(dbuf, cw_ref, [CONV_WIDTH - 1 - j for j in range(CONV_WIDTH)], tm, sbuf)
        dc0 = sbuf[0:tm, :]
        dga_ref[...] = (dc0 * sg).astype(ACT)
        dgb_ref[...] = (dc0 * a * sg * (1.0 - sg)).astype(ACT)

    spec = _rowspec(tm, CONV_CH)
    hprev = pl.BlockSpec((CONV_HALO, CONV_CH), lambda i: (jnp.maximum(i * hb - 1, 0), 0))
    hnext = pl.BlockSpec((CONV_HALO, CONV_CH), lambda i: (jnp.minimum((i + 1) * hb, last_h), 0))
    return pl.pallas_call(
        body, grid=(nt,), name=name,
        in_specs=[spec, hnext, spec, spec, hprev, hprev, _fullspec((CONV_WIDTH, CONV_CH))],
        out_specs=[spec, spec, _fullspec((CONV_HALO * SUBLANES, CONV_CH)), _fullspec((SUBLANES, CONV_CH))],
        out_shape=[jax.ShapeDtypeStruct((s, CONV_CH), ACT)] * 2
        + [jax.ShapeDtypeStruct((CONV_HALO * SUBLANES, CONV_CH), F32), jax.ShapeDtypeStruct((SUBLANES, CONV_CH), F32)],
        scratch_shapes=[pltpu.VMEM((tm + CONV_HALO, CONV_CH), F32)] * 3,
        compiler_params=_cp("arbitrary"))(dc1, dc1, ga, gb, ga, gb, cw)


_GELU_K = math.sqrt(2.0 / math.pi)
_GELU_C = 0.044715


def _gelu(x):
    return 0.5 * x * (1.0 + jnp.tanh(_GELU_K * (x + _GELU_C * x * x * x)))


def _gelu_grad(x):
    t = jnp.tanh(_GELU_K * (x + _GELU_C * x * x * x))
    return 0.5 * (1.0 + t) + 0.5 * x * (1.0 - t * t) * _GELU_K * (1.0 + 3.0 * _GELU_C * x * x)


def _tril():
    qi = lax.broadcasted_iota(jnp.int32, (BLOCK, BLOCK), 0)
    kj = lax.broadcasted_iota(jnp.int32, (BLOCK, BLOCK), 1)
    return kj <= qi


GMLP_CHUNKS = 8


def _gmlp_weights(sw_ref, w_buf, wt_buf):
    tril = _tril()
    for g in range(D_GROUPS):
        w = jnp.where(tril, sw_ref[g], 0.0)
        w_buf[g] = w.astype(ACT)
        if wt_buf is not None:
            wt_buf[g] = w.T.astype(ACT)


def _gmlp_mix(w_buf, gn, sb_ref, m_buf):
    for c in range(GMLP_CHUNKS):
        rows = slice(c * BLOCK, (c + 1) * BLOCK)
        for g in range(D_GROUPS):
            cs = slice(g * HEAD_DIM, (g + 1) * HEAD_DIM)
            m_buf[rows, cs] = _dot(w_buf[g], gn[rows, cs], NN) + sb_ref[:, cs]


def gmlp_fwd(z, lg, lb, sw, sbx, name):
    s = z.shape[0]
    tm = GMLP_CHUNKS * BLOCK

    def body(z_ref, lg_ref, lb_ref, sw_ref, sb_ref, o_ref, w_buf, m_buf):
        @pl.when(pl.program_id(0) == 0)
        def _():
            _gmlp_weights(sw_ref, w_buf, None)

        zz = _gelu(z_ref[...].astype(F32))
        xh, _ = _ln_stats(zz[:, D_CH:])
        gn = (xh * lg_ref[...] + lb_ref[...]).astype(ACT)
        _gmlp_mix(w_buf, gn, sb_ref, m_buf)
        o_ref[...] = (zz[:, :D_CH] * m_buf[...]).astype(ACT)

    return pl.pallas_call(
        body, grid=(s // tm,), name=name,
        in_specs=[_rowspec(tm, 2 * D_CH), _fullspec((1, D_CH)), _fullspec((1, D_CH)),
                  _fullspec((D_GROUPS, BLOCK, BLOCK)), _fullspec((BLOCK, D_CH))],
        out_specs=_rowspec(tm, D_CH), out_shape=jax.ShapeDtypeStruct((s, D_CH), ACT),
        scratch_shapes=[pltpu.VMEM((D_GROUPS, BLOCK, BLOCK), ACT), pltpu.VMEM((tm, D_CH), F32)],
        compiler_params=_cp("arbitrary"))(z, lg, lb, sw, sbx)


def gmlp_bwd(z, dsrc, dcol, lg, lb, sw, sbx, name):
    s = z.shape[0]
    tm = GMLP_CHUNKS * BLOCK

    def body(z_ref, d_ref, lg_ref, lb_ref, sw_ref, sb_ref, dz_ref, dw_ref, dsb_ref, dg_ref, db_ref,
             w_buf, wt_buf, m_buf, dgn_buf):
        @pl.when(pl.program_id(0) == 0)
        def _():
            _gmlp_weights(sw_ref, w_buf, wt_buf)
            dw_ref[...] = jnp.zeros_like(dw_ref)
            dsb_ref[...] = jnp.zeros_like(dsb_ref)
            dg_ref[...] = jnp.zeros_like(dg_ref)
            db_ref[...] = jnp.zeros_like(db_ref)

        zr = z_ref[...].astype(F32)
        zz = _gelu(zr)
        u = zz[:, :D_CH]
        xh, rstd = _ln_stats(zz[:, D_CH:])
        gn = (xh * lg_ref[...] + lb_ref[...]).astype(ACT)
        dd = d_ref[...].astype(F32)
        _gmlp_mix(w_buf, gn, sb_ref, m_buf)
        dz_ref[:, :D_CH] = (dd * m_buf[...] * _gelu_grad(zr[:, :D_CH])).astype(ACT)
        dmix = dd * u
        dmix_a = dmix.astype(ACT)
        dsb = dmix[0:BLOCK]
        for c in range(1, GMLP_CHUNKS):
            dsb = dsb + dmix[c * BLOCK:(c + 1) * BLOCK]
        dsb_ref[...] = dsb_ref[...] + dsb
        tril = _tril()
        for g in range(D_GROUPS):
            cs = slice(g * HEAD_DIM, (g + 1) * HEAD_DIM)
            dw = None
            for c in range(GMLP_CHUNKS):
                rows = slice(c * BLOCK, (c + 1) * BLOCK)
                t = _dot(dmix_a[rows, cs], gn[rows, cs], NT)
                dw = t if dw is None else dw + t
                dgn_buf[rows, cs] = _dot(wt_buf[g], dmix_a[rows, cs], NN)
            dw_ref[g] = dw_ref[g] + jnp.where(tril, dw, 0.0)
        dgn = dgn_buf[...]
        dg_ref[0:1, :] = dg_ref[0:1, :] + jnp.sum(dgn * xh, axis=0, keepdims=True)
        db_ref[0:1, :] = db_ref[0:1, :] + jnp.sum(dgn, axis=0, keepdims=True)
        dxh = dgn * lg_ref[...]
        dgate = rstd * (dxh - jnp.mean(dxh, axis=-1, keepdims=True)
                        - xh * jnp.mean(dxh * xh, axis=-1, keepdims=True))
        dz_ref[:, D_CH:] = (dgate * _gelu_grad(zr[:, D_CH:])).astype(ACT)

    vec = _fullspec((1, D_CH))
    acc = _fullspec((8, D_CH))
    wshape = (D_GROUPS, BLOCK, BLOCK)
    return pl.pallas_call(
        body, grid=(s // tm,), name=name,
        in_specs=[_rowspec(tm, 2 * D_CH), _rowspec(tm, D_CH, dcol), vec, vec, _fullspec(wshape),
                  _fullspec((BLOCK, D_CH))],
        out_specs=[_rowspec(tm, 2 * D_CH), _fullspec(wshape), _fullspec((BLOCK, D_CH)), acc, acc],
        out_shape=[jax.ShapeDtypeStruct((s, 2 * D_CH), ACT), jax.ShapeDtypeStruct(wshape, F32),
                   jax.ShapeDtypeStruct((BLOCK, D_CH), F32),
                   jax.ShapeDtypeStruct((8, D_CH), F32), jax.ShapeDtypeStruct((8, D_CH), F32)],
        scratch_shapes=[pltpu.VMEM(wshape, ACT), pltpu.VMEM(wshape, ACT),
                        pltpu.VMEM((tm, D_CH), F32), pltpu.VMEM((tm, D_CH), F32)],
        compiler_params=_cp("arbitrary"))(z, dsrc, lg, lb, sw, sbx)


def _rms_bwd(dn, x, g):
    r = lax.rsqrt(jnp.mean(x * x, axis=-1, keepdims=True) + RMS_EPS)
    u = dn * g
    dx = r * u - x * (r * r * r) * jnp.mean(x * u, axis=-1, keepdims=True)
    return dx, dn * x * r


def final_loss(h, g, target, name):
    s = h.shape[0]
    tm = _tile(s, (1024, 512))

    def body(h_ref, g_ref, t_ref, loss_ref, dh_ref, dg_ref):
        @pl.when(pl.program_id(0) == 0)
        def _():
            loss_ref[...] = jnp.zeros_like(loss_ref)
            dg_ref[...] = jnp.zeros_like(dg_ref)

        x = h_ref[...]
        r = lax.rsqrt(jnp.mean(x * x, axis=-1, keepdims=True) + RMS_EPS)
        diff = x * r * g_ref[...] - t_ref[...]
        part = jnp.sum(jnp.sum(diff * diff, axis=-1, keepdims=True), axis=0, keepdims=True)
        loss_ref[...] = loss_ref[...] + part * (0.5 / D_MODEL)
        dx, dgt = _rms_bwd(diff * (1.0 / D_MODEL), x, g_ref[...])
        dh_ref[...] = dx
        dg_ref[0:1, :] = dg_ref[0:1, :] + jnp.sum(dgt, axis=0, keepdims=True)

    spec = _rowspec(tm, D_MODEL)
    return pl.pallas_call(
        body, grid=(s // tm,), name=name,
        in_specs=[spec, _fullspec((1, D_MODEL)), spec],
        out_specs=[_fullspec((8, LANES)), spec, _fullspec((8, D_MODEL))],
        out_shape=[jax.ShapeDtypeStruct((8, LANES), F32), jax.ShapeDtypeStruct((s, D_MODEL), F32),
                   jax.ShapeDtypeStruct((8, D_MODEL), F32)],
        compiler_params=_cp("arbitrary"))(h, g, target)


def mm_nt(dy, w, name, fold_dils=(), fold_cols=0):
    s, n = dy.shape
    k = w.shape[0]
    tm = _tile(s, (512,) if fold_dils else (1024, 512))
    tk = k if fold_dils else _tile(k, (1024, 512))

    def body(d_ref, w_ref, o_ref, *f_refs):
        res = _dot(d_ref[...].astype(ACT), w_ref[...], NT).astype(ACT)
        o_ref[...] = res
        cache = {}
        for f_ref, d in zip(f_refs, fold_dils):
            f_ref[...] = _fold_rows(res[:, :fold_cols], d, cache)

    out = pl.pallas_call(
        body, grid=(k // tk, s // tm), name=name,
        in_specs=[pl.BlockSpec((tm, n), lambda j, i: (i, 0)), pl.BlockSpec((tk, n), lambda j, i: (j, 0))],
        out_specs=[pl.BlockSpec((tm, tk), lambda j, i: (i, j))]
        + [pl.BlockSpec((tm // d, d * fold_cols), lambda j, i: (i, 0)) for d in fold_dils],
        out_shape=[jax.ShapeDtypeStruct((s, k), ACT)]
        + [jax.ShapeDtypeStruct((s // d, d * fold_cols), ACT) for d in fold_dils],
        compiler_params=_cp("parallel", "parallel"))(dy, w)
    return out if fold_dils else out[0]


def ffn_down_bwd(dh, wd, dact_dgate, dact_dup, name):
    s = dh.shape[0]
    f = wd.shape[0]
    tm = _tile(s, (1024, 512))
    tf = _tile(f, (1408, 512, 256, 128))

    def body(d_ref, w_ref, g_ref, u_ref, dg_ref, du_ref):
        dact = _dot(d_ref[...].astype(ACT), w_ref[...], NT)
        dg_ref[...] = (dact * g_ref[...].astype(F32)).astype(ACT)
        du_ref[...] = (dact * u_ref[...].astype(F32)).astype(ACT)

    tile = pl.BlockSpec((tm, tf), lambda j, i: (i, j))
    return pl.pallas_call(
        body, grid=(f // tf, s // tm), name=name,
        in_specs=[pl.BlockSpec((tm, D_MODEL), lambda j, i: (i, 0)),
                  pl.BlockSpec((tf, D_MODEL), lambda j, i: (j, 0)), tile, tile],
        out_specs=[tile, tile], out_shape=[jax.ShapeDtypeStruct((s, f), ACT)] * 2,
        compiler_params=_cp("parallel", "parallel"))(dh, wd, dact_dgate, dact_dup)


def mm_nt_rms(parts, h, g, dh, name):
    s = h.shape[0]
    tm = _tile(s, (1024, 512) if len(parts) == 1 else (512,))
    wspec = lambda w: pl.BlockSpec(w.shape, lambda i: (0, 0), pipeline_mode=pl.Buffered(1))
    np_ = len(parts)

    def body(*refs):
        d_refs = refs[:np_]
        w_refs = refs[np_:2 * np_]
        h_ref, g_ref, dh_ref, o_ref, dg_ref = refs[2 * np_:]

        @pl.when(pl.program_id(0) == 0)
        def _():
            dg_ref[...] = jnp.zeros_like(dg_ref)

        dn = None
        for d_ref, w_ref in zip(d_refs, w_refs):
            t = _dot(d_ref[...], w_ref[...], NT)
            dn = t if dn is None else dn + t
        dx, dgt = _rms_bwd(dn, h_ref[...], g_ref[...])
        o_ref[...] = dh_ref[...] + dx
        dg_ref[0:1, :] = dg_ref[0:1, :] + jnp.sum(dgt, axis=0, keepdims=True)

    spec = _rowspec(tm, D_MODEL)
    return pl.pallas_call(
        body, grid=(s // tm,), name=name,
        in_specs=[_rowspec(tm, d.shape[1]) for d, _ in parts] + [wspec(w) for _, w in parts]
        + [spec, _fullspec((1, D_MODEL)), spec],
        out_specs=[spec, _fullspec((8, D_MODEL))],
        out_shape=[jax.ShapeDtypeStruct((s, D_MODEL), F32), jax.ShapeDtypeStruct((8, D_MODEL), F32)],
        compiler_params=_cp("arbitrary"))(*[d for d, _ in parts], *[w for _, w in parts], h, g, dh)


def mm_tn(a, b, name):
    s, k = a.shape
    n = b.shape[1]
    tk = _tile(k, (1024, 1408, 512, 256, 128))
    tn = _tile(n, (1408, 1280, 1024, 896, 512, 256, 128))
    ts = _tile(s, (2048, 512) if b.dtype == ACT else (1024, 512))
    nt = s // ts

    def body(a_ref, b_ref, o_ref, acc):
        t = _dot(a_ref[...].astype(ACT), b_ref[...].astype(ACT), TN)
        step = pl.program_id(2)

        @pl.when(step == 0)
        def _():
            acc[...] = t

        @pl.when(step > 0)
        def _():
            acc[...] = acc[...] + t

        @pl.when(step == nt - 1)
        def _():
            o_ref[...] = acc[...].astype(ACT)

    return pl.pallas_call(
        body, grid=(k // tk, n // tn, nt), name=name,
        in_specs=[pl.BlockSpec((ts, tk), lambda i, j, t: (t, i)), pl.BlockSpec((ts, tn), lambda i, j, t: (t, j))],
        out_specs=pl.BlockSpec((tk, tn), lambda i, j, t: (i, j)),
        out_shape=jax.ShapeDtypeStruct((k, n), ACT),
        scratch_shapes=[pltpu.VMEM((tk, tn), F32)],
        compiler_params=_cp("parallel", "parallel", "arbitrary"))(a, b)


def _adamw_math(w, g, m, v):
    m = ADAM_B1 * m + (1.0 - ADAM_B1) * g
    v = ADAM_B2 * v + (1.0 - ADAM_B2) * (g * g)
    m_hat = m / (1.0 - ADAM_B1 ** ADAM_STEP)
    v_hat = v / (1.0 - ADAM_B2 ** ADAM_STEP)
    delta = -ADAM_LR * (m_hat / (jnp.sqrt(v_hat) + ADAM_EPS) + ADAM_WD * w)
    return delta, m, v


def sum_adamw(parts, w, m, v, layer, others, name):
    nl, r, c = w.shape
    tr = _tile(r, (256, 128, 64, 32, 16, 8))

    def body(p_ref, w_ref, m_ref, v_ref, *rest):
        g_ref, d_ref, mo_ref, vo_ref = rest[-4:]
        g = p_ref[0].astype(F32)
        for i in range(1, N_DEV):
            g = g + p_ref[i].astype(F32)
        d, mm, vv = _adamw_math(w_ref[...], g, m_ref[...], v_ref[...])
        g_ref[...] = g
        d_ref[...] = d
        mo_ref[...] = mm
        vo_ref[...] = vv

    spec = pl.BlockSpec((None, tr, c), lambda i: (layer, i, 0))
    in_specs = [pl.BlockSpec((N_DEV, tr, c), lambda i: (0, i, 0))] + [spec] * 3
    args = [parts, w, m, v]
    aliases = {}
    if others is not None:
        in_specs += [pl.BlockSpec(memory_space=pl.ANY)] * 4
        args += list(others)
        aliases = {4 + j: j for j in range(4)}
    return pl.pallas_call(
        body, grid=(r // tr,), name=name, in_specs=in_specs, out_specs=[spec] * 4,
        out_shape=[jax.ShapeDtypeStruct((nl, r, c), F32)] * 4, input_output_aliases=aliases,
        compiler_params=_cp("parallel"))(*args)


def cast_layers(items, name):
    n = len(items)

    def body(*refs):
        for i in range(n):
            refs[n + i][...] = refs[i][...].astype(ACT)

    return pl.pallas_call(
        body, grid=(1,), name=name,
        in_specs=[pl.BlockSpec((None,) + w.shape[1:], lambda i, l=l: (l, 0, 0)) for w, l in items],
        out_specs=[_fullspec(w.shape[1:]) for w, _ in items],
        out_shape=[jax.ShapeDtypeStruct(w.shape[1:], ACT) for w, _ in items],
        compiler_params=_cp("arbitrary"))(*[w for w, _ in items])


def adamw_small(ws, gs, ms, vs, name):
    k = len(ws)

    def body(*refs):
        for i in range(k):
            w_ref, g_ref, m_ref, v_ref = (refs[j * k + i] for j in range(4))
            d, mm, vv = _adamw_math(w_ref[...], g_ref[...], m_ref[...], v_ref[...])
            refs[4 * k + i][...] = d
            refs[5 * k + i][...] = mm
            refs[6 * k + i][...] = vv

    shapes = [jax.ShapeDtypeStruct(w.shape, F32) for w in ws]
    specs = [_fullspec(w.shape) for w in ws]
    out = pl.pallas_call(
        body, grid=(1,), name=name, in_specs=specs * 4, out_specs=specs * 3, out_shape=shapes * 3,
        compiler_params=_cp("arbitrary"))(*ws, *gs, *ms, *vs)
    return out[:k], out[k:2 * k], out[2 * k:]


def sum_slots(x, out_dtype, name):
    g, r, c = x.shape
    tr = r if x.size * x.dtype.itemsize <= SMALL_BLOCK_BYTES else _tile(r, (256, 128, 64, 32, 16, 8))

    def body(x_ref, o_ref):
        acc = x_ref[0].astype(F32)
        for i in range(1, g):
            acc = acc + x_ref[i].astype(F32)
        o_ref[...] = acc.astype(o_ref.dtype)

    return pl.pallas_call(
        body, grid=(r // tr,), name=name,
        in_specs=[pl.BlockSpec((g, tr, c), lambda i: (0, i, 0))], out_specs=_rowspec(tr, c),
        out_shape=jax.ShapeDtypeStruct((r, c), out_dtype),
        compiler_params=_cp("parallel"))(x)


HBM_SPEC = pl.BlockSpec(memory_space=pltpu.HBM)
SEM_SPEC = pl.BlockSpec(memory_space=pltpu.SEMAPHORE)
DATAFLOW = pltpu.SideEffectType.DATAFLOW_SIDE_EFFECTING


def _my_rank():
    return 4 * lax.axis_index("x") + 2 * lax.axis_index("y") + lax.axis_index("c")


def _exchange_copies(x_refs, land_refs, send, recv, a2a):
    pos = [lax.axis_index(a) for a in AXES]
    me = _my_rank()
    copies = []
    for x_ref, land_ref, s_ref, r_ref in zip(x_refs, land_refs, send, recv):
        for k in range(N_DEV - 1):
            bits = ((k + 1) >> 2 & 1, (k + 1) >> 1 & 1, (k + 1) & 1)
            peer = tuple(1 - p if b else p for p, b in zip(pos, bits))
            prank = 4 * peer[0] + 2 * peer[1] + peer[2]
            copies.append(pltpu.make_async_remote_copy(
                src_ref=x_ref.at[prank] if a2a else x_ref, dst_ref=land_ref.at[me],
                send_sem=s_ref.at[k], recv_sem=r_ref.at[k], device_id=peer, device_id_type=MESH))
    return copies


def exchange_start(xs, a2a, name, after=None):
    n = len(xs)
    me = _my_rank()
    lands = []
    for x in xs:
        own = lax.dynamic_index_in_dim(x, me, 0, keepdims=True) if a2a else x[None]
        shape = x.shape if a2a else (N_DEV,) + x.shape
        lands.append(lax.dynamic_update_slice(lax.empty(shape, x.dtype), own, (me,) + (0,) * (len(shape) - 1)))

    def body(*refs):
        x_refs, land_refs = refs[:n], refs[n:2 * n]
        outs = refs[len(refs) - 4 * n - 1:]
        for cp in _exchange_copies(x_refs, land_refs, outs[:n], outs[n:2 * n], a2a):
            cp.start()
        token = outs[4 * n]
        token[...] = jnp.zeros_like(token)

    sems = [pltpu.SemaphoreType.DMA((N_DEV - 1,))] * n
    out = pl.pallas_call(
        body, name=name,
        out_shape=tuple(sems + sems + [pltpu.HBM(x.shape, x.dtype) for x in xs]
                        + [pltpu.HBM(l.shape, l.dtype) for l in lands] + [jax.ShapeDtypeStruct((8, LANES), F32)]),
        in_specs=[HBM_SPEC] * (2 * n) + ([] if after is None else [pl.BlockSpec(memory_space=pl.ANY)]),
        out_specs=tuple([SEM_SPEC] * (2 * n) + [HBM_SPEC] * (2 * n) + [pl.BlockSpec(memory_space=pltpu.VMEM)]),
        input_output_aliases={i: 2 * n + i for i in range(2 * n)},
        compiler_params=pltpu.CompilerParams(has_side_effects=DATAFLOW),
    )(*[pltpu.with_memory_space_constraint(a, pltpu.HBM) for a in list(xs) + lands], *([] if after is None else [after]))
    return (out[:n], out[n:2 * n], out[2 * n:3 * n], out[3 * n:4 * n]), out[4 * n]


def exchange_wait(handles, after, a2a, name):
    send, recv, x_thru, land_thru = handles
    n = len(x_thru)

    def body(*refs):
        x_refs, land_refs = refs[:n], refs[n:2 * n]
        s_refs, r_refs = refs[2 * n:3 * n], refs[3 * n:4 * n]
        for cp in _exchange_copies(x_refs, land_refs, s_refs, r_refs, a2a):
            cp.wait_send()
            cp.wait_recv()

    out = pl.pallas_call(
        body, name=name,
        out_shape=tuple([pltpu.HBM(a.shape, a.dtype) for a in list(x_thru) + list(land_thru)]),
        in_specs=[HBM_SPEC] * (2 * n) + [SEM_SPEC] * (2 * n) + [pl.BlockSpec(memory_space=pl.ANY)],
        out_specs=tuple([HBM_SPEC] * (2 * n)),
        input_output_aliases={i: i for i in range(2 * n)},
        compiler_params=pltpu.CompilerParams(has_side_effects=DATAFLOW),
    )(*x_thru, *land_thru, *send, *recv, after)
    return out[n:2 * n]


def _local_step(x, target, weight, emit, P):
    s = x.shape[0]
    tabs = _rope_tables(s)
    sinkb = jnp.broadcast_to(P["ev_sinks"].reshape(N_HEADS, 1), (N_HEADS, LANES))
    sbx = jnp.repeat(P["od_spatial_b"].reshape(D_GROUPS, BLOCK).T, HEAD_DIM, axis=1)
    sw = P["od_spatial_w"].reshape(D_GROUPS, BLOCK, BLOCK)
    fg = P["ffn_norm_g"]
    latest = [None]

    def out(name, layer, grad):
        tok = emit(name, layer, grad)
        if tok is not None:
            latest[0] = tok

    def dep(a):
        return a if latest[0] is None else a + latest[0][0:1, 0:1]

    n0 = rmsnorm(x, P["ev_norm_g"], "rms_in")
    qk_e, v_e, ga, gb = inproj(n0, weight("ev_w_in", 0, n0), tabs, ATT_W + 128,
                               (ATT_W + 128, 128, CONV_CH, CONV_CH), (), "ev_inproj")
    a_e, lse_e = attn_fwd(qk_e, v_e, d=1, hkv=A_KV_HEADS, max_dist=BLOCK - 1, sink=sinkb, out_dtype=ACT,
                          name="ev_attn")
    c_act, c1 = conv_fwd(ga, gb, P["ev_conv_w"], P["ev_conv_b"], P["ev_conv_ln_g"], P["ev_conv_ln_b"], "ev_conv")
    h1, n1 = mm_res([a_e, c_act], weight("ev_w_out", 0, c_act), x, fg[0:1], "ev_outproj")
    gate0, up0, act0 = ffn_up(n1, weight("ffn_w_gate", 0, n1), weight("ffn_w_up", 0, n1), "ffn0_up")
    h2, n2 = mm_res([act0], weight("ffn_w_down", 0, act0), h1, P["od_norm_g"], "ffn0_down")
    fold_dils = tuple(dil for _, dil in DILATED if dil > 1)
    qk_o, v_o, z, *pre = inproj(n2, weight("od_w_in", 0, n2), tabs, 2 * ATT_W, (2 * ATT_W, ATT_W, 2 * D_CH),
                                fold_dils, "od_inproj")
    folded = {dil: (pre[2 * i], pre[2 * i + 1]) for i, dil in enumerate(fold_dils)}
    outs, lses = [], []
    for window, dil in DILATED:
        assert window // dil == BLOCK
        o_r, l_r = attn_fwd(qk_o, v_o, d=dil, hkv=N_HEADS, max_dist=BLOCK, sink=None, out_dtype=ACT,
                            name="od_attn_d%d" % dil, folded=folded.get(dil), keep_folded=True)
        outs.append(o_r)
        lses.append(l_r)
    dils = [dil for _, dil in DILATED]
    c_out, lse_o, *c_folded = combine_fwd(outs, lses, dils, "od_combine")
    d_out = gmlp_fwd(z, P["od_sgu_ln_g"], P["od_sgu_ln_b"], sw, sbx, "od_gmlp")
    h3, n3 = mm_res([c_out, d_out], weight("od_w_out", 0, d_out), h2, fg[1:2], "od_outproj")
    gate1, up1, act1 = ffn_up(n3, weight("ffn_w_gate", 1, n3), weight("ffn_w_up", 1, n3), "ffn1_up")
    h4, _ = mm_res([act1], weight("ffn_w_down", 1, act1), h3, None, "ffn1_down")
    loss_part, dh4, dg_final = final_loss(h4, P["final_norm_g"], target, "loss_head")

    def ffn_bwd(layer, dh_out, h_in, n_in, gate, up, act):
        wg, wu, wd = (weight(n, layer, dh_out) for n in ("ffn_w_gate", "ffn_w_up", "ffn_w_down"))
        tag = "ffn%d" % layer
        dgate, dup = ffn_down_bwd(dh_out, wd, gate, up, tag + "_down_bwd")
        g_wd = mm_tn(act, dh_out, tag + "_dwd")
        dh_in, dgn = mm_nt_rms([(dgate, wg), (dup, wu)], h_in, dep(fg[layer:layer + 1]), dh_out, tag + "_up_bwd")
        out("ffn_w_down", layer, g_wd)
        out("ffn_w_gate", layer, mm_tn(n_in, dgate, tag + "_dwg"))
        out("ffn_w_up", layer, mm_tn(n_in, dup, tag + "_dwu"))
        return dh_in, dgn[0:1]

    dh3, dgn_f1 = ffn_bwd(1, dh4, h3, n3, gate1, up1, act1)

    dcd, *dc_folded = mm_nt(dh3, weight("od_w_out", 0, dh3), "od_outproj_bwd", fold_dils, ATT_W)
    do_o = {dil: (dc_folded[i], c_folded[i]) for i, dil in enumerate(fold_dils)}
    dz, g_sw, g_sbx, g_slg, g_slb = gmlp_bwd(z, dcd, 1, dep(P["od_sgu_ln_g"]), P["od_sgu_ln_b"], sw, sbx,
                                             "od_gmlp_bwd")
    g_sb = jnp.sum(g_sbx.reshape(BLOCK, D_GROUPS, HEAD_DIM), axis=-1).T
    out("od_spatial", 0, jnp.concatenate([g_sw.reshape(D_GROUPS * BLOCK, BLOCK), g_sb], axis=0))
    out("od_w_out", 0, jnp.concatenate([mm_tn(c_out, dh3, "od_dwo_c"), mm_tn(d_out, dh3, "od_dwo_d")], axis=0))
    dqkv = [attn_bwd(qk_o, v_o, dcd if dil == 1 else dcd[:, :ATT_W], 0, c_out, lse_o, d=dil, hkv=N_HEADS,
                     max_dist=BLOCK, sink=None, out_dtype=ACT, name="od_attn_bwd_d%d" % dil,
                     folded=folded.get(dil), folded_do_o=do_o.get(dil), keep_folded=True)
            for window, dil in DILATED]
    dproj_o = assemble([([(b[j], dil) for b, dil in zip(dqkv, dils)], j < 2) for j in range(3)] + [([dz], False)],
                       tabs, "od_dproj")
    dh2, dgn_od = mm_nt_rms([(dproj_o, weight("od_w_in", 0, dproj_o))], h2, dep(P["od_norm_g"]), dh3,
                            "od_inproj_bwd")
    out("od_w_in", 0, mm_tn(n2, dproj_o, "od_dwi"))

    dh1, dgn_f0 = ffn_bwd(0, dh2, h1, n1, gate0, up0, act0)

    dac = mm_nt(dh1, weight("ev_w_out", 0, dh1), "ev_outproj_bwd")
    dc1, g_clg, g_clb = conv_bwd_ln(c1, dac, 1, dep(P["ev_conv_ln_g"]), P["ev_conv_ln_b"], "ev_conv_bwd_ln")
    out("ev_w_out", 0, jnp.concatenate([mm_tn(a_e, dh1, "ev_dwo_a"), mm_tn(c_act, dh1, "ev_dwo_c")], axis=0))
    dga, dgb, g_cw, g_cb = conv_bwd_conv(dc1, ga, gb, P["ev_conv_w"], "ev_conv_bwd")
    dq, dk, dv, dsink = attn_bwd(qk_e, v_e, dac, 0, a_e, lse_e, d=1, hkv=A_KV_HEADS, max_dist=BLOCK - 1,
                                 sink=sinkb, out_dtype=F32, name="ev_attn_bwd")
    dproj_e = assemble([([dq], True), ([dk], True), ([dv], False), ([dga], False), ([dgb], False)], tabs,
                       "ev_dproj")
    out("ev_w_in", 0, mm_tn(n0, dproj_e, "ev_dwi"))
    dx, dgn_ev = mm_nt_rms([(dproj_e, weight("ev_w_in", 0, dproj_e))], x, dep(P["ev_norm_g"]), dh1,
                           "ev_inproj_bwd")

    small = {
        "ev_norm_g": dgn_ev[0:1],
        "ev_sinks": dsink[:, 0:1].reshape(1, N_HEADS),
        "ev_conv_w": jnp.sum(g_cw.reshape(CONV_HALO, SUBLANES, CONV_CH), axis=1)[:CONV_WIDTH],
        "ev_conv_b": jnp.sum(g_cb, axis=0, keepdims=True),
        "ev_conv_ln_g": g_clg[0:1],
        "ev_conv_ln_b": g_clb[0:1],
        "od_norm_g": dgn_od[0:1],
        "od_sgu_ln_g": g_slg[0:1],
        "od_sgu_ln_b": g_slb[0:1],
        "od_spatial_w": g_sw.reshape(D_GROUPS * BLOCK, BLOCK),
        "od_spatial_b": g_sb,
        "ffn_norm_g": jnp.concatenate([dgn_f0, dgn_f1], axis=0),
        "final_norm_g": dg_final[0:1],
    }
    return loss_part, dx, small


BIG = ("ev_w_in", "ev_w_out", "od_w_in", "od_w_out", "ffn_w_gate", "ffn_w_up", "ffn_w_down")
COL_SHARDED = ("ev_w_in", "od_w_in", "ffn_w_gate", "ffn_w_up")
GATHER_GROUPS = (
    (("ev_w_in", 0),),
    (("ev_w_out", 0),),
    (("ffn_w_gate", 0), ("ffn_w_up", 0)),
    (("ffn_w_down", 0),),
    (("od_w_in", 0),),
    (("od_w_out", 0),),
    (("ffn_w_gate", 1), ("ffn_w_up", 1)),
    (("ffn_w_down", 1),),
)
GATHER_EARLY = 4
GATHER_LATE_AT = 2
REDUCE_GROUPS = (
    (("ffn_w_down", 1), ("ffn_w_gate", 1), ("ffn_w_up", 1)),
    (("od_w_out", 0),),
    (("od_w_in", 0),),
    (("ffn_w_down", 0), ("ffn_w_gate", 0), ("ffn_w_up", 0)),
    (("ev_w_out", 0),),
    (("ev_w_in", 0),),
)


def _unshard(name, g):
    if name in COL_SHARDED:
        return jnp.moveaxis(g, 0, 1).reshape(g.shape[1], N_DEV * g.shape[2])
    return g.reshape(N_DEV * g.shape[1], g.shape[2])


def _shard_slots(name, full):
    r, c = full.shape
    if name in COL_SHARDED:
        return jnp.moveaxis(full.reshape(r, N_DEV, c // N_DEV), 1, 0)
    return full.reshape(N_DEV, r // N_DEV, c)


def kernel(x, ev_norm_g, ev_w_in, ev_sinks, ev_conv_w, ev_conv_b, ev_conv_ln_g, ev_conv_ln_b, ev_w_out, od_norm_g, od_w_in, od_sgu_ln_g, od_sgu_ln_b, od_spatial_w, od_spatial_b, od_w_out, ffn_norm_g, ffn_w_gate, ffn_w_up, ffn_w_down, final_norm_g, loss_target, m_ev_norm_g, m_ev_w_in, m_ev_sinks, m_ev_conv_w, m_ev_conv_b, m_ev_conv_ln_g, m_ev_conv_ln_b, m_ev_w_out, m_od_norm_g, m_od_w_in, m_od_sgu_ln_g, m_od_sgu_ln_b, m_od_spatial_w, m_od_spatial_b, m_od_w_out, m_ffn_norm_g, m_ffn_w_gate, m_ffn_w_up, m_ffn_w_down, m_final_norm_g, v_ev_norm_g, v_ev_w_in, v_ev_sinks, v_ev_conv_w, v_ev_conv_b, v_ev_conv_ln_g, v_ev_conv_ln_b, v_ev_w_out, v_od_norm_g, v_od_w_in, v_od_sgu_ln_g, v_od_sgu_ln_b, v_od_spatial_w, v_od_spatial_b, v_od_w_out, v_ffn_norm_g, v_ffn_w_gate, v_ffn_w_up, v_ffn_w_down, v_final_norm_g):
    names = ["ev_norm_g", "ev_w_in", "ev_sinks", "ev_conv_w", "ev_conv_b", "ev_conv_ln_g", "ev_conv_ln_b", "ev_w_out",
             "od_norm_g", "od_w_in", "od_sgu_ln_g", "od_sgu_ln_b", "od_spatial_w", "od_spatial_b", "od_w_out",
             "ffn_norm_g", "ffn_w_gate", "ffn_w_up", "ffn_w_down", "final_norm_g"]
    wts = dict(zip(names, [ev_norm_g, ev_w_in, ev_sinks, ev_conv_w, ev_conv_b, ev_conv_ln_g, ev_conv_ln_b, ev_w_out,
                           od_norm_g, od_w_in, od_sgu_ln_g, od_sgu_ln_b, od_spatial_w, od_spatial_b, od_w_out,
                           ffn_norm_g, ffn_w_gate, ffn_w_up, ffn_w_down, final_norm_g]))
    mom = dict(zip(names, [m_ev_norm_g, m_ev_w_in, m_ev_sinks, m_ev_conv_w, m_ev_conv_b, m_ev_conv_ln_g, m_ev_conv_ln_b,
                           m_ev_w_out, m_od_norm_g, m_od_w_in, m_od_sgu_ln_g, m_od_sgu_ln_b, m_od_spatial_w,
                           m_od_spatial_b, m_od_w_out, m_ffn_norm_g, m_ffn_w_gate, m_ffn_w_up, m_ffn_w_down,
                           m_final_norm_g]))
    vel = dict(zip(names, [v_ev_norm_g, v_ev_w_in, v_ev_sinks, v_ev_conv_w, v_ev_conv_b, v_ev_conv_ln_g, v_ev_conv_ln_b,
                           v_ev_w_out, v_od_norm_g, v_od_w_in, v_od_sgu_ln_g, v_od_sgu_ln_b, v_od_spatial_w,
                           v_od_spatial_b, v_od_w_out, v_ffn_norm_g, v_ffn_w_gate, v_ffn_w_up, v_ffn_w_down,
                           v_final_norm_g]))
    me = _my_rank()

    sp = jnp.zeros((40, LANES), F32)
    sp = sp.at[0:CONV_WIDTH, 0:64].set(ev_conv_w[0])
    sp = sp.at[32, :].set(od_norm_g[0])
    sp = sp.at[33, 0:64].set(od_sgu_ln_g[0])
    sp = sp.at[34, 0:64].set(od_sgu_ln_b[0])

    early = [k for grp in GATHER_GROUPS[:GATHER_EARLY] for k in grp]
    late = [k for grp in GATHER_GROUPS[GATHER_EARLY:] for k in grp]
    early_act = dict(zip(early, cast_layers([(wts[n], l) for n, l in early], "cast_early")))
    late_act = dict(zip(late, cast_layers([(wts[n], l) for n, l in late], "cast_late")))
    ag_early, ag_token = exchange_start([sp] + [early_act[k] for k in early], False, "ag_start")
    ag_late = []
    full_w = {}
    P = {
        "ev_norm_g": ev_norm_g + ag_token[0:1, 0:1], "ev_sinks": ev_sinks, "ev_conv_b": ev_conv_b,
        "ev_conv_ln_g": ev_conv_ln_g, "ev_conv_ln_b": ev_conv_ln_b,
        "od_spatial_w": od_spatial_w, "od_spatial_b": od_spatial_b, "ffn_norm_g": ffn_norm_g,
        "final_norm_g": final_norm_g.reshape(1, D_MODEL),
    }

    def weight(name, layer, after):
        if (name, layer) not in full_w:
            gi = [i for i, grp in enumerate(GATHER_GROUPS) if (name, layer) in grp][0]
            if gi < GATHER_EARLY:
                idx = [1 + early.index(k) for k in GATHER_GROUPS[gi]]
                handles = ag_early
            else:
                idx = [late.index(k) for k in GATHER_GROUPS[gi]]
                handles = ag_late[0]
            if gi == 0:
                idx = [0] + idx
            lands = exchange_wait(tuple([h[i] for i in idx] for h in handles), after, False, "ag_wait%d" % gi)
            if gi == 0:
                spg, lands = lands[0], lands[1:]
                P["ev_conv_w"] = jnp.moveaxis(spg[:, 0:CONV_WIDTH, 0:64], 0, 1).reshape(CONV_WIDTH, CONV_CH)
                P["od_norm_g"] = spg[:, 32, :].reshape(1, D_MODEL)
                P["od_sgu_ln_g"] = spg[:, 33, 0:64].reshape(1, D_CH)
                P["od_sgu_ln_b"] = spg[:, 34, 0:64].reshape(1, D_CH)
            if gi == GATHER_LATE_AT:
                ag_late.append(exchange_start([late_act[k] for k in late], False, "ag_start_late", after=lands[0])[0])
            for k, land in zip(GATHER_GROUPS[gi], lands):
                full_w[k] = _unshard(k[0], land)
        return full_w[(name, layer)]

    pending, rs_started, spatial = {}, [], []

    def emit(name, layer, grad):
        if name == "od_spatial":
            handles, token = exchange_start([grad], False, "ar_start_b")
            spatial.append(handles)
            return token
        pending[(name, layer)] = grad
        for gi, grp in enumerate(REDUCE_GROUPS):
            if (name, layer) in grp and all(k in pending for k in grp):
                handles, token = exchange_start([_shard_slots(k[0], pending[k]).astype(ACT) for k in grp], True,
                                                "rs_start%d" % gi)
                rs_started.append((gi, handles))
                return token
        return None

    loss_part, dx, small = _local_step(x[0], loss_target[0], weight, emit, P)
    loss = lax.psum(loss_part[0, 0], AXES)

    wide = ["ev_norm_g", "ev_sinks", "ev_conv_w", "ev_conv_b", "ev_conv_ln_g", "ev_conv_ln_b", "od_norm_g",
            "od_sgu_ln_g", "od_sgu_ln_b", "ffn_norm_g", "final_norm_g"]
    blk_a = jnp.concatenate(
        [jnp.pad(small[n], ((0, 0), (0, D_MODEL - small[n].shape[1]))) for n in wide], axis=0)
    blk_a = jnp.pad(blk_a, ((0, 48 - blk_a.shape[0]), (0, 0)))
    ar_a, ar_token = exchange_start([blk_a], False, "ar_start_a")

    results = {}
    for gi, handles in rs_started:
        lands = exchange_wait(handles, ar_token, True, "rs_wait%d" % gi)
        for (n, l), land in zip(REDUCE_GROUPS[gi], lands):
            results[n] = sum_adamw(land, wts[n], mom[n], vel[n], l, results.get(n), "adamw_%s%d" % (n, l))
    out_g, out_d, out_m, out_v = {}, {}, {}, {}
    for n in BIG:
        out_g[n], out_d[n], out_m[n], out_v[n] = results[n]

    last = results[REDUCE_GROUPS[-1][-1][0]][0]
    sum_b = sum_slots(exchange_wait(spatial[0], last, False, "ar_wait_b")[0], F32, "ar_sum_b")
    sum_a = sum_slots(exchange_wait(ar_a, last, False, "ar_wait_a")[0], F32, "ar_sum_a")
    full = {}
    off = 0
    for n in wide:
        r_, c_ = small[n].shape
        full[n] = sum_a[off:off + r_, 0:c_]
        off += r_
    full["od_spatial_w"] = sum_b[0:D_GROUPS * BLOCK]
    full["od_spatial_b"] = sum_b[D_GROUPS * BLOCK:D_GROUPS * BLOCK + D_GROUPS]
    full["ev_conv_w"] = lax.dynamic_slice_in_dim(full["ev_conv_w"], me * 64, 64, axis=1)
    full["od_norm_g"] = lax.dynamic_slice_in_dim(full["od_norm_g"], me * 128, 128, axis=1)
    full["od_sgu_ln_g"] = lax.dynamic_slice_in_dim(full["od_sgu_ln_g"], me * 64, 64, axis=1)
    full["od_sgu_ln_b"] = lax.dynamic_slice_in_dim(full["od_sgu_ln_b"], me * 64, 64, axis=1)

    small_names = [n for n in names if n not in BIG]
    view = {n: ((-1, wts[n].shape[-1]) if wts[n].ndim > 1 else (1, -1)) for n in small_names}
    ds, ms, vs = adamw_small([wts[n].reshape(view[n]) for n in small_names],
                             [full[n].reshape(view[n]) for n in small_names],
                             [mom[n].reshape(view[n]) for n in small_names],
                             [vel[n].reshape(view[n]) for n in small_names], "adamw_small")
    for i, n in enumerate(small_names):
        shp = wts[n].shape
        out_g[n], out_d[n], out_m[n], out_v[n] = (full[n].reshape(shp), ds[i].reshape(shp), ms[i].reshape(shp),
                                                  vs[i].reshape(shp))

    return (loss, dx[None], *[out_g[n] for n in names], *[out_d[n] for n in names],
            *[out_m[n] for n in names], *[out_v[n] for n in names])
```

```python
import functools
import math

import jax
import jax.numpy as jnp
from jax import lax
from jax.experimental import pallas as pl
from jax.experimental.pallas import tpu as pltpu

F32 = jnp.float32
ACT = jnp.bfloat16

D_MODEL = 1024
HEAD_DIM = 64
N_HEADS = 8
ATT_W = N_HEADS * HEAD_DIM
A_KV_HEADS = 2
CONV_CH = 512
CONV_WIDTH = 31
CONV_HALO = 32
D_CH = 512
D_GROUPS = 8
BLOCK = 128
D_FF = 2816
ROT_DIM = 16
ROPE_THETA = 500000.0
RMS_EPS = 1e-6
LN_EPS = 1e-5
DILATED = ((128, 1), (512, 4), (2048, 16))
NEG = -1e30
LANES = 128

ADAM_LR = 0.001
ADAM_B1 = 0.9
ADAM_B2 = 0.999
ADAM_EPS = 1e-08
ADAM_WD = 0.01
ADAM_STEP = 10

V7X_VMEM_LIMIT = 56 * 1024 * 1024
SMALL_BLOCK_BYTES = 6 * 1024 * 1024
N_DEV = 8

NN = (((1,), (0,)), ((), ()))
NT = (((1,), (1,)), ((), ()))
TN = (((0,), (0,)), ((), ()))
MESH = pl.DeviceIdType.MESH
AXES = ("x", "y", "c")


def _dot(a, b, dims):
    return lax.dot_general(a, b, dims, preferred_element_type=F32)


def _cp(*sem):
    return pltpu.CompilerParams(dimension_semantics=sem if sem else None,
                                vmem_limit_bytes=V7X_VMEM_LIMIT)


def _tile(n, prefs):
    for p in prefs:
        if n % p == 0:
            return p
    return n


def _sigmoid(x):
    return 1.0 / (1.0 + jnp.exp(-x))


def _rowspec(tm, w, col=0):
    return pl.BlockSpec((tm, w), lambda i, col=col: (i, col))


def _fullspec(shape):
    nd = len(shape)
    return pl.BlockSpec(shape, lambda *a, nd=nd: (0,) * nd)


def _rope_tables(seq):
    half = ROT_DIM // 2
    inv_freq = ROPE_THETA ** (-jnp.arange(half, dtype=F32) * (2.0 / ROT_DIM))
    ang = jnp.arange(seq, dtype=jnp.int32).astype(F32)[:, None] * inv_freq[None, :]
    cos, sin = jnp.cos(ang), jnp.sin(ang)
    lane = jnp.arange(LANES)
    jm = lane % HEAD_DIM
    idx = jm % half
    c = jnp.where(jm[None, :] < ROT_DIM, cos[:, idx], 1.0)
    sa = jnp.where(jm[None, :] < half, -sin[:, idx], 0.0)
    sb = jnp.where((jm[None, :] >= half) & (jm[None, :] < ROT_DIM), sin[:, idx], 0.0)
    return c.astype(F32), sa.astype(F32), sb.astype(F32)


def _rope(x, c, sa, sb):
    return x * c + pltpu.roll(x, LANES - 8, 1) * sa + pltpu.roll(x, 8, 1) * sb


def _rope_t(d, c, sa, sb):
    return d * c + pltpu.roll(d * sa, 8, 1) + pltpu.roll(d * sb, LANES - 8, 1)


def rmsnorm(h, g, name):
    s = h.shape[0]
    tm = _tile(s, (1024, 512))

    def body(h_ref, g_ref, o_ref):
        x = h_ref[...]
        r = lax.rsqrt(jnp.mean(x * x, axis=-1, keepdims=True) + RMS_EPS)
        o_ref[...] = (x * r * g_ref[...]).astype(o_ref.dtype)

    return pl.pallas_call(
        body, grid=(s // tm,), name=name,
        in_specs=[_rowspec(tm, D_MODEL), _fullspec((1, D_MODEL))],
        out_specs=_rowspec(tm, D_MODEL),
        out_shape=jax.ShapeDtypeStruct((s, D_MODEL), ACT),
        compiler_params=_cp("parallel"))(h, g)


def _fold_perm(tm, d, inverse, cache):
    key = (tm, d, inverse)
    if key not in cache:
        m = tm // d
        a = lax.broadcasted_iota(jnp.int32, (tm, tm), 1 if inverse else 0)
        b = lax.broadcasted_iota(jnp.int32, (tm, tm), 0 if inverse else 1)
        src = (a & (m - 1)) * d + (a >> (m.bit_length() - 1))
        cache[key] = (b == src).astype(ACT)
    return cache[key]


FOLD_ROWS = 256


def _fold_rows(x, d, cache):
    m = FOLD_ROWS // d
    out = []
    for r0 in range(0, x.shape[0], FOLD_ROWS):
        p = _dot(_fold_perm(FOLD_ROWS, d, False, cache), x[r0:r0 + FOLD_ROWS], NN).astype(ACT)
        out.append(jnp.concatenate([p[r * m:(r + 1) * m] for r in range(d)], axis=1))
    return out[0] if len(out) == 1 else jnp.concatenate(out, axis=0)


def _unfold_rows(blk, d, cache):
    w = blk.shape[1] // d
    m = FOLD_ROWS // d
    out = []
    for r0 in range(0, blk.shape[0], m):
        stacked = jnp.concatenate([blk[r0:r0 + m, r * w:(r + 1) * w] for r in range(d)], axis=0)
        out.append(_dot(_fold_perm(FOLD_ROWS, d, True, cache), stacked, NN))
    return out[0] if len(out) == 1 else jnp.concatenate(out, axis=0)


def _unfold_rows_f32(blk, d, cache):
    hi = blk.astype(ACT)
    r1 = blk - hi.astype(F32)
    mid = r1.astype(ACT)
    lo = (r1 - mid.astype(F32)).astype(ACT)
    return _unfold_rows(hi, d, cache) + _unfold_rows(mid, d, cache) + _unfold_rows(lo, d, cache)


def inproj(n, w, tabs, nqk, splits, fold_dils, name):
    s = n.shape[0]
    ntot = w.shape[1]
    assert sum(splits) == ntot and splits[0] == nqk
    tm = _tile(s, (512,))
    ns = len(splits)

    def body(n_ref, w_ref, c_ref, sa_ref, sb_ref, *outs):
        res = _dot(n_ref[...], w_ref[...], NN)
        c, sa, sb = c_ref[...], sa_ref[...], sb_ref[...]
        for g in range(nqk // LANES):
            x = res[:, g * LANES:(g + 1) * LANES]
            outs[0][:, g * LANES:(g + 1) * LANES] = _rope(x, c, sa, sb).astype(ACT)
        off = nqk
        for o_ref, wd in zip(outs[1:ns], splits[1:]):
            o_ref[...] = res[:, off:off + wd].astype(ACT)
            off += wd
        cache = {}
        for i, d in enumerate(fold_dils):
            outs[ns + 2 * i][...] = _fold_rows(outs[0][...], d, cache)
            outs[ns + 2 * i + 1][...] = _fold_rows(outs[1][...], d, cache)

    out_specs = [_rowspec(tm, wd) for wd in splits]
    out_shape = [jax.ShapeDtypeStruct((s, wd), ACT) for wd in splits]
    for d in fold_dils:
        for wd in splits[:2]:
            out_specs.append(_rowspec(tm // d, d * wd))
            out_shape.append(jax.ShapeDtypeStruct((s // d, d * wd), ACT))
    return pl.pallas_call(
        body, grid=(s // tm,), name=name,
        in_specs=[_rowspec(tm, D_MODEL), _fullspec((D_MODEL, ntot))] + [_rowspec(tm, LANES)] * 3,
        out_specs=out_specs, out_shape=out_shape,
        compiler_params=_cp("parallel"))(n, w, *tabs)


def ffn_up(n, wg, wu, name):
    s = n.shape[0]
    f = wg.shape[1]
    tm = _tile(s, (1024, 512))
    tf = _tile(f, (1408, 512, 256, 128))

    def body(n_ref, wg_ref, wu_ref, g_ref, u_ref, a_ref):
        a = n_ref[...]
        g = _dot(a, wg_ref[...], NN)
        u = _dot(a, wu_ref[...], NN)
        sg = _sigmoid(g)
        silu = g * sg
        g_ref[...] = (u * (sg * (1.0 + g * (1.0 - sg)))).astype(ACT)
        u_ref[...] = silu.astype(ACT)
        a_ref[...] = (silu * u).astype(ACT)

    wspec = pl.BlockSpec((D_MODEL, tf), lambda j, i: (0, j))
    ospec = pl.BlockSpec((tm, tf), lambda j, i: (i, j))
    return pl.pallas_call(
        body, grid=(f // tf, s // tm), name=name,
        in_specs=[pl.BlockSpec((tm, D_MODEL), lambda j, i: (i, 0)), wspec, wspec],
        out_specs=[ospec] * 3,
        out_shape=[jax.ShapeDtypeStruct((s, f), ACT)] * 3,
        compiler_params=_cp("parallel", "parallel"))(n, wg, wu)


def mm_res(parts, w, h, gnext, name):
    s = h.shape[0]
    tm = _tile(s, (1024, 512))
    widths = [p.shape[1] for p in parts]
    assert sum(widths) == w.shape[0]
    np_ = len(parts)

    def body(*refs):
        p_refs = refs[:np_]
        w_ref, h_ref = refs[np_], refs[np_ + 1]
        rest = refs[np_ + 2:]
        acc = h_ref[...]
        off = 0
        for p_ref, wd in zip(p_refs, widths):
            acc = acc + _dot(p_ref[...], w_ref[off:off + wd, :], NN)
            off += wd
        if gnext is None:
            rest[0][...] = acc
        else:
            g_ref, ho_ref, no_ref = rest
            ho_ref[...] = acc
            r = lax.rsqrt(jnp.mean(acc * acc, axis=-1, keepdims=True) + RMS_EPS)
            no_ref[...] = (acc * r * g_ref[...]).astype(ACT)

    wspec = pl.BlockSpec(w.shape, lambda i: (0, 0), pipeline_mode=pl.Buffered(1))
    in_specs = [_rowspec(tm, wd) for wd in widths] + [wspec, _rowspec(tm, D_MODEL)]
    args = list(parts) + [w, h]
    out_specs = [_rowspec(tm, D_MODEL)]
    out_shape = [jax.ShapeDtypeStruct((s, D_MODEL), F32)]
    if gnext is not None:
        in_specs.append(_fullspec((1, D_MODEL)))
        args.append(gnext)
        out_specs.append(_rowspec(tm, D_MODEL))
        out_shape.append(jax.ShapeDtypeStruct((s, D_MODEL), ACT))
    out = pl.pallas_call(
        body, grid=(s // tm,), name=name, in_specs=in_specs, out_specs=out_specs,
        out_shape=out_shape, compiler_params=_cp("parallel"))(*args)
    return (out[0], None) if gnext is None else (out[0], out[1])


def _band_mask(n, max_dist):
    qi = lax.broadcasted_iota(jnp.int32, (BLOCK, 2 * BLOCK), 0)
    kj = lax.broadcasted_iota(jnp.int32, (BLOCK, 2 * BLOCK), 1)
    dist = qi + BLOCK - kj
    valid = jnp.logical_and(dist >= 0, dist <= max_dist)
    return jnp.logical_and(valid, jnp.logical_or(kj >= BLOCK, n > 0))


def _band_mask_t(n, max_dist):
    kj = lax.broadcasted_iota(jnp.int32, (2 * BLOCK, BLOCK), 0)
    qi = lax.broadcasted_iota(jnp.int32, (2 * BLOCK, BLOCK), 1)
    dist = qi + BLOCK - kj
    valid = jnp.logical_and(dist >= 0, dist <= max_dist)
    return jnp.logical_and(valid, jnp.logical_or(kj >= BLOCK, n > 0))


ATT_FWD_BLOCKS = 4


def _fold(a, d):
    return a.reshape(a.shape[0] // d, d * a.shape[1])


def attn_fwd(qk, v, *, d, hkv, max_dist, sink, out_dtype, name, folded=None, keep_folded=False):
    s = qk.shape[0]
    kvw = hkv * HEAD_DIM
    wqk = ATT_W + kvw
    assert qk.shape[1] == wqk and (d == 1 or (wqk % ATT_W == 0 and wqk % kvw == 0))
    assert sink is None or max_dist == BLOCK - 1
    nb = s // d // BLOCK
    grp = N_HEADS // hkv
    qpb, kpb, koff = wqk // ATT_W, wqk // kvw, ATT_W // kvw

    qb = ATT_FWD_BLOCKS if nb % ATT_FWD_BLOCKS == 0 else 1

    def body(*refs):
        if sink is None:
            q_ref, kc_ref, kp_ref, vc_ref, vp_ref, o_ref, l_ref, o_buf = refs
        else:
            q_ref, kc_ref, kp_ref, vc_ref, vp_ref, s_ref, o_ref, l_ref, o_buf = refs
        n = pl.program_id(1)
        qsl = [slice(h * HEAD_DIM, (h + 1) * HEAD_DIM) for h in range(N_HEADS)]
        ksl = [slice((h // grp) * HEAD_DIM, (h // grp + 1) * HEAD_DIM) for h in range(N_HEADS)]
        head_row = lax.broadcasted_iota(jnp.int32, (SUBLANES, BLOCK), 0)
        if sink is not None:
            sink_row = lax.broadcasted_iota(jnp.int32, (2 * BLOCK, BLOCK), 0) == 0
        for j in range(qb):
            rows = slice(j * BLOCK, (j + 1) * BLOCK)
            before = slice((j - 1) * BLOCK, j * BLOCK)
            valid = _band_mask_t(n * qb + j, max_dist)
            kk = jnp.concatenate([kp_ref[...] if j == 0 else kc_ref[before, :], kc_ref[rows, :]], axis=0)
            vv = jnp.concatenate([vp_ref[...] if j == 0 else vc_ref[before, :], vc_ref[rows, :]], axis=0)
            scores = []
            for h in range(N_HEADS):
                q = q_ref[rows, qsl[h]] * 0.125
                scores.append(_dot(kk[:, ksl[h]], q, NT))
            probs = []
            lse8 = jnp.zeros((SUBLANES, BLOCK), F32)
            for h in range(N_HEADS):
                sc = jnp.where(valid, scores[h], NEG)
                if sink is not None:
                    sc = jnp.where(sink_row, s_ref[h:h + 1, 0:1], sc)
                m = jnp.max(sc, axis=0, keepdims=True)
                p = jnp.exp(sc - m)
                l = jnp.sum(p, axis=0, keepdims=True)
                if sink is not None:
                    p = jnp.where(sink_row, 0.0, p)
                lse8 = jnp.where(head_row == h, m + jnp.log(l), lse8)
                probs.append((p * (1.0 / l)).astype(ACT))
            l_ref[rows, :] = jnp.concatenate([lse8, jnp.zeros((BLOCK - SUBLANES, BLOCK), F32)], axis=0).T
            for h in range(N_HEADS):
                o_buf[rows, qsl[h]] = _dot(probs[h], vv[:, ksl[h]], TN)
        o_ref[...] = o_buf[...].astype(o_ref.dtype)

    prev = lambda n: jnp.maximum(n * qb - 1, 0)
    in_specs = [
        pl.BlockSpec((qb * BLOCK, ATT_W), lambda r, n: (n, r * qpb)),
        pl.BlockSpec((qb * BLOCK, kvw), lambda r, n: (n, r * kpb + koff)),
        pl.BlockSpec((BLOCK, kvw), lambda r, n: (prev(n), r * kpb + koff)),
        pl.BlockSpec((qb * BLOCK, kvw), lambda r, n: (n, r)),
        pl.BlockSpec((BLOCK, kvw), lambda r, n: (prev(n), r)),
    ]
    qkf, vf = (_fold(qk, d), _fold(v, d)) if folded is None else folded
    args = [qkf, qkf, qkf, vf, vf]
    if sink is not None:
        in_specs.append(_fullspec((N_HEADS, LANES)))
        args.append(sink)
    ospec = pl.BlockSpec((qb * BLOCK, ATT_W), lambda r, n: (n, r))
    lspec = pl.BlockSpec((qb * BLOCK, LANES), lambda r, n: (n, r))
    o, lse = pl.pallas_call(
        body, grid=(d, nb // qb), name=name, in_specs=in_specs, out_specs=[ospec, lspec],
        out_shape=[jax.ShapeDtypeStruct((s // d, d * ATT_W), out_dtype),
                   jax.ShapeDtypeStruct((s // d, d * LANES), F32)],
        scratch_shapes=[pltpu.VMEM((qb * BLOCK, ATT_W), F32)],
        compiler_params=_cp("parallel", "parallel"))(*args)
    return (o, lse) if keep_folded else (o.reshape(s, ATT_W), lse.reshape(s, LANES))


def attn_bwd(qk, v, do_src, do_col, o, lse, *, d, hkv, max_dist, sink, out_dtype, name, folded=None,
             folded_do_o=None, keep_folded=False):
    s = qk.shape[0]
    kvw = hkv * HEAD_DIM
    wqk = ATT_W + kvw
    nb = s // d // BLOCK
    grp = N_HEADS // hkv
    qpb, kpb, koff = wqk // ATT_W, wqk // kvw, ATT_W // kvw
    dob = do_src.shape[1] // ATT_W
    has_sink = sink is not None

    def body(*refs):
        refs = list(refs)
        q_ref, kc_ref, kp_ref, vc_ref, vp_ref, do_ref, o_ref, l_ref = refs[:8]
        pos = 8
        if has_sink:
            s_ref = refs[pos]
            pos += 1
        dq_ref, dk_ref, dv_ref = refs[pos:pos + 3]
        pos += 3
        if has_sink:
            ds_ref = refs[pos]
            pos += 1
        ck_ref, cv_ref, dq_buf, dk_buf, dv_buf = refs[pos:pos + 5]
        r_id = pl.program_id(0)
        n = pl.program_id(1)

        @pl.when(n == 0)
        def _():
            ck_ref[...] = jnp.zeros_like(ck_ref)
            cv_ref[...] = jnp.zeros_like(cv_ref)

        if has_sink:
            @pl.when(jnp.logical_and(n == 0, r_id == 0))
            def _():
                ds_ref[...] = jnp.zeros_like(ds_ref)

        @pl.when(n < nb)
        def _():
            valid = _band_mask_t(n, max_dist)
            qsl = [slice(h * HEAD_DIM, (h + 1) * HEAD_DIM) for h in range(N_HEADS)]
            ksl = [slice((h // grp) * HEAD_DIM, (h // grp + 1) * HEAD_DIM) for h in range(N_HEADS)]
            kk = jnp.concatenate([kp_ref[...], kc_ref[...]], axis=0)
            vv = jnp.concatenate([vp_ref[...], vc_ref[...]], axis=0)
            qs, first = [], []
            for h in range(N_HEADS):
                q = q_ref[:, qsl[h]] * 0.125
                qs.append(q)
                first.append((_dot(kk[:, ksl[h]], q, NT), _dot(vv[:, ksl[h]], do_ref[:, qsl[h]], NT)))
            lse_t = l_ref[...].T
            prod = do_ref[...].astype(F32) * o_ref[...].astype(F32)
            hi = prod.astype(ACT)
            lo = (prod - hi.astype(F32)).astype(ACT)
            col = lax.broadcasted_iota(jnp.int32, (LANES, ATT_W), 1)
            row = lax.broadcasted_iota(jnp.int32, (LANES, ATT_W), 0)
            head_of = jnp.logical_and(col >= row * HEAD_DIM, col < (row + 1) * HEAD_DIM).astype(ACT)
            e_t = _dot(head_of, hi, NT) + _dot(head_of, lo, NT)
            mid = []
            for h in range(N_HEADS):
                s_t, dp_t = first[h]
                lse_h, e_h = lse_t[h:h + 1, :], e_t[h:h + 1, :]
                p_t = jnp.exp(jnp.where(valid, s_t, NEG) - lse_h)
                mid.append(((p_t * (dp_t - e_h)).astype(ACT), p_t.astype(ACT)))
                if has_sink:
                    sk = s_ref[h:h + 1, 0:1]
                    dsk = -jnp.sum(jnp.exp(sk - lse_h) * e_h, axis=1, keepdims=True)
                    ds_ref[h:h + 1, :] = ds_ref[h:h + 1, :] + dsk
            dkk = [None] * hkv
            dvv = [None] * hkv
            for h in range(N_HEADS):
                kh = h // grp
                ds_t, p_t = mid[h]
                dq_buf[:, qsl[h]] = _dot(ds_t, kk[:, ksl[h]], TN) * 0.125
                for lst, val in ((dkk, _dot(ds_t, qs[h], NN)), (dvv, _dot(p_t, do_ref[:, qsl[h]], NN))):
                    lst[kh] = val if lst[kh] is None else lst[kh] + val
            for kh in range(hkv):
                ks = slice(kh * HEAD_DIM, (kh + 1) * HEAD_DIM)
                dk_buf[:, ks] = ck_ref[:, ks] + dkk[kh][:BLOCK]
                dv_buf[:, ks] = cv_ref[:, ks] + dvv[kh][:BLOCK]
                ck_ref[:, ks] = dkk[kh][BLOCK:]
                cv_ref[:, ks] = dvv[kh][BLOCK:]
            dq_ref[...] = dq_buf[...].astype(dq_ref.dtype)
            dk_ref[...] = dk_buf[...].astype(dk_ref.dtype)
            dv_ref[...] = dv_buf[...].astype(dv_ref.dtype)

        @pl.when(n == nb)
        def _():
            dk_ref[...] = ck_ref[...].astype(dk_ref.dtype)
            dv_ref[...] = cv_ref[...].astype(dv_ref.dtype)

    qrow = lambda n: jnp.minimum(n, nb - 1)
    prow = lambda n: jnp.maximum(jnp.minimum(n, nb - 1) - 1, 0)
    krow = lambda n: jnp.maximum(n - 1, 0)
    in_specs = [
        pl.BlockSpec((BLOCK, ATT_W), lambda r, n: (qrow(n), r * qpb)),
        pl.BlockSpec((BLOCK, kvw), lambda r, n: (qrow(n), r * kpb + koff)),
        pl.BlockSpec((BLOCK, kvw), lambda r, n: (prow(n), r * kpb + koff)),
        pl.BlockSpec((BLOCK, kvw), lambda r, n: (qrow(n), r)),
        pl.BlockSpec((BLOCK, kvw), lambda r, n: (prow(n), r)),
        pl.BlockSpec((BLOCK, ATT_W), lambda r, n: (qrow(n), r * dob + do_col)),
        pl.BlockSpec((BLOCK, ATT_W), lambda r, n: (qrow(n), r)),
        pl.BlockSpec((BLOCK, LANES), lambda r, n: (qrow(n), r)),
    ]
    qkf, vf = (_fold(qk, d), _fold(v, d)) if folded is None else folded
    dof, of = (_fold(do_src, d), _fold(o, d)) if folded_do_o is None else folded_do_o
    args = [qkf, qkf, qkf, vf, vf, dof, of, _fold(lse, d)]
    if has_sink:
        in_specs.append(_fullspec((N_HEADS, LANES)))
        args.append(sink)
    qspec = pl.BlockSpec((BLOCK, ATT_W), lambda r, n: (qrow(n), r))
    kspec = pl.BlockSpec((BLOCK, kvw), lambda r, n: (krow(n), r))
    out_specs = [qspec, kspec, kspec]
    out_shape = [jax.ShapeDtypeStruct((s // d, d * ATT_W), out_dtype),
                 jax.ShapeDtypeStruct((s // d, d * kvw), out_dtype),
                 jax.ShapeDtypeStruct((s // d, d * kvw), out_dtype)]
    if has_sink:
        out_specs.append(_fullspec((N_HEADS, LANES)))
        out_shape.append(jax.ShapeDtypeStruct((N_HEADS, LANES), F32))
    out = pl.pallas_call(
        body, grid=(d, nb + 1), name=name, in_specs=in_specs, out_specs=out_specs,
        out_shape=out_shape,
        scratch_shapes=[pltpu.VMEM((BLOCK, kvw), F32), pltpu.VMEM((BLOCK, kvw), F32),
                        pltpu.VMEM((BLOCK, ATT_W), F32), pltpu.VMEM((BLOCK, kvw), F32), pltpu.VMEM((BLOCK, kvw), F32)],
        compiler_params=_cp("arbitrary", "arbitrary"))(*args)
    res = list(out[:3]) if keep_folded else [out[0].reshape(s, ATT_W), out[1].reshape(s, kvw), out[2].reshape(s, kvw)]
    if has_sink:
        res.append(out[3])
    return res


def combine_fwd(os_, lses, dils, name):
    s = os_[0].shape[0] * dils[0]
    tm = _tile(s, (512,))
    fold_dils = [d for d in dils if d > 1]

    def body(o1, o2, o3, l1, l2, l3, c_ref, l_ref, *rest):
        c_buf = rest[-1]
        cache = {}
        o = [r[...].astype(F32) if d == 1 else _unfold_rows(r[...], d, cache) for r, d in zip((o1, o2, o3), dils)]
        a, b, c = [r[...] if d == 1 else _unfold_rows_f32(r[...], d, cache) for r, d in zip((l1, l2, l3), dils)]
        m = jnp.maximum(jnp.maximum(a, b), c)
        wa, wb, wc = jnp.exp(a - m), jnp.exp(b - m), jnp.exp(c - m)
        tot = wa + wb + wc
        l_ref[...] = m + jnp.log(tot)
        rt = 1.0 / tot
        wa, wb, wc = wa * rt, wb * rt, wc * rt
        for h in range(N_HEADS):
            cs = slice(h * HEAD_DIM, (h + 1) * HEAD_DIM)
            c_buf[:, cs] = (wa[:, h:h + 1] * o[0][:, cs] + wb[:, h:h + 1] * o[1][:, cs] + wc[:, h:h + 1] * o[2][:, cs])
        mix = c_buf[...].astype(ACT)
        c_ref[...] = mix
        for f_ref, d in zip(rest[:-1], fold_dils):
            f_ref[...] = _fold_rows(mix, d, cache)

    return pl.pallas_call(
        body, grid=(s // tm,), name=name,
        in_specs=[_rowspec(tm // d, d * ATT_W) for d in dils] + [_rowspec(tm // d, d * LANES) for d in dils],
        out_specs=[_rowspec(tm, ATT_W), _rowspec(tm, LANES)] + [_rowspec(tm // d, d * ATT_W) for d in fold_dils],
        out_shape=[jax.ShapeDtypeStruct((s, ATT_W), ACT), jax.ShapeDtypeStruct((s, LANES), F32)]
        + [jax.ShapeDtypeStruct((s // d, d * ATT_W), ACT) for d in fold_dils],
        scratch_shapes=[pltpu.VMEM((tm, ATT_W), F32)],
        compiler_params=_cp("parallel"))(*os_, *lses)


def assemble(parts, tabs, name):
    terms_of = [[t if isinstance(t, tuple) else (t, 1) for t in terms] for terms, _ in parts]
    flat = [t for ts in terms_of for t in ts]
    s = terms_of[0][0][0].shape[0] * terms_of[0][0][1]
    tm = _tile(s, (512,))
    widths = [ts[0][0].shape[1] // ts[0][1] for ts in terms_of]
    flags = [f for _, f in parts]

    def body(*refs):
        c_ref, sa_ref, sb_ref, o_ref = refs[len(flat):]
        c, sa, sb = c_ref[...], sa_ref[...], sb_ref[...]
        off = 0
        first = 0
        cache = {}
        for wd, ts, fl in zip(widths, terms_of, flags):
            t_refs = refs[first:first + len(ts)]
            first += len(ts)
            x = None
            for t_ref, (_, d) in zip(t_refs, ts):
                t = t_ref[...].astype(F32) if d == 1 else _unfold_rows(t_ref[...], d, cache)
                x = t if x is None else x + t
            for g in range(wd // LANES):
                cols = slice(g * LANES, (g + 1) * LANES)
                y = _rope_t(x[:, cols], c, sa, sb) if fl else x[:, cols]
                o_ref[:, off + g * LANES:off + (g + 1) * LANES] = y.astype(ACT)
            off += wd

    tot = sum(widths)
    return pl.pallas_call(
        body, grid=(s // tm,), name=name,
        in_specs=[_rowspec(tm // d, a.shape[1]) for a, d in flat] + [_rowspec(tm, LANES)] * 3,
        out_specs=_rowspec(tm, tot), out_shape=jax.ShapeDtypeStruct((s, tot), ACT),
        compiler_params=_cp("parallel"))(*[a for a, _ in flat], *tabs)


def _ln_stats(x):
    mu = jnp.mean(x, axis=-1, keepdims=True)
    xc = x - mu
    var = jnp.mean(xc * xc, axis=-1, keepdims=True)
    rstd = lax.rsqrt(var + LN_EPS)
    return xc * rstd, rstd


SUBLANES = 8


TAP_ROWS = 32


def _tap_sum(buf, cw_ref, offsets, tm, res_ref):
    for r0 in range(0, tm, TAP_ROWS):
        acc = None
        for ph in range(SUBLANES):
            taps = [j for j, off in enumerate(offsets) if off % SUBLANES == ph]
            if not taps:
                continue
            rows = TAP_ROWS if ph == 0 else TAP_ROWS + SUBLANES
            part = None
            for j in taps:
                term = cw_ref[j:j + 1, :] * buf[pl.ds(offsets[j] - ph + r0, rows), :]
                part = term if part is None else part + term
            part = part[ph:ph + TAP_ROWS]
            acc = part if acc is None else acc + part
        res_ref[pl.ds(r0, TAP_ROWS), :] = acc


def conv_fwd(ga, gb, cw, cb, lg, lb, name):
    s = ga.shape[0]
    tm = _tile(s, (512,))
    hb = tm // CONV_HALO

    def body(ga_ref, gb_ref, gah_ref, gbh_ref, cw_ref, cb_ref, lg_ref, lb_ref, c_ref, c1_ref, buf):
        i = pl.program_id(0)
        halo = gah_ref[...].astype(F32) * _sigmoid(gbh_ref[...].astype(F32))
        buf[0:CONV_HALO, :] = jnp.where(i > 0, halo, 0.0)
        buf[CONV_HALO:, :] = ga_ref[...].astype(F32) * _sigmoid(gb_ref[...].astype(F32))
        first = CONV_HALO - (CONV_WIDTH - 1)
        _tap_sum(buf, cw_ref, [first + j for j in range(CONV_WIDTH)], tm, c1_ref)
        acc = c1_ref[...] + cb_ref[...]
        c1_ref[...] = acc
        xh, _ = _ln_stats(acc)
        y = xh * lg_ref[...] + lb_ref[...]
        c_ref[...] = (y * _sigmoid(y)).astype(ACT)

    hspec = pl.BlockSpec((CONV_HALO, CONV_CH), lambda i: (jnp.maximum(i * hb - 1, 0), 0))
    vec = _fullspec((1, CONV_CH))
    spec = _rowspec(tm, CONV_CH)
    return pl.pallas_call(
        body, grid=(s // tm,), name=name,
        in_specs=[spec, spec, hspec, hspec, _fullspec((CONV_WIDTH, CONV_CH)), vec, vec, vec],
        out_specs=[spec, spec],
        out_shape=[jax.ShapeDtypeStruct((s, CONV_CH), ACT), jax.ShapeDtypeStruct((s, CONV_CH), F32)],
        scratch_shapes=[pltpu.VMEM((tm + CONV_HALO, CONV_CH), F32)],
        compiler_params=_cp("parallel"))(ga, gb, ga, gb, cw, cb, lg, lb)


def conv_bwd_ln(c1, dsrc, dcol, lg, lb, name):
    s = c1.shape[0]
    tm = _tile(s, (512,))

    def body(c1_ref, d_ref, lg_ref, lb_ref, o_ref, dg_ref, db_ref):
        @pl.when(pl.program_id(0) == 0)
        def _():
            dg_ref[...] = jnp.zeros_like(dg_ref)
            db_ref[...] = jnp.zeros_like(db_ref)

        xh, rstd = _ln_stats(c1_ref[...].astype(F32))
        y = xh * lg_ref[...] + lb_ref[...]
        sg = _sigmoid(y)
        dy = d_ref[...].astype(F32) * (sg * (1.0 + y * (1.0 - sg)))
        dg_ref[0:1, :] = dg_ref[0:1, :] + jnp.sum(dy * xh, axis=0, keepdims=True)
        db_ref[0:1, :] = db_ref[0:1, :] + jnp.sum(dy, axis=0, keepdims=True)
        dxh = dy * lg_ref[...]
        o_ref[...] = rstd * (dxh - jnp.mean(dxh, axis=-1, keepdims=True)
                             - xh * jnp.mean(dxh * xh, axis=-1, keepdims=True))

    vec = _fullspec((1, CONV_CH))
    acc = _fullspec((8, CONV_CH))
    return pl.pallas_call(
        body, grid=(s // tm,), name=name,
        in_specs=[_rowspec(tm, CONV_CH), _rowspec(tm, CONV_CH, dcol), vec, vec],
        out_specs=[_rowspec(tm, CONV_CH), acc, acc],
        out_shape=[jax.ShapeDtypeStruct((s, CONV_CH), F32)] + [jax.ShapeDtypeStruct((8, CONV_CH), F32)] * 2,
        compiler_params=_cp("arbitrary"))(c1, dsrc, lg, lb)


def conv_bwd_conv(dc1, ga, gb, cw, name):
    s = ga.shape[0]
    tm = _tile(s, (512,))
    hb = tm // CONV_HALO
    nt = s // tm
    last_h = s // CONV_HALO - 1
    first = CONV_HALO - (CONV_WIDTH - 1)


    def rows8(x):
        return jnp.sum(x.reshape(x.shape[0] // SUBLANES, SUBLANES, CONV_CH), axis=0)

    def body(d_ref, dn_ref, ga_ref, gb_ref, gah_ref, gbh_ref, cw_ref,
             dga_ref, dgb_ref, dw_ref, db_ref, dbuf, cbuf, sbuf):
        i = pl.program_id(0)

        @pl.when(i == 0)
        def _():
            dw_ref[...] = jnp.zeros_like(dw_ref)
            db_ref[...] = jnp.zeros_like(db_ref)

        d = d_ref[...]
        dbuf[0:tm, :] = d
        dbuf[tm:, :] = jnp.where(i < nt - 1, dn_ref[...], 0.0)
        halo = gah_ref[...].astype(F32) * _sigmoid(gbh_ref[...].astype(F32))
        cbuf[0:CONV_HALO, :] = jnp.where(i > 0, halo, 0.0)
        a = ga_ref[...].astype(F32)
        sg = _sigmoid(gb_ref[...].astype(F32))
        cbuf[CONV_HALO:, :] = a * sg
        for ph in range(SUBLANES):
            taps = [j for j in range(CONV_WIDTH) if (first + j) % SUBLANES == ph]
            if ph:
                sbuf[0:tm + CONV_HALO - SUBLANES, :] = cbuf[pl.ds(ph, tm + CONV_HALO - SUBLANES), :]
            src = sbuf if ph else cbuf
            for r0 in range(0, tm, TAP_ROWS):
                d_blk = dbuf[pl.ds(r0, TAP_ROWS), :]
                for j in taps:
                    tap = src[pl.ds(first + j - ph + r0, TAP_ROWS), :]
                    rows = slice(j * SUBLANES, (j + 1) * SUBLANES)
                    dw_ref[rows, :] = dw_ref[rows, :] + rows8(d_blk * tap)
        db_ref[...] = db_ref[...] + rows8(d)
        _tap_sum(dbuf, cw_ref, [CONV_WIDTH - 1 - j for j in range(CONV_WIDTH)], tm, sbuf)
        dc0 = sbuf[0:tm, :]
        dga_ref[...] = (dc0 * sg).astype(ACT)
        dgb_ref[...] = (dc0 * a * sg * (1.0 - sg)).astype(ACT)

    spec = _rowspec(tm, CONV_CH)
    hprev = pl.BlockSpec((CONV_HALO, CONV_CH), lambda i: (jnp.maximum(i * hb - 1, 0), 0))
    hnext = pl.BlockSpec((CONV_HALO, CONV_CH), lambda i: (jnp.minimum((i + 1) * hb, last_h), 0))
    return pl.pallas_call(
        body, grid=(nt,), name=name,
        in_specs=[spec, hnext, spec, spec, hprev, hprev, _fullspec((CONV_WIDTH, CONV_CH))],
        out_specs=[spec, spec, _fullspec((CONV_HALO * SUBLANES, CONV_CH)), _fullspec((SUBLANES, CONV_CH))],
        out_shape=[jax.ShapeDtypeStruct((s, CONV_CH), ACT)] * 2
        + [jax.ShapeDtypeStruct((CONV_HALO * SUBLANES, CONV_CH), F32), jax.ShapeDtypeStruct((SUBLANES, CONV_CH), F32)],
        scratch_shapes=[pltpu.VMEM((tm + CONV_HALO, CONV_CH), F32)] * 3,
        compiler_params=_cp("arbitrary"))(dc1, dc1, ga, gb, ga, gb, cw)


_GELU_K = math.sqrt(2.0 / math.pi)
_GELU_C = 0.044715


def _gelu(x):
    return 0.5 * x * (1.0 + jnp.tanh(_GELU_K * (x + _GELU_C * x * x * x)))


def _gelu_grad(x):
    t = jnp.tanh(_GELU_K * (x + _GELU_C * x * x * x))
    return 0.5 * (1.0 + t) + 0.5 * x * (1.0 - t * t) * _GELU_K * (1.0 + 3.0 * _GELU_C * x * x)


def _tril():
    qi = lax.broadcasted_iota(jnp.int32, (BLOCK, BLOCK), 0)
    kj = lax.broadcasted_iota(jnp.int32, (BLOCK, BLOCK), 1)
    return kj <= qi


GMLP_CHUNKS = 4


def _gmlp_weights(sw_ref, w_buf, wt_buf):
    tril = _tril()
    for g in range(D_GROUPS):
        w = jnp.where(tril, sw_ref[g], 0.0)
        w_buf[g] = w.astype(ACT)
        if wt_buf is not None:
            wt_buf[g] = w.T.astype(ACT)


def _gmlp_mix(w_buf, gn, sb_ref, m_buf):
    for c in range(GMLP_CHUNKS):
        rows = slice(c * BLOCK, (c + 1) * BLOCK)
        for g in range(D_GROUPS):
            cs = slice(g * HEAD_DIM, (g + 1) * HEAD_DIM)
            m_buf[rows, cs] = _dot(w_buf[g], gn[rows, cs], NN) + sb_ref[:, cs]


def gmlp_fwd(z, lg, lb, sw, sbx, name):
    s = z.shape[0]
    tm = GMLP_CHUNKS * BLOCK

    def body(z_ref, lg_ref, lb_ref, sw_ref, sb_ref, o_ref, w_buf, m_buf):
        @pl.when(pl.program_id(0) == 0)
        def _():
            _gmlp_weights(sw_ref, w_buf, None)

        zz = _gelu(z_ref[...].astype(F32))
        xh, _ = _ln_stats(zz[:, D_CH:])
        gn = (xh * lg_ref[...] + lb_ref[...]).astype(ACT)
        _gmlp_mix(w_buf, gn, sb_ref, m_buf)
        o_ref[...] = (zz[:, :D_CH] * m_buf[...]).astype(ACT)

    return pl.pallas_call(
        body, grid=(s // tm,), name=name,
        in_specs=[_rowspec(tm, 2 * D_CH), _fullspec((1, D_CH)), _fullspec((1, D_CH)),
                  _fullspec((D_GROUPS, BLOCK, BLOCK)), _fullspec((BLOCK, D_CH))],
        out_specs=_rowspec(tm, D_CH), out_shape=jax.ShapeDtypeStruct((s, D_CH), ACT),
        scratch_shapes=[pltpu.VMEM((D_GROUPS, BLOCK, BLOCK), ACT), pltpu.VMEM((tm, D_CH), F32)],
        compiler_params=_cp("arbitrary"))(z, lg, lb, sw, sbx)


def gmlp_bwd(z, dsrc, dcol, lg, lb, sw, sbx, name):
    s = z.shape[0]
    tm = GMLP_CHUNKS * BLOCK

    def body(z_ref, d_ref, lg_ref, lb_ref, sw_ref, sb_ref, dz_ref, dw_ref, dsb_ref, dg_ref, db_ref,
             w_buf, wt_buf, m_buf, dgn_buf):
        @pl.when(pl.program_id(0) == 0)
        def _():
            _gmlp_weights(sw_ref, w_buf, wt_buf)
            dw_ref[...] = jnp.zeros_like(dw_ref)
            dsb_ref[...] = jnp.zeros_like(dsb_ref)
            dg_ref[...] = jnp.zeros_like(dg_ref)
            db_ref[...] = jnp.zeros_like(db_ref)

        zr = z_ref[...].astype(F32)
        zz = _gelu(zr)
        u = zz[:, :D_CH]
        xh, rstd = _ln_stats(zz[:, D_CH:])
        gn = (xh * lg_ref[...] + lb_ref[...]).astype(ACT)
        dd = d_ref[...].astype(F32)
        _gmlp_mix(w_buf, gn, sb_ref, m_buf)
        dz_ref[:, :D_CH] = (dd * m_buf[...] * _gelu_grad(zr[:, :D_CH])).astype(ACT)
        dmix = dd * u
        dmix_a = dmix.astype(ACT)
        dsb = dmix[0:BLOCK]
        for c in range(1, GMLP_CHUNKS):
            dsb = dsb + dmix[c * BLOCK:(c + 1) * BLOCK]
        dsb_ref[...] = dsb_ref[...] + dsb
        tril = _tril()
        for g in range(D_GROUPS):
            cs = slice(g * HEAD_DIM, (g + 1) * HEAD_DIM)
            dw = None
            for c in range(GMLP_CHUNKS):
                rows = slice(c * BLOCK, (c + 1) * BLOCK)
                t = _dot(dmix_a[rows, cs], gn[rows, cs], NT)
                dw = t if dw is None else dw + t
                dgn_buf[rows, cs] = _dot(wt_buf[g], dmix_a[rows, cs], NN)
            dw_ref[g] = dw_ref[g] + jnp.where(tril, dw, 0.0)
        dgn = dgn_buf[...]
        dg_ref[0:1, :] = dg_ref[0:1, :] + jnp.sum(dgn * xh, axis=0, keepdims=True)
        db_ref[0:1, :] = db_ref[0:1, :] + jnp.sum(dgn, axis=0, keepdims=True)
        dxh = dgn * lg_ref[...]
        dgate = rstd * (dxh - jnp.mean(dxh, axis=-1, keepdims=True)
                        - xh * jnp.mean(dxh * xh, axis=-1, keepdims=True))
        dz_ref[:, D_CH:] = (dgate * _gelu_grad(zr[:, D_CH:])).astype(ACT)

    vec = _fullspec((1, D_CH))
    acc = _fullspec((8, D_CH))
    wshape = (D_GROUPS, BLOCK, BLOCK)
    return pl.pallas_call(
        body, grid=(s // tm,), name=name,
        in_specs=[_rowspec(tm, 2 * D_CH), _rowspec(tm, D_CH, dcol), vec, vec, _fullspec(wshape),
                  _fullspec((BLOCK, D_CH))],
        out_specs=[_rowspec(tm, 2 * D_CH), _fullspec(wshape), _fullspec((BLOCK, D_CH)), acc, acc],
        out_shape=[jax.ShapeDtypeStruct((s, 2 * D_CH), ACT), jax.ShapeDtypeStruct(wshape, F32),
                   jax.ShapeDtypeStruct((BLOCK, D_CH), F32),
                   jax.ShapeDtypeStruct((8, D_CH), F32), jax.ShapeDtypeStruct((8, D_CH), F32)],
        scratch_shapes=[pltpu.VMEM(wshape, ACT), pltpu.VMEM(wshape, ACT),
                        pltpu.VMEM((tm, D_CH), F32), pltpu.VMEM((tm, D_CH), F32)],
        compiler_params=_cp("arbitrary"))(z, dsrc, lg, lb, sw, sbx)


def _rms_bwd(dn, x, g):
    r = lax.rsqrt(jnp.mean(x * x, axis=-1, keepdims=True) + RMS_EPS)
    u = dn * g
    dx = r * u - x * (r * r * r) * jnp.mean(x * u, axis=-1, keepdims=True)
    return dx, dn * x * r


def final_loss(h, g, target, name):
    s = h.shape[0]
    tm = _tile(s, (1024, 512))

    def body(h_ref, g_ref, t_ref, loss_ref, dh_ref, dg_ref):
        @pl.when(pl.program_id(0) == 0)
        def _():
            loss_ref[...] = jnp.zeros_like(loss_ref)
            dg_ref[...] = jnp.zeros_like(dg_ref)

        x = h_ref[...]
        r = lax.rsqrt(jnp.mean(x * x, axis=-1, keepdims=True) + RMS_EPS)
        diff = x * r * g_ref[...] - t_ref[...]
        part = jnp.sum(jnp.sum(diff * diff, axis=-1, keepdims=True), axis=0, keepdims=True)
        loss_ref[...] = loss_ref[...] + part * (0.5 / D_MODEL)
        dx, dgt = _rms_bwd(diff * (1.0 / D_MODEL), x, g_ref[...])
        dh_ref[...] = dx
        dg_ref[0:1, :] = dg_ref[0:1, :] + jnp.sum(dgt, axis=0, keepdims=True)

    spec = _rowspec(tm, D_MODEL)
    return pl.pallas_call(
        body, grid=(s // tm,), name=name,
        in_specs=[spec, _fullspec((1, D_MODEL)), spec],
        out_specs=[_fullspec((8, LANES)), spec, _fullspec((8, D_MODEL))],
        out_shape=[jax.ShapeDtypeStruct((8, LANES), F32), jax.ShapeDtypeStruct((s, D_MODEL), F32),
                   jax.ShapeDtypeStruct((8, D_MODEL), F32)],
        compiler_params=_cp("arbitrary"))(h, g, target)


def mm_nt(dy, w, name, fold_dils=(), fold_cols=0):
    s, n = dy.shape
    k = w.shape[0]
    tm = _tile(s, (512,) if fold_dils else (1024, 512))
    tk = k if fold_dils else _tile(k, (1024, 512))

    def body(d_ref, w_ref, o_ref, *f_refs):
        res = _dot(d_ref[...].astype(ACT), w_ref[...], NT).astype(ACT)
        o_ref[...] = res
        cache = {}
        for f_ref, d in zip(f_refs, fold_dils):
            f_ref[...] = _fold_rows(res[:, :fold_cols], d, cache)

    out = pl.pallas_call(
        body, grid=(k // tk, s // tm), name=name,
        in_specs=[pl.BlockSpec((tm, n), lambda j, i: (i, 0)), pl.BlockSpec((tk, n), lambda j, i: (j, 0))],
        out_specs=[pl.BlockSpec((tm, tk), lambda j, i: (i, j))]
        + [pl.BlockSpec((tm // d, d * fold_cols), lambda j, i: (i, 0)) for d in fold_dils],
        out_shape=[jax.ShapeDtypeStruct((s, k), ACT)]
        + [jax.ShapeDtypeStruct((s // d, d * fold_cols), ACT) for d in fold_dils],
        compiler_params=_cp("parallel", "parallel"))(dy, w)
    return out if fold_dils else out[0]


def ffn_down_bwd(dh, wd, dact_dgate, dact_dup, name):
    s = dh.shape[0]
    f = wd.shape[0]
    tm = _tile(s, (1024, 512))
    tf = _tile(f, (1408, 512, 256, 128))

    def body(d_ref, w_ref, g_ref, u_ref, dg_ref, du_ref):
        dact = _dot(d_ref[...].astype(ACT), w_ref[...], NT)
        dg_ref[...] = (dact * g_ref[...].astype(F32)).astype(ACT)
        du_ref[...] = (dact * u_ref[...].astype(F32)).astype(ACT)

    tile = pl.BlockSpec((tm, tf), lambda j, i: (i, j))
    return pl.pallas_call(
        body, grid=(f // tf, s // tm), name=name,
        in_specs=[pl.BlockSpec((tm, D_MODEL), lambda j, i: (i, 0)),
                  pl.BlockSpec((tf, D_MODEL), lambda j, i: (j, 0)), tile, tile],
        out_specs=[tile, tile], out_shape=[jax.ShapeDtypeStruct((s, f), ACT)] * 2,
        compiler_params=_cp("parallel", "parallel"))(dh, wd, dact_dgate, dact_dup)


def mm_nt_rms(parts, h, g, dh, name):
    s = h.shape[0]
    tm = _tile(s, (512,))
    wspec = lambda w: pl.BlockSpec(w.shape, lambda i: (0, 0), pipeline_mode=pl.Buffered(1))
    np_ = len(parts)

    def body(*refs):
        d_refs = refs[:np_]
        w_refs = refs[np_:2 * np_]
        h_ref, g_ref, dh_ref, o_ref, dg_ref = refs[2 * np_:]

        @pl.when(pl.program_id(0) == 0)
        def _():
            dg_ref[...] = jnp.zeros_like(dg_ref)

        dn = None
        for d_ref, w_ref in zip(d_refs, w_refs):
            t = _dot(d_ref[...], w_ref[...], NT)
            dn = t if dn is None else dn + t
        dx, dgt = _rms_bwd(dn, h_ref[...], g_ref[...])
        o_ref[...] = dh_ref[...] + dx
        dg_ref[0:1, :] = dg_ref[0:1, :] + jnp.sum(dgt, axis=0, keepdims=True)

    spec = _rowspec(tm, D_MODEL)
    return pl.pallas_call(
        body, grid=(s // tm,), name=name,
        in_specs=[_rowspec(tm, d.shape[1]) for d, _ in parts] + [wspec(w) for _, w in parts]
        + [spec, _fullspec((1, D_MODEL)), spec],
        out_specs=[spec, _fullspec((8, D_MODEL))],
        out_shape=[jax.ShapeDtypeStruct((s, D_MODEL), F32), jax.ShapeDtypeStruct((8, D_MODEL), F32)],
        compiler_params=_cp("arbitrary"))(*[d for d, _ in parts], *[w for _, w in parts], h, g, dh)


def mm_tn_pair(a, b1, b2, name):
    s, k = a.shape
    n = b1.shape[1]
    tk = _tile(k, (512, 256, 128))
    tn = _tile(n, (1408, 1280, 1024, 896, 512, 256, 128))
    ts = _tile(s, (2048, 512))
    nt = s // ts

    def body(a_ref, b1_ref, b2_ref, o1_ref, o2_ref, acc1, acc2):
        at = a_ref[...].T
        t1 = _dot(at, b1_ref[...], NN)
        t2 = _dot(at, b2_ref[...], NN)
        step = pl.program_id(2)

        @pl.when(step == 0)
        def _():
            acc1[...] = t1
            acc2[...] = t2

        @pl.when(step > 0)
        def _():
            acc1[...] = acc1[...] + t1
            acc2[...] = acc2[...] + t2

        @pl.when(step == nt - 1)
        def _():
            o1_ref[...] = acc1[...].astype(ACT)
            o2_ref[...] = acc2[...].astype(ACT)

    bspec = pl.BlockSpec((ts, tn), lambda i, j, t: (t, j))
    ospec = pl.BlockSpec((tk, tn), lambda i, j, t: (i, j))
    return pl.pallas_call(
        body, grid=(k // tk, n // tn, nt), name=name,
        in_specs=[pl.BlockSpec((ts, tk), lambda i, j, t: (t, i)), bspec, bspec],
        out_specs=[ospec, ospec], out_shape=[jax.ShapeDtypeStruct((k, n), ACT)] * 2,
        scratch_shapes=[pltpu.VMEM((tk, tn), F32)] * 2,
        compiler_params=_cp("parallel", "parallel", "arbitrary"))(a, b1, b2)


def mm_tn(a, b, name):
    s, k = a.shape
    n = b.shape[1]
    tk = _tile(k, (1024, 1408, 512, 256, 128))
    tn = _tile(n, (1408, 1280, 1024, 896, 512, 256, 128))
    ts = _tile(s, (2048, 512) if b.dtype == ACT else (1024, 512))
    nt = s // ts

    def body(a_ref, b_ref, o_ref, acc):
        t = _dot(a_ref[...].astype(ACT), b_ref[...].astype(ACT), TN)
        step = pl.program_id(2)

        @pl.when(step == 0)
        def _():
            acc[...] = t

        @pl.when(step > 0)
        def _():
            acc[...] = acc[...] + t

        @pl.when(step == nt - 1)
        def _():
            o_ref[...] = acc[...].astype(ACT)

    return pl.pallas_call(
        body, grid=(k // tk, n // tn, nt), name=name,
        in_specs=[pl.BlockSpec((ts, tk), lambda i, j, t: (t, i)), pl.BlockSpec((ts, tn), lambda i, j, t: (t, j))],
        out_specs=pl.BlockSpec((tk, tn), lambda i, j, t: (i, j)),
        out_shape=jax.ShapeDtypeStruct((k, n), ACT),
        scratch_shapes=[pltpu.VMEM((tk, tn), F32)],
        compiler_params=_cp("parallel", "parallel", "arbitrary"))(a, b)


def _adamw_math(w, g, m, v):
    m = ADAM_B1 * m + (1.0 - ADAM_B1) * g
    v = ADAM_B2 * v + (1.0 - ADAM_B2) * (g * g)
    m_hat = m / (1.0 - ADAM_B1 ** ADAM_STEP)
    v_hat = v / (1.0 - ADAM_B2 ** ADAM_STEP)
    delta = -ADAM_LR * (m_hat / (jnp.sqrt(v_hat) + ADAM_EPS) + ADAM_WD * w)
    return delta, m, v


def sum_adamw(parts, w, m, v, layer, others, name):
    nl, r, c = w.shape
    tr = _tile(r, (256, 128, 64, 32, 16, 8))

    def body(p_ref, w_ref, m_ref, v_ref, *rest):
        g_ref, d_ref, mo_ref, vo_ref = rest[-4:]
        g = p_ref[0].astype(F32)
        for i in range(1, N_DEV):
            g = g + p_ref[i].astype(F32)
        d, mm, vv = _adamw_math(w_ref[...], g, m_ref[...], v_ref[...])
        g_ref[...] = g
        d_ref[...] = d
        mo_ref[...] = mm
        vo_ref[...] = vv

    spec = pl.BlockSpec((None, tr, c), lambda i: (layer, i, 0))
    in_specs = [pl.BlockSpec((N_DEV, tr, c), lambda i: (0, i, 0))] + [spec] * 3
    args = [parts, w, m, v]
    aliases = {}
    if others is not None:
        in_specs += [pl.BlockSpec(memory_space=pl.ANY)] * 4
        args += list(others)
        aliases = {4 + j: j for j in range(4)}
    return pl.pallas_call(
        body, grid=(r // tr,), name=name, in_specs=in_specs, out_specs=[spec] * 4,
        out_shape=[jax.ShapeDtypeStruct((nl, r, c), F32)] * 4, input_output_aliases=aliases,
        compiler_params=_cp("parallel"))(*args)


def cast_layers(items, name):
    n = len(items)

    def body(*refs):
        for i in range(n):
            refs[n + i][...] = refs[i][...].astype(ACT)

    return pl.pallas_call(
        body, grid=(1,), name=name,
        in_specs=[pl.BlockSpec((None,) + w.shape[1:], lambda i, l=l: (l, 0, 0)) for w, l in items],
        out_specs=[_fullspec(w.shape[1:]) for w, _ in items],
        out_shape=[jax.ShapeDtypeStruct(w.shape[1:], ACT) for w, _ in items],
        compiler_params=_cp("arbitrary"))(*[w for w, _ in items])


def adamw_small(ws, gs, ms, vs, name):
    k = len(ws)

    def body(*refs):
        for i in range(k):
            w_ref, g_ref, m_ref, v_ref = (refs[j * k + i] for j in range(4))
            d, mm, vv = _adamw_math(w_ref[...], g_ref[...], m_ref[...], v_ref[...])
            refs[4 * k + i][...] = d
            refs[5 * k + i][...] = mm
            refs[6 * k + i][...] = vv

    shapes = [jax.ShapeDtypeStruct(w.shape, F32) for w in ws]
    specs = [_fullspec(w.shape) for w in ws]
    out = pl.pallas_call(
        body, grid=(1,), name=name, in_specs=specs * 4, out_specs=specs * 3, out_shape=shapes * 3,
        compiler_params=_cp("arbitrary"))(*ws, *gs, *ms, *vs)
    return out[:k], out[k:2 * k], out[2 * k:]


def sum_slots(x, out_dtype, name):
    g, r, c = x.shape
    tr = r if x.size * x.dtype.itemsize <= SMALL_BLOCK_BYTES else _tile(r, (256, 128, 64, 32, 16, 8))

    def body(x_ref, o_ref):
        acc = x_ref[0].astype(F32)
        for i in range(1, g):
            acc = acc + x_ref[i].astype(F32)
        o_ref[...] = acc.astype(o_ref.dtype)

    return pl.pallas_call(
        body, grid=(r // tr,), name=name,
        in_specs=[pl.BlockSpec((g, tr, c), lambda i: (0, i, 0))], out_specs=_rowspec(tr, c),
        out_shape=jax.ShapeDtypeStruct((r, c), out_dtype),
        compiler_params=_cp("parallel"))(x)


HBM_SPEC = pl.BlockSpec(memory_space=pltpu.HBM)
SEM_SPEC = pl.BlockSpec(memory_space=pltpu.SEMAPHORE)
DATAFLOW = pltpu.SideEffectType.DATAFLOW_SIDE_EFFECTING


def _my_rank():
    return 4 * lax.axis_index("x") + 2 * lax.axis_index("y") + lax.axis_index("c")


def _exchange_copies(x_refs, land_refs, send, recv, a2a):
    pos = [lax.axis_index(a) for a in AXES]
    me = _my_rank()
    copies = []
    for x_ref, land_ref, s_ref, r_ref in zip(x_refs, land_refs, send, recv):
        for k in range(N_DEV - 1):
            bits = ((k + 1) >> 2 & 1, (k + 1) >> 1 & 1, (k + 1) & 1)
            peer = tuple(1 - p if b else p for p, b in zip(pos, bits))
            prank = 4 * peer[0] + 2 * peer[1] + peer[2]
            copies.append(pltpu.make_async_remote_copy(
                src_ref=x_ref.at[prank] if a2a else x_ref, dst_ref=land_ref.at[me],
                send_sem=s_ref.at[k], recv_sem=r_ref.at[k], device_id=peer, device_id_type=MESH))
    return copies


def exchange_start(xs, a2a, name, after=None):
    n = len(xs)
    me = _my_rank()
    lands = []
    for x in xs:
        own = lax.dynamic_index_in_dim(x, me, 0, keepdims=True) if a2a else x[None]
        shape = x.shape if a2a else (N_DEV,) + x.shape
        lands.append(lax.dynamic_update_slice(lax.empty(shape, x.dtype), own, (me,) + (0,) * (len(shape) - 1)))

    def body(*refs):
        x_refs, land_refs = refs[:n], refs[n:2 * n]
        outs = refs[len(refs) - 4 * n - 1:]
        for cp in _exchange_copies(x_refs, land_refs, outs[:n], outs[n:2 * n], a2a):
            cp.start()
        token = outs[4 * n]
        token[...] = jnp.zeros_like(token)

    sems = [pltpu.SemaphoreType.DMA((N_DEV - 1,))] * n
    out = pl.pallas_call(
        body, name=name,
        out_shape=tuple(sems + sems + [pltpu.HBM(x.shape, x.dtype) for x in xs]
                        + [pltpu.HBM(l.shape, l.dtype) for l in lands] + [jax.ShapeDtypeStruct((8, LANES), F32)]),
        in_specs=[HBM_SPEC] * (2 * n) + ([] if after is None else [pl.BlockSpec(memory_space=pl.ANY)]),
        out_specs=tuple([SEM_SPEC] * (2 * n) + [HBM_SPEC] * (2 * n) + [pl.BlockSpec(memory_space=pltpu.VMEM)]),
        input_output_aliases={i: 2 * n + i for i in range(2 * n)},
        compiler_params=pltpu.CompilerParams(has_side_effects=DATAFLOW),
    )(*[pltpu.with_memory_space_constraint(a, pltpu.HBM) for a in list(xs) + lands], *([] if after is None else [after]))
    return (out[:n], out[n:2 * n], out[2 * n:3 * n], out[3 * n:4 * n]), out[4 * n]


def exchange_wait(handles, after, a2a, name):
    send, recv, x_thru, land_thru = handles
    n = len(x_thru)

    def body(*refs):
        x_refs, land_refs = refs[:n], refs[n:2 * n]
        s_refs, r_refs = refs[2 * n:3 * n], refs[3 * n:4 * n]
        for cp in _exchange_copies(x_refs, land_refs, s_refs, r_refs, a2a):
            cp.wait_send()
            cp.wait_recv()

    out = pl.pallas_call(
        body, name=name,
        out_shape=tuple([pltpu.HBM(a.shape, a.dtype) for a in list(x_thru) + list(land_thru)]),
        in_specs=[HBM_SPEC] * (2 * n) + [SEM_SPEC] * (2 * n) + [pl.BlockSpec(memory_space=pl.ANY)],
        out_specs=tuple([HBM_SPEC] * (2 * n)),
        input_output_aliases={i: i for i in range(2 * n)},
        compiler_params=pltpu.CompilerParams(has_side_effects=DATAFLOW),
    )(*x_thru, *land_thru, *send, *recv, after)
    return out[n:2 * n]


def _local_step(x, target, weight, emit, P):
    s = x.shape[0]
    tabs = _rope_tables(s)
    sinkb = jnp.broadcast_to(P["ev_sinks"].reshape(N_HEADS, 1), (N_HEADS, LANES))
    sbx = jnp.repeat(P["od_spatial_b"].reshape(D_GROUPS, BLOCK).T, HEAD_DIM, axis=1)
    sw = P["od_spatial_w"].reshape(D_GROUPS, BLOCK, BLOCK)
    fg = P["ffn_norm_g"]
    latest = [None]

    def out(name, layer, grad):
        tok = emit(name, layer, grad)
        if tok is not None:
            latest[0] = tok

    def dep(a):
        return a if latest[0] is None else a + latest[0][0:1, 0:1]

    n0 = rmsnorm(x, P["ev_norm_g"], "rms_in")
    qk_e, v_e, ga, gb = inproj(n0, weight("ev_w_in", 0, n0), tabs, ATT_W + 128,
                               (ATT_W + 128, 128, CONV_CH, CONV_CH), (), "ev_inproj")
    a_e, lse_e = attn_fwd(qk_e, v_e, d=1, hkv=A_KV_HEADS, max_dist=BLOCK - 1, sink=sinkb, out_dtype=ACT,
                          name="ev_attn")
    c_act, c1 = conv_fwd(ga, gb, P["ev_conv_w"], P["ev_conv_b"], P["ev_conv_ln_g"], P["ev_conv_ln_b"], "ev_conv")
    h1, n1 = mm_res([a_e, c_act], weight("ev_w_out", 0, c_act), x, fg[0:1], "ev_outproj")
    gate0, up0, act0 = ffn_up(n1, weight("ffn_w_gate", 0, n1), weight("ffn_w_up", 0, n1), "ffn0_up")
    h2, n2 = mm_res([act0], weight("ffn_w_down", 0, act0), h1, P["od_norm_g"], "ffn0_down")
    fold_dils = tuple(dil for _, dil in DILATED if dil > 1)
    qk_o, v_o, z, *pre = inproj(n2, weight("od_w_in", 0, n2), tabs, 2 * ATT_W, (2 * ATT_W, ATT_W, 2 * D_CH),
                                fold_dils, "od_inproj")
    folded = {dil: (pre[2 * i], pre[2 * i + 1]) for i, dil in enumerate(fold_dils)}
    outs, lses = [], []
    for window, dil in DILATED:
        assert window // dil == BLOCK
        o_r, l_r = attn_fwd(qk_o, v_o, d=dil, hkv=N_HEADS, max_dist=BLOCK, sink=None, out_dtype=ACT,
                            name="od_attn_d%d" % dil, folded=folded.get(dil), keep_folded=True)
        outs.append(o_r)
        lses.append(l_r)
    dils = [dil for _, dil in DILATED]
    c_out, lse_o, *c_folded = combine_fwd(outs, lses, dils, "od_combine")
    d_out = gmlp_fwd(z, P["od_sgu_ln_g"], P["od_sgu_ln_b"], sw, sbx, "od_gmlp")
    h3, n3 = mm_res([c_out, d_out], weight("od_w_out", 0, d_out), h2, fg[1:2], "od_outproj")
    gate1, up1, act1 = ffn_up(n3, weight("ffn_w_gate", 1, n3), weight("ffn_w_up", 1, n3), "ffn1_up")
    h4, _ = mm_res([act1], weight("ffn_w_down", 1, act1), h3, None, "ffn1_down")
    loss_part, dh4, dg_final = final_loss(h4, P["final_norm_g"], target, "loss_head")

    def ffn_bwd(layer, dh_out, h_in, n_in, gate, up, act):
        wg, wu, wd = (weight(n, layer, dh_out) for n in ("ffn_w_gate", "ffn_w_up", "ffn_w_down"))
        tag = "ffn%d" % layer
        dgate, dup = ffn_down_bwd(dh_out, wd, gate, up, tag + "_down_bwd")
        g_wd = mm_tn(act, dh_out, tag + "_dwd")
        dh_in, dgn = mm_nt_rms([(dgate, wg), (dup, wu)], h_in, dep(fg[layer:layer + 1]), dh_out, tag + "_up_bwd")
        out("ffn_w_down", layer, g_wd)
        g_wg, g_wu = mm_tn_pair(n_in, dgate, dup, tag + "_dwgu")
        out("ffn_w_gate", layer, g_wg)
        out("ffn_w_up", layer, g_wu)
        return dh_in, dgn[0:1]

    dh3, dgn_f1 = ffn_bwd(1, dh4, h3, n3, gate1, up1, act1)

    dcd, *dc_folded = mm_nt(dh3, weight("od_w_out", 0, dh3), "od_outproj_bwd", fold_dils, ATT_W)
    do_o = {dil: (dc_folded[i], c_folded[i]) for i, dil in enumerate(fold_dils)}
    dz, g_sw, g_sbx, g_slg, g_slb = gmlp_bwd(z, dcd, 1, dep(P["od_sgu_ln_g"]), P["od_sgu_ln_b"], sw, sbx,
                                             "od_gmlp_bwd")
    g_sb = jnp.sum(g_sbx.reshape(BLOCK, D_GROUPS, HEAD_DIM), axis=-1).T
    out("od_spatial", 0, jnp.concatenate([g_sw.reshape(D_GROUPS * BLOCK, BLOCK), g_sb], axis=0))
    out("od_w_out", 0, jnp.concatenate([mm_tn(c_out, dh3, "od_dwo_c"), mm_tn(d_out, dh3, "od_dwo_d")], axis=0))
    dqkv = [attn_bwd(qk_o, v_o, dcd if dil == 1 else dcd[:, :ATT_W], 0, c_out, lse_o, d=dil, hkv=N_HEADS,
                     max_dist=BLOCK, sink=None, out_dtype=ACT, name="od_attn_bwd_d%d" % dil,
                     folded=folded.get(dil), folded_do_o=do_o.get(dil), keep_folded=True)
            for window, dil in DILATED]
    dproj_o = assemble([([(b[j], dil) for b, dil in zip(dqkv, dils)], j < 2) for j in range(3)] + [([dz], False)],
                       tabs, "od_dproj")
    dh2, dgn_od = mm_nt_rms([(dproj_o, weight("od_w_in", 0, dproj_o))], h2, dep(P["od_norm_g"]), dh3,
                            "od_inproj_bwd")
    out("od_w_in", 0, mm_tn(n2, dproj_o, "od_dwi"))

    dh1, dgn_f0 = ffn_bwd(0, dh2, h1, n1, gate0, up0, act0)

    dac = mm_nt(dh1, weight("ev_w_out", 0, dh1), "ev_outproj_bwd")
    dc1, g_clg, g_clb = conv_bwd_ln(c1, dac, 1, dep(P["ev_conv_ln_g"]), P["ev_conv_ln_b"], "ev_conv_bwd_ln")
    out("ev_w_out", 0, jnp.concatenate([mm_tn(a_e, dh1, "ev_dwo_a"), mm_tn(c_act, dh1, "ev_dwo_c")], axis=0))
    dga, dgb, g_cw, g_cb = conv_bwd_conv(dc1, ga, gb, P["ev_conv_w"], "ev_conv_bwd")
    dq, dk, dv, dsink = attn_bwd(qk_e, v_e, dac, 0, a_e, lse_e, d=1, hkv=A_KV_HEADS, max_dist=BLOCK - 1,
                                 sink=sinkb, out_dtype=F32, name="ev_attn_bwd")
    dproj_e = assemble([([dq], True), ([dk], True), ([dv], False), ([dga], False), ([dgb], False)], tabs,
                       "ev_dproj")
    out("ev_w_in", 0, mm_tn(n0, dproj_e, "ev_dwi"))
    dx, dgn_ev = mm_nt_rms([(dproj_e, weight("ev_w_in", 0, dproj_e))], x, dep(P["ev_norm_g"]), dh1,
                           "ev_inproj_bwd")

    small = {
        "ev_norm_g": dgn_ev[0:1],
        "ev_sinks": dsink[:, 0:1].reshape(1, N_HEADS),
        "ev_conv_w": jnp.sum(g_cw.reshape(CONV_HALO, SUBLANES, CONV_CH), axis=1)[:CONV_WIDTH],
        "ev_conv_b": jnp.sum(g_cb, axis=0, keepdims=True),
        "ev_conv_ln_g": g_clg[0:1],
        "ev_conv_ln_b": g_clb[0:1],
        "od_norm_g": dgn_od[0:1],
        "od_sgu_ln_g": g_slg[0:1],
        "od_sgu_ln_b": g_slb[0:1],
        "od_spatial_w": g_sw.reshape(D_GROUPS * BLOCK, BLOCK),
        "od_spatial_b": g_sb,
        "ffn_norm_g": jnp.concatenate([dgn_f0, dgn_f1], axis=0),
        "final_norm_g": dg_final[0:1],
    }
    return loss_part, dx, small


BIG = ("ev_w_in", "ev_w_out", "od_w_in", "od_w_out", "ffn_w_gate", "ffn_w_up", "ffn_w_down")
COL_SHARDED = ("ev_w_in", "od_w_in", "ffn_w_gate", "ffn_w_up")
GATHER_GROUPS = (
    (("ev_w_in", 0),),
    (("ev_w_out", 0),),
    (("ffn_w_gate", 0), ("ffn_w_up", 0)),
    (("ffn_w_down", 0),),
    (("od_w_in", 0),),
    (("od_w_out", 0),),
    (("ffn_w_gate", 1), ("ffn_w_up", 1)),
    (("ffn_w_down", 1),),
)
GATHER_EARLY = 4
GATHER_LATE_AT = 2
REDUCE_GROUPS = (
    (("ffn_w_down", 1), ("ffn_w_gate", 1), ("ffn_w_up", 1)),
    (("od_w_out", 0),),
    (("od_w_in", 0),),
    (("ffn_w_down", 0), ("ffn_w_gate", 0), ("ffn_w_up", 0)),
    (("ev_w_out", 0),),
    (("ev_w_in", 0),),
)


def _unshard(name, g):
    if name in COL_SHARDED:
        return jnp.moveaxis(g, 0, 1).reshape(g.shape[1], N_DEV * g.shape[2])
    return g.reshape(N_DEV * g.shape[1], g.shape[2])


def _shard_slots(name, full):
    r, c = full.shape
    if name in COL_SHARDED:
        return jnp.moveaxis(full.reshape(r, N_DEV, c // N_DEV), 1, 0)
    return full.reshape(N_DEV, r // N_DEV, c)


def kernel(x, ev_norm_g, ev_w_in, ev_sinks, ev_conv_w, ev_conv_b, ev_conv_ln_g, ev_conv_ln_b, ev_w_out, od_norm_g, od_w_in, od_sgu_ln_g, od_sgu_ln_b, od_spatial_w, od_spatial_b, od_w_out, ffn_norm_g, ffn_w_gate, ffn_w_up, ffn_w_down, final_norm_g, loss_target, m_ev_norm_g, m_ev_w_in, m_ev_sinks, m_ev_conv_w, m_ev_conv_b, m_ev_conv_ln_g, m_ev_conv_ln_b, m_ev_w_out, m_od_norm_g, m_od_w_in, m_od_sgu_ln_g, m_od_sgu_ln_b, m_od_spatial_w, m_od_spatial_b, m_od_w_out, m_ffn_norm_g, m_ffn_w_gate, m_ffn_w_up, m_ffn_w_down, m_final_norm_g, v_ev_norm_g, v_ev_w_in, v_ev_sinks, v_ev_conv_w, v_ev_conv_b, v_ev_conv_ln_g, v_ev_conv_ln_b, v_ev_w_out, v_od_norm_g, v_od_w_in, v_od_sgu_ln_g, v_od_sgu_ln_b, v_od_spatial_w, v_od_spatial_b, v_od_w_out, v_ffn_norm_g, v_ffn_w_gate, v_ffn_w_up, v_ffn_w_down, v_final_norm_g):
    names = ["ev_norm_g", "ev_w_in", "ev_sinks", "ev_conv_w", "ev_conv_b", "ev_conv_ln_g", "ev_conv_ln_b", "ev_w_out",
             "od_norm_g", "od_w_in", "od_sgu_ln_g", "od_sgu_ln_b", "od_spatial_w", "od_spatial_b", "od_w_out",
             "ffn_norm_g", "ffn_w_gate", "ffn_w_up", "ffn_w_down", "final_norm_g"]
    wts = dict(zip(names, [ev_norm_g, ev_w_in, ev_sinks, ev_conv_w, ev_conv_b, ev_conv_ln_g, ev_conv_ln_b, ev_w_out,
                           od_norm_g, od_w_in, od_sgu_ln_g, od_sgu_ln_b, od_spatial_w, od_spatial_b, od_w_out,
                           ffn_norm_g, ffn_w_gate, ffn_w_up, ffn_w_down, final_norm_g]))
    mom = dict(zip(names, [m_ev_norm_g, m_ev_w_in, m_ev_sinks, m_ev_conv_w, m_ev_conv_b, m_ev_conv_ln_g, m_ev_conv_ln_b,
                           m_ev_w_out, m_od_norm_g, m_od_w_in, m_od_sgu_ln_g, m_od_sgu_ln_b, m_od_spatial_w,
                           m_od_spatial_b, m_od_w_out, m_ffn_norm_g, m_ffn_w_gate, m_ffn_w_up, m_ffn_w_down,
                           m_final_norm_g]))
    vel = dict(zip(names, [v_ev_norm_g, v_ev_w_in, v_ev_sinks, v_ev_conv_w, v_ev_conv_b, v_ev_conv_ln_g, v_ev_conv_ln_b,
                           v_ev_w_out, v_od_norm_g, v_od_w_in, v_od_sgu_ln_g, v_od_sgu_ln_b, v_od_spatial_w,
                           v_od_spatial_b, v_od_w_out, v_ffn_norm_g, v_ffn_w_gate, v_ffn_w_up, v_ffn_w_down,
                           v_final_norm_g]))
    me = _my_rank()

    sp = jnp.zeros((40, LANES), F32)
    sp = sp.at[0:CONV_WIDTH, 0:64].set(ev_conv_w[0])
    sp = sp.at[32, :].set(od_norm_g[0])
    sp = sp.at[33, 0:64].set(od_sgu_ln_g[0])
    sp = sp.at[34, 0:64].set(od_sgu_ln_b[0])

    early = [k for grp in GATHER_GROUPS[:GATHER_EARLY] for k in grp]
    late = [k for grp in GATHER_GROUPS[GATHER_EARLY:] for k in grp]
    early_act = dict(zip(early, cast_layers([(wts[n], l) for n, l in early], "cast_early")))
    late_act = dict(zip(late, cast_layers([(wts[n], l) for n, l in late], "cast_late")))
    ag_early, ag_token = exchange_start([sp] + [early_act[k] for k in early], False, "ag_start")
    ag_late = []
    full_w = {}
    P = {
        "ev_norm_g": ev_norm_g + ag_token[0:1, 0:1], "ev_sinks": ev_sinks, "ev_conv_b": ev_conv_b,
        "ev_conv_ln_g": ev_conv_ln_g, "ev_conv_ln_b": ev_conv_ln_b,
        "od_spatial_w": od_spatial_w, "od_spatial_b": od_spatial_b, "ffn_norm_g": ffn_norm_g,
        "final_norm_g": final_norm_g.reshape(1, D_MODEL),
    }

    def weight(name, layer, after):
        if (name, layer) not in full_w:
            gi = [i for i, grp in enumerate(GATHER_GROUPS) if (name, layer) in grp][0]
            if gi < GATHER_EARLY:
                idx = [1 + early.index(k) for k in GATHER_GROUPS[gi]]
                handles = ag_early
            else:
                idx = [late.index(k) for k in GATHER_GROUPS[gi]]
                handles = ag_late[0]
            if gi == 0:
                idx = [0] + idx
            lands = exchange_wait(tuple([h[i] for i in idx] for h in handles), after, False, "ag_wait%d" % gi)
            if gi == 0:
                spg, lands = lands[0], lands[1:]
                P["ev_conv_w"] = jnp.moveaxis(spg[:, 0:CONV_WIDTH, 0:64], 0, 1).reshape(CONV_WIDTH, CONV_CH)
                P["od_norm_g"] = spg[:, 32, :].reshape(1, D_MODEL)
                P["od_sgu_ln_g"] = spg[:, 33, 0:64].reshape(1, D_CH)
                P["od_sgu_ln_b"] = spg[:, 34, 0:64].reshape(1, D_CH)
            if gi == GATHER_LATE_AT:
                ag_late.append(exchange_start([late_act[k] for k in late], False, "ag_start_late", after=lands[0])[0])
            for k, land in zip(GATHER_GROUPS[gi], lands):
                full_w[k] = _unshard(k[0], land)
        return full_w[(name, layer)]

    pending, rs_started, spatial = {}, [], []

    def emit(name, layer, grad):
        if name == "od_spatial":
            handles, token = exchange_start([grad], False, "ar_start_b")
            spatial.append(handles)
            return token
        pending[(name, layer)] = grad
        for gi, grp in enumerate(REDUCE_GROUPS):
            if (name, layer) in grp and all(k in pending for k in grp):
                handles, token = exchange_start([_shard_slots(k[0], pending[k]).astype(ACT) for k in grp], True,
                                                "rs_start%d" % gi)
                rs_started.append((gi, handles))
                return token
        return None

    loss_part, dx, small = _local_step(x[0], loss_target[0], weight, emit, P)
    loss = lax.psum(loss_part[0, 0], AXES)

    wide = ["ev_norm_g", "ev_sinks", "ev_conv_w", "ev_conv_b", "ev_conv_ln_g", "ev_conv_ln_b", "od_norm_g",
            "od_sgu_ln_g", "od_sgu_ln_b", "ffn_norm_g", "final_norm_g"]
    blk_a = jnp.concatenate(
        [jnp.pad(small[n], ((0, 0), (0, D_MODEL - small[n].shape[1]))) for n in wide], axis=0)
    blk_a = jnp.pad(blk_a, ((0, 48 - blk_a.shape[0]), (0, 0)))
    ar_a, ar_token = exchange_start([blk_a], False, "ar_start_a")

    results = {}
    for gi, handles in rs_started:
        lands = exchange_wait(handles, ar_token, True, "rs_wait%d" % gi)
        for (n, l), land in zip(REDUCE_GROUPS[gi], lands):
            results[n] = sum_adamw(land, wts[n], mom[n], vel[n], l, results.get(n), "adamw_%s%d" % (n, l))
    out_g, out_d, out_m, out_v = {}, {}, {}, {}
    for n in BIG:
        out_g[n], out_d[n], out_m[n], out_v[n] = results[n]

    last = results[REDUCE_GROUPS[-1][-1][0]][0]
    sum_b = sum_slots(exchange_wait(spatial[0], last, False, "ar_wait_b")[0], F32, "ar_sum_b")
    sum_a = sum_slots(exchange_wait(ar_a, last, False, "ar_wait_a")[0], F32, "ar_sum_a")
    full = {}
    off = 0
    for n in wide:
        r_, c_ = small[n].shape
        full[n] = sum_a[off:off + r_, 0:c_]
        off += r_
    full["od_spatial_w"] = sum_b[0:D_GROUPS * BLOCK]
    full["od_spatial_b"] = sum_b[D_GROUPS * BLOCK:D_GROUPS * BLOCK + D_GROUPS]
    full["ev_conv_w"] = lax.dynamic_slice_in_dim(full["ev_conv_w"], me * 64, 64, axis=1)
    full["od_norm_g"] = lax.dynamic_slice_in_dim(full["od_norm_g"], me * 128, 128, axis=1)
    full["od_sgu_ln_g"] = lax.dynamic_slice_in_dim(full["od_sgu_ln_g"], me * 64, 64, axis=1)
    full["od_sgu_ln_b"] = lax.dynamic_slice_in_dim(full["od_sgu_ln_b"], me * 64, 64, axis=1)

    small_names = [n for n in names if n not in BIG]
    view = {n: ((-1, wts[n].shape[-1]) if wts[n].ndim > 1 else (1, -1)) for n in small_names}
    ds, ms, vs = adamw_small([wts[n].reshape(view[n]) for n in small_names],
                             [full[n].reshape(view[n]) for n in small_names],
                             [mom[n].reshape(view[n]) for n in small_names],
                             [vel[n].reshape(view[n]) for n in small_names], "adamw_small")
    for i, n in enumerate(small_names):
        shp = wts[n].shape
        out_g[n], out_d[n], out_m[n], out_v[n] = (full[n].reshape(shp), ds[i].reshape(shp), ms[i].reshape(shp),
                                                  vs[i].reshape(shp))

    return (loss, dx[None], *[out_g[n] for n in names], *[out_d[n] for n in names],
            *[out_m[n] for n in names], *[out_v[n] for n in names])
```
